```python
import math
import jax
import jax.numpy as jnp
from jax import lax
import numpy as np

D_MODEL = 1024
BATCH = 8
SEQ = 4096
DEPTH = 2

GRID_W = 64
CTX_LEN = 256
EPS = 1e-6
F32 = jnp.float32

MLA_HEADS = 8
MLA_NOPE = 64
MLA_ROPE = 32
MLA_V = 64
MLA_Q_LORA = 256
MLA_KV_LORA = 128
MLA_SCALE = (MLA_NOPE + MLA_ROPE) ** -0.5
ROPE_BASE = 10000.0
Q_BLOCK = 128

GLA_HEADS = 4
GLA_DK = 256
GLA_DV = 512
GLA_HK = GLA_DK // GLA_HEADS
GLA_HV = GLA_DV // GLA_HEADS
GLA_GATE_RANK = 16
GLA_TAU = 16.0
GLA_CHUNK = 64

HY_WIDTH = 512
HY_ORDER = 2
HY_SHORT = 3
HY_BANDS = 16
HY_POS_DIM = 1 + 2 * HY_BANDS
HY_FILTER_HIDDEN = 64
HY_FAST_DECAY = 0.3
HY_SLOW_DECAY = 1.5
HY_DECAY_TARGET = 1e-2

N_BRANCH = 3
D_FF = 4 * D_MODEL
MLA_OUT = MLA_HEADS * MLA_V
IN_SIZES = (MLA_Q_LORA, MLA_KV_LORA, MLA_ROPE, GLA_DK, GLA_DK, GLA_DV, GLA_DV, GLA_GATE_RANK, GLA_GATE_RANK,
            (HY_ORDER + 1) * HY_WIDTH, N_BRANCH * D_MODEL)
D_IN = sum(IN_SIZES)

kernel_name = 'hybrid_mla_gla_hyena_dit_block'


def _split(z, sizes):
    parts, start = [], 0
    for s in sizes:
        parts.append(z[..., start:start + s])
        start += s
    return parts


def rmsnorm(x, g):
    xf = x.astype(F32)
    y = xf * lax.rsqrt(jnp.mean(xf * xf, axis=-1, keepdims=True) + EPS)
    return (y * g.astype(F32)).astype(x.dtype)


def modulate(h, shift, scale):
    return h * (1 + scale) + shift


def axial_rope_tables(rows):
    row = jnp.repeat(jnp.arange(rows, dtype=F32), GRID_W)
    col = jnp.tile(jnp.arange(GRID_W, dtype=F32), rows)
    a = MLA_ROPE // 4
    inv = ROPE_BASE ** (-jnp.arange(a, dtype=F32) / a)
    ang = jnp.concatenate([row[:, None] * inv, col[:, None] * inv], axis=-1)
    return jnp.cos(ang), jnp.sin(ang)


def apply_axial_rope(x, cos, sin):
    a = MLA_ROPE // 4
    cos = cos.astype(x.dtype)
    sin = sin.astype(x.dtype)

    def rot(u, cs, sn):
        u1, u2 = u[..., :a], u[..., a:]
        return jnp.concatenate([u1 * cs - u2 * sn, u1 * sn + u2 * cs], axis=-1)

    return jnp.concatenate([rot(x[..., :2 * a], cos[..., :a], sin[..., :a]),
                            rot(x[..., 2 * a:], cos[..., a:], sin[..., a:])], axis=-1)


def mla_queries(zq, q_norm, w_uq, rope):
    B, L = zq.shape[:2]
    q = (rmsnorm(zq, q_norm) @ w_uq).reshape(B, L, MLA_HEADS, MLA_NOPE + MLA_ROPE)
    q_nope, q_rope = q[..., :MLA_NOPE], q[..., MLA_NOPE:]
    if rope is not None:
        q_rope = apply_axial_rope(q_rope, rope[0][:, None, :], rope[1][:, None, :])
    return q_nope, q_rope


def mla_keys_values(zkv, zkr, kv_norm, w_ukv, rope):
    B, L = zkv.shape[:2]
    kv = (rmsnorm(zkv, kv_norm) @ w_ukv).reshape(B, L, MLA_HEADS, MLA_NOPE + MLA_V)
    k_rope = zkr
    if rope is not None:
        k_rope = apply_axial_rope(k_rope, rope[0], rope[1])
    return kv[..., :MLA_NOPE], k_rope, kv[..., MLA_NOPE:]


def mla_attend(qn, qr, kn, kr, v):
    s = jnp.einsum('bqhd,bkhd->bhqk', qn, kn) + jnp.einsum('bqhr,bkr->bhqk', qr, kr)
    p = jax.nn.softmax(s.astype(F32) * MLA_SCALE, axis=-1).astype(v.dtype)
    return jnp.einsum('bhqk,bkhd->bqhd', p, v)


def mla_latent_attention(qn, qr, kn, kr, v):
    B, L = qn.shape[:2]
    nb = L // Q_BLOCK

    def blocks(t):
        return jnp.moveaxis(t.reshape(B, nb, Q_BLOCK, *t.shape[2:]), 1, 0)

    o = lax.map(lambda qb: mla_attend(qb[0], qb[1], kn, kr, v), (blocks(qn), blocks(qr)))
    return jnp.moveaxis(o, 0, 1).reshape(B, L, MLA_OUT)


def gla_heads(q, k, v, a_f, a_b, w_a2, b_a):
    def heads(t):
        B, L, W = t.shape
        return t.reshape(B, L, GLA_HEADS, W // GLA_HEADS).transpose(0, 2, 1, 3).astype(F32)

    def log_gate(a, d):
        return jax.nn.log_sigmoid((a @ w_a2[d] + b_a[d]).astype(F32)) / GLA_TAU

    return (heads(q) * GLA_HK ** -0.5, heads(k), heads(v), heads(log_gate(a_f, 0)), heads(log_gate(a_b, 1)))


def gla_chunked(q, k, v, g, s0):
    B, H, L, _ = q.shape
    C = min(GLA_CHUNK, L)
    n = L // C
    mask = jnp.tril(jnp.ones((C, C), dtype=bool))

    def to_chunks(t):
        return jnp.moveaxis(t.reshape(B, H, n, C, t.shape[-1]), 2, 0)

    def step(S, inp):
        qc, kc, vc, gc = inp
        G = jnp.cumsum(gc, axis=2)
        o_inter = jnp.einsum('bhck,bhkv->bhcv', qc * jnp.exp(G), S)
        diff = G[:, :, :, None, :] - G[:, :, None, :, :]
        decay = jnp.exp(jnp.where(mask[:, :, None], diff, -jnp.inf))
        A = jnp.einsum('bhik,bhjk,bhijk->bhij', qc, kc, decay)
        o = o_inter + jnp.einsum('bhij,bhjv->bhiv', A, vc)
        G_last = G[:, :, -1:, :]
        S_new = jnp.exp(G_last[:, :, 0, :])[..., None] * S + jnp.einsum('bhck,bhcv->bhkv', kc * jnp.exp(G_last - G), vc)
        return S_new, o

    S_fin, o = lax.scan(step, s0, (to_chunks(q), to_chunks(k), to_chunks(v), to_chunks(g)))
    return jnp.moveaxis(o, 0, 2).reshape(B, H, L, v.shape[-1]), S_fin


def gla_bidirectional(lat, ctx, with_ctx_out):
    q, k, v, gf, gb = lat
    qc, kc, vc, gfc, gbc = ctx
    s0 = jnp.zeros((q.shape[0], GLA_HEADS, GLA_HK, GLA_HV), F32)

    def flip(t):
        return jnp.flip(t, axis=2)

    oc_f, s_f = gla_chunked(qc, kc, vc, gfc, s0)
    oc_b, s_b = gla_chunked(flip(qc), flip(kc), flip(vc), flip(gbc), s0)
    o_f, _ = gla_chunked(q, k, v, gf, s_f)
    o_b, _ = gla_chunked(flip(q), flip(k), flip(v), flip(gb), s_b)
    o_lat = o_f + flip(o_b)
    o_ctx = oc_f + flip(oc_b) if with_ctx_out else None
    return o_lat, o_ctx


def gla_output(o, r, out_norm):
    B, H, L, V = o.shape
    o = rmsnorm(o, out_norm).transpose(0, 2, 1, 3).reshape(B, L, H * V).astype(r.dtype)
    return o * jax.nn.silu(r)


def hyena_filters(L, w1, b1, w2, b2, w3, b3):
    pos = jnp.arange(L, dtype=F32)
    t = pos / max(L - 1, 1)
    f = jnp.linspace(1e-4, HY_BANDS - 1, HY_BANDS, dtype=F32)
    ang = (2.0 * math.pi / L) * pos[:, None] * f
    feat = jnp.concatenate([t[:, None], jnp.cos(ang), jnp.sin(ang)], axis=-1)
    hdn = jnp.sin(feat @ w1.astype(F32) + b1.astype(F32))
    hdn = jnp.sin(hdn @ w2.astype(F32) + b2.astype(F32))
    h = (hdn @ w3.astype(F32) + b3.astype(F32)).reshape(L, 2, HY_ORDER, HY_WIDTH)
    deltas = jnp.linspace(math.log(HY_DECAY_TARGET) / HY_FAST_DECAY, math.log(HY_DECAY_TARGET) / HY_SLOW_DECAY,
                          HY_WIDTH, dtype=F32)
    window = jnp.exp(-t[:, None] * jnp.abs(deltas))
    h = h * window[:, None, None, :]
    return h / jnp.sum(jnp.abs(h), axis=(0, 1), keepdims=True)


def short_conv(u, w, b):
    L = u.shape[1]
    pad = HY_SHORT // 2
    up = jnp.pad(u, ((0, 0), (pad, pad), (0, 0)))
    y = b
    for j in range(HY_SHORT):
        y = y + up[:, j:j + L] * w[j]
    return y


def bidir_long_conv(u, h_fwd, h_bwd):
    L = u.shape[1]
    k = jnp.concatenate([h_fwd, jnp.zeros_like(h_fwd[:1]), h_bwd[:0:-1]], axis=0)
    spec = jnp.fft.rfft(u.astype(F32), n=2 * L, axis=1) * jnp.fft.rfft(k, axis=0)[None]
    return jnp.fft.irfft(spec, n=2 * L, axis=1)[:, :L].astype(u.dtype)


def hyena_branch(z, short_w, short_b, filt, hy_bias):
    x1, x2, v = _split(short_conv(z, short_w, short_b), (HY_WIDTH,) * 3)
    y = v
    for n, gate in enumerate((x1, x2)):
        y = gate * (bidir_long_conv(y, filt[:, 0, n], filt[:, 1, n]) + hy_bias[n] * y)
    return y


def merge_branches(z_gate, y_mla, y_gla, y_hy, w_o_mla, w_o_gla, w_o_hy, w_out):
    g_mla, g_gla, g_hy = _split(jax.nn.sigmoid(z_gate), (D_MODEL,) * N_BRANCH)
    m = g_mla * (y_mla @ w_o_mla) + g_gla * (y_gla @ w_o_gla) + g_hy * (y_hy @ w_o_hy)
    return m @ w_out


def sqrelu_mlp(h, w1, w2):
    return jnp.square(jax.nn.relu(h @ w1)) @ w2


def token_mixer(hx, hc, rope, with_ctx_out, w_in, mla_q_norm, mla_w_uq, mla_kv_norm, mla_w_ukv,
                gla_w_a2, gla_b_a, gla_out_norm, hy_short_w, hy_short_b, hy_filter_w, hy_bias,
                w_o_mla, w_o_gla, w_o_hy, w_out):
    L, Lc = hx.shape[1], hc.shape[1]
    xq, xkv, xkr, xgq, xgk, xgv, xgr, xaf, xab, xhy, xgate = _split(hx @ w_in, IN_SIZES)
    cq, ckv, ckr, cgq, cgk, cgv, cgr, caf, cab, chy, cgate = _split(hc @ w_in, IN_SIZES)

    qn, qr = mla_queries(xq, mla_q_norm, mla_w_uq, rope)
    kn, kr, v = mla_keys_values(xkv, xkr, mla_kv_norm, mla_w_ukv, rope)
    knc, krc, vc = mla_keys_values(ckv, ckr, mla_kv_norm, mla_w_ukv, None)
    y_mla = mla_latent_attention(qn, qr, jnp.concatenate([knc, kn], axis=1),
                                 jnp.concatenate([krc, kr], axis=1), jnp.concatenate([vc, v], axis=1))

    o_gla, o_gla_c = gla_bidirectional(gla_heads(xgq, xgk, xgv, xaf, xab, gla_w_a2, gla_b_a),
                                       gla_heads(cgq, cgk, cgv, caf, cab, gla_w_a2, gla_b_a), with_ctx_out)
    y_gla = gla_output(o_gla, xgr, gla_out_norm)

    y_hy = hyena_branch(xhy, hy_short_w, hy_short_b, hyena_filters(L, *hy_filter_w), hy_bias)

    out_x = merge_branches(xgate, y_mla, y_gla, y_hy, w_o_mla, w_o_gla, w_o_hy, w_out)
    if not with_ctx_out:
        return out_x, None

    qnc, qrc = mla_queries(cq, mla_q_norm, mla_w_uq, None)
    y_mla_c = mla_attend(qnc, qrc, knc, krc, vc).reshape(hc.shape[0], Lc, MLA_OUT)
    y_gla_c = gla_output(o_gla_c, cgr, gla_out_norm)
    y_hy_c = hyena_branch(chy, hy_short_w, hy_short_b, hyena_filters(Lc, *hy_filter_w), hy_bias)
    out_c = merge_branches(cgate, y_mla_c, y_gla_c, y_hy_c, w_o_mla, w_o_gla, w_o_hy, w_out)
    return out_x, out_c


def setup_inputs(seed: int = 0) -> dict:
    key = jax.random.key(seed)
    keys = jax.random.split(key, 32)

    def nrm(i, shape, scale):
        return jax.random.normal(keys[i], shape, F32) * scale

    def gain(i, shape):
        return 1.0 + 0.02 * jax.random.normal(keys[i], shape, F32)

    n_filt = 2 * HY_ORDER * HY_WIDTH
    return {
        'x': nrm(0, (BATCH, SEQ, D_MODEL), 1.0),
        'c': nrm(1, (BATCH, D_MODEL), 1.0),
        'ctx': nrm(2, (BATCH, CTX_LEN, D_MODEL), 1.0),
        'c_ctx': nrm(3, (D_MODEL,), 1.0),
        'ada_w': nrm(4, (DEPTH, D_MODEL, 6 * D_MODEL), 0.5 * D_MODEL ** -0.5),
        'ada_b': nrm(5, (DEPTH, 6 * D_MODEL), 0.02),
        'norm1_g': gain(6, (DEPTH, D_MODEL)),
        'norm2_g': gain(7, (DEPTH, D_MODEL)),
        'w_in': nrm(8, (DEPTH, D_MODEL, D_IN), D_MODEL ** -0.5),
        'mla_q_norm': gain(9, (DEPTH, MLA_Q_LORA)),
        'mla_w_uq': nrm(10, (DEPTH, MLA_Q_LORA, MLA_HEADS * (MLA_NOPE + MLA_ROPE)), MLA_Q_LORA ** -0.5),
        'mla_kv_norm': gain(11, (DEPTH, MLA_KV_LORA)),
        'mla_w_ukv': nrm(12, (DEPTH, MLA_KV_LORA, MLA_HEADS * (MLA_NOPE + MLA_V)), MLA_KV_LORA ** -0.5),
        'gla_w_a2': nrm(13, (DEPTH, 2, GLA_GATE_RANK, GLA_DK), GLA_GATE_RANK ** -0.5),
        'gla_b_a': nrm(14, (DEPTH, 2, GLA_DK), 0.1),
        'gla_out_norm': gain(15, (DEPTH, GLA_HV)),
        'hy_short_w': nrm(16, (DEPTH, HY_SHORT, (HY_ORDER + 1) * HY_WIDTH), HY_SHORT ** -0.5),
        'hy_short_b': nrm(17, (DEPTH, (HY_ORDER + 1) * HY_WIDTH), 0.02),
        'hy_f_w1': nrm(18, (DEPTH, HY_POS_DIM, HY_FILTER_HIDDEN), HY_POS_DIM ** -0.5),
        'hy_f_b1': nrm(19, (DEPTH, HY_FILTER_HIDDEN), 0.1),
        'hy_f_w2': nrm(20, (DEPTH, HY_FILTER_HIDDEN, HY_FILTER_HIDDEN), HY_FILTER_HIDDEN ** -0.5),
        'hy_f_b2': nrm(21, (DEPTH, HY_FILTER_HIDDEN), 0.1),
        'hy_f_w3': nrm(22, (DEPTH, HY_FILTER_HIDDEN, n_filt), HY_FILTER_HIDDEN ** -0.5),
        'hy_f_b3': nrm(23, (DEPTH, n_filt), 0.02),
        'hy_bias': nrm(24, (DEPTH, HY_ORDER, HY_WIDTH), 0.5),
        'w_o_mla': nrm(25, (DEPTH, MLA_OUT, D_MODEL), MLA_OUT ** -0.5),
        'w_o_gla': nrm(26, (DEPTH, GLA_DV, D_MODEL), GLA_DV ** -0.5),
        'w_o_hy': nrm(27, (DEPTH, HY_WIDTH, D_MODEL), HY_WIDTH ** -0.5),
        'w_out': nrm(28, (DEPTH, D_MODEL, D_MODEL), D_MODEL ** -0.5),
        'ff_w1': nrm(29, (DEPTH, D_MODEL, D_FF), D_MODEL ** -0.5),
        'ff_w2': nrm(30, (DEPTH, D_FF, D_MODEL), D_FF ** -0.5),
        'final_norm_g': gain(31, (D_MODEL,)),
    }


def reference(x, c, ctx, c_ctx, ada_w, ada_b, norm1_g, norm2_g, w_in, mla_q_norm, mla_w_uq, mla_kv_norm,
              mla_w_ukv, gla_w_a2, gla_b_a, gla_out_norm, hy_short_w, hy_short_b, hy_f_w1, hy_f_b1, hy_f_w2,
              hy_f_b2, hy_f_w3, hy_f_b3, hy_bias, w_o_mla, w_o_gla, w_o_hy, w_out, ff_w1, ff_w2, final_norm_g):
    rows = x.shape[1] // GRID_W
    rope = axial_rope_tables(rows)
    sc = jax.nn.silu(c)
    scc = jax.nn.silu(c_ctx)
    for l in range(DEPTH):
        with_ctx_out = l < DEPTH - 1
        mod_x = (sc @ ada_w[l] + ada_b[l])[:, None, :]
        mod_c = scc @ ada_w[l] + ada_b[l]
        shx1, scx1, gx1, shx2, scx2, gx2 = _split(mod_x, (D_MODEL,) * 6)
        shc1, scc1, gc1, shc2, scc2, gc2 = _split(mod_c, (D_MODEL,) * 6)

        hx = modulate(rmsnorm(x, norm1_g[l]), shx1, scx1)
        hc = modulate(rmsnorm(ctx, norm1_g[l]), shc1, scc1)
        mx, mc = token_mixer(hx, hc, rope, with_ctx_out, w_in[l], mla_q_norm[l], mla_w_uq[l], mla_kv_norm[l],
                             mla_w_ukv[l], gla_w_a2[l], gla_b_a[l], gla_out_norm[l], hy_short_w[l], hy_short_b[l],
                             (hy_f_w1[l], hy_f_b1[l], hy_f_w2[l], hy_f_b2[l], hy_f_w3[l], hy_f_b3[l]), hy_bias[l],
                             w_o_mla[l], w_o_gla[l], w_o_hy[l], w_out[l])
        x = x + gx1 * mx
        x = x + gx2 * sqrelu_mlp(modulate(rmsnorm(x, norm2_g[l]), shx2, scx2), ff_w1[l], ff_w2[l])
        if with_ctx_out:
            ctx = ctx + gc1 * mc
            ctx = ctx + gc2 * sqrelu_mlp(modulate(rmsnorm(ctx, norm2_g[l]), shc2, scc2), ff_w1[l], ff_w2[l])
    return rmsnorm(x, final_norm_g)
```

```python
import functools
import math

import numpy as np
import jax
import jax.numpy as jnp
from jax import lax
from jax.experimental import pallas as pl
from jax.experimental.pallas import tpu as pltpu

F32 = jnp.float32
BF16 = jnp.bfloat16
HIGHEST = lax.Precision.HIGHEST

D_MODEL = 1024
DEPTH = 2
GRID_W = 64
EPS = 1e-6
MLA_HEADS = 8
MLA_NOPE = 64
MLA_ROPE = 32
MLA_V = 64
MLA_Q_LORA = 256
MLA_KV_LORA = 128
MLA_SCALE = (MLA_NOPE + MLA_ROPE) ** -0.5
ROPE_BASE = 10000.0
GLA_HEADS = 4
GLA_DK = 256
GLA_DV = 512
GLA_HK = GLA_DK // GLA_HEADS
GLA_HV = GLA_DV // GLA_HEADS
GLA_GATE_RANK = 16
GLA_TAU = 16.0
HY_WIDTH = 512
HY_ORDER = 2
HY_SHORT = 3
HY_BANDS = 16
HY_POS_DIM = 1 + 2 * HY_BANDS
HY_FILTER_HIDDEN = 64
HY_FAST_DECAY = 0.3
HY_SLOW_DECAY = 1.5
HY_DECAY_TARGET = 1e-2
D_FF = 4 * D_MODEL
MLA_OUT = MLA_HEADS * MLA_V
IN_SIZES = (MLA_Q_LORA, MLA_KV_LORA, MLA_ROPE, GLA_DK, GLA_DK, GLA_DV, GLA_DV, GLA_GATE_RANK, GLA_GATE_RANK,
            (HY_ORDER + 1) * HY_WIDTH, 3 * D_MODEL)

LANES = 128
SUBLANES = 8
TM = 256
HEAD_SLOT = 128
VMEM_LIMIT = 56 * 1024 * 1024


def _cparams(n_axes, vmem=None):
    return pltpu.CompilerParams(dimension_semantics=("arbitrary",) * n_axes, vmem_limit_bytes=vmem)


def _const_spec(shape):
    nd = len(shape)
    return pl.BlockSpec(shape, lambda *_: (0,) * nd, pipeline_mode=pl.Buffered(1))


def _rms(x):
    return x * lax.rsqrt(jnp.mean(x * x, axis=-1, keepdims=True) + EPS)


def _sigmoid(x):
    return 1.0 / (1.0 + jnp.exp(-x))


def _dot(a, b):
    return jnp.dot(a, b, preferred_element_type=F32)


def _dot_nt(a, b):
    return lax.dot_general(a, b, (((1,), (1,)), ((), ())), preferred_element_type=F32)


def _mod_kernel(cc_ref, w_ref, b_ref, o_ref):
    s = cc_ref[...]
    s = s * _sigmoid(s)
    o_ref[0] = jnp.dot(s, w_ref[0], preferred_element_type=F32, precision=HIGHEST) + b_ref[0]


def _modulation(cc, ada_w, ada_b):
    tn = 1536
    n6 = ada_w.shape[-1]
    return pl.pallas_call(
        _mod_kernel,
        out_shape=jax.ShapeDtypeStruct((DEPTH, 16, n6), F32),
        grid=(DEPTH, n6 // tn),
        in_specs=[
            pl.BlockSpec((16, D_MODEL), lambda l, j: (0, 0)),
            pl.BlockSpec((1, D_MODEL, tn), lambda l, j: (l, 0, j)),
            pl.BlockSpec((1, 1, tn), lambda l, j: (l, 0, j)),
        ],
        out_specs=pl.BlockSpec((1, 16, tn), lambda l, j: (l, 0, j)),
        compiler_params=_cparams(2),
        name="modulation",
    )(cc, ada_w, ada_b.reshape(DEPTH, 1, n6))


W_A = 768
W_G = 2 * GLA_DK + 2 * GLA_DV


def _inproj_kernel(x_ref, mod_ref, g1_ref, wa_ref, wg_ref, wh_ref, wz_ref, qn_ref, kvn_ref, wuq_ref, wukv_ref,
                   wa2_ref, ba_ref, cos_ref, sin_ref,
                   q_out, k_out, v_out, gq_out, gk_out, gv_out, gr_out, glog_out, hy_out, gate_out):
    x = x_ref[0]
    shift = mod_ref[0, 0:1, :]
    scale = mod_ref[0, 1:2, :]
    h = (_rms(x) * g1_ref[...] * (1.0 + scale) + shift).astype(BF16)

    za = _dot(h, wa_ref[...])
    cos = cos_ref[...]
    sin = sin_ref[...]

    cqn = (_rms(za[:, 0:256]) * qn_ref[...]).astype(BF16)
    qab = _dot(cqn, wuq_ref[...])
    nq = MLA_HEADS * HEAD_SLOT
    for hd in range(MLA_HEADS):
        sl = slice(hd * HEAD_SLOT, (hd + 1) * HEAD_SLOT)
        qa = qab[:, hd * HEAD_SLOT:(hd + 1) * HEAD_SLOT]
        qb = qab[:, nq + hd * HEAD_SLOT:nq + (hd + 1) * HEAD_SLOT]
        q_out[0, :, sl] = ((qa * cos + qb * sin) * MLA_SCALE).astype(BF16)

    ckvn = (_rms(za[:, 256:384]) * kvn_ref[...]).astype(BF16)
    kv = _dot(ckvn, wukv_ref[...])
    krot = za[:, 384:512] * cos + za[:, 512:640] * sin
    for hd in range(MLA_HEADS):
        sl = slice(hd * HEAD_SLOT, (hd + 1) * HEAD_SLOT)
        k_out[0, :, sl] = (kv[:, sl] + krot).astype(BF16)
    v_out[0] = kv[:, nq:nq + MLA_OUT].astype(BF16)

    xg = _dot(za[:, 640:768].astype(BF16), wa2_ref[...]) + ba_ref[...]
    glog_out[0] = (jnp.minimum(xg, 0.0) - jnp.log(1.0 + jnp.exp(-jnp.abs(xg)))) * (1.0 / GLA_TAU)

    zg = _dot(h, wg_ref[...])
    gq_out[0] = (zg[:, 0:GLA_DK] * (GLA_HK ** -0.5)).astype(BF16)
    gk_out[0] = zg[:, GLA_DK:2 * GLA_DK].astype(BF16)
    gv_out[0] = zg[:, 2 * GLA_DK:2 * GLA_DK + GLA_DV].astype(BF16)
    gr_out[0] = zg[:, 2 * GLA_DK + GLA_DV:]

    hy_out[0] = _dot(h, wh_ref[...])
    gate_out[0] = _dot(h, wz_ref[...])


def _rope_partner(w):
    a = MLA_ROPE // 4
    perm = np.concatenate([np.arange(a, 2 * a), np.arange(0, a), np.arange(3 * a, 4 * a), np.arange(2 * a, 3 * a)])
    sign = np.concatenate([-np.ones(a), np.ones(a), -np.ones(a), np.ones(a)]).astype(np.float32)
    return w[:, perm] * sign


def _prep_inproj_weights(w_in, mla_w_uq, mla_w_ukv, gla_w_a2, gla_b_a):
    offs = np.concatenate([[0], np.cumsum(IN_SIZES)])
    seg = [w_in[:, offs[i]:offs[i + 1]] for i in range(len(IN_SIZES))]
    w_cq, w_ckv, w_kr, w_gq, w_gk, w_gv, w_gr, w_af, w_ab, w_hy, w_gate = seg
    d = w_in.shape[0]
    z = lambda n: jnp.zeros((d, n), w_in.dtype)
    kr_tile = jnp.concatenate([z(MLA_NOPE), w_kr, z(HEAD_SLOT - MLA_NOPE - MLA_ROPE)], axis=1)
    krp_tile = jnp.concatenate([z(MLA_NOPE), _rope_partner(w_kr), z(HEAD_SLOT - MLA_NOPE - MLA_ROPE)], axis=1)
    a_tile = jnp.concatenate([w_af, w_ab, z(LANES - 2 * GLA_GATE_RANK)], axis=1)
    wa = jnp.concatenate([w_cq, w_ckv, kr_tile, krp_tile, a_tile], axis=1)
    wg = jnp.concatenate([w_gq, w_gk, w_gv, w_gr], axis=1)

    dh = MLA_NOPE + MLA_ROPE
    zq = lambda n: jnp.zeros((MLA_Q_LORA, n), w_in.dtype)
    plain, partner = [], []
    for hd in range(MLA_HEADS):
        blk = mla_w_uq[:, hd * dh:(hd + 1) * dh]
        plain += [blk, zq(HEAD_SLOT - dh)]
        partner += [zq(MLA_NOPE), _rope_partner(blk[:, MLA_NOPE:]), zq(HEAD_SLOT - dh)]
    wuq = jnp.concatenate(plain + partner, axis=1)

    zk = jnp.zeros((MLA_KV_LORA, HEAD_SLOT - MLA_NOPE), w_in.dtype)
    kcols, vcols = [], []
    for hd in range(MLA_HEADS):
        blk = mla_w_ukv[:, hd * (MLA_NOPE + MLA_V):(hd + 1) * (MLA_NOPE + MLA_V)]
        kcols += [blk[:, :MLA_NOPE], zk]
        vcols.append(blk[:, MLA_NOPE:])
    wukv = jnp.concatenate(kcols + vcols, axis=1)

    wa2 = jnp.zeros((LANES, 2 * GLA_DK), w_in.dtype)
    wa2 = wa2.at[0:GLA_GATE_RANK, 0:GLA_DK].set(gla_w_a2[0])
    wa2 = wa2.at[GLA_GATE_RANK:2 * GLA_GATE_RANK, GLA_DK:].set(gla_w_a2[1])
    ba = jnp.concatenate([gla_b_a[0], gla_b_a[1]])[None, :]
    bf = lambda t: t.astype(BF16)
    return bf(wa), bf(wg), bf(w_hy), bf(w_gate), bf(wuq), bf(wukv), bf(wa2), ba


def _rope_tables(seq, ctx_len):
    rows = seq // GRID_W
    row = jnp.repeat(jnp.arange(rows, dtype=F32), GRID_W)
    col = jnp.tile(jnp.arange(GRID_W, dtype=F32), rows)
    a = MLA_ROPE // 4
    inv = ROPE_BASE ** (-jnp.arange(a, dtype=F32) / a)
    ang_r = row[:, None] * inv
    ang_c = col[:, None] * inv
    cos32 = jnp.concatenate([jnp.cos(ang_r), jnp.cos(ang_r), jnp.cos(ang_c), jnp.cos(ang_c)], axis=1)
    sin32 = jnp.concatenate([jnp.sin(ang_r), jnp.sin(ang_r), jnp.sin(ang_c), jnp.sin(ang_c)], axis=1)
    pad_r = HEAD_SLOT - MLA_NOPE - MLA_ROPE
    cos = jnp.concatenate([jnp.ones((seq, MLA_NOPE), F32), cos32, jnp.zeros((seq, pad_r), F32)], axis=1)
    sin = jnp.concatenate([jnp.zeros((seq, MLA_NOPE), F32), sin32, jnp.zeros((seq, pad_r), F32)], axis=1)
    cos_c = jnp.concatenate([jnp.ones((ctx_len, MLA_NOPE + MLA_ROPE), F32), jnp.zeros((ctx_len, pad_r), F32)], axis=1)
    sin_c = jnp.zeros((ctx_len, HEAD_SLOT), F32)
    return jnp.concatenate([cos, cos_c], axis=0), jnp.concatenate([sin, sin_c], axis=0)


def _inproj(xc, mod_l, g1, weights, q_norm, kv_norm, cos, sin, n_lat_tiles):
    bsz, t, d = xc.shape
    nt = t // TM
    wa, wg, wh, wz, wuq, wukv, wa2, ba = weights
    tile = lambda w: pl.BlockSpec((1, TM, w), lambda b, i: (b, i, 0))
    row = lambda w: pl.BlockSpec((1, w), lambda b, i: (0, 0))
    mod_spec = pl.BlockSpec((1, 6, d), lambda b, i: (jnp.where(i < n_lat_tiles, b, bsz), 0, 0))
    tab = pl.BlockSpec((TM, HEAD_SLOT), lambda b, i: (i, 0))
    nq = MLA_HEADS * HEAD_SLOT
    sds = lambda w, dt: jax.ShapeDtypeStruct((bsz, t, w), dt)
    out_shape = (sds(nq, BF16), sds(nq, BF16), sds(MLA_OUT, BF16), sds(GLA_DK, BF16), sds(GLA_DK, BF16),
                 sds(GLA_DV, BF16), sds(GLA_DV, F32), sds(2 * GLA_DK, F32), sds(3 * HY_WIDTH, F32),
                 sds(3 * D_MODEL, F32))
    out_specs = (tile(nq), tile(nq), tile(MLA_OUT), tile(GLA_DK), tile(GLA_DK), tile(GLA_DV), tile(GLA_DV),
                 tile(2 * GLA_DK), tile(3 * HY_WIDTH), tile(3 * D_MODEL))
    return pl.pallas_call(
        _inproj_kernel,
        out_shape=out_shape,
        grid=(bsz, nt),
        in_specs=[tile(d), mod_spec, row(d), _const_spec(wa.shape), _const_spec(wg.shape), _const_spec(wh.shape),
                  _const_spec(wz.shape), row(MLA_Q_LORA), row(MLA_KV_LORA), _const_spec(wuq.shape),
                  _const_spec(wukv.shape), _const_spec(wa2.shape), row(2 * GLA_DK), tab, tab],
        out_specs=out_specs,
        compiler_params=_cparams(2, VMEM_LIMIT),
        name="inproj",
    )(xc, mod_l, g1[None, :], wa, wg, wh, wz, q_norm[None, :], kv_norm[None, :], wuq, wukv, wa2, ba, cos, sin)


def _attn_kernel(q_ref, k_ref, v_ref, o_ref, m_ref, l_ref, acc_ref, *, n_kv, n_lat_tiles):
    qi = pl.program_id(2)
    m_ref[...] = jnp.full(m_ref.shape, -jnp.inf, F32)
    l_ref[...] = jnp.zeros(l_ref.shape, F32)
    acc_ref[...] = jnp.zeros(acc_ref.shape, F32)
    lo = jnp.where(qi < n_lat_tiles, 0, n_lat_tiles)

    def body(j, carry):
        r0 = pl.multiple_of(j * TM, TM)
        k = k_ref[0, pl.ds(r0, TM), :]
        v = v_ref[0, pl.ds(r0, TM), :]
        for hd in range(2):
            q = q_ref[0, :, hd * HEAD_SLOT:(hd + 1) * HEAD_SLOT]
            s = _dot_nt(q, k[:, hd * HEAD_SLOT:(hd + 1) * HEAD_SLOT])
            m_prev = m_ref[hd]
            m_new = jnp.maximum(m_prev, jnp.max(s, axis=1, keepdims=True))
            alpha = jnp.exp(m_prev - m_new)
            p = jnp.exp(s - m_new)
            l_ref[hd] = alpha * l_ref[hd] + jnp.sum(p, axis=1, keepdims=True)
            acc_ref[hd] = alpha * acc_ref[hd] + _dot(p.astype(BF16), v)
            m_ref[hd] = m_new
        return carry

    lax.fori_loop(lo, n_kv, body, 0)
    lane = lax.broadcasted_iota(jnp.int32, (TM, 2 * MLA_V), 1)
    o0 = acc_ref[0] / l_ref[0]
    o1 = acc_ref[1] / l_ref[1]
    o_ref[0] = jnp.where(lane < MLA_V, o0, o1).astype(o_ref.dtype)


def _attention(q, k, v, n_q_tiles, n_lat_tiles):
    bsz, t, _ = q.shape
    n_kv = t // TM
    kern = functools.partial(_attn_kernel, n_kv=n_kv, n_lat_tiles=n_lat_tiles)
    return pl.pallas_call(
        kern,
        out_shape=jax.ShapeDtypeStruct((bsz, t, MLA_OUT), BF16),
        grid=(bsz, MLA_HEADS // 2, n_q_tiles),
        in_specs=[
            pl.BlockSpec((1, TM, 2 * HEAD_SLOT), lambda b, hp, i: (b, i, hp)),
            pl.BlockSpec((1, t, 2 * HEAD_SLOT), lambda b, hp, i: (b, 0, hp)),
            pl.BlockSpec((1, t, 2 * MLA_V), lambda b, hp, i: (b, 0, hp)),
        ],
        out_specs=pl.BlockSpec((1, TM, 2 * MLA_V), lambda b, hp, i: (b, i, hp)),
        scratch_shapes=[pltpu.VMEM((2, TM, 1), F32), pltpu.VMEM((2, TM, 1), F32),
                        pltpu.VMEM((2, TM, 2 * MLA_V), F32)],
        compiler_params=_cparams(3),
        name="mla_attention",
    )(q, k, v)


GLA_LEVELS = int(math.log2(TM))


def _gla_level_matrices():
    i = np.arange(TM)[:, None]
    t = np.arange(TM)[None, :]
    fwd = [(t <= i)]
    for lv in range(GLA_LEVELS):
        m = TM >> (lv + 1)
        lo = (i // m) * m
        later = ((i // m) % 2) == 1
        q_part = later & (t >= lo) & (t <= i)
        k_part = (~later) & (t > i) & (t <= lo + m - 1)
        fwd.append(q_part | k_part)
    fwd = np.concatenate(fwd, axis=0).astype(np.float32)
    nb = 1 + GLA_LEVELS
    bwd = fwd.reshape(nb, TM, TM)[:, ::-1, ::-1].reshape(nb * TM, TM)
    return np.stack([fwd, bwd])


def _gla_kernel(q_ref, k_ref, v_ref, g_ref, mall_ref, o_ref, s_ref, a_ref):
    d = pl.program_id(1)
    step = pl.program_id(2)

    @pl.when(step == 0)
    def _():
        s_ref[...] = jnp.zeros(s_ref.shape, F32)

    q = q_ref[0].astype(F32)
    k = k_ref[0].astype(F32)
    v = v_ref[0]
    g = g_ref[0]
    g_hi = g.astype(BF16)
    g_lo = (g - g_hi.astype(F32)).astype(BF16)
    e2 = _dot(mall_ref[0], jnp.concatenate([g_hi, g_lo], axis=1))
    e_all = e2[:, :GLA_DK] + e2[:, GLA_DK:]
    g_cum = e_all[0:TM]
    g_tot = jnp.sum(g, axis=0, keepdims=True)

    row = lax.broadcasted_iota(jnp.int32, (TM, TM), 0)
    col = lax.broadcasted_iota(jnp.int32, (TM, TM), 1)
    lane_head = lax.broadcasted_iota(jnp.int32, (TM, GLA_DK), 1) // GLA_HK
    tok = lax.broadcasted_iota(jnp.int32, (TM, GLA_DK), 0)

    def stack_heads(t):
        return jnp.concatenate([jnp.where(lane_head == hd, t, 0.0) for hd in range(GLA_HEADS)], axis=0).astype(BF16)

    res = _dot_nt(stack_heads(q), k.astype(BF16))
    eye = row == col
    for hd in range(GLA_HEADS):
        a_ref[hd] = jnp.where(eye, res[hd * TM:(hd + 1) * TM], 0.0)
    for lv in range(GLA_LEVELS):
        m = TM >> (lv + 1)
        w = jnp.exp(e_all[(lv + 1) * TM:(lv + 2) * TM])
        bit = (tok // m) % 2
        q_act = bit != d
        qt = jnp.where(q_act, q * w, 0.0)
        kt = jnp.where(q_act, 0.0, k * w).astype(BF16)
        res = _dot_nt(stack_heads(qt), kt)
        if m == TM // 2:
            for hd in range(GLA_HEADS):
                a_ref[hd] += res[hd * TM:(hd + 1) * TM]
        else:
            same = (row // (2 * m)) == (col // (2 * m))
            for hd in range(GLA_HEADS):
                a_ref[hd] += jnp.where(same, res[hd * TM:(hd + 1) * TM], 0.0)

    s_old = s_ref[...]
    s_bf = s_old.astype(BF16)
    qg = q * jnp.exp(g_cum)
    for hd in range(GLA_HEADS):
        vh = v[:, hd * GLA_HV:(hd + 1) * GLA_HV]
        o_intra = _dot(a_ref[hd].astype(BF16), vh)
        o_inter = _dot(jnp.where(lane_head == hd, qg, 0.0).astype(BF16), s_bf)
        o_ref[0, 0, :, hd * GLA_HV:(hd + 1) * GLA_HV] = o_intra + o_inter

    kdec_t = (k * jnp.exp(g_tot - g_cum)).T.astype(BF16)
    upd = _dot(kdec_t, v)
    row_head = lax.broadcasted_iota(jnp.int32, (GLA_DK, GLA_HV), 0) // GLA_HK
    new = jnp.zeros((GLA_DK, GLA_HV), F32)
    for hd in range(GLA_HEADS):
        new = new + jnp.where(row_head == hd, upd[:, hd * GLA_HV:(hd + 1) * GLA_HV], 0.0)
    dec_col = jnp.sum(jnp.where(eye, jnp.broadcast_to(jnp.exp(g_tot), (TM, GLA_DK)), 0.0), axis=1, keepdims=True)
    s_ref[...] = dec_col * s_old + new


def _gla(gq, gk, gv, glog, n_lat_tiles):
    assert GLA_DK == TM
    bsz, t, _ = gq.shape
    nt = t // TM
    mall = jnp.asarray(_gla_level_matrices(), dtype=BF16)

    def tile_idx(d, s):
        return jnp.where(s == 0, n_lat_tiles, jnp.where(d == 0, s - 1, n_lat_tiles - s))

    tok = lambda w: pl.BlockSpec((1, TM, w), lambda b, d, s: (b, tile_idx(d, s), 0))
    return pl.pallas_call(
        _gla_kernel,
        out_shape=jax.ShapeDtypeStruct((bsz, 2, t, GLA_DV), F32),
        grid=(bsz, 2, nt),
        in_specs=[tok(GLA_DK), tok(GLA_DK), tok(GLA_DV),
                  pl.BlockSpec((1, TM, GLA_DK), lambda b, d, s: (b, tile_idx(d, s), d)),
                  pl.BlockSpec((1,) + mall.shape[1:], lambda b, d, s: (d, 0, 0))],
        out_specs=pl.BlockSpec((1, 1, TM, GLA_DV), lambda b, d, s: (b, d, tile_idx(d, s), 0)),
        scratch_shapes=[pltpu.VMEM((GLA_DK, GLA_HV), F32), pltpu.VMEM((GLA_HEADS, TM, TM), F32)],
        compiler_params=_cparams(3),
        name="gla_scan",
    )(gq, gk, gv, glog, mall)


HY_N1 = 64
HY_N2 = 128
HY_SLAB = SUBLANES
HY_CB = 128


def _hyena_dft_constants(seq):
    n = 2 * seq
    assert n == HY_N1 * HY_N2
    nh = HY_N1 // 2
    kh = HY_N2 // 2
    eye = np.eye(HY_SLAB)
    k1 = np.arange(HY_N1)
    th = 2 * np.pi * np.outer(k1 + 0.5, np.arange(nh)) / HY_N1
    fwd1 = np.concatenate([np.kron(np.cos(th), eye), np.kron(-np.sin(th), eye)], axis=0)
    inv1 = (2.0 / n) * np.concatenate([np.kron(np.cos(th).T, eye), np.kron(-np.sin(th).T, eye)], axis=1)
    nlo = np.arange(HY_N2)
    k2 = np.arange(kh)
    ph = 2 * np.pi * (k2[None, :, None] * nlo[None, None, :] / HY_N2
                      + (k1[:, None, None] + 0.5) * nlo[None, None, :] / n)
    c, s = np.cos(ph), np.sin(ph)
    fwd2 = np.concatenate([np.concatenate([c, s], axis=2), np.concatenate([-s, c], axis=2)], axis=1)
    ct, st = c.transpose(0, 2, 1), s.transpose(0, 2, 1)
    inv2 = np.concatenate([np.concatenate([ct, -st], axis=2), np.concatenate([st, ct], axis=2)], axis=1)
    return fwd1, fwd2, inv2, inv1


def _hyena_ctx_dft_constants(ctx_len):
    n = 2 * ctx_len
    th = 2 * np.pi * np.outer(np.arange(ctx_len) + 0.5, np.arange(ctx_len)) / n
    fwd = np.concatenate([np.cos(th), -np.sin(th)], axis=0)
    inv = (2.0 / n) * np.concatenate([np.cos(th).T, -np.sin(th).T], axis=1)
    return fwd, inv


def _hyena_features(length):
    pos = jnp.arange(length, dtype=F32)
    t = pos / max(length - 1, 1)
    f = jnp.linspace(1e-4, HY_BANDS - 1, HY_BANDS, dtype=F32)
    ang = (2.0 * math.pi / length) * pos[:, None] * f
    feat = jnp.concatenate([t[:, None], jnp.cos(ang), jnp.sin(ang)], axis=-1)
    return jnp.pad(feat, ((0, 0), (0, LANES - HY_POS_DIM)))


def _hy_filter_kernel(feat_ref, w1_ref, b1_ref, w2_ref, b2_ref, w3_ref, b3_ref, absd_ref, h_ref, s_ref):
    i = pl.program_id(0)
    feat = feat_ref[...]
    hp = lambda a, b: jnp.dot(a, b, preferred_element_type=F32, precision=HIGHEST)
    hdn = jnp.sin(hp(feat, w1_ref[...]) + b1_ref[...])
    hdn = jnp.sin(hp(hdn, w2_ref[...]) + b2_ref[...])
    h = hp(hdn, w3_ref[...]) + b3_ref[...]
    window = jnp.exp(-feat[:, 0:1] * absd_ref[...])
    h = h * jnp.concatenate([window] * (2 * HY_ORDER), axis=1)
    h_ref[...] = h

    @pl.when(i == 0)
    def _():
        s_ref[...] = jnp.zeros(s_ref.shape, F32)

    s_ref[...] += jnp.sum(jnp.abs(h), axis=0, keepdims=True)


def _hyena_filters_raw(length, filt_w):
    w1, b1, w2, b2, w3, b3 = filt_w
    nf = 2 * HY_ORDER * HY_WIDTH
    tr = min(length, 512)
    deltas = np.linspace(math.log(HY_DECAY_TARGET) / HY_FAST_DECAY, math.log(HY_DECAY_TARGET) / HY_SLOW_DECAY,
                         HY_WIDTH, dtype=np.float32)
    absd = jnp.asarray(np.abs(deltas))[None, :]
    w1p = jnp.pad(w1, ((0, LANES - HY_POS_DIM), (0, 0)))
    full = lambda shp: pl.BlockSpec(shp, lambda i: (0,) * len(shp))
    return pl.pallas_call(
        _hy_filter_kernel,
        out_shape=(jax.ShapeDtypeStruct((length, nf), F32), jax.ShapeDtypeStruct((1, nf), F32)),
        grid=(length // tr,),
        in_specs=[pl.BlockSpec((tr, LANES), lambda i: (i, 0)), full((LANES, HY_FILTER_HIDDEN)),
                  full((1, HY_FILTER_HIDDEN)), full((HY_FILTER_HIDDEN, HY_FILTER_HIDDEN)), full((1, HY_FILTER_HIDDEN)),
                  full((HY_FILTER_HIDDEN, nf)), full((1, nf)), full((1, HY_WIDTH))],
        out_specs=(pl.BlockSpec((tr, nf), lambda i: (i, 0)), full((1, nf))),
        compiler_params=_cparams(1),
        name="hyena_filter_mlp",
    )(_hyena_features(length), w1p, b1[None, :], w2, b2[None, :], w3, b3[None, :], absd)


def _odft_stage1(src_at, fwd1, s_re, s_im, cast, prec):
    half = HY_N1 * HY_SLAB

    def body(j, carry):
        r0 = pl.multiple_of(j * HY_SLAB, HY_SLAB)
        slab = src_at(r0)
        cb = slab.shape[-1]
        res = jnp.dot(fwd1, slab.reshape(-1, cb).astype(cast), preferred_element_type=F32, precision=prec)
        s_re[:, pl.ds(r0, HY_SLAB), :] = res[:half].reshape(HY_N1, HY_SLAB, cb)
        s_im[:, pl.ds(r0, HY_SLAB), :] = res[half:].reshape(HY_N1, HY_SLAB, cb)
        return carry

    lax.fori_loop(0, HY_N2 // HY_SLAB, body, 0)


def _hy_spectrum_kernel(hf_ref, hb_ref, sf_ref, sb_ref, fwd1_ref, fwd2_ref, o_ref, s_re, s_im):
    kh = HY_N2 // 2
    fwd1 = fwd1_ref[...]

    def middle(sign):
        def body(k1, carry):
            a = jnp.concatenate([s_re[k1], s_im[k1]], axis=0)
            x = jnp.dot(fwd2_ref[k1], a, preferred_element_type=F32, precision=HIGHEST)
            if sign is None:
                o_ref[0, k1] = x[:kh]
                o_ref[1, k1] = x[kh:]
            else:
                inv_norm = 1.0 / (sf_ref[...] + sb_ref[...])
                o_ref[0, k1] = (o_ref[0, k1] + x[:kh]) * inv_norm
                o_ref[1, k1] = (o_ref[1, k1] - x[kh:]) * inv_norm
            return carry
        lax.fori_loop(0, HY_N1, body, 0)

    _odft_stage1(lambda r0: hf_ref[:, pl.ds(r0, HY_SLAB), :], fwd1, s_re, s_im, F32, HIGHEST)
    middle(None)

    def bwd_slab(r0):
        slab = hb_ref[:, pl.ds(r0, HY_SLAB), :]
        nhi = lax.broadcasted_iota(jnp.int32, slab.shape, 0)
        r = lax.broadcasted_iota(jnp.int32, slab.shape, 1)
        return jnp.where((nhi == 0) & (r + r0 == 0), 0.0, slab)

    _odft_stage1(bwd_slab, fwd1, s_re, s_im, F32, HIGHEST)
    middle(-1)


def _hyena_filter_spectrum(seq, w1, b1, w2, b2, w3, b3):
    h_raw, s = _hyena_filters_raw(seq, (w1, b1, w2, b2, w3, b3))
    nh = HY_N1 // 2
    nc = HY_ORDER * HY_WIDTH
    h3 = h_raw.reshape(nh, HY_N2, 2 * nc)
    fwd1, fwd2, _, _ = _hyena_dft_constants(seq)
    fwd1 = jnp.asarray(fwd1, F32)
    fwd2 = jnp.asarray(fwd2, F32)
    ncb = nc // HY_CB
    return pl.pallas_call(
        _hy_spectrum_kernel,
        out_shape=jax.ShapeDtypeStruct((2, HY_N1, HY_N2 // 2, nc), F32),
        grid=(ncb,),
        in_specs=[pl.BlockSpec((nh, HY_N2, HY_CB), lambda c: (0, 0, c)),
                  pl.BlockSpec((nh, HY_N2, HY_CB), lambda c: (0, 0, ncb + c)),
                  pl.BlockSpec((1, HY_CB), lambda c: (0, c)),
                  pl.BlockSpec((1, HY_CB), lambda c: (0, ncb + c)),
                  _const_spec(fwd1.shape), _const_spec(fwd2.shape)],
        out_specs=pl.BlockSpec((2, HY_N1, HY_N2 // 2, HY_CB), lambda c: (0, 0, 0, c)),
        scratch_shapes=[pltpu.VMEM((HY_N1, HY_N2, HY_CB), F32), pltpu.VMEM((HY_N1, HY_N2, HY_CB), F32)],
        compiler_params=_cparams(1, VMEM_LIMIT),
        name="hyena_filter_spectrum",
    )(h3, h3, s, s, fwd1, fwd2)


def _short_conv_chunk(ref, c, n_chunks, w_ref, b_ref):
    per = TM // HY_N2
    cur = ref[0, pl.ds(per * c, per)]
    cb = cur.shape[-1]
    cur = cur.reshape(TM, cb)
    prev = ref[0, jnp.maximum(per * c - 1, 0), pl.ds(HY_N2 - SUBLANES, SUBLANES), :][SUBLANES - 1:SUBLANES]
    nxt = ref[0, jnp.minimum(per * c + per, per * n_chunks - 1), pl.ds(0, SUBLANES), :][0:1]
    prev = jnp.where(c > 0, prev, 0.0)
    nxt = jnp.where(c < n_chunks - 1, nxt, 0.0)
    rowi = lax.broadcasted_iota(jnp.int32, (TM, cb), 0)
    dn = jnp.where(rowi == 0, prev, pltpu.roll(cur, 1, 0))
    up = jnp.where(rowi == TM - 1, nxt, pltpu.roll(cur, TM - 1, 0))
    return b_ref[...] + w_ref[0:1, :] * dn + w_ref[1:2, :] * cur + w_ref[2:3, :] * up


def _hy_conv_kernel(*refs, conv_y):
    if conv_y:
        (y_ref, g_ref, h_ref, bias_ref, wy_ref, by_ref, wg_ref, bg_ref, fwd1_ref, fwd2_ref, inv2_ref, inv1_ref,
         o_ref, s_re, s_im, gs_ref, us_ref) = refs
    else:
        (y_ref, g_ref, h_ref, bias_ref, wg_ref, bg_ref, fwd1_ref, fwd2_ref, inv2_ref, inv1_ref,
         o_ref, s_re, s_im, gs_ref) = refs
    nh = HY_N1 // 2
    per = TM // HY_N2
    n_chunks = nh // per
    cb = o_ref.shape[-1]
    kh = HY_N2 // 2

    def pre(c, carry):
        gs_ref[pl.ds(per * c, per)] = _short_conv_chunk(g_ref, c, n_chunks, wg_ref, bg_ref).reshape(per, HY_N2, cb)
        if conv_y:
            us_ref[pl.ds(per * c, per)] = _short_conv_chunk(y_ref, c, n_chunks, wy_ref, by_ref).reshape(per, HY_N2, cb)
        return carry

    lax.fori_loop(0, n_chunks, pre, 0)
    if conv_y:
        u_at = lambda r0: us_ref[:, pl.ds(r0, HY_SLAB), :]
    else:
        u_at = lambda r0: y_ref[0, :, pl.ds(r0, HY_SLAB), :]

    _odft_stage1(u_at, fwd1_ref[...], s_re, s_im, BF16, None)

    def middle(k1, carry):
        a = jnp.concatenate([s_re[k1], s_im[k1]], axis=0).astype(BF16)
        x = _dot(fwd2_ref[k1], a)
        xr, xi = x[:kh], x[kh:]
        hr, hi = h_ref[0, k1], h_ref[1, k1]
        y = jnp.concatenate([xr * hr - xi * hi, xr * hi + xi * hr], axis=0).astype(BF16)
        bm = _dot(inv2_ref[k1], y)
        s_re[k1] = bm[:HY_N2]
        s_im[k1] = bm[HY_N2:]
        return carry

    lax.fori_loop(0, HY_N1, middle, 0)

    def post(j, carry):
        r0 = pl.multiple_of(j * HY_SLAB, HY_SLAB)
        slab = jnp.concatenate([s_re[:, pl.ds(r0, HY_SLAB), :].reshape(HY_N1 * HY_SLAB, cb),
                                s_im[:, pl.ds(r0, HY_SLAB), :].reshape(HY_N1 * HY_SLAB, cb)], axis=0).astype(BF16)
        conv = _dot(inv1_ref[...], slab).reshape(nh, HY_SLAB, cb)
        u = u_at(r0)
        o_ref[0, :, pl.ds(r0, HY_SLAB), :] = gs_ref[:, pl.ds(r0, HY_SLAB), :] * (conv + bias_ref[...] * u)
        return carry

    lax.fori_loop(0, HY_N2 // HY_SLAB, post, 0)


def _hyena_order(y4, y_col0, z4, gate_col0, hspec, order, hy_bias, short_w, short_b, consts, conv_y):
    bsz = z4.shape[0]
    nh = HY_N1 // 2
    ncb = HY_WIDTH // HY_CB
    fwd1, fwd2, inv2, inv1 = consts
    blk4 = lambda off: pl.BlockSpec((1, nh, HY_N2, HY_CB), lambda c, b: (b, 0, 0, off + c))
    rowspec = lambda rows, off: pl.BlockSpec((rows, HY_CB), lambda c, b: (0, off + c))
    in_specs = [blk4(y_col0), blk4(gate_col0),
                pl.BlockSpec((2, HY_N1, HY_N2 // 2, HY_CB), lambda c, b: (0, 0, 0, order * ncb + c),
                             pipeline_mode=pl.Buffered(1)),
                rowspec(1, 0)]
    args = [y4, z4, hspec, hy_bias.reshape(1, -1)]
    if conv_y:
        in_specs += [rowspec(HY_SHORT, y_col0), rowspec(1, y_col0)]
        args += [short_w, short_b[None, :]]
    in_specs += [rowspec(HY_SHORT, gate_col0), rowspec(1, gate_col0)]
    args += [short_w, short_b[None, :]]
    in_specs += [_const_spec(fwd1.shape), _const_spec(fwd2.shape), _const_spec(inv2.shape), _const_spec(inv1.shape)]
    args += [fwd1, fwd2, inv2, inv1]
    scratch = [pltpu.VMEM((HY_N1, HY_N2, HY_CB), F32), pltpu.VMEM((HY_N1, HY_N2, HY_CB), F32),
               pltpu.VMEM((nh, HY_N2, HY_CB), F32)]
    if conv_y:
        scratch.append(pltpu.VMEM((nh, HY_N2, HY_CB), F32))
    return pl.pallas_call(
        functools.partial(_hy_conv_kernel, conv_y=conv_y),
        out_shape=jax.ShapeDtypeStruct((bsz, nh, HY_N2, HY_WIDTH), F32),
        grid=(ncb, bsz),
        in_specs=in_specs,
        out_specs=pl.BlockSpec((1, nh, HY_N2, HY_CB), lambda c, b: (b, 0, 0, c)),
        scratch_shapes=scratch,
        compiler_params=_cparams(2, VMEM_LIMIT),
        name="hyena_conv%d" % order,
    )(*args)


def _hyena_latent(z_hy, hspec, short_w, short_b, hy_bias):
    bsz, t, _ = z_hy.shape
    seq = HY_N1 * HY_N2 // 2
    consts = tuple(jnp.asarray(m, BF16) for m in _hyena_dft_constants(seq))
    z4 = z_hy.reshape(bsz, t // HY_N2, HY_N2, 3 * HY_WIDTH)
    ncb = HY_WIDTH // HY_CB
    y1 = _hyena_order(z4, 2 * ncb, z4, 0, hspec, 0, hy_bias[0], short_w, short_b, consts, True)
    y2 = _hyena_order(y1, 0, z4, ncb, hspec, 1, hy_bias[1], short_w, short_b, consts, False)
    return y2.reshape(bsz, seq, HY_WIDTH)


def _hy_ctx_spectrum_kernel(h_ref, s_ref, fwd_ref, o_ref):
    lc = h_ref.shape[0]
    nc = HY_ORDER * HY_WIDTH
    hp = lambda a, b: jnp.dot(a, b, preferred_element_type=F32, precision=HIGHEST)
    h = h_ref[...]
    rowi = lax.broadcasted_iota(jnp.int32, (lc, nc), 0)
    xf = hp(fwd_ref[...], h[:, :nc])
    xb = hp(fwd_ref[...], jnp.where(rowi == 0, 0.0, h[:, nc:]))
    inv_norm = 1.0 / (s_ref[:, :nc] + s_ref[:, nc:])
    o_ref[0] = (xf[:lc] + xb[:lc]) * inv_norm
    o_ref[1] = (xf[lc:] - xb[lc:]) * inv_norm


def _hy_ctx_conv_kernel(x1_ref, x2_ref, v_ref, h_ref, bias_ref, w_ref, b_ref, fwd_ref, inv_ref, o_ref):
    lc = o_ref.shape[1]
    rowi = lax.broadcasted_iota(jnp.int32, (lc, HY_WIDTH), 0)

    def short(ref, part):
        cur = ref[0].reshape(lc, HY_WIDTH)
        sl = slice(part * HY_WIDTH, (part + 1) * HY_WIDTH)
        dn = jnp.where(rowi == 0, 0.0, pltpu.roll(cur, 1, 0))
        up = jnp.where(rowi == lc - 1, 0.0, pltpu.roll(cur, lc - 1, 0))
        return b_ref[:, sl] + w_ref[0:1, sl] * dn + w_ref[1:2, sl] * cur + w_ref[2:3, sl] * up

    y = short(v_ref, 2)
    for order, gref in enumerate((x1_ref, x2_ref)):
        sl = slice(order * HY_WIDTH, (order + 1) * HY_WIDTH)
        x = _dot(fwd_ref[...], y.astype(BF16))
        xr, xi = x[:lc], x[lc:]
        hr, hi = h_ref[0, :, sl], h_ref[1, :, sl]
        prod = jnp.concatenate([xr * hr - xi * hi, xr * hi + xi * hr], axis=0).astype(BF16)
        conv = _dot(inv_ref[...], prod)
        y = short(gref, order) * (conv + bias_ref[order:order + 1, :] * y)
    o_ref[0] = y


def _hyena_ctx(z_hy, filt_w, short_w, short_b, hy_bias):
    bsz, t, _ = z_hy.shape
    seq = HY_N1 * HY_N2 // 2
    lc = t - seq
    per = lc // HY_N2
    h_raw, s = _hyena_filters_raw(lc, filt_w)
    fwd, inv = _hyena_ctx_dft_constants(lc)
    nc = HY_ORDER * HY_WIDTH
    full = lambda shp: pl.BlockSpec(shp, lambda *_: (0,) * len(shp))
    hspec = pl.pallas_call(
        _hy_ctx_spectrum_kernel,
        out_shape=jax.ShapeDtypeStruct((2, lc, nc), F32),
        grid=(1,),
        in_specs=[full(h_raw.shape), full(s.shape), full(fwd.shape)],
        out_specs=full((2, lc, nc)),
        compiler_params=_cparams(1),
        name="hyena_ctx_spectrum",
    )(h_raw, s, jnp.asarray(fwd, F32))
    z4 = z_hy.reshape(bsz, t // HY_N2, HY_N2, 3 * HY_WIDTH)
    blk = lambda part: pl.BlockSpec((1, per, HY_N2, HY_WIDTH), lambda b: (b, seq // lc, 0, part))
    return pl.pallas_call(
        _hy_ctx_conv_kernel,
        out_shape=jax.ShapeDtypeStruct((bsz, lc, HY_WIDTH), F32),
        grid=(bsz,),
        in_specs=[blk(0), blk(1), blk(2), full((2, lc, nc)), full((HY_ORDER, HY_WIDTH)),
                  full((HY_SHORT, 3 * HY_WIDTH)), full((1, 3 * HY_WIDTH)), full(fwd.shape), full(inv.shape)],
        out_specs=pl.BlockSpec((1, lc, HY_WIDTH), lambda b: (b, 0, 0)),
        compiler_params=_cparams(1),
        name="hyena_ctx_conv",
    )(z4, z4, z4, hspec, hy_bias, short_w, short_b[None, :], jnp.asarray(fwd, BF16), jnp.asarray(inv, BF16))


def _merge_kernel(x_ref, mod_ref, ymla_ref, of_ref, ob_ref, gr_ref, yhy_ref, zg_ref, on_ref,
                  wm_ref, wgl_ref, wh_ref, wo_ref, o_ref):
    o = of_ref[0, 0] + ob_ref[0, 0]
    gr = gr_ref[0]
    silu = gr * _sigmoid(gr)
    parts = []
    for hd in range(GLA_HEADS):
        sl = slice(hd * GLA_HV, (hd + 1) * GLA_HV)
        parts.append((_rms(o[:, sl]) * on_ref[...] * silu[:, sl]).astype(BF16))
    y_gla = jnp.concatenate(parts, axis=1)
    zg = zg_ref[0]
    d = x_ref.shape[-1]
    m = _sigmoid(zg[:, 0:d]) * _dot(ymla_ref[0], wm_ref[...])
    m = m + _sigmoid(zg[:, d:2 * d]) * _dot(y_gla, wgl_ref[...])
    m = m + _sigmoid(zg[:, 2 * d:3 * d]) * _dot(yhy_ref[0].astype(BF16), wh_ref[...])
    out = _dot(m.astype(BF16), wo_ref[...])
    o_ref[0] = x_ref[0] + mod_ref[0, 2:3, :] * out


def _mod_spec(d, n_lat_tiles, bsz):
    return pl.BlockSpec((1, 6, d), lambda b, i: (jnp.where(i < n_lat_tiles, b, bsz), 0, 0))


def _merge(xc, mod_l, y_mla, o_gla, gr, y_hy, gate, out_norm, w_o_mla, w_o_gla, w_o_hy, w_out, n_tiles, n_lat_tiles):
    bsz, t, d = xc.shape
    tile = lambda w: pl.BlockSpec((1, TM, w), lambda b, i: (b, i, 0))
    dirspec = lambda dr: pl.BlockSpec((1, 1, TM, GLA_DV), lambda b, i: (b, dr, i, 0))
    bf = lambda w: w.astype(BF16)
    return pl.pallas_call(
        _merge_kernel,
        out_shape=jax.ShapeDtypeStruct((bsz, n_tiles * TM, d), F32),
        grid=(bsz, n_tiles),
        in_specs=[tile(d), _mod_spec(d, n_lat_tiles, bsz), tile(MLA_OUT), dirspec(0), dirspec(1), tile(GLA_DV),
                  tile(HY_WIDTH), tile(3 * d), pl.BlockSpec((1, GLA_HV), lambda b, i: (0, 0)),
                  _const_spec(w_o_mla.shape), _const_spec(w_o_gla.shape), _const_spec(w_o_hy.shape),
                  _const_spec(w_out.shape)],
        out_specs=tile(d),
        compiler_params=_cparams(2, VMEM_LIMIT),
        name="merge",
    )(xc, mod_l, y_mla, o_gla, o_gla, gr, y_hy, gate, out_norm[None, :], bf(w_o_mla), bf(w_o_gla), bf(w_o_hy),
      bf(w_out))


FF_CHUNK = 1024


def _mlp_kernel(x_ref, mod_ref, g2_ref, w1_ref, w2_ref, fg_ref, o_ref, *, final):
    x = x_ref[0]
    h = (_rms(x) * g2_ref[...] * (1.0 + mod_ref[0, 4:5, :]) + mod_ref[0, 3:4, :]).astype(BF16)
    acc = jnp.zeros(x.shape, F32)
    for j in range(w1_ref.shape[1] // FF_CHUNK):
        a = jnp.maximum(_dot(h, w1_ref[:, j * FF_CHUNK:(j + 1) * FF_CHUNK]), 0.0)
        acc = acc + _dot((a * a).astype(BF16), w2_ref[j * FF_CHUNK:(j + 1) * FF_CHUNK, :])
    xn = x + mod_ref[0, 5:6, :] * acc
    if final:
        xn = _rms(xn) * fg_ref[...]
    o_ref[0] = xn


def _mlp(xc, mod_l, g2, w1, w2, final_g, n_tiles, n_lat_tiles, final):
    bsz, t, d = xc.shape
    tile = pl.BlockSpec((1, TM, d), lambda b, i: (b, i, 0))
    row = pl.BlockSpec((1, d), lambda b, i: (0, 0))
    t_out = n_tiles * TM if final else t
    return pl.pallas_call(
        functools.partial(_mlp_kernel, final=final),
        out_shape=jax.ShapeDtypeStruct((bsz, t_out, d), F32),
        grid=(bsz, n_tiles),
        in_specs=[tile, _mod_spec(d, n_lat_tiles, bsz), row, _const_spec(w1.shape), _const_spec(w2.shape), row],
        out_specs=tile,
        compiler_params=_cparams(2, VMEM_LIMIT),
        name="mlp",
    )(xc, mod_l, g2[None, :], w1.astype(BF16), w2.astype(BF16), final_g[None, :])


def kernel(x, c, ctx, c_ctx, ada_w, ada_b, norm1_g, norm2_g, w_in, mla_q_norm, mla_w_uq, mla_kv_norm, mla_w_ukv, gla_w_a2, gla_b_a, gla_out_norm, hy_short_w, hy_short_b, hy_f_w1, hy_f_b1, hy_f_w2, hy_f_b2, hy_f_w3, hy_f_b3, hy_bias, w_o_mla, w_o_gla, w_o_hy, w_out, ff_w1, ff_w2, final_norm_g):
    bsz, seq, d = x.shape
    ctx_len = ctx.shape[1]
    n_lat = seq // TM
    n_all = (seq + ctx_len) // TM
    xc = jnp.concatenate([x, ctx], axis=1)
    cc = jnp.zeros((16, d), F32).at[:bsz].set(c).at[bsz].set(c_ctx)
    mod = _modulation(cc, ada_w, ada_b).reshape(DEPTH, 16, 6, d)
    cos, sin = _rope_tables(seq, ctx_len)
    for l in range(DEPTH):
        last = l == DEPTH - 1
        n_tiles = n_lat if last else n_all
        weights = _prep_inproj_weights(w_in[l], mla_w_uq[l], mla_w_ukv[l], gla_w_a2[l], gla_b_a[l])
        q, k, v, gq, gk, gv, gr, glog, z_hy, z_gate = _inproj(xc, mod[l], norm1_g[l], weights, mla_q_norm[l],
                                                             mla_kv_norm[l], cos, sin, n_lat)
        y_mla = _attention(q, k, v, n_tiles, n_lat)
        o_gla = _gla(gq, gk, gv, glog, n_lat)
        filt_w = (hy_f_w1[l], hy_f_b1[l], hy_f_w2[l], hy_f_b2[l], hy_f_w3[l], hy_f_b3[l])
        hspec = _hyena_filter_spectrum(seq, *filt_w)
        y_hy = _hyena_latent(z_hy, hspec, hy_short_w[l], hy_short_b[l], hy_bias[l])
        if not last:
            y_hy_c = _hyena_ctx(z_hy, filt_w, hy_short_w[l], hy_short_b[l], hy_bias[l])
            y_hy = jnp.concatenate([y_hy, y_hy_c], axis=1)
        xc = _merge(xc, mod[l], y_mla, o_gla, gr, y_hy, z_gate, gla_out_norm[l], w_o_mla[l], w_o_gla[l], w_o_hy[l],
                    w_out[l], n_tiles, n_lat)
        xc = _mlp(xc, mod[l], norm2_g[l], ff_w1[l], ff_w2[l], final_norm_g, n_tiles, n_lat, last)
    return xc
```

```python
import functools
import math

import numpy as np
import jax
import jax.numpy as jnp
from jax import lax
from jax.experimental import pallas as pl
from jax.experimental.pallas import tpu as pltpu

F32 = jnp.float32
BF16 = jnp.bfloat16
HIGHEST = lax.Precision.HIGHEST
LOG2E = 1.4426950408889634

D_MODEL = 1024
DEPTH = 2
GRID_W = 64
EPS = 1e-6
MLA_HEADS = 8
MLA_NOPE = 64
MLA_ROPE = 32
MLA_V = 64
MLA_Q_LORA = 256
MLA_KV_LORA = 128
MLA_SCALE = (MLA_NOPE + MLA_ROPE) ** -0.5
ROPE_BASE = 10000.0
GLA_HEADS = 4
GLA_DK = 256
GLA_DV = 512
GLA_HK = GLA_DK // GLA_HEADS
GLA_HV = GLA_DV // GLA_HEADS
GLA_GATE_RANK = 16
GLA_TAU = 16.0
HY_WIDTH = 512
HY_ORDER = 2
HY_SHORT = 3
HY_BANDS = 16
HY_POS_DIM = 1 + 2 * HY_BANDS
HY_FILTER_HIDDEN = 64
HY_FAST_DECAY = 0.3
HY_SLOW_DECAY = 1.5
HY_DECAY_TARGET = 1e-2
D_FF = 4 * D_MODEL
MLA_OUT = MLA_HEADS * MLA_V
IN_SIZES = (MLA_Q_LORA, MLA_KV_LORA, MLA_ROPE, GLA_DK, GLA_DK, GLA_DV, GLA_DV, GLA_GATE_RANK, GLA_GATE_RANK,
            (HY_ORDER + 1) * HY_WIDTH, 3 * D_MODEL)

LANES = 128
SUBLANES = 8
TM = 256
HEAD_SLOT = 128
VMEM_LIMIT = 56 * 1024 * 1024


def _cparams(n_axes, vmem=None):
    return pltpu.CompilerParams(dimension_semantics=("arbitrary",) * n_axes, vmem_limit_bytes=vmem)


def _const_spec(shape):
    nd = len(shape)
    return pl.BlockSpec(shape, lambda *_: (0,) * nd, pipeline_mode=pl.Buffered(1))


def _rms(x):
    return x * lax.rsqrt(jnp.mean(x * x, axis=-1, keepdims=True) + EPS)


def _sigmoid(x):
    return 1.0 / (1.0 + jnp.exp(-x))


def _dot(a, b):
    return jnp.dot(a, b, preferred_element_type=F32)


def _dot_nt(a, b):
    return lax.dot_general(a, b, (((1,), (1,)), ((), ())), preferred_element_type=F32)


def _mod_kernel(cc_ref, w_ref, b_ref, o_ref):
    s = cc_ref[...]
    s = s * _sigmoid(s)
    o_ref[0] = jnp.dot(s, w_ref[0], preferred_element_type=F32, precision=HIGHEST) + b_ref[0]


def _modulation(cc, ada_w, ada_b):
    tn = 1536
    n6 = ada_w.shape[-1]
    return pl.pallas_call(
        _mod_kernel,
        out_shape=jax.ShapeDtypeStruct((DEPTH, 16, n6), F32),
        grid=(DEPTH, n6 // tn),
        in_specs=[
            pl.BlockSpec((16, D_MODEL), lambda l, j: (0, 0)),
            pl.BlockSpec((1, D_MODEL, tn), lambda l, j: (l, 0, j)),
            pl.BlockSpec((1, 1, tn), lambda l, j: (l, 0, j)),
        ],
        out_specs=pl.BlockSpec((1, 16, tn), lambda l, j: (l, 0, j)),
        compiler_params=_cparams(2),
        name="modulation",
    )(cc, ada_w, ada_b.reshape(DEPTH, 1, n6))


W_A = 768
W_G = 2 * GLA_DK + 2 * GLA_DV


def _inproj_kernel(x_ref, mod_ref, g1_ref, wa_ref, wg_ref, wh_ref, wz_ref, qn_ref, kvn_ref, wuq_ref, wukv_ref,
                   wa2_ref, ba_ref, cos_ref, sin_ref,
                   q_out, k_out, v_out, gq_out, gk_out, gv_out, gr_out, glog_out, hy_out, gate_out):
    x = x_ref[0]
    shift = mod_ref[0, 0:1, :]
    scale = mod_ref[0, 1:2, :]
    h = (_rms(x) * g1_ref[...] * (1.0 + scale) + shift).astype(BF16)

    za = _dot(h, wa_ref[...])
    cos = cos_ref[...]
    sin = sin_ref[...]

    cqn = (_rms(za[:, 0:256]) * qn_ref[...]).astype(BF16)
    qab = _dot(cqn, wuq_ref[...])
    nq = MLA_HEADS * HEAD_SLOT
    for hd in range(MLA_HEADS):
        sl = slice(hd * HEAD_SLOT, (hd + 1) * HEAD_SLOT)
        qa = qab[:, hd * HEAD_SLOT:(hd + 1) * HEAD_SLOT]
        qb = qab[:, nq + hd * HEAD_SLOT:nq + (hd + 1) * HEAD_SLOT]
        q_out[0, :, sl] = ((qa * cos + qb * sin) * (MLA_SCALE * LOG2E)).astype(BF16)

    ckvn = (_rms(za[:, 256:384]) * kvn_ref[...]).astype(BF16)
    kv = _dot(ckvn, wukv_ref[...])
    krot = za[:, 384:512] * cos + za[:, 512:640] * sin
    for hd in range(MLA_HEADS):
        sl = slice(hd * HEAD_SLOT, (hd + 1) * HEAD_SLOT)
        k_out[0, :, sl] = (kv[:, sl] + krot).astype(BF16)
    ones_hi = (lax.broadcasted_iota(jnp.int32, (1, HEAD_SLOT), 1) >= MLA_V).astype(F32)
    for hd in range(MLA_HEADS):
        sl = slice(hd * HEAD_SLOT, (hd + 1) * HEAD_SLOT)
        v_out[0, :, sl] = (kv[:, nq + hd * HEAD_SLOT:nq + (hd + 1) * HEAD_SLOT] + ones_hi).astype(BF16)

    xg = _dot(za[:, 640:768].astype(BF16), wa2_ref[...]) + ba_ref[...]
    glog_out[0] = (jnp.minimum(xg, 0.0) - jnp.log(1.0 + jnp.exp(-jnp.abs(xg)))) * (1.0 / GLA_TAU)

    zg = _dot(h, wg_ref[...])
    gq_out[0] = (zg[:, 0:GLA_DK] * (GLA_HK ** -0.5)).astype(BF16)
    gk_out[0] = zg[:, GLA_DK:2 * GLA_DK].astype(BF16)
    gv_out[0] = zg[:, 2 * GLA_DK:2 * GLA_DK + GLA_DV].astype(BF16)
    gr_out[0] = zg[:, 2 * GLA_DK + GLA_DV:]

    hy_out[0] = _dot(h, wh_ref[...])
    gate_out[0] = _dot(h, wz_ref[...])


def _rope_partner(w):
    a = MLA_ROPE // 4
    perm = np.concatenate([np.arange(a, 2 * a), np.arange(0, a), np.arange(3 * a, 4 * a), np.arange(2 * a, 3 * a)])
    sign = np.concatenate([-np.ones(a), np.ones(a), -np.ones(a), np.ones(a)]).astype(np.float32)
    return w[:, perm] * sign


def _prep_inproj_weights(w_in, mla_w_uq, mla_w_ukv, gla_w_a2, gla_b_a):
    offs = np.concatenate([[0], np.cumsum(IN_SIZES)])
    seg = [w_in[:, offs[i]:offs[i + 1]] for i in range(len(IN_SIZES))]
    w_cq, w_ckv, w_kr, w_gq, w_gk, w_gv, w_gr, w_af, w_ab, w_hy, w_gate = seg
    d = w_in.shape[0]
    z = lambda n: jnp.zeros((d, n), w_in.dtype)
    kr_tile = jnp.concatenate([z(MLA_NOPE), w_kr, z(HEAD_SLOT - MLA_NOPE - MLA_ROPE)], axis=1)
    krp_tile = jnp.concatenate([z(MLA_NOPE), _rope_partner(w_kr), z(HEAD_SLOT - MLA_NOPE - MLA_ROPE)], axis=1)
    a_tile = jnp.concatenate([w_af, w_ab, z(LANES - 2 * GLA_GATE_RANK)], axis=1)
    wa = jnp.concatenate([w_cq, w_ckv, kr_tile, krp_tile, a_tile], axis=1)
    wg = jnp.concatenate([w_gq, w_gk, w_gv, w_gr], axis=1)

    dh = MLA_NOPE + MLA_ROPE
    zq = lambda n: jnp.zeros((MLA_Q_LORA, n), w_in.dtype)
    plain, partner = [], []
    for hd in range(MLA_HEADS):
        blk = mla_w_uq[:, hd * dh:(hd + 1) * dh]
        plain += [blk, zq(HEAD_SLOT - dh)]
        partner += [zq(MLA_NOPE), _rope_partner(blk[:, MLA_NOPE:]), zq(HEAD_SLOT - dh)]
    wuq = jnp.concatenate(plain + partner, axis=1)

    zk = jnp.zeros((MLA_KV_LORA, HEAD_SLOT - MLA_NOPE), w_in.dtype)
    kcols, vcols = [], []
    for hd in range(MLA_HEADS):
        blk = mla_w_ukv[:, hd * (MLA_NOPE + MLA_V):(hd + 1) * (MLA_NOPE + MLA_V)]
        kcols += [blk[:, :MLA_NOPE], zk]
        vcols += [blk[:, MLA_NOPE:], zk]
    wukv = jnp.concatenate(kcols + vcols, axis=1)

    wa2 = jnp.zeros((LANES, 2 * GLA_DK), w_in.dtype)
    wa2 = wa2.at[0:GLA_GATE_RANK, 0:GLA_DK].set(gla_w_a2[0])
    wa2 = wa2.at[GLA_GATE_RANK:2 * GLA_GATE_RANK, GLA_DK:].set(gla_w_a2[1])
    ba = jnp.concatenate([gla_b_a[0], gla_b_a[1]])[None, :]
    bf = lambda t: t.astype(BF16)
    return bf(wa), bf(wg), bf(w_hy), bf(w_gate), bf(wuq), bf(wukv), bf(wa2), ba


def _rope_tables(seq, ctx_len):
    rows = seq // GRID_W
    row = jnp.repeat(jnp.arange(rows, dtype=F32), GRID_W)
    col = jnp.tile(jnp.arange(GRID_W, dtype=F32), rows)
    a = MLA_ROPE // 4
    inv = ROPE_BASE ** (-jnp.arange(a, dtype=F32) / a)
    ang_r = row[:, None] * inv
    ang_c = col[:, None] * inv
    cos32 = jnp.concatenate([jnp.cos(ang_r), jnp.cos(ang_r), jnp.cos(ang_c), jnp.cos(ang_c)], axis=1)
    sin32 = jnp.concatenate([jnp.sin(ang_r), jnp.sin(ang_r), jnp.sin(ang_c), jnp.sin(ang_c)], axis=1)
    pad_r = HEAD_SLOT - MLA_NOPE - MLA_ROPE
    cos = jnp.concatenate([jnp.ones((seq, MLA_NOPE), F32), cos32, jnp.zeros((seq, pad_r), F32)], axis=1)
    sin = jnp.concatenate([jnp.zeros((seq, MLA_NOPE), F32), sin32, jnp.zeros((seq, pad_r), F32)], axis=1)
    cos_c = jnp.concatenate([jnp.ones((ctx_len, MLA_NOPE + MLA_ROPE), F32), jnp.zeros((ctx_len, pad_r), F32)], axis=1)
    sin_c = jnp.zeros((ctx_len, HEAD_SLOT), F32)
    return jnp.concatenate([cos, cos_c], axis=0), jnp.concatenate([sin, sin_c], axis=0)


def _inproj(xc, mod_l, g1, weights, q_norm, kv_norm, cos, sin, n_lat_tiles):
    bsz, t, d = xc.shape
    nt = t // TM
    wa, wg, wh, wz, wuq, wukv, wa2, ba = weights
    tile = lambda w: pl.BlockSpec((1, TM, w), lambda b, i: (b, i, 0))
    row = lambda w: pl.BlockSpec((1, w), lambda b, i: (0, 0))
    mod_spec = pl.BlockSpec((1, 6, d), lambda b, i: (jnp.where(i < n_lat_tiles, b, bsz), 0, 0))
    tab = pl.BlockSpec((TM, HEAD_SLOT), lambda b, i: (i, 0))
    nq = MLA_HEADS * HEAD_SLOT
    sds = lambda w, dt: jax.ShapeDtypeStruct((bsz, t, w), dt)
    out_shape = (sds(nq, BF16), sds(nq, BF16), sds(nq, BF16), sds(GLA_DK, BF16), sds(GLA_DK, BF16),
                 sds(GLA_DV, BF16), sds(GLA_DV, F32), sds(2 * GLA_DK, F32), sds(3 * HY_WIDTH, F32),
                 sds(3 * D_MODEL, F32))
    out_specs = (tile(nq), tile(nq), tile(nq), tile(GLA_DK), tile(GLA_DK), tile(GLA_DV), tile(GLA_DV),
                 tile(2 * GLA_DK), tile(3 * HY_WIDTH), tile(3 * D_MODEL))
    return pl.pallas_call(
        _inproj_kernel,
        out_shape=out_shape,
        grid=(bsz, nt),
        in_specs=[tile(d), mod_spec, row(d), _const_spec(wa.shape), _const_spec(wg.shape), _const_spec(wh.shape),
                  _const_spec(wz.shape), row(MLA_Q_LORA), row(MLA_KV_LORA), _const_spec(wuq.shape),
                  _const_spec(wukv.shape), _const_spec(wa2.shape), row(2 * GLA_DK), tab, tab],
        out_specs=out_specs,
        compiler_params=_cparams(2, VMEM_LIMIT),
        name="inproj",
    )(xc, mod_l, g1[None, :], wa, wg, wh, wz, q_norm[None, :], kv_norm[None, :], wuq, wukv, wa2, ba, cos, sin)


ATT_TK = 1024


def _attn_kernel(q_ref, k_ref, v_ref, o_ref, m_ref, acc_ref, *, n_lat_tiles):
    qi = pl.program_id(2)
    seq = n_lat_tiles * TM
    ctx_len = k_ref.shape[1] - seq
    m_ref[...] = jnp.full(m_ref.shape, -jnp.inf, F32)
    acc_ref[...] = jnp.zeros(acc_ref.shape, F32)

    def chunk(r0, size):
        k = k_ref[0, pl.ds(r0, size), :]
        v = v_ref[0, pl.ds(r0, size), :]
        for hd in range(2):
            sl = slice(hd * HEAD_SLOT, (hd + 1) * HEAD_SLOT)
            s = _dot_nt(q_ref[0, :, sl], k[:, sl])
            m_prev = m_ref[hd]
            m_new = jnp.maximum(m_prev, jnp.max(s, axis=1, keepdims=True))
            p = jnp.exp2(s - pltpu.repeat(m_new, size // LANES, axis=1))
            acc_ref[hd] = jnp.exp2(m_prev - m_new) * acc_ref[hd] + _dot(p.astype(BF16), v[:, sl])
            m_ref[hd] = m_new

    @pl.when(qi < n_lat_tiles)
    def _():
        n_chunks = seq // ATT_TK
        for j in range(n_chunks - 1):
            chunk(j * ATT_TK, ATT_TK)
        chunk((n_chunks - 1) * ATT_TK, ATT_TK + ctx_len)

    @pl.when(qi >= n_lat_tiles)
    def _():
        chunk(seq, ctx_len)

    a0 = acc_ref[0]
    a1 = acc_ref[1]
    lane = lax.broadcasted_iota(jnp.int32, (TM, HEAD_SLOT), 1)
    o0 = a0 / pltpu.roll(a0, MLA_V, 1)
    o1 = pltpu.roll(a1, MLA_V, 1) / a1
    o_ref[0] = jnp.where(lane < MLA_V, o0, o1).astype(o_ref.dtype)


def _attention(q, k, v, n_q_tiles, n_lat_tiles):
    bsz, t, _ = q.shape
    assert (n_lat_tiles * TM) % ATT_TK == 0 and 2 * MLA_V == HEAD_SLOT
    kern = functools.partial(_attn_kernel, n_lat_tiles=n_lat_tiles)
    pair = 2 * HEAD_SLOT
    return pl.pallas_call(
        kern,
        out_shape=jax.ShapeDtypeStruct((bsz, t, MLA_OUT), BF16),
        grid=(bsz, MLA_HEADS // 2, n_q_tiles),
        in_specs=[
            pl.BlockSpec((1, TM, pair), lambda b, hp, i: (b, i, hp)),
            pl.BlockSpec((1, t, pair), lambda b, hp, i: (b, 0, hp)),
            pl.BlockSpec((1, t, pair), lambda b, hp, i: (b, 0, hp)),
        ],
        out_specs=pl.BlockSpec((1, TM, HEAD_SLOT), lambda b, hp, i: (b, i, hp)),
        scratch_shapes=[pltpu.VMEM((2, TM, LANES), F32), pltpu.VMEM((2, TM, HEAD_SLOT), F32)],
        compiler_params=_cparams(3),
        name="mla_attention",
    )(q, k, v)


GLA_LEVELS = int(math.log2(TM))


def _gla_level_matrices():
    i = np.arange(TM)[:, None]
    t = np.arange(TM)[None, :]
    fwd = [(t <= i)]
    for lv in range(GLA_LEVELS):
        m = TM >> (lv + 1)
        lo = (i // m) * m
        later = ((i // m) % 2) == 1
        q_part = later & (t >= lo) & (t <= i)
        k_part = (~later) & (t > i) & (t <= lo + m - 1)
        fwd.append(q_part | k_part)
    fwd = np.concatenate(fwd, axis=0).astype(np.float32)
    nb = 1 + GLA_LEVELS
    bwd = fwd.reshape(nb, TM, TM)[:, ::-1, ::-1].reshape(nb * TM, TM)
    return np.stack([fwd, bwd])


def _gla_kernel(q_ref, k_ref, v_ref, g_ref, mall_ref, o_ref, s_ref, a_ref):
    d = pl.program_id(1)
    step = pl.program_id(2)

    @pl.when(step == 0)
    def _():
        s_ref[...] = jnp.zeros(s_ref.shape, F32)

    q = q_ref[0].astype(F32)
    k = k_ref[0].astype(F32)
    v = v_ref[0]
    g = g_ref[0]
    g_hi = g.astype(BF16)
    g_lo = (g - g_hi.astype(F32)).astype(BF16)
    e2 = _dot(mall_ref[0], jnp.concatenate([g_hi, g_lo], axis=1))
    e_all = e2[:, :GLA_DK] + e2[:, GLA_DK:]
    g_cum = e_all[0:TM]
    g_tot = jnp.sum(g, axis=0, keepdims=True)

    row = lax.broadcasted_iota(jnp.int32, (TM, TM), 0)
    col = lax.broadcasted_iota(jnp.int32, (TM, TM), 1)
    lane_head = lax.broadcasted_iota(jnp.int32, (TM, GLA_DK), 1) // GLA_HK
    tok = lax.broadcasted_iota(jnp.int32, (TM, GLA_DK), 0)

    def stack_heads(t):
        return jnp.concatenate([jnp.where(lane_head == hd, t, 0.0) for hd in range(GLA_HEADS)], axis=0).astype(BF16)

    res = _dot_nt(stack_heads(q), k.astype(BF16))
    eye = row == col
    for hd in range(GLA_HEADS):
        a_ref[hd] = jnp.where(eye, res[hd * TM:(hd + 1) * TM], 0.0)
    for lv in range(GLA_LEVELS):
        m = TM >> (lv + 1)
        w = jnp.exp(e_all[(lv + 1) * TM:(lv + 2) * TM])
        bit = (tok // m) % 2
        q_act = bit != d
        qt = jnp.where(q_act, q * w, 0.0)
        kt = jnp.where(q_act, 0.0, k * w).astype(BF16)
        res = _dot_nt(stack_heads(qt), kt)
        if m == TM // 2:
            for hd in range(GLA_HEADS):
                a_ref[hd] += res[hd * TM:(hd + 1) * TM]
        else:
            same = (row // (2 * m)) == (col // (2 * m))
            for hd in range(GLA_HEADS):
                a_ref[hd] += jnp.where(same, res[hd * TM:(hd + 1) * TM], 0.0)

    s_old = s_ref[...]
    s_bf = s_old.astype(BF16)
    qg = q * jnp.exp(g_cum)
    for hd in range(GLA_HEADS):
        vh = v[:, hd * GLA_HV:(hd + 1) * GLA_HV]
        o_intra = _dot(a_ref[hd].astype(BF16), vh)
        o_inter = _dot(jnp.where(lane_head == hd, qg, 0.0).astype(BF16), s_bf)
        o_ref[0, 0, :, hd * GLA_HV:(hd + 1) * GLA_HV] = o_intra + o_inter

    kdec_t = (k * jnp.exp(g_tot - g_cum)).T.astype(BF16)
    upd = _dot(kdec_t, v)
    row_head = lax.broadcasted_iota(jnp.int32, (GLA_DK, GLA_HV), 0) // GLA_HK
    new = jnp.zeros((GLA_DK, GLA_HV), F32)
    for hd in range(GLA_HEADS):
        new = new + jnp.where(row_head == hd, upd[:, hd * GLA_HV:(hd + 1) * GLA_HV], 0.0)
    dec_col = jnp.sum(jnp.where(eye, jnp.broadcast_to(jnp.exp(g_tot), (TM, GLA_DK)), 0.0), axis=1, keepdims=True)
    s_ref[...] = dec_col * s_old + new


def _gla(gq, gk, gv, glog, n_lat_tiles):
    assert GLA_DK == TM
    bsz, t, _ = gq.shape
    nt = t // TM
    mall = jnp.asarray(_gla_level_matrices(), dtype=BF16)

    def tile_idx(d, s):
        return jnp.where(s == 0, n_lat_tiles, jnp.where(d == 0, s - 1, n_lat_tiles - s))

    tok = lambda w: pl.BlockSpec((1, TM, w), lambda b, d, s: (b, tile_idx(d, s), 0))
    return pl.pallas_call(
        _gla_kernel,
        out_shape=jax.ShapeDtypeStruct((bsz, 2, t, GLA_DV), F32),
        grid=(bsz, 2, nt),
        in_specs=[tok(GLA_DK), tok(GLA_DK), tok(GLA_DV),
                  pl.BlockSpec((1, TM, GLA_DK), lambda b, d, s: (b, tile_idx(d, s), d)),
                  pl.BlockSpec((1,) + mall.shape[1:], lambda b, d, s: (d, 0, 0))],
        out_specs=pl.BlockSpec((1, 1, TM, GLA_DV), lambda b, d, s: (b, d, tile_idx(d, s), 0)),
        scratch_shapes=[pltpu.VMEM((GLA_DK, GLA_HV), F32), pltpu.VMEM((GLA_HEADS, TM, TM), F32)],
        compiler_params=_cparams(3),
        name="gla_scan",
    )(gq, gk, gv, glog, mall)


HY_N1 = 64
HY_N2 = 128
HY_SLAB = SUBLANES
HY_CB = 128


def _hyena_dft_constants(seq):
    n = 2 * seq
    assert n == HY_N1 * HY_N2
    nh = HY_N1 // 2
    kh = HY_N2 // 2
    eye = np.eye(HY_SLAB)
    k1 = np.arange(HY_N1)
    th = 2 * np.pi * np.outer(k1 + 0.5, np.arange(nh)) / HY_N1
    fwd1 = np.concatenate([np.kron(np.cos(th), eye), np.kron(-np.sin(th), eye)], axis=0)
    inv1 = (2.0 / n) * np.concatenate([np.kron(np.cos(th).T, eye), np.kron(-np.sin(th).T, eye)], axis=1)
    nlo = np.arange(HY_N2)
    k2 = np.arange(kh)
    ph = 2 * np.pi * (k2[None, :, None] * nlo[None, None, :] / HY_N2
                      + (k1[:, None, None] + 0.5) * nlo[None, None, :] / n)
    c, s = np.cos(ph), np.sin(ph)
    fwd2 = np.concatenate([np.concatenate([c, s], axis=2), np.concatenate([-s, c], axis=2)], axis=1)
    ct, st = c.transpose(0, 2, 1), s.transpose(0, 2, 1)
    inv2 = np.concatenate([np.concatenate([ct, -st], axis=2), np.concatenate([st, ct], axis=2)], axis=1)
    return fwd1, fwd2, inv2, inv1


def _hyena_ctx_dft_constants(ctx_len):
    n = 2 * ctx_len
    th = 2 * np.pi * np.outer(np.arange(ctx_len) + 0.5, np.arange(ctx_len)) / n
    fwd = np.concatenate([np.cos(th), -np.sin(th)], axis=0)
    inv = (2.0 / n) * np.concatenate([np.cos(th).T, -np.sin(th).T], axis=1)
    return fwd, inv


def _hyena_features(length):
    pos = jnp.arange(length, dtype=F32)
    t = pos / max(length - 1, 1)
    f = jnp.linspace(1e-4, HY_BANDS - 1, HY_BANDS, dtype=F32)
    ang = (2.0 * math.pi / length) * pos[:, None] * f
    feat = jnp.concatenate([t[:, None], jnp.cos(ang), jnp.sin(ang)], axis=-1)
    return jnp.pad(feat, ((0, 0), (0, LANES - HY_POS_DIM)))


def _hy_filter_kernel(feat_ref, w1_ref, b1_ref, w2_ref, b2_ref, w3_ref, b3_ref, absd_ref, h_ref, s_ref):
    i = pl.program_id(0)
    feat = feat_ref[...]
    hp = lambda a, b: jnp.dot(a, b, preferred_element_type=F32, precision=HIGHEST)
    hdn = jnp.sin(hp(feat, w1_ref[...]) + b1_ref[...])
    hdn = jnp.sin(hp(hdn, w2_ref[...]) + b2_ref[...])
    h = hp(hdn, w3_ref[...]) + b3_ref[...]
    window = jnp.exp(-feat[:, 0:1] * absd_ref[...])
    h = h * jnp.concatenate([window] * (2 * HY_ORDER), axis=1)
    h_ref[...] = h

    @pl.when(i == 0)
    def _():
        s_ref[...] = jnp.zeros(s_ref.shape, F32)

    s_ref[...] += jnp.sum(jnp.abs(h), axis=0, keepdims=True)


def _hyena_filters_raw(length, filt_w):
    w1, b1, w2, b2, w3, b3 = filt_w
    nf = 2 * HY_ORDER * HY_WIDTH
    tr = min(length, 512)
    deltas = np.linspace(math.log(HY_DECAY_TARGET) / HY_FAST_DECAY, math.log(HY_DECAY_TARGET) / HY_SLOW_DECAY,
                         HY_WIDTH, dtype=np.float32)
    absd = jnp.asarray(np.abs(deltas))[None, :]
    w1p = jnp.pad(w1, ((0, LANES - HY_POS_DIM), (0, 0)))
    full = lambda shp: pl.BlockSpec(shp, lambda i: (0,) * len(shp))
    return pl.pallas_call(
        _hy_filter_kernel,
        out_shape=(jax.ShapeDtypeStruct((length, nf), F32), jax.ShapeDtypeStruct((1, nf), F32)),
        grid=(length // tr,),
        in_specs=[pl.BlockSpec((tr, LANES), lambda i: (i, 0)), full((LANES, HY_FILTER_HIDDEN)),
                  full((1, HY_FILTER_HIDDEN)), full((HY_FILTER_HIDDEN, HY_FILTER_HIDDEN)), full((1, HY_FILTER_HIDDEN)),
                  full((HY_FILTER_HIDDEN, nf)), full((1, nf)), full((1, HY_WIDTH))],
        out_specs=(pl.BlockSpec((tr, nf), lambda i: (i, 0)), full((1, nf))),
        compiler_params=_cparams(1),
        name="hyena_filter_mlp",
    )(_hyena_features(length), w1p, b1[None, :], w2, b2[None, :], w3, b3[None, :], absd)


def _split_bf16(m):
    hi = m.astype(BF16)
    return jnp.stack([hi, (m - hi.astype(F32)).astype(BF16)])


def _dot_split(m_hi, m_lo, x):
    x_hi = x.astype(BF16)
    x_lo = (x - x_hi.astype(F32)).astype(BF16)
    n = x.shape[1]
    r = _dot(m_hi, jnp.concatenate([x_hi, x_lo], axis=1))
    return r[:, :n] + r[:, n:] + _dot(m_lo, x_hi)


def _odft_stage1(src_at, mm, s_re, s_im):
    half = HY_N1 * HY_SLAB

    def body(j, carry):
        r0 = pl.multiple_of(j * HY_SLAB, HY_SLAB)
        slab = src_at(r0)
        cb = slab.shape[-1]
        res = mm(slab.reshape(-1, cb))
        s_re[:, pl.ds(r0, HY_SLAB), :] = res[:half].reshape(HY_N1, HY_SLAB, cb)
        s_im[:, pl.ds(r0, HY_SLAB), :] = res[half:].reshape(HY_N1, HY_SLAB, cb)
        return carry

    lax.fori_loop(0, HY_N2 // HY_SLAB, body, 0, unroll=2)


def _hy_spectrum_kernel(hf_ref, hb_ref, sf_ref, sb_ref, fwd1_ref, fwd2_ref, o_ref, s_re, s_im):
    kh = HY_N2 // 2
    mm1 = lambda x: _dot_split(fwd1_ref[0], fwd1_ref[1], x)

    def middle(sign):
        def body(k1, carry):
            a = jnp.concatenate([s_re[k1], s_im[k1]], axis=0)
            x = _dot_split(fwd2_ref[0, k1], fwd2_ref[1, k1], a)
            if sign is None:
                o_ref[0, k1] = x[:kh]
                o_ref[1, k1] = x[kh:]
            else:
                inv_norm = 1.0 / (sf_ref[...] + sb_ref[...])
                o_ref[0, k1] = (o_ref[0, k1] + x[:kh]) * inv_norm
                o_ref[1, k1] = (o_ref[1, k1] - x[kh:]) * inv_norm
            return carry
        lax.fori_loop(0, HY_N1, body, 0, unroll=2)

    _odft_stage1(lambda r0: hf_ref[:, pl.ds(r0, HY_SLAB), :], mm1, s_re, s_im)
    middle(None)

    def bwd_slab(r0):
        slab = hb_ref[:, pl.ds(r0, HY_SLAB), :]
        nhi = lax.broadcasted_iota(jnp.int32, slab.shape, 0)
        r = lax.broadcasted_iota(jnp.int32, slab.shape, 1)
        return jnp.where((nhi == 0) & (r + r0 == 0), 0.0, slab)

    _odft_stage1(bwd_slab, mm1, s_re, s_im)
    middle(-1)


def _hyena_filter_spectrum(seq, w1, b1, w2, b2, w3, b3):
    h_raw, s = _hyena_filters_raw(seq, (w1, b1, w2, b2, w3, b3))
    nh = HY_N1 // 2
    nc = HY_ORDER * HY_WIDTH
    h3 = h_raw.reshape(nh, HY_N2, 2 * nc)
    fwd1, fwd2, _, _ = _hyena_dft_constants(seq)
    fwd1 = _split_bf16(jnp.asarray(fwd1, F32))
    fwd2 = _split_bf16(jnp.asarray(fwd2, F32))
    ncb = nc // HY_CB
    return pl.pallas_call(
        _hy_spectrum_kernel,
        out_shape=jax.ShapeDtypeStruct((2, HY_N1, HY_N2 // 2, nc), F32),
        grid=(ncb,),
        in_specs=[pl.BlockSpec((nh, HY_N2, HY_CB), lambda c: (0, 0, c)),
                  pl.BlockSpec((nh, HY_N2, HY_CB), lambda c: (0, 0, ncb + c)),
                  pl.BlockSpec((1, HY_CB), lambda c: (0, c)),
                  pl.BlockSpec((1, HY_CB), lambda c: (0, ncb + c)),
                  _const_spec(fwd1.shape), _const_spec(fwd2.shape)],
        out_specs=pl.BlockSpec((2, HY_N1, HY_N2 // 2, HY_CB), lambda c: (0, 0, 0, c)),
        scratch_shapes=[pltpu.VMEM((HY_N1, HY_N2, HY_CB), F32), pltpu.VMEM((HY_N1, HY_N2, HY_CB), F32)],
        compiler_params=_cparams(1, VMEM_LIMIT),
        name="hyena_filter_spectrum",
    )(h3, h3, s, s, fwd1, fwd2)


def _short_conv_chunk(ref, c, n_chunks, w_ref, b_ref):
    per = TM // HY_N2
    cur = ref[0, pl.ds(per * c, per)]
    cb = cur.shape[-1]
    cur = cur.reshape(TM, cb)
    prev = ref[0, jnp.maximum(per * c - 1, 0), pl.ds(HY_N2 - SUBLANES, SUBLANES), :][SUBLANES - 1:SUBLANES]
    nxt = ref[0, jnp.minimum(per * c + per, per * n_chunks - 1), pl.ds(0, SUBLANES), :][0:1]
    prev = jnp.where(c > 0, prev, 0.0)
    nxt = jnp.where(c < n_chunks - 1, nxt, 0.0)
    rowi = lax.broadcasted_iota(jnp.int32, (TM, cb), 0)
    dn = jnp.where(rowi == 0, prev, pltpu.roll(cur, 1, 0))
    up = jnp.where(rowi == TM - 1, nxt, pltpu.roll(cur, TM - 1, 0))
    return b_ref[...] + w_ref[0:1, :] * dn + w_ref[1:2, :] * cur + w_ref[2:3, :] * up


def _hy_conv_kernel(*refs, conv_y):
    if conv_y:
        (y_ref, g_ref, h_ref, bias_ref, wy_ref, by_ref, wg_ref, bg_ref, fwd1_ref, fwd2_ref, inv2_ref, inv1_ref,
         o_ref, s_re, s_im, gs_ref, us_ref) = refs
    else:
        (y_ref, g_ref, h_ref, bias_ref, wg_ref, bg_ref, fwd1_ref, fwd2_ref, inv2_ref, inv1_ref,
         o_ref, s_re, s_im, gs_ref) = refs
    nh = HY_N1 // 2
    per = TM // HY_N2
    n_chunks = nh // per
    cb = o_ref.shape[-1]
    kh = HY_N2 // 2

    def pre(c, carry):
        gs_ref[pl.ds(per * c, per)] = _short_conv_chunk(g_ref, c, n_chunks, wg_ref, bg_ref).reshape(per, HY_N2, cb)
        if conv_y:
            us_ref[pl.ds(per * c, per)] = _short_conv_chunk(y_ref, c, n_chunks, wy_ref, by_ref).reshape(per, HY_N2, cb)
        return carry

    lax.fori_loop(0, n_chunks, pre, 0)
    if conv_y:
        u_at = lambda r0: us_ref[:, pl.ds(r0, HY_SLAB), :]
    else:
        u_at = lambda r0: y_ref[0, :, pl.ds(r0, HY_SLAB), :]

    _odft_stage1(u_at, lambda x: _dot(fwd1_ref[...], x.astype(BF16)), s_re, s_im)

    def middle(k1, carry):
        a = jnp.concatenate([s_re[k1], s_im[k1]], axis=0).astype(BF16)
        x = _dot(fwd2_ref[k1], a)
        xr, xi = x[:kh], x[kh:]
        hr, hi = h_ref[0, k1], h_ref[1, k1]
        y = jnp.concatenate([xr * hr - xi * hi, xr * hi + xi * hr], axis=0).astype(BF16)
        bm = _dot(inv2_ref[k1], y)
        s_re[k1] = bm[:HY_N2]
        s_im[k1] = bm[HY_N2:]
        return carry

    lax.fori_loop(0, HY_N1, middle, 0, unroll=4)

    def post(j, carry):
        r0 = pl.multiple_of(j * HY_SLAB, HY_SLAB)
        slab = jnp.concatenate([s_re[:, pl.ds(r0, HY_SLAB), :].reshape(HY_N1 * HY_SLAB, cb),
                                s_im[:, pl.ds(r0, HY_SLAB), :].reshape(HY_N1 * HY_SLAB, cb)], axis=0).astype(BF16)
        conv = _dot(inv1_ref[...], slab).reshape(nh, HY_SLAB, cb)
        u = u_at(r0)
        o_ref[0, :, pl.ds(r0, HY_SLAB), :] = gs_ref[:, pl.ds(r0, HY_SLAB), :] * (conv + bias_ref[...] * u)
        return carry

    lax.fori_loop(0, HY_N2 // HY_SLAB, post, 0, unroll=2)


def _hyena_order(y4, y_col0, z4, gate_col0, hspec, order, hy_bias, short_w, short_b, consts, conv_y):
    bsz = z4.shape[0]
    nh = HY_N1 // 2
    ncb = HY_WIDTH // HY_CB
    fwd1, fwd2, inv2, inv1 = consts
    blk4 = lambda off: pl.BlockSpec((1, nh, HY_N2, HY_CB), lambda c, b: (b, 0, 0, off + c))
    rowspec = lambda rows, off: pl.BlockSpec((rows, HY_CB), lambda c, b: (0, off + c))
    in_specs = [blk4(y_col0), blk4(gate_col0),
                pl.BlockSpec((2, HY_N1, HY_N2 // 2, HY_CB), lambda c, b: (0, 0, 0, order * ncb + c),
                             pipeline_mode=pl.Buffered(1)),
                rowspec(1, 0)]
    args = [y4, z4, hspec, hy_bias.reshape(1, -1)]
    if conv_y:
        in_specs += [rowspec(HY_SHORT, y_col0), rowspec(1, y_col0)]
        args += [short_w, short_b[None, :]]
    in_specs += [rowspec(HY_SHORT, gate_col0), rowspec(1, gate_col0)]
    args += [short_w, short_b[None, :]]
    in_specs += [_const_spec(fwd1.shape), _const_spec(fwd2.shape), _const_spec(inv2.shape), _const_spec(inv1.shape)]
    args += [fwd1, fwd2, inv2, inv1]
    scratch = [pltpu.VMEM((HY_N1, HY_N2, HY_CB), F32), pltpu.VMEM((HY_N1, HY_N2, HY_CB), F32),
               pltpu.VMEM((nh, HY_N2, HY_CB), F32)]
    if conv_y:
        scratch.append(pltpu.VMEM((nh, HY_N2, HY_CB), F32))
    return pl.pallas_call(
        functools.partial(_hy_conv_kernel, conv_y=conv_y),
        out_shape=jax.ShapeDtypeStruct((bsz, nh, HY_N2, HY_WIDTH), F32),
        grid=(ncb, bsz),
        in_specs=in_specs,
        out_specs=pl.BlockSpec((1, nh, HY_N2, HY_CB), lambda c, b: (b, 0, 0, c)),
        scratch_shapes=scratch,
        compiler_params=_cparams(2, VMEM_LIMIT),
        name="hyena_conv%d" % order,
    )(*args)


def _hyena_latent(z_hy, hspec, short_w, short_b, hy_bias):
    bsz, t, _ = z_hy.shape
    seq = HY_N1 * HY_N2 // 2
    consts = tuple(jnp.asarray(m, BF16) for m in _hyena_dft_constants(seq))
    z4 = z_hy.reshape(bsz, t // HY_N2, HY_N2, 3 * HY_WIDTH)
    ncb = HY_WIDTH // HY_CB
    y1 = _hyena_order(z4, 2 * ncb, z4, 0, hspec, 0, hy_bias[0], short_w, short_b, consts, True)
    y2 = _hyena_order(y1, 0, z4, ncb, hspec, 1, hy_bias[1], short_w, short_b, consts, False)
    return y2.reshape(bsz, seq, HY_WIDTH)


def _hy_ctx_spectrum_kernel(h_ref, s_ref, fwd_ref, o_ref):
    lc = h_ref.shape[0]
    nc = HY_ORDER * HY_WIDTH
    hp = lambda a, b: jnp.dot(a, b, preferred_element_type=F32, precision=HIGHEST)
    h = h_ref[...]
    rowi = lax.broadcasted_iota(jnp.int32, (lc, nc), 0)
    xf = hp(fwd_ref[...], h[:, :nc])
    xb = hp(fwd_ref[...], jnp.where(rowi == 0, 0.0, h[:, nc:]))
    inv_norm = 1.0 / (s_ref[:, :nc] + s_ref[:, nc:])
    o_ref[0] = (xf[:lc] + xb[:lc]) * inv_norm
    o_ref[1] = (xf[lc:] - xb[lc:]) * inv_norm


def _hy_ctx_conv_kernel(x1_ref, x2_ref, v_ref, h_ref, bias_ref, w_ref, b_ref, fwd_ref, inv_ref, o_ref):
    lc = o_ref.shape[1]
    rowi = lax.broadcasted_iota(jnp.int32, (lc, HY_WIDTH), 0)

    def short(ref, part):
        cur = ref[0].reshape(lc, HY_WIDTH)
        sl = slice(part * HY_WIDTH, (part + 1) * HY_WIDTH)
        dn = jnp.where(rowi == 0, 0.0, pltpu.roll(cur, 1, 0))
        up = jnp.where(rowi == lc - 1, 0.0, pltpu.roll(cur, lc - 1, 0))
        return b_ref[:, sl] + w_ref[0:1, sl] * dn + w_ref[1:2, sl] * cur + w_ref[2:3, sl] * up

    y = short(v_ref, 2)
    for order, gref in enumerate((x1_ref, x2_ref)):
        sl = slice(order * HY_WIDTH, (order + 1) * HY_WIDTH)
        x = _dot(fwd_ref[...], y.astype(BF16))
        xr, xi = x[:lc], x[lc:]
        hr, hi = h_ref[0, :, sl], h_ref[1, :, sl]
        prod = jnp.concatenate([xr * hr - xi * hi, xr * hi + xi * hr], axis=0).astype(BF16)
        conv = _dot(inv_ref[...], prod)
        y = short(gref, order) * (conv + bias_ref[order:order + 1, :] * y)
    o_ref[0] = y


def _hyena_ctx(z_hy, filt_w, short_w, short_b, hy_bias):
    bsz, t, _ = z_hy.shape
    seq = HY_N1 * HY_N2 // 2
    lc = t - seq
    per = lc // HY_N2
    h_raw, s = _hyena_filters_raw(lc, filt_w)
    fwd, inv = _hyena_ctx_dft_constants(lc)
    nc = HY_ORDER * HY_WIDTH
    full = lambda shp: pl.BlockSpec(shp, lambda *_: (0,) * len(shp))
    hspec = pl.pallas_call(
        _hy_ctx_spectrum_kernel,
        out_shape=jax.ShapeDtypeStruct((2, lc, nc), F32),
        grid=(1,),
        in_specs=[full(h_raw.shape), full(s.shape), full(fwd.shape)],
        out_specs=full((2, lc, nc)),
        compiler_params=_cparams(1),
        name="hyena_ctx_spectrum",
    )(h_raw, s, jnp.asarray(fwd, F32))
    z4 = z_hy.reshape(bsz, t // HY_N2, HY_N2, 3 * HY_WIDTH)
    blk = lambda part: pl.BlockSpec((1, per, HY_N2, HY_WIDTH), lambda b: (b, seq // lc, 0, part))
    return pl.pallas_call(
        _hy_ctx_conv_kernel,
        out_shape=jax.ShapeDtypeStruct((bsz, lc, HY_WIDTH), F32),
        grid=(bsz,),
        in_specs=[blk(0), blk(1), blk(2), full((2, lc, nc)), full((HY_ORDER, HY_WIDTH)),
                  full((HY_SHORT, 3 * HY_WIDTH)), full((1, 3 * HY_WIDTH)), full(fwd.shape), full(inv.shape)],
        out_specs=pl.BlockSpec((1, lc, HY_WIDTH), lambda b: (b, 0, 0)),
        compiler_params=_cparams(1),
        name="hyena_ctx_conv",
    )(z4, z4, z4, hspec, hy_bias, short_w, short_b[None, :], jnp.asarray(fwd, BF16), jnp.asarray(inv, BF16))


def _merge_kernel(x_ref, mod_ref, ymla_ref, of_ref, ob_ref, gr_ref, yhy_ref, zg_ref, on_ref,
                  wm_ref, wgl_ref, wh_ref, wo_ref, o_ref):
    o = of_ref[0, 0] + ob_ref[0, 0]
    gr = gr_ref[0]
    silu = gr * _sigmoid(gr)
    parts = []
    for hd in range(GLA_HEADS):
        sl = slice(hd * GLA_HV, (hd + 1) * GLA_HV)
        parts.append((_rms(o[:, sl]) * on_ref[...] * silu[:, sl]).astype(BF16))
    y_gla = jnp.concatenate(parts, axis=1)
    zg = zg_ref[0]
    d = x_ref.shape[-1]
    m = _sigmoid(zg[:, 0:d]) * _dot(ymla_ref[0], wm_ref[...])
    m = m + _sigmoid(zg[:, d:2 * d]) * _dot(y_gla, wgl_ref[...])
    m = m + _sigmoid(zg[:, 2 * d:3 * d]) * _dot(yhy_ref[0].astype(BF16), wh_ref[...])
    out = _dot(m.astype(BF16), wo_ref[...])
    o_ref[0] = x_ref[0] + mod_ref[0, 2:3, :] * out


def _mod_spec(d, n_lat_tiles, bsz):
    return pl.BlockSpec((1, 6, d), lambda b, i: (jnp.where(i < n_lat_tiles, b, bsz), 0, 0))


def _merge(xc, mod_l, y_mla, o_gla, gr, y_hy, gate, out_norm, w_o_mla, w_o_gla, w_o_hy, w_out, n_tiles, n_lat_tiles):
    bsz, t, d = xc.shape
    tile = lambda w: pl.BlockSpec((1, TM, w), lambda b, i: (b, i, 0))
    dirspec = lambda dr: pl.BlockSpec((1, 1, TM, GLA_DV), lambda b, i: (b, dr, i, 0))
    bf = lambda w: w.astype(BF16)
    return pl.pallas_call(
        _merge_kernel,
        out_shape=jax.ShapeDtypeStruct((bsz, n_tiles * TM, d), F32),
        grid=(bsz, n_tiles),
        in_specs=[tile(d), _mod_spec(d, n_lat_tiles, bsz), tile(MLA_OUT), dirspec(0), dirspec(1), tile(GLA_DV),
                  tile(HY_WIDTH), tile(3 * d), pl.BlockSpec((1, GLA_HV), lambda b, i: (0, 0)),
                  _const_spec(w_o_mla.shape), _const_spec(w_o_gla.shape), _const_spec(w_o_hy.shape),
                  _const_spec(w_out.shape)],
        out_specs=tile(d),
        compiler_params=_cparams(2, VMEM_LIMIT),
        name="merge",
    )(xc, mod_l, y_mla, o_gla, o_gla, gr, y_hy, gate, out_norm[None, :], bf(w_o_mla), bf(w_o_gla), bf(w_o_hy),
      bf(w_out))


FF_CHUNK = 1024


def _mlp_kernel(x_ref, mod_ref, g2_ref, w1_ref, w2_ref, fg_ref, o_ref, *, final):
    x = x_ref[0]
    h = (_rms(x) * g2_ref[...] * (1.0 + mod_ref[0, 4:5, :]) + mod_ref[0, 3:4, :]).astype(BF16)
    acc = jnp.zeros(x.shape, F32)
    for j in range(w1_ref.shape[1] // FF_CHUNK):
        a = jnp.maximum(_dot(h, w1_ref[:, j * FF_CHUNK:(j + 1) * FF_CHUNK]), 0.0)
        acc = acc + _dot((a * a).astype(BF16), w2_ref[j * FF_CHUNK:(j + 1) * FF_CHUNK, :])
    xn = x + mod_ref[0, 5:6, :] * acc
    if final:
        xn = _rms(xn) * fg_ref[...]
    o_ref[0] = xn


def _mlp(xc, mod_l, g2, w1, w2, final_g, n_tiles, n_lat_tiles, final):
    bsz, t, d = xc.shape
    tile = pl.BlockSpec((1, TM, d), lambda b, i: (b, i, 0))
    row = pl.BlockSpec((1, d), lambda b, i: (0, 0))
    t_out = n_tiles * TM if final else t
    return pl.pallas_call(
        functools.partial(_mlp_kernel, final=final),
        out_shape=jax.ShapeDtypeStruct((bsz, t_out, d), F32),
        grid=(bsz, n_tiles),
        in_specs=[tile, _mod_spec(d, n_lat_tiles, bsz), row, _const_spec(w1.shape), _const_spec(w2.shape), row],
        out_specs=tile,
        compiler_params=_cparams(2, VMEM_LIMIT),
        name="mlp",
    )(xc, mod_l, g2[None, :], w1.astype(BF16), w2.astype(BF16), final_g[None, :])


def kernel(x, c, ctx, c_ctx, ada_w, ada_b, norm1_g, norm2_g, w_in, mla_q_norm, mla_w_uq, mla_kv_norm, mla_w_ukv, gla_w_a2, gla_b_a, gla_out_norm, hy_short_w, hy_short_b, hy_f_w1, hy_f_b1, hy_f_w2, hy_f_b2, hy_f_w3, hy_f_b3, hy_bias, w_o_mla, w_o_gla, w_o_hy, w_out, ff_w1, ff_w2, final_norm_g):
    bsz, seq, d = x.shape
    ctx_len = ctx.shape[1]
    n_lat = seq // TM
    n_all = (seq + ctx_len) // TM
    xc = jnp.concatenate([x, ctx], axis=1)
    cc = jnp.zeros((16, d), F32).at[:bsz].set(c).at[bsz].set(c_ctx)
    mod = _modulation(cc, ada_w, ada_b).reshape(DEPTH, 16, 6, d)
    cos, sin = _rope_tables(seq, ctx_len)
    for l in range(DEPTH):
        last = l == DEPTH - 1
        n_tiles = n_lat if last else n_all
        weights = _prep_inproj_weights(w_in[l], mla_w_uq[l], mla_w_ukv[l], gla_w_a2[l], gla_b_a[l])
        q, k, v, gq, gk, gv, gr, glog, z_hy, z_gate = _inproj(xc, mod[l], norm1_g[l], weights, mla_q_norm[l],
                                                             mla_kv_norm[l], cos, sin, n_lat)
        y_mla = _attention(q, k, v, n_tiles, n_lat)
        o_gla = _gla(gq, gk, gv, glog, n_lat)
        filt_w = (hy_f_w1[l], hy_f_b1[l], hy_f_w2[l], hy_f_b2[l], hy_f_w3[l], hy_f_b3[l])
        hspec = _hyena_filter_spectrum(seq, *filt_w)
        y_hy = _hyena_latent(z_hy, hspec, hy_short_w[l], hy_short_b[l], hy_bias[l])
        if not last:
            y_hy_c = _hyena_ctx(z_hy, filt_w, hy_short_w[l], hy_short_b[l], hy_bias[l])
            y_hy = jnp.concatenate([y_hy, y_hy_c], axis=1)
        xc = _merge(xc, mod[l], y_mla, o_gla, gr, y_hy, z_gate, gla_out_norm[l], w_o_mla[l], w_o_gla[l], w_o_hy[l],
                    w_out[l], n_tiles, n_lat)
        xc = _mlp(xc, mod[l], norm2_g[l], ff_w1[l], ff_w2[l], final_norm_g, n_tiles, n_lat, last)
    return xc
```

```python
import functools
import math

import numpy as np
import jax
import jax.numpy as jnp
from jax import lax
from jax.experimental import pallas as pl
from jax.experimental.pallas import tpu as pltpu

F32 = jnp.float32
BF16 = jnp.bfloat16
HIGHEST = lax.Precision.HIGHEST
LOG2E = 1.4426950408889634

D_MODEL = 1024
DEPTH = 2
GRID_W = 64
EPS = 1e-6
MLA_HEADS = 8
MLA_NOPE = 64
MLA_ROPE = 32
MLA_V = 64
MLA_Q_LORA = 256
MLA_KV_LORA = 128
MLA_SCALE = (MLA_NOPE + MLA_ROPE) ** -0.5
ROPE_BASE = 10000.0
GLA_HEADS = 4
GLA_DK = 256
GLA_DV = 512
GLA_HK = GLA_DK // GLA_HEADS
GLA_HV = GLA_DV // GLA_HEADS
GLA_GATE_RANK = 16
GLA_TAU = 16.0
HY_WIDTH = 512
HY_ORDER = 2
HY_SHORT = 3
HY_BANDS = 16
HY_POS_DIM = 1 + 2 * HY_BANDS
HY_FILTER_HIDDEN = 64
HY_FAST_DECAY = 0.3
HY_SLOW_DECAY = 1.5
HY_DECAY_TARGET = 1e-2
D_FF = 4 * D_MODEL
MLA_OUT = MLA_HEADS * MLA_V
IN_SIZES = (MLA_Q_LORA, MLA_KV_LORA, MLA_ROPE, GLA_DK, GLA_DK, GLA_DV, GLA_DV, GLA_GATE_RANK, GLA_GATE_RANK,
            (HY_ORDER + 1) * HY_WIDTH, 3 * D_MODEL)

LANES = 128
SUBLANES = 8
TM = 256
HEAD_SLOT = 128
VMEM_LIMIT = 56 * 1024 * 1024


def _cparams(n_axes, vmem=None):
    return pltpu.CompilerParams(dimension_semantics=("arbitrary",) * n_axes, vmem_limit_bytes=vmem)


def _const_spec(shape):
    nd = len(shape)
    return pl.BlockSpec(shape, lambda *_: (0,) * nd, pipeline_mode=pl.Buffered(1))


def _rms(x):
    return x * lax.rsqrt(jnp.mean(x * x, axis=-1, keepdims=True) + EPS)


def _sigmoid(x):
    return 1.0 / (1.0 + jnp.exp(-x))


def _dot(a, b):
    return jnp.dot(a, b, preferred_element_type=F32)


def _dot_nt(a, b):
    return lax.dot_general(a, b, (((1,), (1,)), ((), ())), preferred_element_type=F32)


def _mod_kernel(cc_ref, w_ref, b_ref, o_ref):
    s = cc_ref[...]
    s = s * _sigmoid(s)
    o_ref[0] = jnp.dot(s, w_ref[0], preferred_element_type=F32, precision=HIGHEST) + b_ref[0]


def _modulation(cc, ada_w, ada_b):
    tn = 1536
    n6 = ada_w.shape[-1]
    return pl.pallas_call(
        _mod_kernel,
        out_shape=jax.ShapeDtypeStruct((DEPTH, 16, n6), F32),
        grid=(DEPTH, n6 // tn),
        in_specs=[
            pl.BlockSpec((16, D_MODEL), lambda l, j: (0, 0)),
            pl.BlockSpec((1, D_MODEL, tn), lambda l, j: (l, 0, j)),
            pl.BlockSpec((1, 1, tn), lambda l, j: (l, 0, j)),
        ],
        out_specs=pl.BlockSpec((1, 16, tn), lambda l, j: (l, 0, j)),
        compiler_params=_cparams(2),
        name="modulation",
    )(cc, ada_w, ada_b.reshape(DEPTH, 1, n6))


W_A = 768
W_G = 2 * GLA_DK + 2 * GLA_DV


def _inproj_kernel(x_ref, mod_ref, g1_ref, wa_ref, wg_ref, wh_ref, wz_ref, qn_ref, kvn_ref, wuq_ref, wukv_ref,
                   wa2_ref, ba_ref, cos_ref, sin_ref,
                   q_out, k_out, v_out, gq_out, gk_out, gv_out, gr_out, glog_out, hy_out, gate_out):
    x = x_ref[0]
    shift = mod_ref[0, 0:1, :]
    scale = mod_ref[0, 1:2, :]
    h = (_rms(x) * g1_ref[...] * (1.0 + scale) + shift).astype(BF16)

    za = _dot(h, wa_ref[...])
    cos = cos_ref[...]
    sin = sin_ref[...]

    cqn = (_rms(za[:, 0:256]) * qn_ref[...]).astype(BF16)
    qab = _dot(cqn, wuq_ref[...])
    nq = MLA_HEADS * HEAD_SLOT
    for hd in range(MLA_HEADS):
        sl = slice(hd * HEAD_SLOT, (hd + 1) * HEAD_SLOT)
        qa = qab[:, hd * HEAD_SLOT:(hd + 1) * HEAD_SLOT]
        qb = qab[:, nq + hd * HEAD_SLOT:nq + (hd + 1) * HEAD_SLOT]
        q_out[0, :, sl] = ((qa * cos + qb * sin) * (MLA_SCALE * LOG2E)).astype(BF16)

    ckvn = (_rms(za[:, 256:384]) * kvn_ref[...]).astype(BF16)
    kv = _dot(ckvn, wukv_ref[...])
    krot = za[:, 384:512] * cos + za[:, 512:640] * sin
    for hd in range(MLA_HEADS):
        sl = slice(hd * HEAD_SLOT, (hd + 1) * HEAD_SLOT)
        k_out[0, :, sl] = (kv[:, sl] + krot).astype(BF16)
    ones_hi = (lax.broadcasted_iota(jnp.int32, (1, HEAD_SLOT), 1) >= MLA_V).astype(F32)
    for hd in range(MLA_HEADS):
        sl = slice(hd * HEAD_SLOT, (hd + 1) * HEAD_SLOT)
        v_out[0, :, sl] = (kv[:, nq + hd * HEAD_SLOT:nq + (hd + 1) * HEAD_SLOT] + ones_hi).astype(BF16)

    xg = _dot(za[:, 640:768].astype(BF16), wa2_ref[...]) + ba_ref[...]
    glog_out[0] = (jnp.minimum(xg, 0.0) - jnp.log(1.0 + jnp.exp(-jnp.abs(xg)))) * (1.0 / GLA_TAU)

    zg = _dot(h, wg_ref[...])
    gq_out[0] = (zg[:, 0:GLA_DK] * (GLA_HK ** -0.5)).astype(BF16)
    gk_out[0] = zg[:, GLA_DK:2 * GLA_DK].astype(BF16)
    gv_out[0] = zg[:, 2 * GLA_DK:2 * GLA_DK + GLA_DV].astype(BF16)
    gr = zg[:, 2 * GLA_DK + GLA_DV:]
    gr_out[0] = (gr * _sigmoid(gr)).astype(BF16)

    hy_out[0] = _dot(h, wh_ref[...])
    gate_out[0] = _sigmoid(_dot(h, wz_ref[...])).astype(BF16)


def _rope_partner(w):
    a = MLA_ROPE // 4
    perm = np.concatenate([np.arange(a, 2 * a), np.arange(0, a), np.arange(3 * a, 4 * a), np.arange(2 * a, 3 * a)])
    sign = np.concatenate([-np.ones(a), np.ones(a), -np.ones(a), np.ones(a)]).astype(np.float32)
    return w[:, perm] * sign


def _prep_inproj_weights(w_in, mla_w_uq, mla_w_ukv, gla_w_a2, gla_b_a):
    offs = np.concatenate([[0], np.cumsum(IN_SIZES)])
    seg = [w_in[:, offs[i]:offs[i + 1]] for i in range(len(IN_SIZES))]
    w_cq, w_ckv, w_kr, w_gq, w_gk, w_gv, w_gr, w_af, w_ab, w_hy, w_gate = seg
    d = w_in.shape[0]
    z = lambda n: jnp.zeros((d, n), w_in.dtype)
    kr_tile = jnp.concatenate([z(MLA_NOPE), w_kr, z(HEAD_SLOT - MLA_NOPE - MLA_ROPE)], axis=1)
    krp_tile = jnp.concatenate([z(MLA_NOPE), _rope_partner(w_kr), z(HEAD_SLOT - MLA_NOPE - MLA_ROPE)], axis=1)
    a_tile = jnp.concatenate([w_af, w_ab, z(LANES - 2 * GLA_GATE_RANK)], axis=1)
    wa = jnp.concatenate([w_cq, w_ckv, kr_tile, krp_tile, a_tile], axis=1)
    wg = jnp.concatenate([w_gq, w_gk, w_gv, w_gr], axis=1)

    dh = MLA_NOPE + MLA_ROPE
    zq = lambda n: jnp.zeros((MLA_Q_LORA, n), w_in.dtype)
    plain, partner = [], []
    for hd in range(MLA_HEADS):
        blk = mla_w_uq[:, hd * dh:(hd + 1) * dh]
        plain += [blk, zq(HEAD_SLOT - dh)]
        partner += [zq(MLA_NOPE), _rope_partner(blk[:, MLA_NOPE:]), zq(HEAD_SLOT - dh)]
    wuq = jnp.concatenate(plain + partner, axis=1)

    zk = jnp.zeros((MLA_KV_LORA, HEAD_SLOT - MLA_NOPE), w_in.dtype)
    kcols, vcols = [], []
    for hd in range(MLA_HEADS):
        blk = mla_w_ukv[:, hd * (MLA_NOPE + MLA_V):(hd + 1) * (MLA_NOPE + MLA_V)]
        kcols += [blk[:, :MLA_NOPE], zk]
        vcols += [blk[:, MLA_NOPE:], zk]
    wukv = jnp.concatenate(kcols + vcols, axis=1)

    wa2 = jnp.zeros((LANES, 2 * GLA_DK), w_in.dtype)
    wa2 = wa2.at[0:GLA_GATE_RANK, 0:GLA_DK].set(gla_w_a2[0])
    wa2 = wa2.at[GLA_GATE_RANK:2 * GLA_GATE_RANK, GLA_DK:].set(gla_w_a2[1])
    ba = jnp.concatenate([gla_b_a[0], gla_b_a[1]])[None, :]
    bf = lambda t: t.astype(BF16)
    return bf(wa), bf(wg), bf(w_hy), bf(w_gate), bf(wuq), bf(wukv), bf(wa2), ba


def _rope_tables(seq, ctx_len):
    rows = seq // GRID_W
    row = jnp.repeat(jnp.arange(rows, dtype=F32), GRID_W)
    col = jnp.tile(jnp.arange(GRID_W, dtype=F32), rows)
    a = MLA_ROPE // 4
    inv = ROPE_BASE ** (-jnp.arange(a, dtype=F32) / a)
    ang_r = row[:, None] * inv
    ang_c = col[:, None] * inv
    cos32 = jnp.concatenate([jnp.cos(ang_r), jnp.cos(ang_r), jnp.cos(ang_c), jnp.cos(ang_c)], axis=1)
    sin32 = jnp.concatenate([jnp.sin(ang_r), jnp.sin(ang_r), jnp.sin(ang_c), jnp.sin(ang_c)], axis=1)
    pad_r = HEAD_SLOT - MLA_NOPE - MLA_ROPE
    cos = jnp.concatenate([jnp.ones((seq, MLA_NOPE), F32), cos32, jnp.zeros((seq, pad_r), F32)], axis=1)
    sin = jnp.concatenate([jnp.zeros((seq, MLA_NOPE), F32), sin32, jnp.zeros((seq, pad_r), F32)], axis=1)
    cos_c = jnp.concatenate([jnp.ones((ctx_len, MLA_NOPE + MLA_ROPE), F32), jnp.zeros((ctx_len, pad_r), F32)], axis=1)
    sin_c = jnp.zeros((ctx_len, HEAD_SLOT), F32)
    return jnp.concatenate([cos, cos_c], axis=0), jnp.concatenate([sin, sin_c], axis=0)


def _inproj(xc, mod_l, g1, weights, q_norm, kv_norm, cos, sin, n_lat_tiles):
    bsz, t, d = xc.shape
    nt = t // TM
    wa, wg, wh, wz, wuq, wukv, wa2, ba = weights
    tile = lambda w: pl.BlockSpec((1, TM, w), lambda b, i: (b, i, 0))
    row = lambda w: pl.BlockSpec((1, w), lambda b, i: (0, 0))
    mod_spec = pl.BlockSpec((1, 6, d), lambda b, i: (jnp.where(i < n_lat_tiles, b, bsz), 0, 0))
    tab = pl.BlockSpec((TM, HEAD_SLOT), lambda b, i: (i, 0))
    nq = MLA_HEADS * HEAD_SLOT
    sds = lambda w, dt: jax.ShapeDtypeStruct((bsz, t, w), dt)
    out_shape = (sds(nq, BF16), sds(nq, BF16), sds(nq, BF16), sds(GLA_DK, BF16), sds(GLA_DK, BF16),
                 sds(GLA_DV, BF16), sds(GLA_DV, BF16), sds(2 * GLA_DK, F32), sds(3 * HY_WIDTH, F32),
                 sds(3 * D_MODEL, BF16))
    out_specs = (tile(nq), tile(nq), tile(nq), tile(GLA_DK), tile(GLA_DK), tile(GLA_DV), tile(GLA_DV),
                 tile(2 * GLA_DK), tile(3 * HY_WIDTH), tile(3 * D_MODEL))
    return pl.pallas_call(
        _inproj_kernel,
        out_shape=out_shape,
        grid=(bsz, nt),
        in_specs=[tile(d), mod_spec, row(d), _const_spec(wa.shape), _const_spec(wg.shape), _const_spec(wh.shape),
                  _const_spec(wz.shape), row(MLA_Q_LORA), row(MLA_KV_LORA), _const_spec(wuq.shape),
                  _const_spec(wukv.shape), _const_spec(wa2.shape), row(2 * GLA_DK), tab, tab],
        out_specs=out_specs,
        compiler_params=_cparams(2, VMEM_LIMIT),
        name="inproj",
    )(xc, mod_l, g1[None, :], wa, wg, wh, wz, q_norm[None, :], kv_norm[None, :], wuq, wukv, wa2, ba, cos, sin)


ATT_TK = 1024


ATT_TQ = 512


def _attn_kernel(*refs, chunks, aliased):
    if aliased:
        q_ref, k_ref, v_ref, _, o_ref, m_ref, acc_ref = refs
    else:
        q_ref, k_ref, v_ref, o_ref, m_ref, acc_ref = refs
    tq = q_ref.shape[1]
    m_ref[...] = jnp.full(m_ref.shape, -jnp.inf, F32)
    acc_ref[...] = jnp.zeros(acc_ref.shape, F32)
    for r0, size in chunks:
        k = k_ref[0, pl.ds(r0, size), :]
        v = v_ref[0, pl.ds(r0, size), :]
        for hd in range(2):
            sl = slice(hd * HEAD_SLOT, (hd + 1) * HEAD_SLOT)
            s = _dot_nt(q_ref[0, :, sl], k[:, sl])
            m_prev = m_ref[hd]
            m_new = jnp.maximum(m_prev, jnp.max(s, axis=1, keepdims=True))
            p = jnp.exp2((s - jnp.concatenate([m_new] * (size // LANES), axis=1)).astype(BF16))
            acc_ref[hd] = jnp.exp2(m_prev - m_new) * acc_ref[hd] + _dot(p, v[:, sl])
            m_ref[hd] = m_new
    a0 = acc_ref[0]
    a1 = acc_ref[1]
    lane = lax.broadcasted_iota(jnp.int32, (tq, HEAD_SLOT), 1)
    o0 = a0 / pltpu.roll(a0, MLA_V, 1)
    o1 = pltpu.roll(a1, MLA_V, 1) / a1
    o_ref[0] = jnp.where(lane < MLA_V, o0, o1).astype(o_ref.dtype)


def _attention(q, k, v, seq, with_ctx_queries):
    bsz, t, _ = q.shape
    ctx_len = t - seq
    assert seq % ATT_TK == 0 and seq % ATT_TQ == 0 and seq % ctx_len == 0 and 2 * MLA_V == HEAD_SLOT
    pair = 2 * HEAD_SLOT
    n_chunks = seq // ATT_TK
    chunks = tuple((j * ATT_TK, ATT_TK) for j in range(n_chunks - 1))
    chunks += (((n_chunks - 1) * ATT_TK, ATT_TK + ctx_len),)
    scratch = lambda tq: [pltpu.VMEM((2, tq, LANES), F32), pltpu.VMEM((2, tq, HEAD_SLOT), F32)]
    y = pl.pallas_call(
        functools.partial(_attn_kernel, chunks=chunks, aliased=False),
        out_shape=jax.ShapeDtypeStruct((bsz, t, MLA_OUT), BF16),
        grid=(bsz, MLA_HEADS // 2, seq // ATT_TQ),
        in_specs=[
            pl.BlockSpec((1, ATT_TQ, pair), lambda b, hp, i: (b, i, hp)),
            pl.BlockSpec((1, t, pair), lambda b, hp, i: (b, 0, hp)),
            pl.BlockSpec((1, t, pair), lambda b, hp, i: (b, 0, hp)),
        ],
        out_specs=pl.BlockSpec((1, ATT_TQ, HEAD_SLOT), lambda b, hp, i: (b, i, hp)),
        scratch_shapes=scratch(ATT_TQ),
        compiler_params=_cparams(3, VMEM_LIMIT),
        name="mla_attention",
    )(q, k, v)
    if not with_ctx_queries:
        return y
    cblk = seq // ctx_len
    ctx_rows = lambda w: pl.BlockSpec((1, ctx_len, w), lambda b, hp: (b, cblk, hp))
    return pl.pallas_call(
        functools.partial(_attn_kernel, chunks=((0, ctx_len),), aliased=True),
        out_shape=jax.ShapeDtypeStruct((bsz, t, MLA_OUT), BF16),
        grid=(bsz, MLA_HEADS // 2),
        in_specs=[ctx_rows(pair), ctx_rows(pair), ctx_rows(pair), pl.BlockSpec(memory_space=pl.ANY)],
        out_specs=ctx_rows(HEAD_SLOT),
        scratch_shapes=scratch(ctx_len),
        input_output_aliases={3: 0},
        compiler_params=_cparams(2),
        name="mla_attention_ctx",
    )(q, k, v, y)


GLA_LEVELS = int(math.log2(TM))


def _gla_level_matrices():
    i = np.arange(TM)[:, None]
    t = np.arange(TM)[None, :]
    fwd = [(t <= i)]
    for lv in range(GLA_LEVELS):
        m = TM >> (lv + 1)
        lo = (i // m) * m
        later = ((i // m) % 2) == 1
        q_part = later & (t >= lo) & (t <= i)
        k_part = (~later) & (t > i) & (t <= lo + m - 1)
        fwd.append(q_part | k_part)
    fwd = np.concatenate(fwd, axis=0).astype(np.float32)
    nb = 1 + GLA_LEVELS
    bwd = fwd.reshape(nb, TM, TM)[:, ::-1, ::-1].reshape(nb * TM, TM)
    return np.stack([fwd, bwd])


def _gla_kernel(q_ref, k_ref, v_ref, g_ref, mall_ref, o_ref, s_ref, a_ref):
    d = pl.program_id(1)
    step = pl.program_id(2)

    @pl.when(step == 0)
    def _():
        s_ref[...] = jnp.zeros(s_ref.shape, F32)

    q = q_ref[0].astype(F32)
    k = k_ref[0].astype(F32)
    v = v_ref[0]
    g = g_ref[0]
    g_hi = g.astype(BF16)
    g_lo = (g - g_hi.astype(F32)).astype(BF16)
    e2 = _dot(mall_ref[0], jnp.concatenate([g_hi, g_lo], axis=1))
    e_all = e2[:, :GLA_DK] + e2[:, GLA_DK:]
    g_cum = e_all[0:TM]
    g_tot = jnp.sum(g, axis=0, keepdims=True)

    row = lax.broadcasted_iota(jnp.int32, (TM, TM), 0)
    col = lax.broadcasted_iota(jnp.int32, (TM, TM), 1)
    lane_head = lax.broadcasted_iota(jnp.int32, (TM, GLA_DK), 1) // GLA_HK
    tok = lax.broadcasted_iota(jnp.int32, (TM, GLA_DK), 0)

    def stack_heads(t):
        return jnp.concatenate([jnp.where(lane_head == hd, t, 0.0) for hd in range(GLA_HEADS)], axis=0).astype(BF16)

    res = _dot_nt(stack_heads(q), k.astype(BF16))
    eye = row == col
    for hd in range(GLA_HEADS):
        a_ref[hd] = jnp.where(eye, res[hd * TM:(hd + 1) * TM], 0.0)
    for lv in range(GLA_LEVELS):
        m = TM >> (lv + 1)
        w = jnp.exp(e_all[(lv + 1) * TM:(lv + 2) * TM])
        bit = (tok // m) % 2
        q_act = bit != d
        qt = jnp.where(q_act, q * w, 0.0)
        kt = jnp.where(q_act, 0.0, k * w).astype(BF16)
        res = _dot_nt(stack_heads(qt), kt)
        if m == TM // 2:
            for hd in range(GLA_HEADS):
                a_ref[hd] += res[hd * TM:(hd + 1) * TM]
        else:
            same = (row // (2 * m)) == (col // (2 * m))
            for hd in range(GLA_HEADS):
                a_ref[hd] += jnp.where(same, res[hd * TM:(hd + 1) * TM], 0.0)

    s_old = s_ref[...]
    s_bf = s_old.astype(BF16)
    qg = q * jnp.exp(g_cum)
    for hd in range(GLA_HEADS):
        vh = v[:, hd * GLA_HV:(hd + 1) * GLA_HV]
        o_intra = _dot(a_ref[hd].astype(BF16), vh)
        o_inter = _dot(jnp.where(lane_head == hd, qg, 0.0).astype(BF16), s_bf)
        o_ref[0, 0, :, hd * GLA_HV:(hd + 1) * GLA_HV] = (o_intra + o_inter).astype(o_ref.dtype)

    kdec_t = (k * jnp.exp(g_tot - g_cum)).T.astype(BF16)
    upd = _dot(kdec_t, v)
    row_head = lax.broadcasted_iota(jnp.int32, (GLA_DK, GLA_HV), 0) // GLA_HK
    new = jnp.zeros((GLA_DK, GLA_HV), F32)
    for hd in range(GLA_HEADS):
        new = new + jnp.where(row_head == hd, upd[:, hd * GLA_HV:(hd + 1) * GLA_HV], 0.0)
    dec_col = jnp.sum(jnp.where(eye, jnp.broadcast_to(jnp.exp(g_tot), (TM, GLA_DK)), 0.0), axis=1, keepdims=True)
    s_ref[...] = dec_col * s_old + new


def _gla(gq, gk, gv, glog, n_lat_tiles):
    assert GLA_DK == TM
    bsz, t, _ = gq.shape
    nt = t // TM
    mall = jnp.asarray(_gla_level_matrices(), dtype=BF16)

    def tile_idx(d, s):
        return jnp.where(s == 0, n_lat_tiles, jnp.where(d == 0, s - 1, n_lat_tiles - s))

    tok = lambda w: pl.BlockSpec((1, TM, w), lambda b, d, s: (b, tile_idx(d, s), 0))
    return pl.pallas_call(
        _gla_kernel,
        out_shape=jax.ShapeDtypeStruct((bsz, 2, t, GLA_DV), BF16),
        grid=(bsz, 2, nt),
        in_specs=[tok(GLA_DK), tok(GLA_DK), tok(GLA_DV),
                  pl.BlockSpec((1, TM, GLA_DK), lambda b, d, s: (b, tile_idx(d, s), d)),
                  pl.BlockSpec((1,) + mall.shape[1:], lambda b, d, s: (d, 0, 0))],
        out_specs=pl.BlockSpec((1, 1, TM, GLA_DV), lambda b, d, s: (b, d, tile_idx(d, s), 0)),
        scratch_shapes=[pltpu.VMEM((GLA_DK, GLA_HV), F32), pltpu.VMEM((GLA_HEADS, TM, TM), F32)],
        compiler_params=_cparams(3),
        name="gla_scan",
    )(gq, gk, gv, glog, mall)


HY_N1 = 64
HY_N2 = 128
HY_SLAB = SUBLANES
HY_CB = 128


def _hyena_dft_constants(seq):
    n = 2 * seq
    assert n == HY_N1 * HY_N2
    nh = HY_N1 // 2
    kh = HY_N2 // 2
    eye = np.eye(HY_SLAB)
    k1 = np.arange(HY_N1)
    th = 2 * np.pi * np.outer(k1 + 0.5, np.arange(nh)) / HY_N1
    fwd1 = np.concatenate([np.kron(np.cos(th), eye), np.kron(-np.sin(th), eye)], axis=0)
    inv1 = (2.0 / n) * np.concatenate([np.kron(np.cos(th).T, eye), np.kron(-np.sin(th).T, eye)], axis=1)
    nlo = np.arange(HY_N2)
    k2 = np.arange(kh)
    ph = 2 * np.pi * (k2[None, :, None] * nlo[None, None, :] / HY_N2
                      + (k1[:, None, None] + 0.5) * nlo[None, None, :] / n)
    c, s = np.cos(ph), np.sin(ph)
    fwd2 = np.concatenate([np.concatenate([c, s], axis=2), np.concatenate([-s, c], axis=2)], axis=1)
    ct, st = c.transpose(0, 2, 1), s.transpose(0, 2, 1)
    inv2 = np.concatenate([np.concatenate([ct, -st], axis=2), np.concatenate([st, ct], axis=2)], axis=1)
    return fwd1, fwd2, inv2, inv1


def _hyena_ctx_dft_constants(ctx_len):
    n = 2 * ctx_len
    th = 2 * np.pi * np.outer(np.arange(ctx_len) + 0.5, np.arange(ctx_len)) / n
    fwd = np.concatenate([np.cos(th), -np.sin(th)], axis=0)
    inv = (2.0 / n) * np.concatenate([np.cos(th).T, -np.sin(th).T], axis=1)
    return fwd, inv


def _hyena_features(length):
    pos = jnp.arange(length, dtype=F32)
    t = pos / max(length - 1, 1)
    f = jnp.linspace(1e-4, HY_BANDS - 1, HY_BANDS, dtype=F32)
    ang = (2.0 * math.pi / length) * pos[:, None] * f
    feat = jnp.concatenate([t[:, None], jnp.cos(ang), jnp.sin(ang)], axis=-1)
    return jnp.pad(feat, ((0, 0), (0, LANES - HY_POS_DIM)))


def _hy_filter_kernel(feat_ref, w1_ref, b1_ref, w2_ref, b2_ref, w3_ref, b3_ref, absd_ref, h_ref, s_ref):
    i = pl.program_id(0)
    feat = feat_ref[...]
    hp = lambda a, b: jnp.dot(a, b, preferred_element_type=F32, precision=HIGHEST)
    hdn = jnp.sin(hp(feat, w1_ref[...]) + b1_ref[...])
    hdn = jnp.sin(hp(hdn, w2_ref[...]) + b2_ref[...])
    h = hp(hdn, w3_ref[...]) + b3_ref[...]
    window = jnp.exp(-feat[:, 0:1] * absd_ref[...])
    h = h * jnp.concatenate([window] * (2 * HY_ORDER), axis=1)
    h_ref[...] = h

    @pl.when(i == 0)
    def _():
        s_ref[...] = jnp.zeros(s_ref.shape, F32)

    s_ref[...] += jnp.sum(jnp.abs(h), axis=0, keepdims=True)


def _hyena_filters_raw(length, filt_w):
    w1, b1, w2, b2, w3, b3 = filt_w
    nf = 2 * HY_ORDER * HY_WIDTH
    tr = min(length, 512)
    deltas = np.linspace(math.log(HY_DECAY_TARGET) / HY_FAST_DECAY, math.log(HY_DECAY_TARGET) / HY_SLOW_DECAY,
                         HY_WIDTH, dtype=np.float32)
    absd = jnp.asarray(np.abs(deltas))[None, :]
    w1p = jnp.pad(w1, ((0, LANES - HY_POS_DIM), (0, 0)))
    full = lambda shp: pl.BlockSpec(shp, lambda i: (0,) * len(shp))
    return pl.pallas_call(
        _hy_filter_kernel,
        out_shape=(jax.ShapeDtypeStruct((length, nf), F32), jax.ShapeDtypeStruct((1, nf), F32)),
        grid=(length // tr,),
        in_specs=[pl.BlockSpec((tr, LANES), lambda i: (i, 0)), full((LANES, HY_FILTER_HIDDEN)),
                  full((1, HY_FILTER_HIDDEN)), full((HY_FILTER_HIDDEN, HY_FILTER_HIDDEN)), full((1, HY_FILTER_HIDDEN)),
                  full((HY_FILTER_HIDDEN, nf)), full((1, nf)), full((1, HY_WIDTH))],
        out_specs=(pl.BlockSpec((tr, nf), lambda i: (i, 0)), full((1, nf))),
        compiler_params=_cparams(1),
        name="hyena_filter_mlp",
    )(_hyena_features(length), w1p, b1[None, :], w2, b2[None, :], w3, b3[None, :], absd)


def _split_bf16(m):
    hi = m.astype(BF16)
    return jnp.stack([hi, (m - hi.astype(F32)).astype(BF16)])


def _dot_split(m_hi, m_lo, x):
    x_hi = x.astype(BF16)
    x_lo = (x - x_hi.astype(F32)).astype(BF16)
    n = x.shape[1]
    r = _dot(m_hi, jnp.concatenate([x_hi, x_lo], axis=1))
    return r[:, :n] + r[:, n:] + _dot(m_lo, x_hi)


def _odft_stage1(src_at, mm, s_re, s_im):
    half = HY_N1 * HY_SLAB

    def body(j, carry):
        r0 = pl.multiple_of(j * HY_SLAB, HY_SLAB)
        slab = src_at(r0)
        cb = slab.shape[-1]
        res = mm(slab.reshape(-1, cb))
        s_re[:, pl.ds(r0, HY_SLAB), :] = res[:half].reshape(HY_N1, HY_SLAB, cb)
        s_im[:, pl.ds(r0, HY_SLAB), :] = res[half:].reshape(HY_N1, HY_SLAB, cb)
        return carry

    lax.fori_loop(0, HY_N2 // HY_SLAB, body, 0, unroll=2)


def _hy_spectrum_kernel(hf_ref, hb_ref, sf_ref, sb_ref, fwd1_ref, fwd2_ref, o_ref, s_re, s_im):
    kh = HY_N2 // 2
    mm1 = lambda x: _dot_split(fwd1_ref[0], fwd1_ref[1], x)

    def middle(sign):
        def body(k1, carry):
            a = jnp.concatenate([s_re[k1], s_im[k1]], axis=0)
            x = _dot_split(fwd2_ref[0, k1], fwd2_ref[1, k1], a)
            if sign is None:
                o_ref[0, k1] = x[:kh]
                o_ref[1, k1] = x[kh:]
            else:
                inv_norm = 1.0 / (sf_ref[...] + sb_ref[...])
                o_ref[0, k1] = (o_ref[0, k1] + x[:kh]) * inv_norm
                o_ref[1, k1] = (o_ref[1, k1] - x[kh:]) * inv_norm
            return carry
        lax.fori_loop(0, HY_N1, body, 0, unroll=2)

    _odft_stage1(lambda r0: hf_ref[:, pl.ds(r0, HY_SLAB), :], mm1, s_re, s_im)
    middle(None)

    def bwd_slab(r0):
        slab = hb_ref[:, pl.ds(r0, HY_SLAB), :]
        nhi = lax.broadcasted_iota(jnp.int32, slab.shape, 0)
        r = lax.broadcasted_iota(jnp.int32, slab.shape, 1)
        return jnp.where((nhi == 0) & (r + r0 == 0), 0.0, slab)

    _odft_stage1(bwd_slab, mm1, s_re, s_im)
    middle(-1)


def _hyena_filter_spectrum(seq, w1, b1, w2, b2, w3, b3):
    h_raw, s = _hyena_filters_raw(seq, (w1, b1, w2, b2, w3, b3))
    nh = HY_N1 // 2
    nc = HY_ORDER * HY_WIDTH
    h3 = h_raw.reshape(nh, HY_N2, 2 * nc)
    fwd1, fwd2, _, _ = _hyena_dft_constants(seq)
    fwd1 = _split_bf16(jnp.asarray(fwd1, F32))
    fwd2 = _split_bf16(jnp.asarray(fwd2, F32))
    ncb = nc // HY_CB
    return pl.pallas_call(
        _hy_spectrum_kernel,
        out_shape=jax.ShapeDtypeStruct((2, HY_N1, HY_N2 // 2, nc), F32),
        grid=(ncb,),
        in_specs=[pl.BlockSpec((nh, HY_N2, HY_CB), lambda c: (0, 0, c)),
                  pl.BlockSpec((nh, HY_N2, HY_CB), lambda c: (0, 0, ncb + c)),
                  pl.BlockSpec((1, HY_CB), lambda c: (0, c)),
                  pl.BlockSpec((1, HY_CB), lambda c: (0, ncb + c)),
                  _const_spec(fwd1.shape), _const_spec(fwd2.shape)],
        out_specs=pl.BlockSpec((2, HY_N1, HY_N2 // 2, HY_CB), lambda c: (0, 0, 0, c)),
        scratch_shapes=[pltpu.VMEM((HY_N1, HY_N2, HY_CB), F32), pltpu.VMEM((HY_N1, HY_N2, HY_CB), F32)],
        compiler_params=_cparams(1, VMEM_LIMIT),
        name="hyena_filter_spectrum",
    )(h3, h3, s, s, fwd1, fwd2)


def _short_conv_chunk(ref, c, n_chunks, w_ref, b_ref):
    per = TM // HY_N2
    cur = ref[0, pl.ds(per * c, per)]
    cb = cur.shape[-1]
    cur = cur.reshape(TM, cb)
    prev = ref[0, jnp.maximum(per * c - 1, 0), pl.ds(HY_N2 - SUBLANES, SUBLANES), :][SUBLANES - 1:SUBLANES]
    nxt = ref[0, jnp.minimum(per * c + per, per * n_chunks - 1), pl.ds(0, SUBLANES), :][0:1]
    prev = jnp.where(c > 0, prev, 0.0)
    nxt = jnp.where(c < n_chunks - 1, nxt, 0.0)
    rowi = lax.broadcasted_iota(jnp.int32, (TM, cb), 0)
    dn = jnp.where(rowi == 0, prev, pltpu.roll(cur, 1, 0))
    up = jnp.where(rowi == TM - 1, nxt, pltpu.roll(cur, TM - 1, 0))
    return b_ref[...] + w_ref[0:1, :] * dn + w_ref[1:2, :] * cur + w_ref[2:3, :] * up


def _hy_conv_kernel(*refs, conv_y):
    if conv_y:
        (y_ref, g_ref, h_ref, bias_ref, wy_ref, by_ref, wg_ref, bg_ref, fwd1_ref, fwd2_ref, inv2_ref, inv1_ref,
         o_ref, s_re, s_im, gs_ref, us_ref) = refs
    else:
        (y_ref, g_ref, h_ref, bias_ref, wg_ref, bg_ref, fwd1_ref, fwd2_ref, inv2_ref, inv1_ref,
         o_ref, s_re, s_im, gs_ref) = refs
    nh = HY_N1 // 2
    per = TM // HY_N2
    n_chunks = nh // per
    cb = o_ref.shape[-1]
    kh = HY_N2 // 2

    def pre(c, carry):
        gs_ref[pl.ds(per * c, per)] = _short_conv_chunk(g_ref, c, n_chunks, wg_ref, bg_ref).reshape(per, HY_N2, cb)
        if conv_y:
            us_ref[pl.ds(per * c, per)] = _short_conv_chunk(y_ref, c, n_chunks, wy_ref, by_ref).reshape(per, HY_N2, cb)
        return carry

    lax.fori_loop(0, n_chunks, pre, 0)
    if conv_y:
        u_at = lambda r0: us_ref[:, pl.ds(r0, HY_SLAB), :]
    else:
        u_at = lambda r0: y_ref[0, :, pl.ds(r0, HY_SLAB), :]

    _odft_stage1(u_at, lambda x: _dot(fwd1_ref[...], x.astype(BF16)), s_re, s_im)

    def middle(k1, carry):
        a = jnp.concatenate([s_re[k1], s_im[k1]], axis=0).astype(BF16)
        x = _dot(fwd2_ref[k1], a)
        xr, xi = x[:kh], x[kh:]
        hr, hi = h_ref[0, k1], h_ref[1, k1]
        y = jnp.concatenate([xr * hr - xi * hi, xr * hi + xi * hr], axis=0).astype(BF16)
        bm = _dot(inv2_ref[k1], y)
        s_re[k1] = bm[:HY_N2]
        s_im[k1] = bm[HY_N2:]
        return carry

    lax.fori_loop(0, HY_N1, middle, 0, unroll=4)

    def post(j, carry):
        r0 = pl.multiple_of(j * HY_SLAB, HY_SLAB)
        slab = jnp.concatenate([s_re[:, pl.ds(r0, HY_SLAB), :].reshape(HY_N1 * HY_SLAB, cb),
                                s_im[:, pl.ds(r0, HY_SLAB), :].reshape(HY_N1 * HY_SLAB, cb)], axis=0).astype(BF16)
        conv = _dot(inv1_ref[...], slab).reshape(nh, HY_SLAB, cb)
        u = u_at(r0)
        o_ref[0, :, pl.ds(r0, HY_SLAB), :] = gs_ref[:, pl.ds(r0, HY_SLAB), :] * (conv + bias_ref[...] * u)
        return carry

    lax.fori_loop(0, HY_N2 // HY_SLAB, post, 0, unroll=2)


def _hyena_order(y4, y_col0, z4, gate_col0, hspec, order, hy_bias, short_w, short_b, consts, conv_y):
    bsz = z4.shape[0]
    nh = HY_N1 // 2
    ncb = HY_WIDTH // HY_CB
    fwd1, fwd2, inv2, inv1 = consts
    blk4 = lambda off: pl.BlockSpec((1, nh, HY_N2, HY_CB), lambda c, b: (b, 0, 0, off + c))
    rowspec = lambda rows, off: pl.BlockSpec((rows, HY_CB), lambda c, b: (0, off + c))
    in_specs = [blk4(y_col0), blk4(gate_col0),
                pl.BlockSpec((2, HY_N1, HY_N2 // 2, HY_CB), lambda c, b: (0, 0, 0, order * ncb + c),
                             pipeline_mode=pl.Buffered(1)),
                rowspec(1, 0)]
    args = [y4, z4, hspec, hy_bias.reshape(1, -1)]
    if conv_y:
        in_specs += [rowspec(HY_SHORT, y_col0), rowspec(1, y_col0)]
        args += [short_w, short_b[None, :]]
    in_specs += [rowspec(HY_SHORT, gate_col0), rowspec(1, gate_col0)]
    args += [short_w, short_b[None, :]]
    in_specs += [_const_spec(fwd1.shape), _const_spec(fwd2.shape), _const_spec(inv2.shape), _const_spec(inv1.shape)]
    args += [fwd1, fwd2, inv2, inv1]
    scratch = [pltpu.VMEM((HY_N1, HY_N2, HY_CB), F32), pltpu.VMEM((HY_N1, HY_N2, HY_CB), F32),
               pltpu.VMEM((nh, HY_N2, HY_CB), F32)]
    if conv_y:
        scratch.append(pltpu.VMEM((nh, HY_N2, HY_CB), F32))
    return pl.pallas_call(
        functools.partial(_hy_conv_kernel, conv_y=conv_y),
        out_shape=jax.ShapeDtypeStruct((bsz, nh, HY_N2, HY_WIDTH), F32),
        grid=(ncb, bsz),
        in_specs=in_specs,
        out_specs=pl.BlockSpec((1, nh, HY_N2, HY_CB), lambda c, b: (b, 0, 0, c)),
        scratch_shapes=scratch,
        compiler_params=_cparams(2, VMEM_LIMIT),
        name="hyena_conv%d" % order,
    )(*args)


def _hyena_latent(z_hy, hspec, short_w, short_b, hy_bias):
    bsz, t, _ = z_hy.shape
    seq = HY_N1 * HY_N2 // 2
    consts = tuple(jnp.asarray(m, BF16) for m in _hyena_dft_constants(seq))
    z4 = z_hy.reshape(bsz, t // HY_N2, HY_N2, 3 * HY_WIDTH)
    ncb = HY_WIDTH // HY_CB
    y1 = _hyena_order(z4, 2 * ncb, z4, 0, hspec, 0, hy_bias[0], short_w, short_b, consts, True)
    y2 = _hyena_order(y1, 0, z4, ncb, hspec, 1, hy_bias[1], short_w, short_b, consts, False)
    return y2.reshape(bsz, seq, HY_WIDTH)


def _hy_ctx_spectrum_kernel(h_ref, s_ref, fwd_ref, o_ref):
    lc = h_ref.shape[0]
    nc = HY_ORDER * HY_WIDTH
    hp = lambda a, b: jnp.dot(a, b, preferred_element_type=F32, precision=HIGHEST)
    h = h_ref[...]
    rowi = lax.broadcasted_iota(jnp.int32, (lc, nc), 0)
    xf = hp(fwd_ref[...], h[:, :nc])
    xb = hp(fwd_ref[...], jnp.where(rowi == 0, 0.0, h[:, nc:]))
    inv_norm = 1.0 / (s_ref[:, :nc] + s_ref[:, nc:])
    o_ref[0] = (xf[:lc] + xb[:lc]) * inv_norm
    o_ref[1] = (xf[lc:] - xb[lc:]) * inv_norm


def _hy_ctx_conv_kernel(x1_ref, x2_ref, v_ref, h_ref, bias_ref, w_ref, b_ref, fwd_ref, inv_ref, o_ref):
    lc = o_ref.shape[1]
    rowi = lax.broadcasted_iota(jnp.int32, (lc, HY_WIDTH), 0)

    def short(ref, part):
        cur = ref[0].reshape(lc, HY_WIDTH)
        sl = slice(part * HY_WIDTH, (part + 1) * HY_WIDTH)
        dn = jnp.where(rowi == 0, 0.0, pltpu.roll(cur, 1, 0))
        up = jnp.where(rowi == lc - 1, 0.0, pltpu.roll(cur, lc - 1, 0))
        return b_ref[:, sl] + w_ref[0:1, sl] * dn + w_ref[1:2, sl] * cur + w_ref[2:3, sl] * up

    y = short(v_ref, 2)
    for order, gref in enumerate((x1_ref, x2_ref)):
        sl = slice(order * HY_WIDTH, (order + 1) * HY_WIDTH)
        x = _dot(fwd_ref[...], y.astype(BF16))
        xr, xi = x[:lc], x[lc:]
        hr, hi = h_ref[0, :, sl], h_ref[1, :, sl]
        prod = jnp.concatenate([xr * hr - xi * hi, xr * hi + xi * hr], axis=0).astype(BF16)
        conv = _dot(inv_ref[...], prod)
        y = short(gref, order) * (conv + bias_ref[order:order + 1, :] * y)
    o_ref[0] = y


def _hyena_ctx(z_hy, filt_w, short_w, short_b, hy_bias):
    bsz, t, _ = z_hy.shape
    seq = HY_N1 * HY_N2 // 2
    lc = t - seq
    per = lc // HY_N2
    h_raw, s = _hyena_filters_raw(lc, filt_w)
    fwd, inv = _hyena_ctx_dft_constants(lc)
    nc = HY_ORDER * HY_WIDTH
    full = lambda shp: pl.BlockSpec(shp, lambda *_: (0,) * len(shp))
    hspec = pl.pallas_call(
        _hy_ctx_spectrum_kernel,
        out_shape=jax.ShapeDtypeStruct((2, lc, nc), F32),
        grid=(1,),
        in_specs=[full(h_raw.shape), full(s.shape), full(fwd.shape)],
        out_specs=full((2, lc, nc)),
        compiler_params=_cparams(1),
        name="hyena_ctx_spectrum",
    )(h_raw, s, jnp.asarray(fwd, F32))
    z4 = z_hy.reshape(bsz, t // HY_N2, HY_N2, 3 * HY_WIDTH)
    blk = lambda part: pl.BlockSpec((1, per, HY_N2, HY_WIDTH), lambda b: (b, seq // lc, 0, part))
    return pl.pallas_call(
        _hy_ctx_conv_kernel,
        out_shape=jax.ShapeDtypeStruct((bsz, lc, HY_WIDTH), F32),
        grid=(bsz,),
        in_specs=[blk(0), blk(1), blk(2), full((2, lc, nc)), full((HY_ORDER, HY_WIDTH)),
                  full((HY_SHORT, 3 * HY_WIDTH)), full((1, 3 * HY_WIDTH)), full(fwd.shape), full(inv.shape)],
        out_specs=pl.BlockSpec((1, lc, HY_WIDTH), lambda b: (b, 0, 0)),
        compiler_params=_cparams(1),
        name="hyena_ctx_conv",
    )(z4, z4, z4, hspec, hy_bias, short_w, short_b[None, :], jnp.asarray(fwd, BF16), jnp.asarray(inv, BF16))


def _merge_kernel(x_ref, mod_ref, ymla_ref, of_ref, ob_ref, gr_ref, yhy_ref, zg_ref, on_ref,
                  wm_ref, wgl_ref, wh_ref, wo_ref, o_ref):
    o = of_ref[0, 0].astype(F32) + ob_ref[0, 0].astype(F32)
    silu = gr_ref[0].astype(F32)
    parts = []
    for hd in range(GLA_HEADS):
        sl = slice(hd * GLA_HV, (hd + 1) * GLA_HV)
        parts.append((_rms(o[:, sl]) * on_ref[...] * silu[:, sl]).astype(BF16))
    y_gla = jnp.concatenate(parts, axis=1)
    zg = zg_ref[0].astype(F32)
    d = x_ref.shape[-1]
    m = zg[:, 0:d] * _dot(ymla_ref[0], wm_ref[...])
    m = m + zg[:, d:2 * d] * _dot(y_gla, wgl_ref[...])
    m = m + zg[:, 2 * d:3 * d] * _dot(yhy_ref[0].astype(BF16), wh_ref[...])
    out = _dot(m.astype(BF16), wo_ref[...])
    o_ref[0] = x_ref[0] + mod_ref[0, 2:3, :] * out


def _mod_spec(d, n_lat_tiles, bsz):
    return pl.BlockSpec((1, 6, d), lambda b, i: (jnp.where(i < n_lat_tiles, b, bsz), 0, 0))


def _merge(xc, mod_l, y_mla, o_gla, gr, y_hy, gate, out_norm, w_o_mla, w_o_gla, w_o_hy, w_out, n_tiles, n_lat_tiles):
    bsz, t, d = xc.shape
    tile = lambda w: pl.BlockSpec((1, TM, w), lambda b, i: (b, i, 0))
    dirspec = lambda dr: pl.BlockSpec((1, 1, TM, GLA_DV), lambda b, i: (b, dr, i, 0))
    bf = lambda w: w.astype(BF16)
    return pl.pallas_call(
        _merge_kernel,
        out_shape=jax.ShapeDtypeStruct((bsz, n_tiles * TM, d), F32),
        grid=(bsz, n_tiles),
        in_specs=[tile(d), _mod_spec(d, n_lat_tiles, bsz), tile(MLA_OUT), dirspec(0), dirspec(1), tile(GLA_DV),
                  tile(HY_WIDTH), tile(3 * d), pl.BlockSpec((1, GLA_HV), lambda b, i: (0, 0)),
                  _const_spec(w_o_mla.shape), _const_spec(w_o_gla.shape), _const_spec(w_o_hy.shape),
                  _const_spec(w_out.shape)],
        out_specs=tile(d),
        compiler_params=_cparams(2, VMEM_LIMIT),
        name="merge",
    )(xc, mod_l, y_mla, o_gla, o_gla, gr, y_hy, gate, out_norm[None, :], bf(w_o_mla), bf(w_o_gla), bf(w_o_hy),
      bf(w_out))


FF_CHUNK = 1024


def _mlp_kernel(x_ref, mod_ref, g2_ref, w1_ref, w2_ref, fg_ref, o_ref, *, final):
    x = x_ref[0]
    h = (_rms(x) * g2_ref[...] * (1.0 + mod_ref[0, 4:5, :]) + mod_ref[0, 3:4, :]).astype(BF16)
    acc = jnp.zeros(x.shape, F32)
    for j in range(w1_ref.shape[1] // FF_CHUNK):
        a = jnp.maximum(_dot(h, w1_ref[:, j * FF_CHUNK:(j + 1) * FF_CHUNK]), 0.0)
        acc = acc + _dot((a * a).astype(BF16), w2_ref[j * FF_CHUNK:(j + 1) * FF_CHUNK, :])
    xn = x + mod_ref[0, 5:6, :] * acc
    if final:
        xn = _rms(xn) * fg_ref[...]
    o_ref[0] = xn


def _mlp(xc, mod_l, g2, w1, w2, final_g, n_tiles, n_lat_tiles, final):
    bsz, t, d = xc.shape
    tile = pl.BlockSpec((1, TM, d), lambda b, i: (b, i, 0))
    row = pl.BlockSpec((1, d), lambda b, i: (0, 0))
    t_out = n_tiles * TM if final else t
    return pl.pallas_call(
        functools.partial(_mlp_kernel, final=final),
        out_shape=jax.ShapeDtypeStruct((bsz, t_out, d), F32),
        grid=(bsz, n_tiles),
        in_specs=[tile, _mod_spec(d, n_lat_tiles, bsz), row, _const_spec(w1.shape), _const_spec(w2.shape), row],
        out_specs=tile,
        compiler_params=_cparams(2, VMEM_LIMIT),
        name="mlp",
    )(xc, mod_l, g2[None, :], w1.astype(BF16), w2.astype(BF16), final_g[None, :])


def kernel(x, c, ctx, c_ctx, ada_w, ada_b, norm1_g, norm2_g, w_in, mla_q_norm, mla_w_uq, mla_kv_norm, mla_w_ukv, gla_w_a2, gla_b_a, gla_out_norm, hy_short_w, hy_short_b, hy_f_w1, hy_f_b1, hy_f_w2, hy_f_b2, hy_f_w3, hy_f_b3, hy_bias, w_o_mla, w_o_gla, w_o_hy, w_out, ff_w1, ff_w2, final_norm_g):
    bsz, seq, d = x.shape
    ctx_len = ctx.shape[1]
    n_lat = seq // TM
    n_all = (seq + ctx_len) // TM
    xc = jnp.concatenate([x, ctx], axis=1)
    cc = jnp.zeros((16, d), F32).at[:bsz].set(c).at[bsz].set(c_ctx)
    mod = _modulation(cc, ada_w, ada_b).reshape(DEPTH, 16, 6, d)
    cos, sin = _rope_tables(seq, ctx_len)
    for l in range(DEPTH):
        last = l == DEPTH - 1
        n_tiles = n_lat if last else n_all
        weights = _prep_inproj_weights(w_in[l], mla_w_uq[l], mla_w_ukv[l], gla_w_a2[l], gla_b_a[l])
        q, k, v, gq, gk, gv, gr, glog, z_hy, z_gate = _inproj(xc, mod[l], norm1_g[l], weights, mla_q_norm[l],
                                                             mla_kv_norm[l], cos, sin, n_lat)
        y_mla = _attention(q, k, v, seq, not last)
        o_gla = _gla(gq, gk, gv, glog, n_lat)
        filt_w = (hy_f_w1[l], hy_f_b1[l], hy_f_w2[l], hy_f_b2[l], hy_f_w3[l], hy_f_b3[l])
        hspec = _hyena_filter_spectrum(seq, *filt_w)
        y_hy = _hyena_latent(z_hy, hspec, hy_short_w[l], hy_short_b[l], hy_bias[l])
        if not last:
            y_hy_c = _hyena_ctx(z_hy, filt_w, hy_short_w[l], hy_short_b[l], hy_bias[l])
            y_hy = jnp.concatenate([y_hy, y_hy_c], axis=1)
        xc = _merge(xc, mod[l], y_mla, o_gla, gr, y_hy, z_gate, gla_out_norm[l], w_o_mla[l], w_o_gla[l], w_o_hy[l],
                    w_out[l], n_tiles, n_lat)
        xc = _mlp(xc, mod[l], norm2_g[l], ff_w1[l], ff_w2[l], final_norm_g, n_tiles, n_lat, last)
    return xc
```

```python
import functools
import math

import numpy as np
import jax
import jax.numpy as jnp
from jax import lax
from jax.experimental import pallas as pl
from jax.experimental.pallas import tpu as pltpu

F32 = jnp.float32
BF16 = jnp.bfloat16
HIGHEST = lax.Precision.HIGHEST
LOG2E = 1.4426950408889634

D_MODEL = 1024
DEPTH = 2
GRID_W = 64
EPS = 1e-6
MLA_HEADS = 8
MLA_NOPE = 64
MLA_ROPE = 32
MLA_V = 64
MLA_Q_LORA = 256
MLA_KV_LORA = 128
MLA_SCALE = (MLA_NOPE + MLA_ROPE) ** -0.5
ROPE_BASE = 10000.0
GLA_HEADS = 4
GLA_DK = 256
GLA_DV = 512
GLA_HK = GLA_DK // GLA_HEADS
GLA_HV = GLA_DV // GLA_HEADS
GLA_GATE_RANK = 16
GLA_TAU = 16.0
HY_WIDTH = 512
HY_ORDER = 2
HY_SHORT = 3
HY_BANDS = 16
HY_POS_DIM = 1 + 2 * HY_BANDS
HY_FILTER_HIDDEN = 64
HY_FAST_DECAY = 0.3
HY_SLOW_DECAY = 1.5
HY_DECAY_TARGET = 1e-2
D_FF = 4 * D_MODEL
MLA_OUT = MLA_HEADS * MLA_V
IN_SIZES = (MLA_Q_LORA, MLA_KV_LORA, MLA_ROPE, GLA_DK, GLA_DK, GLA_DV, GLA_DV, GLA_GATE_RANK, GLA_GATE_RANK,
            (HY_ORDER + 1) * HY_WIDTH, 3 * D_MODEL)

LANES = 128
SUBLANES = 8
TM = 256
HEAD_SLOT = 128
VMEM_LIMIT = 56 * 1024 * 1024


def _cparams(n_axes, vmem=None):
    return pltpu.CompilerParams(dimension_semantics=("arbitrary",) * n_axes, vmem_limit_bytes=vmem)


def _const_spec(shape):
    nd = len(shape)
    return pl.BlockSpec(shape, lambda *_: (0,) * nd, pipeline_mode=pl.Buffered(1))


def _rms(x):
    return x * lax.rsqrt(jnp.mean(x * x, axis=-1, keepdims=True) + EPS)


def _sigmoid(x):
    return 1.0 / (1.0 + jnp.exp(-x))


def _dot(a, b):
    return jnp.dot(a, b, preferred_element_type=F32)


def _dot_nt(a, b):
    return lax.dot_general(a, b, (((1,), (1,)), ((), ())), preferred_element_type=F32)


def _mod_kernel(cc_ref, w_ref, b_ref, o_ref):
    s = cc_ref[...]
    s = s * _sigmoid(s)
    o_ref[0] = jnp.dot(s, w_ref[0], preferred_element_type=F32, precision=HIGHEST) + b_ref[0]


def _modulation(cc, ada_w, ada_b):
    tn = 1536
    n6 = ada_w.shape[-1]
    return pl.pallas_call(
        _mod_kernel,
        out_shape=jax.ShapeDtypeStruct((DEPTH, 16, n6), F32),
        grid=(DEPTH, n6 // tn),
        in_specs=[
            pl.BlockSpec((16, D_MODEL), lambda l, j: (0, 0)),
            pl.BlockSpec((1, D_MODEL, tn), lambda l, j: (l, 0, j)),
            pl.BlockSpec((1, 1, tn), lambda l, j: (l, 0, j)),
        ],
        out_specs=pl.BlockSpec((1, 16, tn), lambda l, j: (l, 0, j)),
        compiler_params=_cparams(2),
        name="modulation",
    )(cc, ada_w, ada_b.reshape(DEPTH, 1, n6))


W_A = 768
W_G = 2 * GLA_DK + 2 * GLA_DV


def _inproj_kernel(x_ref, mod_ref, g1_ref, wa_ref, wg_ref, wh_ref, wz_ref, qn_ref, kvn_ref, wuq_ref, wukv_ref,
                   wa2_ref, ba_ref, cos_ref, sin_ref,
                   q_out, k_out, v_out, gq_out, gk_out, gv_out, gr_out, glog_out, hy_out, gate_out):
    x = x_ref[0]
    shift = mod_ref[0, 0:1, :]
    scale = mod_ref[0, 1:2, :]
    h = (_rms(x) * g1_ref[...] * (1.0 + scale) + shift).astype(BF16)

    za = _dot(h, wa_ref[...])
    cos = cos_ref[...]
    sin = sin_ref[...]

    cqn = (_rms(za[:, 0:256]) * qn_ref[...]).astype(BF16)
    qab = _dot(cqn, wuq_ref[...])
    nq = MLA_HEADS * HEAD_SLOT
    for hd in range(MLA_HEADS):
        sl = slice(hd * HEAD_SLOT, (hd + 1) * HEAD_SLOT)
        qa = qab[:, hd * HEAD_SLOT:(hd + 1) * HEAD_SLOT]
        qb = qab[:, nq + hd * HEAD_SLOT:nq + (hd + 1) * HEAD_SLOT]
        q_out[0, :, sl] = ((qa * cos + qb * sin) * (MLA_SCALE * LOG2E)).astype(BF16)

    ckvn = (_rms(za[:, 256:384]) * kvn_ref[...]).astype(BF16)
    kv = _dot(ckvn, wukv_ref[...])
    krot = za[:, 384:512] * cos + za[:, 512:640] * sin
    for hd in range(MLA_HEADS):
        sl = slice(hd * HEAD_SLOT, (hd + 1) * HEAD_SLOT)
        k_out[0, :, sl] = (kv[:, sl] + krot).astype(BF16)
    ones_hi = (lax.broadcasted_iota(jnp.int32, (1, HEAD_SLOT), 1) >= MLA_V).astype(F32)
    for hd in range(MLA_HEADS):
        sl = slice(hd * HEAD_SLOT, (hd + 1) * HEAD_SLOT)
        v_out[0, :, sl] = (kv[:, nq + hd * HEAD_SLOT:nq + (hd + 1) * HEAD_SLOT] + ones_hi).astype(BF16)

    xg = _dot(za[:, 640:768].astype(BF16), wa2_ref[...]) + ba_ref[...]
    glog_out[0] = (jnp.minimum(xg, 0.0) - jnp.log(1.0 + jnp.exp(-jnp.abs(xg)))) * (1.0 / GLA_TAU)

    zg = _dot(h, wg_ref[...])
    gq_out[0] = (zg[:, 0:GLA_DK] * (GLA_HK ** -0.5)).astype(BF16)
    gk_out[0] = zg[:, GLA_DK:2 * GLA_DK].astype(BF16)
    gv_out[0] = zg[:, 2 * GLA_DK:2 * GLA_DK + GLA_DV].astype(BF16)
    gr = zg[:, 2 * GLA_DK + GLA_DV:]
    gr_out[0] = (gr * _sigmoid(gr)).astype(BF16)

    hy_out[0] = _dot(h, wh_ref[...])
    gate_out[0] = _sigmoid(_dot(h, wz_ref[...])).astype(BF16)


def _rope_partner(w):
    a = MLA_ROPE // 4
    perm = np.concatenate([np.arange(a, 2 * a), np.arange(0, a), np.arange(3 * a, 4 * a), np.arange(2 * a, 3 * a)])
    sign = np.concatenate([-np.ones(a), np.ones(a), -np.ones(a), np.ones(a)]).astype(np.float32)
    return w[:, perm] * sign


def _prep_inproj_weights(w_in, mla_w_uq, mla_w_ukv, gla_w_a2, gla_b_a):
    offs = np.concatenate([[0], np.cumsum(IN_SIZES)])
    seg = [w_in[:, offs[i]:offs[i + 1]] for i in range(len(IN_SIZES))]
    w_cq, w_ckv, w_kr, w_gq, w_gk, w_gv, w_gr, w_af, w_ab, w_hy, w_gate = seg
    d = w_in.shape[0]
    z = lambda n: jnp.zeros((d, n), w_in.dtype)
    kr_tile = jnp.concatenate([z(MLA_NOPE), w_kr, z(HEAD_SLOT - MLA_NOPE - MLA_ROPE)], axis=1)
    krp_tile = jnp.concatenate([z(MLA_NOPE), _rope_partner(w_kr), z(HEAD_SLOT - MLA_NOPE - MLA_ROPE)], axis=1)
    a_tile = jnp.concatenate([w_af, w_ab, z(LANES - 2 * GLA_GATE_RANK)], axis=1)
    wa = jnp.concatenate([w_cq, w_ckv, kr_tile, krp_tile, a_tile], axis=1)
    wg = jnp.concatenate([w_gq, w_gk, w_gv, w_gr], axis=1)

    dh = MLA_NOPE + MLA_ROPE
    zq = lambda n: jnp.zeros((MLA_Q_LORA, n), w_in.dtype)
    plain, partner = [], []
    for hd in range(MLA_HEADS):
        blk = mla_w_uq[:, hd * dh:(hd + 1) * dh]
        plain += [blk, zq(HEAD_SLOT - dh)]
        partner += [zq(MLA_NOPE), _rope_partner(blk[:, MLA_NOPE:]), zq(HEAD_SLOT - dh)]
    wuq = jnp.concatenate(plain + partner, axis=1)

    zk = jnp.zeros((MLA_KV_LORA, HEAD_SLOT - MLA_NOPE), w_in.dtype)
    kcols, vcols = [], []
    for hd in range(MLA_HEADS):
        blk = mla_w_ukv[:, hd * (MLA_NOPE + MLA_V):(hd + 1) * (MLA_NOPE + MLA_V)]
        kcols += [blk[:, :MLA_NOPE], zk]
        vcols += [blk[:, MLA_NOPE:], zk]
    wukv = jnp.concatenate(kcols + vcols, axis=1)

    wa2 = jnp.zeros((LANES, 2 * GLA_DK), w_in.dtype)
    wa2 = wa2.at[0:GLA_GATE_RANK, 0:GLA_DK].set(gla_w_a2[0])
    wa2 = wa2.at[GLA_GATE_RANK:2 * GLA_GATE_RANK, GLA_DK:].set(gla_w_a2[1])
    ba = jnp.concatenate([gla_b_a[0], gla_b_a[1]])[None, :]
    bf = lambda t: t.astype(BF16)
    return bf(wa), bf(wg), bf(w_hy), bf(w_gate), bf(wuq), bf(wukv), bf(wa2), ba


def _rope_tables(seq, ctx_len):
    rows = seq // GRID_W
    row = np.repeat(np.arange(rows, dtype=np.float64), GRID_W)
    col = np.tile(np.arange(GRID_W, dtype=np.float64), rows)
    a = MLA_ROPE // 4
    inv = ROPE_BASE ** (-np.arange(a, dtype=np.float64) / a)
    ang_r = row[:, None] * inv
    ang_c = col[:, None] * inv
    cos32 = np.concatenate([np.cos(ang_r), np.cos(ang_r), np.cos(ang_c), np.cos(ang_c)], axis=1)
    sin32 = np.concatenate([np.sin(ang_r), np.sin(ang_r), np.sin(ang_c), np.sin(ang_c)], axis=1)
    pad_r = HEAD_SLOT - MLA_NOPE - MLA_ROPE
    cos = np.concatenate([np.ones((seq, MLA_NOPE)), cos32, np.zeros((seq, pad_r))], axis=1)
    sin = np.concatenate([np.zeros((seq, MLA_NOPE)), sin32, np.zeros((seq, pad_r))], axis=1)
    cos_c = np.concatenate([np.ones((ctx_len, MLA_NOPE + MLA_ROPE)), np.zeros((ctx_len, pad_r))], axis=1)
    sin_c = np.zeros((ctx_len, HEAD_SLOT))
    return (jnp.asarray(np.concatenate([cos, cos_c], axis=0), F32),
            jnp.asarray(np.concatenate([sin, sin_c], axis=0), F32))


def _inproj(xc, mod_l, g1, weights, q_norm, kv_norm, cos, sin, n_lat_tiles):
    bsz, t, d = xc.shape
    nt = t // TM
    wa, wg, wh, wz, wuq, wukv, wa2, ba = weights
    tile = lambda w: pl.BlockSpec((1, TM, w), lambda b, i: (b, i, 0))
    row = lambda w: pl.BlockSpec((1, w), lambda b, i: (0, 0))
    mod_spec = pl.BlockSpec((1, 6, d), lambda b, i: (jnp.where(i < n_lat_tiles, b, bsz), 0, 0))
    tab = pl.BlockSpec((TM, HEAD_SLOT), lambda b, i: (i, 0))
    nq = MLA_HEADS * HEAD_SLOT
    sds = lambda w, dt: jax.ShapeDtypeStruct((bsz, t, w), dt)
    out_shape = (sds(nq, BF16), sds(nq, BF16), sds(nq, BF16), sds(GLA_DK, BF16), sds(GLA_DK, BF16),
                 sds(GLA_DV, BF16), sds(GLA_DV, BF16), sds(2 * GLA_DK, F32), sds(3 * HY_WIDTH, F32),
                 sds(3 * D_MODEL, BF16))
    out_specs = (tile(nq), tile(nq), tile(nq), tile(GLA_DK), tile(GLA_DK), tile(GLA_DV), tile(GLA_DV),
                 tile(2 * GLA_DK), tile(3 * HY_WIDTH), tile(3 * D_MODEL))
    return pl.pallas_call(
        _inproj_kernel,
        out_shape=out_shape,
        grid=(bsz, nt),
        in_specs=[tile(d), mod_spec, row(d), _const_spec(wa.shape), _const_spec(wg.shape), _const_spec(wh.shape),
                  _const_spec(wz.shape), row(MLA_Q_LORA), row(MLA_KV_LORA), _const_spec(wuq.shape),
                  _const_spec(wukv.shape), _const_spec(wa2.shape), row(2 * GLA_DK), tab, tab],
        out_specs=out_specs,
        compiler_params=_cparams(2, VMEM_LIMIT),
        name="inproj",
    )(xc, mod_l, g1[None, :], wa, wg, wh, wz, q_norm[None, :], kv_norm[None, :], wuq, wukv, wa2, ba, cos, sin)


ATT_TK = 1024


ATT_TQ = 1024


def _attn_kernel(*refs, chunks, aliased):
    if aliased:
        q_ref, k_ref, v_ref, _, o_ref, m_ref, acc_ref = refs
    else:
        q_ref, k_ref, v_ref, o_ref, m_ref, acc_ref = refs
    tq = q_ref.shape[1]
    m_ref[...] = jnp.full(m_ref.shape, -jnp.inf, F32)
    acc_ref[...] = jnp.zeros(acc_ref.shape, F32)
    for r0, size in chunks:
        k = k_ref[0, pl.ds(r0, size), :]
        v = v_ref[0, pl.ds(r0, size), :]
        for hd in range(2):
            sl = slice(hd * HEAD_SLOT, (hd + 1) * HEAD_SLOT)
            s = _dot_nt(q_ref[0, :, sl], k[:, sl])
            m_prev = m_ref[hd]
            m_new = jnp.maximum(m_prev, jnp.max(s, axis=1, keepdims=True))
            p = jnp.exp2((s - jnp.concatenate([m_new] * (size // LANES), axis=1)).astype(BF16))
            acc_ref[hd] = jnp.exp2(m_prev - m_new) * acc_ref[hd] + _dot(p, v[:, sl])
            m_ref[hd] = m_new
    a0 = acc_ref[0]
    a1 = acc_ref[1]
    lane = lax.broadcasted_iota(jnp.int32, (tq, HEAD_SLOT), 1)
    o0 = a0 / pltpu.roll(a0, MLA_V, 1)
    o1 = pltpu.roll(a1, MLA_V, 1) / a1
    o_ref[0] = jnp.where(lane < MLA_V, o0, o1).astype(o_ref.dtype)


def _attention(q, k, v, seq, with_ctx_queries):
    bsz, t, _ = q.shape
    ctx_len = t - seq
    assert seq % ATT_TK == 0 and seq % ATT_TQ == 0 and seq % ctx_len == 0 and 2 * MLA_V == HEAD_SLOT
    pair = 2 * HEAD_SLOT
    n_chunks = seq // ATT_TK
    chunks = tuple((j * ATT_TK, ATT_TK) for j in range(n_chunks - 1))
    chunks += (((n_chunks - 1) * ATT_TK, ATT_TK + ctx_len),)
    scratch = lambda tq: [pltpu.VMEM((2, tq, LANES), F32), pltpu.VMEM((2, tq, HEAD_SLOT), F32)]
    y = pl.pallas_call(
        functools.partial(_attn_kernel, chunks=chunks, aliased=False),
        out_shape=jax.ShapeDtypeStruct((bsz, t, MLA_OUT), BF16),
        grid=(bsz, MLA_HEADS // 2, seq // ATT_TQ),
        in_specs=[
            pl.BlockSpec((1, ATT_TQ, pair), lambda b, hp, i: (b, i, hp)),
            pl.BlockSpec((1, t, pair), lambda b, hp, i: (b, 0, hp)),
            pl.BlockSpec((1, t, pair), lambda b, hp, i: (b, 0, hp)),
        ],
        out_specs=pl.BlockSpec((1, ATT_TQ, HEAD_SLOT), lambda b, hp, i: (b, i, hp)),
        scratch_shapes=scratch(ATT_TQ),
        compiler_params=_cparams(3, VMEM_LIMIT),
        name="mla_attention",
    )(q, k, v)
    if not with_ctx_queries:
        return y
    cblk = seq // ctx_len
    ctx_rows = lambda w: pl.BlockSpec((1, ctx_len, w), lambda b, hp: (b, cblk, hp))
    return pl.pallas_call(
        functools.partial(_attn_kernel, chunks=((0, ctx_len),), aliased=True),
        out_shape=jax.ShapeDtypeStruct((bsz, t, MLA_OUT), BF16),
        grid=(bsz, MLA_HEADS // 2),
        in_specs=[ctx_rows(pair), ctx_rows(pair), ctx_rows(pair), pl.BlockSpec(memory_space=pl.ANY)],
        out_specs=ctx_rows(HEAD_SLOT),
        scratch_shapes=scratch(ctx_len),
        input_output_aliases={3: 0},
        compiler_params=_cparams(2),
        name="mla_attention_ctx",
    )(q, k, v, y)


GLA_LEVELS = int(math.log2(TM))


def _gla_level_matrices():
    i = np.arange(TM)[:, None]
    t = np.arange(TM)[None, :]
    fwd = [(t <= i)]
    for lv in range(GLA_LEVELS):
        m = TM >> (lv + 1)
        lo = (i // m) * m
        later = ((i // m) % 2) == 1
        q_part = later & (t >= lo) & (t <= i)
        k_part = (~later) & (t > i) & (t <= lo + m - 1)
        fwd.append(q_part | k_part)
    fwd = np.concatenate(fwd, axis=0).astype(np.float32)
    nb = 1 + GLA_LEVELS
    bwd = fwd.reshape(nb, TM, TM)[:, ::-1, ::-1].reshape(nb * TM, TM)
    return np.stack([fwd, bwd])


def _gla_kernel(q_ref, k_ref, v_ref, g_ref, mall_ref, o_ref, s_ref, a_ref):
    d = pl.program_id(1)
    step = pl.program_id(2)

    @pl.when(step == 0)
    def _():
        s_ref[...] = jnp.zeros(s_ref.shape, F32)

    q = q_ref[0].astype(F32)
    k = k_ref[0].astype(F32)
    v = v_ref[0]
    g = g_ref[0]
    g_hi = g.astype(BF16)
    g_lo = (g - g_hi.astype(F32)).astype(BF16)
    e2 = _dot(mall_ref[0], jnp.concatenate([g_hi, g_lo], axis=1))
    e_all = e2[:, :GLA_DK] + e2[:, GLA_DK:]
    g_cum = e_all[0:TM]
    g_tot = jnp.sum(g, axis=0, keepdims=True)

    row = lax.broadcasted_iota(jnp.int32, (TM, TM), 0)
    col = lax.broadcasted_iota(jnp.int32, (TM, TM), 1)
    lane_head = lax.broadcasted_iota(jnp.int32, (TM, GLA_DK), 1) // GLA_HK
    tok = lax.broadcasted_iota(jnp.int32, (TM, GLA_DK), 0)

    def stack_heads(t):
        return jnp.concatenate([jnp.where(lane_head == hd, t, 0.0) for hd in range(GLA_HEADS)], axis=0).astype(BF16)

    res = _dot_nt(stack_heads(q), k.astype(BF16))
    eye = row == col
    for hd in range(GLA_HEADS):
        a_ref[hd] = jnp.where(eye, res[hd * TM:(hd + 1) * TM], 0.0)
    for lv in range(GLA_LEVELS):
        m = TM >> (lv + 1)
        w = jnp.exp(e_all[(lv + 1) * TM:(lv + 2) * TM])
        bit = (tok // m) % 2
        q_act = bit != d
        qt = jnp.where(q_act, q * w, 0.0)
        kt = jnp.where(q_act, 0.0, k * w).astype(BF16)
        res = _dot_nt(stack_heads(qt), kt)
        if m == TM // 2:
            for hd in range(GLA_HEADS):
                a_ref[hd] += res[hd * TM:(hd + 1) * TM]
        else:
            same = (row // (2 * m)) == (col // (2 * m))
            for hd in range(GLA_HEADS):
                a_ref[hd] += jnp.where(same, res[hd * TM:(hd + 1) * TM], 0.0)

    s_old = s_ref[...]
    s_bf = s_old.astype(BF16)
    qg = q * jnp.exp(g_cum)
    for hd in range(GLA_HEADS):
        vh = v[:, hd * GLA_HV:(hd + 1) * GLA_HV]
        o_intra = _dot(a_ref[hd].astype(BF16), vh)
        o_inter = _dot(jnp.where(lane_head == hd, qg, 0.0).astype(BF16), s_bf)
        o_ref[0, 0, :, hd * GLA_HV:(hd + 1) * GLA_HV] = (o_intra + o_inter).astype(o_ref.dtype)

    kdec_t = (k * jnp.exp(g_tot - g_cum)).T.astype(BF16)
    upd = _dot(kdec_t, v)
    row_head = lax.broadcasted_iota(jnp.int32, (GLA_DK, GLA_HV), 0) // GLA_HK
    new = jnp.zeros((GLA_DK, GLA_HV), F32)
    for hd in range(GLA_HEADS):
        new = new + jnp.where(row_head == hd, upd[:, hd * GLA_HV:(hd + 1) * GLA_HV], 0.0)
    dec_col = jnp.sum(jnp.where(eye, jnp.broadcast_to(jnp.exp(g_tot), (TM, GLA_DK)), 0.0), axis=1, keepdims=True)
    s_ref[...] = dec_col * s_old + new


def _gla(gq, gk, gv, glog, n_lat_tiles):
    assert GLA_DK == TM
    bsz, t, _ = gq.shape
    nt = t // TM
    mall = jnp.asarray(_gla_level_matrices(), dtype=BF16)

    def tile_idx(d, s):
        return jnp.where(s == 0, n_lat_tiles, jnp.where(d == 0, s - 1, n_lat_tiles - s))

    tok = lambda w: pl.BlockSpec((1, TM, w), lambda b, d, s: (b, tile_idx(d, s), 0))
    return pl.pallas_call(
        _gla_kernel,
        out_shape=jax.ShapeDtypeStruct((bsz, 2, t, GLA_DV), BF16),
        grid=(bsz, 2, nt),
        in_specs=[tok(GLA_DK), tok(GLA_DK), tok(GLA_DV),
                  pl.BlockSpec((1, TM, GLA_DK), lambda b, d, s: (b, tile_idx(d, s), d)),
                  pl.BlockSpec((1,) + mall.shape[1:], lambda b, d, s: (d, 0, 0))],
        out_specs=pl.BlockSpec((1, 1, TM, GLA_DV), lambda b, d, s: (b, d, tile_idx(d, s), 0)),
        scratch_shapes=[pltpu.VMEM((GLA_DK, GLA_HV), F32), pltpu.VMEM((GLA_HEADS, TM, TM), F32)],
        compiler_params=_cparams(3),
        name="gla_scan",
    )(gq, gk, gv, glog, mall)


HY_N1 = 64
HY_N2 = 128
HY_SLAB = SUBLANES
HY_CB = 256
HY_KG = 16
HY_SPEC_CB = 128


def _hyena_dft_constants(seq):
    n = 2 * seq
    assert n == HY_N1 * HY_N2
    nh = HY_N1 // 2
    kh = HY_N2 // 2
    eye = np.eye(HY_SLAB)
    k1 = np.arange(HY_N1)
    th = 2 * np.pi * np.outer(k1 + 0.5, np.arange(nh)) / HY_N1
    fwd1 = np.concatenate([np.kron(np.cos(th), eye), np.kron(-np.sin(th), eye)], axis=0)
    inv1 = (2.0 / n) * np.concatenate([np.kron(np.cos(th).T, eye), np.kron(-np.sin(th).T, eye)], axis=1)
    nlo = np.arange(HY_N2)
    k2 = np.arange(kh)
    ph = 2 * np.pi * (k2[None, :, None] * nlo[None, None, :] / HY_N2
                      + (k1[:, None, None] + 0.5) * nlo[None, None, :] / n)
    c, s = np.cos(ph), np.sin(ph)
    fwd2 = np.concatenate([np.concatenate([c, s], axis=2), np.concatenate([-s, c], axis=2)], axis=1)
    ct, st = c.transpose(0, 2, 1), s.transpose(0, 2, 1)
    inv2 = np.concatenate([np.concatenate([ct, -st], axis=2), np.concatenate([st, ct], axis=2)], axis=1)
    return fwd1, fwd2, inv2, inv1


def _hyena_ctx_dft_constants(ctx_len):
    n = 2 * ctx_len
    th = 2 * np.pi * np.outer(np.arange(ctx_len) + 0.5, np.arange(ctx_len)) / n
    fwd = np.concatenate([np.cos(th), -np.sin(th)], axis=0)
    inv = (2.0 / n) * np.concatenate([np.cos(th).T, -np.sin(th).T], axis=1)
    return fwd, inv


def _hyena_features(length):
    pos = np.arange(length, dtype=np.float64)
    t = pos / max(length - 1, 1)
    f = np.linspace(1e-4, HY_BANDS - 1, HY_BANDS)
    ang = (2.0 * math.pi / length) * pos[:, None] * f
    feat = np.concatenate([t[:, None], np.cos(ang), np.sin(ang)], axis=-1)
    return jnp.asarray(np.pad(feat, ((0, 0), (0, LANES - HY_POS_DIM))), F32)


def _hy_filter_kernel(feat_ref, w1_ref, b1_ref, w2_ref, b2_ref, w3_ref, b3_ref, absd_ref, h_ref, s_ref):
    i = pl.program_id(0)
    feat = feat_ref[...]
    hp = lambda a, b: jnp.dot(a, b, preferred_element_type=F32, precision=HIGHEST)
    hdn = jnp.sin(hp(feat, w1_ref[...]) + b1_ref[...])
    hdn = jnp.sin(hp(hdn, w2_ref[...]) + b2_ref[...])
    h = hp(hdn, w3_ref[...]) + b3_ref[...]
    window = jnp.exp(-feat[:, 0:1] * absd_ref[...])
    h = h * jnp.concatenate([window] * (2 * HY_ORDER), axis=1)
    h_ref[...] = h

    @pl.when(i == 0)
    def _():
        s_ref[...] = jnp.zeros(s_ref.shape, F32)

    s_ref[...] += jnp.sum(jnp.abs(h), axis=0, keepdims=True)


def _hyena_filters_raw(length, filt_w):
    w1, b1, w2, b2, w3, b3 = filt_w
    nf = 2 * HY_ORDER * HY_WIDTH
    tr = min(length, 512)
    deltas = np.linspace(math.log(HY_DECAY_TARGET) / HY_FAST_DECAY, math.log(HY_DECAY_TARGET) / HY_SLOW_DECAY,
                         HY_WIDTH, dtype=np.float32)
    absd = jnp.asarray(np.abs(deltas))[None, :]
    w1p = jnp.pad(w1, ((0, LANES - HY_POS_DIM), (0, 0)))
    full = lambda shp: pl.BlockSpec(shp, lambda i: (0,) * len(shp))
    return pl.pallas_call(
        _hy_filter_kernel,
        out_shape=(jax.ShapeDtypeStruct((length, nf), F32), jax.ShapeDtypeStruct((1, nf), F32)),
        grid=(length // tr,),
        in_specs=[pl.BlockSpec((tr, LANES), lambda i: (i, 0)), full((LANES, HY_FILTER_HIDDEN)),
                  full((1, HY_FILTER_HIDDEN)), full((HY_FILTER_HIDDEN, HY_FILTER_HIDDEN)), full((1, HY_FILTER_HIDDEN)),
                  full((HY_FILTER_HIDDEN, nf)), full((1, nf)), full((1, HY_WIDTH))],
        out_specs=(pl.BlockSpec((tr, nf), lambda i: (i, 0)), full((1, nf))),
        compiler_params=_cparams(1),
        name="hyena_filter_mlp",
    )(_hyena_features(length), w1p, b1[None, :], w2, b2[None, :], w3, b3[None, :], absd)


def _split_bf16(m):
    hi = m.astype(BF16)
    return jnp.asarray(np.stack([hi, (m - hi.astype(np.float64)).astype(BF16)]))


def _dot_split(m_hi, m_lo, x):
    x_hi = x.astype(BF16)
    x_lo = (x - x_hi.astype(F32)).astype(BF16)
    n = x.shape[1]
    r = _dot(m_hi, jnp.concatenate([x_hi, x_lo], axis=1))
    return r[:, :n] + r[:, n:] + _dot(m_lo, x_hi)


def _odft_stage1(src_at, mm, s_re, s_im):
    nk = s_re.shape[0]
    half = nk * HY_SLAB

    def body(j, carry):
        r0 = pl.multiple_of(j * HY_SLAB, HY_SLAB)
        slab = src_at(r0)
        cb = slab.shape[-1]
        res = mm(slab.reshape(-1, cb))
        s_re[:, pl.ds(r0, HY_SLAB), :] = res[:half].reshape(nk, HY_SLAB, cb)
        s_im[:, pl.ds(r0, HY_SLAB), :] = res[half:].reshape(nk, HY_SLAB, cb)
        return carry

    lax.fori_loop(0, HY_N2 // HY_SLAB, body, 0, unroll=2)


def _hy_spectrum_kernel(hf_ref, hb_ref, sf_ref, sb_ref, fwd1_ref, fwd2_ref, o_ref, s_re, s_im):
    kh = HY_N2 // 2
    mm1 = lambda x: _dot_split(fwd1_ref[0], fwd1_ref[1], x)

    def middle(sign):
        def body(k1, carry):
            a = jnp.concatenate([s_re[k1], s_im[k1]], axis=0)
            x = _dot_split(fwd2_ref[0, k1], fwd2_ref[1, k1], a)
            if sign is None:
                o_ref[0, k1] = x[:kh]
                o_ref[1, k1] = x[kh:]
            else:
                inv_norm = 1.0 / (sf_ref[...] + sb_ref[...])
                o_ref[0, k1] = (o_ref[0, k1] + x[:kh]) * inv_norm
                o_ref[1, k1] = (o_ref[1, k1] - x[kh:]) * inv_norm
            return carry
        lax.fori_loop(0, HY_N1, body, 0, unroll=2)

    _odft_stage1(lambda r0: hf_ref[:, pl.ds(r0, HY_SLAB), :], mm1, s_re, s_im)
    middle(None)

    def bwd_slab(r0):
        slab = hb_ref[:, pl.ds(r0, HY_SLAB), :]
        nhi = lax.broadcasted_iota(jnp.int32, slab.shape, 0)
        r = lax.broadcasted_iota(jnp.int32, slab.shape, 1)
        return jnp.where((nhi == 0) & (r + r0 == 0), 0.0, slab)

    _odft_stage1(bwd_slab, mm1, s_re, s_im)
    middle(-1)


def _hyena_filter_spectrum(seq, w1, b1, w2, b2, w3, b3):
    h_raw, s = _hyena_filters_raw(seq, (w1, b1, w2, b2, w3, b3))
    nh = HY_N1 // 2
    nc = HY_ORDER * HY_WIDTH
    h3 = h_raw.reshape(nh, HY_N2, 2 * nc)
    fwd1, fwd2, _, _ = _hyena_dft_constants(seq)
    fwd1 = _split_bf16(fwd1)
    fwd2 = _split_bf16(fwd2)
    scb = HY_SPEC_CB
    ncb = nc // scb
    return pl.pallas_call(
        _hy_spectrum_kernel,
        out_shape=jax.ShapeDtypeStruct((2, HY_N1, HY_N2 // 2, nc), F32),
        grid=(ncb,),
        in_specs=[pl.BlockSpec((nh, HY_N2, scb), lambda c: (0, 0, c)),
                  pl.BlockSpec((nh, HY_N2, scb), lambda c: (0, 0, ncb + c)),
                  pl.BlockSpec((1, scb), lambda c: (0, c)),
                  pl.BlockSpec((1, scb), lambda c: (0, ncb + c)),
                  _const_spec(fwd1.shape), _const_spec(fwd2.shape)],
        out_specs=pl.BlockSpec((2, HY_N1, HY_N2 // 2, scb), lambda c: (0, 0, 0, c)),
        scratch_shapes=[pltpu.VMEM((HY_N1, HY_N2, scb), F32), pltpu.VMEM((HY_N1, HY_N2, scb), F32)],
        compiler_params=_cparams(1, VMEM_LIMIT),
        name="hyena_filter_spectrum",
    )(h3, h3, s, s, fwd1, fwd2)


def _short_conv_chunk(ref, c, n_chunks, w_ref, b_ref):
    per = TM // HY_N2
    cur = ref[0, pl.ds(per * c, per)]
    cb = cur.shape[-1]
    cur = cur.reshape(TM, cb)
    prev = ref[0, jnp.maximum(per * c - 1, 0), pl.ds(HY_N2 - SUBLANES, SUBLANES), :][SUBLANES - 1:SUBLANES]
    nxt = ref[0, jnp.minimum(per * c + per, per * n_chunks - 1), pl.ds(0, SUBLANES), :][0:1]
    prev = jnp.where(c > 0, prev, 0.0)
    nxt = jnp.where(c < n_chunks - 1, nxt, 0.0)
    rowi = lax.broadcasted_iota(jnp.int32, (TM, cb), 0)
    dn = jnp.where(rowi == 0, prev, pltpu.roll(cur, 1, 0))
    up = jnp.where(rowi == TM - 1, nxt, pltpu.roll(cur, TM - 1, 0))
    return b_ref[...] + w_ref[0:1, :] * dn + w_ref[1:2, :] * cur + w_ref[2:3, :] * up


def _hy_conv_kernel(*refs, conv_y):
    if conv_y:
        (y_ref, g_ref, h_ref, bias_ref, wy_ref, by_ref, wg_ref, bg_ref, fwd1_ref, fwd2_ref, inv2_ref, inv1_ref,
         o_ref, s_re, s_im, gs_ref, us_ref) = refs
    else:
        (y_ref, g_ref, h_ref, bias_ref, wg_ref, bg_ref, fwd1_ref, fwd2_ref, inv2_ref, inv1_ref,
         o_ref, s_re, s_im, gs_ref) = refs
    grp = pl.program_id(2)
    last = pl.num_programs(2) - 1
    nh = HY_N1 // 2
    per = TM // HY_N2
    n_chunks = nh // per
    cb = o_ref.shape[-1]
    kh = HY_N2 // 2

    @pl.when(grp == 0)
    def _():
        def pre(c, carry):
            gs_ref[pl.ds(per * c, per)] = _short_conv_chunk(g_ref, c, n_chunks, wg_ref, bg_ref).reshape(per, HY_N2, cb)
            if conv_y:
                us_ref[pl.ds(per * c, per)] = _short_conv_chunk(y_ref, c, n_chunks, wy_ref, by_ref).reshape(
                    per, HY_N2, cb)
            return carry
        lax.fori_loop(0, n_chunks, pre, 0)

    if conv_y:
        u_at = lambda r0: us_ref[:, pl.ds(r0, HY_SLAB), :]
    else:
        u_at = lambda r0: y_ref[0, :, pl.ds(r0, HY_SLAB), :]

    _odft_stage1(u_at, lambda x: _dot(fwd1_ref[0], x.astype(BF16)), s_re, s_im)

    def middle(k1, carry):
        a = jnp.concatenate([s_re[k1], s_im[k1]], axis=0).astype(BF16)
        x = _dot(fwd2_ref[k1], a)
        xr, xi = x[:kh], x[kh:]
        hr, hi = h_ref[0, k1], h_ref[1, k1]
        y = jnp.concatenate([xr * hr - xi * hi, xr * hi + xi * hr], axis=0).astype(BF16)
        bm = _dot(inv2_ref[k1], y)
        s_re[k1] = bm[:HY_N2]
        s_im[k1] = bm[HY_N2:]
        return carry

    lax.fori_loop(0, HY_KG, middle, 0, unroll=4)

    def partial_conv(r0):
        slab = jnp.concatenate([s_re[:, pl.ds(r0, HY_SLAB), :].reshape(HY_KG * HY_SLAB, cb),
                                s_im[:, pl.ds(r0, HY_SLAB), :].reshape(HY_KG * HY_SLAB, cb)], axis=0).astype(BF16)
        return _dot(inv1_ref[0], slab).reshape(nh, HY_SLAB, cb)

    def post_loop(fn):
        def post(j, carry):
            r0 = pl.multiple_of(j * HY_SLAB, HY_SLAB)
            o_ref[0, :, pl.ds(r0, HY_SLAB), :] = fn(r0, partial_conv(r0))
            return carry
        lax.fori_loop(0, HY_N2 // HY_SLAB, post, 0, unroll=2)

    @pl.when(grp == 0)
    def _():
        post_loop(lambda r0, part: part)

    @pl.when((grp > 0) & (grp < last))
    def _():
        post_loop(lambda r0, part: o_ref[0, :, pl.ds(r0, HY_SLAB), :] + part)

    @pl.when(grp == last)
    def _():
        post_loop(lambda r0, part: gs_ref[:, pl.ds(r0, HY_SLAB), :]
                  * (o_ref[0, :, pl.ds(r0, HY_SLAB), :] + part + bias_ref[...] * u_at(r0)))


def _hyena_order(y4, y_col0, z4, gate_col0, hspec, order, hy_bias, short_w, short_b, consts, conv_y, rows_out):
    bsz = z4.shape[0]
    nh = HY_N1 // 2
    ncb = HY_WIDTH // HY_CB
    ngrp = HY_N1 // HY_KG
    fwd1, fwd2, inv2, inv1 = consts
    blk4 = lambda off: pl.BlockSpec((1, nh, HY_N2, HY_CB), lambda c, b, g: (b, 0, 0, off + c))
    rowspec = lambda rows, off: pl.BlockSpec((rows, HY_CB), lambda c, b, g: (0, off + c))
    grouped = lambda shp: pl.BlockSpec(shp, lambda c, b, g: (g, 0, 0))
    in_specs = [blk4(y_col0), blk4(gate_col0),
                pl.BlockSpec((2, HY_KG, HY_N2 // 2, HY_CB), lambda c, b, g: (0, g, 0, order * ncb + c)),
                rowspec(1, 0)]
    args = [y4, z4, hspec, hy_bias.reshape(1, -1)]
    if conv_y:
        in_specs += [rowspec(HY_SHORT, y_col0), rowspec(1, y_col0)]
        args += [short_w, short_b[None, :]]
    in_specs += [rowspec(HY_SHORT, gate_col0), rowspec(1, gate_col0)]
    args += [short_w, short_b[None, :]]
    in_specs += [grouped((1,) + fwd1.shape[1:]), grouped((HY_KG,) + fwd2.shape[1:]),
                 grouped((HY_KG,) + inv2.shape[1:]), grouped((1,) + inv1.shape[1:])]
    args += [fwd1, fwd2, inv2, inv1]
    scratch = [pltpu.VMEM((HY_KG, HY_N2, HY_CB), F32), pltpu.VMEM((HY_KG, HY_N2, HY_CB), F32),
               pltpu.VMEM((nh, HY_N2, HY_CB), F32)]
    if conv_y:
        scratch.append(pltpu.VMEM((nh, HY_N2, HY_CB), F32))
    return pl.pallas_call(
        functools.partial(_hy_conv_kernel, conv_y=conv_y),
        out_shape=jax.ShapeDtypeStruct((bsz, rows_out, HY_N2, HY_WIDTH), F32),
        grid=(ncb, bsz, ngrp),
        in_specs=in_specs,
        out_specs=pl.BlockSpec((1, nh, HY_N2, HY_CB), lambda c, b, g: (b, 0, 0, c)),
        scratch_shapes=scratch,
        compiler_params=_cparams(3, VMEM_LIMIT),
        name="hyena_conv%d" % order,
    )(*args)


def _hyena_latent(z_hy, hspec, short_w, short_b, hy_bias, with_ctx_rows):
    bsz, t, _ = z_hy.shape
    seq = HY_N1 * HY_N2 // 2
    fwd1, fwd2, inv2, inv1 = _hyena_dft_constants(seq)
    ngrp = HY_N1 // HY_KG
    rows = HY_KG * HY_SLAB
    fwd1 = fwd1.reshape(2, ngrp, rows, -1).transpose(1, 0, 2, 3).reshape(ngrp, 2 * rows, -1)
    inv1 = inv1.reshape(-1, 2, ngrp, rows).transpose(2, 0, 1, 3).reshape(ngrp, -1, 2 * rows)
    consts = tuple(jnp.asarray(m, BF16) for m in (fwd1, fwd2, inv2, inv1))
    z4 = z_hy.reshape(bsz, t // HY_N2, HY_N2, 3 * HY_WIDTH)
    ncb = HY_WIDTH // HY_CB
    nh = seq // HY_N2
    y1 = _hyena_order(z4, 2 * ncb, z4, 0, hspec, 0, hy_bias[0], short_w, short_b, consts, True, nh)
    rows_out = t // HY_N2 if with_ctx_rows else nh
    y2 = _hyena_order(y1, 0, z4, ncb, hspec, 1, hy_bias[1], short_w, short_b, consts, False, rows_out)
    return y2.reshape(bsz, rows_out * HY_N2, HY_WIDTH)


def _hy_ctx_spectrum_kernel(h_ref, s_ref, fwd_ref, o_ref):
    lc = h_ref.shape[0]
    nc = HY_ORDER * HY_WIDTH
    hp = lambda a, b: jnp.dot(a, b, preferred_element_type=F32, precision=HIGHEST)
    h = h_ref[...]
    rowi = lax.broadcasted_iota(jnp.int32, (lc, nc), 0)
    xf = hp(fwd_ref[...], h[:, :nc])
    xb = hp(fwd_ref[...], jnp.where(rowi == 0, 0.0, h[:, nc:]))
    inv_norm = 1.0 / (s_ref[:, :nc] + s_ref[:, nc:])
    o_ref[0] = (xf[:lc] + xb[:lc]) * inv_norm
    o_ref[1] = (xf[lc:] - xb[lc:]) * inv_norm


def _hy_ctx_conv_kernel(x1_ref, x2_ref, v_ref, h_ref, bias_ref, w_ref, b_ref, fwd_ref, inv_ref, _, o_ref):
    lc = o_ref.shape[1]
    rowi = lax.broadcasted_iota(jnp.int32, (lc, HY_WIDTH), 0)

    def short(ref, part):
        cur = ref[0].reshape(lc, HY_WIDTH)
        sl = slice(part * HY_WIDTH, (part + 1) * HY_WIDTH)
        dn = jnp.where(rowi == 0, 0.0, pltpu.roll(cur, 1, 0))
        up = jnp.where(rowi == lc - 1, 0.0, pltpu.roll(cur, lc - 1, 0))
        return b_ref[:, sl] + w_ref[0:1, sl] * dn + w_ref[1:2, sl] * cur + w_ref[2:3, sl] * up

    y = short(v_ref, 2)
    for order, gref in enumerate((x1_ref, x2_ref)):
        sl = slice(order * HY_WIDTH, (order + 1) * HY_WIDTH)
        x = _dot(fwd_ref[...], y.astype(BF16))
        xr, xi = x[:lc], x[lc:]
        hr, hi = h_ref[0, :, sl], h_ref[1, :, sl]
        prod = jnp.concatenate([xr * hr - xi * hi, xr * hi + xi * hr], axis=0).astype(BF16)
        conv = _dot(inv_ref[...], prod)
        y = short(gref, order) * (conv + bias_ref[order:order + 1, :] * y)
    o_ref[0] = y


def _hyena_ctx(z_hy, y_lat, filt_w, short_w, short_b, hy_bias):
    bsz, t, _ = z_hy.shape
    seq = HY_N1 * HY_N2 // 2
    lc = t - seq
    per = lc // HY_N2
    h_raw, s = _hyena_filters_raw(lc, filt_w)
    fwd, inv = _hyena_ctx_dft_constants(lc)
    nc = HY_ORDER * HY_WIDTH
    full = lambda shp: pl.BlockSpec(shp, lambda *_: (0,) * len(shp))
    hspec = pl.pallas_call(
        _hy_ctx_spectrum_kernel,
        out_shape=jax.ShapeDtypeStruct((2, lc, nc), F32),
        grid=(1,),
        in_specs=[full(h_raw.shape), full(s.shape), full(fwd.shape)],
        out_specs=full((2, lc, nc)),
        compiler_params=_cparams(1),
        name="hyena_ctx_spectrum",
    )(h_raw, s, jnp.asarray(fwd, F32))
    z4 = z_hy.reshape(bsz, t // HY_N2, HY_N2, 3 * HY_WIDTH)
    blk = lambda part: pl.BlockSpec((1, per, HY_N2, HY_WIDTH), lambda b: (b, seq // lc, 0, part))
    return pl.pallas_call(
        _hy_ctx_conv_kernel,
        out_shape=jax.ShapeDtypeStruct((bsz, t, HY_WIDTH), F32),
        grid=(bsz,),
        in_specs=[blk(0), blk(1), blk(2), full((2, lc, nc)), full((HY_ORDER, HY_WIDTH)),
                  full((HY_SHORT, 3 * HY_WIDTH)), full((1, 3 * HY_WIDTH)), full(fwd.shape), full(inv.shape),
                  pl.BlockSpec(memory_space=pl.ANY)],
        out_specs=pl.BlockSpec((1, lc, HY_WIDTH), lambda b: (b, seq // lc, 0)),
        input_output_aliases={9: 0},
        compiler_params=_cparams(1),
        name="hyena_ctx_conv",
    )(z4, z4, z4, hspec, hy_bias, short_w, short_b[None, :], jnp.asarray(fwd, BF16), jnp.asarray(inv, BF16), y_lat)


def _merge_kernel(x_ref, mod_ref, ymla_ref, of_ref, ob_ref, gr_ref, yhy_ref, zg_ref, on_ref,
                  wm_ref, wgl_ref, wh_ref, wo_ref, o_ref):
    o = of_ref[0, 0].astype(F32) + ob_ref[0, 0].astype(F32)
    silu = gr_ref[0].astype(F32)
    parts = []
    for hd in range(GLA_HEADS):
        sl = slice(hd * GLA_HV, (hd + 1) * GLA_HV)
        parts.append((_rms(o[:, sl]) * on_ref[...] * silu[:, sl]).astype(BF16))
    y_gla = jnp.concatenate(parts, axis=1)
    zg = zg_ref[0].astype(F32)
    d = x_ref.shape[-1]
    m = zg[:, 0:d] * _dot(ymla_ref[0], wm_ref[...])
    m = m + zg[:, d:2 * d] * _dot(y_gla, wgl_ref[...])
    m = m + zg[:, 2 * d:3 * d] * _dot(yhy_ref[0].astype(BF16), wh_ref[...])
    out = _dot(m.astype(BF16), wo_ref[...])
    o_ref[0] = x_ref[0] + mod_ref[0, 2:3, :] * out


def _mod_spec(d, n_lat_tiles, bsz):
    return pl.BlockSpec((1, 6, d), lambda b, i: (jnp.where(i < n_lat_tiles, b, bsz), 0, 0))


def _merge(xc, mod_l, y_mla, o_gla, gr, y_hy, gate, out_norm, w_o_mla, w_o_gla, w_o_hy, w_out, n_tiles, n_lat_tiles):
    bsz, t, d = xc.shape
    tile = lambda w: pl.BlockSpec((1, TM, w), lambda b, i: (b, i, 0))
    dirspec = lambda dr: pl.BlockSpec((1, 1, TM, GLA_DV), lambda b, i: (b, dr, i, 0))
    bf = lambda w: w.astype(BF16)
    return pl.pallas_call(
        _merge_kernel,
        out_shape=jax.ShapeDtypeStruct((bsz, n_tiles * TM, d), F32),
        grid=(bsz, n_tiles),
        in_specs=[tile(d), _mod_spec(d, n_lat_tiles, bsz), tile(MLA_OUT), dirspec(0), dirspec(1), tile(GLA_DV),
                  tile(HY_WIDTH), tile(3 * d), pl.BlockSpec((1, GLA_HV), lambda b, i: (0, 0)),
                  _const_spec(w_o_mla.shape), _const_spec(w_o_gla.shape), _const_spec(w_o_hy.shape),
                  _const_spec(w_out.shape)],
        out_specs=tile(d),
        compiler_params=_cparams(2, VMEM_LIMIT),
        name="merge",
    )(xc, mod_l, y_mla, o_gla, o_gla, gr, y_hy, gate, out_norm[None, :], bf(w_o_mla), bf(w_o_gla), bf(w_o_hy),
      bf(w_out))


FF_CHUNK = 1024


def _mlp_kernel(x_ref, mod_ref, g2_ref, w1_ref, w2_ref, fg_ref, o_ref, *, final):
    x = x_ref[0]
    h = (_rms(x) * g2_ref[...] * (1.0 + mod_ref[0, 4:5, :]) + mod_ref[0, 3:4, :]).astype(BF16)
    acc = jnp.zeros(x.shape, F32)
    for j in range(w1_ref.shape[1] // FF_CHUNK):
        a = jnp.maximum(_dot(h, w1_ref[:, j * FF_CHUNK:(j + 1) * FF_CHUNK]), 0.0)
        acc = acc + _dot((a * a).astype(BF16), w2_ref[j * FF_CHUNK:(j + 1) * FF_CHUNK, :])
    xn = x + mod_ref[0, 5:6, :] * acc
    if final:
        xn = _rms(xn) * fg_ref[...]
    o_ref[0] = xn


def _mlp(xc, mod_l, g2, w1, w2, final_g, n_tiles, n_lat_tiles, final):
    bsz, t, d = xc.shape
    tile = pl.BlockSpec((1, TM, d), lambda b, i: (b, i, 0))
    row = pl.BlockSpec((1, d), lambda b, i: (0, 0))
    t_out = n_tiles * TM if final else t
    return pl.pallas_call(
        functools.partial(_mlp_kernel, final=final),
        out_shape=jax.ShapeDtypeStruct((bsz, t_out, d), F32),
        grid=(bsz, n_tiles),
        in_specs=[tile, _mod_spec(d, n_lat_tiles, bsz), row, _const_spec(w1.shape), _const_spec(w2.shape), row],
        out_specs=tile,
        compiler_params=_cparams(2, VMEM_LIMIT),
        name="mlp",
    )(xc, mod_l, g2[None, :], w1.astype(BF16), w2.astype(BF16), final_g[None, :])


def kernel(x, c, ctx, c_ctx, ada_w, ada_b, norm1_g, norm2_g, w_in, mla_q_norm, mla_w_uq, mla_kv_norm, mla_w_ukv, gla_w_a2, gla_b_a, gla_out_norm, hy_short_w, hy_short_b, hy_f_w1, hy_f_b1, hy_f_w2, hy_f_b2, hy_f_w3, hy_f_b3, hy_bias, w_o_mla, w_o_gla, w_o_hy, w_out, ff_w1, ff_w2, final_norm_g):
    bsz, seq, d = x.shape
    ctx_len = ctx.shape[1]
    n_lat = seq // TM
    n_all = (seq + ctx_len) // TM
    xc = jnp.concatenate([x, ctx], axis=1)
    cc = jnp.zeros((16, d), F32).at[:bsz].set(c).at[bsz].set(c_ctx)
    mod = _modulation(cc, ada_w, ada_b).reshape(DEPTH, 16, 6, d)
    cos, sin = _rope_tables(seq, ctx_len)
    for l in range(DEPTH):
        last = l == DEPTH - 1
        n_tiles = n_lat if last else n_all
        weights = _prep_inproj_weights(w_in[l], mla_w_uq[l], mla_w_ukv[l], gla_w_a2[l], gla_b_a[l])
        q, k, v, gq, gk, gv, gr, glog, z_hy, z_gate = _inproj(xc, mod[l], norm1_g[l], weights, mla_q_norm[l],
                                                             mla_kv_norm[l], cos, sin, n_lat)
        y_mla = _attention(q, k, v, seq, not last)
        o_gla = _gla(gq, gk, gv, glog, n_lat)
        filt_w = (hy_f_w1[l], hy_f_b1[l], hy_f_w2[l], hy_f_b2[l], hy_f_w3[l], hy_f_b3[l])
        hspec = _hyena_filter_spectrum(seq, *filt_w)
        y_hy = _hyena_latent(z_hy, hspec, hy_short_w[l], hy_short_b[l], hy_bias[l], not last)
        if not last:
            y_hy = _hyena_ctx(z_hy, y_hy, filt_w, hy_short_w[l], hy_short_b[l], hy_bias[l])
        xc = _merge(xc, mod[l], y_mla, o_gla, gr, y_hy, z_gate, gla_out_norm[l], w_o_mla[l], w_o_gla[l], w_o_hy[l],
                    w_out[l], n_tiles, n_lat)
        xc = _mlp(xc, mod[l], norm2_g[l], ff_w1[l], ff_w2[l], final_norm_g, n_tiles, n_lat, last)
    return xc
```

```python
import functools
import math

import numpy as np
import jax
import jax.numpy as jnp
from jax import lax
from jax.experimental import pallas as pl
from jax.experimental.pallas import tpu as pltpu

F32 = jnp.float32
BF16 = jnp.bfloat16
HIGHEST = lax.Precision.HIGHEST
LOG2E = 1.4426950408889634

D_MODEL = 1024
DEPTH = 2
GRID_W = 64
EPS = 1e-6
MLA_HEADS = 8
MLA_NOPE = 64
MLA_ROPE = 32
MLA_V = 64
MLA_Q_LORA = 256
MLA_KV_LORA = 128
MLA_SCALE = (MLA_NOPE + MLA_ROPE) ** -0.5
ROPE_BASE = 10000.0
GLA_HEADS = 4
GLA_DK = 256
GLA_DV = 512
GLA_HK = GLA_DK // GLA_HEADS
GLA_HV = GLA_DV // GLA_HEADS
GLA_GATE_RANK = 16
GLA_TAU = 16.0
HY_WIDTH = 512
HY_ORDER = 2
HY_SHORT = 3
HY_BANDS = 16
HY_POS_DIM = 1 + 2 * HY_BANDS
HY_FILTER_HIDDEN = 64
HY_FAST_DECAY = 0.3
HY_SLOW_DECAY = 1.5
HY_DECAY_TARGET = 1e-2
D_FF = 4 * D_MODEL
MLA_OUT = MLA_HEADS * MLA_V
IN_SIZES = (MLA_Q_LORA, MLA_KV_LORA, MLA_ROPE, GLA_DK, GLA_DK, GLA_DV, GLA_DV, GLA_GATE_RANK, GLA_GATE_RANK,
            (HY_ORDER + 1) * HY_WIDTH, 3 * D_MODEL)

LANES = 128
SUBLANES = 8
TM = 256
HEAD_SLOT = 128
VMEM_LIMIT = 56 * 1024 * 1024


def _cparams(n_axes, vmem=None):
    return pltpu.CompilerParams(dimension_semantics=("arbitrary",) * n_axes, vmem_limit_bytes=vmem)


def _const_spec(shape):
    nd = len(shape)
    return pl.BlockSpec(shape, lambda *_: (0,) * nd, pipeline_mode=pl.Buffered(1))


def _rms(x):
    return x * lax.rsqrt(jnp.mean(x * x, axis=-1, keepdims=True) + EPS)


def _sigmoid(x):
    return 1.0 / (1.0 + jnp.exp(-x))


def _dot(a, b):
    return jnp.dot(a, b, preferred_element_type=F32)


def _dot_nt(a, b):
    return lax.dot_general(a, b, (((1,), (1,)), ((), ())), preferred_element_type=F32)


def _mod_kernel(cc_ref, w_ref, b_ref, o_ref):
    s = cc_ref[...]
    s = s * _sigmoid(s)
    o_ref[0] = jnp.dot(s, w_ref[0], preferred_element_type=F32, precision=HIGHEST) + b_ref[0]


def _modulation(cc, ada_w, ada_b):
    tn = 1536
    n6 = ada_w.shape[-1]
    return pl.pallas_call(
        _mod_kernel,
        out_shape=jax.ShapeDtypeStruct((DEPTH, 16, n6), F32),
        grid=(DEPTH, n6 // tn),
        in_specs=[
            pl.BlockSpec((16, D_MODEL), lambda l, j: (0, 0)),
            pl.BlockSpec((1, D_MODEL, tn), lambda l, j: (l, 0, j)),
            pl.BlockSpec((1, 1, tn), lambda l, j: (l, 0, j)),
        ],
        out_specs=pl.BlockSpec((1, 16, tn), lambda l, j: (l, 0, j)),
        compiler_params=_cparams(2),
        name="modulation",
    )(cc, ada_w, ada_b.reshape(DEPTH, 1, n6))


W_A = 768
W_G = 2 * GLA_DK + 2 * GLA_DV


def _inproj_kernel(x_ref, mod_ref, g1_ref, wa_ref, wg_ref, wh_ref, wz_ref, qn_ref, kvn_ref, wuq_ref, wukv_ref,
                   wa2_ref, ba_ref, cos_ref, sin_ref,
                   q_out, k_out, v_out, gq_out, gk_out, gv_out, gr_out, glog_out, hy_out, gate_out):
    x = x_ref[0]
    shift = mod_ref[0, 0:1, :]
    scale = mod_ref[0, 1:2, :]
    h = (_rms(x) * g1_ref[...] * (1.0 + scale) + shift).astype(BF16)

    za = _dot(h, wa_ref[...])
    cos = cos_ref[...]
    sin = sin_ref[...]

    cqn = (_rms(za[:, 0:256]) * qn_ref[...]).astype(BF16)
    qab = _dot(cqn, wuq_ref[...])
    nq = MLA_HEADS * HEAD_SLOT
    for hd in range(MLA_HEADS):
        sl = slice(hd * HEAD_SLOT, (hd + 1) * HEAD_SLOT)
        qa = qab[:, hd * HEAD_SLOT:(hd + 1) * HEAD_SLOT]
        qb = qab[:, nq + hd * HEAD_SLOT:nq + (hd + 1) * HEAD_SLOT]
        q_out[0, :, sl] = ((qa * cos + qb * sin) * (MLA_SCALE * LOG2E)).astype(BF16)

    ckvn = (_rms(za[:, 256:384]) * kvn_ref[...]).astype(BF16)
    kv = _dot(ckvn, wukv_ref[...])
    krot = za[:, 384:512] * cos + za[:, 512:640] * sin
    for hd in range(MLA_HEADS):
        sl = slice(hd * HEAD_SLOT, (hd + 1) * HEAD_SLOT)
        k_out[0, :, sl] = (kv[:, sl] + krot).astype(BF16)
    ones_hi = (lax.broadcasted_iota(jnp.int32, (1, HEAD_SLOT), 1) >= MLA_V).astype(F32)
    for hd in range(MLA_HEADS):
        sl = slice(hd * HEAD_SLOT, (hd + 1) * HEAD_SLOT)
        v_out[0, :, sl] = (kv[:, nq + hd * HEAD_SLOT:nq + (hd + 1) * HEAD_SLOT] + ones_hi).astype(BF16)

    xg = _dot(za[:, 640:768].astype(BF16), wa2_ref[...]) + ba_ref[...]
    glog_out[0] = (jnp.minimum(xg, 0.0) - jnp.log(1.0 + jnp.exp(-jnp.abs(xg)))) * (1.0 / GLA_TAU)

    zg = _dot(h, wg_ref[...])
    gq_out[0] = (zg[:, 0:GLA_DK] * (GLA_HK ** -0.5)).astype(BF16)
    gk_out[0] = zg[:, GLA_DK:2 * GLA_DK].astype(BF16)
    gv_out[0] = zg[:, 2 * GLA_DK:2 * GLA_DK + GLA_DV].astype(BF16)
    gr = zg[:, 2 * GLA_DK + GLA_DV:]
    gr_out[0] = (gr * _sigmoid(gr)).astype(BF16)

    hy_out[0] = _dot(h, wh_ref[...])
    gate_out[0] = _sigmoid(_dot(h, wz_ref[...])).astype(BF16)


def _rope_partner(w):
    a = MLA_ROPE // 4
    perm = np.concatenate([np.arange(a, 2 * a), np.arange(0, a), np.arange(3 * a, 4 * a), np.arange(2 * a, 3 * a)])
    sign = np.concatenate([-np.ones(a), np.ones(a), -np.ones(a), np.ones(a)]).astype(np.float32)
    return w[:, perm] * sign


def _prep_inproj_weights(w_in, mla_w_uq, mla_w_ukv, gla_w_a2, gla_b_a):
    offs = np.concatenate([[0], np.cumsum(IN_SIZES)])
    seg = [w_in[:, offs[i]:offs[i + 1]] for i in range(len(IN_SIZES))]
    w_cq, w_ckv, w_kr, w_gq, w_gk, w_gv, w_gr, w_af, w_ab, w_hy, w_gate = seg
    d = w_in.shape[0]
    z = lambda n: jnp.zeros((d, n), w_in.dtype)
    kr_tile = jnp.concatenate([z(MLA_NOPE), w_kr, z(HEAD_SLOT - MLA_NOPE - MLA_ROPE)], axis=1)
    krp_tile = jnp.concatenate([z(MLA_NOPE), _rope_partner(w_kr), z(HEAD_SLOT - MLA_NOPE - MLA_ROPE)], axis=1)
    a_tile = jnp.concatenate([w_af, w_ab, z(LANES - 2 * GLA_GATE_RANK)], axis=1)
    wa = jnp.concatenate([w_cq, w_ckv, kr_tile, krp_tile, a_tile], axis=1)
    wg = jnp.concatenate([w_gq, w_gk, w_gv, w_gr], axis=1)

    dh = MLA_NOPE + MLA_ROPE
    zq = lambda n: jnp.zeros((MLA_Q_LORA, n), w_in.dtype)
    plain, partner = [], []
    for hd in range(MLA_HEADS):
        blk = mla_w_uq[:, hd * dh:(hd + 1) * dh]
        plain += [blk, zq(HEAD_SLOT - dh)]
        partner += [zq(MLA_NOPE), _rope_partner(blk[:, MLA_NOPE:]), zq(HEAD_SLOT - dh)]
    wuq = jnp.concatenate(plain + partner, axis=1)

    zk = jnp.zeros((MLA_KV_LORA, HEAD_SLOT - MLA_NOPE), w_in.dtype)
    kcols, vcols = [], []
    for hd in range(MLA_HEADS):
        blk = mla_w_ukv[:, hd * (MLA_NOPE + MLA_V):(hd + 1) * (MLA_NOPE + MLA_V)]
        kcols += [blk[:, :MLA_NOPE], zk]
        vcols += [blk[:, MLA_NOPE:], zk]
    wukv = jnp.concatenate(kcols + vcols, axis=1)

    wa2 = jnp.zeros((LANES, 2 * GLA_DK), w_in.dtype)
    wa2 = wa2.at[0:GLA_GATE_RANK, 0:GLA_DK].set(gla_w_a2[0])
    wa2 = wa2.at[GLA_GATE_RANK:2 * GLA_GATE_RANK, GLA_DK:].set(gla_w_a2[1])
    ba = jnp.concatenate([gla_b_a[0], gla_b_a[1]])[None, :]
    bf = lambda t: t.astype(BF16)
    return bf(wa), bf(wg), bf(w_hy), bf(w_gate), bf(wuq), bf(wukv), bf(wa2), ba


def _rope_tables(seq, ctx_len):
    rows = seq // GRID_W
    row = np.repeat(np.arange(rows, dtype=np.float64), GRID_W)
    col = np.tile(np.arange(GRID_W, dtype=np.float64), rows)
    a = MLA_ROPE // 4
    inv = ROPE_BASE ** (-np.arange(a, dtype=np.float64) / a)
    ang_r = row[:, None] * inv
    ang_c = col[:, None] * inv
    cos32 = np.concatenate([np.cos(ang_r), np.cos(ang_r), np.cos(ang_c), np.cos(ang_c)], axis=1)
    sin32 = np.concatenate([np.sin(ang_r), np.sin(ang_r), np.sin(ang_c), np.sin(ang_c)], axis=1)
    pad_r = HEAD_SLOT - MLA_NOPE - MLA_ROPE
    cos = np.concatenate([np.ones((seq, MLA_NOPE)), cos32, np.zeros((seq, pad_r))], axis=1)
    sin = np.concatenate([np.zeros((seq, MLA_NOPE)), sin32, np.zeros((seq, pad_r))], axis=1)
    cos_c = np.concatenate([np.ones((ctx_len, MLA_NOPE + MLA_ROPE)), np.zeros((ctx_len, pad_r))], axis=1)
    sin_c = np.zeros((ctx_len, HEAD_SLOT))
    return (jnp.asarray(np.concatenate([cos, cos_c], axis=0), F32),
            jnp.asarray(np.concatenate([sin, sin_c], axis=0), F32))


def _inproj(xc, mod_l, g1, weights, q_norm, kv_norm, cos, sin, n_lat_tiles):
    bsz, t, d = xc.shape
    nt = t // TM
    wa, wg, wh, wz, wuq, wukv, wa2, ba = weights
    tile = lambda w: pl.BlockSpec((1, TM, w), lambda b, i: (b, i, 0))
    row = lambda w: pl.BlockSpec((1, w), lambda b, i: (0, 0))
    mod_spec = pl.BlockSpec((1, 6, d), lambda b, i: (jnp.where(i < n_lat_tiles, b, bsz), 0, 0))
    tab = pl.BlockSpec((TM, HEAD_SLOT), lambda b, i: (i, 0))
    nq = MLA_HEADS * HEAD_SLOT
    sds = lambda w, dt: jax.ShapeDtypeStruct((bsz, t, w), dt)
    out_shape = (sds(nq, BF16), sds(nq, BF16), sds(nq, BF16), sds(GLA_DK, BF16), sds(GLA_DK, BF16),
                 sds(GLA_DV, BF16), sds(GLA_DV, BF16), sds(2 * GLA_DK, F32), sds(3 * HY_WIDTH, F32),
                 sds(3 * D_MODEL, BF16))
    out_specs = (tile(nq), tile(nq), tile(nq), tile(GLA_DK), tile(GLA_DK), tile(GLA_DV), tile(GLA_DV),
                 tile(2 * GLA_DK), tile(3 * HY_WIDTH), tile(3 * D_MODEL))
    return pl.pallas_call(
        _inproj_kernel,
        out_shape=out_shape,
        grid=(bsz, nt),
        in_specs=[tile(d), mod_spec, row(d), _const_spec(wa.shape), _const_spec(wg.shape), _const_spec(wh.shape),
                  _const_spec(wz.shape), row(MLA_Q_LORA), row(MLA_KV_LORA), _const_spec(wuq.shape),
                  _const_spec(wukv.shape), _const_spec(wa2.shape), row(2 * GLA_DK), tab, tab],
        out_specs=out_specs,
        compiler_params=_cparams(2, VMEM_LIMIT),
        name="inproj",
    )(xc, mod_l, g1[None, :], wa, wg, wh, wz, q_norm[None, :], kv_norm[None, :], wuq, wukv, wa2, ba, cos, sin)


ATT_TK = 1024


ATT_TQ = 1024


def _attn_kernel(*refs, chunks, aliased):
    if aliased:
        q_ref, k_ref, v_ref, _, o_ref, m_ref, acc_ref = refs
    else:
        q_ref, k_ref, v_ref, o_ref, m_ref, acc_ref = refs
    tq = q_ref.shape[1]
    m_ref[...] = jnp.full(m_ref.shape, -jnp.inf, F32)
    acc_ref[...] = jnp.zeros(acc_ref.shape, F32)
    for r0, size in chunks:
        k = k_ref[0, pl.ds(r0, size), :]
        v = v_ref[0, pl.ds(r0, size), :]
        for hd in range(2):
            sl = slice(hd * HEAD_SLOT, (hd + 1) * HEAD_SLOT)
            s = _dot_nt(q_ref[0, :, sl], k[:, sl])
            m_prev = m_ref[hd]
            m_new = jnp.maximum(m_prev, jnp.max(s, axis=1, keepdims=True))
            p = jnp.exp2((s - jnp.concatenate([m_new] * (size // LANES), axis=1)).astype(BF16))
            acc_ref[hd] = jnp.exp2(m_prev - m_new) * acc_ref[hd] + _dot(p, v[:, sl])
            m_ref[hd] = m_new
    a0 = acc_ref[0]
    a1 = acc_ref[1]
    lane = lax.broadcasted_iota(jnp.int32, (tq, HEAD_SLOT), 1)
    o0 = a0 / pltpu.roll(a0, MLA_V, 1)
    o1 = pltpu.roll(a1, MLA_V, 1) / a1
    o_ref[0] = jnp.where(lane < MLA_V, o0, o1).astype(o_ref.dtype)


def _attention(q, k, v, seq, with_ctx_queries):
    bsz, t, _ = q.shape
    ctx_len = t - seq
    assert seq % ATT_TK == 0 and seq % ATT_TQ == 0 and seq % ctx_len == 0 and 2 * MLA_V == HEAD_SLOT
    pair = 2 * HEAD_SLOT
    n_chunks = seq // ATT_TK
    chunks = tuple((j * ATT_TK, ATT_TK) for j in range(n_chunks - 1))
    chunks += (((n_chunks - 1) * ATT_TK, ATT_TK + ctx_len),)
    scratch = lambda tq: [pltpu.VMEM((2, tq, LANES), F32), pltpu.VMEM((2, tq, HEAD_SLOT), F32)]
    y = pl.pallas_call(
        functools.partial(_attn_kernel, chunks=chunks, aliased=False),
        out_shape=jax.ShapeDtypeStruct((bsz, t, MLA_OUT), BF16),
        grid=(bsz, MLA_HEADS // 2, seq // ATT_TQ),
        in_specs=[
            pl.BlockSpec((1, ATT_TQ, pair), lambda b, hp, i: (b, i, hp)),
            pl.BlockSpec((1, t, pair), lambda b, hp, i: (b, 0, hp)),
            pl.BlockSpec((1, t, pair), lambda b, hp, i: (b, 0, hp)),
        ],
        out_specs=pl.BlockSpec((1, ATT_TQ, HEAD_SLOT), lambda b, hp, i: (b, i, hp)),
        scratch_shapes=scratch(ATT_TQ),
        compiler_params=_cparams(3, VMEM_LIMIT),
        name="mla_attention",
    )(q, k, v)
    if not with_ctx_queries:
        return y
    cblk = seq // ctx_len
    ctx_rows = lambda w: pl.BlockSpec((1, ctx_len, w), lambda b, hp: (b, cblk, hp))
    return pl.pallas_call(
        functools.partial(_attn_kernel, chunks=((0, ctx_len),), aliased=True),
        out_shape=jax.ShapeDtypeStruct((bsz, t, MLA_OUT), BF16),
        grid=(bsz, MLA_HEADS // 2),
        in_specs=[ctx_rows(pair), ctx_rows(pair), ctx_rows(pair), pl.BlockSpec(memory_space=pl.ANY)],
        out_specs=ctx_rows(HEAD_SLOT),
        scratch_shapes=scratch(ctx_len),
        input_output_aliases={3: 0},
        compiler_params=_cparams(2),
        name="mla_attention_ctx",
    )(q, k, v, y)


GLA_LEVELS = int(math.log2(TM))
GLA_SAFE_SPAN = 60.0


def _gla_level_matrices():
    i = np.arange(TM)[:, None]
    t = np.arange(TM)[None, :]
    fwd = [(t <= i)]
    for lv in range(GLA_LEVELS):
        m = TM >> (lv + 1)
        lo = (i // m) * m
        later = ((i // m) % 2) == 1
        q_part = later & (t >= lo) & (t <= i)
        k_part = (~later) & (t > i) & (t <= lo + m - 1)
        fwd.append(q_part | k_part)
    fwd = np.concatenate(fwd, axis=0).astype(np.float32)
    nb = 1 + GLA_LEVELS
    bwd = fwd.reshape(nb, TM, TM)[:, ::-1, ::-1].reshape(nb * TM, TM)
    return np.stack([fwd, bwd])


def _gla_kernel(q_ref, k_ref, v_ref, g_ref, mall_ref, o_ref, s_ref, a_ref):
    d = pl.program_id(1)
    step = pl.program_id(2)

    @pl.when(step == 0)
    def _():
        s_ref[...] = jnp.zeros(s_ref.shape, F32)

    q = q_ref[0].astype(F32)
    k = k_ref[0].astype(F32)
    v = v_ref[0]
    g = g_ref[0]
    g2 = jnp.concatenate([g.astype(BF16), (g - g.astype(BF16).astype(F32)).astype(BF16)], axis=1)
    e2 = _dot(mall_ref[0, 0:TM, :], g2)
    g_cum = e2[:, :GLA_DK] + e2[:, GLA_DK:]
    g_tot = jnp.sum(g, axis=0, keepdims=True)

    row = lax.broadcasted_iota(jnp.int32, (TM, TM), 0)
    col = lax.broadcasted_iota(jnp.int32, (TM, TM), 1)
    lane_head = lax.broadcasted_iota(jnp.int32, (TM, GLA_DK), 1) // GLA_HK
    tok = lax.broadcasted_iota(jnp.int32, (TM, GLA_DK), 0)
    eye = row == col

    def stack_heads(t):
        return jnp.concatenate([jnp.where(lane_head == hd, t, 0.0) for hd in range(GLA_HEADS)], axis=0).astype(BF16)

    span = jnp.max(-g_tot)

    @pl.when(span < GLA_SAFE_SPAN)
    def _():
        res = _dot_nt(stack_heads(q * jnp.exp(g_cum)), (k * jnp.exp(-g_cum)).astype(BF16))
        seen = (col - row) * (1 - 2 * d) <= 0
        for hd in range(GLA_HEADS):
            a_ref[hd] = jnp.where(seen, res[hd * TM:(hd + 1) * TM], 0.0)

    @pl.when(span >= GLA_SAFE_SPAN)
    def _():
        e2l = _dot(mall_ref[0, TM:, :], g2)
        e_lv = e2l[:, :GLA_DK] + e2l[:, GLA_DK:]
        res = _dot_nt(stack_heads(q), k.astype(BF16))
        for hd in range(GLA_HEADS):
            a_ref[hd] = jnp.where(eye, res[hd * TM:(hd + 1) * TM], 0.0)
        for lv in range(GLA_LEVELS):
            m = TM >> (lv + 1)
            w = jnp.exp(e_lv[lv * TM:(lv + 1) * TM])
            bit = (tok // m) % 2
            q_act = bit != d
            qt = jnp.where(q_act, q * w, 0.0)
            kt = jnp.where(q_act, 0.0, k * w).astype(BF16)
            res = _dot_nt(stack_heads(qt), kt)
            if m == TM // 2:
                for hd in range(GLA_HEADS):
                    a_ref[hd] += res[hd * TM:(hd + 1) * TM]
            else:
                same = (row // (2 * m)) == (col // (2 * m))
                for hd in range(GLA_HEADS):
                    a_ref[hd] += jnp.where(same, res[hd * TM:(hd + 1) * TM], 0.0)

    s_old = s_ref[...]
    s_bf = s_old.astype(BF16)
    qg = q * jnp.exp(g_cum)
    for hd in range(GLA_HEADS):
        vh = v[:, hd * GLA_HV:(hd + 1) * GLA_HV]
        o_intra = _dot(a_ref[hd].astype(BF16), vh)
        o_inter = _dot(jnp.where(lane_head == hd, qg, 0.0).astype(BF16), s_bf)
        o_ref[0, 0, :, hd * GLA_HV:(hd + 1) * GLA_HV] = (o_intra + o_inter).astype(o_ref.dtype)

    kdec_t = (k * jnp.exp(g_tot - g_cum)).T.astype(BF16)
    upd = _dot(kdec_t, v)
    row_head = lax.broadcasted_iota(jnp.int32, (GLA_DK, GLA_HV), 0) // GLA_HK
    new = jnp.zeros((GLA_DK, GLA_HV), F32)
    for hd in range(GLA_HEADS):
        new = new + jnp.where(row_head == hd, upd[:, hd * GLA_HV:(hd + 1) * GLA_HV], 0.0)
    dec_col = jnp.sum(jnp.where(eye, jnp.broadcast_to(jnp.exp(g_tot), (TM, GLA_DK)), 0.0), axis=1, keepdims=True)
    s_ref[...] = dec_col * s_old + new


def _gla(gq, gk, gv, glog, n_lat_tiles):
    assert GLA_DK == TM
    bsz, t, _ = gq.shape
    nt = t // TM
    mall = jnp.asarray(_gla_level_matrices(), dtype=BF16)

    def tile_idx(d, s):
        return jnp.where(s == 0, n_lat_tiles, jnp.where(d == 0, s - 1, n_lat_tiles - s))

    tok = lambda w: pl.BlockSpec((1, TM, w), lambda b, d, s: (b, tile_idx(d, s), 0))
    return pl.pallas_call(
        _gla_kernel,
        out_shape=jax.ShapeDtypeStruct((bsz, 2, t, GLA_DV), BF16),
        grid=(bsz, 2, nt),
        in_specs=[tok(GLA_DK), tok(GLA_DK), tok(GLA_DV),
                  pl.BlockSpec((1, TM, GLA_DK), lambda b, d, s: (b, tile_idx(d, s), d)),
                  pl.BlockSpec((1,) + mall.shape[1:], lambda b, d, s: (d, 0, 0))],
        out_specs=pl.BlockSpec((1, 1, TM, GLA_DV), lambda b, d, s: (b, d, tile_idx(d, s), 0)),
        scratch_shapes=[pltpu.VMEM((GLA_DK, GLA_HV), F32), pltpu.VMEM((GLA_HEADS, TM, TM), F32)],
        compiler_params=_cparams(3),
        name="gla_scan",
    )(gq, gk, gv, glog, mall)


HY_N1 = 64
HY_N2 = 128
HY_SLAB = SUBLANES
HY_CB = 256
HY_KG = 16
HY_SPEC_CB = 128


def _hyena_dft_constants(seq):
    n = 2 * seq
    assert n == HY_N1 * HY_N2
    nh = HY_N1 // 2
    kh = HY_N2 // 2
    eye = np.eye(HY_SLAB)
    k1 = np.arange(HY_N1)
    th = 2 * np.pi * np.outer(k1 + 0.5, np.arange(nh)) / HY_N1
    fwd1 = np.concatenate([np.kron(np.cos(th), eye), np.kron(-np.sin(th), eye)], axis=0)
    inv1 = (2.0 / n) * np.concatenate([np.kron(np.cos(th).T, eye), np.kron(-np.sin(th).T, eye)], axis=1)
    nlo = np.arange(HY_N2)
    k2 = np.arange(kh)
    ph = 2 * np.pi * (k2[None, :, None] * nlo[None, None, :] / HY_N2
                      + (k1[:, None, None] + 0.5) * nlo[None, None, :] / n)
    c, s = np.cos(ph), np.sin(ph)
    fwd2 = np.concatenate([np.concatenate([c, s], axis=2), np.concatenate([-s, c], axis=2)], axis=1)
    ct, st = c.transpose(0, 2, 1), s.transpose(0, 2, 1)
    inv2 = np.concatenate([np.concatenate([ct, -st], axis=2), np.concatenate([st, ct], axis=2)], axis=1)
    return fwd1, fwd2, inv2, inv1


def _hyena_ctx_dft_constants(ctx_len):
    n = 2 * ctx_len
    th = 2 * np.pi * np.outer(np.arange(ctx_len) + 0.5, np.arange(ctx_len)) / n
    fwd = np.concatenate([np.cos(th), -np.sin(th)], axis=0)
    inv = (2.0 / n) * np.concatenate([np.cos(th).T, -np.sin(th).T], axis=1)
    return fwd, inv


def _hyena_features(length):
    pos = np.arange(length, dtype=np.float64)
    t = pos / max(length - 1, 1)
    f = np.linspace(1e-4, HY_BANDS - 1, HY_BANDS)
    ang = (2.0 * math.pi / length) * pos[:, None] * f
    feat = np.concatenate([t[:, None], np.cos(ang), np.sin(ang)], axis=-1)
    return jnp.asarray(np.pad(feat, ((0, 0), (0, LANES - HY_POS_DIM))), F32)


def _hy_filter_kernel(feat_ref, w1_ref, b1_ref, w2_ref, b2_ref, w3_ref, b3_ref, absd_ref, h_ref, s_ref):
    i = pl.program_id(0)
    feat = feat_ref[...]
    hp = lambda a, b: jnp.dot(a, b, preferred_element_type=F32, precision=HIGHEST)
    hdn = jnp.sin(hp(feat, w1_ref[...]) + b1_ref[...])
    hdn = jnp.sin(hp(hdn, w2_ref[...]) + b2_ref[...])
    h = hp(hdn, w3_ref[...]) + b3_ref[...]
    window = jnp.exp(-feat[:, 0:1] * absd_ref[...])
    h = h * jnp.concatenate([window] * (2 * HY_ORDER), axis=1)
    h_ref[...] = h

    @pl.when(i == 0)
    def _():
        s_ref[...] = jnp.zeros(s_ref.shape, F32)

    s_ref[...] += jnp.sum(jnp.abs(h), axis=0, keepdims=True)


def _hyena_filters_raw(length, filt_w):
    w1, b1, w2, b2, w3, b3 = filt_w
    nf = 2 * HY_ORDER * HY_WIDTH
    tr = min(length, 512)
    deltas = np.linspace(math.log(HY_DECAY_TARGET) / HY_FAST_DECAY, math.log(HY_DECAY_TARGET) / HY_SLOW_DECAY,
                         HY_WIDTH, dtype=np.float32)
    absd = jnp.asarray(np.abs(deltas))[None, :]
    w1p = jnp.pad(w1, ((0, LANES - HY_POS_DIM), (0, 0)))
    full = lambda shp: pl.BlockSpec(shp, lambda i: (0,) * len(shp))
    return pl.pallas_call(
        _hy_filter_kernel,
        out_shape=(jax.ShapeDtypeStruct((length, nf), F32), jax.ShapeDtypeStruct((1, nf), F32)),
        grid=(length // tr,),
        in_specs=[pl.BlockSpec((tr, LANES), lambda i: (i, 0)), full((LANES, HY_FILTER_HIDDEN)),
                  full((1, HY_FILTER_HIDDEN)), full((HY_FILTER_HIDDEN, HY_FILTER_HIDDEN)), full((1, HY_FILTER_HIDDEN)),
                  full((HY_FILTER_HIDDEN, nf)), full((1, nf)), full((1, HY_WIDTH))],
        out_specs=(pl.BlockSpec((tr, nf), lambda i: (i, 0)), full((1, nf))),
        compiler_params=_cparams(1),
        name="hyena_filter_mlp",
    )(_hyena_features(length), w1p, b1[None, :], w2, b2[None, :], w3, b3[None, :], absd)


def _split_bf16(m):
    hi = m.astype(BF16)
    return jnp.asarray(np.stack([hi, (m - hi.astype(np.float64)).astype(BF16)]))


def _dot_split(m_hi, m_lo, x):
    x_hi = x.astype(BF16)
    x_lo = (x - x_hi.astype(F32)).astype(BF16)
    n = x.shape[1]
    r = _dot(m_hi, jnp.concatenate([x_hi, x_lo], axis=1))
    return r[:, :n] + r[:, n:] + _dot(m_lo, x_hi)


def _odft_stage1(src_at, mm, s_re, s_im):
    nk = s_re.shape[0]
    half = nk * HY_SLAB

    def body(j, carry):
        r0 = pl.multiple_of(j * HY_SLAB, HY_SLAB)
        slab = src_at(r0)
        cb = slab.shape[-1]
        res = mm(slab.reshape(-1, cb))
        s_re[:, pl.ds(r0, HY_SLAB), :] = res[:half].reshape(nk, HY_SLAB, cb)
        s_im[:, pl.ds(r0, HY_SLAB), :] = res[half:].reshape(nk, HY_SLAB, cb)
        return carry

    lax.fori_loop(0, HY_N2 // HY_SLAB, body, 0, unroll=2)


def _hy_spectrum_kernel(hf_ref, hb_ref, sf_ref, sb_ref, fwd1_ref, fwd2_ref, o_ref, s_re, s_im):
    kh = HY_N2 // 2
    mm1 = lambda x: _dot_split(fwd1_ref[0], fwd1_ref[1], x)

    def middle(sign):
        def body(k1, carry):
            a = jnp.concatenate([s_re[k1], s_im[k1]], axis=0)
            x = _dot_split(fwd2_ref[0, k1], fwd2_ref[1, k1], a)
            if sign is None:
                o_ref[0, k1] = x[:kh]
                o_ref[1, k1] = x[kh:]
            else:
                inv_norm = 1.0 / (sf_ref[...] + sb_ref[...])
                o_ref[0, k1] = (o_ref[0, k1] + x[:kh]) * inv_norm
                o_ref[1, k1] = (o_ref[1, k1] - x[kh:]) * inv_norm
            return carry
        lax.fori_loop(0, HY_N1, body, 0, unroll=2)

    _odft_stage1(lambda r0: hf_ref[:, pl.ds(r0, HY_SLAB), :], mm1, s_re, s_im)
    middle(None)

    def bwd_slab(r0):
        slab = hb_ref[:, pl.ds(r0, HY_SLAB), :]
        nhi = lax.broadcasted_iota(jnp.int32, slab.shape, 0)
        r = lax.broadcasted_iota(jnp.int32, slab.shape, 1)
        return jnp.where((nhi == 0) & (r + r0 == 0), 0.0, slab)

    _odft_stage1(bwd_slab, mm1, s_re, s_im)
    middle(-1)


def _hyena_filter_spectrum(seq, w1, b1, w2, b2, w3, b3):
    h_raw, s = _hyena_filters_raw(seq, (w1, b1, w2, b2, w3, b3))
    nh = HY_N1 // 2
    nc = HY_ORDER * HY_WIDTH
    h3 = h_raw.reshape(nh, HY_N2, 2 * nc)
    fwd1, fwd2, _, _ = _hyena_dft_constants(seq)
    fwd1 = _split_bf16(fwd1)
    fwd2 = _split_bf16(fwd2)
    scb = HY_SPEC_CB
    ncb = nc // scb
    return pl.pallas_call(
        _hy_spectrum_kernel,
        out_shape=jax.ShapeDtypeStruct((2, HY_N1, HY_N2 // 2, nc), F32),
        grid=(ncb,),
        in_specs=[pl.BlockSpec((nh, HY_N2, scb), lambda c: (0, 0, c)),
                  pl.BlockSpec((nh, HY_N2, scb), lambda c: (0, 0, ncb + c)),
                  pl.BlockSpec((1, scb), lambda c: (0, c)),
                  pl.BlockSpec((1, scb), lambda c: (0, ncb + c)),
                  _const_spec(fwd1.shape), _const_spec(fwd2.shape)],
        out_specs=pl.BlockSpec((2, HY_N1, HY_N2 // 2, scb), lambda c: (0, 0, 0, c)),
        scratch_shapes=[pltpu.VMEM((HY_N1, HY_N2, scb), F32), pltpu.VMEM((HY_N1, HY_N2, scb), F32)],
        compiler_params=_cparams(1, VMEM_LIMIT),
        name="hyena_filter_spectrum",
    )(h3, h3, s, s, fwd1, fwd2)


def _short_conv_chunk(ref, c, n_chunks, w_ref, b_ref):
    per = TM // HY_N2
    cur = ref[0, pl.ds(per * c, per)]
    cb = cur.shape[-1]
    cur = cur.reshape(TM, cb)
    prev = ref[0, jnp.maximum(per * c - 1, 0), pl.ds(HY_N2 - SUBLANES, SUBLANES), :][SUBLANES - 1:SUBLANES]
    nxt = ref[0, jnp.minimum(per * c + per, per * n_chunks - 1), pl.ds(0, SUBLANES), :][0:1]
    prev = jnp.where(c > 0, prev, 0.0)
    nxt = jnp.where(c < n_chunks - 1, nxt, 0.0)
    rowi = lax.broadcasted_iota(jnp.int32, (TM, cb), 0)
    dn = jnp.where(rowi == 0, prev, pltpu.roll(cur, 1, 0))
    up = jnp.where(rowi == TM - 1, nxt, pltpu.roll(cur, TM - 1, 0))
    return b_ref[...] + w_ref[0:1, :] * dn + w_ref[1:2, :] * cur + w_ref[2:3, :] * up


def _hy_conv_kernel(*refs, conv_y):
    if conv_y:
        (y_ref, g_ref, h_ref, bias_ref, wy_ref, by_ref, wg_ref, bg_ref, fwd1_ref, fwd2_ref, inv2_ref, inv1_ref,
         o_ref, s_re, s_im, gs_ref, us_ref) = refs
    else:
        (y_ref, g_ref, h_ref, bias_ref, wg_ref, bg_ref, fwd1_ref, fwd2_ref, inv2_ref, inv1_ref,
         o_ref, s_re, s_im, gs_ref) = refs
    grp = pl.program_id(2)
    last = pl.num_programs(2) - 1
    nh = HY_N1 // 2
    per = TM // HY_N2
    n_chunks = nh // per
    cb = o_ref.shape[-1]
    kh = HY_N2 // 2

    @pl.when(grp == 0)
    def _():
        def pre(c, carry):
            gs_ref[pl.ds(per * c, per)] = _short_conv_chunk(g_ref, c, n_chunks, wg_ref, bg_ref).reshape(per, HY_N2, cb)
            if conv_y:
                us_ref[pl.ds(per * c, per)] = _short_conv_chunk(y_ref, c, n_chunks, wy_ref, by_ref).reshape(
                    per, HY_N2, cb)
            return carry
        lax.fori_loop(0, n_chunks, pre, 0)

    if conv_y:
        u_at = lambda r0: us_ref[:, pl.ds(r0, HY_SLAB), :]
    else:
        u_at = lambda r0: y_ref[0, :, pl.ds(r0, HY_SLAB), :]

    _odft_stage1(u_at, lambda x: _dot(fwd1_ref[0], x.astype(BF16)), s_re, s_im)

    def middle(k1, carry):
        a = jnp.concatenate([s_re[k1], s_im[k1]], axis=0).astype(BF16)
        x = _dot(fwd2_ref[k1], a)
        xr, xi = x[:kh], x[kh:]
        hr, hi = h_ref[0, k1], h_ref[1, k1]
        y = jnp.concatenate([xr * hr - xi * hi, xr * hi + xi * hr], axis=0).astype(BF16)
        bm = _dot(inv2_ref[k1], y)
        s_re[k1] = bm[:HY_N2]
        s_im[k1] = bm[HY_N2:]
        return carry

    lax.fori_loop(0, HY_KG, middle, 0, unroll=4)

    def partial_conv(r0):
        slab = jnp.concatenate([s_re[:, pl.ds(r0, HY_SLAB), :].reshape(HY_KG * HY_SLAB, cb),
                                s_im[:, pl.ds(r0, HY_SLAB), :].reshape(HY_KG * HY_SLAB, cb)], axis=0).astype(BF16)
        return _dot(inv1_ref[0], slab).reshape(nh, HY_SLAB, cb)

    def post_loop(fn):
        def post(j, carry):
            r0 = pl.multiple_of(j * HY_SLAB, HY_SLAB)
            o_ref[0, :, pl.ds(r0, HY_SLAB), :] = fn(r0, partial_conv(r0))
            return carry
        lax.fori_loop(0, HY_N2 // HY_SLAB, post, 0, unroll=2)

    @pl.when(grp == 0)
    def _():
        post_loop(lambda r0, part: part)

    @pl.when((grp > 0) & (grp < last))
    def _():
        post_loop(lambda r0, part: o_ref[0, :, pl.ds(r0, HY_SLAB), :] + part)

    @pl.when(grp == last)
    def _():
        post_loop(lambda r0, part: gs_ref[:, pl.ds(r0, HY_SLAB), :]
                  * (o_ref[0, :, pl.ds(r0, HY_SLAB), :] + part + bias_ref[...] * u_at(r0)))


def _hyena_order(y4, y_col0, z4, gate_col0, hspec, order, hy_bias, short_w, short_b, consts, conv_y, rows_out):
    bsz = z4.shape[0]
    nh = HY_N1 // 2
    ncb = HY_WIDTH // HY_CB
    ngrp = HY_N1 // HY_KG
    fwd1, fwd2, inv2, inv1 = consts
    blk4 = lambda off: pl.BlockSpec((1, nh, HY_N2, HY_CB), lambda c, b, g: (b, 0, 0, off + c))
    rowspec = lambda rows, off: pl.BlockSpec((rows, HY_CB), lambda c, b, g: (0, off + c))
    grouped = lambda shp: pl.BlockSpec(shp, lambda c, b, g: (g, 0, 0))
    in_specs = [blk4(y_col0), blk4(gate_col0),
                pl.BlockSpec((2, HY_KG, HY_N2 // 2, HY_CB), lambda c, b, g: (0, g, 0, order * ncb + c)),
                rowspec(1, 0)]
    args = [y4, z4, hspec, hy_bias.reshape(1, -1)]
    if conv_y:
        in_specs += [rowspec(HY_SHORT, y_col0), rowspec(1, y_col0)]
        args += [short_w, short_b[None, :]]
    in_specs += [rowspec(HY_SHORT, gate_col0), rowspec(1, gate_col0)]
    args += [short_w, short_b[None, :]]
    in_specs += [grouped((1,) + fwd1.shape[1:]), grouped((HY_KG,) + fwd2.shape[1:]),
                 grouped((HY_KG,) + inv2.shape[1:]), grouped((1,) + inv1.shape[1:])]
    args += [fwd1, fwd2, inv2, inv1]
    scratch = [pltpu.VMEM((HY_KG, HY_N2, HY_CB), F32), pltpu.VMEM((HY_KG, HY_N2, HY_CB), F32),
               pltpu.VMEM((nh, HY_N2, HY_CB), F32)]
    if conv_y:
        scratch.append(pltpu.VMEM((nh, HY_N2, HY_CB), F32))
    return pl.pallas_call(
        functools.partial(_hy_conv_kernel, conv_y=conv_y),
        out_shape=jax.ShapeDtypeStruct((bsz, rows_out, HY_N2, HY_WIDTH), F32),
        grid=(ncb, bsz, ngrp),
        in_specs=in_specs,
        out_specs=pl.BlockSpec((1, nh, HY_N2, HY_CB), lambda c, b, g: (b, 0, 0, c)),
        scratch_shapes=scratch,
        compiler_params=_cparams(3, VMEM_LIMIT),
        name="hyena_conv%d" % order,
    )(*args)


def _hyena_latent(z_hy, hspec, short_w, short_b, hy_bias, with_ctx_rows):
    bsz, t, _ = z_hy.shape
    seq = HY_N1 * HY_N2 // 2
    fwd1, fwd2, inv2, inv1 = _hyena_dft_constants(seq)
    ngrp = HY_N1 // HY_KG
    rows = HY_KG * HY_SLAB
    fwd1 = fwd1.reshape(2, ngrp, rows, -1).transpose(1, 0, 2, 3).reshape(ngrp, 2 * rows, -1)
    inv1 = inv1.reshape(-1, 2, ngrp, rows).transpose(2, 0, 1, 3).reshape(ngrp, -1, 2 * rows)
    consts = tuple(jnp.asarray(m, BF16) for m in (fwd1, fwd2, inv2, inv1))
    z4 = z_hy.reshape(bsz, t // HY_N2, HY_N2, 3 * HY_WIDTH)
    ncb = HY_WIDTH // HY_CB
    nh = seq // HY_N2
    y1 = _hyena_order(z4, 2 * ncb, z4, 0, hspec, 0, hy_bias[0], short_w, short_b, consts, True, nh)
    rows_out = t // HY_N2 if with_ctx_rows else nh
    y2 = _hyena_order(y1, 0, z4, ncb, hspec, 1, hy_bias[1], short_w, short_b, consts, False, rows_out)
    return y2.reshape(bsz, rows_out * HY_N2, HY_WIDTH)


def _hy_ctx_spectrum_kernel(h_ref, s_ref, fwd_ref, o_ref):
    lc = h_ref.shape[0]
    nc = HY_ORDER * HY_WIDTH
    hp = lambda a, b: jnp.dot(a, b, preferred_element_type=F32, precision=HIGHEST)
    h = h_ref[...]
    rowi = lax.broadcasted_iota(jnp.int32, (lc, nc), 0)
    xf = hp(fwd_ref[...], h[:, :nc])
    xb = hp(fwd_ref[...], jnp.where(rowi == 0, 0.0, h[:, nc:]))
    inv_norm = 1.0 / (s_ref[:, :nc] + s_ref[:, nc:])
    o_ref[0] = (xf[:lc] + xb[:lc]) * inv_norm
    o_ref[1] = (xf[lc:] - xb[lc:]) * inv_norm


def _hy_ctx_conv_kernel(x1_ref, x2_ref, v_ref, h_ref, bias_ref, w_ref, b_ref, fwd_ref, inv_ref, _, o_ref):
    lc = o_ref.shape[1]
    rowi = lax.broadcasted_iota(jnp.int32, (lc, HY_WIDTH), 0)

    def short(ref, part):
        cur = ref[0].reshape(lc, HY_WIDTH)
        sl = slice(part * HY_WIDTH, (part + 1) * HY_WIDTH)
        dn = jnp.where(rowi == 0, 0.0, pltpu.roll(cur, 1, 0))
        up = jnp.where(rowi == lc - 1, 0.0, pltpu.roll(cur, lc - 1, 0))
        return b_ref[:, sl] + w_ref[0:1, sl] * dn + w_ref[1:2, sl] * cur + w_ref[2:3, sl] * up

    y = short(v_ref, 2)
    for order, gref in enumerate((x1_ref, x2_ref)):
        sl = slice(order * HY_WIDTH, (order + 1) * HY_WIDTH)
        x = _dot(fwd_ref[...], y.astype(BF16))
        xr, xi = x[:lc], x[lc:]
        hr, hi = h_ref[0, :, sl], h_ref[1, :, sl]
        prod = jnp.concatenate([xr * hr - xi * hi, xr * hi + xi * hr], axis=0).astype(BF16)
        conv = _dot(inv_ref[...], prod)
        y = short(gref, order) * (conv + bias_ref[order:order + 1, :] * y)
    o_ref[0] = y


def _hyena_ctx(z_hy, y_lat, filt_w, short_w, short_b, hy_bias):
    bsz, t, _ = z_hy.shape
    seq = HY_N1 * HY_N2 // 2
    lc = t - seq
    per = lc // HY_N2
    h_raw, s = _hyena_filters_raw(lc, filt_w)
    fwd, inv = _hyena_ctx_dft_constants(lc)
    nc = HY_ORDER * HY_WIDTH
    full = lambda shp: pl.BlockSpec(shp, lambda *_: (0,) * len(shp))
    hspec = pl.pallas_call(
        _hy_ctx_spectrum_kernel,
        out_shape=jax.ShapeDtypeStruct((2, lc, nc), F32),
        grid=(1,),
        in_specs=[full(h_raw.shape), full(s.shape), full(fwd.shape)],
        out_specs=full((2, lc, nc)),
        compiler_params=_cparams(1),
        name="hyena_ctx_spectrum",
    )(h_raw, s, jnp.asarray(fwd, F32))
    z4 = z_hy.reshape(bsz, t // HY_N2, HY_N2, 3 * HY_WIDTH)
    blk = lambda part: pl.BlockSpec((1, per, HY_N2, HY_WIDTH), lambda b: (b, seq // lc, 0, part))
    return pl.pallas_call(
        _hy_ctx_conv_kernel,
        out_shape=jax.ShapeDtypeStruct((bsz, t, HY_WIDTH), F32),
        grid=(bsz,),
        in_specs=[blk(0), blk(1), blk(2), full((2, lc, nc)), full((HY_ORDER, HY_WIDTH)),
                  full((HY_SHORT, 3 * HY_WIDTH)), full((1, 3 * HY_WIDTH)), full(fwd.shape), full(inv.shape),
                  pl.BlockSpec(memory_space=pl.ANY)],
        out_specs=pl.BlockSpec((1, lc, HY_WIDTH), lambda b: (b, seq // lc, 0)),
        input_output_aliases={9: 0},
        compiler_params=_cparams(1),
        name="hyena_ctx_conv",
    )(z4, z4, z4, hspec, hy_bias, short_w, short_b[None, :], jnp.asarray(fwd, BF16), jnp.asarray(inv, BF16), y_lat)


def _merge_kernel(x_ref, mod_ref, ymla_ref, of_ref, ob_ref, gr_ref, yhy_ref, zg_ref, on_ref,
                  wm_ref, wgl_ref, wh_ref, wo_ref, o_ref):
    o = of_ref[0, 0].astype(F32) + ob_ref[0, 0].astype(F32)
    silu = gr_ref[0].astype(F32)
    parts = []
    for hd in range(GLA_HEADS):
        sl = slice(hd * GLA_HV, (hd + 1) * GLA_HV)
        parts.append((_rms(o[:, sl]) * on_ref[...] * silu[:, sl]).astype(BF16))
    y_gla = jnp.concatenate(parts, axis=1)
    zg = zg_ref[0].astype(F32)
    d = x_ref.shape[-1]
    m = zg[:, 0:d] * _dot(ymla_ref[0], wm_ref[...])
    m = m + zg[:, d:2 * d] * _dot(y_gla, wgl_ref[...])
    m = m + zg[:, 2 * d:3 * d] * _dot(yhy_ref[0].astype(BF16), wh_ref[...])
    out = _dot(m.astype(BF16), wo_ref[...])
    o_ref[0] = x_ref[0] + mod_ref[0, 2:3, :] * out


def _mod_spec(d, n_lat_tiles, bsz):
    return pl.BlockSpec((1, 6, d), lambda b, i: (jnp.where(i < n_lat_tiles, b, bsz), 0, 0))


def _merge(xc, mod_l, y_mla, o_gla, gr, y_hy, gate, out_norm, w_o_mla, w_o_gla, w_o_hy, w_out, n_tiles, n_lat_tiles):
    bsz, t, d = xc.shape
    tile = lambda w: pl.BlockSpec((1, TM, w), lambda b, i: (b, i, 0))
    dirspec = lambda dr: pl.BlockSpec((1, 1, TM, GLA_DV), lambda b, i: (b, dr, i, 0))
    bf = lambda w: w.astype(BF16)
    return pl.pallas_call(
        _merge_kernel,
        out_shape=jax.ShapeDtypeStruct((bsz, n_tiles * TM, d), F32),
        grid=(bsz, n_tiles),
        in_specs=[tile(d), _mod_spec(d, n_lat_tiles, bsz), tile(MLA_OUT), dirspec(0), dirspec(1), tile(GLA_DV),
                  tile(HY_WIDTH), tile(3 * d), pl.BlockSpec((1, GLA_HV), lambda b, i: (0, 0)),
                  _const_spec(w_o_mla.shape), _const_spec(w_o_gla.shape), _const_spec(w_o_hy.shape),
                  _const_spec(w_out.shape)],
        out_specs=tile(d),
        compiler_params=_cparams(2, VMEM_LIMIT),
        name="merge",
    )(xc, mod_l, y_mla, o_gla, o_gla, gr, y_hy, gate, out_norm[None, :], bf(w_o_mla), bf(w_o_gla), bf(w_o_hy),
      bf(w_out))


FF_CHUNK = 1024


def _mlp_kernel(x_ref, mod_ref, g2_ref, w1_ref, w2_ref, fg_ref, o_ref, *, final):
    x = x_ref[0]
    h = (_rms(x) * g2_ref[...] * (1.0 + mod_ref[0, 4:5, :]) + mod_ref[0, 3:4, :]).astype(BF16)
    acc = jnp.zeros(x.shape, F32)
    for j in range(w1_ref.shape[1] // FF_CHUNK):
        a = jnp.maximum(_dot(h, w1_ref[:, j * FF_CHUNK:(j + 1) * FF_CHUNK]), 0.0)
        acc = acc + _dot((a * a).astype(BF16), w2_ref[j * FF_CHUNK:(j + 1) * FF_CHUNK, :])
    xn = x + mod_ref[0, 5:6, :] * acc
    if final:
        xn = _rms(xn) * fg_ref[...]
    o_ref[0] = xn


def _mlp(xc, mod_l, g2, w1, w2, final_g, n_tiles, n_lat_tiles, final):
    bsz, t, d = xc.shape
    tile = pl.BlockSpec((1, TM, d), lambda b, i: (b, i, 0))
    row = pl.BlockSpec((1, d), lambda b, i: (0, 0))
    t_out = n_tiles * TM if final else t
    return pl.pallas_call(
        functools.partial(_mlp_kernel, final=final),
        out_shape=jax.ShapeDtypeStruct((bsz, t_out, d), F32),
        grid=(bsz, n_tiles),
        in_specs=[tile, _mod_spec(d, n_lat_tiles, bsz), row, _const_spec(w1.shape), _const_spec(w2.shape), row],
        out_specs=tile,
        compiler_params=_cparams(2, VMEM_LIMIT),
        name="mlp",
    )(xc, mod_l, g2[None, :], w1.astype(BF16), w2.astype(BF16), final_g[None, :])


def kernel(x, c, ctx, c_ctx, ada_w, ada_b, norm1_g, norm2_g, w_in, mla_q_norm, mla_w_uq, mla_kv_norm, mla_w_ukv, gla_w_a2, gla_b_a, gla_out_norm, hy_short_w, hy_short_b, hy_f_w1, hy_f_b1, hy_f_w2, hy_f_b2, hy_f_w3, hy_f_b3, hy_bias, w_o_mla, w_o_gla, w_o_hy, w_out, ff_w1, ff_w2, final_norm_g):
    bsz, seq, d = x.shape
    ctx_len = ctx.shape[1]
    n_lat = seq // TM
    n_all = (seq + ctx_len) // TM
    xc = jnp.concatenate([x, ctx], axis=1)
    cc = jnp.zeros((16, d), F32).at[:bsz].set(c).at[bsz].set(c_ctx)
    mod = _modulation(cc, ada_w, ada_b).reshape(DEPTH, 16, 6, d)
    cos, sin = _rope_tables(seq, ctx_len)
    for l in range(DEPTH):
        last = l == DEPTH - 1
        n_tiles = n_lat if last else n_all
        weights = _prep_inproj_weights(w_in[l], mla_w_uq[l], mla_w_ukv[l], gla_w_a2[l], gla_b_a[l])
        q, k, v, gq, gk, gv, gr, glog, z_hy, z_gate = _inproj(xc, mod[l], norm1_g[l], weights, mla_q_norm[l],
                                                             mla_kv_norm[l], cos, sin, n_lat)
        y_mla = _attention(q, k, v, seq, not last)
        o_gla = _gla(gq, gk, gv, glog, n_lat)
        filt_w = (hy_f_w1[l], hy_f_b1[l], hy_f_w2[l], hy_f_b2[l], hy_f_w3[l], hy_f_b3[l])
        hspec = _hyena_filter_spectrum(seq, *filt_w)
        y_hy = _hyena_latent(z_hy, hspec, hy_short_w[l], hy_short_b[l], hy_bias[l], not last)
        if not last:
            y_hy = _hyena_ctx(z_hy, y_hy, filt_w, hy_short_w[l], hy_short_b[l], hy_bias[l])
        xc = _merge(xc, mod[l], y_mla, o_gla, gr, y_hy, z_gate, gla_out_norm[l], w_o_mla[l], w_o_gla[l], w_o_hy[l],
                    w_out[l], n_tiles, n_lat)
        xc = _mlp(xc, mod[l], norm2_g[l], ff_w1[l], ff_w2[l], final_norm_g, n_tiles, n_lat, last)
    return xc
```

```python
import functools
import math

import numpy as np
import jax
import jax.numpy as jnp
from jax import lax
from jax.experimental import pallas as pl
from jax.experimental.pallas import tpu as pltpu

F32 = jnp.float32
BF16 = jnp.bfloat16
HIGHEST = lax.Precision.HIGHEST
LOG2E = 1.4426950408889634

D_MODEL = 1024
DEPTH = 2
GRID_W = 64
EPS = 1e-6
MLA_HEADS = 8
MLA_NOPE = 64
MLA_ROPE = 32
MLA_V = 64
MLA_Q_LORA = 256
MLA_KV_LORA = 128
MLA_SCALE = (MLA_NOPE + MLA_ROPE) ** -0.5
ROPE_BASE = 10000.0
GLA_HEADS = 4
GLA_DK = 256
GLA_DV = 512
GLA_HK = GLA_DK // GLA_HEADS
GLA_HV = GLA_DV // GLA_HEADS
GLA_GATE_RANK = 16
GLA_TAU = 16.0
HY_WIDTH = 512
HY_ORDER = 2
HY_SHORT = 3
HY_BANDS = 16
HY_POS_DIM = 1 + 2 * HY_BANDS
HY_FILTER_HIDDEN = 64
HY_FAST_DECAY = 0.3
HY_SLOW_DECAY = 1.5
HY_DECAY_TARGET = 1e-2
D_FF = 4 * D_MODEL
MLA_OUT = MLA_HEADS * MLA_V
IN_SIZES = (MLA_Q_LORA, MLA_KV_LORA, MLA_ROPE, GLA_DK, GLA_DK, GLA_DV, GLA_DV, GLA_GATE_RANK, GLA_GATE_RANK,
            (HY_ORDER + 1) * HY_WIDTH, 3 * D_MODEL)

LANES = 128
SUBLANES = 8
TM = 256
HEAD_SLOT = 128
VMEM_LIMIT = 56 * 1024 * 1024


def _cparams(n_axes, vmem=None):
    return pltpu.CompilerParams(dimension_semantics=("arbitrary",) * n_axes, vmem_limit_bytes=vmem)


def _const_spec(shape):
    nd = len(shape)
    return pl.BlockSpec(shape, lambda *_: (0,) * nd, pipeline_mode=pl.Buffered(1))


def _rms(x):
    return x * lax.rsqrt(jnp.mean(x * x, axis=-1, keepdims=True) + EPS)


def _sigmoid(x):
    return 1.0 / (1.0 + jnp.exp(-x))


def _dot(a, b):
    return jnp.dot(a, b, preferred_element_type=F32)


def _dot_nt(a, b):
    return lax.dot_general(a, b, (((1,), (1,)), ((), ())), preferred_element_type=F32)


def _mod_kernel(cc_ref, w_ref, b_ref, o_ref):
    s = cc_ref[...]
    s = s * _sigmoid(s)
    o_ref[0] = jnp.dot(s, w_ref[0], preferred_element_type=F32, precision=HIGHEST) + b_ref[0]


def _modulation(cc, ada_w, ada_b):
    tn = 1536
    n6 = ada_w.shape[-1]
    return pl.pallas_call(
        _mod_kernel,
        out_shape=jax.ShapeDtypeStruct((DEPTH, 16, n6), F32),
        grid=(DEPTH, n6 // tn),
        in_specs=[
            pl.BlockSpec((16, D_MODEL), lambda l, j: (0, 0)),
            pl.BlockSpec((1, D_MODEL, tn), lambda l, j: (l, 0, j)),
            pl.BlockSpec((1, 1, tn), lambda l, j: (l, 0, j)),
        ],
        out_specs=pl.BlockSpec((1, 16, tn), lambda l, j: (l, 0, j)),
        compiler_params=_cparams(2),
        name="modulation",
    )(cc, ada_w, ada_b.reshape(DEPTH, 1, n6))


W_A = 768
W_G = 2 * GLA_DK + 2 * GLA_DV


def _inproj_kernel(x_ref, mod_ref, g1_ref, wa_ref, wg_ref, wh_ref, wz_ref, qn_ref, kvn_ref, wuq_ref, wukv_ref,
                   wa2_ref, ba_ref, cos_ref, sin_ref,
                   q_out, k_out, v_out, gq_out, gk_out, gv_out, gr_out, glog_out, hy_out, gate_out):
    x = x_ref[0]
    shift = mod_ref[0, 0:1, :]
    scale = mod_ref[0, 1:2, :]
    h = (_rms(x) * g1_ref[...] * (1.0 + scale) + shift).astype(BF16)

    za = _dot(h, wa_ref[...])
    cos = cos_ref[...]
    sin = sin_ref[...]

    cqn = (_rms(za[:, 0:256]) * qn_ref[...]).astype(BF16)
    qab = _dot(cqn, wuq_ref[...])
    nq = MLA_HEADS * HEAD_SLOT
    for hd in range(MLA_HEADS):
        sl = slice(hd * HEAD_SLOT, (hd + 1) * HEAD_SLOT)
        qa = qab[:, hd * HEAD_SLOT:(hd + 1) * HEAD_SLOT]
        qb = qab[:, nq + hd * HEAD_SLOT:nq + (hd + 1) * HEAD_SLOT]
        q_out[0, :, sl] = ((qa * cos + qb * sin) * (MLA_SCALE * LOG2E)).astype(BF16)

    ckvn = (_rms(za[:, 256:384]) * kvn_ref[...]).astype(BF16)
    kv = _dot(ckvn, wukv_ref[...])
    krot = za[:, 384:512] * cos + za[:, 512:640] * sin
    for hd in range(MLA_HEADS):
        sl = slice(hd * HEAD_SLOT, (hd + 1) * HEAD_SLOT)
        k_out[0, :, sl] = (kv[:, sl] + krot).astype(BF16)
    ones_hi = (lax.broadcasted_iota(jnp.int32, (1, HEAD_SLOT), 1) >= MLA_V).astype(F32)
    for hd in range(MLA_HEADS):
        sl = slice(hd * HEAD_SLOT, (hd + 1) * HEAD_SLOT)
        v_out[0, :, sl] = (kv[:, nq + hd * HEAD_SLOT:nq + (hd + 1) * HEAD_SLOT] + ones_hi).astype(BF16)

    xg = _dot(za[:, 640:768].astype(BF16), wa2_ref[...]) + ba_ref[...]
    glog_out[0] = (jnp.minimum(xg, 0.0) - jnp.log(1.0 + jnp.exp(-jnp.abs(xg)))) * (1.0 / GLA_TAU)

    zg = _dot(h, wg_ref[...])
    gq_out[0] = (zg[:, 0:GLA_DK] * (GLA_HK ** -0.5)).astype(BF16)
    gk_out[0] = zg[:, GLA_DK:2 * GLA_DK].astype(BF16)
    gv_out[0] = zg[:, 2 * GLA_DK:2 * GLA_DK + GLA_DV].astype(BF16)
    gr = zg[:, 2 * GLA_DK + GLA_DV:]
    gr_out[0] = (gr * _sigmoid(gr)).astype(BF16)

    hy_out[0] = _dot(h, wh_ref[...])
    gate_out[0] = _sigmoid(_dot(h, wz_ref[...])).astype(BF16)


def _rope_partner(w):
    a = MLA_ROPE // 4
    perm = np.concatenate([np.arange(a, 2 * a), np.arange(0, a), np.arange(3 * a, 4 * a), np.arange(2 * a, 3 * a)])
    sign = np.concatenate([-np.ones(a), np.ones(a), -np.ones(a), np.ones(a)]).astype(np.float32)
    return w[:, perm] * sign


def _prep_inproj_weights(w_in, mla_w_uq, mla_w_ukv, gla_w_a2, gla_b_a):
    offs = np.concatenate([[0], np.cumsum(IN_SIZES)])
    seg = [w_in[:, offs[i]:offs[i + 1]] for i in range(len(IN_SIZES))]
    w_cq, w_ckv, w_kr, w_gq, w_gk, w_gv, w_gr, w_af, w_ab, w_hy, w_gate = seg
    d = w_in.shape[0]
    z = lambda n: jnp.zeros((d, n), w_in.dtype)
    kr_tile = jnp.concatenate([z(MLA_NOPE), w_kr, z(HEAD_SLOT - MLA_NOPE - MLA_ROPE)], axis=1)
    krp_tile = jnp.concatenate([z(MLA_NOPE), _rope_partner(w_kr), z(HEAD_SLOT - MLA_NOPE - MLA_ROPE)], axis=1)
    a_tile = jnp.concatenate([w_af, w_ab, z(LANES - 2 * GLA_GATE_RANK)], axis=1)
    wa = jnp.concatenate([w_cq, w_ckv, kr_tile, krp_tile, a_tile], axis=1)
    wg = jnp.concatenate([w_gq, w_gk, w_gv, w_gr], axis=1)

    dh = MLA_NOPE + MLA_ROPE
    zq = lambda n: jnp.zeros((MLA_Q_LORA, n), w_in.dtype)
    plain, partner = [], []
    for hd in range(MLA_HEADS):
        blk = mla_w_uq[:, hd * dh:(hd + 1) * dh]
        plain += [blk, zq(HEAD_SLOT - dh)]
        partner += [zq(MLA_NOPE), _rope_partner(blk[:, MLA_NOPE:]), zq(HEAD_SLOT - dh)]
    wuq = jnp.concatenate(plain + partner, axis=1)

    zk = jnp.zeros((MLA_KV_LORA, HEAD_SLOT - MLA_NOPE), w_in.dtype)
    kcols, vcols = [], []
    for hd in range(MLA_HEADS):
        blk = mla_w_ukv[:, hd * (MLA_NOPE + MLA_V):(hd + 1) * (MLA_NOPE + MLA_V)]
        kcols += [blk[:, :MLA_NOPE], zk]
        vcols += [blk[:, MLA_NOPE:], zk]
    wukv = jnp.concatenate(kcols + vcols, axis=1)

    wa2 = jnp.zeros((LANES, 2 * GLA_DK), w_in.dtype)
    wa2 = wa2.at[0:GLA_GATE_RANK, 0:GLA_DK].set(gla_w_a2[0])
    wa2 = wa2.at[GLA_GATE_RANK:2 * GLA_GATE_RANK, GLA_DK:].set(gla_w_a2[1])
    ba = jnp.concatenate([gla_b_a[0], gla_b_a[1]])[None, :]
    bf = lambda t: t.astype(BF16)
    return bf(wa), bf(wg), bf(w_hy), bf(w_gate), bf(wuq), bf(wukv), bf(wa2), ba


def _rope_tables(seq, ctx_len):
    rows = seq // GRID_W
    row = np.repeat(np.arange(rows, dtype=np.float64), GRID_W)
    col = np.tile(np.arange(GRID_W, dtype=np.float64), rows)
    a = MLA_ROPE // 4
    inv = ROPE_BASE ** (-np.arange(a, dtype=np.float64) / a)
    ang_r = row[:, None] * inv
    ang_c = col[:, None] * inv
    cos32 = np.concatenate([np.cos(ang_r), np.cos(ang_r), np.cos(ang_c), np.cos(ang_c)], axis=1)
    sin32 = np.concatenate([np.sin(ang_r), np.sin(ang_r), np.sin(ang_c), np.sin(ang_c)], axis=1)
    pad_r = HEAD_SLOT - MLA_NOPE - MLA_ROPE
    cos = np.concatenate([np.ones((seq, MLA_NOPE)), cos32, np.zeros((seq, pad_r))], axis=1)
    sin = np.concatenate([np.zeros((seq, MLA_NOPE)), sin32, np.zeros((seq, pad_r))], axis=1)
    cos_c = np.concatenate([np.ones((ctx_len, MLA_NOPE + MLA_ROPE)), np.zeros((ctx_len, pad_r))], axis=1)
    sin_c = np.zeros((ctx_len, HEAD_SLOT))
    return (jnp.asarray(np.concatenate([cos, cos_c], axis=0), F32),
            jnp.asarray(np.concatenate([sin, sin_c], axis=0), F32))


def _inproj(xc, mod_l, g1, weights, q_norm, kv_norm, cos, sin, n_lat_tiles):
    bsz, t, d = xc.shape
    nt = t // TM
    wa, wg, wh, wz, wuq, wukv, wa2, ba = weights
    tile = lambda w: pl.BlockSpec((1, TM, w), lambda b, i: (b, i, 0))
    row = lambda w: pl.BlockSpec((1, w), lambda b, i: (0, 0))
    mod_spec = pl.BlockSpec((1, 6, d), lambda b, i: (jnp.where(i < n_lat_tiles, b, bsz), 0, 0))
    tab = pl.BlockSpec((TM, HEAD_SLOT), lambda b, i: (i, 0))
    nq = MLA_HEADS * HEAD_SLOT
    sds = lambda w, dt: jax.ShapeDtypeStruct((bsz, t, w), dt)
    out_shape = (sds(nq, BF16), sds(nq, BF16), sds(nq, BF16), sds(GLA_DK, BF16), sds(GLA_DK, BF16),
                 sds(GLA_DV, BF16), sds(GLA_DV, BF16), sds(2 * GLA_DK, F32), sds(3 * HY_WIDTH, F32),
                 sds(3 * D_MODEL, BF16))
    out_specs = (tile(nq), tile(nq), tile(nq), tile(GLA_DK), tile(GLA_DK), tile(GLA_DV), tile(GLA_DV),
                 tile(2 * GLA_DK), tile(3 * HY_WIDTH), tile(3 * D_MODEL))
    return pl.pallas_call(
        _inproj_kernel,
        out_shape=out_shape,
        grid=(bsz, nt),
        in_specs=[tile(d), mod_spec, row(d), _const_spec(wa.shape), _const_spec(wg.shape), _const_spec(wh.shape),
                  _const_spec(wz.shape), row(MLA_Q_LORA), row(MLA_KV_LORA), _const_spec(wuq.shape),
                  _const_spec(wukv.shape), _const_spec(wa2.shape), row(2 * GLA_DK), tab, tab],
        out_specs=out_specs,
        compiler_params=_cparams(2, VMEM_LIMIT),
        name="inproj",
    )(xc, mod_l, g1[None, :], wa, wg, wh, wz, q_norm[None, :], kv_norm[None, :], wuq, wukv, wa2, ba, cos, sin)


ATT_TK = 1024


ATT_TQ = 1024


def _attn_kernel(*refs, chunks, aliased):
    if aliased:
        q_ref, k_ref, v_ref, _, o_ref, m_ref, acc_ref = refs
    else:
        q_ref, k_ref, v_ref, o_ref, m_ref, acc_ref = refs
    tq = q_ref.shape[1]
    m_ref[...] = jnp.full(m_ref.shape, -jnp.inf, F32)
    acc_ref[...] = jnp.zeros(acc_ref.shape, F32)
    for r0, size in chunks:
        k = k_ref[0, pl.ds(r0, size), :]
        v = v_ref[0, pl.ds(r0, size), :]
        for hd in range(2):
            sl = slice(hd * HEAD_SLOT, (hd + 1) * HEAD_SLOT)
            s = _dot_nt(q_ref[0, :, sl], k[:, sl])
            m_prev = m_ref[hd]
            m_new = jnp.maximum(m_prev, jnp.max(s, axis=1, keepdims=True))
            p = jnp.exp2((s - jnp.concatenate([m_new] * (size // LANES), axis=1)).astype(BF16))
            acc_ref[hd] = jnp.exp2(m_prev - m_new) * acc_ref[hd] + _dot(p, v[:, sl])
            m_ref[hd] = m_new
    a0 = acc_ref[0]
    a1 = acc_ref[1]
    lane = lax.broadcasted_iota(jnp.int32, (tq, HEAD_SLOT), 1)
    o0 = a0 / pltpu.roll(a0, MLA_V, 1)
    o1 = pltpu.roll(a1, MLA_V, 1) / a1
    o_ref[0] = jnp.where(lane < MLA_V, o0, o1).astype(o_ref.dtype)


def _attention(q, k, v, seq, with_ctx_queries):
    bsz, t, _ = q.shape
    ctx_len = t - seq
    assert seq % ATT_TK == 0 and seq % ATT_TQ == 0 and seq % ctx_len == 0 and 2 * MLA_V == HEAD_SLOT
    pair = 2 * HEAD_SLOT
    n_chunks = seq // ATT_TK
    chunks = tuple((j * ATT_TK, ATT_TK) for j in range(n_chunks - 1))
    chunks += (((n_chunks - 1) * ATT_TK, ATT_TK + ctx_len),)
    scratch = lambda tq: [pltpu.VMEM((2, tq, LANES), F32), pltpu.VMEM((2, tq, HEAD_SLOT), F32)]
    y = pl.pallas_call(
        functools.partial(_attn_kernel, chunks=chunks, aliased=False),
        out_shape=jax.ShapeDtypeStruct((bsz, t, MLA_OUT), BF16),
        grid=(bsz, MLA_HEADS // 2, seq // ATT_TQ),
        in_specs=[
            pl.BlockSpec((1, ATT_TQ, pair), lambda b, hp, i: (b, i, hp)),
            pl.BlockSpec((1, t, pair), lambda b, hp, i: (b, 0, hp)),
            pl.BlockSpec((1, t, pair), lambda b, hp, i: (b, 0, hp)),
        ],
        out_specs=pl.BlockSpec((1, ATT_TQ, HEAD_SLOT), lambda b, hp, i: (b, i, hp)),
        scratch_shapes=scratch(ATT_TQ),
        compiler_params=_cparams(3, VMEM_LIMIT),
        name="mla_attention",
    )(q, k, v)
    if not with_ctx_queries:
        return y
    cblk = seq // ctx_len
    ctx_rows = lambda w: pl.BlockSpec((1, ctx_len, w), lambda b, hp: (b, cblk, hp))
    return pl.pallas_call(
        functools.partial(_attn_kernel, chunks=((0, ctx_len),), aliased=True),
        out_shape=jax.ShapeDtypeStruct((bsz, t, MLA_OUT), BF16),
        grid=(bsz, MLA_HEADS // 2),
        in_specs=[ctx_rows(pair), ctx_rows(pair), ctx_rows(pair), pl.BlockSpec(memory_space=pl.ANY)],
        out_specs=ctx_rows(HEAD_SLOT),
        scratch_shapes=scratch(ctx_len),
        input_output_aliases={3: 0},
        compiler_params=_cparams(2),
        name="mla_attention_ctx",
    )(q, k, v, y)


GLA_LEVELS = int(math.log2(TM))
GLA_SAFE_SPAN = 60.0


def _gla_level_matrices():
    i = np.arange(TM)[:, None]
    t = np.arange(TM)[None, :]
    fwd = [(t <= i)]
    for lv in range(GLA_LEVELS):
        m = TM >> (lv + 1)
        lo = (i // m) * m
        later = ((i // m) % 2) == 1
        q_part = later & (t >= lo) & (t <= i)
        k_part = (~later) & (t > i) & (t <= lo + m - 1)
        fwd.append(q_part | k_part)
    fwd = np.concatenate(fwd, axis=0).astype(np.float32)
    nb = 1 + GLA_LEVELS
    bwd = fwd.reshape(nb, TM, TM)[:, ::-1, ::-1].reshape(nb * TM, TM)
    return np.stack([fwd, bwd])


def _gla_kernel(q_ref, k_ref, v_ref, g_ref, mall_ref, o_ref, s_ref, a_ref):
    d = pl.program_id(1)
    step = pl.program_id(2)

    @pl.when(step == 0)
    def _():
        s_ref[...] = jnp.zeros(s_ref.shape, F32)

    q = q_ref[0].astype(F32)
    k = k_ref[0].astype(F32)
    v = v_ref[0]
    g = g_ref[0]
    g2 = jnp.concatenate([g.astype(BF16), (g - g.astype(BF16).astype(F32)).astype(BF16)], axis=1)
    e2 = _dot(mall_ref[0, 0:TM, :], g2)
    g_cum = e2[:, :GLA_DK] + e2[:, GLA_DK:]
    g_tot = jnp.sum(g, axis=0, keepdims=True)

    row = lax.broadcasted_iota(jnp.int32, (TM, TM), 0)
    col = lax.broadcasted_iota(jnp.int32, (TM, TM), 1)
    lane_head = lax.broadcasted_iota(jnp.int32, (TM, GLA_DK), 1) // GLA_HK
    tok = lax.broadcasted_iota(jnp.int32, (TM, GLA_DK), 0)
    eye = row == col

    def stack_heads(t):
        return jnp.concatenate([jnp.where(lane_head == hd, t, 0.0) for hd in range(GLA_HEADS)], axis=0).astype(BF16)

    span = jnp.max(-g_tot)

    @pl.when(span < GLA_SAFE_SPAN)
    def _():
        res = _dot_nt(stack_heads(q * jnp.exp(g_cum)), (k * jnp.exp(-g_cum)).astype(BF16))
        seen = (col - row) * (1 - 2 * d) <= 0
        for hd in range(GLA_HEADS):
            a_ref[hd] = jnp.where(seen, res[hd * TM:(hd + 1) * TM], 0.0)

    @pl.when(span >= GLA_SAFE_SPAN)
    def _():
        e2l = _dot(mall_ref[0, TM:, :], g2)
        e_lv = e2l[:, :GLA_DK] + e2l[:, GLA_DK:]
        res = _dot_nt(stack_heads(q), k.astype(BF16))
        for hd in range(GLA_HEADS):
            a_ref[hd] = jnp.where(eye, res[hd * TM:(hd + 1) * TM], 0.0)
        for lv in range(GLA_LEVELS):
            m = TM >> (lv + 1)
            w = jnp.exp(e_lv[lv * TM:(lv + 1) * TM])
            bit = (tok // m) % 2
            q_act = bit != d
            qt = jnp.where(q_act, q * w, 0.0)
            kt = jnp.where(q_act, 0.0, k * w).astype(BF16)
            res = _dot_nt(stack_heads(qt), kt)
            if m == TM // 2:
                for hd in range(GLA_HEADS):
                    a_ref[hd] += res[hd * TM:(hd + 1) * TM]
            else:
                same = (row // (2 * m)) == (col // (2 * m))
                for hd in range(GLA_HEADS):
                    a_ref[hd] += jnp.where(same, res[hd * TM:(hd + 1) * TM], 0.0)

    s_old = s_ref[...]
    o_inter = _dot((q * jnp.exp(g_cum)).astype(BF16), s_old.astype(BF16))
    for hd in range(GLA_HEADS):
        sl = slice(hd * GLA_HV, (hd + 1) * GLA_HV)
        o_intra = _dot(a_ref[hd].astype(BF16), v[:, sl])
        o_ref[0, 0, :, sl] = (o_intra + o_inter[:, sl]).astype(o_ref.dtype)

    kdec_t = (k * jnp.exp(g_tot - g_cum)).T.astype(BF16)
    upd = _dot(kdec_t, v)
    same_head = (lax.broadcasted_iota(jnp.int32, (GLA_DK, GLA_DV), 0) // GLA_HK
                 == lax.broadcasted_iota(jnp.int32, (GLA_DK, GLA_DV), 1) // GLA_HV)
    dec_col = jnp.sum(jnp.where(eye, jnp.broadcast_to(jnp.exp(g_tot), (TM, GLA_DK)), 0.0), axis=1, keepdims=True)
    s_ref[...] = dec_col * s_old + jnp.where(same_head, upd, 0.0)


def _gla(gq, gk, gv, glog, n_lat_tiles):
    assert GLA_DK == TM
    bsz, t, _ = gq.shape
    nt = t // TM
    mall = jnp.asarray(_gla_level_matrices(), dtype=BF16)

    def tile_idx(d, s):
        return jnp.where(s == 0, n_lat_tiles, jnp.where(d == 0, s - 1, n_lat_tiles - s))

    tok = lambda w: pl.BlockSpec((1, TM, w), lambda b, d, s: (b, tile_idx(d, s), 0))
    return pl.pallas_call(
        _gla_kernel,
        out_shape=jax.ShapeDtypeStruct((bsz, 2, t, GLA_DV), BF16),
        grid=(bsz, 2, nt),
        in_specs=[tok(GLA_DK), tok(GLA_DK), tok(GLA_DV),
                  pl.BlockSpec((1, TM, GLA_DK), lambda b, d, s: (b, tile_idx(d, s), d)),
                  pl.BlockSpec((1,) + mall.shape[1:], lambda b, d, s: (d, 0, 0))],
        out_specs=pl.BlockSpec((1, 1, TM, GLA_DV), lambda b, d, s: (b, d, tile_idx(d, s), 0)),
        scratch_shapes=[pltpu.VMEM((GLA_DK, GLA_DV), F32), pltpu.VMEM((GLA_HEADS, TM, TM), F32)],
        compiler_params=_cparams(3),
        name="gla_scan",
    )(gq, gk, gv, glog, mall)


HY_N1 = 64
HY_N2 = 128
HY_SLAB = SUBLANES
HY_CB = 256
HY_KG = 16
HY_SPEC_CB = 128


def _hyena_dft_constants(seq):
    n = 2 * seq
    assert n == HY_N1 * HY_N2
    nh = HY_N1 // 2
    kh = HY_N2 // 2
    eye = np.eye(HY_SLAB)
    k1 = np.arange(HY_N1)
    th = 2 * np.pi * np.outer(k1 + 0.5, np.arange(nh)) / HY_N1
    fwd1 = np.concatenate([np.kron(np.cos(th), eye), np.kron(-np.sin(th), eye)], axis=0)
    inv1 = (2.0 / n) * np.concatenate([np.kron(np.cos(th).T, eye), np.kron(-np.sin(th).T, eye)], axis=1)
    nlo = np.arange(HY_N2)
    k2 = np.arange(kh)
    ph = 2 * np.pi * (k2[None, :, None] * nlo[None, None, :] / HY_N2
                      + (k1[:, None, None] + 0.5) * nlo[None, None, :] / n)
    c, s = np.cos(ph), np.sin(ph)
    fwd2 = np.concatenate([np.concatenate([c, s], axis=2), np.concatenate([-s, c], axis=2)], axis=1)
    ct, st = c.transpose(0, 2, 1), s.transpose(0, 2, 1)
    inv2 = np.concatenate([np.concatenate([ct, -st], axis=2), np.concatenate([st, ct], axis=2)], axis=1)
    return fwd1, fwd2, inv2, inv1


def _hyena_ctx_dft_constants(ctx_len):
    n = 2 * ctx_len
    th = 2 * np.pi * np.outer(np.arange(ctx_len) + 0.5, np.arange(ctx_len)) / n
    fwd = np.concatenate([np.cos(th), -np.sin(th)], axis=0)
    inv = (2.0 / n) * np.concatenate([np.cos(th).T, -np.sin(th).T], axis=1)
    return fwd, inv


def _hyena_features(length):
    pos = np.arange(length, dtype=np.float64)
    t = pos / max(length - 1, 1)
    f = np.linspace(1e-4, HY_BANDS - 1, HY_BANDS)
    ang = (2.0 * math.pi / length) * pos[:, None] * f
    feat = np.concatenate([t[:, None], np.cos(ang), np.sin(ang)], axis=-1)
    return jnp.asarray(np.pad(feat, ((0, 0), (0, LANES - HY_POS_DIM))), F32)


def _hy_filter_kernel(feat_ref, w1_ref, b1_ref, w2_ref, b2_ref, w3_ref, b3_ref, absd_ref, h_ref, s_ref):
    i = pl.program_id(0)
    feat = feat_ref[...]
    hp = lambda a, b: jnp.dot(a, b, preferred_element_type=F32, precision=HIGHEST)
    hdn = jnp.sin(hp(feat, w1_ref[...]) + b1_ref[...])
    hdn = jnp.sin(hp(hdn, w2_ref[...]) + b2_ref[...])
    h = hp(hdn, w3_ref[...]) + b3_ref[...]
    window = jnp.exp(-feat[:, 0:1] * absd_ref[...])
    h = h * jnp.concatenate([window] * (2 * HY_ORDER), axis=1)
    h_ref[...] = h

    @pl.when(i == 0)
    def _():
        s_ref[...] = jnp.zeros(s_ref.shape, F32)

    s_ref[...] += jnp.sum(jnp.abs(h), axis=0, keepdims=True)


def _hyena_filters_raw(length, filt_w):
    w1, b1, w2, b2, w3, b3 = filt_w
    nf = 2 * HY_ORDER * HY_WIDTH
    tr = min(length, 512)
    deltas = np.linspace(math.log(HY_DECAY_TARGET) / HY_FAST_DECAY, math.log(HY_DECAY_TARGET) / HY_SLOW_DECAY,
                         HY_WIDTH, dtype=np.float32)
    absd = jnp.asarray(np.abs(deltas))[None, :]
    w1p = jnp.pad(w1, ((0, LANES - HY_POS_DIM), (0, 0)))
    full = lambda shp: pl.BlockSpec(shp, lambda i: (0,) * len(shp))
    return pl.pallas_call(
        _hy_filter_kernel,
        out_shape=(jax.ShapeDtypeStruct((length, nf), F32), jax.ShapeDtypeStruct((1, nf), F32)),
        grid=(length // tr,),
        in_specs=[pl.BlockSpec((tr, LANES), lambda i: (i, 0)), full((LANES, HY_FILTER_HIDDEN)),
                  full((1, HY_FILTER_HIDDEN)), full((HY_FILTER_HIDDEN, HY_FILTER_HIDDEN)), full((1, HY_FILTER_HIDDEN)),
                  full((HY_FILTER_HIDDEN, nf)), full((1, nf)), full((1, HY_WIDTH))],
        out_specs=(pl.BlockSpec((tr, nf), lambda i: (i, 0)), full((1, nf))),
        compiler_params=_cparams(1),
        name="hyena_filter_mlp",
    )(_hyena_features(length), w1p, b1[None, :], w2, b2[None, :], w3, b3[None, :], absd)


def _dot_split(m, x):
    x_hi = x.astype(BF16)
    x_lo = (x - x_hi.astype(F32)).astype(BF16)
    n = x.shape[1]
    r = _dot(m, jnp.concatenate([x_hi, x_lo], axis=1))
    return r[:, :n] + r[:, n:]


def _odft_stage1(src_at, mm, s_re, s_im):
    nk = s_re.shape[0]
    half = nk * HY_SLAB

    def body(j, carry):
        r0 = pl.multiple_of(j * HY_SLAB, HY_SLAB)
        slab = src_at(r0)
        cb = slab.shape[-1]
        res = mm(slab.reshape(-1, cb))
        s_re[:, pl.ds(r0, HY_SLAB), :] = res[:half].reshape(nk, HY_SLAB, cb)
        s_im[:, pl.ds(r0, HY_SLAB), :] = res[half:].reshape(nk, HY_SLAB, cb)
        return carry

    lax.fori_loop(0, HY_N2 // HY_SLAB, body, 0, unroll=2)


def _hy_spectrum_kernel(hf_ref, hb_ref, sf_ref, sb_ref, fwd1_ref, fwd2_ref, o_ref, s_re, s_im):
    kh = HY_N2 // 2
    mm1 = lambda x: _dot_split(fwd1_ref[...], x)

    def middle(sign):
        def body(k1, carry):
            a = jnp.concatenate([s_re[k1], s_im[k1]], axis=0)
            x = _dot_split(fwd2_ref[k1], a)
            if sign is None:
                o_ref[0, k1] = x[:kh]
                o_ref[1, k1] = x[kh:]
            else:
                inv_norm = 1.0 / (sf_ref[...] + sb_ref[...])
                o_ref[0, k1] = (o_ref[0, k1] + x[:kh]) * inv_norm
                o_ref[1, k1] = (o_ref[1, k1] - x[kh:]) * inv_norm
            return carry
        lax.fori_loop(0, HY_N1, body, 0, unroll=2)

    _odft_stage1(lambda r0: hf_ref[:, pl.ds(r0, HY_SLAB), :], mm1, s_re, s_im)
    middle(None)

    def bwd_slab(r0):
        slab = hb_ref[:, pl.ds(r0, HY_SLAB), :]
        nhi = lax.broadcasted_iota(jnp.int32, slab.shape, 0)
        r = lax.broadcasted_iota(jnp.int32, slab.shape, 1)
        return jnp.where((nhi == 0) & (r + r0 == 0), 0.0, slab)

    _odft_stage1(bwd_slab, mm1, s_re, s_im)
    middle(-1)


def _hyena_filter_spectrum(seq, w1, b1, w2, b2, w3, b3):
    h_raw, s = _hyena_filters_raw(seq, (w1, b1, w2, b2, w3, b3))
    nh = HY_N1 // 2
    nc = HY_ORDER * HY_WIDTH
    h3 = h_raw.reshape(nh, HY_N2, 2 * nc)
    fwd1, fwd2, _, _ = _hyena_dft_constants(seq)
    fwd1 = jnp.asarray(fwd1, BF16)
    fwd2 = jnp.asarray(fwd2, BF16)
    scb = HY_SPEC_CB
    ncb = nc // scb
    return pl.pallas_call(
        _hy_spectrum_kernel,
        out_shape=jax.ShapeDtypeStruct((2, HY_N1, HY_N2 // 2, nc), F32),
        grid=(ncb,),
        in_specs=[pl.BlockSpec((nh, HY_N2, scb), lambda c: (0, 0, c)),
                  pl.BlockSpec((nh, HY_N2, scb), lambda c: (0, 0, ncb + c)),
                  pl.BlockSpec((1, scb), lambda c: (0, c)),
                  pl.BlockSpec((1, scb), lambda c: (0, ncb + c)),
                  _const_spec(fwd1.shape), _const_spec(fwd2.shape)],
        out_specs=pl.BlockSpec((2, HY_N1, HY_N2 // 2, scb), lambda c: (0, 0, 0, c)),
        scratch_shapes=[pltpu.VMEM((HY_N1, HY_N2, scb), F32), pltpu.VMEM((HY_N1, HY_N2, scb), F32)],
        compiler_params=_cparams(1, VMEM_LIMIT),
        name="hyena_filter_spectrum",
    )(h3, h3, s, s, fwd1, fwd2)


def _short_conv_chunk(ref, c, n_chunks, w_ref, b_ref):
    per = TM // HY_N2
    cur = ref[0, pl.ds(per * c, per)]
    cb = cur.shape[-1]
    cur = cur.reshape(TM, cb)
    prev = ref[0, jnp.maximum(per * c - 1, 0), pl.ds(HY_N2 - SUBLANES, SUBLANES), :][SUBLANES - 1:SUBLANES]
    nxt = ref[0, jnp.minimum(per * c + per, per * n_chunks - 1), pl.ds(0, SUBLANES), :][0:1]
    prev = jnp.where(c > 0, prev, 0.0)
    nxt = jnp.where(c < n_chunks - 1, nxt, 0.0)
    rowi = lax.broadcasted_iota(jnp.int32, (TM, cb), 0)
    dn = jnp.where(rowi == 0, prev, pltpu.roll(cur, 1, 0))
    up = jnp.where(rowi == TM - 1, nxt, pltpu.roll(cur, TM - 1, 0))
    return b_ref[...] + w_ref[0:1, :] * dn + w_ref[1:2, :] * cur + w_ref[2:3, :] * up


def _hy_conv_kernel(*refs, conv_y):
    if conv_y:
        (y_ref, g_ref, h_ref, bias_ref, wy_ref, by_ref, wg_ref, bg_ref, fwd1_ref, fwd2_ref, inv2_ref, inv1_ref,
         o_ref, s_re, s_im, gs_ref, us_ref) = refs
    else:
        (y_ref, g_ref, h_ref, bias_ref, wg_ref, bg_ref, fwd1_ref, fwd2_ref, inv2_ref, inv1_ref,
         o_ref, s_re, s_im, gs_ref) = refs
    grp = pl.program_id(2)
    last = pl.num_programs(2) - 1
    nh = HY_N1 // 2
    per = TM // HY_N2
    n_chunks = nh // per
    cb = o_ref.shape[-1]
    kh = HY_N2 // 2

    @pl.when(grp == 0)
    def _():
        def pre(c, carry):
            gs_ref[pl.ds(per * c, per)] = _short_conv_chunk(g_ref, c, n_chunks, wg_ref, bg_ref).reshape(per, HY_N2, cb)
            if conv_y:
                us_ref[pl.ds(per * c, per)] = _short_conv_chunk(y_ref, c, n_chunks, wy_ref, by_ref).reshape(
                    per, HY_N2, cb)
            return carry
        lax.fori_loop(0, n_chunks, pre, 0)

    if conv_y:
        u_at = lambda r0: us_ref[:, pl.ds(r0, HY_SLAB), :]
    else:
        u_at = lambda r0: y_ref[0, :, pl.ds(r0, HY_SLAB), :]

    _odft_stage1(u_at, lambda x: _dot(fwd1_ref[0], x.astype(BF16)), s_re, s_im)

    def middle(k1, carry):
        a = jnp.concatenate([s_re[k1], s_im[k1]], axis=0).astype(BF16)
        x = _dot(fwd2_ref[k1], a)
        xr, xi = x[:kh], x[kh:]
        hr, hi = h_ref[0, k1], h_ref[1, k1]
        y = jnp.concatenate([xr * hr - xi * hi, xr * hi + xi * hr], axis=0).astype(BF16)
        bm = _dot(inv2_ref[k1], y)
        s_re[k1] = bm[:HY_N2]
        s_im[k1] = bm[HY_N2:]
        return carry

    lax.fori_loop(0, HY_KG, middle, 0, unroll=4)

    def partial_conv(r0):
        slab = jnp.concatenate([s_re[:, pl.ds(r0, HY_SLAB), :].reshape(HY_KG * HY_SLAB, cb),
                                s_im[:, pl.ds(r0, HY_SLAB), :].reshape(HY_KG * HY_SLAB, cb)], axis=0).astype(BF16)
        return _dot(inv1_ref[0], slab).reshape(nh, HY_SLAB, cb)

    def post_loop(fn):
        def post(j, carry):
            r0 = pl.multiple_of(j * HY_SLAB, HY_SLAB)
            o_ref[0, :, pl.ds(r0, HY_SLAB), :] = fn(r0, partial_conv(r0))
            return carry
        lax.fori_loop(0, HY_N2 // HY_SLAB, post, 0, unroll=2)

    @pl.when(grp == 0)
    def _():
        post_loop(lambda r0, part: part)

    @pl.when((grp > 0) & (grp < last))
    def _():
        post_loop(lambda r0, part: o_ref[0, :, pl.ds(r0, HY_SLAB), :] + part)

    @pl.when(grp == last)
    def _():
        post_loop(lambda r0, part: gs_ref[:, pl.ds(r0, HY_SLAB), :]
                  * (o_ref[0, :, pl.ds(r0, HY_SLAB), :] + part + bias_ref[...] * u_at(r0)))


def _hyena_order(y4, y_col0, z4, gate_col0, hspec, order, hy_bias, short_w, short_b, consts, conv_y, rows_out):
    bsz = z4.shape[0]
    nh = HY_N1 // 2
    ncb = HY_WIDTH // HY_CB
    ngrp = HY_N1 // HY_KG
    fwd1, fwd2, inv2, inv1 = consts
    blk4 = lambda off: pl.BlockSpec((1, nh, HY_N2, HY_CB), lambda c, b, g: (b, 0, 0, off + c))
    rowspec = lambda rows, off: pl.BlockSpec((rows, HY_CB), lambda c, b, g: (0, off + c))
    grouped = lambda shp: pl.BlockSpec(shp, lambda c, b, g: (g, 0, 0))
    in_specs = [blk4(y_col0), blk4(gate_col0),
                pl.BlockSpec((2, HY_KG, HY_N2 // 2, HY_CB), lambda c, b, g: (0, g, 0, order * ncb + c)),
                rowspec(1, 0)]
    args = [y4, z4, hspec, hy_bias.reshape(1, -1)]
    if conv_y:
        in_specs += [rowspec(HY_SHORT, y_col0), rowspec(1, y_col0)]
        args += [short_w, short_b[None, :]]
    in_specs += [rowspec(HY_SHORT, gate_col0), rowspec(1, gate_col0)]
    args += [short_w, short_b[None, :]]
    in_specs += [grouped((1,) + fwd1.shape[1:]), grouped((HY_KG,) + fwd2.shape[1:]),
                 grouped((HY_KG,) + inv2.shape[1:]), grouped((1,) + inv1.shape[1:])]
    args += [fwd1, fwd2, inv2, inv1]
    scratch = [pltpu.VMEM((HY_KG, HY_N2, HY_CB), F32), pltpu.VMEM((HY_KG, HY_N2, HY_CB), F32),
               pltpu.VMEM((nh, HY_N2, HY_CB), F32)]
    if conv_y:
        scratch.append(pltpu.VMEM((nh, HY_N2, HY_CB), F32))
    return pl.pallas_call(
        functools.partial(_hy_conv_kernel, conv_y=conv_y),
        out_shape=jax.ShapeDtypeStruct((bsz, rows_out, HY_N2, HY_WIDTH), F32),
        grid=(ncb, bsz, ngrp),
        in_specs=in_specs,
        out_specs=pl.BlockSpec((1, nh, HY_N2, HY_CB), lambda c, b, g: (b, 0, 0, c)),
        scratch_shapes=scratch,
        compiler_params=_cparams(3, VMEM_LIMIT),
        name="hyena_conv%d" % order,
    )(*args)


def _hyena_latent(z_hy, hspec, short_w, short_b, hy_bias, with_ctx_rows):
    bsz, t, _ = z_hy.shape
    seq = HY_N1 * HY_N2 // 2
    fwd1, fwd2, inv2, inv1 = _hyena_dft_constants(seq)
    ngrp = HY_N1 // HY_KG
    rows = HY_KG * HY_SLAB
    fwd1 = fwd1.reshape(2, ngrp, rows, -1).transpose(1, 0, 2, 3).reshape(ngrp, 2 * rows, -1)
    inv1 = inv1.reshape(-1, 2, ngrp, rows).transpose(2, 0, 1, 3).reshape(ngrp, -1, 2 * rows)
    consts = tuple(jnp.asarray(m, BF16) for m in (fwd1, fwd2, inv2, inv1))
    z4 = z_hy.reshape(bsz, t // HY_N2, HY_N2, 3 * HY_WIDTH)
    ncb = HY_WIDTH // HY_CB
    nh = seq // HY_N2
    y1 = _hyena_order(z4, 2 * ncb, z4, 0, hspec, 0, hy_bias[0], short_w, short_b, consts, True, nh)
    rows_out = t // HY_N2 if with_ctx_rows else nh
    y2 = _hyena_order(y1, 0, z4, ncb, hspec, 1, hy_bias[1], short_w, short_b, consts, False, rows_out)
    return y2.reshape(bsz, rows_out * HY_N2, HY_WIDTH)


def _hy_ctx_spectrum_kernel(h_ref, s_ref, fwd_ref, o_ref):
    lc = h_ref.shape[0]
    nc = HY_ORDER * HY_WIDTH
    hp = lambda a, b: jnp.dot(a, b, preferred_element_type=F32, precision=HIGHEST)
    h = h_ref[...]
    rowi = lax.broadcasted_iota(jnp.int32, (lc, nc), 0)
    xf = hp(fwd_ref[...], h[:, :nc])
    xb = hp(fwd_ref[...], jnp.where(rowi == 0, 0.0, h[:, nc:]))
    inv_norm = 1.0 / (s_ref[:, :nc] + s_ref[:, nc:])
    o_ref[0] = (xf[:lc] + xb[:lc]) * inv_norm
    o_ref[1] = (xf[lc:] - xb[lc:]) * inv_norm


def _hy_ctx_conv_kernel(x1_ref, x2_ref, v_ref, h_ref, bias_ref, w_ref, b_ref, fwd_ref, inv_ref, _, o_ref):
    lc = o_ref.shape[1]
    rowi = lax.broadcasted_iota(jnp.int32, (lc, HY_WIDTH), 0)

    def short(ref, part):
        cur = ref[0].reshape(lc, HY_WIDTH)
        sl = slice(part * HY_WIDTH, (part + 1) * HY_WIDTH)
        dn = jnp.where(rowi == 0, 0.0, pltpu.roll(cur, 1, 0))
        up = jnp.where(rowi == lc - 1, 0.0, pltpu.roll(cur, lc - 1, 0))
        return b_ref[:, sl] + w_ref[0:1, sl] * dn + w_ref[1:2, sl] * cur + w_ref[2:3, sl] * up

    y = short(v_ref, 2)
    for order, gref in enumerate((x1_ref, x2_ref)):
        sl = slice(order * HY_WIDTH, (order + 1) * HY_WIDTH)
        x = _dot(fwd_ref[...], y.astype(BF16))
        xr, xi = x[:lc], x[lc:]
        hr, hi = h_ref[0, :, sl], h_ref[1, :, sl]
        prod = jnp.concatenate([xr * hr - xi * hi, xr * hi + xi * hr], axis=0).astype(BF16)
        conv = _dot(inv_ref[...], prod)
        y = short(gref, order) * (conv + bias_ref[order:order + 1, :] * y)
    o_ref[0] = y


def _hyena_ctx(z_hy, y_lat, filt_w, short_w, short_b, hy_bias):
    bsz, t, _ = z_hy.shape
    seq = HY_N1 * HY_N2 // 2
    lc = t - seq
    per = lc // HY_N2
    h_raw, s = _hyena_filters_raw(lc, filt_w)
    fwd, inv = _hyena_ctx_dft_constants(lc)
    nc = HY_ORDER * HY_WIDTH
    full = lambda shp: pl.BlockSpec(shp, lambda *_: (0,) * len(shp))
    hspec = pl.pallas_call(
        _hy_ctx_spectrum_kernel,
        out_shape=jax.ShapeDtypeStruct((2, lc, nc), F32),
        grid=(1,),
        in_specs=[full(h_raw.shape), full(s.shape), full(fwd.shape)],
        out_specs=full((2, lc, nc)),
        compiler_params=_cparams(1),
        name="hyena_ctx_spectrum",
    )(h_raw, s, jnp.asarray(fwd, F32))
    z4 = z_hy.reshape(bsz, t // HY_N2, HY_N2, 3 * HY_WIDTH)
    blk = lambda part: pl.BlockSpec((1, per, HY_N2, HY_WIDTH), lambda b: (b, seq // lc, 0, part))
    return pl.pallas_call(
        _hy_ctx_conv_kernel,
        out_shape=jax.ShapeDtypeStruct((bsz, t, HY_WIDTH), F32),
        grid=(bsz,),
        in_specs=[blk(0), blk(1), blk(2), full((2, lc, nc)), full((HY_ORDER, HY_WIDTH)),
                  full((HY_SHORT, 3 * HY_WIDTH)), full((1, 3 * HY_WIDTH)), full(fwd.shape), full(inv.shape),
                  pl.BlockSpec(memory_space=pl.ANY)],
        out_specs=pl.BlockSpec((1, lc, HY_WIDTH), lambda b: (b, seq // lc, 0)),
        input_output_aliases={9: 0},
        compiler_params=_cparams(1),
        name="hyena_ctx_conv",
    )(z4, z4, z4, hspec, hy_bias, short_w, short_b[None, :], jnp.asarray(fwd, BF16), jnp.asarray(inv, BF16), y_lat)


FF_CHUNK = 1024


def _merge_mlp_kernel(x_ref, mod_ref, ymla_ref, of_ref, ob_ref, gr_ref, yhy_ref, zg_ref, on_ref,
                      wm_ref, wgl_ref, wh_ref, wo_ref, g2_ref, w1_ref, w2_ref, fg_ref, o_ref, *, final):
    o = of_ref[0, 0].astype(F32) + ob_ref[0, 0].astype(F32)
    silu = gr_ref[0].astype(F32)
    parts = []
    for hd in range(GLA_HEADS):
        sl = slice(hd * GLA_HV, (hd + 1) * GLA_HV)
        parts.append((_rms(o[:, sl]) * on_ref[...] * silu[:, sl]).astype(BF16))
    y_gla = jnp.concatenate(parts, axis=1)
    zg = zg_ref[0].astype(F32)
    d = x_ref.shape[-1]
    m = zg[:, 0:d] * _dot(ymla_ref[0], wm_ref[...])
    m = m + zg[:, d:2 * d] * _dot(y_gla, wgl_ref[...])
    m = m + zg[:, 2 * d:3 * d] * _dot(yhy_ref[0].astype(BF16), wh_ref[...])
    out = _dot(m.astype(BF16), wo_ref[...])
    x = x_ref[0] + mod_ref[0, 2:3, :] * out

    h = (_rms(x) * g2_ref[...] * (1.0 + mod_ref[0, 4:5, :]) + mod_ref[0, 3:4, :]).astype(BF16)
    acc = jnp.zeros(x.shape, F32)
    for j in range(w1_ref.shape[1] // FF_CHUNK):
        a = jnp.maximum(_dot(h, w1_ref[:, j * FF_CHUNK:(j + 1) * FF_CHUNK]), 0.0)
        acc = acc + _dot((a * a).astype(BF16), w2_ref[j * FF_CHUNK:(j + 1) * FF_CHUNK, :])
    xn = x + mod_ref[0, 5:6, :] * acc
    if final:
        xn = _rms(xn) * fg_ref[...]
    o_ref[0] = xn


def _mod_spec(d, n_lat_tiles, bsz):
    return pl.BlockSpec((1, 6, d), lambda b, i: (jnp.where(i < n_lat_tiles, b, bsz), 0, 0))


def _merge_mlp(xc, mod_l, y_mla, o_gla, gr, y_hy, gate, out_norm, w_o_mla, w_o_gla, w_o_hy, w_out,
               g2, w1, w2, final_g, n_tiles, n_lat_tiles, final):
    bsz, t, d = xc.shape
    tile = lambda w: pl.BlockSpec((1, TM, w), lambda b, i: (b, i, 0))
    row = lambda w: pl.BlockSpec((1, w), lambda b, i: (0, 0))
    dirspec = lambda dr: pl.BlockSpec((1, 1, TM, GLA_DV), lambda b, i: (b, dr, i, 0))
    bf = lambda w: w.astype(BF16)
    return pl.pallas_call(
        functools.partial(_merge_mlp_kernel, final=final),
        out_shape=jax.ShapeDtypeStruct((bsz, n_tiles * TM, d), F32),
        grid=(bsz, n_tiles),
        in_specs=[tile(d), _mod_spec(d, n_lat_tiles, bsz), tile(MLA_OUT), dirspec(0), dirspec(1), tile(GLA_DV),
                  tile(HY_WIDTH), tile(3 * d), row(GLA_HV),
                  _const_spec(w_o_mla.shape), _const_spec(w_o_gla.shape), _const_spec(w_o_hy.shape),
                  _const_spec(w_out.shape), row(d), _const_spec(w1.shape), _const_spec(w2.shape), row(d)],
        out_specs=tile(d),
        compiler_params=_cparams(2, VMEM_LIMIT),
        name="merge_mlp",
    )(xc, mod_l, y_mla, o_gla, o_gla, gr, y_hy, gate, out_norm[None, :], bf(w_o_mla), bf(w_o_gla), bf(w_o_hy),
      bf(w_out), g2[None, :], bf(w1), bf(w2), final_g[None, :])


def kernel(x, c, ctx, c_ctx, ada_w, ada_b, norm1_g, norm2_g, w_in, mla_q_norm, mla_w_uq, mla_kv_norm, mla_w_ukv, gla_w_a2, gla_b_a, gla_out_norm, hy_short_w, hy_short_b, hy_f_w1, hy_f_b1, hy_f_w2, hy_f_b2, hy_f_w3, hy_f_b3, hy_bias, w_o_mla, w_o_gla, w_o_hy, w_out, ff_w1, ff_w2, final_norm_g):
    bsz, seq, d = x.shape
    ctx_len = ctx.shape[1]
    n_lat = seq // TM
    n_all = (seq + ctx_len) // TM
    xc = jnp.concatenate([x, ctx], axis=1)
    cc = jnp.zeros((16, d), F32).at[:bsz].set(c).at[bsz].set(c_ctx)
    mod = _modulation(cc, ada_w, ada_b).reshape(DEPTH, 16, 6, d)
    cos, sin = _rope_tables(seq, ctx_len)
    for l in range(DEPTH):
        last = l == DEPTH - 1
        n_tiles = n_lat if last else n_all
        weights = _prep_inproj_weights(w_in[l], mla_w_uq[l], mla_w_ukv[l], gla_w_a2[l], gla_b_a[l])
        q, k, v, gq, gk, gv, gr, glog, z_hy, z_gate = _inproj(xc, mod[l], norm1_g[l], weights, mla_q_norm[l],
                                                             mla_kv_norm[l], cos, sin, n_lat)
        y_mla = _attention(q, k, v, seq, not last)
        o_gla = _gla(gq, gk, gv, glog, n_lat)
        filt_w = (hy_f_w1[l], hy_f_b1[l], hy_f_w2[l], hy_f_b2[l], hy_f_w3[l], hy_f_b3[l])
        hspec = _hyena_filter_spectrum(seq, *filt_w)
        y_hy = _hyena_latent(z_hy, hspec, hy_short_w[l], hy_short_b[l], hy_bias[l], not last)
        if not last:
            y_hy = _hyena_ctx(z_hy, y_hy, filt_w, hy_short_w[l], hy_short_b[l], hy_bias[l])
        xc = _merge_mlp(xc, mod[l], y_mla, o_gla, gr, y_hy, z_gate, gla_out_norm[l], w_o_mla[l], w_o_gla[l],
                        w_o_hy[l], w_out[l], norm2_g[l], ff_w1[l], ff_w2[l], final_norm_g, n_tiles, n_lat, last)
    return xc
```

```python
import functools
import math

import numpy as np
import jax
import jax.numpy as jnp
from jax import lax
from jax.experimental import pallas as pl
from jax.experimental.pallas import tpu as pltpu

F32 = jnp.float32
BF16 = jnp.bfloat16
HIGHEST = lax.Precision.HIGHEST
LOG2E = 1.4426950408889634

D_MODEL = 1024
DEPTH = 2
GRID_W = 64
EPS = 1e-6
MLA_HEADS = 8
MLA_NOPE = 64
MLA_ROPE = 32
MLA_V = 64
MLA_Q_LORA = 256
MLA_KV_LORA = 128
MLA_SCALE = (MLA_NOPE + MLA_ROPE) ** -0.5
ROPE_BASE = 10000.0
GLA_HEADS = 4
GLA_DK = 256
GLA_DV = 512
GLA_HK = GLA_DK // GLA_HEADS
GLA_HV = GLA_DV // GLA_HEADS
GLA_GATE_RANK = 16
GLA_TAU = 16.0
HY_WIDTH = 512
HY_ORDER = 2
HY_SHORT = 3
HY_BANDS = 16
HY_POS_DIM = 1 + 2 * HY_BANDS
HY_FILTER_HIDDEN = 64
HY_FAST_DECAY = 0.3
HY_SLOW_DECAY = 1.5
HY_DECAY_TARGET = 1e-2
D_FF = 4 * D_MODEL
MLA_OUT = MLA_HEADS * MLA_V
IN_SIZES = (MLA_Q_LORA, MLA_KV_LORA, MLA_ROPE, GLA_DK, GLA_DK, GLA_DV, GLA_DV, GLA_GATE_RANK, GLA_GATE_RANK,
            (HY_ORDER + 1) * HY_WIDTH, 3 * D_MODEL)

LANES = 128
SUBLANES = 8
TM = 256
HEAD_SLOT = 128
VMEM_LIMIT = 56 * 1024 * 1024


def _cparams(n_axes, vmem=None):
    return pltpu.CompilerParams(dimension_semantics=("arbitrary",) * n_axes, vmem_limit_bytes=vmem)


def _const_spec(shape):
    nd = len(shape)
    return pl.BlockSpec(shape, lambda *_: (0,) * nd, pipeline_mode=pl.Buffered(1))


def _rms(x):
    return x * lax.rsqrt(jnp.mean(x * x, axis=-1, keepdims=True) + EPS)


def _sigmoid(x):
    return 1.0 / (1.0 + jnp.exp(-x))


def _dot(a, b):
    return jnp.dot(a, b, preferred_element_type=F32)


def _dot_nt(a, b):
    return lax.dot_general(a, b, (((1,), (1,)), ((), ())), preferred_element_type=F32)


def _mod_kernel(cc_ref, w_ref, b_ref, o_ref):
    s = cc_ref[...]
    s = s * _sigmoid(s)
    o_ref[0] = jnp.dot(s, w_ref[0], preferred_element_type=F32, precision=HIGHEST) + b_ref[0]


def _modulation(cc, ada_w, ada_b):
    tn = 1536
    n6 = ada_w.shape[-1]
    return pl.pallas_call(
        _mod_kernel,
        out_shape=jax.ShapeDtypeStruct((DEPTH, 16, n6), F32),
        grid=(DEPTH, n6 // tn),
        in_specs=[
            pl.BlockSpec((16, D_MODEL), lambda l, j: (0, 0)),
            pl.BlockSpec((1, D_MODEL, tn), lambda l, j: (l, 0, j)),
            pl.BlockSpec((1, 1, tn), lambda l, j: (l, 0, j)),
        ],
        out_specs=pl.BlockSpec((1, 16, tn), lambda l, j: (l, 0, j)),
        compiler_params=_cparams(2),
        name="modulation",
    )(cc, ada_w, ada_b.reshape(DEPTH, 1, n6))


W_A = 768
W_G = 2 * GLA_DK + 2 * GLA_DV


def _inproj_kernel(x_ref, mod_ref, g1_ref, wa_ref, wg_ref, wh_ref, wz_ref, qn_ref, kvn_ref, wuq_ref, wukv_ref,
                   wa2_ref, ba_ref, cos_ref, sin_ref,
                   q_out, k_out, v_out, gq_out, gk_out, gv_out, gr_out, glog_out, hy_out, gate_out):
    x = x_ref[0]
    shift = mod_ref[0, 0:1, :]
    scale = mod_ref[0, 1:2, :]
    h = (_rms(x) * g1_ref[...] * (1.0 + scale) + shift).astype(BF16)

    za = _dot(h, wa_ref[...])
    cos = cos_ref[...]
    sin = sin_ref[...]

    cqn = (_rms(za[:, 0:256]) * qn_ref[...]).astype(BF16)
    qab = _dot(cqn, wuq_ref[...])
    nq = MLA_HEADS * HEAD_SLOT
    for hd in range(MLA_HEADS):
        sl = slice(hd * HEAD_SLOT, (hd + 1) * HEAD_SLOT)
        qa = qab[:, hd * HEAD_SLOT:(hd + 1) * HEAD_SLOT]
        qb = qab[:, nq + hd * HEAD_SLOT:nq + (hd + 1) * HEAD_SLOT]
        q_out[0, :, sl] = ((qa * cos + qb * sin) * (MLA_SCALE * LOG2E)).astype(BF16)

    ckvn = (_rms(za[:, 256:384]) * kvn_ref[...]).astype(BF16)
    kv = _dot(ckvn, wukv_ref[...])
    krot = za[:, 384:512] * cos + za[:, 512:640] * sin
    for hd in range(MLA_HEADS):
        sl = slice(hd * HEAD_SLOT, (hd + 1) * HEAD_SLOT)
        k_out[0, :, sl] = (kv[:, sl] + krot).astype(BF16)
    ones_hi = (lax.broadcasted_iota(jnp.int32, (1, HEAD_SLOT), 1) >= MLA_V).astype(F32)
    for hd in range(MLA_HEADS):
        sl = slice(hd * HEAD_SLOT, (hd + 1) * HEAD_SLOT)
        v_out[0, :, sl] = (kv[:, nq + hd * HEAD_SLOT:nq + (hd + 1) * HEAD_SLOT] + ones_hi).astype(BF16)

    xg = _dot(za[:, 640:768].astype(BF16), wa2_ref[...]) + ba_ref[...]
    glog_out[0] = (jnp.minimum(xg, 0.0) - jnp.log(1.0 + jnp.exp(-jnp.abs(xg)))) * (1.0 / GLA_TAU)

    zg = _dot(h, wg_ref[...])
    gq_out[0] = (zg[:, 0:GLA_DK] * (GLA_HK ** -0.5)).astype(BF16)
    gk_out[0] = zg[:, GLA_DK:2 * GLA_DK].astype(BF16)
    gv_out[0] = zg[:, 2 * GLA_DK:2 * GLA_DK + GLA_DV].astype(BF16)
    gr = zg[:, 2 * GLA_DK + GLA_DV:]
    gr_out[0] = (gr * _sigmoid(gr)).astype(BF16)

    hy_out[0] = _dot(h, wh_ref[...])
    gate_out[0] = _sigmoid(_dot(h, wz_ref[...])).astype(BF16)


def _rope_partner(w):
    a = MLA_ROPE // 4
    perm = np.concatenate([np.arange(a, 2 * a), np.arange(0, a), np.arange(3 * a, 4 * a), np.arange(2 * a, 3 * a)])
    sign = np.concatenate([-np.ones(a), np.ones(a), -np.ones(a), np.ones(a)]).astype(np.float32)
    return w[:, perm] * sign


def _prep_inproj_weights(w_in, mla_w_uq, mla_w_ukv, gla_w_a2, gla_b_a):
    offs = np.concatenate([[0], np.cumsum(IN_SIZES)])
    seg = [w_in[:, offs[i]:offs[i + 1]] for i in range(len(IN_SIZES))]
    w_cq, w_ckv, w_kr, w_gq, w_gk, w_gv, w_gr, w_af, w_ab, w_hy, w_gate = seg
    d = w_in.shape[0]
    z = lambda n: jnp.zeros((d, n), w_in.dtype)
    kr_tile = jnp.concatenate([z(MLA_NOPE), w_kr, z(HEAD_SLOT - MLA_NOPE - MLA_ROPE)], axis=1)
    krp_tile = jnp.concatenate([z(MLA_NOPE), _rope_partner(w_kr), z(HEAD_SLOT - MLA_NOPE - MLA_ROPE)], axis=1)
    a_tile = jnp.concatenate([w_af, w_ab, z(LANES - 2 * GLA_GATE_RANK)], axis=1)
    wa = jnp.concatenate([w_cq, w_ckv, kr_tile, krp_tile, a_tile], axis=1)
    wg = jnp.concatenate([w_gq, w_gk, w_gv, w_gr], axis=1)

    dh = MLA_NOPE + MLA_ROPE
    zq = lambda n: jnp.zeros((MLA_Q_LORA, n), w_in.dtype)
    plain, partner = [], []
    for hd in range(MLA_HEADS):
        blk = mla_w_uq[:, hd * dh:(hd + 1) * dh]
        plain += [blk, zq(HEAD_SLOT - dh)]
        partner += [zq(MLA_NOPE), _rope_partner(blk[:, MLA_NOPE:]), zq(HEAD_SLOT - dh)]
    wuq = jnp.concatenate(plain + partner, axis=1)

    zk = jnp.zeros((MLA_KV_LORA, HEAD_SLOT - MLA_NOPE), w_in.dtype)
    kcols, vcols = [], []
    for hd in range(MLA_HEADS):
        blk = mla_w_ukv[:, hd * (MLA_NOPE + MLA_V):(hd + 1) * (MLA_NOPE + MLA_V)]
        kcols += [blk[:, :MLA_NOPE], zk]
        vcols += [blk[:, MLA_NOPE:], zk]
    wukv = jnp.concatenate(kcols + vcols, axis=1)

    wa2 = jnp.zeros((LANES, 2 * GLA_DK), w_in.dtype)
    wa2 = wa2.at[0:GLA_GATE_RANK, 0:GLA_DK].set(gla_w_a2[0])
    wa2 = wa2.at[GLA_GATE_RANK:2 * GLA_GATE_RANK, GLA_DK:].set(gla_w_a2[1])
    ba = jnp.concatenate([gla_b_a[0], gla_b_a[1]])[None, :]
    bf = lambda t: t.astype(BF16)
    return bf(wa), bf(wg), bf(w_hy), bf(w_gate), bf(wuq), bf(wukv), bf(wa2), ba


def _rope_tables(seq, ctx_len):
    rows = seq // GRID_W
    row = np.repeat(np.arange(rows, dtype=np.float64), GRID_W)
    col = np.tile(np.arange(GRID_W, dtype=np.float64), rows)
    a = MLA_ROPE // 4
    inv = ROPE_BASE ** (-np.arange(a, dtype=np.float64) / a)
    ang_r = row[:, None] * inv
    ang_c = col[:, None] * inv
    cos32 = np.concatenate([np.cos(ang_r), np.cos(ang_r), np.cos(ang_c), np.cos(ang_c)], axis=1)
    sin32 = np.concatenate([np.sin(ang_r), np.sin(ang_r), np.sin(ang_c), np.sin(ang_c)], axis=1)
    pad_r = HEAD_SLOT - MLA_NOPE - MLA_ROPE
    cos = np.concatenate([np.ones((seq, MLA_NOPE)), cos32, np.zeros((seq, pad_r))], axis=1)
    sin = np.concatenate([np.zeros((seq, MLA_NOPE)), sin32, np.zeros((seq, pad_r))], axis=1)
    cos_c = np.concatenate([np.ones((ctx_len, MLA_NOPE + MLA_ROPE)), np.zeros((ctx_len, pad_r))], axis=1)
    sin_c = np.zeros((ctx_len, HEAD_SLOT))
    return (jnp.asarray(np.concatenate([cos, cos_c], axis=0), F32),
            jnp.asarray(np.concatenate([sin, sin_c], axis=0), F32))


def _inproj(xc, mod_l, g1, weights, q_norm, kv_norm, cos, sin, n_lat_tiles):
    bsz, t, d = xc.shape
    nt = t // TM
    wa, wg, wh, wz, wuq, wukv, wa2, ba = weights
    tile = lambda w: pl.BlockSpec((1, TM, w), lambda b, i: (b, i, 0))
    row = lambda w: pl.BlockSpec((1, w), lambda b, i: (0, 0))
    mod_spec = pl.BlockSpec((1, 6, d), lambda b, i: (jnp.where(i < n_lat_tiles, b, bsz), 0, 0))
    tab = pl.BlockSpec((TM, HEAD_SLOT), lambda b, i: (i, 0))
    nq = MLA_HEADS * HEAD_SLOT
    sds = lambda w, dt: jax.ShapeDtypeStruct((bsz, t, w), dt)
    out_shape = (sds(nq, BF16), sds(nq, BF16), sds(nq, BF16), sds(GLA_DK, BF16), sds(GLA_DK, BF16),
                 sds(GLA_DV, BF16), sds(GLA_DV, BF16), sds(2 * GLA_DK, F32), sds(3 * HY_WIDTH, F32),
                 sds(3 * D_MODEL, BF16))
    out_specs = (tile(nq), tile(nq), tile(nq), tile(GLA_DK), tile(GLA_DK), tile(GLA_DV), tile(GLA_DV),
                 tile(2 * GLA_DK), tile(3 * HY_WIDTH), tile(3 * D_MODEL))
    return pl.pallas_call(
        _inproj_kernel,
        out_shape=out_shape,
        grid=(bsz, nt),
        in_specs=[tile(d), mod_spec, row(d), _const_spec(wa.shape), _const_spec(wg.shape), _const_spec(wh.shape),
                  _const_spec(wz.shape), row(MLA_Q_LORA), row(MLA_KV_LORA), _const_spec(wuq.shape),
                  _const_spec(wukv.shape), _const_spec(wa2.shape), row(2 * GLA_DK), tab, tab],
        out_specs=out_specs,
        compiler_params=_cparams(2, VMEM_LIMIT),
        name="inproj",
    )(xc, mod_l, g1[None, :], wa, wg, wh, wz, q_norm[None, :], kv_norm[None, :], wuq, wukv, wa2, ba, cos, sin)


ATT_TK = 1024


ATT_TQ = 1024


def _attn_kernel(*refs, chunks, aliased):
    if aliased:
        q_ref, k_ref, v_ref, _, o_ref, m_ref, acc_ref = refs
    else:
        q_ref, k_ref, v_ref, o_ref, m_ref, acc_ref = refs
    tq = q_ref.shape[1]
    m_ref[...] = jnp.full(m_ref.shape, -jnp.inf, F32)
    acc_ref[...] = jnp.zeros(acc_ref.shape, F32)
    for r0, size in chunks:
        k = k_ref[0, pl.ds(r0, size), :]
        v = v_ref[0, pl.ds(r0, size), :]
        for hd in range(2):
            sl = slice(hd * HEAD_SLOT, (hd + 1) * HEAD_SLOT)
            s = _dot_nt(q_ref[0, :, sl], k[:, sl])
            m_prev = m_ref[hd]
            m_new = jnp.maximum(m_prev, jnp.max(s, axis=1, keepdims=True))
            p = jnp.exp2((s - jnp.concatenate([m_new] * (size // LANES), axis=1)).astype(BF16))
            acc_ref[hd] = jnp.exp2(m_prev - m_new) * acc_ref[hd] + _dot(p, v[:, sl])
            m_ref[hd] = m_new
    a0 = acc_ref[0]
    a1 = acc_ref[1]
    lane = lax.broadcasted_iota(jnp.int32, (tq, HEAD_SLOT), 1)
    o0 = a0 / pltpu.roll(a0, MLA_V, 1)
    o1 = pltpu.roll(a1, MLA_V, 1) / a1
    o_ref[0] = jnp.where(lane < MLA_V, o0, o1).astype(o_ref.dtype)


def _attention(q, k, v, seq, with_ctx_queries):
    bsz, t, _ = q.shape
    ctx_len = t - seq
    assert seq % ATT_TK == 0 and seq % ATT_TQ == 0 and seq % ctx_len == 0 and 2 * MLA_V == HEAD_SLOT
    pair = 2 * HEAD_SLOT
    n_chunks = seq // ATT_TK
    chunks = tuple((j * ATT_TK, ATT_TK) for j in range(n_chunks - 1))
    chunks += (((n_chunks - 1) * ATT_TK, ATT_TK + ctx_len),)
    scratch = lambda tq: [pltpu.VMEM((2, tq, LANES), F32), pltpu.VMEM((2, tq, HEAD_SLOT), F32)]
    y = pl.pallas_call(
        functools.partial(_attn_kernel, chunks=chunks, aliased=False),
        out_shape=jax.ShapeDtypeStruct((bsz, t, MLA_OUT), BF16),
        grid=(bsz, MLA_HEADS // 2, seq // ATT_TQ),
        in_specs=[
            pl.BlockSpec((1, ATT_TQ, pair), lambda b, hp, i: (b, i, hp)),
            pl.BlockSpec((1, t, pair), lambda b, hp, i: (b, 0, hp)),
            pl.BlockSpec((1, t, pair), lambda b, hp, i: (b, 0, hp)),
        ],
        out_specs=pl.BlockSpec((1, ATT_TQ, HEAD_SLOT), lambda b, hp, i: (b, i, hp)),
        scratch_shapes=scratch(ATT_TQ),
        compiler_params=_cparams(3, VMEM_LIMIT),
        name="mla_attention",
    )(q, k, v)
    if not with_ctx_queries:
        return y
    cblk = seq // ctx_len
    ctx_rows = lambda w: pl.BlockSpec((1, ctx_len, w), lambda b, hp: (b, cblk, hp))
    return pl.pallas_call(
        functools.partial(_attn_kernel, chunks=((0, ctx_len),), aliased=True),
        out_shape=jax.ShapeDtypeStruct((bsz, t, MLA_OUT), BF16),
        grid=(bsz, MLA_HEADS // 2),
        in_specs=[ctx_rows(pair), ctx_rows(pair), ctx_rows(pair), pl.BlockSpec(memory_space=pl.ANY)],
        out_specs=ctx_rows(HEAD_SLOT),
        scratch_shapes=scratch(ctx_len),
        input_output_aliases={3: 0},
        compiler_params=_cparams(2),
        name="mla_attention_ctx",
    )(q, k, v, y)


GLA_LEVELS = int(math.log2(TM))
GLA_SAFE_SPAN = 60.0


def _gla_level_matrices():
    i = np.arange(TM)[:, None]
    t = np.arange(TM)[None, :]
    fwd = [(t <= i)]
    for lv in range(GLA_LEVELS):
        m = TM >> (lv + 1)
        lo = (i // m) * m
        later = ((i // m) % 2) == 1
        q_part = later & (t >= lo) & (t <= i)
        k_part = (~later) & (t > i) & (t <= lo + m - 1)
        fwd.append(q_part | k_part)
    fwd = np.concatenate(fwd, axis=0).astype(np.float32)
    nb = 1 + GLA_LEVELS
    bwd = fwd.reshape(nb, TM, TM)[:, ::-1, ::-1].reshape(nb * TM, TM)
    return np.stack([fwd, bwd])


def _gla_kernel(q_ref, k_ref, v_ref, g_ref, mall_ref, o_ref, s_ref, a_ref):
    d = pl.program_id(1)
    step = pl.program_id(2)

    @pl.when(step == 0)
    def _():
        s_ref[...] = jnp.zeros(s_ref.shape, F32)

    q = q_ref[0].astype(F32)
    k = k_ref[0].astype(F32)
    v = v_ref[0]
    g = g_ref[0]
    g2 = jnp.concatenate([g.astype(BF16), (g - g.astype(BF16).astype(F32)).astype(BF16)], axis=1)
    e2 = _dot(mall_ref[0, 0:TM, :], g2)
    g_cum = e2[:, :GLA_DK] + e2[:, GLA_DK:]
    g_tot = jnp.sum(g, axis=0, keepdims=True)

    row = lax.broadcasted_iota(jnp.int32, (TM, TM), 0)
    col = lax.broadcasted_iota(jnp.int32, (TM, TM), 1)
    lane_head = lax.broadcasted_iota(jnp.int32, (TM, GLA_DK), 1) // GLA_HK
    tok = lax.broadcasted_iota(jnp.int32, (TM, GLA_DK), 0)
    eye = row == col

    def stack_heads(t):
        return jnp.concatenate([jnp.where(lane_head == hd, t, 0.0) for hd in range(GLA_HEADS)], axis=0).astype(BF16)

    span = jnp.max(-g_tot)

    @pl.when(span < GLA_SAFE_SPAN)
    def _():
        res = _dot_nt(stack_heads(q * jnp.exp(g_cum)), (k * jnp.exp(-g_cum)).astype(BF16))
        seen = (col - row) * (1 - 2 * d) <= 0
        for hd in range(GLA_HEADS):
            a_ref[hd] = jnp.where(seen, res[hd * TM:(hd + 1) * TM], 0.0)

    @pl.when(span >= GLA_SAFE_SPAN)
    def _():
        e2l = _dot(mall_ref[0, TM:, :], g2)
        e_lv = e2l[:, :GLA_DK] + e2l[:, GLA_DK:]
        res = _dot_nt(stack_heads(q), k.astype(BF16))
        for hd in range(GLA_HEADS):
            a_ref[hd] = jnp.where(eye, res[hd * TM:(hd + 1) * TM], 0.0)
        for lv in range(GLA_LEVELS):
            m = TM >> (lv + 1)
            w = jnp.exp(e_lv[lv * TM:(lv + 1) * TM])
            bit = (tok // m) % 2
            q_act = bit != d
            qt = jnp.where(q_act, q * w, 0.0)
            kt = jnp.where(q_act, 0.0, k * w).astype(BF16)
            res = _dot_nt(stack_heads(qt), kt)
            if m == TM // 2:
                for hd in range(GLA_HEADS):
                    a_ref[hd] += res[hd * TM:(hd + 1) * TM]
            else:
                same = (row // (2 * m)) == (col // (2 * m))
                for hd in range(GLA_HEADS):
                    a_ref[hd] += jnp.where(same, res[hd * TM:(hd + 1) * TM], 0.0)

    s_old = s_ref[...]
    o_inter = _dot((q * jnp.exp(g_cum)).astype(BF16), s_old.astype(BF16))
    for hd in range(GLA_HEADS):
        sl = slice(hd * GLA_HV, (hd + 1) * GLA_HV)
        o_intra = _dot(a_ref[hd].astype(BF16), v[:, sl])
        o_ref[0, 0, :, sl] = (o_intra + o_inter[:, sl]).astype(o_ref.dtype)

    kdec_t = (k * jnp.exp(g_tot - g_cum)).T.astype(BF16)
    upd = _dot(kdec_t, v)
    same_head = (lax.broadcasted_iota(jnp.int32, (GLA_DK, GLA_DV), 0) // GLA_HK
                 == lax.broadcasted_iota(jnp.int32, (GLA_DK, GLA_DV), 1) // GLA_HV)
    dec_col = jnp.sum(jnp.where(eye, jnp.broadcast_to(jnp.exp(g_tot), (TM, GLA_DK)), 0.0), axis=1, keepdims=True)
    s_ref[...] = dec_col * s_old + jnp.where(same_head, upd, 0.0)


def _gla(gq, gk, gv, glog, n_lat_tiles):
    assert GLA_DK == TM
    bsz, t, _ = gq.shape
    nt = t // TM
    mall = jnp.asarray(_gla_level_matrices(), dtype=BF16)

    def tile_idx(d, s):
        return jnp.where(s == 0, n_lat_tiles, jnp.where(d == 0, s - 1, n_lat_tiles - s))

    tok = lambda w: pl.BlockSpec((1, TM, w), lambda b, d, s: (b, tile_idx(d, s), 0))
    return pl.pallas_call(
        _gla_kernel,
        out_shape=jax.ShapeDtypeStruct((bsz, 2, t, GLA_DV), BF16),
        grid=(bsz, 2, nt),
        in_specs=[tok(GLA_DK), tok(GLA_DK), tok(GLA_DV),
                  pl.BlockSpec((1, TM, GLA_DK), lambda b, d, s: (b, tile_idx(d, s), d)),
                  pl.BlockSpec((1,) + mall.shape[1:], lambda b, d, s: (d, 0, 0))],
        out_specs=pl.BlockSpec((1, 1, TM, GLA_DV), lambda b, d, s: (b, d, tile_idx(d, s), 0)),
        scratch_shapes=[pltpu.VMEM((GLA_DK, GLA_DV), F32), pltpu.VMEM((GLA_HEADS, TM, TM), F32)],
        compiler_params=_cparams(3),
        name="gla_scan",
    )(gq, gk, gv, glog, mall)


HY_N1 = 64
HY_N2 = 128
HY_SLAB = SUBLANES
HY_CB = 256
HY_KG = 16
HY_SPEC_CB = 128
HY_UNROLL = 8


def _hyena_dft_constants(seq):
    n = 2 * seq
    assert n == HY_N1 * HY_N2
    nh = HY_N1 // 2
    kh = HY_N2 // 2
    eye = np.eye(HY_SLAB)
    k1 = np.arange(HY_N1)
    th = 2 * np.pi * np.outer(k1 + 0.5, np.arange(nh)) / HY_N1
    fwd1 = np.concatenate([np.kron(np.cos(th), eye), np.kron(-np.sin(th), eye)], axis=0)
    inv1 = (2.0 / n) * np.concatenate([np.kron(np.cos(th).T, eye), np.kron(-np.sin(th).T, eye)], axis=1)
    nlo = np.arange(HY_N2)
    k2 = np.arange(kh)
    ph = 2 * np.pi * (k2[None, :, None] * nlo[None, None, :] / HY_N2
                      + (k1[:, None, None] + 0.5) * nlo[None, None, :] / n)
    c, s = np.cos(ph), np.sin(ph)
    fwd2 = np.concatenate([np.concatenate([c, s], axis=2), np.concatenate([-s, c], axis=2)], axis=1)
    ct, st = c.transpose(0, 2, 1), s.transpose(0, 2, 1)
    inv2 = np.concatenate([np.concatenate([ct, -st], axis=2), np.concatenate([st, ct], axis=2)], axis=1)
    return fwd1, fwd2, inv2, inv1


def _hyena_ctx_dft_constants(ctx_len):
    n = 2 * ctx_len
    th = 2 * np.pi * np.outer(np.arange(ctx_len) + 0.5, np.arange(ctx_len)) / n
    fwd = np.concatenate([np.cos(th), -np.sin(th)], axis=0)
    inv = (2.0 / n) * np.concatenate([np.cos(th).T, -np.sin(th).T], axis=1)
    return fwd, inv


def _hyena_features(length):
    pos = np.arange(length, dtype=np.float64)
    t = pos / max(length - 1, 1)
    f = np.linspace(1e-4, HY_BANDS - 1, HY_BANDS)
    ang = (2.0 * math.pi / length) * pos[:, None] * f
    feat = np.concatenate([t[:, None], np.cos(ang), np.sin(ang)], axis=-1)
    return jnp.asarray(np.pad(feat, ((0, 0), (0, LANES - HY_POS_DIM))), F32)


def _hy_filter_kernel(feat_ref, w1_ref, b1_ref, w2_ref, b2_ref, w3_ref, b3_ref, absd_ref, h_ref, s_ref):
    i = pl.program_id(0)
    feat = feat_ref[...]
    hp = lambda a, b: jnp.dot(a, b, preferred_element_type=F32, precision=HIGHEST)
    hdn = jnp.sin(hp(feat, w1_ref[...]) + b1_ref[...])
    hdn = jnp.sin(hp(hdn, w2_ref[...]) + b2_ref[...])
    h = hp(hdn, w3_ref[...]) + b3_ref[...]
    window = jnp.exp(-feat[:, 0:1] * absd_ref[...])
    h = h * jnp.concatenate([window] * (2 * HY_ORDER), axis=1)
    h_ref[...] = h

    @pl.when(i == 0)
    def _():
        s_ref[...] = jnp.zeros(s_ref.shape, F32)

    s_ref[...] += jnp.sum(jnp.abs(h), axis=0, keepdims=True)


def _hyena_filters_raw(length, filt_w):
    w1, b1, w2, b2, w3, b3 = filt_w
    nf = 2 * HY_ORDER * HY_WIDTH
    tr = min(length, 512)
    deltas = np.linspace(math.log(HY_DECAY_TARGET) / HY_FAST_DECAY, math.log(HY_DECAY_TARGET) / HY_SLOW_DECAY,
                         HY_WIDTH, dtype=np.float32)
    absd = jnp.asarray(np.abs(deltas))[None, :]
    w1p = jnp.pad(w1, ((0, LANES - HY_POS_DIM), (0, 0)))
    full = lambda shp: pl.BlockSpec(shp, lambda i: (0,) * len(shp))
    return pl.pallas_call(
        _hy_filter_kernel,
        out_shape=(jax.ShapeDtypeStruct((length, nf), F32), jax.ShapeDtypeStruct((1, nf), F32)),
        grid=(length // tr,),
        in_specs=[pl.BlockSpec((tr, LANES), lambda i: (i, 0)), full((LANES, HY_FILTER_HIDDEN)),
                  full((1, HY_FILTER_HIDDEN)), full((HY_FILTER_HIDDEN, HY_FILTER_HIDDEN)), full((1, HY_FILTER_HIDDEN)),
                  full((HY_FILTER_HIDDEN, nf)), full((1, nf)), full((1, HY_WIDTH))],
        out_specs=(pl.BlockSpec((tr, nf), lambda i: (i, 0)), full((1, nf))),
        compiler_params=_cparams(1),
        name="hyena_filter_mlp",
    )(_hyena_features(length), w1p, b1[None, :], w2, b2[None, :], w3, b3[None, :], absd)


def _dot_split(m, x):
    x_hi = x.astype(BF16)
    x_lo = (x - x_hi.astype(F32)).astype(BF16)
    n = x.shape[1]
    r = _dot(m, jnp.concatenate([x_hi, x_lo], axis=1))
    return r[:, :n] + r[:, n:]


def _odft_stage1(src_at, mm, s_re, s_im, unroll=2):
    nk = s_re.shape[0]
    half = nk * HY_SLAB

    def body(j, carry):
        r0 = pl.multiple_of(j * HY_SLAB, HY_SLAB)
        slab = src_at(r0)
        cb = slab.shape[-1]
        res = mm(slab.reshape(-1, cb))
        s_re[:, pl.ds(r0, HY_SLAB), :] = res[:half].reshape(nk, HY_SLAB, cb)
        s_im[:, pl.ds(r0, HY_SLAB), :] = res[half:].reshape(nk, HY_SLAB, cb)
        return carry

    lax.fori_loop(0, HY_N2 // HY_SLAB, body, 0, unroll=unroll)


def _hy_spectrum_kernel(hf_ref, hb_ref, sf_ref, sb_ref, fwd1_ref, fwd2_ref, o_ref, s_re, s_im):
    kh = HY_N2 // 2
    mm1 = lambda x: _dot_split(fwd1_ref[...], x)

    def middle(sign):
        def body(k1, carry):
            a = jnp.concatenate([s_re[k1], s_im[k1]], axis=0)
            x = _dot_split(fwd2_ref[k1], a)
            if sign is None:
                o_ref[0, k1] = x[:kh]
                o_ref[1, k1] = x[kh:]
            else:
                inv_norm = 1.0 / (sf_ref[...] + sb_ref[...])
                o_ref[0, k1] = (o_ref[0, k1] + x[:kh]) * inv_norm
                o_ref[1, k1] = (o_ref[1, k1] - x[kh:]) * inv_norm
            return carry
        lax.fori_loop(0, HY_N1, body, 0, unroll=8)

    _odft_stage1(lambda r0: hf_ref[:, pl.ds(r0, HY_SLAB), :], mm1, s_re, s_im, unroll=4)
    middle(None)

    def bwd_slab(r0):
        slab = hb_ref[:, pl.ds(r0, HY_SLAB), :]
        nhi = lax.broadcasted_iota(jnp.int32, slab.shape, 0)
        r = lax.broadcasted_iota(jnp.int32, slab.shape, 1)
        return jnp.where((nhi == 0) & (r + r0 == 0), 0.0, slab)

    _odft_stage1(bwd_slab, mm1, s_re, s_im, unroll=4)
    middle(-1)


def _hyena_filter_spectrum(seq, w1, b1, w2, b2, w3, b3):
    h_raw, s = _hyena_filters_raw(seq, (w1, b1, w2, b2, w3, b3))
    nh = HY_N1 // 2
    nc = HY_ORDER * HY_WIDTH
    h3 = h_raw.reshape(nh, HY_N2, 2 * nc)
    fwd1, fwd2, _, _ = _hyena_dft_constants(seq)
    fwd1 = jnp.asarray(fwd1, BF16)
    fwd2 = jnp.asarray(fwd2, BF16)
    scb = HY_SPEC_CB
    ncb = nc // scb
    return pl.pallas_call(
        _hy_spectrum_kernel,
        out_shape=jax.ShapeDtypeStruct((2, HY_N1, HY_N2 // 2, nc), F32),
        grid=(ncb,),
        in_specs=[pl.BlockSpec((nh, HY_N2, scb), lambda c: (0, 0, c)),
                  pl.BlockSpec((nh, HY_N2, scb), lambda c: (0, 0, ncb + c)),
                  pl.BlockSpec((1, scb), lambda c: (0, c)),
                  pl.BlockSpec((1, scb), lambda c: (0, ncb + c)),
                  _const_spec(fwd1.shape), _const_spec(fwd2.shape)],
        out_specs=pl.BlockSpec((2, HY_N1, HY_N2 // 2, scb), lambda c: (0, 0, 0, c)),
        scratch_shapes=[pltpu.VMEM((HY_N1, HY_N2, scb), F32), pltpu.VMEM((HY_N1, HY_N2, scb), F32)],
        compiler_params=_cparams(1, VMEM_LIMIT),
        name="hyena_filter_spectrum",
    )(h3, h3, s, s, fwd1, fwd2)


def _short_conv_chunk(ref, c, n_chunks, w_ref, b_ref):
    per = TM // HY_N2
    cur = ref[0, pl.ds(per * c, per)]
    cb = cur.shape[-1]
    cur = cur.reshape(TM, cb)
    prev = ref[0, jnp.maximum(per * c - 1, 0), pl.ds(HY_N2 - SUBLANES, SUBLANES), :][SUBLANES - 1:SUBLANES]
    nxt = ref[0, jnp.minimum(per * c + per, per * n_chunks - 1), pl.ds(0, SUBLANES), :][0:1]
    prev = jnp.where(c > 0, prev, 0.0)
    nxt = jnp.where(c < n_chunks - 1, nxt, 0.0)
    rowi = lax.broadcasted_iota(jnp.int32, (TM, cb), 0)
    dn = jnp.where(rowi == 0, prev, pltpu.roll(cur, 1, 0))
    up = jnp.where(rowi == TM - 1, nxt, pltpu.roll(cur, TM - 1, 0))
    return b_ref[...] + w_ref[0:1, :] * dn + w_ref[1:2, :] * cur + w_ref[2:3, :] * up


def _hy_conv_kernel(*refs, conv_y):
    if conv_y:
        (y_ref, g_ref, h_ref, bias_ref, wy_ref, by_ref, wg_ref, bg_ref, fwd1_ref, fwd2_ref, inv2_ref, inv1_ref,
         o_ref, s_re, s_im, gs_ref, us_ref) = refs
    else:
        (y_ref, g_ref, h_ref, bias_ref, wg_ref, bg_ref, fwd1_ref, fwd2_ref, inv2_ref, inv1_ref,
         o_ref, s_re, s_im, gs_ref) = refs
    grp = pl.program_id(2)
    last = pl.num_programs(2) - 1
    nh = HY_N1 // 2
    per = TM // HY_N2
    n_chunks = nh // per
    cb = o_ref.shape[-1]
    kh = HY_N2 // 2

    @pl.when(grp == 0)
    def _():
        def pre(c, carry):
            gs_ref[pl.ds(per * c, per)] = _short_conv_chunk(g_ref, c, n_chunks, wg_ref, bg_ref).reshape(per, HY_N2, cb)
            if conv_y:
                us_ref[pl.ds(per * c, per)] = _short_conv_chunk(y_ref, c, n_chunks, wy_ref, by_ref).reshape(
                    per, HY_N2, cb)
            return carry
        lax.fori_loop(0, n_chunks, pre, 0)

    if conv_y:
        u_at = lambda r0: us_ref[:, pl.ds(r0, HY_SLAB), :]
    else:
        u_at = lambda r0: y_ref[0, :, pl.ds(r0, HY_SLAB), :]

    _odft_stage1(u_at, lambda x: _dot(fwd1_ref[0], x.astype(BF16)), s_re, s_im, unroll=HY_UNROLL)

    def middle(k1, carry):
        a = jnp.concatenate([s_re[k1], s_im[k1]], axis=0).astype(BF16)
        x = _dot(fwd2_ref[k1], a)
        xr, xi = x[:kh], x[kh:]
        hr, hi = h_ref[0, k1], h_ref[1, k1]
        y = jnp.concatenate([xr * hr - xi * hi, xr * hi + xi * hr], axis=0).astype(BF16)
        bm = _dot(inv2_ref[k1], y)
        s_re[k1] = bm[:HY_N2]
        s_im[k1] = bm[HY_N2:]
        return carry

    lax.fori_loop(0, HY_KG, middle, 0, unroll=2 * HY_UNROLL)

    def partial_conv(r0):
        slab = jnp.concatenate([s_re[:, pl.ds(r0, HY_SLAB), :].reshape(HY_KG * HY_SLAB, cb),
                                s_im[:, pl.ds(r0, HY_SLAB), :].reshape(HY_KG * HY_SLAB, cb)], axis=0).astype(BF16)
        return _dot(inv1_ref[0], slab).reshape(nh, HY_SLAB, cb)

    def post_loop(fn):
        def post(j, carry):
            r0 = pl.multiple_of(j * HY_SLAB, HY_SLAB)
            o_ref[0, :, pl.ds(r0, HY_SLAB), :] = fn(r0, partial_conv(r0))
            return carry
        lax.fori_loop(0, HY_N2 // HY_SLAB, post, 0, unroll=HY_UNROLL)

    @pl.when(grp == 0)
    def _():
        post_loop(lambda r0, part: part)

    @pl.when((grp > 0) & (grp < last))
    def _():
        post_loop(lambda r0, part: o_ref[0, :, pl.ds(r0, HY_SLAB), :] + part)

    @pl.when(grp == last)
    def _():
        post_loop(lambda r0, part: gs_ref[:, pl.ds(r0, HY_SLAB), :]
                  * (o_ref[0, :, pl.ds(r0, HY_SLAB), :] + part + bias_ref[...] * u_at(r0)))


def _hyena_order(y4, y_col0, z4, gate_col0, hspec, order, hy_bias, short_w, short_b, consts, conv_y, rows_out):
    bsz = z4.shape[0]
    nh = HY_N1 // 2
    ncb = HY_WIDTH // HY_CB
    ngrp = HY_N1 // HY_KG
    fwd1, fwd2, inv2, inv1 = consts
    blk4 = lambda off: pl.BlockSpec((1, nh, HY_N2, HY_CB), lambda c, b, g: (b, 0, 0, off + c))
    rowspec = lambda rows, off: pl.BlockSpec((rows, HY_CB), lambda c, b, g: (0, off + c))
    grouped = lambda shp: pl.BlockSpec(shp, lambda c, b, g: (g, 0, 0))
    in_specs = [blk4(y_col0), blk4(gate_col0),
                pl.BlockSpec((2, HY_KG, HY_N2 // 2, HY_CB), lambda c, b, g: (0, g, 0, order * ncb + c)),
                rowspec(1, 0)]
    args = [y4, z4, hspec, hy_bias.reshape(1, -1)]
    if conv_y:
        in_specs += [rowspec(HY_SHORT, y_col0), rowspec(1, y_col0)]
        args += [short_w, short_b[None, :]]
    in_specs += [rowspec(HY_SHORT, gate_col0), rowspec(1, gate_col0)]
    args += [short_w, short_b[None, :]]
    in_specs += [grouped((1,) + fwd1.shape[1:]), grouped((HY_KG,) + fwd2.shape[1:]),
                 grouped((HY_KG,) + inv2.shape[1:]), grouped((1,) + inv1.shape[1:])]
    args += [fwd1, fwd2, inv2, inv1]
    scratch = [pltpu.VMEM((HY_KG, HY_N2, HY_CB), F32), pltpu.VMEM((HY_KG, HY_N2, HY_CB), F32),
               pltpu.VMEM((nh, HY_N2, HY_CB), F32)]
    if conv_y:
        scratch.append(pltpu.VMEM((nh, HY_N2, HY_CB), F32))
    return pl.pallas_call(
        functools.partial(_hy_conv_kernel, conv_y=conv_y),
        out_shape=jax.ShapeDtypeStruct((bsz, rows_out, HY_N2, HY_WIDTH), F32),
        grid=(ncb, bsz, ngrp),
        in_specs=in_specs,
        out_specs=pl.BlockSpec((1, nh, HY_N2, HY_CB), lambda c, b, g: (b, 0, 0, c)),
        scratch_shapes=scratch,
        compiler_params=_cparams(3, VMEM_LIMIT),
        name="hyena_conv%d" % order,
    )(*args)


def _hyena_latent(z_hy, hspec, short_w, short_b, hy_bias, with_ctx_rows):
    bsz, t, _ = z_hy.shape
    seq = HY_N1 * HY_N2 // 2
    fwd1, fwd2, inv2, inv1 = _hyena_dft_constants(seq)
    ngrp = HY_N1 // HY_KG
    rows = HY_KG * HY_SLAB
    fwd1 = fwd1.reshape(2, ngrp, rows, -1).transpose(1, 0, 2, 3).reshape(ngrp, 2 * rows, -1)
    inv1 = inv1.reshape(-1, 2, ngrp, rows).transpose(2, 0, 1, 3).reshape(ngrp, -1, 2 * rows)
    consts = tuple(jnp.asarray(m, BF16) for m in (fwd1, fwd2, inv2, inv1))
    z4 = z_hy.reshape(bsz, t // HY_N2, HY_N2, 3 * HY_WIDTH)
    ncb = HY_WIDTH // HY_CB
    nh = seq // HY_N2
    y1 = _hyena_order(z4, 2 * ncb, z4, 0, hspec, 0, hy_bias[0], short_w, short_b, consts, True, nh)
    rows_out = t // HY_N2 if with_ctx_rows else nh
    y2 = _hyena_order(y1, 0, z4, ncb, hspec, 1, hy_bias[1], short_w, short_b, consts, False, rows_out)
    return y2.reshape(bsz, rows_out * HY_N2, HY_WIDTH)


def _hy_ctx_spectrum_kernel(h_ref, s_ref, fwd_ref, o_ref):
    lc = h_ref.shape[0]
    nc = HY_ORDER * HY_WIDTH
    hp = lambda a, b: jnp.dot(a, b, preferred_element_type=F32, precision=HIGHEST)
    h = h_ref[...]
    rowi = lax.broadcasted_iota(jnp.int32, (lc, nc), 0)
    xf = hp(fwd_ref[...], h[:, :nc])
    xb = hp(fwd_ref[...], jnp.where(rowi == 0, 0.0, h[:, nc:]))
    inv_norm = 1.0 / (s_ref[:, :nc] + s_ref[:, nc:])
    o_ref[0] = (xf[:lc] + xb[:lc]) * inv_norm
    o_ref[1] = (xf[lc:] - xb[lc:]) * inv_norm


def _hy_ctx_conv_kernel(x1_ref, x2_ref, v_ref, h_ref, bias_ref, w_ref, b_ref, fwd_ref, inv_ref, _, o_ref):
    lc = o_ref.shape[1]
    rowi = lax.broadcasted_iota(jnp.int32, (lc, HY_WIDTH), 0)

    def short(ref, part):
        cur = ref[0].reshape(lc, HY_WIDTH)
        sl = slice(part * HY_WIDTH, (part + 1) * HY_WIDTH)
        dn = jnp.where(rowi == 0, 0.0, pltpu.roll(cur, 1, 0))
        up = jnp.where(rowi == lc - 1, 0.0, pltpu.roll(cur, lc - 1, 0))
        return b_ref[:, sl] + w_ref[0:1, sl] * dn + w_ref[1:2, sl] * cur + w_ref[2:3, sl] * up

    y = short(v_ref, 2)
    for order, gref in enumerate((x1_ref, x2_ref)):
        sl = slice(order * HY_WIDTH, (order + 1) * HY_WIDTH)
        x = _dot(fwd_ref[...], y.astype(BF16))
        xr, xi = x[:lc], x[lc:]
        hr, hi = h_ref[0, :, sl], h_ref[1, :, sl]
        prod = jnp.concatenate([xr * hr - xi * hi, xr * hi + xi * hr], axis=0).astype(BF16)
        conv = _dot(inv_ref[...], prod)
        y = short(gref, order) * (conv + bias_ref[order:order + 1, :] * y)
    o_ref[0] = y


def _hyena_ctx(z_hy, y_lat, filt_w, short_w, short_b, hy_bias):
    bsz, t, _ = z_hy.shape
    seq = HY_N1 * HY_N2 // 2
    lc = t - seq
    per = lc // HY_N2
    h_raw, s = _hyena_filters_raw(lc, filt_w)
    fwd, inv = _hyena_ctx_dft_constants(lc)
    nc = HY_ORDER * HY_WIDTH
    full = lambda shp: pl.BlockSpec(shp, lambda *_: (0,) * len(shp))
    hspec = pl.pallas_call(
        _hy_ctx_spectrum_kernel,
        out_shape=jax.ShapeDtypeStruct((2, lc, nc), F32),
        grid=(1,),
        in_specs=[full(h_raw.shape), full(s.shape), full(fwd.shape)],
        out_specs=full((2, lc, nc)),
        compiler_params=_cparams(1),
        name="hyena_ctx_spectrum",
    )(h_raw, s, jnp.asarray(fwd, F32))
    z4 = z_hy.reshape(bsz, t // HY_N2, HY_N2, 3 * HY_WIDTH)
    blk = lambda part: pl.BlockSpec((1, per, HY_N2, HY_WIDTH), lambda b: (b, seq // lc, 0, part))
    return pl.pallas_call(
        _hy_ctx_conv_kernel,
        out_shape=jax.ShapeDtypeStruct((bsz, t, HY_WIDTH), F32),
        grid=(bsz,),
        in_specs=[blk(0), blk(1), blk(2), full((2, lc, nc)), full((HY_ORDER, HY_WIDTH)),
                  full((HY_SHORT, 3 * HY_WIDTH)), full((1, 3 * HY_WIDTH)), full(fwd.shape), full(inv.shape),
                  pl.BlockSpec(memory_space=pl.ANY)],
        out_specs=pl.BlockSpec((1, lc, HY_WIDTH), lambda b: (b, seq // lc, 0)),
        input_output_aliases={9: 0},
        compiler_params=_cparams(1),
        name="hyena_ctx_conv",
    )(z4, z4, z4, hspec, hy_bias, short_w, short_b[None, :], jnp.asarray(fwd, BF16), jnp.asarray(inv, BF16), y_lat)


FF_CHUNK = 1024


def _merge_mlp_kernel(x_ref, mod_ref, ymla_ref, of_ref, ob_ref, gr_ref, yhy_ref, zg_ref, on_ref,
                      wm_ref, wgl_ref, wh_ref, wo_ref, g2_ref, w1_ref, w2_ref, fg_ref, o_ref, *, final):
    o = of_ref[0, 0].astype(F32) + ob_ref[0, 0].astype(F32)
    silu = gr_ref[0].astype(F32)
    parts = []
    for hd in range(GLA_HEADS):
        sl = slice(hd * GLA_HV, (hd + 1) * GLA_HV)
        parts.append((_rms(o[:, sl]) * on_ref[...] * silu[:, sl]).astype(BF16))
    y_gla = jnp.concatenate(parts, axis=1)
    zg = zg_ref[0].astype(F32)
    d = x_ref.shape[-1]
    m = zg[:, 0:d] * _dot(ymla_ref[0], wm_ref[...])
    m = m + zg[:, d:2 * d] * _dot(y_gla, wgl_ref[...])
    m = m + zg[:, 2 * d:3 * d] * _dot(yhy_ref[0].astype(BF16), wh_ref[...])
    out = _dot(m.astype(BF16), wo_ref[...])
    x = x_ref[0] + mod_ref[0, 2:3, :] * out

    h = (_rms(x) * g2_ref[...] * (1.0 + mod_ref[0, 4:5, :]) + mod_ref[0, 3:4, :]).astype(BF16)
    acc = jnp.zeros(x.shape, F32)
    for j in range(w1_ref.shape[1] // FF_CHUNK):
        a = jnp.maximum(_dot(h, w1_ref[:, j * FF_CHUNK:(j + 1) * FF_CHUNK]), 0.0)
        acc = acc + _dot((a * a).astype(BF16), w2_ref[j * FF_CHUNK:(j + 1) * FF_CHUNK, :])
    xn = x + mod_ref[0, 5:6, :] * acc
    if final:
        xn = _rms(xn) * fg_ref[...]
    o_ref[0] = xn


def _mod_spec(d, n_lat_tiles, bsz):
    return pl.BlockSpec((1, 6, d), lambda b, i: (jnp.where(i < n_lat_tiles, b, bsz), 0, 0))


def _merge_mlp(xc, mod_l, y_mla, o_gla, gr, y_hy, gate, out_norm, w_o_mla, w_o_gla, w_o_hy, w_out,
               g2, w1, w2, final_g, n_tiles, n_lat_tiles, final):
    bsz, t, d = xc.shape
    tile = lambda w: pl.BlockSpec((1, TM, w), lambda b, i: (b, i, 0))
    row = lambda w: pl.BlockSpec((1, w), lambda b, i: (0, 0))
    dirspec = lambda dr: pl.BlockSpec((1, 1, TM, GLA_DV), lambda b, i: (b, dr, i, 0))
    bf = lambda w: w.astype(BF16)
    return pl.pallas_call(
        functools.partial(_merge_mlp_kernel, final=final),
        out_shape=jax.ShapeDtypeStruct((bsz, n_tiles * TM, d), F32),
        grid=(bsz, n_tiles),
        in_specs=[tile(d), _mod_spec(d, n_lat_tiles, bsz), tile(MLA_OUT), dirspec(0), dirspec(1), tile(GLA_DV),
                  tile(HY_WIDTH), tile(3 * d), row(GLA_HV),
                  _const_spec(w_o_mla.shape), _const_spec(w_o_gla.shape), _const_spec(w_o_hy.shape),
                  _const_spec(w_out.shape), row(d), _const_spec(w1.shape), _const_spec(w2.shape), row(d)],
        out_specs=tile(d),
        compiler_params=_cparams(2, VMEM_LIMIT),
        name="merge_mlp",
    )(xc, mod_l, y_mla, o_gla, o_gla, gr, y_hy, gate, out_norm[None, :], bf(w_o_mla), bf(w_o_gla), bf(w_o_hy),
      bf(w_out), g2[None, :], bf(w1), bf(w2), final_g[None, :])


def kernel(x, c, ctx, c_ctx, ada_w, ada_b, norm1_g, norm2_g, w_in, mla_q_norm, mla_w_uq, mla_kv_norm, mla_w_ukv, gla_w_a2, gla_b_a, gla_out_norm, hy_short_w, hy_short_b, hy_f_w1, hy_f_b1, hy_f_w2, hy_f_b2, hy_f_w3, hy_f_b3, hy_bias, w_o_mla, w_o_gla, w_o_hy, w_out, ff_w1, ff_w2, final_norm_g):
    bsz, seq, d = x.shape
    ctx_len = ctx.shape[1]
    n_lat = seq // TM
    n_all = (seq + ctx_len) // TM
    xc = jnp.concatenate([x, ctx], axis=1)
    cc = jnp.zeros((16, d), F32).at[:bsz].set(c).at[bsz].set(c_ctx)
    mod = _modulation(cc, ada_w, ada_b).reshape(DEPTH, 16, 6, d)
    cos, sin = _rope_tables(seq, ctx_len)
    for l in range(DEPTH):
        last = l == DEPTH - 1
        n_tiles = n_lat if last else n_all
        weights = _prep_inproj_weights(w_in[l], mla_w_uq[l], mla_w_ukv[l], gla_w_a2[l], gla_b_a[l])
        q, k, v, gq, gk, gv, gr, glog, z_hy, z_gate = _inproj(xc, mod[l], norm1_g[l], weights, mla_q_norm[l],
                                                             mla_kv_norm[l], cos, sin, n_lat)
        y_mla = _attention(q, k, v, seq, not last)
        o_gla = _gla(gq, gk, gv, glog, n_lat)
        filt_w = (hy_f_w1[l], hy_f_b1[l], hy_f_w2[l], hy_f_b2[l], hy_f_w3[l], hy_f_b3[l])
        hspec = _hyena_filter_spectrum(seq, *filt_w)
        y_hy = _hyena_latent(z_hy, hspec, hy_short_w[l], hy_short_b[l], hy_bias[l], not last)
        if not last:
            y_hy = _hyena_ctx(z_hy, y_hy, filt_w, hy_short_w[l], hy_short_b[l], hy_bias[l])
        xc = _merge_mlp(xc, mod[l], y_mla, o_gla, gr, y_hy, z_gate, gla_out_norm[l], w_o_mla[l], w_o_gla[l],
                        w_o_hy[l], w_out[l], norm2_g[l], ff_w1[l], ff_w2[l], final_norm_g, n_tiles, n_lat, last)
    return xc
```

```python
import functools
import math

import numpy as np
import jax
import jax.numpy as jnp
from jax import lax
from jax.experimental import pallas as pl
from jax.experimental.pallas import tpu as pltpu

F32 = jnp.float32
BF16 = jnp.bfloat16
HIGHEST = lax.Precision.HIGHEST
LOG2E = 1.4426950408889634

D_MODEL = 1024
DEPTH = 2
GRID_W = 64
EPS = 1e-6
MLA_HEADS = 8
MLA_NOPE = 64
MLA_ROPE = 32
MLA_V = 64
MLA_Q_LORA = 256
MLA_KV_LORA = 128
MLA_SCALE = (MLA_NOPE + MLA_ROPE) ** -0.5
ROPE_BASE = 10000.0
GLA_HEADS = 4
GLA_DK = 256
GLA_DV = 512
GLA_HK = GLA_DK // GLA_HEADS
GLA_HV = GLA_DV // GLA_HEADS
GLA_GATE_RANK = 16
GLA_TAU = 16.0
HY_WIDTH = 512
HY_ORDER = 2
HY_SHORT = 3
HY_BANDS = 16
HY_POS_DIM = 1 + 2 * HY_BANDS
HY_FILTER_HIDDEN = 64
HY_FAST_DECAY = 0.3
HY_SLOW_DECAY = 1.5
HY_DECAY_TARGET = 1e-2
D_FF = 4 * D_MODEL
MLA_OUT = MLA_HEADS * MLA_V
IN_SIZES = (MLA_Q_LORA, MLA_KV_LORA, MLA_ROPE, GLA_DK, GLA_DK, GLA_DV, GLA_DV, GLA_GATE_RANK, GLA_GATE_RANK,
            (HY_ORDER + 1) * HY_WIDTH, 3 * D_MODEL)

LANES = 128
SUBLANES = 8
TM = 256
HEAD_SLOT = 128
VMEM_LIMIT = 56 * 1024 * 1024


def _cparams(n_axes, vmem=None):
    return pltpu.CompilerParams(dimension_semantics=("arbitrary",) * n_axes, vmem_limit_bytes=vmem)


def _const_spec(shape):
    nd = len(shape)
    return pl.BlockSpec(shape, lambda *_: (0,) * nd, pipeline_mode=pl.Buffered(1))


def _rms(x):
    return x * lax.rsqrt(jnp.mean(x * x, axis=-1, keepdims=True) + EPS)


def _sigmoid(x):
    return 1.0 / (1.0 + jnp.exp(-x))


def _dot(a, b):
    return jnp.dot(a, b, preferred_element_type=F32)


def _dot_nt(a, b):
    return lax.dot_general(a, b, (((1,), (1,)), ((), ())), preferred_element_type=F32)


def _mod_kernel(cc_ref, w_ref, b_ref, o_ref):
    s = cc_ref[...]
    s = s * _sigmoid(s)
    o_ref[0] = jnp.dot(s, w_ref[0], preferred_element_type=F32, precision=HIGHEST) + b_ref[0]


def _modulation(cc, ada_w, ada_b):
    tn = 1536
    n6 = ada_w.shape[-1]
    return pl.pallas_call(
        _mod_kernel,
        out_shape=jax.ShapeDtypeStruct((DEPTH, 16, n6), F32),
        grid=(DEPTH, n6 // tn),
        in_specs=[
            pl.BlockSpec((16, D_MODEL), lambda l, j: (0, 0)),
            pl.BlockSpec((1, D_MODEL, tn), lambda l, j: (l, 0, j)),
            pl.BlockSpec((1, 1, tn), lambda l, j: (l, 0, j)),
        ],
        out_specs=pl.BlockSpec((1, 16, tn), lambda l, j: (l, 0, j)),
        compiler_params=_cparams(2),
        name="modulation",
    )(cc, ada_w, ada_b.reshape(DEPTH, 1, n6))


W_A = 768
W_G = 2 * GLA_DK + 2 * GLA_DV


def _token_specs(d, n_lat_tiles, ctx_blk):
    return [pl.BlockSpec((1, TM, d), lambda b, i: (b, jnp.minimum(i, n_lat_tiles - 1), 0)),
            pl.BlockSpec((1, TM, d), lambda b, i: (b, ctx_blk, 0))]


def _token_tile(x_ref, c_ref, n_lat_tiles):
    return jnp.where(pl.program_id(1) < n_lat_tiles, x_ref[0], c_ref[0])


def _inproj_kernel(x_ref, c_ref, mod_ref, g1_ref, wa_ref, wg_ref, wh_ref, wz_ref, qn_ref, kvn_ref, wuq_ref, wukv_ref,
                   wa2_ref, ba_ref, cos_ref, sin_ref,
                   q_out, k_out, v_out, gq_out, gk_out, gv_out, gr_out, glog_out, hy_out, gate_out, *, n_lat_tiles):
    x = _token_tile(x_ref, c_ref, n_lat_tiles)
    shift = mod_ref[0, 0:1, :]
    scale = mod_ref[0, 1:2, :]
    h = (_rms(x) * g1_ref[...] * (1.0 + scale) + shift).astype(BF16)

    za = _dot(h, wa_ref[...])
    cos = cos_ref[...]
    sin = sin_ref[...]

    cqn = (_rms(za[:, 0:256]) * qn_ref[...]).astype(BF16)
    qab = _dot(cqn, wuq_ref[...])
    nq = MLA_HEADS * HEAD_SLOT
    for hd in range(MLA_HEADS):
        sl = slice(hd * HEAD_SLOT, (hd + 1) * HEAD_SLOT)
        qa = qab[:, hd * HEAD_SLOT:(hd + 1) * HEAD_SLOT]
        qb = qab[:, nq + hd * HEAD_SLOT:nq + (hd + 1) * HEAD_SLOT]
        q_out[0, :, sl] = ((qa * cos + qb * sin) * (MLA_SCALE * LOG2E)).astype(BF16)

    ckvn = (_rms(za[:, 256:384]) * kvn_ref[...]).astype(BF16)
    kv = _dot(ckvn, wukv_ref[...])
    krot = za[:, 384:512] * cos + za[:, 512:640] * sin
    for hd in range(MLA_HEADS):
        sl = slice(hd * HEAD_SLOT, (hd + 1) * HEAD_SLOT)
        k_out[0, :, sl] = (kv[:, sl] + krot).astype(BF16)
    ones_hi = (lax.broadcasted_iota(jnp.int32, (1, HEAD_SLOT), 1) >= MLA_V).astype(F32)
    for hd in range(MLA_HEADS):
        sl = slice(hd * HEAD_SLOT, (hd + 1) * HEAD_SLOT)
        v_out[0, :, sl] = (kv[:, nq + hd * HEAD_SLOT:nq + (hd + 1) * HEAD_SLOT] + ones_hi).astype(BF16)

    xg = _dot(za[:, 640:768].astype(BF16), wa2_ref[...]) + ba_ref[...]
    glog_out[0] = (jnp.minimum(xg, 0.0) - jnp.log(1.0 + jnp.exp(-jnp.abs(xg)))) * (1.0 / GLA_TAU)

    zg = _dot(h, wg_ref[...])
    gq_out[0] = (zg[:, 0:GLA_DK] * (GLA_HK ** -0.5)).astype(BF16)
    gk_out[0] = zg[:, GLA_DK:2 * GLA_DK].astype(BF16)
    gv_out[0] = zg[:, 2 * GLA_DK:2 * GLA_DK + GLA_DV].astype(BF16)
    gr = zg[:, 2 * GLA_DK + GLA_DV:]
    gr_out[0] = (gr * _sigmoid(gr)).astype(BF16)

    hy_out[0] = _dot(h, wh_ref[...])
    gate_out[0] = _sigmoid(_dot(h, wz_ref[...])).astype(BF16)


def _rope_partner(w):
    a = MLA_ROPE // 4
    perm = np.concatenate([np.arange(a, 2 * a), np.arange(0, a), np.arange(3 * a, 4 * a), np.arange(2 * a, 3 * a)])
    sign = np.concatenate([-np.ones(a), np.ones(a), -np.ones(a), np.ones(a)]).astype(np.float32)
    return w[:, perm] * sign


def _prep_inproj_weights(w_in, mla_w_uq, mla_w_ukv, gla_w_a2, gla_b_a):
    offs = np.concatenate([[0], np.cumsum(IN_SIZES)])
    seg = [w_in[:, offs[i]:offs[i + 1]] for i in range(len(IN_SIZES))]
    w_cq, w_ckv, w_kr, w_gq, w_gk, w_gv, w_gr, w_af, w_ab, w_hy, w_gate = seg
    d = w_in.shape[0]
    z = lambda n: jnp.zeros((d, n), w_in.dtype)
    kr_tile = jnp.concatenate([z(MLA_NOPE), w_kr, z(HEAD_SLOT - MLA_NOPE - MLA_ROPE)], axis=1)
    krp_tile = jnp.concatenate([z(MLA_NOPE), _rope_partner(w_kr), z(HEAD_SLOT - MLA_NOPE - MLA_ROPE)], axis=1)
    a_tile = jnp.concatenate([w_af, w_ab, z(LANES - 2 * GLA_GATE_RANK)], axis=1)
    wa = jnp.concatenate([w_cq, w_ckv, kr_tile, krp_tile, a_tile], axis=1)
    wg = jnp.concatenate([w_gq, w_gk, w_gv, w_gr], axis=1)

    dh = MLA_NOPE + MLA_ROPE
    zq = lambda n: jnp.zeros((MLA_Q_LORA, n), w_in.dtype)
    plain, partner = [], []
    for hd in range(MLA_HEADS):
        blk = mla_w_uq[:, hd * dh:(hd + 1) * dh]
        plain += [blk, zq(HEAD_SLOT - dh)]
        partner += [zq(MLA_NOPE), _rope_partner(blk[:, MLA_NOPE:]), zq(HEAD_SLOT - dh)]
    wuq = jnp.concatenate(plain + partner, axis=1)

    zk = jnp.zeros((MLA_KV_LORA, HEAD_SLOT - MLA_NOPE), w_in.dtype)
    kcols, vcols = [], []
    for hd in range(MLA_HEADS):
        blk = mla_w_ukv[:, hd * (MLA_NOPE + MLA_V):(hd + 1) * (MLA_NOPE + MLA_V)]
        kcols += [blk[:, :MLA_NOPE], zk]
        vcols += [blk[:, MLA_NOPE:], zk]
    wukv = jnp.concatenate(kcols + vcols, axis=1)

    wa2 = jnp.zeros((LANES, 2 * GLA_DK), w_in.dtype)
    wa2 = wa2.at[0:GLA_GATE_RANK, 0:GLA_DK].set(gla_w_a2[0])
    wa2 = wa2.at[GLA_GATE_RANK:2 * GLA_GATE_RANK, GLA_DK:].set(gla_w_a2[1])
    ba = jnp.concatenate([gla_b_a[0], gla_b_a[1]])[None, :]
    bf = lambda t: t.astype(BF16)
    return bf(wa), bf(wg), bf(w_hy), bf(w_gate), bf(wuq), bf(wukv), bf(wa2), ba


def _rope_tables(seq, ctx_len):
    rows = seq // GRID_W
    row = np.repeat(np.arange(rows, dtype=np.float64), GRID_W)
    col = np.tile(np.arange(GRID_W, dtype=np.float64), rows)
    a = MLA_ROPE // 4
    inv = ROPE_BASE ** (-np.arange(a, dtype=np.float64) / a)
    ang_r = row[:, None] * inv
    ang_c = col[:, None] * inv
    cos32 = np.concatenate([np.cos(ang_r), np.cos(ang_r), np.cos(ang_c), np.cos(ang_c)], axis=1)
    sin32 = np.concatenate([np.sin(ang_r), np.sin(ang_r), np.sin(ang_c), np.sin(ang_c)], axis=1)
    pad_r = HEAD_SLOT - MLA_NOPE - MLA_ROPE
    cos = np.concatenate([np.ones((seq, MLA_NOPE)), cos32, np.zeros((seq, pad_r))], axis=1)
    sin = np.concatenate([np.zeros((seq, MLA_NOPE)), sin32, np.zeros((seq, pad_r))], axis=1)
    cos_c = np.concatenate([np.ones((ctx_len, MLA_NOPE + MLA_ROPE)), np.zeros((ctx_len, pad_r))], axis=1)
    sin_c = np.zeros((ctx_len, HEAD_SLOT))
    return (jnp.asarray(np.concatenate([cos, cos_c], axis=0), F32),
            jnp.asarray(np.concatenate([sin, sin_c], axis=0), F32))


def _inproj(x_lat, x_ctx, ctx_blk, mod_l, g1, weights, q_norm, kv_norm, cos, sin, n_lat_tiles):
    bsz, _, d = x_lat.shape
    nt = n_lat_tiles + 1
    t = nt * TM
    wa, wg, wh, wz, wuq, wukv, wa2, ba = weights
    tile = lambda w: pl.BlockSpec((1, TM, w), lambda b, i: (b, i, 0))
    row = lambda w: pl.BlockSpec((1, w), lambda b, i: (0, 0))
    mod_spec = pl.BlockSpec((1, 6, d), lambda b, i: (jnp.where(i < n_lat_tiles, b, bsz), 0, 0))
    tab = pl.BlockSpec((TM, HEAD_SLOT), lambda b, i: (i, 0))
    nq = MLA_HEADS * HEAD_SLOT
    sds = lambda w, dt: jax.ShapeDtypeStruct((bsz, t, w), dt)
    out_shape = (sds(nq, BF16), sds(nq, BF16), sds(nq, BF16), sds(GLA_DK, BF16), sds(GLA_DK, BF16),
                 sds(GLA_DV, BF16), sds(GLA_DV, BF16), sds(2 * GLA_DK, F32), sds(3 * HY_WIDTH, F32),
                 sds(3 * D_MODEL, BF16))
    out_specs = (tile(nq), tile(nq), tile(nq), tile(GLA_DK), tile(GLA_DK), tile(GLA_DV), tile(GLA_DV),
                 tile(2 * GLA_DK), tile(3 * HY_WIDTH), tile(3 * D_MODEL))
    return pl.pallas_call(
        functools.partial(_inproj_kernel, n_lat_tiles=n_lat_tiles),
        out_shape=out_shape,
        grid=(bsz, nt),
        in_specs=_token_specs(d, n_lat_tiles, ctx_blk) + [
            mod_spec, row(d), _const_spec(wa.shape), _const_spec(wg.shape), _const_spec(wh.shape),
            _const_spec(wz.shape), row(MLA_Q_LORA), row(MLA_KV_LORA), _const_spec(wuq.shape),
            _const_spec(wukv.shape), _const_spec(wa2.shape), row(2 * GLA_DK), tab, tab],
        out_specs=out_specs,
        compiler_params=_cparams(2, VMEM_LIMIT),
        name="inproj",
    )(x_lat, x_ctx, mod_l, g1[None, :], wa, wg, wh, wz, q_norm[None, :], kv_norm[None, :], wuq, wukv, wa2, ba,
      cos, sin)


ATT_TK = 1024


ATT_TQ = 1024


def _attn_kernel(*refs, chunks, aliased):
    if aliased:
        q_ref, k_ref, v_ref, _, o_ref, m_ref, acc_ref = refs
    else:
        q_ref, k_ref, v_ref, o_ref, m_ref, acc_ref = refs
    tq = q_ref.shape[1]
    m_ref[...] = jnp.full(m_ref.shape, -jnp.inf, F32)
    acc_ref[...] = jnp.zeros(acc_ref.shape, F32)
    for r0, size in chunks:
        k = k_ref[0, pl.ds(r0, size), :]
        v = v_ref[0, pl.ds(r0, size), :]
        for hd in range(2):
            sl = slice(hd * HEAD_SLOT, (hd + 1) * HEAD_SLOT)
            s = _dot_nt(q_ref[0, :, sl], k[:, sl])
            m_prev = m_ref[hd]
            m_new = jnp.maximum(m_prev, jnp.max(s, axis=1, keepdims=True))
            p = jnp.exp2((s - jnp.concatenate([m_new] * (size // LANES), axis=1)).astype(BF16))
            acc_ref[hd] = jnp.exp2(m_prev - m_new) * acc_ref[hd] + _dot(p, v[:, sl])
            m_ref[hd] = m_new
    a0 = acc_ref[0]
    a1 = acc_ref[1]
    lane = lax.broadcasted_iota(jnp.int32, (tq, HEAD_SLOT), 1)
    o0 = a0 / pltpu.roll(a0, MLA_V, 1)
    o1 = pltpu.roll(a1, MLA_V, 1) / a1
    o_ref[0] = jnp.where(lane < MLA_V, o0, o1).astype(o_ref.dtype)


def _attention(q, k, v, seq, with_ctx_queries):
    bsz, t, _ = q.shape
    ctx_len = t - seq
    assert seq % ATT_TK == 0 and seq % ATT_TQ == 0 and seq % ctx_len == 0 and 2 * MLA_V == HEAD_SLOT
    pair = 2 * HEAD_SLOT
    n_chunks = seq // ATT_TK
    chunks = tuple((j * ATT_TK, ATT_TK) for j in range(n_chunks - 1))
    chunks += (((n_chunks - 1) * ATT_TK, ATT_TK + ctx_len),)
    scratch = lambda tq: [pltpu.VMEM((2, tq, LANES), F32), pltpu.VMEM((2, tq, HEAD_SLOT), F32)]
    y = pl.pallas_call(
        functools.partial(_attn_kernel, chunks=chunks, aliased=False),
        out_shape=jax.ShapeDtypeStruct((bsz, t, MLA_OUT), BF16),
        grid=(bsz, MLA_HEADS // 2, seq // ATT_TQ),
        in_specs=[
            pl.BlockSpec((1, ATT_TQ, pair), lambda b, hp, i: (b, i, hp)),
            pl.BlockSpec((1, t, pair), lambda b, hp, i: (b, 0, hp)),
            pl.BlockSpec((1, t, pair), lambda b, hp, i: (b, 0, hp)),
        ],
        out_specs=pl.BlockSpec((1, ATT_TQ, HEAD_SLOT), lambda b, hp, i: (b, i, hp)),
        scratch_shapes=scratch(ATT_TQ),
        compiler_params=_cparams(3, VMEM_LIMIT),
        name="mla_attention",
    )(q, k, v)
    if not with_ctx_queries:
        return y
    cblk = seq // ctx_len
    ctx_rows = lambda w: pl.BlockSpec((1, ctx_len, w), lambda b, hp: (b, cblk, hp))
    return pl.pallas_call(
        functools.partial(_attn_kernel, chunks=((0, ctx_len),), aliased=True),
        out_shape=jax.ShapeDtypeStruct((bsz, t, MLA_OUT), BF16),
        grid=(bsz, MLA_HEADS // 2),
        in_specs=[ctx_rows(pair), ctx_rows(pair), ctx_rows(pair), pl.BlockSpec(memory_space=pl.ANY)],
        out_specs=ctx_rows(HEAD_SLOT),
        scratch_shapes=scratch(ctx_len),
        input_output_aliases={3: 0},
        compiler_params=_cparams(2),
        name="mla_attention_ctx",
    )(q, k, v, y)


GLA_LEVELS = int(math.log2(TM))
GLA_SAFE_SPAN = 60.0


def _gla_level_matrices():
    i = np.arange(TM)[:, None]
    t = np.arange(TM)[None, :]
    fwd = [(t <= i)]
    for lv in range(GLA_LEVELS):
        m = TM >> (lv + 1)
        lo = (i // m) * m
        later = ((i // m) % 2) == 1
        q_part = later & (t >= lo) & (t <= i)
        k_part = (~later) & (t > i) & (t <= lo + m - 1)
        fwd.append(q_part | k_part)
    fwd = np.concatenate(fwd, axis=0).astype(np.float32)
    nb = 1 + GLA_LEVELS
    bwd = fwd.reshape(nb, TM, TM)[:, ::-1, ::-1].reshape(nb * TM, TM)
    return np.stack([fwd, bwd])


def _gla_kernel(q_ref, k_ref, v_ref, g_ref, mall_ref, o_ref, s_ref, a_ref):
    d = pl.program_id(1)
    step = pl.program_id(2)

    @pl.when(step == 0)
    def _():
        s_ref[...] = jnp.zeros(s_ref.shape, F32)

    q = q_ref[0].astype(F32)
    k = k_ref[0].astype(F32)
    v = v_ref[0]
    g = g_ref[0]
    g2 = jnp.concatenate([g.astype(BF16), (g - g.astype(BF16).astype(F32)).astype(BF16)], axis=1)
    e2 = _dot(mall_ref[0, 0:TM, :], g2)
    g_cum = e2[:, :GLA_DK] + e2[:, GLA_DK:]
    g_tot = jnp.sum(g, axis=0, keepdims=True)

    row = lax.broadcasted_iota(jnp.int32, (TM, TM), 0)
    col = lax.broadcasted_iota(jnp.int32, (TM, TM), 1)
    lane_head = lax.broadcasted_iota(jnp.int32, (TM, GLA_DK), 1) // GLA_HK
    tok = lax.broadcasted_iota(jnp.int32, (TM, GLA_DK), 0)
    eye = row == col

    def stack_heads(t):
        return jnp.concatenate([jnp.where(lane_head == hd, t, 0.0) for hd in range(GLA_HEADS)], axis=0).astype(BF16)

    span = jnp.max(-g_tot)

    @pl.when(span < GLA_SAFE_SPAN)
    def _():
        res = _dot_nt(stack_heads(q * jnp.exp(g_cum)), (k * jnp.exp(-g_cum)).astype(BF16))
        seen = (col - row) * (1 - 2 * d) <= 0
        for hd in range(GLA_HEADS):
            a_ref[hd] = jnp.where(seen, res[hd * TM:(hd + 1) * TM], 0.0)

    @pl.when(span >= GLA_SAFE_SPAN)
    def _():
        e2l = _dot(mall_ref[0, TM:, :], g2)
        e_lv = e2l[:, :GLA_DK] + e2l[:, GLA_DK:]
        res = _dot_nt(stack_heads(q), k.astype(BF16))
        for hd in range(GLA_HEADS):
            a_ref[hd] = jnp.where(eye, res[hd * TM:(hd + 1) * TM], 0.0)
        for lv in range(GLA_LEVELS):
            m = TM >> (lv + 1)
            w = jnp.exp(e_lv[lv * TM:(lv + 1) * TM])
            bit = (tok // m) % 2
            q_act = bit != d
            qt = jnp.where(q_act, q * w, 0.0)
            kt = jnp.where(q_act, 0.0, k * w).astype(BF16)
            res = _dot_nt(stack_heads(qt), kt)
            if m == TM // 2:
                for hd in range(GLA_HEADS):
                    a_ref[hd] += res[hd * TM:(hd + 1) * TM]
            else:
                same = (row // (2 * m)) == (col // (2 * m))
                for hd in range(GLA_HEADS):
                    a_ref[hd] += jnp.where(same, res[hd * TM:(hd + 1) * TM], 0.0)

    s_old = s_ref[...]
    o_inter = _dot((q * jnp.exp(g_cum)).astype(BF16), s_old.astype(BF16))
    for hd in range(GLA_HEADS):
        sl = slice(hd * GLA_HV, (hd + 1) * GLA_HV)
        o_intra = _dot(a_ref[hd].astype(BF16), v[:, sl])
        o_ref[0, 0, :, sl] = (o_intra + o_inter[:, sl]).astype(o_ref.dtype)

    kdec_t = (k * jnp.exp(g_tot - g_cum)).T.astype(BF16)
    upd = _dot(kdec_t, v)
    same_head = (lax.broadcasted_iota(jnp.int32, (GLA_DK, GLA_DV), 0) // GLA_HK
                 == lax.broadcasted_iota(jnp.int32, (GLA_DK, GLA_DV), 1) // GLA_HV)
    dec_col = jnp.sum(jnp.where(eye, jnp.broadcast_to(jnp.exp(g_tot), (TM, GLA_DK)), 0.0), axis=1, keepdims=True)
    s_ref[...] = dec_col * s_old + jnp.where(same_head, upd, 0.0)


def _gla(gq, gk, gv, glog, n_lat_tiles):
    assert GLA_DK == TM
    bsz, t, _ = gq.shape
    nt = t // TM
    mall = jnp.asarray(_gla_level_matrices(), dtype=BF16)

    def tile_idx(d, s):
        return jnp.where(s == 0, n_lat_tiles, jnp.where(d == 0, s - 1, n_lat_tiles - s))

    tok = lambda w: pl.BlockSpec((1, TM, w), lambda b, d, s: (b, tile_idx(d, s), 0))
    return pl.pallas_call(
        _gla_kernel,
        out_shape=jax.ShapeDtypeStruct((bsz, 2, t, GLA_DV), BF16),
        grid=(bsz, 2, nt),
        in_specs=[tok(GLA_DK), tok(GLA_DK), tok(GLA_DV),
                  pl.BlockSpec((1, TM, GLA_DK), lambda b, d, s: (b, tile_idx(d, s), d)),
                  pl.BlockSpec((1,) + mall.shape[1:], lambda b, d, s: (d, 0, 0))],
        out_specs=pl.BlockSpec((1, 1, TM, GLA_DV), lambda b, d, s: (b, d, tile_idx(d, s), 0)),
        scratch_shapes=[pltpu.VMEM((GLA_DK, GLA_DV), F32), pltpu.VMEM((GLA_HEADS, TM, TM), F32)],
        compiler_params=_cparams(3),
        name="gla_scan",
    )(gq, gk, gv, glog, mall)


HY_N1 = 64
HY_N2 = 128
HY_SLAB = SUBLANES
HY_CB = 256
HY_KG = 16
HY_SPEC_CB = 128
HY_UNROLL = 8


def _hyena_dft_constants(seq):
    n = 2 * seq
    assert n == HY_N1 * HY_N2
    nh = HY_N1 // 2
    kh = HY_N2 // 2
    eye = np.eye(HY_SLAB)
    k1 = np.arange(HY_N1)
    th = 2 * np.pi * np.outer(k1 + 0.5, np.arange(nh)) / HY_N1
    fwd1 = np.concatenate([np.kron(np.cos(th), eye), np.kron(-np.sin(th), eye)], axis=0)
    inv1 = (2.0 / n) * np.concatenate([np.kron(np.cos(th).T, eye), np.kron(-np.sin(th).T, eye)], axis=1)
    nlo = np.arange(HY_N2)
    k2 = np.arange(kh)
    ph = 2 * np.pi * (k2[None, :, None] * nlo[None, None, :] / HY_N2
                      + (k1[:, None, None] + 0.5) * nlo[None, None, :] / n)
    c, s = np.cos(ph), np.sin(ph)
    fwd2 = np.concatenate([np.concatenate([c, s], axis=2), np.concatenate([-s, c], axis=2)], axis=1)
    ct, st = c.transpose(0, 2, 1), s.transpose(0, 2, 1)
    inv2 = np.concatenate([np.concatenate([ct, -st], axis=2), np.concatenate([st, ct], axis=2)], axis=1)
    return fwd1, fwd2, inv2, inv1


def _hyena_ctx_dft_constants(ctx_len):
    n = 2 * ctx_len
    th = 2 * np.pi * np.outer(np.arange(ctx_len) + 0.5, np.arange(ctx_len)) / n
    fwd = np.concatenate([np.cos(th), -np.sin(th)], axis=0)
    inv = (2.0 / n) * np.concatenate([np.cos(th).T, -np.sin(th).T], axis=1)
    return fwd, inv


def _hyena_features(length):
    pos = np.arange(length, dtype=np.float64)
    t = pos / max(length - 1, 1)
    f = np.linspace(1e-4, HY_BANDS - 1, HY_BANDS)
    ang = (2.0 * math.pi / length) * pos[:, None] * f
    feat = np.concatenate([t[:, None], np.cos(ang), np.sin(ang)], axis=-1)
    return jnp.asarray(np.pad(feat, ((0, 0), (0, LANES - HY_POS_DIM))), F32)


def _hy_filter_kernel(feat_ref, w1_ref, b1_ref, w2_ref, b2_ref, w3_ref, b3_ref, absd_ref, h_ref, s_ref):
    i = pl.program_id(0)
    feat = feat_ref[...]
    hp = lambda a, b: jnp.dot(a, b, preferred_element_type=F32, precision=HIGHEST)
    hdn = jnp.sin(hp(feat, w1_ref[...]) + b1_ref[...])
    hdn = jnp.sin(hp(hdn, w2_ref[...]) + b2_ref[...])
    h = hp(hdn, w3_ref[...]) + b3_ref[...]
    window = jnp.exp(-feat[:, 0:1] * absd_ref[...])
    h = h * jnp.concatenate([window] * (2 * HY_ORDER), axis=1)
    h_ref[...] = h

    @pl.when(i == 0)
    def _():
        s_ref[...] = jnp.zeros(s_ref.shape, F32)

    s_ref[...] += jnp.sum(jnp.abs(h), axis=0, keepdims=True)


def _hyena_filters_raw(length, filt_w):
    w1, b1, w2, b2, w3, b3 = filt_w
    nf = 2 * HY_ORDER * HY_WIDTH
    tr = min(length, 512)
    deltas = np.linspace(math.log(HY_DECAY_TARGET) / HY_FAST_DECAY, math.log(HY_DECAY_TARGET) / HY_SLOW_DECAY,
                         HY_WIDTH, dtype=np.float32)
    absd = jnp.asarray(np.abs(deltas))[None, :]
    w1p = jnp.pad(w1, ((0, LANES - HY_POS_DIM), (0, 0)))
    full = lambda shp: pl.BlockSpec(shp, lambda i: (0,) * len(shp))
    return pl.pallas_call(
        _hy_filter_kernel,
        out_shape=(jax.ShapeDtypeStruct((length, nf), F32), jax.ShapeDtypeStruct((1, nf), F32)),
        grid=(length // tr,),
        in_specs=[pl.BlockSpec((tr, LANES), lambda i: (i, 0)), full((LANES, HY_FILTER_HIDDEN)),
                  full((1, HY_FILTER_HIDDEN)), full((HY_FILTER_HIDDEN, HY_FILTER_HIDDEN)), full((1, HY_FILTER_HIDDEN)),
                  full((HY_FILTER_HIDDEN, nf)), full((1, nf)), full((1, HY_WIDTH))],
        out_specs=(pl.BlockSpec((tr, nf), lambda i: (i, 0)), full((1, nf))),
        compiler_params=_cparams(1),
        name="hyena_filter_mlp",
    )(_hyena_features(length), w1p, b1[None, :], w2, b2[None, :], w3, b3[None, :], absd)


def _dot_split(m, x):
    x_hi = x.astype(BF16)
    x_lo = (x - x_hi.astype(F32)).astype(BF16)
    n = x.shape[1]
    r = _dot(m, jnp.concatenate([x_hi, x_lo], axis=1))
    return r[:, :n] + r[:, n:]


def _odft_stage1(src_at, mm, s_re, s_im, unroll=2):
    nk = s_re.shape[0]
    half = nk * HY_SLAB

    def body(j, carry):
        r0 = pl.multiple_of(j * HY_SLAB, HY_SLAB)
        slab = src_at(r0)
        cb = slab.shape[-1]
        res = mm(slab.reshape(-1, cb))
        s_re[:, pl.ds(r0, HY_SLAB), :] = res[:half].reshape(nk, HY_SLAB, cb)
        s_im[:, pl.ds(r0, HY_SLAB), :] = res[half:].reshape(nk, HY_SLAB, cb)
        return carry

    lax.fori_loop(0, HY_N2 // HY_SLAB, body, 0, unroll=unroll)


def _hy_spectrum_kernel(hf_ref, hb_ref, sf_ref, sb_ref, fwd1_ref, fwd2_ref, o_ref, s_re, s_im):
    kh = HY_N2 // 2
    mm1 = lambda x: _dot_split(fwd1_ref[...], x)

    def middle(sign):
        def body(k1, carry):
            a = jnp.concatenate([s_re[k1], s_im[k1]], axis=0)
            x = _dot_split(fwd2_ref[k1], a)
            if sign is None:
                o_ref[0, k1] = x[:kh]
                o_ref[1, k1] = x[kh:]
            else:
                inv_norm = 1.0 / (sf_ref[...] + sb_ref[...])
                o_ref[0, k1] = (o_ref[0, k1] + x[:kh]) * inv_norm
                o_ref[1, k1] = (o_ref[1, k1] - x[kh:]) * inv_norm
            return carry
        lax.fori_loop(0, HY_N1, body, 0, unroll=8)

    _odft_stage1(lambda r0: hf_ref[:, pl.ds(r0, HY_SLAB), :], mm1, s_re, s_im, unroll=4)
    middle(None)

    def bwd_slab(r0):
        slab = hb_ref[:, pl.ds(r0, HY_SLAB), :]
        nhi = lax.broadcasted_iota(jnp.int32, slab.shape, 0)
        r = lax.broadcasted_iota(jnp.int32, slab.shape, 1)
        return jnp.where((nhi == 0) & (r + r0 == 0), 0.0, slab)

    _odft_stage1(bwd_slab, mm1, s_re, s_im, unroll=4)
    middle(-1)


def _hyena_filter_spectrum(seq, w1, b1, w2, b2, w3, b3):
    h_raw, s = _hyena_filters_raw(seq, (w1, b1, w2, b2, w3, b3))
    nh = HY_N1 // 2
    nc = HY_ORDER * HY_WIDTH
    h3 = h_raw.reshape(nh, HY_N2, 2 * nc)
    fwd1, fwd2, _, _ = _hyena_dft_constants(seq)
    fwd1 = jnp.asarray(fwd1, BF16)
    fwd2 = jnp.asarray(fwd2, BF16)
    scb = HY_SPEC_CB
    ncb = nc // scb
    return pl.pallas_call(
        _hy_spectrum_kernel,
        out_shape=jax.ShapeDtypeStruct((2, HY_N1, HY_N2 // 2, nc), F32),
        grid=(ncb,),
        in_specs=[pl.BlockSpec((nh, HY_N2, scb), lambda c: (0, 0, c)),
                  pl.BlockSpec((nh, HY_N2, scb), lambda c: (0, 0, ncb + c)),
                  pl.BlockSpec((1, scb), lambda c: (0, c)),
                  pl.BlockSpec((1, scb), lambda c: (0, ncb + c)),
                  _const_spec(fwd1.shape), _const_spec(fwd2.shape)],
        out_specs=pl.BlockSpec((2, HY_N1, HY_N2 // 2, scb), lambda c: (0, 0, 0, c)),
        scratch_shapes=[pltpu.VMEM((HY_N1, HY_N2, scb), F32), pltpu.VMEM((HY_N1, HY_N2, scb), F32)],
        compiler_params=_cparams(1, VMEM_LIMIT),
        name="hyena_filter_spectrum",
    )(h3, h3, s, s, fwd1, fwd2)


def _short_conv_chunk(ref, c, n_chunks, w_ref, b_ref):
    per = TM // HY_N2
    cur = ref[0, pl.ds(per * c, per)]
    cb = cur.shape[-1]
    cur = cur.reshape(TM, cb)
    prev = ref[0, jnp.maximum(per * c - 1, 0), pl.ds(HY_N2 - SUBLANES, SUBLANES), :][SUBLANES - 1:SUBLANES]
    nxt = ref[0, jnp.minimum(per * c + per, per * n_chunks - 1), pl.ds(0, SUBLANES), :][0:1]
    prev = jnp.where(c > 0, prev, 0.0)
    nxt = jnp.where(c < n_chunks - 1, nxt, 0.0)
    rowi = lax.broadcasted_iota(jnp.int32, (TM, cb), 0)
    dn = jnp.where(rowi == 0, prev, pltpu.roll(cur, 1, 0))
    up = jnp.where(rowi == TM - 1, nxt, pltpu.roll(cur, TM - 1, 0))
    return b_ref[...] + w_ref[0:1, :] * dn + w_ref[1:2, :] * cur + w_ref[2:3, :] * up


def _hy_conv_kernel(*refs, conv_y):
    if conv_y:
        (y_ref, g_ref, h_ref, bias_ref, wy_ref, by_ref, wg_ref, bg_ref, fwd1_ref, fwd2_ref, inv2_ref, inv1_ref,
         o_ref, s_re, s_im, gs_ref, us_ref) = refs
    else:
        (y_ref, g_ref, h_ref, bias_ref, wg_ref, bg_ref, fwd1_ref, fwd2_ref, inv2_ref, inv1_ref,
         o_ref, s_re, s_im, gs_ref) = refs
    grp = pl.program_id(2)
    last = pl.num_programs(2) - 1
    nh = HY_N1 // 2
    per = TM // HY_N2
    n_chunks = nh // per
    cb = o_ref.shape[-1]
    kh = HY_N2 // 2

    @pl.when(grp == 0)
    def _():
        def pre(c, carry):
            gs_ref[pl.ds(per * c, per)] = _short_conv_chunk(g_ref, c, n_chunks, wg_ref, bg_ref).reshape(per, HY_N2, cb)
            if conv_y:
                us_ref[pl.ds(per * c, per)] = _short_conv_chunk(y_ref, c, n_chunks, wy_ref, by_ref).reshape(
                    per, HY_N2, cb)
            return carry
        lax.fori_loop(0, n_chunks, pre, 0)

    if conv_y:
        u_at = lambda r0: us_ref[:, pl.ds(r0, HY_SLAB), :]
    else:
        u_at = lambda r0: y_ref[0, :, pl.ds(r0, HY_SLAB), :]

    _odft_stage1(u_at, lambda x: _dot(fwd1_ref[0], x.astype(BF16)), s_re, s_im, unroll=HY_UNROLL)

    def middle(k1, carry):
        a = jnp.concatenate([s_re[k1], s_im[k1]], axis=0).astype(BF16)
        x = _dot(fwd2_ref[k1], a)
        xr, xi = x[:kh], x[kh:]
        hr, hi = h_ref[0, k1], h_ref[1, k1]
        y = jnp.concatenate([xr * hr - xi * hi, xr * hi + xi * hr], axis=0).astype(BF16)
        bm = _dot(inv2_ref[k1], y)
        s_re[k1] = bm[:HY_N2]
        s_im[k1] = bm[HY_N2:]
        return carry

    lax.fori_loop(0, HY_KG, middle, 0, unroll=2 * HY_UNROLL)

    def partial_conv(r0):
        slab = jnp.concatenate([s_re[:, pl.ds(r0, HY_SLAB), :].reshape(HY_KG * HY_SLAB, cb),
                                s_im[:, pl.ds(r0, HY_SLAB), :].reshape(HY_KG * HY_SLAB, cb)], axis=0).astype(BF16)
        return _dot(inv1_ref[0], slab).reshape(nh, HY_SLAB, cb)

    def post_loop(fn):
        def post(j, carry):
            r0 = pl.multiple_of(j * HY_SLAB, HY_SLAB)
            o_ref[0, :, pl.ds(r0, HY_SLAB), :] = fn(r0, partial_conv(r0))
            return carry
        lax.fori_loop(0, HY_N2 // HY_SLAB, post, 0, unroll=HY_UNROLL)

    @pl.when(grp == 0)
    def _():
        post_loop(lambda r0, part: part)

    @pl.when((grp > 0) & (grp < last))
    def _():
        post_loop(lambda r0, part: o_ref[0, :, pl.ds(r0, HY_SLAB), :] + part)

    @pl.when(grp == last)
    def _():
        post_loop(lambda r0, part: gs_ref[:, pl.ds(r0, HY_SLAB), :]
                  * (o_ref[0, :, pl.ds(r0, HY_SLAB), :] + part + bias_ref[...] * u_at(r0)))


def _hyena_order(y4, y_col0, z4, gate_col0, hspec, order, hy_bias, short_w, short_b, consts, conv_y, rows_out):
    bsz = z4.shape[0]
    nh = HY_N1 // 2
    ncb = HY_WIDTH // HY_CB
    ngrp = HY_N1 // HY_KG
    fwd1, fwd2, inv2, inv1 = consts
    blk4 = lambda off: pl.BlockSpec((1, nh, HY_N2, HY_CB), lambda c, b, g: (b, 0, 0, off + c))
    rowspec = lambda rows, off: pl.BlockSpec((rows, HY_CB), lambda c, b, g: (0, off + c))
    grouped = lambda shp: pl.BlockSpec(shp, lambda c, b, g: (g, 0, 0))
    in_specs = [blk4(y_col0), blk4(gate_col0),
                pl.BlockSpec((2, HY_KG, HY_N2 // 2, HY_CB), lambda c, b, g: (0, g, 0, order * ncb + c)),
                rowspec(1, 0)]
    args = [y4, z4, hspec, hy_bias.reshape(1, -1)]
    if conv_y:
        in_specs += [rowspec(HY_SHORT, y_col0), rowspec(1, y_col0)]
        args += [short_w, short_b[None, :]]
    in_specs += [rowspec(HY_SHORT, gate_col0), rowspec(1, gate_col0)]
    args += [short_w, short_b[None, :]]
    in_specs += [grouped((1,) + fwd1.shape[1:]), grouped((HY_KG,) + fwd2.shape[1:]),
                 grouped((HY_KG,) + inv2.shape[1:]), grouped((1,) + inv1.shape[1:])]
    args += [fwd1, fwd2, inv2, inv1]
    scratch = [pltpu.VMEM((HY_KG, HY_N2, HY_CB), F32), pltpu.VMEM((HY_KG, HY_N2, HY_CB), F32),
               pltpu.VMEM((nh, HY_N2, HY_CB), F32)]
    if conv_y:
        scratch.append(pltpu.VMEM((nh, HY_N2, HY_CB), F32))
    return pl.pallas_call(
        functools.partial(_hy_conv_kernel, conv_y=conv_y),
        out_shape=jax.ShapeDtypeStruct((bsz, rows_out, HY_N2, HY_WIDTH), F32),
        grid=(ncb, bsz, ngrp),
        in_specs=in_specs,
        out_specs=pl.BlockSpec((1, nh, HY_N2, HY_CB), lambda c, b, g: (b, 0, 0, c)),
        scratch_shapes=scratch,
        compiler_params=_cparams(3, VMEM_LIMIT),
        name="hyena_conv%d" % order,
    )(*args)


def _hyena_latent(z_hy, hspec, short_w, short_b, hy_bias, with_ctx_rows):
    bsz, t, _ = z_hy.shape
    seq = HY_N1 * HY_N2 // 2
    fwd1, fwd2, inv2, inv1 = _hyena_dft_constants(seq)
    ngrp = HY_N1 // HY_KG
    rows = HY_KG * HY_SLAB
    fwd1 = fwd1.reshape(2, ngrp, rows, -1).transpose(1, 0, 2, 3).reshape(ngrp, 2 * rows, -1)
    inv1 = inv1.reshape(-1, 2, ngrp, rows).transpose(2, 0, 1, 3).reshape(ngrp, -1, 2 * rows)
    consts = tuple(jnp.asarray(m, BF16) for m in (fwd1, fwd2, inv2, inv1))
    z4 = z_hy.reshape(bsz, t // HY_N2, HY_N2, 3 * HY_WIDTH)
    ncb = HY_WIDTH // HY_CB
    nh = seq // HY_N2
    y1 = _hyena_order(z4, 2 * ncb, z4, 0, hspec, 0, hy_bias[0], short_w, short_b, consts, True, nh)
    rows_out = t // HY_N2 if with_ctx_rows else nh
    y2 = _hyena_order(y1, 0, z4, ncb, hspec, 1, hy_bias[1], short_w, short_b, consts, False, rows_out)
    return y2.reshape(bsz, rows_out * HY_N2, HY_WIDTH)


def _hy_ctx_spectrum_kernel(h_ref, s_ref, fwd_ref, o_ref):
    lc = h_ref.shape[0]
    nc = HY_ORDER * HY_WIDTH
    hp = lambda a, b: jnp.dot(a, b, preferred_element_type=F32, precision=HIGHEST)
    h = h_ref[...]
    rowi = lax.broadcasted_iota(jnp.int32, (lc, nc), 0)
    xf = hp(fwd_ref[...], h[:, :nc])
    xb = hp(fwd_ref[...], jnp.where(rowi == 0, 0.0, h[:, nc:]))
    inv_norm = 1.0 / (s_ref[:, :nc] + s_ref[:, nc:])
    o_ref[0] = (xf[:lc] + xb[:lc]) * inv_norm
    o_ref[1] = (xf[lc:] - xb[lc:]) * inv_norm


def _hy_ctx_conv_kernel(x1_ref, x2_ref, v_ref, h_ref, bias_ref, w_ref, b_ref, fwd_ref, inv_ref, _, o_ref):
    lc = o_ref.shape[1]
    rowi = lax.broadcasted_iota(jnp.int32, (lc, HY_WIDTH), 0)

    def short(ref, part):
        cur = ref[0].reshape(lc, HY_WIDTH)
        sl = slice(part * HY_WIDTH, (part + 1) * HY_WIDTH)
        dn = jnp.where(rowi == 0, 0.0, pltpu.roll(cur, 1, 0))
        up = jnp.where(rowi == lc - 1, 0.0, pltpu.roll(cur, lc - 1, 0))
        return b_ref[:, sl] + w_ref[0:1, sl] * dn + w_ref[1:2, sl] * cur + w_ref[2:3, sl] * up

    y = short(v_ref, 2)
    for order, gref in enumerate((x1_ref, x2_ref)):
        sl = slice(order * HY_WIDTH, (order + 1) * HY_WIDTH)
        x = _dot(fwd_ref[...], y.astype(BF16))
        xr, xi = x[:lc], x[lc:]
        hr, hi = h_ref[0, :, sl], h_ref[1, :, sl]
        prod = jnp.concatenate([xr * hr - xi * hi, xr * hi + xi * hr], axis=0).astype(BF16)
        conv = _dot(inv_ref[...], prod)
        y = short(gref, order) * (conv + bias_ref[order:order + 1, :] * y)
    o_ref[0] = y


def _hyena_ctx(z_hy, y_lat, filt_w, short_w, short_b, hy_bias):
    bsz, t, _ = z_hy.shape
    seq = HY_N1 * HY_N2 // 2
    lc = t - seq
    per = lc // HY_N2
    h_raw, s = _hyena_filters_raw(lc, filt_w)
    fwd, inv = _hyena_ctx_dft_constants(lc)
    nc = HY_ORDER * HY_WIDTH
    full = lambda shp: pl.BlockSpec(shp, lambda *_: (0,) * len(shp))
    hspec = pl.pallas_call(
        _hy_ctx_spectrum_kernel,
        out_shape=jax.ShapeDtypeStruct((2, lc, nc), F32),
        grid=(1,),
        in_specs=[full(h_raw.shape), full(s.shape), full(fwd.shape)],
        out_specs=full((2, lc, nc)),
        compiler_params=_cparams(1),
        name="hyena_ctx_spectrum",
    )(h_raw, s, jnp.asarray(fwd, F32))
    z4 = z_hy.reshape(bsz, t // HY_N2, HY_N2, 3 * HY_WIDTH)
    blk = lambda part: pl.BlockSpec((1, per, HY_N2, HY_WIDTH), lambda b: (b, seq // lc, 0, part))
    return pl.pallas_call(
        _hy_ctx_conv_kernel,
        out_shape=jax.ShapeDtypeStruct((bsz, t, HY_WIDTH), F32),
        grid=(bsz,),
        in_specs=[blk(0), blk(1), blk(2), full((2, lc, nc)), full((HY_ORDER, HY_WIDTH)),
                  full((HY_SHORT, 3 * HY_WIDTH)), full((1, 3 * HY_WIDTH)), full(fwd.shape), full(inv.shape),
                  pl.BlockSpec(memory_space=pl.ANY)],
        out_specs=pl.BlockSpec((1, lc, HY_WIDTH), lambda b: (b, seq // lc, 0)),
        input_output_aliases={9: 0},
        compiler_params=_cparams(1),
        name="hyena_ctx_conv",
    )(z4, z4, z4, hspec, hy_bias, short_w, short_b[None, :], jnp.asarray(fwd, BF16), jnp.asarray(inv, BF16), y_lat)


FF_CHUNK = 1024


def _merge_mlp_kernel(x_ref, c_ref, mod_ref, ymla_ref, of_ref, ob_ref, gr_ref, yhy_ref, zg_ref, on_ref,
                      wm_ref, wgl_ref, wh_ref, wo_ref, g2_ref, w1_ref, w2_ref, fg_ref, o_ref, *, final, n_lat_tiles):
    o = of_ref[0, 0].astype(F32) + ob_ref[0, 0].astype(F32)
    silu = gr_ref[0].astype(F32)
    parts = []
    for hd in range(GLA_HEADS):
        sl = slice(hd * GLA_HV, (hd + 1) * GLA_HV)
        parts.append((_rms(o[:, sl]) * on_ref[...] * silu[:, sl]).astype(BF16))
    y_gla = jnp.concatenate(parts, axis=1)
    zg = zg_ref[0].astype(F32)
    d = x_ref.shape[-1]
    m = zg[:, 0:d] * _dot(ymla_ref[0], wm_ref[...])
    m = m + zg[:, d:2 * d] * _dot(y_gla, wgl_ref[...])
    m = m + zg[:, 2 * d:3 * d] * _dot(yhy_ref[0].astype(BF16), wh_ref[...])
    out = _dot(m.astype(BF16), wo_ref[...])
    x = _token_tile(x_ref, c_ref, n_lat_tiles) + mod_ref[0, 2:3, :] * out

    h = (_rms(x) * g2_ref[...] * (1.0 + mod_ref[0, 4:5, :]) + mod_ref[0, 3:4, :]).astype(BF16)
    acc = jnp.zeros(x.shape, F32)
    for j in range(w1_ref.shape[1] // FF_CHUNK):
        a = jnp.maximum(_dot(h, w1_ref[:, j * FF_CHUNK:(j + 1) * FF_CHUNK]), 0.0)
        acc = acc + _dot((a * a).astype(BF16), w2_ref[j * FF_CHUNK:(j + 1) * FF_CHUNK, :])
    xn = x + mod_ref[0, 5:6, :] * acc
    if final:
        xn = _rms(xn) * fg_ref[...]
    o_ref[0] = xn


def _mod_spec(d, n_lat_tiles, bsz):
    return pl.BlockSpec((1, 6, d), lambda b, i: (jnp.where(i < n_lat_tiles, b, bsz), 0, 0))


def _merge_mlp(x_lat, x_ctx, ctx_blk, mod_l, y_mla, o_gla, gr, y_hy, gate, out_norm, w_o_mla, w_o_gla, w_o_hy, w_out,
               g2, w1, w2, final_g, n_tiles, n_lat_tiles, final):
    bsz, _, d = x_lat.shape
    tile = lambda w: pl.BlockSpec((1, TM, w), lambda b, i: (b, i, 0))
    row = lambda w: pl.BlockSpec((1, w), lambda b, i: (0, 0))
    dirspec = lambda dr: pl.BlockSpec((1, 1, TM, GLA_DV), lambda b, i: (b, dr, i, 0))
    bf = lambda w: w.astype(BF16)
    return pl.pallas_call(
        functools.partial(_merge_mlp_kernel, final=final, n_lat_tiles=n_lat_tiles),
        out_shape=jax.ShapeDtypeStruct((bsz, n_tiles * TM, d), F32),
        grid=(bsz, n_tiles),
        in_specs=_token_specs(d, n_lat_tiles, ctx_blk) + [
            _mod_spec(d, n_lat_tiles, bsz), tile(MLA_OUT), dirspec(0), dirspec(1), tile(GLA_DV),
            tile(HY_WIDTH), tile(3 * d), row(GLA_HV),
            _const_spec(w_o_mla.shape), _const_spec(w_o_gla.shape), _const_spec(w_o_hy.shape),
            _const_spec(w_out.shape), row(d), _const_spec(w1.shape), _const_spec(w2.shape), row(d)],
        out_specs=tile(d),
        compiler_params=_cparams(2, VMEM_LIMIT),
        name="merge_mlp",
    )(x_lat, x_ctx, mod_l, y_mla, o_gla, o_gla, gr, y_hy, gate, out_norm[None, :], bf(w_o_mla), bf(w_o_gla),
      bf(w_o_hy), bf(w_out), g2[None, :], bf(w1), bf(w2), final_g[None, :])


def kernel(x, c, ctx, c_ctx, ada_w, ada_b, norm1_g, norm2_g, w_in, mla_q_norm, mla_w_uq, mla_kv_norm, mla_w_ukv, gla_w_a2, gla_b_a, gla_out_norm, hy_short_w, hy_short_b, hy_f_w1, hy_f_b1, hy_f_w2, hy_f_b2, hy_f_w3, hy_f_b3, hy_bias, w_o_mla, w_o_gla, w_o_hy, w_out, ff_w1, ff_w2, final_norm_g):
    bsz, seq, d = x.shape
    ctx_len = ctx.shape[1]
    n_lat = seq // TM
    assert ctx_len == TM and seq % TM == 0
    n_all = n_lat + 1
    x_lat, x_ctx, ctx_blk = x, ctx, 0
    cc = jnp.zeros((16, d), F32).at[:bsz].set(c).at[bsz].set(c_ctx)
    mod = _modulation(cc, ada_w, ada_b).reshape(DEPTH, 16, 6, d)
    cos, sin = _rope_tables(seq, ctx_len)
    for l in range(DEPTH):
        last = l == DEPTH - 1
        n_tiles = n_lat if last else n_all
        weights = _prep_inproj_weights(w_in[l], mla_w_uq[l], mla_w_ukv[l], gla_w_a2[l], gla_b_a[l])
        q, k, v, gq, gk, gv, gr, glog, z_hy, z_gate = _inproj(x_lat, x_ctx, ctx_blk, mod[l], norm1_g[l], weights,
                                                             mla_q_norm[l], mla_kv_norm[l], cos, sin, n_lat)
        y_mla = _attention(q, k, v, seq, not last)
        o_gla = _gla(gq, gk, gv, glog, n_lat)
        filt_w = (hy_f_w1[l], hy_f_b1[l], hy_f_w2[l], hy_f_b2[l], hy_f_w3[l], hy_f_b3[l])
        hspec = _hyena_filter_spectrum(seq, *filt_w)
        y_hy = _hyena_latent(z_hy, hspec, hy_short_w[l], hy_short_b[l], hy_bias[l], not last)
        if not last:
            y_hy = _hyena_ctx(z_hy, y_hy, filt_w, hy_short_w[l], hy_short_b[l], hy_bias[l])
        xc = _merge_mlp(x_lat, x_ctx, ctx_blk, mod[l], y_mla, o_gla, gr, y_hy, z_gate, gla_out_norm[l], w_o_mla[l],
                        w_o_gla[l], w_o_hy[l], w_out[l], norm2_g[l], ff_w1[l], ff_w2[l], final_norm_g, n_tiles,
                        n_lat, last)
        x_lat, x_ctx, ctx_blk = xc, xc, n_lat
    return xc
```

```python
import functools
import math

import numpy as np
import jax
import jax.numpy as jnp
from jax import lax
from jax.experimental import pallas as pl
from jax.experimental.pallas import tpu as pltpu

F32 = jnp.float32
BF16 = jnp.bfloat16
HIGHEST = lax.Precision.HIGHEST
LOG2E = 1.4426950408889634

D_MODEL = 1024
DEPTH = 2
GRID_W = 64
EPS = 1e-6
MLA_HEADS = 8
MLA_NOPE = 64
MLA_ROPE = 32
MLA_V = 64
MLA_Q_LORA = 256
MLA_KV_LORA = 128
MLA_SCALE = (MLA_NOPE + MLA_ROPE) ** -0.5
ROPE_BASE = 10000.0
GLA_HEADS = 4
GLA_DK = 256
GLA_DV = 512
GLA_HK = GLA_DK // GLA_HEADS
GLA_HV = GLA_DV // GLA_HEADS
GLA_GATE_RANK = 16
GLA_TAU = 16.0
HY_WIDTH = 512
HY_ORDER = 2
HY_SHORT = 3
HY_BANDS = 16
HY_POS_DIM = 1 + 2 * HY_BANDS
HY_FILTER_HIDDEN = 64
HY_FAST_DECAY = 0.3
HY_SLOW_DECAY = 1.5
HY_DECAY_TARGET = 1e-2
D_FF = 4 * D_MODEL
MLA_OUT = MLA_HEADS * MLA_V
IN_SIZES = (MLA_Q_LORA, MLA_KV_LORA, MLA_ROPE, GLA_DK, GLA_DK, GLA_DV, GLA_DV, GLA_GATE_RANK, GLA_GATE_RANK,
            (HY_ORDER + 1) * HY_WIDTH, 3 * D_MODEL)

LANES = 128
SUBLANES = 8
TM = 256
HEAD_SLOT = 128
VMEM_LIMIT = 56 * 1024 * 1024


def _cparams(n_axes, vmem=None):
    return pltpu.CompilerParams(dimension_semantics=("arbitrary",) * n_axes, vmem_limit_bytes=vmem)


def _const_spec(shape):
    nd = len(shape)
    return pl.BlockSpec(shape, lambda *_: (0,) * nd, pipeline_mode=pl.Buffered(1))


def _rms(x):
    return x * lax.rsqrt(jnp.mean(x * x, axis=-1, keepdims=True) + EPS)


def _sigmoid(x):
    return 1.0 / (1.0 + jnp.exp(-x))


def _dot(a, b):
    return jnp.dot(a, b, preferred_element_type=F32)


def _dot_nt(a, b):
    return lax.dot_general(a, b, (((1,), (1,)), ((), ())), preferred_element_type=F32)


def _mod_kernel(cc_ref, w_ref, b_ref, o_ref):
    s = cc_ref[...]
    s = s * _sigmoid(s)
    o_ref[0] = jnp.dot(s, w_ref[0], preferred_element_type=F32, precision=HIGHEST) + b_ref[0]


def _modulation(cc, ada_w, ada_b):
    tn = 1536
    n6 = ada_w.shape[-1]
    return pl.pallas_call(
        _mod_kernel,
        out_shape=jax.ShapeDtypeStruct((DEPTH, 16, n6), F32),
        grid=(DEPTH, n6 // tn),
        in_specs=[
            pl.BlockSpec((16, D_MODEL), lambda l, j: (0, 0)),
            pl.BlockSpec((1, D_MODEL, tn), lambda l, j: (l, 0, j)),
            pl.BlockSpec((1, 1, tn), lambda l, j: (l, 0, j)),
        ],
        out_specs=pl.BlockSpec((1, 16, tn), lambda l, j: (l, 0, j)),
        compiler_params=_cparams(2),
        name="modulation",
    )(cc, ada_w, ada_b.reshape(DEPTH, 1, n6))


W_A = 768
W_G = 2 * GLA_DK + 2 * GLA_DV


def _token_specs(d, n_lat_tiles, ctx_blk):
    return [pl.BlockSpec((1, TM, d), lambda b, i: (b, jnp.minimum(i, n_lat_tiles - 1), 0)),
            pl.BlockSpec((1, TM, d), lambda b, i: (b, ctx_blk, 0))]


def _token_tile(x_ref, c_ref, n_lat_tiles):
    return jnp.where(pl.program_id(1) < n_lat_tiles, x_ref[0], c_ref[0])


def _inproj_kernel(x_ref, c_ref, mod_ref, g1_ref, wa_ref, wg_ref, wh_ref, wz_ref, qn_ref, kvn_ref, wuq_ref, wukv_ref,
                   wa2_ref, ba_ref, cos_ref, sin_ref,
                   q_out, k_out, v_out, gq_out, gk_out, gv_out, gr_out, glog_out, hy_out, gate_out, *, n_lat_tiles):
    x = _token_tile(x_ref, c_ref, n_lat_tiles)
    shift = mod_ref[0, 0:1, :]
    scale = mod_ref[0, 1:2, :]
    h = (_rms(x) * g1_ref[...] * (1.0 + scale) + shift).astype(BF16)

    za = _dot(h, wa_ref[...])
    cos = cos_ref[...]
    sin = sin_ref[...]

    cqn = (_rms(za[:, 0:256]) * qn_ref[...]).astype(BF16)
    qab = _dot(cqn, wuq_ref[...])
    nq = MLA_HEADS * HEAD_SLOT
    for hd in range(MLA_HEADS):
        sl = slice(hd * HEAD_SLOT, (hd + 1) * HEAD_SLOT)
        qa = qab[:, hd * HEAD_SLOT:(hd + 1) * HEAD_SLOT]
        qb = qab[:, nq + hd * HEAD_SLOT:nq + (hd + 1) * HEAD_SLOT]
        q_out[0, :, sl] = ((qa * cos + qb * sin) * (MLA_SCALE * LOG2E)).astype(BF16)

    ckvn = (_rms(za[:, 256:384]) * kvn_ref[...]).astype(BF16)
    kv = _dot(ckvn, wukv_ref[...])
    krot = za[:, 384:512] * cos + za[:, 512:640] * sin
    for hd in range(MLA_HEADS):
        sl = slice(hd * HEAD_SLOT, (hd + 1) * HEAD_SLOT)
        k_out[0, :, sl] = (kv[:, sl] + krot).astype(BF16)
    ones_hi = (lax.broadcasted_iota(jnp.int32, (1, HEAD_SLOT), 1) >= MLA_V).astype(F32)
    for hd in range(MLA_HEADS):
        sl = slice(hd * HEAD_SLOT, (hd + 1) * HEAD_SLOT)
        v_out[0, :, sl] = (kv[:, nq + hd * HEAD_SLOT:nq + (hd + 1) * HEAD_SLOT] + ones_hi).astype(BF16)

    xg = _dot(za[:, 640:768].astype(BF16), wa2_ref[...]) + ba_ref[...]
    glog_out[0] = (jnp.minimum(xg, 0.0) - jnp.log(1.0 + jnp.exp(-jnp.abs(xg)))) * (1.0 / GLA_TAU)

    zg = _dot(h, wg_ref[...])
    gq_out[0] = (zg[:, 0:GLA_DK] * (GLA_HK ** -0.5)).astype(BF16)
    gk_out[0] = zg[:, GLA_DK:2 * GLA_DK].astype(BF16)
    gv_out[0] = zg[:, 2 * GLA_DK:2 * GLA_DK + GLA_DV].astype(BF16)
    gr = zg[:, 2 * GLA_DK + GLA_DV:]
    gr_out[0] = (gr * _sigmoid(gr)).astype(BF16)

    hy_out[0] = _dot(h, wh_ref[...])
    gate_out[0] = _sigmoid(_dot(h, wz_ref[...])).astype(BF16)


def _rope_partner(w):
    a = MLA_ROPE // 4
    perm = np.concatenate([np.arange(a, 2 * a), np.arange(0, a), np.arange(3 * a, 4 * a), np.arange(2 * a, 3 * a)])
    sign = np.concatenate([-np.ones(a), np.ones(a), -np.ones(a), np.ones(a)]).astype(np.float32)
    return w[:, perm] * sign


def _prep_inproj_weights(w_in, mla_w_uq, mla_w_ukv, gla_w_a2, gla_b_a):
    offs = np.concatenate([[0], np.cumsum(IN_SIZES)])
    seg = [w_in[:, offs[i]:offs[i + 1]] for i in range(len(IN_SIZES))]
    w_cq, w_ckv, w_kr, w_gq, w_gk, w_gv, w_gr, w_af, w_ab, w_hy, w_gate = seg
    d = w_in.shape[0]
    z = lambda n: jnp.zeros((d, n), w_in.dtype)
    kr_tile = jnp.concatenate([z(MLA_NOPE), w_kr, z(HEAD_SLOT - MLA_NOPE - MLA_ROPE)], axis=1)
    krp_tile = jnp.concatenate([z(MLA_NOPE), _rope_partner(w_kr), z(HEAD_SLOT - MLA_NOPE - MLA_ROPE)], axis=1)
    a_tile = jnp.concatenate([w_af, w_ab, z(LANES - 2 * GLA_GATE_RANK)], axis=1)
    wa = jnp.concatenate([w_cq, w_ckv, kr_tile, krp_tile, a_tile], axis=1)
    wg = jnp.concatenate([w_gq, w_gk, w_gv, w_gr], axis=1)

    dh = MLA_NOPE + MLA_ROPE
    zq = lambda n: jnp.zeros((MLA_Q_LORA, n), w_in.dtype)
    plain, partner = [], []
    for hd in range(MLA_HEADS):
        blk = mla_w_uq[:, hd * dh:(hd + 1) * dh]
        plain += [blk, zq(HEAD_SLOT - dh)]
        partner += [zq(MLA_NOPE), _rope_partner(blk[:, MLA_NOPE:]), zq(HEAD_SLOT - dh)]
    wuq = jnp.concatenate(plain + partner, axis=1)

    zk = jnp.zeros((MLA_KV_LORA, HEAD_SLOT - MLA_NOPE), w_in.dtype)
    kcols, vcols = [], []
    for hd in range(MLA_HEADS):
        blk = mla_w_ukv[:, hd * (MLA_NOPE + MLA_V):(hd + 1) * (MLA_NOPE + MLA_V)]
        kcols += [blk[:, :MLA_NOPE], zk]
        vcols += [blk[:, MLA_NOPE:], zk]
    wukv = jnp.concatenate(kcols + vcols, axis=1)

    wa2 = jnp.zeros((LANES, 2 * GLA_DK), w_in.dtype)
    wa2 = wa2.at[0:GLA_GATE_RANK, 0:GLA_DK].set(gla_w_a2[0])
    wa2 = wa2.at[GLA_GATE_RANK:2 * GLA_GATE_RANK, GLA_DK:].set(gla_w_a2[1])
    ba = jnp.concatenate([gla_b_a[0], gla_b_a[1]])[None, :]
    bf = lambda t: t.astype(BF16)
    return bf(wa), bf(wg), bf(w_hy), bf(w_gate), bf(wuq), bf(wukv), bf(wa2), ba


def _rope_tables(seq, ctx_len):
    rows = seq // GRID_W
    row = np.repeat(np.arange(rows, dtype=np.float64), GRID_W)
    col = np.tile(np.arange(GRID_W, dtype=np.float64), rows)
    a = MLA_ROPE // 4
    inv = ROPE_BASE ** (-np.arange(a, dtype=np.float64) / a)
    ang_r = row[:, None] * inv
    ang_c = col[:, None] * inv
    cos32 = np.concatenate([np.cos(ang_r), np.cos(ang_r), np.cos(ang_c), np.cos(ang_c)], axis=1)
    sin32 = np.concatenate([np.sin(ang_r), np.sin(ang_r), np.sin(ang_c), np.sin(ang_c)], axis=1)
    pad_r = HEAD_SLOT - MLA_NOPE - MLA_ROPE
    cos = np.concatenate([np.ones((seq, MLA_NOPE)), cos32, np.zeros((seq, pad_r))], axis=1)
    sin = np.concatenate([np.zeros((seq, MLA_NOPE)), sin32, np.zeros((seq, pad_r))], axis=1)
    cos_c = np.concatenate([np.ones((ctx_len, MLA_NOPE + MLA_ROPE)), np.zeros((ctx_len, pad_r))], axis=1)
    sin_c = np.zeros((ctx_len, HEAD_SLOT))
    return (jnp.asarray(np.concatenate([cos, cos_c], axis=0), F32),
            jnp.asarray(np.concatenate([sin, sin_c], axis=0), F32))


def _inproj(x_lat, x_ctx, ctx_blk, mod_l, g1, weights, q_norm, kv_norm, cos, sin, n_lat_tiles):
    bsz, _, d = x_lat.shape
    nt = n_lat_tiles + 1
    t = nt * TM
    wa, wg, wh, wz, wuq, wukv, wa2, ba = weights
    tile = lambda w: pl.BlockSpec((1, TM, w), lambda b, i: (b, i, 0))
    row = lambda w: pl.BlockSpec((1, w), lambda b, i: (0, 0))
    mod_spec = pl.BlockSpec((1, 6, d), lambda b, i: (jnp.where(i < n_lat_tiles, b, bsz), 0, 0))
    tab = pl.BlockSpec((TM, HEAD_SLOT), lambda b, i: (i, 0))
    nq = MLA_HEADS * HEAD_SLOT
    sds = lambda w, dt: jax.ShapeDtypeStruct((bsz, t, w), dt)
    out_shape = (sds(nq, BF16), sds(nq, BF16), sds(nq, BF16), sds(GLA_DK, BF16), sds(GLA_DK, BF16),
                 sds(GLA_DV, BF16), sds(GLA_DV, BF16), sds(2 * GLA_DK, F32), sds(3 * HY_WIDTH, F32),
                 sds(3 * D_MODEL, BF16))
    out_specs = (tile(nq), tile(nq), tile(nq), tile(GLA_DK), tile(GLA_DK), tile(GLA_DV), tile(GLA_DV),
                 tile(2 * GLA_DK), tile(3 * HY_WIDTH), tile(3 * D_MODEL))
    return pl.pallas_call(
        functools.partial(_inproj_kernel, n_lat_tiles=n_lat_tiles),
        out_shape=out_shape,
        grid=(bsz, nt),
        in_specs=_token_specs(d, n_lat_tiles, ctx_blk) + [
            mod_spec, row(d), _const_spec(wa.shape), _const_spec(wg.shape), _const_spec(wh.shape),
            _const_spec(wz.shape), row(MLA_Q_LORA), row(MLA_KV_LORA), _const_spec(wuq.shape),
            _const_spec(wukv.shape), _const_spec(wa2.shape), row(2 * GLA_DK), tab, tab],
        out_specs=out_specs,
        compiler_params=_cparams(2, VMEM_LIMIT),
        name="inproj",
    )(x_lat, x_ctx, mod_l, g1[None, :], wa, wg, wh, wz, q_norm[None, :], kv_norm[None, :], wuq, wukv, wa2, ba,
      cos, sin)


ATT_TK = 1024


ATT_TQ = 1024


def _attn_kernel(q_ref, k_ref, v_ref, o_ref, m_ref, acc_ref, *, chunks):
    tq = q_ref.shape[1]
    m_ref[...] = jnp.full(m_ref.shape, -jnp.inf, F32)
    acc_ref[...] = jnp.zeros(acc_ref.shape, F32)
    for r0, size in chunks:
        k = k_ref[0, pl.ds(r0, size), :]
        v = v_ref[0, pl.ds(r0, size), :]
        for hd in range(2):
            sl = slice(hd * HEAD_SLOT, (hd + 1) * HEAD_SLOT)
            s = _dot_nt(q_ref[0, :, sl], k[:, sl])
            m_prev = m_ref[hd]
            m_new = jnp.maximum(m_prev, jnp.max(s, axis=1, keepdims=True))
            p = jnp.exp2((s - jnp.concatenate([m_new] * (size // LANES), axis=1)).astype(BF16))
            acc_ref[hd] = jnp.exp2(m_prev - m_new) * acc_ref[hd] + _dot(p, v[:, sl])
            m_ref[hd] = m_new
    a0 = acc_ref[0]
    a1 = acc_ref[1]
    lane = lax.broadcasted_iota(jnp.int32, (tq, HEAD_SLOT), 1)
    o0 = a0 / pltpu.roll(a0, MLA_V, 1)
    o1 = pltpu.roll(a1, MLA_V, 1) / a1
    o_ref[0] = jnp.where(lane < MLA_V, o0, o1).astype(o_ref.dtype)


def _attention(q, k, v, seq, with_ctx_queries):
    bsz, t, _ = q.shape
    ctx_len = t - seq
    assert seq % ATT_TK == 0 and seq % ATT_TQ == 0 and seq % ctx_len == 0 and 2 * MLA_V == HEAD_SLOT
    pair = 2 * HEAD_SLOT
    n_chunks = seq // ATT_TK
    chunks = tuple((j * ATT_TK, ATT_TK) for j in range(n_chunks - 1))
    chunks += (((n_chunks - 1) * ATT_TK, ATT_TK + ctx_len),)
    scratch = lambda tq: [pltpu.VMEM((2, tq, LANES), F32), pltpu.VMEM((2, tq, HEAD_SLOT), F32)]
    y = pl.pallas_call(
        functools.partial(_attn_kernel, chunks=chunks),
        out_shape=jax.ShapeDtypeStruct((bsz, seq, MLA_OUT), BF16),
        grid=(bsz, MLA_HEADS // 2, seq // ATT_TQ),
        in_specs=[
            pl.BlockSpec((1, ATT_TQ, pair), lambda b, hp, i: (b, i, hp)),
            pl.BlockSpec((1, t, pair), lambda b, hp, i: (b, 0, hp)),
            pl.BlockSpec((1, t, pair), lambda b, hp, i: (b, 0, hp)),
        ],
        out_specs=pl.BlockSpec((1, ATT_TQ, HEAD_SLOT), lambda b, hp, i: (b, i, hp)),
        scratch_shapes=scratch(ATT_TQ),
        compiler_params=_cparams(3, VMEM_LIMIT),
        name="mla_attention",
    )(q, k, v)
    if not with_ctx_queries:
        return y, None
    cblk = seq // ctx_len
    ctx_rows = lambda w: pl.BlockSpec((1, ctx_len, w), lambda b, hp: (b, cblk, hp))
    y_ctx = pl.pallas_call(
        functools.partial(_attn_kernel, chunks=((0, ctx_len),)),
        out_shape=jax.ShapeDtypeStruct((bsz, ctx_len, MLA_OUT), BF16),
        grid=(bsz, MLA_HEADS // 2),
        in_specs=[ctx_rows(pair), ctx_rows(pair), ctx_rows(pair)],
        out_specs=pl.BlockSpec((1, ctx_len, HEAD_SLOT), lambda b, hp: (b, 0, hp)),
        scratch_shapes=scratch(ctx_len),
        compiler_params=_cparams(2),
        name="mla_attention_ctx",
    )(q, k, v)
    return y, y_ctx


GLA_LEVELS = int(math.log2(TM))
GLA_SAFE_SPAN = 60.0


def _gla_level_matrices():
    i = np.arange(TM)[:, None]
    t = np.arange(TM)[None, :]
    fwd = [(t <= i)]
    for lv in range(GLA_LEVELS):
        m = TM >> (lv + 1)
        lo = (i // m) * m
        later = ((i // m) % 2) == 1
        q_part = later & (t >= lo) & (t <= i)
        k_part = (~later) & (t > i) & (t <= lo + m - 1)
        fwd.append(q_part | k_part)
    fwd = np.concatenate(fwd, axis=0).astype(np.float32)
    nb = 1 + GLA_LEVELS
    bwd = fwd.reshape(nb, TM, TM)[:, ::-1, ::-1].reshape(nb * TM, TM)
    return np.stack([fwd, bwd])


def _gla_kernel(q_ref, k_ref, v_ref, g_ref, mall_ref, o_ref, s_ref, a_ref):
    d = pl.program_id(1)
    step = pl.program_id(2)

    @pl.when(step == 0)
    def _():
        s_ref[...] = jnp.zeros(s_ref.shape, F32)

    q = q_ref[0].astype(F32)
    k = k_ref[0].astype(F32)
    v = v_ref[0]
    g = g_ref[0]
    g2 = jnp.concatenate([g.astype(BF16), (g - g.astype(BF16).astype(F32)).astype(BF16)], axis=1)
    e2 = _dot(mall_ref[0, 0:TM, :], g2)
    g_cum = e2[:, :GLA_DK] + e2[:, GLA_DK:]
    g_tot = jnp.sum(g, axis=0, keepdims=True)

    row = lax.broadcasted_iota(jnp.int32, (TM, TM), 0)
    col = lax.broadcasted_iota(jnp.int32, (TM, TM), 1)
    lane_head = lax.broadcasted_iota(jnp.int32, (TM, GLA_DK), 1) // GLA_HK
    tok = lax.broadcasted_iota(jnp.int32, (TM, GLA_DK), 0)
    eye = row == col

    def stack_heads(t):
        return jnp.concatenate([jnp.where(lane_head == hd, t, 0.0) for hd in range(GLA_HEADS)], axis=0).astype(BF16)

    span = jnp.max(-g_tot)

    @pl.when(span < GLA_SAFE_SPAN)
    def _():
        res = _dot_nt(stack_heads(q * jnp.exp(g_cum)), (k * jnp.exp(-g_cum)).astype(BF16))
        seen = (col - row) * (1 - 2 * d) <= 0
        for hd in range(GLA_HEADS):
            a_ref[hd] = jnp.where(seen, res[hd * TM:(hd + 1) * TM], 0.0)

    @pl.when(span >= GLA_SAFE_SPAN)
    def _():
        e2l = _dot(mall_ref[0, TM:, :], g2)
        e_lv = e2l[:, :GLA_DK] + e2l[:, GLA_DK:]
        res = _dot_nt(stack_heads(q), k.astype(BF16))
        for hd in range(GLA_HEADS):
            a_ref[hd] = jnp.where(eye, res[hd * TM:(hd + 1) * TM], 0.0)
        for lv in range(GLA_LEVELS):
            m = TM >> (lv + 1)
            w = jnp.exp(e_lv[lv * TM:(lv + 1) * TM])
            bit = (tok // m) % 2
            q_act = bit != d
            qt = jnp.where(q_act, q * w, 0.0)
            kt = jnp.where(q_act, 0.0, k * w).astype(BF16)
            res = _dot_nt(stack_heads(qt), kt)
            if m == TM // 2:
                for hd in range(GLA_HEADS):
                    a_ref[hd] += res[hd * TM:(hd + 1) * TM]
            else:
                same = (row // (2 * m)) == (col // (2 * m))
                for hd in range(GLA_HEADS):
                    a_ref[hd] += jnp.where(same, res[hd * TM:(hd + 1) * TM], 0.0)

    s_old = s_ref[...]
    o_inter = _dot((q * jnp.exp(g_cum)).astype(BF16), s_old.astype(BF16))
    for hd in range(GLA_HEADS):
        sl = slice(hd * GLA_HV, (hd + 1) * GLA_HV)
        o_intra = _dot(a_ref[hd].astype(BF16), v[:, sl])
        o_ref[0, 0, :, sl] = (o_intra + o_inter[:, sl]).astype(o_ref.dtype)

    kdec_t = (k * jnp.exp(g_tot - g_cum)).T.astype(BF16)
    upd = _dot(kdec_t, v)
    same_head = (lax.broadcasted_iota(jnp.int32, (GLA_DK, GLA_DV), 0) // GLA_HK
                 == lax.broadcasted_iota(jnp.int32, (GLA_DK, GLA_DV), 1) // GLA_HV)
    dec_col = jnp.sum(jnp.where(eye, jnp.broadcast_to(jnp.exp(g_tot), (TM, GLA_DK)), 0.0), axis=1, keepdims=True)
    s_ref[...] = dec_col * s_old + jnp.where(same_head, upd, 0.0)


def _gla(gq, gk, gv, glog, n_lat_tiles):
    assert GLA_DK == TM
    bsz, t, _ = gq.shape
    nt = t // TM
    mall = jnp.asarray(_gla_level_matrices(), dtype=BF16)

    def tile_idx(d, s):
        return jnp.where(s == 0, n_lat_tiles, jnp.where(d == 0, s - 1, n_lat_tiles - s))

    tok = lambda w: pl.BlockSpec((1, TM, w), lambda b, d, s: (b, tile_idx(d, s), 0))
    return pl.pallas_call(
        _gla_kernel,
        out_shape=jax.ShapeDtypeStruct((bsz, 2, t, GLA_DV), BF16),
        grid=(bsz, 2, nt),
        in_specs=[tok(GLA_DK), tok(GLA_DK), tok(GLA_DV),
                  pl.BlockSpec((1, TM, GLA_DK), lambda b, d, s: (b, tile_idx(d, s), d)),
                  pl.BlockSpec((1,) + mall.shape[1:], lambda b, d, s: (d, 0, 0))],
        out_specs=pl.BlockSpec((1, 1, TM, GLA_DV), lambda b, d, s: (b, d, tile_idx(d, s), 0)),
        scratch_shapes=[pltpu.VMEM((GLA_DK, GLA_DV), F32), pltpu.VMEM((GLA_HEADS, TM, TM), F32)],
        compiler_params=_cparams(3),
        name="gla_scan",
    )(gq, gk, gv, glog, mall)


HY_N1 = 64
HY_N2 = 128
HY_SLAB = SUBLANES
HY_CB = 256
HY_KG = 16
HY_SPEC_CB = 128
HY_UNROLL = 8


def _hyena_dft_constants(seq):
    n = 2 * seq
    assert n == HY_N1 * HY_N2
    nh = HY_N1 // 2
    kh = HY_N2 // 2
    eye = np.eye(HY_SLAB)
    k1 = np.arange(HY_N1)
    th = 2 * np.pi * np.outer(k1 + 0.5, np.arange(nh)) / HY_N1
    fwd1 = np.concatenate([np.kron(np.cos(th), eye), np.kron(-np.sin(th), eye)], axis=0)
    inv1 = (2.0 / n) * np.concatenate([np.kron(np.cos(th).T, eye), np.kron(-np.sin(th).T, eye)], axis=1)
    nlo = np.arange(HY_N2)
    k2 = np.arange(kh)
    ph = 2 * np.pi * (k2[None, :, None] * nlo[None, None, :] / HY_N2
                      + (k1[:, None, None] + 0.5) * nlo[None, None, :] / n)
    c, s = np.cos(ph), np.sin(ph)
    fwd2 = np.concatenate([np.concatenate([c, s], axis=2), np.concatenate([-s, c], axis=2)], axis=1)
    ct, st = c.transpose(0, 2, 1), s.transpose(0, 2, 1)
    inv2 = np.concatenate([np.concatenate([ct, -st], axis=2), np.concatenate([st, ct], axis=2)], axis=1)
    return fwd1, fwd2, inv2, inv1


def _hyena_ctx_dft_constants(ctx_len):
    n = 2 * ctx_len
    th = 2 * np.pi * np.outer(np.arange(ctx_len) + 0.5, np.arange(ctx_len)) / n
    fwd = np.concatenate([np.cos(th), -np.sin(th)], axis=0)
    inv = (2.0 / n) * np.concatenate([np.cos(th).T, -np.sin(th).T], axis=1)
    return fwd, inv


def _hyena_features(length):
    pos = np.arange(length, dtype=np.float64)
    t = pos / max(length - 1, 1)
    f = np.linspace(1e-4, HY_BANDS - 1, HY_BANDS)
    ang = (2.0 * math.pi / length) * pos[:, None] * f
    feat = np.concatenate([t[:, None], np.cos(ang), np.sin(ang)], axis=-1)
    return jnp.asarray(np.pad(feat, ((0, 0), (0, LANES - HY_POS_DIM))), F32)


def _hy_filter_kernel(feat_ref, w1_ref, b1_ref, w2_ref, b2_ref, w3_ref, b3_ref, absd_ref, h_ref, s_ref):
    i = pl.program_id(0)
    feat = feat_ref[...]
    hp = lambda a, b: jnp.dot(a, b, preferred_element_type=F32, precision=HIGHEST)
    hdn = jnp.sin(hp(feat, w1_ref[...]) + b1_ref[...])
    hdn = jnp.sin(hp(hdn, w2_ref[...]) + b2_ref[...])
    h = hp(hdn, w3_ref[...]) + b3_ref[...]
    window = jnp.exp(-feat[:, 0:1] * absd_ref[...])
    h = h * jnp.concatenate([window] * (2 * HY_ORDER), axis=1)
    h_ref[...] = h

    @pl.when(i == 0)
    def _():
        s_ref[...] = jnp.zeros(s_ref.shape, F32)

    s_ref[...] += jnp.sum(jnp.abs(h), axis=0, keepdims=True)


def _hyena_filters_raw(length, filt_w):
    w1, b1, w2, b2, w3, b3 = filt_w
    nf = 2 * HY_ORDER * HY_WIDTH
    tr = min(length, 512)
    deltas = np.linspace(math.log(HY_DECAY_TARGET) / HY_FAST_DECAY, math.log(HY_DECAY_TARGET) / HY_SLOW_DECAY,
                         HY_WIDTH, dtype=np.float32)
    absd = jnp.asarray(np.abs(deltas))[None, :]
    w1p = jnp.pad(w1, ((0, LANES - HY_POS_DIM), (0, 0)))
    full = lambda shp: pl.BlockSpec(shp, lambda i: (0,) * len(shp))
    return pl.pallas_call(
        _hy_filter_kernel,
        out_shape=(jax.ShapeDtypeStruct((length, nf), F32), jax.ShapeDtypeStruct((1, nf), F32)),
        grid=(length // tr,),
        in_specs=[pl.BlockSpec((tr, LANES), lambda i: (i, 0)), full((LANES, HY_FILTER_HIDDEN)),
                  full((1, HY_FILTER_HIDDEN)), full((HY_FILTER_HIDDEN, HY_FILTER_HIDDEN)), full((1, HY_FILTER_HIDDEN)),
                  full((HY_FILTER_HIDDEN, nf)), full((1, nf)), full((1, HY_WIDTH))],
        out_specs=(pl.BlockSpec((tr, nf), lambda i: (i, 0)), full((1, nf))),
        compiler_params=_cparams(1),
        name="hyena_filter_mlp",
    )(_hyena_features(length), w1p, b1[None, :], w2, b2[None, :], w3, b3[None, :], absd)


def _dot_split(m, x):
    x_hi = x.astype(BF16)
    x_lo = (x - x_hi.astype(F32)).astype(BF16)
    n = x.shape[1]
    r = _dot(m, jnp.concatenate([x_hi, x_lo], axis=1))
    return r[:, :n] + r[:, n:]


def _odft_stage1(src_at, mm, s_re, s_im, unroll=2):
    nk = s_re.shape[0]
    half = nk * HY_SLAB

    def body(j, carry):
        r0 = pl.multiple_of(j * HY_SLAB, HY_SLAB)
        slab = src_at(r0)
        cb = slab.shape[-1]
        res = mm(slab.reshape(-1, cb))
        s_re[:, pl.ds(r0, HY_SLAB), :] = res[:half].reshape(nk, HY_SLAB, cb)
        s_im[:, pl.ds(r0, HY_SLAB), :] = res[half:].reshape(nk, HY_SLAB, cb)
        return carry

    lax.fori_loop(0, HY_N2 // HY_SLAB, body, 0, unroll=unroll)


def _hy_spectrum_kernel(hf_ref, hb_ref, sf_ref, sb_ref, fwd1_ref, fwd2_ref, o_ref, s_re, s_im):
    kh = HY_N2 // 2
    mm1 = lambda x: _dot_split(fwd1_ref[...], x)

    def middle(sign):
        def body(k1, carry):
            a = jnp.concatenate([s_re[k1], s_im[k1]], axis=0)
            x = _dot_split(fwd2_ref[k1], a)
            if sign is None:
                o_ref[0, k1] = x[:kh]
                o_ref[1, k1] = x[kh:]
            else:
                inv_norm = 1.0 / (sf_ref[...] + sb_ref[...])
                o_ref[0, k1] = (o_ref[0, k1] + x[:kh]) * inv_norm
                o_ref[1, k1] = (o_ref[1, k1] - x[kh:]) * inv_norm
            return carry
        lax.fori_loop(0, HY_N1, body, 0, unroll=8)

    _odft_stage1(lambda r0: hf_ref[:, pl.ds(r0, HY_SLAB), :], mm1, s_re, s_im, unroll=4)
    middle(None)

    def bwd_slab(r0):
        slab = hb_ref[:, pl.ds(r0, HY_SLAB), :]
        nhi = lax.broadcasted_iota(jnp.int32, slab.shape, 0)
        r = lax.broadcasted_iota(jnp.int32, slab.shape, 1)
        return jnp.where((nhi == 0) & (r + r0 == 0), 0.0, slab)

    _odft_stage1(bwd_slab, mm1, s_re, s_im, unroll=4)
    middle(-1)


def _hyena_filter_spectrum(seq, w1, b1, w2, b2, w3, b3):
    h_raw, s = _hyena_filters_raw(seq, (w1, b1, w2, b2, w3, b3))
    nh = HY_N1 // 2
    nc = HY_ORDER * HY_WIDTH
    h3 = h_raw.reshape(nh, HY_N2, 2 * nc)
    fwd1, fwd2, _, _ = _hyena_dft_constants(seq)
    fwd1 = jnp.asarray(fwd1, BF16)
    fwd2 = jnp.asarray(fwd2, BF16)
    scb = HY_SPEC_CB
    ncb = nc // scb
    return pl.pallas_call(
        _hy_spectrum_kernel,
        out_shape=jax.ShapeDtypeStruct((2, HY_N1, HY_N2 // 2, nc), F32),
        grid=(ncb,),
        in_specs=[pl.BlockSpec((nh, HY_N2, scb), lambda c: (0, 0, c)),
                  pl.BlockSpec((nh, HY_N2, scb), lambda c: (0, 0, ncb + c)),
                  pl.BlockSpec((1, scb), lambda c: (0, c)),
                  pl.BlockSpec((1, scb), lambda c: (0, ncb + c)),
                  _const_spec(fwd1.shape), _const_spec(fwd2.shape)],
        out_specs=pl.BlockSpec((2, HY_N1, HY_N2 // 2, scb), lambda c: (0, 0, 0, c)),
        scratch_shapes=[pltpu.VMEM((HY_N1, HY_N2, scb), F32), pltpu.VMEM((HY_N1, HY_N2, scb), F32)],
        compiler_params=_cparams(1, VMEM_LIMIT),
        name="hyena_filter_spectrum",
    )(h3, h3, s, s, fwd1, fwd2)


def _short_conv_chunk(ref, c, n_chunks, w_ref, b_ref):
    per = TM // HY_N2
    cur = ref[0, pl.ds(per * c, per)]
    cb = cur.shape[-1]
    cur = cur.reshape(TM, cb)
    prev = ref[0, jnp.maximum(per * c - 1, 0), pl.ds(HY_N2 - SUBLANES, SUBLANES), :][SUBLANES - 1:SUBLANES]
    nxt = ref[0, jnp.minimum(per * c + per, per * n_chunks - 1), pl.ds(0, SUBLANES), :][0:1]
    prev = jnp.where(c > 0, prev, 0.0)
    nxt = jnp.where(c < n_chunks - 1, nxt, 0.0)
    rowi = lax.broadcasted_iota(jnp.int32, (TM, cb), 0)
    dn = jnp.where(rowi == 0, prev, pltpu.roll(cur, 1, 0))
    up = jnp.where(rowi == TM - 1, nxt, pltpu.roll(cur, TM - 1, 0))
    return b_ref[...] + w_ref[0:1, :] * dn + w_ref[1:2, :] * cur + w_ref[2:3, :] * up


def _hy_conv_kernel(*refs, conv_y):
    if conv_y:
        (y_ref, g_ref, h_ref, bias_ref, wy_ref, by_ref, wg_ref, bg_ref, fwd1_ref, fwd2_ref, inv2_ref, inv1_ref,
         o_ref, s_re, s_im, gs_ref, us_ref) = refs
    else:
        (y_ref, g_ref, h_ref, bias_ref, wg_ref, bg_ref, fwd1_ref, fwd2_ref, inv2_ref, inv1_ref,
         o_ref, s_re, s_im, gs_ref) = refs
    grp = pl.program_id(2)
    last = pl.num_programs(2) - 1
    nh = HY_N1 // 2
    per = TM // HY_N2
    n_chunks = nh // per
    cb = o_ref.shape[-1]
    kh = HY_N2 // 2

    @pl.when(grp == 0)
    def _():
        def pre(c, carry):
            gs_ref[pl.ds(per * c, per)] = _short_conv_chunk(g_ref, c, n_chunks, wg_ref, bg_ref).reshape(per, HY_N2, cb)
            if conv_y:
                us_ref[pl.ds(per * c, per)] = _short_conv_chunk(y_ref, c, n_chunks, wy_ref, by_ref).reshape(
                    per, HY_N2, cb)
            return carry
        lax.fori_loop(0, n_chunks, pre, 0)

    if conv_y:
        u_at = lambda r0: us_ref[:, pl.ds(r0, HY_SLAB), :]
    else:
        u_at = lambda r0: y_ref[0, :, pl.ds(r0, HY_SLAB), :]

    _odft_stage1(u_at, lambda x: _dot(fwd1_ref[0], x.astype(BF16)), s_re, s_im, unroll=HY_UNROLL)

    def middle(k1, carry):
        a = jnp.concatenate([s_re[k1], s_im[k1]], axis=0).astype(BF16)
        x = _dot(fwd2_ref[k1], a)
        xr, xi = x[:kh], x[kh:]
        hr, hi = h_ref[0, k1], h_ref[1, k1]
        y = jnp.concatenate([xr * hr - xi * hi, xr * hi + xi * hr], axis=0).astype(BF16)
        bm = _dot(inv2_ref[k1], y)
        s_re[k1] = bm[:HY_N2]
        s_im[k1] = bm[HY_N2:]
        return carry

    lax.fori_loop(0, HY_KG, middle, 0, unroll=2 * HY_UNROLL)

    def partial_conv(r0):
        slab = jnp.concatenate([s_re[:, pl.ds(r0, HY_SLAB), :].reshape(HY_KG * HY_SLAB, cb),
                                s_im[:, pl.ds(r0, HY_SLAB), :].reshape(HY_KG * HY_SLAB, cb)], axis=0).astype(BF16)
        return _dot(inv1_ref[0], slab).reshape(nh, HY_SLAB, cb)

    def post_loop(fn):
        def post(j, carry):
            r0 = pl.multiple_of(j * HY_SLAB, HY_SLAB)
            o_ref[0, :, pl.ds(r0, HY_SLAB), :] = fn(r0, partial_conv(r0))
            return carry
        lax.fori_loop(0, HY_N2 // HY_SLAB, post, 0, unroll=HY_UNROLL)

    @pl.when(grp == 0)
    def _():
        post_loop(lambda r0, part: part)

    @pl.when((grp > 0) & (grp < last))
    def _():
        post_loop(lambda r0, part: o_ref[0, :, pl.ds(r0, HY_SLAB), :] + part)

    @pl.when(grp == last)
    def _():
        post_loop(lambda r0, part: gs_ref[:, pl.ds(r0, HY_SLAB), :]
                  * (o_ref[0, :, pl.ds(r0, HY_SLAB), :] + part + bias_ref[...] * u_at(r0)))


def _hyena_order(y4, y_col0, z4, gate_col0, hspec, order, hy_bias, short_w, short_b, consts, conv_y):
    bsz = z4.shape[0]
    nh = HY_N1 // 2
    ncb = HY_WIDTH // HY_CB
    ngrp = HY_N1 // HY_KG
    fwd1, fwd2, inv2, inv1 = consts
    blk4 = lambda off: pl.BlockSpec((1, nh, HY_N2, HY_CB), lambda c, b, g: (b, 0, 0, off + c))
    rowspec = lambda rows, off: pl.BlockSpec((rows, HY_CB), lambda c, b, g: (0, off + c))
    grouped = lambda shp: pl.BlockSpec(shp, lambda c, b, g: (g, 0, 0))
    in_specs = [blk4(y_col0), blk4(gate_col0),
                pl.BlockSpec((2, HY_KG, HY_N2 // 2, HY_CB), lambda c, b, g: (0, g, 0, order * ncb + c)),
                rowspec(1, 0)]
    args = [y4, z4, hspec, hy_bias.reshape(1, -1)]
    if conv_y:
        in_specs += [rowspec(HY_SHORT, y_col0), rowspec(1, y_col0)]
        args += [short_w, short_b[None, :]]
    in_specs += [rowspec(HY_SHORT, gate_col0), rowspec(1, gate_col0)]
    args += [short_w, short_b[None, :]]
    in_specs += [grouped((1,) + fwd1.shape[1:]), grouped((HY_KG,) + fwd2.shape[1:]),
                 grouped((HY_KG,) + inv2.shape[1:]), grouped((1,) + inv1.shape[1:])]
    args += [fwd1, fwd2, inv2, inv1]
    scratch = [pltpu.VMEM((HY_KG, HY_N2, HY_CB), F32), pltpu.VMEM((HY_KG, HY_N2, HY_CB), F32),
               pltpu.VMEM((nh, HY_N2, HY_CB), F32)]
    if conv_y:
        scratch.append(pltpu.VMEM((nh, HY_N2, HY_CB), F32))
    return pl.pallas_call(
        functools.partial(_hy_conv_kernel, conv_y=conv_y),
        out_shape=jax.ShapeDtypeStruct((bsz, nh, HY_N2, HY_WIDTH), F32),
        grid=(ncb, bsz, ngrp),
        in_specs=in_specs,
        out_specs=pl.BlockSpec((1, nh, HY_N2, HY_CB), lambda c, b, g: (b, 0, 0, c)),
        scratch_shapes=scratch,
        compiler_params=_cparams(3, VMEM_LIMIT),
        name="hyena_conv%d" % order,
    )(*args)


def _hyena_latent(z_hy, hspec, short_w, short_b, hy_bias):
    bsz, t, _ = z_hy.shape
    seq = HY_N1 * HY_N2 // 2
    fwd1, fwd2, inv2, inv1 = _hyena_dft_constants(seq)
    ngrp = HY_N1 // HY_KG
    rows = HY_KG * HY_SLAB
    fwd1 = fwd1.reshape(2, ngrp, rows, -1).transpose(1, 0, 2, 3).reshape(ngrp, 2 * rows, -1)
    inv1 = inv1.reshape(-1, 2, ngrp, rows).transpose(2, 0, 1, 3).reshape(ngrp, -1, 2 * rows)
    consts = tuple(jnp.asarray(m, BF16) for m in (fwd1, fwd2, inv2, inv1))
    z4 = z_hy.reshape(bsz, t // HY_N2, HY_N2, 3 * HY_WIDTH)
    ncb = HY_WIDTH // HY_CB
    nh = seq // HY_N2
    y1 = _hyena_order(z4, 2 * ncb, z4, 0, hspec, 0, hy_bias[0], short_w, short_b, consts, True)
    y2 = _hyena_order(y1, 0, z4, ncb, hspec, 1, hy_bias[1], short_w, short_b, consts, False)
    return y2.reshape(bsz, seq, HY_WIDTH)


def _hy_ctx_spectrum_kernel(h_ref, s_ref, fwd_ref, o_ref):
    lc = h_ref.shape[0]
    nc = HY_ORDER * HY_WIDTH
    hp = lambda a, b: jnp.dot(a, b, preferred_element_type=F32, precision=HIGHEST)
    h = h_ref[...]
    rowi = lax.broadcasted_iota(jnp.int32, (lc, nc), 0)
    xf = hp(fwd_ref[...], h[:, :nc])
    xb = hp(fwd_ref[...], jnp.where(rowi == 0, 0.0, h[:, nc:]))
    inv_norm = 1.0 / (s_ref[:, :nc] + s_ref[:, nc:])
    o_ref[0] = (xf[:lc] + xb[:lc]) * inv_norm
    o_ref[1] = (xf[lc:] - xb[lc:]) * inv_norm


def _hy_ctx_conv_kernel(x1_ref, x2_ref, v_ref, h_ref, bias_ref, w_ref, b_ref, fwd_ref, inv_ref, o_ref):
    lc = o_ref.shape[1]
    rowi = lax.broadcasted_iota(jnp.int32, (lc, HY_WIDTH), 0)

    def short(ref, part):
        cur = ref[0].reshape(lc, HY_WIDTH)
        sl = slice(part * HY_WIDTH, (part + 1) * HY_WIDTH)
        dn = jnp.where(rowi == 0, 0.0, pltpu.roll(cur, 1, 0))
        up = jnp.where(rowi == lc - 1, 0.0, pltpu.roll(cur, lc - 1, 0))
        return b_ref[:, sl] + w_ref[0:1, sl] * dn + w_ref[1:2, sl] * cur + w_ref[2:3, sl] * up

    y = short(v_ref, 2)
    for order, gref in enumerate((x1_ref, x2_ref)):
        sl = slice(order * HY_WIDTH, (order + 1) * HY_WIDTH)
        x = _dot(fwd_ref[...], y.astype(BF16))
        xr, xi = x[:lc], x[lc:]
        hr, hi = h_ref[0, :, sl], h_ref[1, :, sl]
        prod = jnp.concatenate([xr * hr - xi * hi, xr * hi + xi * hr], axis=0).astype(BF16)
        conv = _dot(inv_ref[...], prod)
        y = short(gref, order) * (conv + bias_ref[order:order + 1, :] * y)
    o_ref[0] = y


def _hyena_ctx(z_hy, filt_w, short_w, short_b, hy_bias):
    bsz, t, _ = z_hy.shape
    seq = HY_N1 * HY_N2 // 2
    lc = t - seq
    per = lc // HY_N2
    h_raw, s = _hyena_filters_raw(lc, filt_w)
    fwd, inv = _hyena_ctx_dft_constants(lc)
    nc = HY_ORDER * HY_WIDTH
    full = lambda shp: pl.BlockSpec(shp, lambda *_: (0,) * len(shp))
    hspec = pl.pallas_call(
        _hy_ctx_spectrum_kernel,
        out_shape=jax.ShapeDtypeStruct((2, lc, nc), F32),
        grid=(1,),
        in_specs=[full(h_raw.shape), full(s.shape), full(fwd.shape)],
        out_specs=full((2, lc, nc)),
        compiler_params=_cparams(1),
        name="hyena_ctx_spectrum",
    )(h_raw, s, jnp.asarray(fwd, F32))
    z4 = z_hy.reshape(bsz, t // HY_N2, HY_N2, 3 * HY_WIDTH)
    blk = lambda part: pl.BlockSpec((1, per, HY_N2, HY_WIDTH), lambda b: (b, seq // lc, 0, part))
    return pl.pallas_call(
        _hy_ctx_conv_kernel,
        out_shape=jax.ShapeDtypeStruct((bsz, lc, HY_WIDTH), F32),
        grid=(bsz,),
        in_specs=[blk(0), blk(1), blk(2), full((2, lc, nc)), full((HY_ORDER, HY_WIDTH)),
                  full((HY_SHORT, 3 * HY_WIDTH)), full((1, 3 * HY_WIDTH)), full(fwd.shape), full(inv.shape)],
        out_specs=pl.BlockSpec((1, lc, HY_WIDTH), lambda b: (b, 0, 0)),
        compiler_params=_cparams(1),
        name="hyena_ctx_conv",
    )(z4, z4, z4, hspec, hy_bias, short_w, short_b[None, :], jnp.asarray(fwd, BF16), jnp.asarray(inv, BF16))


FF_CHUNK = 1024


def _merge_mlp_kernel(x_ref, c_ref, mod_ref, ymla_ref, ymlac_ref, of_ref, ob_ref, gr_ref, yhy_ref, yhyc_ref, zg_ref, on_ref,
                      wm_ref, wgl_ref, wh_ref, wo_ref, g2_ref, w1_ref, w2_ref, fg_ref, o_ref, *, final, n_lat_tiles):
    o = of_ref[0, 0].astype(F32) + ob_ref[0, 0].astype(F32)
    silu = gr_ref[0].astype(F32)
    parts = []
    for hd in range(GLA_HEADS):
        sl = slice(hd * GLA_HV, (hd + 1) * GLA_HV)
        parts.append((_rms(o[:, sl]) * on_ref[...] * silu[:, sl]).astype(BF16))
    y_gla = jnp.concatenate(parts, axis=1)
    zg = zg_ref[0].astype(F32)
    d = x_ref.shape[-1]
    m = zg[:, 0:d] * _dot(_token_tile(ymla_ref, ymlac_ref, n_lat_tiles), wm_ref[...])
    m = m + zg[:, d:2 * d] * _dot(y_gla, wgl_ref[...])
    m = m + zg[:, 2 * d:3 * d] * _dot(_token_tile(yhy_ref, yhyc_ref, n_lat_tiles).astype(BF16), wh_ref[...])
    out = _dot(m.astype(BF16), wo_ref[...])
    x = _token_tile(x_ref, c_ref, n_lat_tiles) + mod_ref[0, 2:3, :] * out

    h = (_rms(x) * g2_ref[...] * (1.0 + mod_ref[0, 4:5, :]) + mod_ref[0, 3:4, :]).astype(BF16)
    acc = jnp.zeros(x.shape, F32)
    for j in range(w1_ref.shape[1] // FF_CHUNK):
        a = jnp.maximum(_dot(h, w1_ref[:, j * FF_CHUNK:(j + 1) * FF_CHUNK]), 0.0)
        acc = acc + _dot((a * a).astype(BF16), w2_ref[j * FF_CHUNK:(j + 1) * FF_CHUNK, :])
    xn = x + mod_ref[0, 5:6, :] * acc
    if final:
        xn = _rms(xn) * fg_ref[...]
    o_ref[0] = xn


def _mod_spec(d, n_lat_tiles, bsz):
    return pl.BlockSpec((1, 6, d), lambda b, i: (jnp.where(i < n_lat_tiles, b, bsz), 0, 0))


def _merge_mlp(x_lat, x_ctx, ctx_blk, mod_l, y_mla, y_mla_ctx, o_gla, gr, y_hy, y_hy_ctx, gate, out_norm,
               w_o_mla, w_o_gla, w_o_hy, w_out, g2, w1, w2, final_g, n_tiles, n_lat_tiles, final):
    bsz, _, d = x_lat.shape
    if y_mla_ctx is None:
        y_mla_ctx, y_hy_ctx = y_mla, y_hy
    tile = lambda w: pl.BlockSpec((1, TM, w), lambda b, i: (b, i, 0))
    row = lambda w: pl.BlockSpec((1, w), lambda b, i: (0, 0))
    dirspec = lambda dr: pl.BlockSpec((1, 1, TM, GLA_DV), lambda b, i: (b, dr, i, 0))
    bf = lambda w: w.astype(BF16)
    return pl.pallas_call(
        functools.partial(_merge_mlp_kernel, final=final, n_lat_tiles=n_lat_tiles),
        out_shape=jax.ShapeDtypeStruct((bsz, n_tiles * TM, d), F32),
        grid=(bsz, n_tiles),
        in_specs=_token_specs(d, n_lat_tiles, ctx_blk) + [_mod_spec(d, n_lat_tiles, bsz)]
        + _token_specs(MLA_OUT, n_lat_tiles, 0) + [dirspec(0), dirspec(1), tile(GLA_DV)]
        + _token_specs(HY_WIDTH, n_lat_tiles, 0) + [
            tile(3 * d), row(GLA_HV),
            _const_spec(w_o_mla.shape), _const_spec(w_o_gla.shape), _const_spec(w_o_hy.shape),
            _const_spec(w_out.shape), row(d), _const_spec(w1.shape), _const_spec(w2.shape), row(d)],
        out_specs=tile(d),
        compiler_params=_cparams(2, VMEM_LIMIT),
        name="merge_mlp",
    )(x_lat, x_ctx, mod_l, y_mla, y_mla_ctx, o_gla, o_gla, gr, y_hy, y_hy_ctx, gate, out_norm[None, :],
      bf(w_o_mla), bf(w_o_gla), bf(w_o_hy), bf(w_out), g2[None, :], bf(w1), bf(w2), final_g[None, :])


def kernel(x, c, ctx, c_ctx, ada_w, ada_b, norm1_g, norm2_g, w_in, mla_q_norm, mla_w_uq, mla_kv_norm, mla_w_ukv, gla_w_a2, gla_b_a, gla_out_norm, hy_short_w, hy_short_b, hy_f_w1, hy_f_b1, hy_f_w2, hy_f_b2, hy_f_w3, hy_f_b3, hy_bias, w_o_mla, w_o_gla, w_o_hy, w_out, ff_w1, ff_w2, final_norm_g):
    bsz, seq, d = x.shape
    ctx_len = ctx.shape[1]
    n_lat = seq // TM
    assert ctx_len == TM and seq % TM == 0
    n_all = n_lat + 1
    x_lat, x_ctx, ctx_blk = x, ctx, 0
    cc = jnp.zeros((16, d), F32).at[:bsz].set(c).at[bsz].set(c_ctx)
    mod = _modulation(cc, ada_w, ada_b).reshape(DEPTH, 16, 6, d)
    cos, sin = _rope_tables(seq, ctx_len)
    for l in range(DEPTH):
        last = l == DEPTH - 1
        n_tiles = n_lat if last else n_all
        weights = _prep_inproj_weights(w_in[l], mla_w_uq[l], mla_w_ukv[l], gla_w_a2[l], gla_b_a[l])
        q, k, v, gq, gk, gv, gr, glog, z_hy, z_gate = _inproj(x_lat, x_ctx, ctx_blk, mod[l], norm1_g[l], weights,
                                                             mla_q_norm[l], mla_kv_norm[l], cos, sin, n_lat)
        y_mla, y_mla_c = _attention(q, k, v, seq, not last)
        o_gla = _gla(gq, gk, gv, glog, n_lat)
        filt_w = (hy_f_w1[l], hy_f_b1[l], hy_f_w2[l], hy_f_b2[l], hy_f_w3[l], hy_f_b3[l])
        hspec = _hyena_filter_spectrum(seq, *filt_w)
        y_hy = _hyena_latent(z_hy, hspec, hy_short_w[l], hy_short_b[l], hy_bias[l])
        y_hy_c = None if last else _hyena_ctx(z_hy, filt_w, hy_short_w[l], hy_short_b[l], hy_bias[l])
        xc = _merge_mlp(x_lat, x_ctx, ctx_blk, mod[l], y_mla, y_mla_c, o_gla, gr, y_hy, y_hy_c, z_gate,
                        gla_out_norm[l], w_o_mla[l], w_o_gla[l], w_o_hy[l], w_out[l], norm2_g[l], ff_w1[l], ff_w2[l],
                        final_norm_g, n_tiles, n_lat, last)
        x_lat, x_ctx, ctx_blk = xc, xc, n_lat
    return xc
```

```python
import functools
import math

import numpy as np
import jax
import jax.numpy as jnp
from jax import lax
from jax.experimental import pallas as pl
from jax.experimental.pallas import tpu as pltpu

F32 = jnp.float32
BF16 = jnp.bfloat16
LOG2E = 1.4426950408889634

D_MODEL = 1024
DEPTH = 2
GRID_W = 64
EPS = 1e-6
MLA_HEADS = 8
MLA_NOPE = 64
MLA_ROPE = 32
MLA_V = 64
MLA_Q_LORA = 256
MLA_KV_LORA = 128
MLA_SCALE = (MLA_NOPE + MLA_ROPE) ** -0.5
ROPE_BASE = 10000.0
GLA_HEADS = 4
GLA_DK = 256
GLA_DV = 512
GLA_HK = GLA_DK // GLA_HEADS
GLA_HV = GLA_DV // GLA_HEADS
GLA_GATE_RANK = 16
GLA_TAU = 16.0
HY_WIDTH = 512
HY_ORDER = 2
HY_SHORT = 3
HY_BANDS = 16
HY_POS_DIM = 1 + 2 * HY_BANDS
HY_FILTER_HIDDEN = 64
HY_FAST_DECAY = 0.3
HY_SLOW_DECAY = 1.5
HY_DECAY_TARGET = 1e-2
D_FF = 4 * D_MODEL
MLA_OUT = MLA_HEADS * MLA_V
IN_SIZES = (MLA_Q_LORA, MLA_KV_LORA, MLA_ROPE, GLA_DK, GLA_DK, GLA_DV, GLA_DV, GLA_GATE_RANK, GLA_GATE_RANK,
            (HY_ORDER + 1) * HY_WIDTH, 3 * D_MODEL)

LANES = 128
SUBLANES = 8
TM = 256
HEAD_SLOT = 128
VMEM_LIMIT = 56 * 1024 * 1024


def _cparams(n_axes, vmem=None):
    return pltpu.CompilerParams(dimension_semantics=("arbitrary",) * n_axes, vmem_limit_bytes=vmem)


def _const_spec(shape):
    nd = len(shape)
    return pl.BlockSpec(shape, lambda *_: (0,) * nd, pipeline_mode=pl.Buffered(1))


def _rms(x):
    return x * lax.rsqrt(jnp.mean(x * x, axis=-1, keepdims=True) + EPS)


def _sigmoid(x):
    return 1.0 / (1.0 + jnp.exp(-x))


def _dot(a, b):
    return jnp.dot(a, b, preferred_element_type=F32)


def _dot_nt(a, b):
    return lax.dot_general(a, b, (((1,), (1,)), ((), ())), preferred_element_type=F32)


def _dot3(a, b):
    a_hi = a.astype(BF16)
    a_lo = (a - a_hi.astype(F32)).astype(BF16)
    b_hi = b.astype(BF16)
    b_lo = (b - b_hi.astype(F32)).astype(BF16)
    m = a.shape[0]
    r = _dot(jnp.concatenate([a_hi, a_lo], axis=0), b_hi)
    return r[:m] + (r[m:] + _dot(a_hi, b_lo))


def _mod_kernel(cc_ref, w_ref, b_ref, o_ref):
    s = cc_ref[...]
    s = s * _sigmoid(s)
    o_ref[0] = _dot3(s, w_ref[0]) + b_ref[0]


def _modulation(cc, ada_w, ada_b):
    tn = 1536
    n6 = ada_w.shape[-1]
    return pl.pallas_call(
        _mod_kernel,
        out_shape=jax.ShapeDtypeStruct((DEPTH, 16, n6), F32),
        grid=(DEPTH, n6 // tn),
        in_specs=[
            pl.BlockSpec((16, D_MODEL), lambda l, j: (0, 0)),
            pl.BlockSpec((1, D_MODEL, tn), lambda l, j: (l, 0, j)),
            pl.BlockSpec((1, 1, tn), lambda l, j: (l, 0, j)),
        ],
        out_specs=pl.BlockSpec((1, 16, tn), lambda l, j: (l, 0, j)),
        compiler_params=_cparams(2),
        name="modulation",
    )(cc, ada_w, ada_b.reshape(DEPTH, 1, n6))


W_A = 768
W_G = 2 * GLA_DK + 2 * GLA_DV


def _token_specs(d, n_lat_tiles, ctx_blk):
    return [pl.BlockSpec((1, TM, d), lambda b, i: (b, jnp.minimum(i, n_lat_tiles - 1), 0)),
            pl.BlockSpec((1, TM, d), lambda b, i: (b, ctx_blk, 0))]


def _token_tile(x_ref, c_ref, n_lat_tiles):
    return jnp.where(pl.program_id(1) < n_lat_tiles, x_ref[0], c_ref[0])


def _inproj_kernel(x_ref, c_ref, mod_ref, g1_ref, wa_ref, wg_ref, wh_ref, wz_ref, qn_ref, kvn_ref, wuq_ref, wukv_ref,
                   wa2_ref, ba_ref, cos_ref, sin_ref,
                   q_out, k_out, v_out, gq_out, gk_out, gv_out, gr_out, glog_out, hy_out, gate_out, *, n_lat_tiles):
    x = _token_tile(x_ref, c_ref, n_lat_tiles)
    shift = mod_ref[0, 0:1, :]
    scale = mod_ref[0, 1:2, :]
    h = (_rms(x) * g1_ref[...] * (1.0 + scale) + shift).astype(BF16)

    za = _dot(h, wa_ref[...])
    cos = cos_ref[...]
    sin = sin_ref[...]

    cqn = (_rms(za[:, 0:256]) * qn_ref[...]).astype(BF16)
    qab = _dot(cqn, wuq_ref[...])
    nq = MLA_HEADS * HEAD_SLOT
    for hd in range(MLA_HEADS):
        sl = slice(hd * HEAD_SLOT, (hd + 1) * HEAD_SLOT)
        qa = qab[:, hd * HEAD_SLOT:(hd + 1) * HEAD_SLOT]
        qb = qab[:, nq + hd * HEAD_SLOT:nq + (hd + 1) * HEAD_SLOT]
        q_out[0, :, sl] = ((qa * cos + qb * sin) * (MLA_SCALE * LOG2E)).astype(BF16)

    ckvn = (_rms(za[:, 256:384]) * kvn_ref[...]).astype(BF16)
    kv = _dot(ckvn, wukv_ref[...])
    krot = za[:, 384:512] * cos + za[:, 512:640] * sin
    for hd in range(MLA_HEADS):
        sl = slice(hd * HEAD_SLOT, (hd + 1) * HEAD_SLOT)
        k_out[0, :, sl] = (kv[:, sl] + krot).astype(BF16)
    ones_hi = (lax.broadcasted_iota(jnp.int32, (1, HEAD_SLOT), 1) >= MLA_V).astype(F32)
    for hd in range(MLA_HEADS):
        sl = slice(hd * HEAD_SLOT, (hd + 1) * HEAD_SLOT)
        v_out[0, :, sl] = (kv[:, nq + hd * HEAD_SLOT:nq + (hd + 1) * HEAD_SLOT] + ones_hi).astype(BF16)

    xg = _dot(za[:, 640:768].astype(BF16), wa2_ref[...]) + ba_ref[...]
    glog_out[0] = (jnp.minimum(xg, 0.0) - jnp.log(1.0 + jnp.exp(-jnp.abs(xg)))) * (1.0 / GLA_TAU)

    zg = _dot(h, wg_ref[...])
    gq_out[0] = (zg[:, 0:GLA_DK] * (GLA_HK ** -0.5)).astype(BF16)
    gk_out[0] = zg[:, GLA_DK:2 * GLA_DK].astype(BF16)
    gv_out[0] = zg[:, 2 * GLA_DK:2 * GLA_DK + GLA_DV].astype(BF16)
    gr = zg[:, 2 * GLA_DK + GLA_DV:]
    gr_out[0] = (gr * _sigmoid(gr)).astype(BF16)

    hy_out[0] = _dot(h, wh_ref[...])
    gate_out[0] = _sigmoid(_dot(h, wz_ref[...])).astype(BF16)


def _rope_partner(w):
    a = MLA_ROPE // 4
    perm = np.concatenate([np.arange(a, 2 * a), np.arange(0, a), np.arange(3 * a, 4 * a), np.arange(2 * a, 3 * a)])
    sign = np.concatenate([-np.ones(a), np.ones(a), -np.ones(a), np.ones(a)]).astype(np.float32)
    return w[:, perm] * sign


def _prep_inproj_weights(w_in, mla_w_uq, mla_w_ukv, gla_w_a2, gla_b_a):
    offs = np.concatenate([[0], np.cumsum(IN_SIZES)])
    seg = [w_in[:, offs[i]:offs[i + 1]] for i in range(len(IN_SIZES))]
    w_cq, w_ckv, w_kr, w_gq, w_gk, w_gv, w_gr, w_af, w_ab, w_hy, w_gate = seg
    d = w_in.shape[0]
    z = lambda n: jnp.zeros((d, n), w_in.dtype)
    kr_tile = jnp.concatenate([z(MLA_NOPE), w_kr, z(HEAD_SLOT - MLA_NOPE - MLA_ROPE)], axis=1)
    krp_tile = jnp.concatenate([z(MLA_NOPE), _rope_partner(w_kr), z(HEAD_SLOT - MLA_NOPE - MLA_ROPE)], axis=1)
    a_tile = jnp.concatenate([w_af, w_ab, z(LANES - 2 * GLA_GATE_RANK)], axis=1)
    wa = jnp.concatenate([w_cq, w_ckv, kr_tile, krp_tile, a_tile], axis=1)
    wg = jnp.concatenate([w_gq, w_gk, w_gv, w_gr], axis=1)

    dh = MLA_NOPE + MLA_ROPE
    zq = lambda n: jnp.zeros((MLA_Q_LORA, n), w_in.dtype)
    plain, partner = [], []
    for hd in range(MLA_HEADS):
        blk = mla_w_uq[:, hd * dh:(hd + 1) * dh]
        plain += [blk, zq(HEAD_SLOT - dh)]
        partner += [zq(MLA_NOPE), _rope_partner(blk[:, MLA_NOPE:]), zq(HEAD_SLOT - dh)]
    wuq = jnp.concatenate(plain + partner, axis=1)

    zk = jnp.zeros((MLA_KV_LORA, HEAD_SLOT - MLA_NOPE), w_in.dtype)
    kcols, vcols = [], []
    for hd in range(MLA_HEADS):
        blk = mla_w_ukv[:, hd * (MLA_NOPE + MLA_V):(hd + 1) * (MLA_NOPE + MLA_V)]
        kcols += [blk[:, :MLA_NOPE], zk]
        vcols += [blk[:, MLA_NOPE:], zk]
    wukv = jnp.concatenate(kcols + vcols, axis=1)

    wa2 = jnp.zeros((LANES, 2 * GLA_DK), w_in.dtype)
    wa2 = wa2.at[0:GLA_GATE_RANK, 0:GLA_DK].set(gla_w_a2[0])
    wa2 = wa2.at[GLA_GATE_RANK:2 * GLA_GATE_RANK, GLA_DK:].set(gla_w_a2[1])
    ba = jnp.concatenate([gla_b_a[0], gla_b_a[1]])[None, :]
    bf = lambda t: t.astype(BF16)
    return bf(wa), bf(wg), bf(w_hy), bf(w_gate), bf(wuq), bf(wukv), bf(wa2), ba


def _rope_tables(seq, ctx_len):
    rows = seq // GRID_W
    row = np.repeat(np.arange(rows, dtype=np.float64), GRID_W)
    col = np.tile(np.arange(GRID_W, dtype=np.float64), rows)
    a = MLA_ROPE // 4
    inv = ROPE_BASE ** (-np.arange(a, dtype=np.float64) / a)
    ang_r = row[:, None] * inv
    ang_c = col[:, None] * inv
    cos32 = np.concatenate([np.cos(ang_r), np.cos(ang_r), np.cos(ang_c), np.cos(ang_c)], axis=1)
    sin32 = np.concatenate([np.sin(ang_r), np.sin(ang_r), np.sin(ang_c), np.sin(ang_c)], axis=1)
    pad_r = HEAD_SLOT - MLA_NOPE - MLA_ROPE
    cos = np.concatenate([np.ones((seq, MLA_NOPE)), cos32, np.zeros((seq, pad_r))], axis=1)
    sin = np.concatenate([np.zeros((seq, MLA_NOPE)), sin32, np.zeros((seq, pad_r))], axis=1)
    cos_c = np.concatenate([np.ones((ctx_len, MLA_NOPE + MLA_ROPE)), np.zeros((ctx_len, pad_r))], axis=1)
    sin_c = np.zeros((ctx_len, HEAD_SLOT))
    return (jnp.asarray(np.concatenate([cos, cos_c], axis=0), F32),
            jnp.asarray(np.concatenate([sin, sin_c], axis=0), F32))


def _inproj(x_lat, x_ctx, ctx_blk, mod_l, g1, weights, q_norm, kv_norm, cos, sin, n_lat_tiles):
    bsz, _, d = x_lat.shape
    nt = n_lat_tiles + 1
    t = nt * TM
    wa, wg, wh, wz, wuq, wukv, wa2, ba = weights
    tile = lambda w: pl.BlockSpec((1, TM, w), lambda b, i: (b, i, 0))
    row = lambda w: pl.BlockSpec((1, w), lambda b, i: (0, 0))
    mod_spec = pl.BlockSpec((1, 6, d), lambda b, i: (jnp.where(i < n_lat_tiles, b, bsz), 0, 0))
    tab = pl.BlockSpec((TM, HEAD_SLOT), lambda b, i: (i, 0))
    nq = MLA_HEADS * HEAD_SLOT
    sds = lambda w, dt: jax.ShapeDtypeStruct((bsz, t, w), dt)
    out_shape = (sds(nq, BF16), sds(nq, BF16), sds(nq, BF16), sds(GLA_DK, BF16), sds(GLA_DK, BF16),
                 sds(GLA_DV, BF16), sds(GLA_DV, BF16), sds(2 * GLA_DK, F32), sds(3 * HY_WIDTH, F32),
                 sds(3 * D_MODEL, BF16))
    out_specs = (tile(nq), tile(nq), tile(nq), tile(GLA_DK), tile(GLA_DK), tile(GLA_DV), tile(GLA_DV),
                 tile(2 * GLA_DK), tile(3 * HY_WIDTH), tile(3 * D_MODEL))
    return pl.pallas_call(
        functools.partial(_inproj_kernel, n_lat_tiles=n_lat_tiles),
        out_shape=out_shape,
        grid=(bsz, nt),
        in_specs=_token_specs(d, n_lat_tiles, ctx_blk) + [
            mod_spec, row(d), _const_spec(wa.shape), _const_spec(wg.shape), _const_spec(wh.shape),
            _const_spec(wz.shape), row(MLA_Q_LORA), row(MLA_KV_LORA), _const_spec(wuq.shape),
            _const_spec(wukv.shape), _const_spec(wa2.shape), row(2 * GLA_DK), tab, tab],
        out_specs=out_specs,
        compiler_params=_cparams(2, VMEM_LIMIT),
        name="inproj",
    )(x_lat, x_ctx, mod_l, g1[None, :], wa, wg, wh, wz, q_norm[None, :], kv_norm[None, :], wuq, wukv, wa2, ba,
      cos, sin)


ATT_TK = 1024


ATT_TQ = 1024


def _attn_kernel(q_ref, k_ref, v_ref, o_ref, m_ref, acc_ref, *, chunks):
    tq = q_ref.shape[1]
    m_ref[...] = jnp.full(m_ref.shape, -jnp.inf, F32)
    acc_ref[...] = jnp.zeros(acc_ref.shape, F32)
    for r0, size in chunks:
        k = k_ref[0, pl.ds(r0, size), :]
        v = v_ref[0, pl.ds(r0, size), :]
        for hd in range(2):
            sl = slice(hd * HEAD_SLOT, (hd + 1) * HEAD_SLOT)
            s = _dot_nt(q_ref[0, :, sl], k[:, sl])
            m_prev = m_ref[hd]
            m_new = jnp.maximum(m_prev, jnp.max(s, axis=1, keepdims=True))
            p = jnp.exp2((s - jnp.concatenate([m_new] * (size // LANES), axis=1)).astype(BF16))
            acc_ref[hd] = jnp.exp2(m_prev - m_new) * acc_ref[hd] + _dot(p, v[:, sl])
            m_ref[hd] = m_new
    a0 = acc_ref[0]
    a1 = acc_ref[1]
    lane = lax.broadcasted_iota(jnp.int32, (tq, HEAD_SLOT), 1)
    o0 = a0 / pltpu.roll(a0, MLA_V, 1)
    o1 = pltpu.roll(a1, MLA_V, 1) / a1
    o_ref[0] = jnp.where(lane < MLA_V, o0, o1).astype(o_ref.dtype)


def _attention(q, k, v, seq, with_ctx_queries):
    bsz, t, _ = q.shape
    ctx_len = t - seq
    assert seq % ATT_TK == 0 and seq % ATT_TQ == 0 and seq % ctx_len == 0 and 2 * MLA_V == HEAD_SLOT
    pair = 2 * HEAD_SLOT
    n_chunks = seq // ATT_TK
    chunks = tuple((j * ATT_TK, ATT_TK) for j in range(n_chunks - 1))
    chunks += (((n_chunks - 1) * ATT_TK, ATT_TK + ctx_len),)
    scratch = lambda tq: [pltpu.VMEM((2, tq, LANES), F32), pltpu.VMEM((2, tq, HEAD_SLOT), F32)]
    y = pl.pallas_call(
        functools.partial(_attn_kernel, chunks=chunks),
        out_shape=jax.ShapeDtypeStruct((bsz, seq, MLA_OUT), BF16),
        grid=(bsz, MLA_HEADS // 2, seq // ATT_TQ),
        in_specs=[
            pl.BlockSpec((1, ATT_TQ, pair), lambda b, hp, i: (b, i, hp)),
            pl.BlockSpec((1, t, pair), lambda b, hp, i: (b, 0, hp)),
            pl.BlockSpec((1, t, pair), lambda b, hp, i: (b, 0, hp)),
        ],
        out_specs=pl.BlockSpec((1, ATT_TQ, HEAD_SLOT), lambda b, hp, i: (b, i, hp)),
        scratch_shapes=scratch(ATT_TQ),
        compiler_params=_cparams(3, VMEM_LIMIT),
        name="mla_attention",
    )(q, k, v)
    if not with_ctx_queries:
        return y, None
    cblk = seq // ctx_len
    ctx_rows = lambda w: pl.BlockSpec((1, ctx_len, w), lambda b, hp: (b, cblk, hp))
    y_ctx = pl.pallas_call(
        functools.partial(_attn_kernel, chunks=((0, ctx_len),)),
        out_shape=jax.ShapeDtypeStruct((bsz, ctx_len, MLA_OUT), BF16),
        grid=(bsz, MLA_HEADS // 2),
        in_specs=[ctx_rows(pair), ctx_rows(pair), ctx_rows(pair)],
        out_specs=pl.BlockSpec((1, ctx_len, HEAD_SLOT), lambda b, hp: (b, 0, hp)),
        scratch_shapes=scratch(ctx_len),
        compiler_params=_cparams(2),
        name="mla_attention_ctx",
    )(q, k, v)
    return y, y_ctx


GLA_LEVELS = int(math.log2(TM))
GLA_SAFE_SPAN = 60.0


def _gla_level_matrices():
    i = np.arange(TM)[:, None]
    t = np.arange(TM)[None, :]
    fwd = [(t <= i)]
    for lv in range(GLA_LEVELS):
        m = TM >> (lv + 1)
        lo = (i // m) * m
        later = ((i // m) % 2) == 1
        q_part = later & (t >= lo) & (t <= i)
        k_part = (~later) & (t > i) & (t <= lo + m - 1)
        fwd.append(q_part | k_part)
    fwd = np.concatenate(fwd, axis=0).astype(np.float32)
    nb = 1 + GLA_LEVELS
    bwd = fwd.reshape(nb, TM, TM)[:, ::-1, ::-1].reshape(nb * TM, TM)
    return np.stack([fwd, bwd])


def _gla_kernel(q_ref, k_ref, v_ref, g_ref, mall_ref, o_ref, s_ref, a_ref):
    d = pl.program_id(1)
    step = pl.program_id(2)

    @pl.when(step == 0)
    def _():
        s_ref[...] = jnp.zeros(s_ref.shape, F32)

    q = q_ref[0].astype(F32)
    k = k_ref[0].astype(F32)
    v = v_ref[0]
    g = g_ref[0]
    g2 = jnp.concatenate([g.astype(BF16), (g - g.astype(BF16).astype(F32)).astype(BF16)], axis=1)
    e2 = _dot(mall_ref[0, 0:TM, :], g2)
    g_cum = e2[:, :GLA_DK] + e2[:, GLA_DK:]
    g_tot = jnp.sum(g, axis=0, keepdims=True)

    row = lax.broadcasted_iota(jnp.int32, (TM, TM), 0)
    col = lax.broadcasted_iota(jnp.int32, (TM, TM), 1)
    lane_head = lax.broadcasted_iota(jnp.int32, (TM, GLA_DK), 1) // GLA_HK
    tok = lax.broadcasted_iota(jnp.int32, (TM, GLA_DK), 0)
    eye = row == col

    def stack_heads(t):
        return jnp.concatenate([jnp.where(lane_head == hd, t, 0.0) for hd in range(GLA_HEADS)], axis=0).astype(BF16)

    span = jnp.max(-g_tot)

    @pl.when(span < GLA_SAFE_SPAN)
    def _():
        res = _dot_nt(stack_heads(q * jnp.exp(g_cum)), (k * jnp.exp(-g_cum)).astype(BF16))
        seen = (col - row) * (1 - 2 * d) <= 0
        for hd in range(GLA_HEADS):
            a_ref[hd] = jnp.where(seen, res[hd * TM:(hd + 1) * TM], 0.0)

    @pl.when(span >= GLA_SAFE_SPAN)
    def _():
        e2l = _dot(mall_ref[0, TM:, :], g2)
        e_lv = e2l[:, :GLA_DK] + e2l[:, GLA_DK:]
        res = _dot_nt(stack_heads(q), k.astype(BF16))
        for hd in range(GLA_HEADS):
            a_ref[hd] = jnp.where(eye, res[hd * TM:(hd + 1) * TM], 0.0)
        for lv in range(GLA_LEVELS):
            m = TM >> (lv + 1)
            w = jnp.exp(e_lv[lv * TM:(lv + 1) * TM])
            bit = (tok // m) % 2
            q_act = bit != d
            qt = jnp.where(q_act, q * w, 0.0)
            kt = jnp.where(q_act, 0.0, k * w).astype(BF16)
            res = _dot_nt(stack_heads(qt), kt)
            if m == TM // 2:
                for hd in range(GLA_HEADS):
                    a_ref[hd] += res[hd * TM:(hd + 1) * TM]
            else:
                same = (row // (2 * m)) == (col // (2 * m))
                for hd in range(GLA_HEADS):
                    a_ref[hd] += jnp.where(same, res[hd * TM:(hd + 1) * TM], 0.0)

    s_old = s_ref[...]
    o_inter = _dot((q * jnp.exp(g_cum)).astype(BF16), s_old.astype(BF16))
    for hd in range(GLA_HEADS):
        sl = slice(hd * GLA_HV, (hd + 1) * GLA_HV)
        o_intra = _dot(a_ref[hd].astype(BF16), v[:, sl])
        o_ref[0, 0, :, sl] = (o_intra + o_inter[:, sl]).astype(o_ref.dtype)

    kdec_t = (k * jnp.exp(g_tot - g_cum)).T.astype(BF16)
    upd = _dot(kdec_t, v)
    same_head = (lax.broadcasted_iota(jnp.int32, (GLA_DK, GLA_DV), 0) // GLA_HK
                 == lax.broadcasted_iota(jnp.int32, (GLA_DK, GLA_DV), 1) // GLA_HV)
    dec_col = jnp.sum(jnp.where(eye, jnp.broadcast_to(jnp.exp(g_tot), (TM, GLA_DK)), 0.0), axis=1, keepdims=True)
    s_ref[...] = dec_col * s_old + jnp.where(same_head, upd, 0.0)


def _gla(gq, gk, gv, glog, n_lat_tiles):
    assert GLA_DK == TM
    bsz, t, _ = gq.shape
    nt = t // TM
    mall = jnp.asarray(_gla_level_matrices(), dtype=BF16)

    def tile_idx(d, s):
        return jnp.where(s == 0, n_lat_tiles, jnp.where(d == 0, s - 1, n_lat_tiles - s))

    tok = lambda w: pl.BlockSpec((1, TM, w), lambda b, d, s: (b, tile_idx(d, s), 0))
    return pl.pallas_call(
        _gla_kernel,
        out_shape=jax.ShapeDtypeStruct((bsz, 2, t, GLA_DV), BF16),
        grid=(bsz, 2, nt),
        in_specs=[tok(GLA_DK), tok(GLA_DK), tok(GLA_DV),
                  pl.BlockSpec((1, TM, GLA_DK), lambda b, d, s: (b, tile_idx(d, s), d)),
                  pl.BlockSpec((1,) + mall.shape[1:], lambda b, d, s: (d, 0, 0))],
        out_specs=pl.BlockSpec((1, 1, TM, GLA_DV), lambda b, d, s: (b, d, tile_idx(d, s), 0)),
        scratch_shapes=[pltpu.VMEM((GLA_DK, GLA_DV), F32), pltpu.VMEM((GLA_HEADS, TM, TM), F32)],
        compiler_params=_cparams(3),
        name="gla_scan",
    )(gq, gk, gv, glog, mall)


HY_N1 = 64
HY_N2 = 128
HY_SLAB = SUBLANES
HY_CB = 256
HY_KG = 16
HY_SPEC_CB = 128
HY_UNROLL = 8


def _hyena_dft_constants(seq):
    n = 2 * seq
    assert n == HY_N1 * HY_N2
    nh = HY_N1 // 2
    kh = HY_N2 // 2
    eye = np.eye(HY_SLAB)
    k1 = np.arange(HY_N1)
    th = 2 * np.pi * np.outer(k1 + 0.5, np.arange(nh)) / HY_N1
    fwd1 = np.concatenate([np.kron(np.cos(th), eye), np.kron(-np.sin(th), eye)], axis=0)
    inv1 = (2.0 / n) * np.concatenate([np.kron(np.cos(th).T, eye), np.kron(-np.sin(th).T, eye)], axis=1)
    nlo = np.arange(HY_N2)
    k2 = np.arange(kh)
    ph = 2 * np.pi * (k2[None, :, None] * nlo[None, None, :] / HY_N2
                      + (k1[:, None, None] + 0.5) * nlo[None, None, :] / n)
    c, s = np.cos(ph), np.sin(ph)
    fwd2 = np.concatenate([np.concatenate([c, s], axis=2), np.concatenate([-s, c], axis=2)], axis=1)
    ct, st = c.transpose(0, 2, 1), s.transpose(0, 2, 1)
    inv2 = np.concatenate([np.concatenate([ct, -st], axis=2), np.concatenate([st, ct], axis=2)], axis=1)
    return fwd1, fwd2, inv2, inv1


def _hyena_ctx_dft_constants(ctx_len):
    n = 2 * ctx_len
    th = 2 * np.pi * np.outer(np.arange(ctx_len) + 0.5, np.arange(ctx_len)) / n
    fwd = np.concatenate([np.cos(th), -np.sin(th)], axis=0)
    inv = (2.0 / n) * np.concatenate([np.cos(th).T, -np.sin(th).T], axis=1)
    return fwd, inv


def _hyena_features(length):
    pos = np.arange(length, dtype=np.float64)
    t = pos / max(length - 1, 1)
    f = np.linspace(1e-4, HY_BANDS - 1, HY_BANDS)
    ang = (2.0 * math.pi / length) * pos[:, None] * f
    feat = np.concatenate([t[:, None], np.cos(ang), np.sin(ang)], axis=-1)
    return jnp.asarray(np.pad(feat, ((0, 0), (0, LANES - HY_POS_DIM))), F32)


def _hy_filter_kernel(feat_ref, w1_ref, b1_ref, w2_ref, b2_ref, w3_ref, b3_ref, absd_ref, h_ref, s_ref):
    i = pl.program_id(0)
    feat = feat_ref[...]
    hp = _dot3
    hdn = jnp.sin(hp(feat, w1_ref[...]) + b1_ref[...])
    hdn = jnp.sin(hp(hdn, w2_ref[...]) + b2_ref[...])
    h = hp(hdn, w3_ref[...]) + b3_ref[...]
    window = jnp.exp(-feat[:, 0:1] * absd_ref[...])
    h = h * jnp.concatenate([window] * (2 * HY_ORDER), axis=1)
    h_ref[...] = h

    @pl.when(i == 0)
    def _():
        s_ref[...] = jnp.zeros(s_ref.shape, F32)

    s_ref[...] += jnp.sum(jnp.abs(h), axis=0, keepdims=True)


def _hyena_filters_raw(length, filt_w):
    w1, b1, w2, b2, w3, b3 = filt_w
    nf = 2 * HY_ORDER * HY_WIDTH
    tr = min(length, 512)
    deltas = np.linspace(math.log(HY_DECAY_TARGET) / HY_FAST_DECAY, math.log(HY_DECAY_TARGET) / HY_SLOW_DECAY,
                         HY_WIDTH, dtype=np.float32)
    absd = jnp.asarray(np.abs(deltas))[None, :]
    w1p = jnp.pad(w1, ((0, LANES - HY_POS_DIM), (0, 0)))
    full = lambda shp: pl.BlockSpec(shp, lambda i: (0,) * len(shp))
    return pl.pallas_call(
        _hy_filter_kernel,
        out_shape=(jax.ShapeDtypeStruct((length, nf), F32), jax.ShapeDtypeStruct((1, nf), F32)),
        grid=(length // tr,),
        in_specs=[pl.BlockSpec((tr, LANES), lambda i: (i, 0)), full((LANES, HY_FILTER_HIDDEN)),
                  full((1, HY_FILTER_HIDDEN)), full((HY_FILTER_HIDDEN, HY_FILTER_HIDDEN)), full((1, HY_FILTER_HIDDEN)),
                  full((HY_FILTER_HIDDEN, nf)), full((1, nf)), full((1, HY_WIDTH))],
        out_specs=(pl.BlockSpec((tr, nf), lambda i: (i, 0)), full((1, nf))),
        compiler_params=_cparams(1),
        name="hyena_filter_mlp",
    )(_hyena_features(length), w1p, b1[None, :], w2, b2[None, :], w3, b3[None, :], absd)


def _dot_split(m, x):
    x_hi = x.astype(BF16)
    x_lo = (x - x_hi.astype(F32)).astype(BF16)
    n = x.shape[1]
    r = _dot(m, jnp.concatenate([x_hi, x_lo], axis=1))
    return r[:, :n] + r[:, n:]


def _odft_stage1(src_at, mm, s_re, s_im, unroll=2):
    nk = s_re.shape[0]
    half = nk * HY_SLAB

    def body(j, carry):
        r0 = pl.multiple_of(j * HY_SLAB, HY_SLAB)
        slab = src_at(r0)
        cb = slab.shape[-1]
        res = mm(slab.reshape(-1, cb))
        s_re[:, pl.ds(r0, HY_SLAB), :] = res[:half].reshape(nk, HY_SLAB, cb)
        s_im[:, pl.ds(r0, HY_SLAB), :] = res[half:].reshape(nk, HY_SLAB, cb)
        return carry

    lax.fori_loop(0, HY_N2 // HY_SLAB, body, 0, unroll=unroll)


def _hy_spectrum_kernel(hf_ref, hb_ref, sf_ref, sb_ref, fwd1_ref, fwd2_ref, o_ref, s_re, s_im):
    kh = HY_N2 // 2
    mm1 = lambda x: _dot_split(fwd1_ref[...], x)

    def middle(sign):
        def body(k1, carry):
            a = jnp.concatenate([s_re[k1], s_im[k1]], axis=0)
            x = _dot_split(fwd2_ref[k1], a)
            if sign is None:
                o_ref[0, k1] = x[:kh]
                o_ref[1, k1] = x[kh:]
            else:
                inv_norm = 1.0 / (sf_ref[...] + sb_ref[...])
                o_ref[0, k1] = (o_ref[0, k1] + x[:kh]) * inv_norm
                o_ref[1, k1] = (o_ref[1, k1] - x[kh:]) * inv_norm
            return carry
        lax.fori_loop(0, HY_N1, body, 0, unroll=8)

    _odft_stage1(lambda r0: hf_ref[:, pl.ds(r0, HY_SLAB), :], mm1, s_re, s_im, unroll=4)
    middle(None)

    def bwd_slab(r0):
        slab = hb_ref[:, pl.ds(r0, HY_SLAB), :]
        nhi = lax.broadcasted_iota(jnp.int32, slab.shape, 0)
        r = lax.broadcasted_iota(jnp.int32, slab.shape, 1)
        return jnp.where((nhi == 0) & (r + r0 == 0), 0.0, slab)

    _odft_stage1(bwd_slab, mm1, s_re, s_im, unroll=4)
    middle(-1)


def _hyena_filter_spectrum(seq, w1, b1, w2, b2, w3, b3):
    h_raw, s = _hyena_filters_raw(seq, (w1, b1, w2, b2, w3, b3))
    nh = HY_N1 // 2
    nc = HY_ORDER * HY_WIDTH
    h3 = h_raw.reshape(nh, HY_N2, 2 * nc)
    fwd1, fwd2, _, _ = _hyena_dft_constants(seq)
    fwd1 = jnp.asarray(fwd1, BF16)
    fwd2 = jnp.asarray(fwd2, BF16)
    scb = HY_SPEC_CB
    ncb = nc // scb
    return pl.pallas_call(
        _hy_spectrum_kernel,
        out_shape=jax.ShapeDtypeStruct((2, HY_N1, HY_N2 // 2, nc), F32),
        grid=(ncb,),
        in_specs=[pl.BlockSpec((nh, HY_N2, scb), lambda c: (0, 0, c)),
                  pl.BlockSpec((nh, HY_N2, scb), lambda c: (0, 0, ncb + c)),
                  pl.BlockSpec((1, scb), lambda c: (0, c)),
                  pl.BlockSpec((1, scb), lambda c: (0, ncb + c)),
                  _const_spec(fwd1.shape), _const_spec(fwd2.shape)],
        out_specs=pl.BlockSpec((2, HY_N1, HY_N2 // 2, scb), lambda c: (0, 0, 0, c)),
        scratch_shapes=[pltpu.VMEM((HY_N1, HY_N2, scb), F32), pltpu.VMEM((HY_N1, HY_N2, scb), F32)],
        compiler_params=_cparams(1, VMEM_LIMIT),
        name="hyena_filter_spectrum",
    )(h3, h3, s, s, fwd1, fwd2)


def _short_conv_chunk(ref, c, n_chunks, w_ref, b_ref):
    per = TM // HY_N2
    cur = ref[0, pl.ds(per * c, per)]
    cb = cur.shape[-1]
    cur = cur.reshape(TM, cb)
    prev = ref[0, jnp.maximum(per * c - 1, 0), pl.ds(HY_N2 - SUBLANES, SUBLANES), :][SUBLANES - 1:SUBLANES]
    nxt = ref[0, jnp.minimum(per * c + per, per * n_chunks - 1), pl.ds(0, SUBLANES), :][0:1]
    prev = jnp.where(c > 0, prev, 0.0)
    nxt = jnp.where(c < n_chunks - 1, nxt, 0.0)
    rowi = lax.broadcasted_iota(jnp.int32, (TM, cb), 0)
    dn = jnp.where(rowi == 0, prev, pltpu.roll(cur, 1, 0))
    up = jnp.where(rowi == TM - 1, nxt, pltpu.roll(cur, TM - 1, 0))
    return b_ref[...] + w_ref[0:1, :] * dn + w_ref[1:2, :] * cur + w_ref[2:3, :] * up


def _hy_conv_kernel(*refs, conv_y):
    if conv_y:
        (y_ref, g_ref, h_ref, bias_ref, wy_ref, by_ref, wg_ref, bg_ref, fwd1_ref, fwd2_ref, inv2_ref, inv1_ref,
         o_ref, s_re, s_im, gs_ref, us_ref) = refs
    else:
        (y_ref, g_ref, h_ref, bias_ref, wg_ref, bg_ref, fwd1_ref, fwd2_ref, inv2_ref, inv1_ref,
         o_ref, s_re, s_im, gs_ref) = refs
    grp = pl.program_id(2)
    last = pl.num_programs(2) - 1
    nh = HY_N1 // 2
    per = TM // HY_N2
    n_chunks = nh // per
    cb = o_ref.shape[-1]
    kh = HY_N2 // 2

    @pl.when(grp == 0)
    def _():
        def pre(c, carry):
            gs_ref[pl.ds(per * c, per)] = _short_conv_chunk(g_ref, c, n_chunks, wg_ref, bg_ref).reshape(per, HY_N2, cb)
            if conv_y:
                us_ref[pl.ds(per * c, per)] = _short_conv_chunk(y_ref, c, n_chunks, wy_ref, by_ref).reshape(
                    per, HY_N2, cb)
            return carry
        lax.fori_loop(0, n_chunks, pre, 0)

    if conv_y:
        u_at = lambda r0: us_ref[:, pl.ds(r0, HY_SLAB), :]
    else:
        u_at = lambda r0: y_ref[0, :, pl.ds(r0, HY_SLAB), :]

    _odft_stage1(u_at, lambda x: _dot(fwd1_ref[0], x.astype(BF16)), s_re, s_im, unroll=HY_UNROLL)

    def middle(k1, carry):
        a = jnp.concatenate([s_re[k1], s_im[k1]], axis=0).astype(BF16)
        x = _dot(fwd2_ref[k1], a)
        xr, xi = x[:kh], x[kh:]
        hr, hi = h_ref[0, k1], h_ref[1, k1]
        y = jnp.concatenate([xr * hr - xi * hi, xr * hi + xi * hr], axis=0).astype(BF16)
        bm = _dot(inv2_ref[k1], y)
        s_re[k1] = bm[:HY_N2]
        s_im[k1] = bm[HY_N2:]
        return carry

    lax.fori_loop(0, HY_KG, middle, 0, unroll=2 * HY_UNROLL)

    def partial_conv(r0):
        slab = jnp.concatenate([s_re[:, pl.ds(r0, HY_SLAB), :].reshape(HY_KG * HY_SLAB, cb),
                                s_im[:, pl.ds(r0, HY_SLAB), :].reshape(HY_KG * HY_SLAB, cb)], axis=0).astype(BF16)
        return _dot(inv1_ref[0], slab).reshape(nh, HY_SLAB, cb)

    def post_loop(fn):
        def post(j, carry):
            r0 = pl.multiple_of(j * HY_SLAB, HY_SLAB)
            o_ref[0, :, pl.ds(r0, HY_SLAB), :] = fn(r0, partial_conv(r0))
            return carry
        lax.fori_loop(0, HY_N2 // HY_SLAB, post, 0, unroll=HY_UNROLL)

    @pl.when(grp == 0)
    def _():
        post_loop(lambda r0, part: part)

    @pl.when((grp > 0) & (grp < last))
    def _():
        post_loop(lambda r0, part: o_ref[0, :, pl.ds(r0, HY_SLAB), :] + part)

    @pl.when(grp == last)
    def _():
        post_loop(lambda r0, part: gs_ref[:, pl.ds(r0, HY_SLAB), :]
                  * (o_ref[0, :, pl.ds(r0, HY_SLAB), :] + part + bias_ref[...] * u_at(r0)))


def _hyena_order(y4, y_col0, z4, gate_col0, hspec, order, hy_bias, short_w, short_b, consts, conv_y):
    bsz = z4.shape[0]
    nh = HY_N1 // 2
    ncb = HY_WIDTH // HY_CB
    ngrp = HY_N1 // HY_KG
    fwd1, fwd2, inv2, inv1 = consts
    blk4 = lambda off: pl.BlockSpec((1, nh, HY_N2, HY_CB), lambda c, b, g: (b, 0, 0, off + c))
    rowspec = lambda rows, off: pl.BlockSpec((rows, HY_CB), lambda c, b, g: (0, off + c))
    grouped = lambda shp: pl.BlockSpec(shp, lambda c, b, g: (g, 0, 0))
    in_specs = [blk4(y_col0), blk4(gate_col0),
                pl.BlockSpec((2, HY_KG, HY_N2 // 2, HY_CB), lambda c, b, g: (0, g, 0, order * ncb + c)),
                rowspec(1, 0)]
    args = [y4, z4, hspec, hy_bias.reshape(1, -1)]
    if conv_y:
        in_specs += [rowspec(HY_SHORT, y_col0), rowspec(1, y_col0)]
        args += [short_w, short_b[None, :]]
    in_specs += [rowspec(HY_SHORT, gate_col0), rowspec(1, gate_col0)]
    args += [short_w, short_b[None, :]]
    in_specs += [grouped((1,) + fwd1.shape[1:]), grouped((HY_KG,) + fwd2.shape[1:]),
                 grouped((HY_KG,) + inv2.shape[1:]), grouped((1,) + inv1.shape[1:])]
    args += [fwd1, fwd2, inv2, inv1]
    scratch = [pltpu.VMEM((HY_KG, HY_N2, HY_CB), F32), pltpu.VMEM((HY_KG, HY_N2, HY_CB), F32),
               pltpu.VMEM((nh, HY_N2, HY_CB), F32)]
    if conv_y:
        scratch.append(pltpu.VMEM((nh, HY_N2, HY_CB), F32))
    return pl.pallas_call(
        functools.partial(_hy_conv_kernel, conv_y=conv_y),
        out_shape=jax.ShapeDtypeStruct((bsz, nh, HY_N2, HY_WIDTH), F32),
        grid=(ncb, bsz, ngrp),
        in_specs=in_specs,
        out_specs=pl.BlockSpec((1, nh, HY_N2, HY_CB), lambda c, b, g: (b, 0, 0, c)),
        scratch_shapes=scratch,
        compiler_params=_cparams(3, VMEM_LIMIT),
        name="hyena_conv%d" % order,
    )(*args)


def _hyena_latent(z_hy, hspec, short_w, short_b, hy_bias):
    bsz, t, _ = z_hy.shape
    seq = HY_N1 * HY_N2 // 2
    fwd1, fwd2, inv2, inv1 = _hyena_dft_constants(seq)
    ngrp = HY_N1 // HY_KG
    rows = HY_KG * HY_SLAB
    fwd1 = fwd1.reshape(2, ngrp, rows, -1).transpose(1, 0, 2, 3).reshape(ngrp, 2 * rows, -1)
    inv1 = inv1.reshape(-1, 2, ngrp, rows).transpose(2, 0, 1, 3).reshape(ngrp, -1, 2 * rows)
    consts = tuple(jnp.asarray(m, BF16) for m in (fwd1, fwd2, inv2, inv1))
    z4 = z_hy.reshape(bsz, t // HY_N2, HY_N2, 3 * HY_WIDTH)
    ncb = HY_WIDTH // HY_CB
    nh = seq // HY_N2
    y1 = _hyena_order(z4, 2 * ncb, z4, 0, hspec, 0, hy_bias[0], short_w, short_b, consts, True)
    y2 = _hyena_order(y1, 0, z4, ncb, hspec, 1, hy_bias[1], short_w, short_b, consts, False)
    return y2.reshape(bsz, seq, HY_WIDTH)


def _hy_ctx_spectrum_kernel(h_ref, s_ref, fwd_ref, o_ref):
    lc = h_ref.shape[0]
    nc = HY_ORDER * HY_WIDTH
    hp = _dot3
    h = h_ref[...]
    rowi = lax.broadcasted_iota(jnp.int32, (lc, nc), 0)
    xf = hp(fwd_ref[...], h[:, :nc])
    xb = hp(fwd_ref[...], jnp.where(rowi == 0, 0.0, h[:, nc:]))
    inv_norm = 1.0 / (s_ref[:, :nc] + s_ref[:, nc:])
    o_ref[0] = (xf[:lc] + xb[:lc]) * inv_norm
    o_ref[1] = (xf[lc:] - xb[lc:]) * inv_norm


def _hy_ctx_conv_kernel(x1_ref, x2_ref, v_ref, h_ref, bias_ref, w_ref, b_ref, fwd_ref, inv_ref, o_ref):
    lc = o_ref.shape[1]
    rowi = lax.broadcasted_iota(jnp.int32, (lc, HY_WIDTH), 0)

    def short(ref, part):
        cur = ref[0].reshape(lc, HY_WIDTH)
        sl = slice(part * HY_WIDTH, (part + 1) * HY_WIDTH)
        dn = jnp.where(rowi == 0, 0.0, pltpu.roll(cur, 1, 0))
        up = jnp.where(rowi == lc - 1, 0.0, pltpu.roll(cur, lc - 1, 0))
        return b_ref[:, sl] + w_ref[0:1, sl] * dn + w_ref[1:2, sl] * cur + w_ref[2:3, sl] * up

    y = short(v_ref, 2)
    for order, gref in enumerate((x1_ref, x2_ref)):
        sl = slice(order * HY_WIDTH, (order + 1) * HY_WIDTH)
        x = _dot(fwd_ref[...], y.astype(BF16))
        xr, xi = x[:lc], x[lc:]
        hr, hi = h_ref[0, :, sl], h_ref[1, :, sl]
        prod = jnp.concatenate([xr * hr - xi * hi, xr * hi + xi * hr], axis=0).astype(BF16)
        conv = _dot(inv_ref[...], prod)
        y = short(gref, order) * (conv + bias_ref[order:order + 1, :] * y)
    o_ref[0] = y


def _hyena_ctx(z_hy, filt_w, short_w, short_b, hy_bias):
    bsz, t, _ = z_hy.shape
    seq = HY_N1 * HY_N2 // 2
    lc = t - seq
    per = lc // HY_N2
    h_raw, s = _hyena_filters_raw(lc, filt_w)
    fwd, inv = _hyena_ctx_dft_constants(lc)
    nc = HY_ORDER * HY_WIDTH
    full = lambda shp: pl.BlockSpec(shp, lambda *_: (0,) * len(shp))
    hspec = pl.pallas_call(
        _hy_ctx_spectrum_kernel,
        out_shape=jax.ShapeDtypeStruct((2, lc, nc), F32),
        grid=(1,),
        in_specs=[full(h_raw.shape), full(s.shape), full(fwd.shape)],
        out_specs=full((2, lc, nc)),
        compiler_params=_cparams(1),
        name="hyena_ctx_spectrum",
    )(h_raw, s, jnp.asarray(fwd, F32))
    z4 = z_hy.reshape(bsz, t // HY_N2, HY_N2, 3 * HY_WIDTH)
    blk = lambda part: pl.BlockSpec((1, per, HY_N2, HY_WIDTH), lambda b: (b, seq // lc, 0, part))
    return pl.pallas_call(
        _hy_ctx_conv_kernel,
        out_shape=jax.ShapeDtypeStruct((bsz, lc, HY_WIDTH), F32),
        grid=(bsz,),
        in_specs=[blk(0), blk(1), blk(2), full((2, lc, nc)), full((HY_ORDER, HY_WIDTH)),
                  full((HY_SHORT, 3 * HY_WIDTH)), full((1, 3 * HY_WIDTH)), full(fwd.shape), full(inv.shape)],
        out_specs=pl.BlockSpec((1, lc, HY_WIDTH), lambda b: (b, 0, 0)),
        compiler_params=_cparams(1),
        name="hyena_ctx_conv",
    )(z4, z4, z4, hspec, hy_bias, short_w, short_b[None, :], jnp.asarray(fwd, BF16), jnp.asarray(inv, BF16))


FF_CHUNK = 1024


def _merge_mlp_kernel(x_ref, c_ref, mod_ref, ymla_ref, ymlac_ref, of_ref, ob_ref, gr_ref, yhy_ref, yhyc_ref, zg_ref, on_ref,
                      wm_ref, wgl_ref, wh_ref, wo_ref, g2_ref, w1_ref, w2_ref, fg_ref, o_ref, *, final, n_lat_tiles):
    o = of_ref[0, 0].astype(F32) + ob_ref[0, 0].astype(F32)
    silu = gr_ref[0].astype(F32)
    parts = []
    for hd in range(GLA_HEADS):
        sl = slice(hd * GLA_HV, (hd + 1) * GLA_HV)
        parts.append((_rms(o[:, sl]) * on_ref[...] * silu[:, sl]).astype(BF16))
    y_gla = jnp.concatenate(parts, axis=1)
    zg = zg_ref[0].astype(F32)
    d = x_ref.shape[-1]
    m = zg[:, 0:d] * _dot(_token_tile(ymla_ref, ymlac_ref, n_lat_tiles), wm_ref[...])
    m = m + zg[:, d:2 * d] * _dot(y_gla, wgl_ref[...])
    m = m + zg[:, 2 * d:3 * d] * _dot(_token_tile(yhy_ref, yhyc_ref, n_lat_tiles).astype(BF16), wh_ref[...])
    out = _dot(m.astype(BF16), wo_ref[...])
    x = _token_tile(x_ref, c_ref, n_lat_tiles) + mod_ref[0, 2:3, :] * out

    h = (_rms(x) * g2_ref[...] * (1.0 + mod_ref[0, 4:5, :]) + mod_ref[0, 3:4, :]).astype(BF16)
    acc = jnp.zeros(x.shape, F32)
    for j in range(w1_ref.shape[1] // FF_CHUNK):
        a = jnp.maximum(_dot(h, w1_ref[:, j * FF_CHUNK:(j + 1) * FF_CHUNK]), 0.0)
        acc = acc + _dot((a * a).astype(BF16), w2_ref[j * FF_CHUNK:(j + 1) * FF_CHUNK, :])
    xn = x + mod_ref[0, 5:6, :] * acc
    if final:
        xn = _rms(xn) * fg_ref[...]
    o_ref[0] = xn


def _mod_spec(d, n_lat_tiles, bsz):
    return pl.BlockSpec((1, 6, d), lambda b, i: (jnp.where(i < n_lat_tiles, b, bsz), 0, 0))


def _merge_mlp(x_lat, x_ctx, ctx_blk, mod_l, y_mla, y_mla_ctx, o_gla, gr, y_hy, y_hy_ctx, gate, out_norm,
               w_o_mla, w_o_gla, w_o_hy, w_out, g2, w1, w2, final_g, n_tiles, n_lat_tiles, final):
    bsz, _, d = x_lat.shape
    if y_mla_ctx is None:
        y_mla_ctx, y_hy_ctx = y_mla, y_hy
    tile = lambda w: pl.BlockSpec((1, TM, w), lambda b, i: (b, i, 0))
    row = lambda w: pl.BlockSpec((1, w), lambda b, i: (0, 0))
    dirspec = lambda dr: pl.BlockSpec((1, 1, TM, GLA_DV), lambda b, i: (b, dr, i, 0))
    bf = lambda w: w.astype(BF16)
    return pl.pallas_call(
        functools.partial(_merge_mlp_kernel, final=final, n_lat_tiles=n_lat_tiles),
        out_shape=jax.ShapeDtypeStruct((bsz, n_tiles * TM, d), F32),
        grid=(bsz, n_tiles),
        in_specs=_token_specs(d, n_lat_tiles, ctx_blk) + [_mod_spec(d, n_lat_tiles, bsz)]
        + _token_specs(MLA_OUT, n_lat_tiles, 0) + [dirspec(0), dirspec(1), tile(GLA_DV)]
        + _token_specs(HY_WIDTH, n_lat_tiles, 0) + [
            tile(3 * d), row(GLA_HV),
            _const_spec(w_o_mla.shape), _const_spec(w_o_gla.shape), _const_spec(w_o_hy.shape),
            _const_spec(w_out.shape), row(d), _const_spec(w1.shape), _const_spec(w2.shape), row(d)],
        out_specs=tile(d),
        compiler_params=_cparams(2, VMEM_LIMIT),
        name="merge_mlp",
    )(x_lat, x_ctx, mod_l, y_mla, y_mla_ctx, o_gla, o_gla, gr, y_hy, y_hy_ctx, gate, out_norm[None, :],
      bf(w_o_mla), bf(w_o_gla), bf(w_o_hy), bf(w_out), g2[None, :], bf(w1), bf(w2), final_g[None, :])


def kernel(x, c, ctx, c_ctx, ada_w, ada_b, norm1_g, norm2_g, w_in, mla_q_norm, mla_w_uq, mla_kv_norm, mla_w_ukv, gla_w_a2, gla_b_a, gla_out_norm, hy_short_w, hy_short_b, hy_f_w1, hy_f_b1, hy_f_w2, hy_f_b2, hy_f_w3, hy_f_b3, hy_bias, w_o_mla, w_o_gla, w_o_hy, w_out, ff_w1, ff_w2, final_norm_g):
    bsz, seq, d = x.shape
    ctx_len = ctx.shape[1]
    n_lat = seq // TM
    assert ctx_len == TM and seq % TM == 0
    n_all = n_lat + 1
    x_lat, x_ctx, ctx_blk = x, ctx, 0
    cc = jnp.zeros((16, d), F32).at[:bsz].set(c).at[bsz].set(c_ctx)
    mod = _modulation(cc, ada_w, ada_b).reshape(DEPTH, 16, 6, d)
    cos, sin = _rope_tables(seq, ctx_len)
    for l in range(DEPTH):
        last = l == DEPTH - 1
        n_tiles = n_lat if last else n_all
        weights = _prep_inproj_weights(w_in[l], mla_w_uq[l], mla_w_ukv[l], gla_w_a2[l], gla_b_a[l])
        q, k, v, gq, gk, gv, gr, glog, z_hy, z_gate = _inproj(x_lat, x_ctx, ctx_blk, mod[l], norm1_g[l], weights,
                                                             mla_q_norm[l], mla_kv_norm[l], cos, sin, n_lat)
        y_mla, y_mla_c = _attention(q, k, v, seq, not last)
        o_gla = _gla(gq, gk, gv, glog, n_lat)
        filt_w = (hy_f_w1[l], hy_f_b1[l], hy_f_w2[l], hy_f_b2[l], hy_f_w3[l], hy_f_b3[l])
        hspec = _hyena_filter_spectrum(seq, *filt_w)
        y_hy = _hyena_latent(z_hy, hspec, hy_short_w[l], hy_short_b[l], hy_bias[l])
        y_hy_c = None if last else _hyena_ctx(z_hy, filt_w, hy_short_w[l], hy_short_b[l], hy_bias[l])
        xc = _merge_mlp(x_lat, x_ctx, ctx_blk, mod[l], y_mla, y_mla_c, o_gla, gr, y_hy, y_hy_c, z_gate,
                        gla_out_norm[l], w_o_mla[l], w_o_gla[l], w_o_hy[l], w_out[l], norm2_g[l], ff_w1[l], ff_w2[l],
                        final_norm_g, n_tiles, n_lat, last)
        x_lat, x_ctx, ctx_blk = xc, xc, n_lat
    return xc
```

```python
import functools
import math

import numpy as np
import jax
import jax.numpy as jnp
from jax import lax
from jax.experimental import pallas as pl
from jax.experimental.pallas import tpu as pltpu

F32 = jnp.float32
BF16 = jnp.bfloat16
LOG2E = 1.4426950408889634

D_MODEL = 1024
DEPTH = 2
GRID_W = 64
EPS = 1e-6
MLA_HEADS = 8
MLA_NOPE = 64
MLA_ROPE = 32
MLA_V = 64
MLA_Q_LORA = 256
MLA_KV_LORA = 128
MLA_SCALE = (MLA_NOPE + MLA_ROPE) ** -0.5
ROPE_BASE = 10000.0
GLA_HEADS = 4
GLA_DK = 256
GLA_DV = 512
GLA_HK = GLA_DK // GLA_HEADS
GLA_HV = GLA_DV // GLA_HEADS
GLA_GATE_RANK = 16
GLA_TAU = 16.0
HY_WIDTH = 512
HY_ORDER = 2
HY_SHORT = 3
HY_BANDS = 16
HY_POS_DIM = 1 + 2 * HY_BANDS
HY_FILTER_HIDDEN = 64
HY_FAST_DECAY = 0.3
HY_SLOW_DECAY = 1.5
HY_DECAY_TARGET = 1e-2
D_FF = 4 * D_MODEL
MLA_OUT = MLA_HEADS * MLA_V
IN_SIZES = (MLA_Q_LORA, MLA_KV_LORA, MLA_ROPE, GLA_DK, GLA_DK, GLA_DV, GLA_DV, GLA_GATE_RANK, GLA_GATE_RANK,
            (HY_ORDER + 1) * HY_WIDTH, 3 * D_MODEL)

LANES = 128
SUBLANES = 8
TM = 256
HEAD_SLOT = 128
V7X_VMEM_BYTES = 64 * 1024 * 1024
VMEM_LIMIT = V7X_VMEM_BYTES * 7 // 8


def _cparams(n_axes, vmem=None):
    return pltpu.CompilerParams(dimension_semantics=("arbitrary",) * n_axes, vmem_limit_bytes=vmem)


def _const_spec(shape):
    nd = len(shape)
    return pl.BlockSpec(shape, lambda *_: (0,) * nd, pipeline_mode=pl.Buffered(1))


def _rms(x):
    return x * lax.rsqrt(jnp.mean(x * x, axis=-1, keepdims=True) + EPS)


def _sigmoid(x):
    return 1.0 / (1.0 + jnp.exp(-x))


def _dot(a, b):
    return jnp.dot(a, b, preferred_element_type=F32)


def _dot_nt(a, b):
    return lax.dot_general(a, b, (((1,), (1,)), ((), ())), preferred_element_type=F32)


def _dot3(a, b):
    a_hi = a.astype(BF16)
    a_lo = (a - a_hi.astype(F32)).astype(BF16)
    b_hi = b.astype(BF16)
    b_lo = (b - b_hi.astype(F32)).astype(BF16)
    m = a.shape[0]
    r = _dot(jnp.concatenate([a_hi, a_lo], axis=0), b_hi)
    return r[:m] + (r[m:] + _dot(a_hi, b_lo))


def _mod_kernel(cc_ref, w_ref, b_ref, o_ref):
    s = cc_ref[...]
    s = s * _sigmoid(s)
    o_ref[0] = _dot3(s, w_ref[0]) + b_ref[0]


def _modulation(cc, ada_w, ada_b):
    tn = 1536
    n6 = ada_w.shape[-1]
    return pl.pallas_call(
        _mod_kernel,
        out_shape=jax.ShapeDtypeStruct((DEPTH, 16, n6), F32),
        grid=(DEPTH, n6 // tn),
        in_specs=[
            pl.BlockSpec((16, D_MODEL), lambda l, j: (0, 0)),
            pl.BlockSpec((1, D_MODEL, tn), lambda l, j: (l, 0, j)),
            pl.BlockSpec((1, 1, tn), lambda l, j: (l, 0, j)),
        ],
        out_specs=pl.BlockSpec((1, 16, tn), lambda l, j: (l, 0, j)),
        compiler_params=_cparams(2),
        name="modulation",
    )(cc, ada_w, ada_b.reshape(DEPTH, 1, n6))


W_A = 768
W_G = 2 * GLA_DK + 2 * GLA_DV


def _token_specs(d, n_lat_tiles, ctx_blk):
    return [pl.BlockSpec((1, TM, d), lambda b, i: (b, jnp.minimum(i, n_lat_tiles - 1), 0)),
            pl.BlockSpec((1, TM, d), lambda b, i: (b, ctx_blk, 0))]


def _token_tile(x_ref, c_ref, n_lat_tiles):
    return jnp.where(pl.program_id(1) < n_lat_tiles, x_ref[0], c_ref[0])


def _inproj_kernel(x_ref, c_ref, mod_ref, g1_ref, wa_ref, wg_ref, wh_ref, wz_ref, qn_ref, kvn_ref, wuq_ref, wukv_ref,
                   wa2_ref, ba_ref, cos_ref, sin_ref,
                   q_out, k_out, v_out, gq_out, gk_out, gv_out, gr_out, glog_out, hy_out, gate_out, *, n_lat_tiles):
    x = _token_tile(x_ref, c_ref, n_lat_tiles)
    shift = mod_ref[0, 0:1, :]
    scale = mod_ref[0, 1:2, :]
    h = (_rms(x) * g1_ref[...] * (1.0 + scale) + shift).astype(BF16)

    za = _dot(h, wa_ref[...])
    cos = cos_ref[...]
    sin = sin_ref[...]

    cqn = (_rms(za[:, 0:256]) * qn_ref[...]).astype(BF16)
    qab = _dot(cqn, wuq_ref[...])
    nq = MLA_HEADS * HEAD_SLOT
    for hd in range(MLA_HEADS):
        sl = slice(hd * HEAD_SLOT, (hd + 1) * HEAD_SLOT)
        qa = qab[:, hd * HEAD_SLOT:(hd + 1) * HEAD_SLOT]
        qb = qab[:, nq + hd * HEAD_SLOT:nq + (hd + 1) * HEAD_SLOT]
        q_out[0, :, sl] = ((qa * cos + qb * sin) * (MLA_SCALE * LOG2E)).astype(BF16)

    ckvn = (_rms(za[:, 256:384]) * kvn_ref[...]).astype(BF16)
    kv = _dot(ckvn, wukv_ref[...])
    krot = za[:, 384:512] * cos + za[:, 512:640] * sin
    for hd in range(MLA_HEADS):
        sl = slice(hd * HEAD_SLOT, (hd + 1) * HEAD_SLOT)
        k_out[0, :, sl] = (kv[:, sl] + krot).astype(BF16)
    ones_hi = (lax.broadcasted_iota(jnp.int32, (1, HEAD_SLOT), 1) >= MLA_V).astype(F32)
    for hd in range(MLA_HEADS):
        sl = slice(hd * HEAD_SLOT, (hd + 1) * HEAD_SLOT)
        v_out[0, :, sl] = (kv[:, nq + hd * HEAD_SLOT:nq + (hd + 1) * HEAD_SLOT] + ones_hi).astype(BF16)

    xg = _dot(za[:, 640:768].astype(BF16), wa2_ref[...]) + ba_ref[...]
    glog_out[0] = (jnp.minimum(xg, 0.0) - jnp.log(1.0 + jnp.exp(-jnp.abs(xg)))) * (1.0 / GLA_TAU)

    zg = _dot(h, wg_ref[...])
    gq_out[0] = (zg[:, 0:GLA_DK] * (GLA_HK ** -0.5)).astype(BF16)
    gk_out[0] = zg[:, GLA_DK:2 * GLA_DK].astype(BF16)
    gv_out[0] = zg[:, 2 * GLA_DK:2 * GLA_DK + GLA_DV].astype(BF16)
    gr = zg[:, 2 * GLA_DK + GLA_DV:]
    gr_out[0] = (gr * _sigmoid(gr)).astype(BF16)

    hy_out[0] = _dot(h, wh_ref[...])
    gate_out[0] = _sigmoid(_dot(h, wz_ref[...])).astype(BF16)


def _rope_partner(w):
    a = MLA_ROPE // 4
    perm = np.concatenate([np.arange(a, 2 * a), np.arange(0, a), np.arange(3 * a, 4 * a), np.arange(2 * a, 3 * a)])
    sign = np.concatenate([-np.ones(a), np.ones(a), -np.ones(a), np.ones(a)]).astype(np.float32)
    return w[:, perm] * sign


def _prep_inproj_weights(w_in, mla_w_uq, mla_w_ukv, gla_w_a2, gla_b_a):
    offs = np.concatenate([[0], np.cumsum(IN_SIZES)])
    seg = [w_in[:, offs[i]:offs[i + 1]] for i in range(len(IN_SIZES))]
    w_cq, w_ckv, w_kr, w_gq, w_gk, w_gv, w_gr, w_af, w_ab, w_hy, w_gate = seg
    d = w_in.shape[0]
    z = lambda n: jnp.zeros((d, n), w_in.dtype)
    kr_tile = jnp.concatenate([z(MLA_NOPE), w_kr, z(HEAD_SLOT - MLA_NOPE - MLA_ROPE)], axis=1)
    krp_tile = jnp.concatenate([z(MLA_NOPE), _rope_partner(w_kr), z(HEAD_SLOT - MLA_NOPE - MLA_ROPE)], axis=1)
    a_tile = jnp.concatenate([w_af, w_ab, z(LANES - 2 * GLA_GATE_RANK)], axis=1)
    wa = jnp.concatenate([w_cq, w_ckv, kr_tile, krp_tile, a_tile], axis=1)
    wg = jnp.concatenate([w_gq, w_gk, w_gv, w_gr], axis=1)

    dh = MLA_NOPE + MLA_ROPE
    zq = lambda n: jnp.zeros((MLA_Q_LORA, n), w_in.dtype)
    plain, partner = [], []
    for hd in range(MLA_HEADS):
        blk = mla_w_uq[:, hd * dh:(hd + 1) * dh]
        plain += [blk, zq(HEAD_SLOT - dh)]
        partner += [zq(MLA_NOPE), _rope_partner(blk[:, MLA_NOPE:]), zq(HEAD_SLOT - dh)]
    wuq = jnp.concatenate(plain + partner, axis=1)

    zk = jnp.zeros((MLA_KV_LORA, HEAD_SLOT - MLA_NOPE), w_in.dtype)
    kcols, vcols = [], []
    for hd in range(MLA_HEADS):
        blk = mla_w_ukv[:, hd * (MLA_NOPE + MLA_V):(hd + 1) * (MLA_NOPE + MLA_V)]
        kcols += [blk[:, :MLA_NOPE], zk]
        vcols += [blk[:, MLA_NOPE:], zk]
    wukv = jnp.concatenate(kcols + vcols, axis=1)

    wa2 = jnp.zeros((LANES, 2 * GLA_DK), w_in.dtype)
    wa2 = wa2.at[0:GLA_GATE_RANK, 0:GLA_DK].set(gla_w_a2[0])
    wa2 = wa2.at[GLA_GATE_RANK:2 * GLA_GATE_RANK, GLA_DK:].set(gla_w_a2[1])
    ba = jnp.concatenate([gla_b_a[0], gla_b_a[1]])[None, :]
    bf = lambda t: t.astype(BF16)
    return bf(wa), bf(wg), bf(w_hy), bf(w_gate), bf(wuq), bf(wukv), bf(wa2), ba


def _rope_tables(seq, ctx_len):
    rows = seq // GRID_W
    row = np.repeat(np.arange(rows, dtype=np.float64), GRID_W)
    col = np.tile(np.arange(GRID_W, dtype=np.float64), rows)
    a = MLA_ROPE // 4
    inv = ROPE_BASE ** (-np.arange(a, dtype=np.float64) / a)
    ang_r = row[:, None] * inv
    ang_c = col[:, None] * inv
    cos32 = np.concatenate([np.cos(ang_r), np.cos(ang_r), np.cos(ang_c), np.cos(ang_c)], axis=1)
    sin32 = np.concatenate([np.sin(ang_r), np.sin(ang_r), np.sin(ang_c), np.sin(ang_c)], axis=1)
    pad_r = HEAD_SLOT - MLA_NOPE - MLA_ROPE
    cos = np.concatenate([np.ones((seq, MLA_NOPE)), cos32, np.zeros((seq, pad_r))], axis=1)
    sin = np.concatenate([np.zeros((seq, MLA_NOPE)), sin32, np.zeros((seq, pad_r))], axis=1)
    cos_c = np.concatenate([np.ones((ctx_len, MLA_NOPE + MLA_ROPE)), np.zeros((ctx_len, pad_r))], axis=1)
    sin_c = np.zeros((ctx_len, HEAD_SLOT))
    return (jnp.asarray(np.concatenate([cos, cos_c], axis=0), F32),
            jnp.asarray(np.concatenate([sin, sin_c], axis=0), F32))


def _inproj(x_lat, x_ctx, ctx_blk, mod_l, g1, weights, q_norm, kv_norm, cos, sin, n_lat_tiles):
    bsz, _, d = x_lat.shape
    nt = n_lat_tiles + 1
    t = nt * TM
    wa, wg, wh, wz, wuq, wukv, wa2, ba = weights
    tile = lambda w: pl.BlockSpec((1, TM, w), lambda b, i: (b, i, 0))
    row = lambda w: pl.BlockSpec((1, w), lambda b, i: (0, 0))
    mod_spec = pl.BlockSpec((1, 6, d), lambda b, i: (jnp.where(i < n_lat_tiles, b, bsz), 0, 0))
    tab = pl.BlockSpec((TM, HEAD_SLOT), lambda b, i: (i, 0))
    nq = MLA_HEADS * HEAD_SLOT
    sds = lambda w, dt: jax.ShapeDtypeStruct((bsz, t, w), dt)
    out_shape = (sds(nq, BF16), sds(nq, BF16), sds(nq, BF16), sds(GLA_DK, BF16), sds(GLA_DK, BF16),
                 sds(GLA_DV, BF16), sds(GLA_DV, BF16), sds(2 * GLA_DK, F32), sds(3 * HY_WIDTH, F32),
                 sds(3 * D_MODEL, BF16))
    out_specs = (tile(nq), tile(nq), tile(nq), tile(GLA_DK), tile(GLA_DK), tile(GLA_DV), tile(GLA_DV),
                 tile(2 * GLA_DK), tile(3 * HY_WIDTH), tile(3 * D_MODEL))
    return pl.pallas_call(
        functools.partial(_inproj_kernel, n_lat_tiles=n_lat_tiles),
        out_shape=out_shape,
        grid=(bsz, nt),
        in_specs=_token_specs(d, n_lat_tiles, ctx_blk) + [
            mod_spec, row(d), _const_spec(wa.shape), _const_spec(wg.shape), _const_spec(wh.shape),
            _const_spec(wz.shape), row(MLA_Q_LORA), row(MLA_KV_LORA), _const_spec(wuq.shape),
            _const_spec(wukv.shape), _const_spec(wa2.shape), row(2 * GLA_DK), tab, tab],
        out_specs=out_specs,
        compiler_params=_cparams(2, VMEM_LIMIT),
        name="inproj",
    )(x_lat, x_ctx, mod_l, g1[None, :], wa, wg, wh, wz, q_norm[None, :], kv_norm[None, :], wuq, wukv, wa2, ba,
      cos, sin)


ATT_TK = 1024


ATT_TQ = 1024


def _attn_kernel(q_ref, k_ref, v_ref, o_ref, m_ref, acc_ref, *, chunks):
    tq = q_ref.shape[1]
    m_ref[...] = jnp.full(m_ref.shape, -jnp.inf, F32)
    acc_ref[...] = jnp.zeros(acc_ref.shape, F32)
    for r0, size in chunks:
        k = k_ref[0, pl.ds(r0, size), :]
        v = v_ref[0, pl.ds(r0, size), :]
        for hd in range(2):
            sl = slice(hd * HEAD_SLOT, (hd + 1) * HEAD_SLOT)
            s = _dot_nt(q_ref[0, :, sl], k[:, sl])
            m_prev = m_ref[hd]
            m_new = jnp.maximum(m_prev, jnp.max(s, axis=1, keepdims=True))
            p = jnp.exp2((s - jnp.concatenate([m_new] * (size // LANES), axis=1)).astype(BF16))
            acc_ref[hd] = jnp.exp2(m_prev - m_new) * acc_ref[hd] + _dot(p, v[:, sl])
            m_ref[hd] = m_new
    a0 = acc_ref[0]
    a1 = acc_ref[1]
    lane = lax.broadcasted_iota(jnp.int32, (tq, HEAD_SLOT), 1)
    o0 = a0 / pltpu.roll(a0, MLA_V, 1)
    o1 = pltpu.roll(a1, MLA_V, 1) / a1
    o_ref[0] = jnp.where(lane < MLA_V, o0, o1).astype(o_ref.dtype)


def _attention(q, k, v, seq, with_ctx_queries):
    bsz, t, _ = q.shape
    ctx_len = t - seq
    assert seq % ATT_TK == 0 and seq % ATT_TQ == 0 and seq % ctx_len == 0 and 2 * MLA_V == HEAD_SLOT
    pair = 2 * HEAD_SLOT
    n_chunks = seq // ATT_TK
    chunks = tuple((j * ATT_TK, ATT_TK) for j in range(n_chunks - 1))
    chunks += (((n_chunks - 1) * ATT_TK, ATT_TK + ctx_len),)
    scratch = lambda tq: [pltpu.VMEM((2, tq, LANES), F32), pltpu.VMEM((2, tq, HEAD_SLOT), F32)]
    y = pl.pallas_call(
        functools.partial(_attn_kernel, chunks=chunks),
        out_shape=jax.ShapeDtypeStruct((bsz, seq, MLA_OUT), BF16),
        grid=(bsz, MLA_HEADS // 2, seq // ATT_TQ),
        in_specs=[
            pl.BlockSpec((1, ATT_TQ, pair), lambda b, hp, i: (b, i, hp)),
            pl.BlockSpec((1, t, pair), lambda b, hp, i: (b, 0, hp)),
            pl.BlockSpec((1, t, pair), lambda b, hp, i: (b, 0, hp)),
        ],
        out_specs=pl.BlockSpec((1, ATT_TQ, HEAD_SLOT), lambda b, hp, i: (b, i, hp)),
        scratch_shapes=scratch(ATT_TQ),
        compiler_params=_cparams(3, VMEM_LIMIT),
        name="mla_attention",
    )(q, k, v)
    if not with_ctx_queries:
        return y, None
    cblk = seq // ctx_len
    ctx_rows = lambda w: pl.BlockSpec((1, ctx_len, w), lambda b, hp: (b, cblk, hp))
    y_ctx = pl.pallas_call(
        functools.partial(_attn_kernel, chunks=((0, ctx_len),)),
        out_shape=jax.ShapeDtypeStruct((bsz, ctx_len, MLA_OUT), BF16),
        grid=(bsz, MLA_HEADS // 2),
        in_specs=[ctx_rows(pair), ctx_rows(pair), ctx_rows(pair)],
        out_specs=pl.BlockSpec((1, ctx_len, HEAD_SLOT), lambda b, hp: (b, 0, hp)),
        scratch_shapes=scratch(ctx_len),
        compiler_params=_cparams(2),
        name="mla_attention_ctx",
    )(q, k, v)
    return y, y_ctx


GLA_LEVELS = int(math.log2(TM))
GLA_SAFE_SPAN = 60.0


def _gla_level_matrices():
    i = np.arange(TM)[:, None]
    t = np.arange(TM)[None, :]
    fwd = [(t <= i)]
    for lv in range(GLA_LEVELS):
        m = TM >> (lv + 1)
        lo = (i // m) * m
        later = ((i // m) % 2) == 1
        q_part = later & (t >= lo) & (t <= i)
        k_part = (~later) & (t > i) & (t <= lo + m - 1)
        fwd.append(q_part | k_part)
    fwd = np.concatenate(fwd, axis=0).astype(np.float32)
    nb = 1 + GLA_LEVELS
    bwd = fwd.reshape(nb, TM, TM)[:, ::-1, ::-1].reshape(nb * TM, TM)
    return np.stack([fwd, bwd])


def _gla_kernel(qf_ref, kf_ref, vf_ref, gf_ref, qb_ref, kb_ref, vb_ref, gb_ref, mall_ref, of_ref, ob_ref, s_ref, a_ref):
    step = pl.program_id(1)

    @pl.when(step == 0)
    def _():
        s_ref[...] = jnp.zeros(s_ref.shape, F32)

    ins = ((qf_ref, kf_ref, vf_ref, gf_ref), (qb_ref, kb_ref, vb_ref, gb_ref))
    outs = (of_ref, ob_ref)
    qs, ks, vs, g2s, g_cums, g_tots = [], [], [], [], [], []
    for d in range(2):
        q_ref, k_ref, v_ref, g_ref = ins[d]
        qs.append(q_ref[0].astype(F32))
        ks.append(k_ref[0].astype(F32))
        vs.append(v_ref[0])
        g = g_ref[0]
        g2 = jnp.concatenate([g.astype(BF16), (g - g.astype(BF16).astype(F32)).astype(BF16)], axis=1)
        e2 = _dot(mall_ref[d, 0:TM, :], g2)
        g2s.append(g2)
        g_cums.append(e2[:, :GLA_DK] + e2[:, GLA_DK:])
        g_tots.append(jnp.sum(g, axis=0, keepdims=True))

    row = lax.broadcasted_iota(jnp.int32, (TM, TM), 0)
    col = lax.broadcasted_iota(jnp.int32, (TM, TM), 1)
    lane_head = lax.broadcasted_iota(jnp.int32, (TM, GLA_DK), 1) // GLA_HK
    tok = lax.broadcasted_iota(jnp.int32, (TM, GLA_DK), 0)
    eye = row == col

    def stack_heads(t):
        return jnp.concatenate([jnp.where(lane_head == hd, t, 0.0) for hd in range(GLA_HEADS)], axis=0).astype(BF16)

    span = jnp.maximum(jnp.max(-g_tots[0]), jnp.max(-g_tots[1]))

    @pl.when(span < GLA_SAFE_SPAN)
    def _():
        for d in range(2):
            res = _dot_nt(stack_heads(qs[d] * jnp.exp(g_cums[d])), (ks[d] * jnp.exp(-g_cums[d])).astype(BF16))
            seen = (col <= row) if d == 0 else (col >= row)
            for hd in range(GLA_HEADS):
                a_ref[d, hd] = jnp.where(seen, res[hd * TM:(hd + 1) * TM], 0.0)

    @pl.when(span >= GLA_SAFE_SPAN)
    def _():
        for d in range(2):
            q, k = qs[d], ks[d]
            e2l = _dot(mall_ref[d, TM:, :], g2s[d])
            e_lv = e2l[:, :GLA_DK] + e2l[:, GLA_DK:]
            res = _dot_nt(stack_heads(q), k.astype(BF16))
            for hd in range(GLA_HEADS):
                a_ref[d, hd] = jnp.where(eye, res[hd * TM:(hd + 1) * TM], 0.0)
            for lv in range(GLA_LEVELS):
                m = TM >> (lv + 1)
                w = jnp.exp(e_lv[lv * TM:(lv + 1) * TM])
                bit = (tok // m) % 2
                q_act = bit != d
                qt = jnp.where(q_act, q * w, 0.0)
                kt = jnp.where(q_act, 0.0, k * w).astype(BF16)
                res = _dot_nt(stack_heads(qt), kt)
                if m == TM // 2:
                    for hd in range(GLA_HEADS):
                        a_ref[d, hd] += res[hd * TM:(hd + 1) * TM]
                else:
                    same = (row // (2 * m)) == (col // (2 * m))
                    for hd in range(GLA_HEADS):
                        a_ref[d, hd] += jnp.where(same, res[hd * TM:(hd + 1) * TM], 0.0)

    same_head = (lax.broadcasted_iota(jnp.int32, (GLA_DK, GLA_DV), 0) // GLA_HK
                 == lax.broadcasted_iota(jnp.int32, (GLA_DK, GLA_DV), 1) // GLA_HV)
    for d in range(2):
        q, k, v, g_cum, g_tot = qs[d], ks[d], vs[d], g_cums[d], g_tots[d]
        s_old = s_ref[d]
        o_inter = _dot((q * jnp.exp(g_cum)).astype(BF16), s_old.astype(BF16))
        for hd in range(GLA_HEADS):
            sl = slice(hd * GLA_HV, (hd + 1) * GLA_HV)
            o_intra = _dot(a_ref[d, hd].astype(BF16), v[:, sl])
            outs[d][0, :, sl] = (o_intra + o_inter[:, sl]).astype(outs[d].dtype)

        kdec_t = (k * jnp.exp(g_tot - g_cum)).T.astype(BF16)
        upd = _dot(kdec_t, v)
        dec_col = jnp.sum(jnp.where(eye, jnp.broadcast_to(jnp.exp(g_tot), (TM, GLA_DK)), 0.0), axis=1,
                          keepdims=True)
        s_ref[d] = dec_col * s_old + jnp.where(same_head, upd, 0.0)


def _gla(gq, gk, gv, glog, n_lat_tiles):
    assert GLA_DK == TM
    bsz, t, _ = gq.shape
    nt = t // TM
    mall = jnp.asarray(_gla_level_matrices(), dtype=BF16)

    def tile_idx(d, s):
        return jnp.where(s == 0, n_lat_tiles, s - 1 if d == 0 else n_lat_tiles - s)

    def specs(d):
        tok = lambda w: pl.BlockSpec((1, TM, w), lambda b, s: (b, tile_idx(d, s), 0))
        return [tok(GLA_DK), tok(GLA_DK), tok(GLA_DV), pl.BlockSpec((1, TM, GLA_DK), lambda b, s: (b, tile_idx(d, s), d))]

    out_spec = lambda d: pl.BlockSpec((1, TM, GLA_DV), lambda b, s: (b, tile_idx(d, s), 0))
    sds = jax.ShapeDtypeStruct((bsz, t, GLA_DV), BF16)
    return pl.pallas_call(
        _gla_kernel,
        out_shape=(sds, sds),
        grid=(bsz, nt),
        in_specs=specs(0) + specs(1) + [_const_spec(mall.shape)],
        out_specs=(out_spec(0), out_spec(1)),
        scratch_shapes=[pltpu.VMEM((2, GLA_DK, GLA_DV), F32), pltpu.VMEM((2, GLA_HEADS, TM, TM), F32)],
        compiler_params=_cparams(2),
        name="gla_scan",
    )(gq, gk, gv, glog, gq, gk, gv, glog, mall)


HY_N1 = 64
HY_N2 = 128
HY_SLAB = SUBLANES
HY_CB = 256
HY_KG = 16
HY_SPEC_CB = 128
HY_UNROLL = 8


def _hyena_dft_constants(seq):
    n = 2 * seq
    assert n == HY_N1 * HY_N2
    nh = HY_N1 // 2
    kh = HY_N2 // 2
    eye = np.eye(HY_SLAB)
    k1 = np.arange(HY_N1)
    th = 2 * np.pi * np.outer(k1 + 0.5, np.arange(nh)) / HY_N1
    fwd1 = np.concatenate([np.kron(np.cos(th), eye), np.kron(-np.sin(th), eye)], axis=0)
    inv1 = (2.0 / n) * np.concatenate([np.kron(np.cos(th).T, eye), np.kron(-np.sin(th).T, eye)], axis=1)
    nlo = np.arange(HY_N2)
    k2 = np.arange(kh)
    ph = 2 * np.pi * (k2[None, :, None] * nlo[None, None, :] / HY_N2
                      + (k1[:, None, None] + 0.5) * nlo[None, None, :] / n)
    c, s = np.cos(ph), np.sin(ph)
    fwd2 = np.concatenate([np.concatenate([c, s], axis=2), np.concatenate([-s, c], axis=2)], axis=1)
    ct, st = c.transpose(0, 2, 1), s.transpose(0, 2, 1)
    inv2 = np.concatenate([np.concatenate([ct, -st], axis=2), np.concatenate([st, ct], axis=2)], axis=1)
    return fwd1, fwd2, inv2, inv1


def _hyena_ctx_dft_constants(ctx_len):
    n = 2 * ctx_len
    th = 2 * np.pi * np.outer(np.arange(ctx_len) + 0.5, np.arange(ctx_len)) / n
    fwd = np.concatenate([np.cos(th), -np.sin(th)], axis=0)
    inv = (2.0 / n) * np.concatenate([np.cos(th).T, -np.sin(th).T], axis=1)
    return fwd, inv


def _hyena_features(length):
    pos = np.arange(length, dtype=np.float64)
    t = pos / max(length - 1, 1)
    f = np.linspace(1e-4, HY_BANDS - 1, HY_BANDS)
    ang = (2.0 * math.pi / length) * pos[:, None] * f
    feat = np.concatenate([t[:, None], np.cos(ang), np.sin(ang)], axis=-1)
    return jnp.asarray(np.pad(feat, ((0, 0), (0, LANES - HY_POS_DIM))), F32)


def _hy_filter_kernel(feat_ref, w1_ref, b1_ref, w2_ref, b2_ref, w3_ref, b3_ref, absd_ref, h_ref, s_ref):
    i = pl.program_id(0)
    feat = feat_ref[...]
    hp = _dot3
    hdn = jnp.sin(hp(feat, w1_ref[...]) + b1_ref[...])
    hdn = jnp.sin(hp(hdn, w2_ref[...]) + b2_ref[...])
    h = hp(hdn, w3_ref[...]) + b3_ref[...]
    window = jnp.exp(-feat[:, 0:1] * absd_ref[...])
    h = h * jnp.concatenate([window] * (2 * HY_ORDER), axis=1)
    h_ref[...] = h

    @pl.when(i == 0)
    def _():
        s_ref[...] = jnp.zeros(s_ref.shape, F32)

    s_ref[...] += jnp.sum(jnp.abs(h), axis=0, keepdims=True)


def _hyena_filters_raw(length, filt_w):
    w1, b1, w2, b2, w3, b3 = filt_w
    nf = 2 * HY_ORDER * HY_WIDTH
    tr = min(length, 512)
    deltas = np.linspace(math.log(HY_DECAY_TARGET) / HY_FAST_DECAY, math.log(HY_DECAY_TARGET) / HY_SLOW_DECAY,
                         HY_WIDTH, dtype=np.float32)
    absd = jnp.asarray(np.abs(deltas))[None, :]
    w1p = jnp.pad(w1, ((0, LANES - HY_POS_DIM), (0, 0)))
    full = lambda shp: pl.BlockSpec(shp, lambda i: (0,) * len(shp))
    return pl.pallas_call(
        _hy_filter_kernel,
        out_shape=(jax.ShapeDtypeStruct((length, nf), F32), jax.ShapeDtypeStruct((1, nf), F32)),
        grid=(length // tr,),
        in_specs=[pl.BlockSpec((tr, LANES), lambda i: (i, 0)), full((LANES, HY_FILTER_HIDDEN)),
                  full((1, HY_FILTER_HIDDEN)), full((HY_FILTER_HIDDEN, HY_FILTER_HIDDEN)), full((1, HY_FILTER_HIDDEN)),
                  full((HY_FILTER_HIDDEN, nf)), full((1, nf)), full((1, HY_WIDTH))],
        out_specs=(pl.BlockSpec((tr, nf), lambda i: (i, 0)), full((1, nf))),
        compiler_params=_cparams(1),
        name="hyena_filter_mlp",
    )(_hyena_features(length), w1p, b1[None, :], w2, b2[None, :], w3, b3[None, :], absd)


def _dot_split(m, x):
    x_hi = x.astype(BF16)
    x_lo = (x - x_hi.astype(F32)).astype(BF16)
    n = x.shape[1]
    r = _dot(m, jnp.concatenate([x_hi, x_lo], axis=1))
    return r[:, :n] + r[:, n:]


def _odft_stage1(src_at, mm, s_re, s_im, unroll=2):
    nk = s_re.shape[0]
    half = nk * HY_SLAB

    def body(j, carry):
        r0 = pl.multiple_of(j * HY_SLAB, HY_SLAB)
        slab = src_at(r0)
        cb = slab.shape[-1]
        res = mm(slab.reshape(-1, cb))
        s_re[:, pl.ds(r0, HY_SLAB), :] = res[:half].reshape(nk, HY_SLAB, cb)
        s_im[:, pl.ds(r0, HY_SLAB), :] = res[half:].reshape(nk, HY_SLAB, cb)
        return carry

    lax.fori_loop(0, HY_N2 // HY_SLAB, body, 0, unroll=unroll)


def _hy_spectrum_kernel(hf_ref, hb_ref, sf_ref, sb_ref, fwd1_ref, fwd2_ref, o_ref, s_re, s_im):
    kh = HY_N2 // 2
    mm1 = lambda x: _dot_split(fwd1_ref[...], x)

    def middle(sign):
        def body(k1, carry):
            a = jnp.concatenate([s_re[k1], s_im[k1]], axis=0)
            x = _dot_split(fwd2_ref[k1], a)
            if sign is None:
                o_ref[0, k1] = x[:kh]
                o_ref[1, k1] = x[kh:]
            else:
                inv_norm = 1.0 / (sf_ref[...] + sb_ref[...])
                o_ref[0, k1] = (o_ref[0, k1] + x[:kh]) * inv_norm
                o_ref[1, k1] = (o_ref[1, k1] - x[kh:]) * inv_norm
            return carry
        lax.fori_loop(0, HY_N1, body, 0, unroll=8)

    _odft_stage1(lambda r0: hf_ref[:, pl.ds(r0, HY_SLAB), :], mm1, s_re, s_im, unroll=4)
    middle(None)

    def bwd_slab(r0):
        slab = hb_ref[:, pl.ds(r0, HY_SLAB), :]
        nhi = lax.broadcasted_iota(jnp.int32, slab.shape, 0)
        r = lax.broadcasted_iota(jnp.int32, slab.shape, 1)
        return jnp.where((nhi == 0) & (r + r0 == 0), 0.0, slab)

    _odft_stage1(bwd_slab, mm1, s_re, s_im, unroll=4)
    middle(-1)


def _hyena_filter_spectrum(seq, w1, b1, w2, b2, w3, b3):
    h_raw, s = _hyena_filters_raw(seq, (w1, b1, w2, b2, w3, b3))
    nh = HY_N1 // 2
    nc = HY_ORDER * HY_WIDTH
    h3 = h_raw.reshape(nh, HY_N2, 2 * nc)
    fwd1, fwd2, _, _ = _hyena_dft_constants(seq)
    fwd1 = jnp.asarray(fwd1, BF16)
    fwd2 = jnp.asarray(fwd2, BF16)
    scb = HY_SPEC_CB
    ncb = nc // scb
    return pl.pallas_call(
        _hy_spectrum_kernel,
        out_shape=jax.ShapeDtypeStruct((2, HY_N1, HY_N2 // 2, nc), F32),
        grid=(ncb,),
        in_specs=[pl.BlockSpec((nh, HY_N2, scb), lambda c: (0, 0, c)),
                  pl.BlockSpec((nh, HY_N2, scb), lambda c: (0, 0, ncb + c)),
                  pl.BlockSpec((1, scb), lambda c: (0, c)),
                  pl.BlockSpec((1, scb), lambda c: (0, ncb + c)),
                  _const_spec(fwd1.shape), _const_spec(fwd2.shape)],
        out_specs=pl.BlockSpec((2, HY_N1, HY_N2 // 2, scb), lambda c: (0, 0, 0, c)),
        scratch_shapes=[pltpu.VMEM((HY_N1, HY_N2, scb), F32), pltpu.VMEM((HY_N1, HY_N2, scb), F32)],
        compiler_params=_cparams(1, VMEM_LIMIT),
        name="hyena_filter_spectrum",
    )(h3, h3, s, s, fwd1, fwd2)


def _short_conv_chunk(ref, c, n_chunks, w_ref, b_ref):
    per = TM // HY_N2
    cur = ref[0, pl.ds(per * c, per)]
    cb = cur.shape[-1]
    cur = cur.reshape(TM, cb)
    prev = ref[0, jnp.maximum(per * c - 1, 0), pl.ds(HY_N2 - SUBLANES, SUBLANES), :][SUBLANES - 1:SUBLANES]
    nxt = ref[0, jnp.minimum(per * c + per, per * n_chunks - 1), pl.ds(0, SUBLANES), :][0:1]
    prev = jnp.where(c > 0, prev, 0.0)
    nxt = jnp.where(c < n_chunks - 1, nxt, 0.0)
    rowi = lax.broadcasted_iota(jnp.int32, (TM, cb), 0)
    dn = jnp.where(rowi == 0, prev, pltpu.roll(cur, 1, 0))
    up = jnp.where(rowi == TM - 1, nxt, pltpu.roll(cur, TM - 1, 0))
    return b_ref[...] + w_ref[0:1, :] * dn + w_ref[1:2, :] * cur + w_ref[2:3, :] * up


def _hy_conv_kernel(*refs, conv_y):
    if conv_y:
        (y_ref, g_ref, h_ref, bias_ref, wy_ref, by_ref, wg_ref, bg_ref, fwd1_ref, fwd2_ref, inv2_ref, inv1_ref,
         o_ref, s_re, s_im, gs_ref, us_ref) = refs
    else:
        (y_ref, g_ref, h_ref, bias_ref, wg_ref, bg_ref, fwd1_ref, fwd2_ref, inv2_ref, inv1_ref,
         o_ref, s_re, s_im, gs_ref) = refs
    grp = pl.program_id(2)
    last = pl.num_programs(2) - 1
    nh = HY_N1 // 2
    per = TM // HY_N2
    n_chunks = nh // per
    cb = o_ref.shape[-1]
    kh = HY_N2 // 2

    @pl.when(grp == 0)
    def _():
        def pre(c, carry):
            gs_ref[pl.ds(per * c, per)] = _short_conv_chunk(g_ref, c, n_chunks, wg_ref, bg_ref).reshape(per, HY_N2, cb)
            if conv_y:
                us_ref[pl.ds(per * c, per)] = _short_conv_chunk(y_ref, c, n_chunks, wy_ref, by_ref).reshape(
                    per, HY_N2, cb)
            return carry
        lax.fori_loop(0, n_chunks, pre, 0)

    if conv_y:
        u_at = lambda r0: us_ref[:, pl.ds(r0, HY_SLAB), :]
    else:
        u_at = lambda r0: y_ref[0, :, pl.ds(r0, HY_SLAB), :]

    _odft_stage1(u_at, lambda x: _dot(fwd1_ref[0], x.astype(BF16)), s_re, s_im, unroll=HY_UNROLL)

    def middle(k1, carry):
        a = jnp.concatenate([s_re[k1], s_im[k1]], axis=0).astype(BF16)
        x = _dot(fwd2_ref[k1], a)
        xr, xi = x[:kh], x[kh:]
        hr, hi = h_ref[0, k1], h_ref[1, k1]
        y = jnp.concatenate([xr * hr - xi * hi, xr * hi + xi * hr], axis=0).astype(BF16)
        bm = _dot(inv2_ref[k1], y)
        s_re[k1] = bm[:HY_N2]
        s_im[k1] = bm[HY_N2:]
        return carry

    lax.fori_loop(0, HY_KG, middle, 0, unroll=2 * HY_UNROLL)

    def partial_conv(r0):
        slab = jnp.concatenate([s_re[:, pl.ds(r0, HY_SLAB), :].reshape(HY_KG * HY_SLAB, cb),
                                s_im[:, pl.ds(r0, HY_SLAB), :].reshape(HY_KG * HY_SLAB, cb)], axis=0).astype(BF16)
        return _dot(inv1_ref[0], slab).reshape(nh, HY_SLAB, cb)

    def post_loop(fn):
        def post(j, carry):
            r0 = pl.multiple_of(j * HY_SLAB, HY_SLAB)
            o_ref[0, :, pl.ds(r0, HY_SLAB), :] = fn(r0, partial_conv(r0))
            return carry
        lax.fori_loop(0, HY_N2 // HY_SLAB, post, 0, unroll=HY_UNROLL)

    @pl.when(grp == 0)
    def _():
        post_loop(lambda r0, part: part)

    @pl.when((grp > 0) & (grp < last))
    def _():
        post_loop(lambda r0, part: o_ref[0, :, pl.ds(r0, HY_SLAB), :] + part)

    @pl.when(grp == last)
    def _():
        post_loop(lambda r0, part: gs_ref[:, pl.ds(r0, HY_SLAB), :]
                  * (o_ref[0, :, pl.ds(r0, HY_SLAB), :] + part + bias_ref[...] * u_at(r0)))


def _hyena_order(y4, y_col0, z4, gate_col0, hspec, order, hy_bias, short_w, short_b, consts, conv_y):
    bsz = z4.shape[0]
    nh = HY_N1 // 2
    ncb = HY_WIDTH // HY_CB
    ngrp = HY_N1 // HY_KG
    fwd1, fwd2, inv2, inv1 = consts
    blk4 = lambda off: pl.BlockSpec((1, nh, HY_N2, HY_CB), lambda c, b, g: (b, 0, 0, off + c))
    rowspec = lambda rows, off: pl.BlockSpec((rows, HY_CB), lambda c, b, g: (0, off + c))
    grouped = lambda shp: pl.BlockSpec(shp, lambda c, b, g: (g, 0, 0))
    in_specs = [blk4(y_col0), blk4(gate_col0),
                pl.BlockSpec((2, HY_KG, HY_N2 // 2, HY_CB), lambda c, b, g: (0, g, 0, order * ncb + c)),
                rowspec(1, 0)]
    args = [y4, z4, hspec, hy_bias.reshape(1, -1)]
    if conv_y:
        in_specs += [rowspec(HY_SHORT, y_col0), rowspec(1, y_col0)]
        args += [short_w, short_b[None, :]]
    in_specs += [rowspec(HY_SHORT, gate_col0), rowspec(1, gate_col0)]
    args += [short_w, short_b[None, :]]
    in_specs += [grouped((1,) + fwd1.shape[1:]), grouped((HY_KG,) + fwd2.shape[1:]),
                 grouped((HY_KG,) + inv2.shape[1:]), grouped((1,) + inv1.shape[1:])]
    args += [fwd1, fwd2, inv2, inv1]
    scratch = [pltpu.VMEM((HY_KG, HY_N2, HY_CB), F32), pltpu.VMEM((HY_KG, HY_N2, HY_CB), F32),
               pltpu.VMEM((nh, HY_N2, HY_CB), F32)]
    if conv_y:
        scratch.append(pltpu.VMEM((nh, HY_N2, HY_CB), F32))
    return pl.pallas_call(
        functools.partial(_hy_conv_kernel, conv_y=conv_y),
        out_shape=jax.ShapeDtypeStruct((bsz, nh, HY_N2, HY_WIDTH), F32),
        grid=(ncb, bsz, ngrp),
        in_specs=in_specs,
        out_specs=pl.BlockSpec((1, nh, HY_N2, HY_CB), lambda c, b, g: (b, 0, 0, c)),
        scratch_shapes=scratch,
        compiler_params=_cparams(3, VMEM_LIMIT),
        name="hyena_conv%d" % order,
    )(*args)


def _hyena_latent(z_hy, hspec, short_w, short_b, hy_bias):
    bsz, t, _ = z_hy.shape
    seq = HY_N1 * HY_N2 // 2
    fwd1, fwd2, inv2, inv1 = _hyena_dft_constants(seq)
    ngrp = HY_N1 // HY_KG
    rows = HY_KG * HY_SLAB
    fwd1 = fwd1.reshape(2, ngrp, rows, -1).transpose(1, 0, 2, 3).reshape(ngrp, 2 * rows, -1)
    inv1 = inv1.reshape(-1, 2, ngrp, rows).transpose(2, 0, 1, 3).reshape(ngrp, -1, 2 * rows)
    consts = tuple(jnp.asarray(m, BF16) for m in (fwd1, fwd2, inv2, inv1))
    z4 = z_hy.reshape(bsz, t // HY_N2, HY_N2, 3 * HY_WIDTH)
    ncb = HY_WIDTH // HY_CB
    y1 = _hyena_order(z4, 2 * ncb, z4, 0, hspec, 0, hy_bias[0], short_w, short_b, consts, True)
    y2 = _hyena_order(y1, 0, z4, ncb, hspec, 1, hy_bias[1], short_w, short_b, consts, False)
    return y2.reshape(bsz, seq, HY_WIDTH)


def _hy_ctx_spectrum_kernel(h_ref, s_ref, fwd_ref, o_ref):
    lc = h_ref.shape[0]
    nc = HY_ORDER * HY_WIDTH
    hp = _dot3
    h = h_ref[...]
    rowi = lax.broadcasted_iota(jnp.int32, (lc, nc), 0)
    xf = hp(fwd_ref[...], h[:, :nc])
    xb = hp(fwd_ref[...], jnp.where(rowi == 0, 0.0, h[:, nc:]))
    inv_norm = 1.0 / (s_ref[:, :nc] + s_ref[:, nc:])
    o_ref[0] = (xf[:lc] + xb[:lc]) * inv_norm
    o_ref[1] = (xf[lc:] - xb[lc:]) * inv_norm


def _hy_ctx_conv_kernel(x1_ref, x2_ref, v_ref, h_ref, bias_ref, w_ref, b_ref, fwd_ref, inv_ref, o_ref):
    lc = o_ref.shape[1]
    rowi = lax.broadcasted_iota(jnp.int32, (lc, HY_WIDTH), 0)

    def short(ref, part):
        cur = ref[0].reshape(lc, HY_WIDTH)
        sl = slice(part * HY_WIDTH, (part + 1) * HY_WIDTH)
        dn = jnp.where(rowi == 0, 0.0, pltpu.roll(cur, 1, 0))
        up = jnp.where(rowi == lc - 1, 0.0, pltpu.roll(cur, lc - 1, 0))
        return b_ref[:, sl] + w_ref[0:1, sl] * dn + w_ref[1:2, sl] * cur + w_ref[2:3, sl] * up

    y = short(v_ref, 2)
    for order, gref in enumerate((x1_ref, x2_ref)):
        sl = slice(order * HY_WIDTH, (order + 1) * HY_WIDTH)
        x = _dot(fwd_ref[...], y.astype(BF16))
        xr, xi = x[:lc], x[lc:]
        hr, hi = h_ref[0, :, sl], h_ref[1, :, sl]
        prod = jnp.concatenate([xr * hr - xi * hi, xr * hi + xi * hr], axis=0).astype(BF16)
        conv = _dot(inv_ref[...], prod)
        y = short(gref, order) * (conv + bias_ref[order:order + 1, :] * y)
    o_ref[0] = y


def _hyena_ctx(z_hy, filt_w, short_w, short_b, hy_bias):
    bsz, t, _ = z_hy.shape
    seq = HY_N1 * HY_N2 // 2
    lc = t - seq
    per = lc // HY_N2
    h_raw, s = _hyena_filters_raw(lc, filt_w)
    fwd, inv = _hyena_ctx_dft_constants(lc)
    nc = HY_ORDER * HY_WIDTH
    full = lambda shp: pl.BlockSpec(shp, lambda *_: (0,) * len(shp))
    hspec = pl.pallas_call(
        _hy_ctx_spectrum_kernel,
        out_shape=jax.ShapeDtypeStruct((2, lc, nc), F32),
        grid=(1,),
        in_specs=[full(h_raw.shape), full(s.shape), full(fwd.shape)],
        out_specs=full((2, lc, nc)),
        compiler_params=_cparams(1),
        name="hyena_ctx_spectrum",
    )(h_raw, s, jnp.asarray(fwd, F32))
    z4 = z_hy.reshape(bsz, t // HY_N2, HY_N2, 3 * HY_WIDTH)
    blk = lambda part: pl.BlockSpec((1, per, HY_N2, HY_WIDTH), lambda b: (b, seq // lc, 0, part))
    return pl.pallas_call(
        _hy_ctx_conv_kernel,
        out_shape=jax.ShapeDtypeStruct((bsz, lc, HY_WIDTH), F32),
        grid=(bsz,),
        in_specs=[blk(0), blk(1), blk(2), full((2, lc, nc)), full((HY_ORDER, HY_WIDTH)),
                  full((HY_SHORT, 3 * HY_WIDTH)), full((1, 3 * HY_WIDTH)), full(fwd.shape), full(inv.shape)],
        out_specs=pl.BlockSpec((1, lc, HY_WIDTH), lambda b: (b, 0, 0)),
        compiler_params=_cparams(1),
        name="hyena_ctx_conv",
    )(z4, z4, z4, hspec, hy_bias, short_w, short_b[None, :], jnp.asarray(fwd, BF16), jnp.asarray(inv, BF16))


FF_CHUNK = 1024


def _merge_mlp_kernel(x_ref, c_ref, mod_ref, ymla_ref, ymlac_ref, of_ref, ob_ref, gr_ref, yhy_ref, yhyc_ref, zg_ref, on_ref,
                      wm_ref, wgl_ref, wh_ref, wo_ref, g2_ref, w1_ref, w2_ref, fg_ref, o_ref, *, final, n_lat_tiles):
    o = of_ref[0].astype(F32) + ob_ref[0].astype(F32)
    silu = gr_ref[0].astype(F32)
    parts = []
    for hd in range(GLA_HEADS):
        sl = slice(hd * GLA_HV, (hd + 1) * GLA_HV)
        parts.append((_rms(o[:, sl]) * on_ref[...] * silu[:, sl]).astype(BF16))
    y_gla = jnp.concatenate(parts, axis=1)
    zg = zg_ref[0].astype(F32)
    d = x_ref.shape[-1]
    m = zg[:, 0:d] * _dot(_token_tile(ymla_ref, ymlac_ref, n_lat_tiles), wm_ref[...])
    m = m + zg[:, d:2 * d] * _dot(y_gla, wgl_ref[...])
    m = m + zg[:, 2 * d:3 * d] * _dot(_token_tile(yhy_ref, yhyc_ref, n_lat_tiles).astype(BF16), wh_ref[...])
    out = _dot(m.astype(BF16), wo_ref[...])
    x = _token_tile(x_ref, c_ref, n_lat_tiles) + mod_ref[0, 2:3, :] * out

    h = (_rms(x) * g2_ref[...] * (1.0 + mod_ref[0, 4:5, :]) + mod_ref[0, 3:4, :]).astype(BF16)
    acc = jnp.zeros(x.shape, F32)
    for j in range(w1_ref.shape[1] // FF_CHUNK):
        a = jnp.maximum(_dot(h, w1_ref[:, j * FF_CHUNK:(j + 1) * FF_CHUNK]), 0.0)
        acc = acc + _dot((a * a).astype(BF16), w2_ref[j * FF_CHUNK:(j + 1) * FF_CHUNK, :])
    xn = x + mod_ref[0, 5:6, :] * acc
    if final:
        xn = _rms(xn) * fg_ref[...]
    o_ref[0] = xn


def _mod_spec(d, n_lat_tiles, bsz):
    return pl.BlockSpec((1, 6, d), lambda b, i: (jnp.where(i < n_lat_tiles, b, bsz), 0, 0))


def _merge_mlp(x_lat, x_ctx, ctx_blk, mod_l, y_mla, y_mla_ctx, o_gla, gr, y_hy, y_hy_ctx, gate, out_norm,
               w_o_mla, w_o_gla, w_o_hy, w_out, g2, w1, w2, final_g, n_tiles, n_lat_tiles, final):
    bsz, _, d = x_lat.shape
    if y_mla_ctx is None:
        y_mla_ctx, y_hy_ctx = y_mla, y_hy
    tile = lambda w: pl.BlockSpec((1, TM, w), lambda b, i: (b, i, 0))
    row = lambda w: pl.BlockSpec((1, w), lambda b, i: (0, 0))
    bf = lambda w: w.astype(BF16)
    return pl.pallas_call(
        functools.partial(_merge_mlp_kernel, final=final, n_lat_tiles=n_lat_tiles),
        out_shape=jax.ShapeDtypeStruct((bsz, n_tiles * TM, d), F32),
        grid=(bsz, n_tiles),
        in_specs=_token_specs(d, n_lat_tiles, ctx_blk) + [_mod_spec(d, n_lat_tiles, bsz)]
        + _token_specs(MLA_OUT, n_lat_tiles, 0) + [tile(GLA_DV), tile(GLA_DV), tile(GLA_DV)]
        + _token_specs(HY_WIDTH, n_lat_tiles, 0) + [
            tile(3 * d), row(GLA_HV),
            _const_spec(w_o_mla.shape), _const_spec(w_o_gla.shape), _const_spec(w_o_hy.shape),
            _const_spec(w_out.shape), row(d), _const_spec(w1.shape), _const_spec(w2.shape), row(d)],
        out_specs=tile(d),
        compiler_params=_cparams(2, VMEM_LIMIT),
        name="merge_mlp",
    )(x_lat, x_ctx, mod_l, y_mla, y_mla_ctx, o_gla[0], o_gla[1], gr, y_hy, y_hy_ctx, gate, out_norm[None, :],
      bf(w_o_mla), bf(w_o_gla), bf(w_o_hy), bf(w_out), g2[None, :], bf(w1), bf(w2), final_g[None, :])


def kernel(x, c, ctx, c_ctx, ada_w, ada_b, norm1_g, norm2_g, w_in, mla_q_norm, mla_w_uq, mla_kv_norm, mla_w_ukv, gla_w_a2, gla_b_a, gla_out_norm, hy_short_w, hy_short_b, hy_f_w1, hy_f_b1, hy_f_w2, hy_f_b2, hy_f_w3, hy_f_b3, hy_bias, w_o_mla, w_o_gla, w_o_hy, w_out, ff_w1, ff_w2, final_norm_g):
    bsz, seq, d = x.shape
    ctx_len = ctx.shape[1]
    n_lat = seq // TM
    assert ctx_len == TM and seq % TM == 0
    n_all = n_lat + 1
    x_lat, x_ctx, ctx_blk = x, ctx, 0
    cc = jnp.zeros((16, d), F32).at[:bsz].set(c).at[bsz].set(c_ctx)
    mod = _modulation(cc, ada_w, ada_b).reshape(DEPTH, 16, 6, d)
    cos, sin = _rope_tables(seq, ctx_len)
    for l in range(DEPTH):
        last = l == DEPTH - 1
        n_tiles = n_lat if last else n_all
        weights = _prep_inproj_weights(w_in[l], mla_w_uq[l], mla_w_ukv[l], gla_w_a2[l], gla_b_a[l])
        q, k, v, gq, gk, gv, gr, glog, z_hy, z_gate = _inproj(x_lat, x_ctx, ctx_blk, mod[l], norm1_g[l], weights,
                                                             mla_q_norm[l], mla_kv_norm[l], cos, sin, n_lat)
        y_mla, y_mla_c = _attention(q, k, v, seq, not last)
        o_gla = _gla(gq, gk, gv, glog, n_lat)
        filt_w = (hy_f_w1[l], hy_f_b1[l], hy_f_w2[l], hy_f_b2[l], hy_f_w3[l], hy_f_b3[l])
        hspec = _hyena_filter_spectrum(seq, *filt_w)
        y_hy = _hyena_latent(z_hy, hspec, hy_short_w[l], hy_short_b[l], hy_bias[l])
        y_hy_c = None if last else _hyena_ctx(z_hy, filt_w, hy_short_w[l], hy_short_b[l], hy_bias[l])
        xc = _merge_mlp(x_lat, x_ctx, ctx_blk, mod[l], y_mla, y_mla_c, o_gla, gr, y_hy, y_hy_c, z_gate,
                        gla_out_norm[l], w_o_mla[l], w_o_gla[l], w_o_hy[l], w_out[l], norm2_g[l], ff_w1[l], ff_w2[l],
                        final_norm_g, n_tiles, n_lat, last)
        x_lat, x_ctx, ctx_blk = xc, xc, n_lat
    return xc
```

```python
import functools
import math

import numpy as np
import jax
import jax.numpy as jnp
from jax import lax
from jax.experimental import pallas as pl
from jax.experimental.pallas import tpu as pltpu

F32 = jnp.float32
BF16 = jnp.bfloat16
LOG2E = 1.4426950408889634

D_MODEL = 1024
DEPTH = 2
GRID_W = 64
EPS = 1e-6
MLA_HEADS = 8
MLA_NOPE = 64
MLA_ROPE = 32
MLA_V = 64
MLA_Q_LORA = 256
MLA_KV_LORA = 128
MLA_SCALE = (MLA_NOPE + MLA_ROPE) ** -0.5
ROPE_BASE = 10000.0
GLA_HEADS = 4
GLA_DK = 256
GLA_DV = 512
GLA_HK = GLA_DK // GLA_HEADS
GLA_HV = GLA_DV // GLA_HEADS
GLA_GATE_RANK = 16
GLA_TAU = 16.0
HY_WIDTH = 512
HY_ORDER = 2
HY_SHORT = 3
HY_BANDS = 16
HY_POS_DIM = 1 + 2 * HY_BANDS
HY_FILTER_HIDDEN = 64
HY_FAST_DECAY = 0.3
HY_SLOW_DECAY = 1.5
HY_DECAY_TARGET = 1e-2
D_FF = 4 * D_MODEL
MLA_OUT = MLA_HEADS * MLA_V
IN_SIZES = (MLA_Q_LORA, MLA_KV_LORA, MLA_ROPE, GLA_DK, GLA_DK, GLA_DV, GLA_DV, GLA_GATE_RANK, GLA_GATE_RANK,
            (HY_ORDER + 1) * HY_WIDTH, 3 * D_MODEL)

LANES = 128
SUBLANES = 8
TM = 256
HEAD_SLOT = 128
V7X_VMEM_BYTES = 64 * 1024 * 1024
VMEM_LIMIT = V7X_VMEM_BYTES * 7 // 8


def _cparams(n_axes, vmem=None):
    return pltpu.CompilerParams(dimension_semantics=("arbitrary",) * n_axes, vmem_limit_bytes=vmem)


def _const_spec(shape):
    nd = len(shape)
    return pl.BlockSpec(shape, lambda *_: (0,) * nd, pipeline_mode=pl.Buffered(1))


def _rms(x):
    return x * lax.rsqrt(jnp.mean(x * x, axis=-1, keepdims=True) + EPS)


def _sigmoid(x):
    return 1.0 / (1.0 + jnp.exp(-x))


def _dot(a, b):
    return jnp.dot(a, b, preferred_element_type=F32)


def _dot_nt(a, b):
    return lax.dot_general(a, b, (((1,), (1,)), ((), ())), preferred_element_type=F32)


def _dot3(a, b):
    a_hi = a.astype(BF16)
    a_lo = (a - a_hi.astype(F32)).astype(BF16)
    b_hi = b.astype(BF16)
    b_lo = (b - b_hi.astype(F32)).astype(BF16)
    m = a.shape[0]
    r = _dot(jnp.concatenate([a_hi, a_lo], axis=0), b_hi)
    return r[:m] + (r[m:] + _dot(a_hi, b_lo))


def _mod_kernel(cc_ref, w_ref, b_ref, o_ref):
    s = cc_ref[...]
    s = s * _sigmoid(s)
    o_ref[0] = _dot3(s, w_ref[0]) + b_ref[0]


def _modulation(cc, ada_w, ada_b):
    tn = 1536
    n6 = ada_w.shape[-1]
    return pl.pallas_call(
        _mod_kernel,
        out_shape=jax.ShapeDtypeStruct((DEPTH, 16, n6), F32),
        grid=(DEPTH, n6 // tn),
        in_specs=[
            pl.BlockSpec((16, D_MODEL), lambda l, j: (0, 0)),
            pl.BlockSpec((1, D_MODEL, tn), lambda l, j: (l, 0, j)),
            pl.BlockSpec((1, 1, tn), lambda l, j: (l, 0, j)),
        ],
        out_specs=pl.BlockSpec((1, 16, tn), lambda l, j: (l, 0, j)),
        compiler_params=_cparams(2),
        name="modulation",
    )(cc, ada_w, ada_b.reshape(DEPTH, 1, n6))


W_A = 768
W_G = 2 * GLA_DK + 2 * GLA_DV


def _token_specs(d, n_lat_tiles, ctx_blk):
    return [pl.BlockSpec((1, TM, d), lambda b, i: (b, jnp.minimum(i, n_lat_tiles - 1), 0)),
            pl.BlockSpec((1, TM, d), lambda b, i: (b, ctx_blk, 0))]


def _token_tile(x_ref, c_ref, n_lat_tiles):
    return jnp.where(pl.program_id(1) < n_lat_tiles, x_ref[0], c_ref[0])


def _inproj_kernel(x_ref, c_ref, mod_ref, g1_ref, wa_ref, wg_ref, wh_ref, wz_ref, qn_ref, kvn_ref, wuq_ref, wukv_ref,
                   wa2_ref, ba_ref, cos_ref, sin_ref,
                   q_out, k_out, v_out, gq_out, gk_out, gv_out, gr_out, glog_out, hy_out, gate_out, *, n_lat_tiles):
    x = _token_tile(x_ref, c_ref, n_lat_tiles)
    shift = mod_ref[0, 0:1, :]
    scale = mod_ref[0, 1:2, :]
    h = (_rms(x) * g1_ref[...] * (1.0 + scale) + shift).astype(BF16)

    za = _dot(h, wa_ref[...])
    cos = cos_ref[...]
    sin = sin_ref[...]

    cqn = (_rms(za[:, 0:256]) * qn_ref[...]).astype(BF16)
    qab = _dot(cqn, wuq_ref[...])
    nq = MLA_HEADS * HEAD_SLOT
    for hd in range(MLA_HEADS):
        sl = slice(hd * HEAD_SLOT, (hd + 1) * HEAD_SLOT)
        qa = qab[:, hd * HEAD_SLOT:(hd + 1) * HEAD_SLOT]
        qb = qab[:, nq + hd * HEAD_SLOT:nq + (hd + 1) * HEAD_SLOT]
        q_out[0, :, sl] = ((qa * cos + qb * sin) * (MLA_SCALE * LOG2E)).astype(BF16)

    ckvn = (_rms(za[:, 256:384]) * kvn_ref[...]).astype(BF16)
    kv = _dot(ckvn, wukv_ref[...])
    krot = za[:, 384:512] * cos + za[:, 512:640] * sin
    for hd in range(MLA_HEADS):
        sl = slice(hd * HEAD_SLOT, (hd + 1) * HEAD_SLOT)
        k_out[0, :, sl] = (kv[:, sl] + krot).astype(BF16)
    ones_hi = (lax.broadcasted_iota(jnp.int32, (1, HEAD_SLOT), 1) >= MLA_V).astype(F32)
    for hd in range(MLA_HEADS):
        sl = slice(hd * HEAD_SLOT, (hd + 1) * HEAD_SLOT)
        v_out[0, :, sl] = (kv[:, nq + hd * HEAD_SLOT:nq + (hd + 1) * HEAD_SLOT] + ones_hi).astype(BF16)

    xg = _dot(za[:, 640:768].astype(BF16), wa2_ref[...]) + ba_ref[...]
    glog_out[0] = (jnp.minimum(xg, 0.0) - jnp.log(1.0 + jnp.exp(-jnp.abs(xg)))) * (1.0 / GLA_TAU)

    zg = _dot(h, wg_ref[...])
    gq_out[0] = (zg[:, 0:GLA_DK] * (GLA_HK ** -0.5)).astype(BF16)
    gk_out[0] = zg[:, GLA_DK:2 * GLA_DK].astype(BF16)
    gv_out[0] = zg[:, 2 * GLA_DK:2 * GLA_DK + GLA_DV].astype(BF16)
    gr = zg[:, 2 * GLA_DK + GLA_DV:]
    gr_out[0] = (gr * _sigmoid(gr)).astype(BF16)

    hy_out[0] = _dot(h, wh_ref[...])
    gate_out[0] = _sigmoid(_dot(h, wz_ref[...])).astype(BF16)


def _rope_partner(w):
    a = MLA_ROPE // 4
    perm = np.concatenate([np.arange(a, 2 * a), np.arange(0, a), np.arange(3 * a, 4 * a), np.arange(2 * a, 3 * a)])
    sign = np.concatenate([-np.ones(a), np.ones(a), -np.ones(a), np.ones(a)]).astype(np.float32)
    return w[:, perm] * sign


def _prep_inproj_weights(w_in, mla_w_uq, mla_w_ukv, gla_w_a2, gla_b_a):
    offs = np.concatenate([[0], np.cumsum(IN_SIZES)])
    seg = [w_in[:, offs[i]:offs[i + 1]] for i in range(len(IN_SIZES))]
    w_cq, w_ckv, w_kr, w_gq, w_gk, w_gv, w_gr, w_af, w_ab, w_hy, w_gate = seg
    d = w_in.shape[0]
    z = lambda n: jnp.zeros((d, n), w_in.dtype)
    kr_tile = jnp.concatenate([z(MLA_NOPE), w_kr, z(HEAD_SLOT - MLA_NOPE - MLA_ROPE)], axis=1)
    krp_tile = jnp.concatenate([z(MLA_NOPE), _rope_partner(w_kr), z(HEAD_SLOT - MLA_NOPE - MLA_ROPE)], axis=1)
    a_tile = jnp.concatenate([w_af, w_ab, z(LANES - 2 * GLA_GATE_RANK)], axis=1)
    wa = jnp.concatenate([w_cq, w_ckv, kr_tile, krp_tile, a_tile], axis=1)
    wg = jnp.concatenate([w_gq, w_gk, w_gv, w_gr], axis=1)

    dh = MLA_NOPE + MLA_ROPE
    zq = lambda n: jnp.zeros((MLA_Q_LORA, n), w_in.dtype)
    plain, partner = [], []
    for hd in range(MLA_HEADS):
        blk = mla_w_uq[:, hd * dh:(hd + 1) * dh]
        plain += [blk, zq(HEAD_SLOT - dh)]
        partner += [zq(MLA_NOPE), _rope_partner(blk[:, MLA_NOPE:]), zq(HEAD_SLOT - dh)]
    wuq = jnp.concatenate(plain + partner, axis=1)

    zk = jnp.zeros((MLA_KV_LORA, HEAD_SLOT - MLA_NOPE), w_in.dtype)
    kcols, vcols = [], []
    for hd in range(MLA_HEADS):
        blk = mla_w_ukv[:, hd * (MLA_NOPE + MLA_V):(hd + 1) * (MLA_NOPE + MLA_V)]
        kcols += [blk[:, :MLA_NOPE], zk]
        vcols += [blk[:, MLA_NOPE:], zk]
    wukv = jnp.concatenate(kcols + vcols, axis=1)

    wa2 = jnp.zeros((LANES, 2 * GLA_DK), w_in.dtype)
    wa2 = wa2.at[0:GLA_GATE_RANK, 0:GLA_DK].set(gla_w_a2[0])
    wa2 = wa2.at[GLA_GATE_RANK:2 * GLA_GATE_RANK, GLA_DK:].set(gla_w_a2[1])
    ba = jnp.concatenate([gla_b_a[0], gla_b_a[1]])[None, :]
    bf = lambda t: t.astype(BF16)
    return bf(wa), bf(wg), bf(w_hy), bf(w_gate), bf(wuq), bf(wukv), bf(wa2), ba


def _rope_tables(seq, ctx_len):
    rows = seq // GRID_W
    row = np.repeat(np.arange(rows, dtype=np.float64), GRID_W)
    col = np.tile(np.arange(GRID_W, dtype=np.float64), rows)
    a = MLA_ROPE // 4
    inv = ROPE_BASE ** (-np.arange(a, dtype=np.float64) / a)
    ang_r = row[:, None] * inv
    ang_c = col[:, None] * inv
    cos32 = np.concatenate([np.cos(ang_r), np.cos(ang_r), np.cos(ang_c), np.cos(ang_c)], axis=1)
    sin32 = np.concatenate([np.sin(ang_r), np.sin(ang_r), np.sin(ang_c), np.sin(ang_c)], axis=1)
    pad_r = HEAD_SLOT - MLA_NOPE - MLA_ROPE
    cos = np.concatenate([np.ones((seq, MLA_NOPE)), cos32, np.zeros((seq, pad_r))], axis=1)
    sin = np.concatenate([np.zeros((seq, MLA_NOPE)), sin32, np.zeros((seq, pad_r))], axis=1)
    cos_c = np.concatenate([np.ones((ctx_len, MLA_NOPE + MLA_ROPE)), np.zeros((ctx_len, pad_r))], axis=1)
    sin_c = np.zeros((ctx_len, HEAD_SLOT))
    return (jnp.asarray(np.concatenate([cos, cos_c], axis=0), F32),
            jnp.asarray(np.concatenate([sin, sin_c], axis=0), F32))


def _inproj(x_lat, x_ctx, ctx_blk, mod_l, g1, weights, q_norm, kv_norm, cos, sin, n_lat_tiles):
    bsz, _, d = x_lat.shape
    nt = n_lat_tiles + 1
    t = nt * TM
    wa, wg, wh, wz, wuq, wukv, wa2, ba = weights
    tile = lambda w: pl.BlockSpec((1, TM, w), lambda b, i: (b, i, 0))
    row = lambda w: pl.BlockSpec((1, w), lambda b, i: (0, 0))
    mod_spec = pl.BlockSpec((1, 6, d), lambda b, i: (jnp.where(i < n_lat_tiles, b, bsz), 0, 0))
    tab = pl.BlockSpec((TM, HEAD_SLOT), lambda b, i: (i, 0))
    nq = MLA_HEADS * HEAD_SLOT
    sds = lambda w, dt: jax.ShapeDtypeStruct((bsz, t, w), dt)
    out_shape = (sds(nq, BF16), sds(nq, BF16), sds(nq, BF16), sds(GLA_DK, BF16), sds(GLA_DK, BF16),
                 sds(GLA_DV, BF16), sds(GLA_DV, BF16), sds(2 * GLA_DK, F32), sds(3 * HY_WIDTH, F32),
                 sds(3 * D_MODEL, BF16))
    out_specs = (tile(nq), tile(nq), tile(nq), tile(GLA_DK), tile(GLA_DK), tile(GLA_DV), tile(GLA_DV),
                 tile(2 * GLA_DK), tile(3 * HY_WIDTH), tile(3 * D_MODEL))
    return pl.pallas_call(
        functools.partial(_inproj_kernel, n_lat_tiles=n_lat_tiles),
        out_shape=out_shape,
        grid=(bsz, nt),
        in_specs=_token_specs(d, n_lat_tiles, ctx_blk) + [
            mod_spec, row(d), _const_spec(wa.shape), _const_spec(wg.shape), _const_spec(wh.shape),
            _const_spec(wz.shape), row(MLA_Q_LORA), row(MLA_KV_LORA), _const_spec(wuq.shape),
            _const_spec(wukv.shape), _const_spec(wa2.shape), row(2 * GLA_DK), tab, tab],
        out_specs=out_specs,
        compiler_params=_cparams(2, VMEM_LIMIT),
        name="inproj",
    )(x_lat, x_ctx, mod_l, g1[None, :], wa, wg, wh, wz, q_norm[None, :], kv_norm[None, :], wuq, wukv, wa2, ba,
      cos, sin)


ATT_TK = 1024


ATT_TQ = 1024


def _attn_kernel(q_ref, k_ref, v_ref, o_ref, m_ref, acc_ref, *, chunks):
    tq = q_ref.shape[1]
    m_ref[...] = jnp.full(m_ref.shape, -jnp.inf, F32)
    acc_ref[...] = jnp.zeros(acc_ref.shape, F32)
    for r0, size in chunks:
        k = k_ref[0, pl.ds(r0, size), :]
        v = v_ref[0, pl.ds(r0, size), :]
        for hd in range(2):
            sl = slice(hd * HEAD_SLOT, (hd + 1) * HEAD_SLOT)
            s = _dot_nt(q_ref[0, :, sl], k[:, sl])
            m_prev = m_ref[hd]
            m_new = jnp.maximum(m_prev, jnp.max(s, axis=1, keepdims=True))
            p = jnp.exp2((s - jnp.concatenate([m_new] * (size // LANES), axis=1)).astype(BF16))
            acc_ref[hd] = jnp.exp2(m_prev - m_new) * acc_ref[hd] + _dot(p, v[:, sl])
            m_ref[hd] = m_new
    a0 = acc_ref[0]
    a1 = acc_ref[1]
    lane = lax.broadcasted_iota(jnp.int32, (tq, HEAD_SLOT), 1)
    o0 = a0 / pltpu.roll(a0, MLA_V, 1)
    o1 = pltpu.roll(a1, MLA_V, 1) / a1
    o_ref[0] = jnp.where(lane < MLA_V, o0, o1).astype(o_ref.dtype)


def _attention(q, k, v, seq, with_ctx_queries):
    bsz, t, _ = q.shape
    ctx_len = t - seq
    assert seq % ATT_TK == 0 and seq % ATT_TQ == 0 and seq % ctx_len == 0 and 2 * MLA_V == HEAD_SLOT
    pair = 2 * HEAD_SLOT
    n_chunks = seq // ATT_TK
    chunks = tuple((j * ATT_TK, ATT_TK) for j in range(n_chunks - 1))
    chunks += (((n_chunks - 1) * ATT_TK, ATT_TK + ctx_len),)
    scratch = lambda tq: [pltpu.VMEM((2, tq, LANES), F32), pltpu.VMEM((2, tq, HEAD_SLOT), F32)]
    y = pl.pallas_call(
        functools.partial(_attn_kernel, chunks=chunks),
        out_shape=jax.ShapeDtypeStruct((bsz, seq, MLA_OUT), BF16),
        grid=(bsz, MLA_HEADS // 2, seq // ATT_TQ),
        in_specs=[
            pl.BlockSpec((1, ATT_TQ, pair), lambda b, hp, i: (b, i, hp)),
            pl.BlockSpec((1, t, pair), lambda b, hp, i: (b, 0, hp)),
            pl.BlockSpec((1, t, pair), lambda b, hp, i: (b, 0, hp)),
        ],
        out_specs=pl.BlockSpec((1, ATT_TQ, HEAD_SLOT), lambda b, hp, i: (b, i, hp)),
        scratch_shapes=scratch(ATT_TQ),
        compiler_params=_cparams(3, VMEM_LIMIT),
        name="mla_attention",
    )(q, k, v)
    if not with_ctx_queries:
        return y, None
    cblk = seq // ctx_len
    ctx_rows = lambda w: pl.BlockSpec((1, ctx_len, w), lambda b, hp: (b, cblk, hp))
    y_ctx = pl.pallas_call(
        functools.partial(_attn_kernel, chunks=((0, ctx_len),)),
        out_shape=jax.ShapeDtypeStruct((bsz, ctx_len, MLA_OUT), BF16),
        grid=(bsz, MLA_HEADS // 2),
        in_specs=[ctx_rows(pair), ctx_rows(pair), ctx_rows(pair)],
        out_specs=pl.BlockSpec((1, ctx_len, HEAD_SLOT), lambda b, hp: (b, 0, hp)),
        scratch_shapes=scratch(ctx_len),
        compiler_params=_cparams(2),
        name="mla_attention_ctx",
    )(q, k, v)
    return y, y_ctx


GLA_LEVELS = int(math.log2(TM))
GLA_SAFE_SPAN = 60.0


def _gla_level_matrices():
    i = np.arange(TM)[:, None]
    t = np.arange(TM)[None, :]
    fwd = [(t <= i)]
    for lv in range(GLA_LEVELS):
        m = TM >> (lv + 1)
        lo = (i // m) * m
        later = ((i // m) % 2) == 1
        q_part = later & (t >= lo) & (t <= i)
        k_part = (~later) & (t > i) & (t <= lo + m - 1)
        fwd.append(q_part | k_part)
    fwd = np.concatenate(fwd, axis=0).astype(np.float32)
    nb = 1 + GLA_LEVELS
    bwd = fwd.reshape(nb, TM, TM)[:, ::-1, ::-1].reshape(nb * TM, TM)
    return np.stack([fwd, bwd])


def _gla_kernel(qf_ref, kf_ref, vf_ref, gf_ref, qb_ref, kb_ref, vb_ref, gb_ref, mall_ref, of_ref, ob_ref, s_ref, a_ref):
    step = pl.program_id(1)

    @pl.when(step == 0)
    def _():
        s_ref[...] = jnp.zeros(s_ref.shape, F32)

    ins = ((qf_ref, kf_ref, vf_ref, gf_ref), (qb_ref, kb_ref, vb_ref, gb_ref))
    outs = (of_ref, ob_ref)
    qs, ks, vs, g2s, g_cums, g_tots = [], [], [], [], [], []
    for d in range(2):
        q_ref, k_ref, v_ref, g_ref = ins[d]
        qs.append(q_ref[0].astype(F32))
        ks.append(k_ref[0].astype(F32))
        vs.append(v_ref[0])
        g = g_ref[0]
        g2 = jnp.concatenate([g.astype(BF16), (g - g.astype(BF16).astype(F32)).astype(BF16)], axis=1)
        e2 = _dot(mall_ref[d, 0:TM, :], g2)
        g2s.append(g2)
        g_cums.append(e2[:, :GLA_DK] + e2[:, GLA_DK:])
        g_tots.append(jnp.sum(g, axis=0, keepdims=True))

    row = lax.broadcasted_iota(jnp.int32, (TM, TM), 0)
    col = lax.broadcasted_iota(jnp.int32, (TM, TM), 1)
    lane_head = lax.broadcasted_iota(jnp.int32, (TM, GLA_DK), 1) // GLA_HK
    tok = lax.broadcasted_iota(jnp.int32, (TM, GLA_DK), 0)
    eye = row == col

    def stack_heads(t):
        return jnp.concatenate([jnp.where(lane_head == hd, t, 0.0) for hd in range(GLA_HEADS)], axis=0).astype(BF16)

    span = jnp.maximum(jnp.max(-g_tots[0]), jnp.max(-g_tots[1]))

    @pl.when(span < GLA_SAFE_SPAN)
    def _():
        for d in range(2):
            res = _dot_nt(stack_heads(qs[d] * jnp.exp(g_cums[d])), (ks[d] * jnp.exp(-g_cums[d])).astype(BF16))
            seen = (col <= row) if d == 0 else (col >= row)
            for hd in range(GLA_HEADS):
                a_ref[d, hd] = jnp.where(seen, res[hd * TM:(hd + 1) * TM], 0.0)

    @pl.when(span >= GLA_SAFE_SPAN)
    def _():
        for d in range(2):
            q, k = qs[d], ks[d]
            e2l = _dot(mall_ref[d, TM:, :], g2s[d])
            e_lv = e2l[:, :GLA_DK] + e2l[:, GLA_DK:]
            res = _dot_nt(stack_heads(q), k.astype(BF16))
            for hd in range(GLA_HEADS):
                a_ref[d, hd] = jnp.where(eye, res[hd * TM:(hd + 1) * TM], 0.0)
            for lv in range(GLA_LEVELS):
                m = TM >> (lv + 1)
                w = jnp.exp(e_lv[lv * TM:(lv + 1) * TM])
                bit = (tok // m) % 2
                q_act = bit != d
                qt = jnp.where(q_act, q * w, 0.0)
                kt = jnp.where(q_act, 0.0, k * w).astype(BF16)
                res = _dot_nt(stack_heads(qt), kt)
                if m == TM // 2:
                    for hd in range(GLA_HEADS):
                        a_ref[d, hd] += res[hd * TM:(hd + 1) * TM]
                else:
                    same = (row // (2 * m)) == (col // (2 * m))
                    for hd in range(GLA_HEADS):
                        a_ref[d, hd] += jnp.where(same, res[hd * TM:(hd + 1) * TM], 0.0)

    same_head = (lax.broadcasted_iota(jnp.int32, (GLA_DK, GLA_DV), 0) // GLA_HK
                 == lax.broadcasted_iota(jnp.int32, (GLA_DK, GLA_DV), 1) // GLA_HV)
    for d in range(2):
        q, k, v, g_cum, g_tot = qs[d], ks[d], vs[d], g_cums[d], g_tots[d]
        s_old = s_ref[d]
        o_inter = _dot((q * jnp.exp(g_cum)).astype(BF16), s_old.astype(BF16))
        for hd in range(GLA_HEADS):
            sl = slice(hd * GLA_HV, (hd + 1) * GLA_HV)
            o_intra = _dot(a_ref[d, hd].astype(BF16), v[:, sl])
            outs[d][0, :, sl] = (o_intra + o_inter[:, sl]).astype(outs[d].dtype)

        kdec_t = (k * jnp.exp(g_tot - g_cum)).T.astype(BF16)
        upd = _dot(kdec_t, v)
        dec_col = jnp.sum(jnp.where(eye, jnp.broadcast_to(jnp.exp(g_tot), (TM, GLA_DK)), 0.0), axis=1,
                          keepdims=True)
        s_ref[d] = dec_col * s_old + jnp.where(same_head, upd, 0.0)


def _gla(gq, gk, gv, glog, n_lat_tiles):
    assert GLA_DK == TM
    bsz, t, _ = gq.shape
    nt = t // TM
    mall = jnp.asarray(_gla_level_matrices(), dtype=BF16)

    def tile_idx(d, s):
        return jnp.where(s == 0, n_lat_tiles, s - 1 if d == 0 else n_lat_tiles - s)

    def specs(d):
        tok = lambda w: pl.BlockSpec((1, TM, w), lambda b, s: (b, tile_idx(d, s), 0))
        return [tok(GLA_DK), tok(GLA_DK), tok(GLA_DV), pl.BlockSpec((1, TM, GLA_DK), lambda b, s: (b, tile_idx(d, s), d))]

    out_spec = lambda d: pl.BlockSpec((1, TM, GLA_DV), lambda b, s: (b, tile_idx(d, s), 0))
    sds = jax.ShapeDtypeStruct((bsz, t, GLA_DV), BF16)
    return pl.pallas_call(
        _gla_kernel,
        out_shape=(sds, sds),
        grid=(bsz, nt),
        in_specs=specs(0) + specs(1) + [_const_spec(mall.shape)],
        out_specs=(out_spec(0), out_spec(1)),
        scratch_shapes=[pltpu.VMEM((2, GLA_DK, GLA_DV), F32), pltpu.VMEM((2, GLA_HEADS, TM, TM), F32)],
        compiler_params=_cparams(2),
        name="gla_scan",
    )(gq, gk, gv, glog, gq, gk, gv, glog, mall)


HY_N1 = 64
HY_N2 = 128
HY_SLAB = SUBLANES
HY_CB = 256
HY_KG = 16
HY_SPEC_CB = 128
HY_UNROLL = 16


def _hyena_dft_constants(seq):
    n = 2 * seq
    assert n == HY_N1 * HY_N2
    nh = HY_N1 // 2
    kh = HY_N2 // 2
    eye = np.eye(HY_SLAB)
    k1 = np.arange(HY_N1)
    th = 2 * np.pi * np.outer(k1 + 0.5, np.arange(nh)) / HY_N1
    fwd1 = np.concatenate([np.kron(np.cos(th), eye), np.kron(-np.sin(th), eye)], axis=0)
    inv1 = (2.0 / n) * np.concatenate([np.kron(np.cos(th).T, eye), np.kron(-np.sin(th).T, eye)], axis=1)
    nlo = np.arange(HY_N2)
    k2 = np.arange(kh)
    ph = 2 * np.pi * (k2[None, :, None] * nlo[None, None, :] / HY_N2
                      + (k1[:, None, None] + 0.5) * nlo[None, None, :] / n)
    c, s = np.cos(ph), np.sin(ph)
    fwd2 = np.concatenate([np.concatenate([c, s], axis=2), np.concatenate([-s, c], axis=2)], axis=1)
    ct, st = c.transpose(0, 2, 1), s.transpose(0, 2, 1)
    inv2 = np.concatenate([np.concatenate([ct, -st], axis=2), np.concatenate([st, ct], axis=2)], axis=1)
    return fwd1, fwd2, inv2, inv1


def _hyena_ctx_dft_constants(ctx_len):
    n = 2 * ctx_len
    th = 2 * np.pi * np.outer(np.arange(ctx_len) + 0.5, np.arange(ctx_len)) / n
    fwd = np.concatenate([np.cos(th), -np.sin(th)], axis=0)
    inv = (2.0 / n) * np.concatenate([np.cos(th).T, -np.sin(th).T], axis=1)
    return fwd, inv


def _hyena_features(length):
    pos = np.arange(length, dtype=np.float64)
    t = pos / max(length - 1, 1)
    f = np.linspace(1e-4, HY_BANDS - 1, HY_BANDS)
    ang = (2.0 * math.pi / length) * pos[:, None] * f
    feat = np.concatenate([t[:, None], np.cos(ang), np.sin(ang)], axis=-1)
    return jnp.asarray(np.pad(feat, ((0, 0), (0, LANES - HY_POS_DIM))), F32)


def _hy_filter_kernel(feat_ref, w1_ref, b1_ref, w2_ref, b2_ref, w3_ref, b3_ref, absd_ref, h_ref, s_ref):
    i = pl.program_id(0)
    feat = feat_ref[...]
    hp = _dot3
    hdn = jnp.sin(hp(feat, w1_ref[...]) + b1_ref[...])
    hdn = jnp.sin(hp(hdn, w2_ref[...]) + b2_ref[...])
    h = hp(hdn, w3_ref[...]) + b3_ref[...]
    window = jnp.exp(-feat[:, 0:1] * absd_ref[...])
    h = h * jnp.concatenate([window] * (2 * HY_ORDER), axis=1)
    h_ref[...] = h

    @pl.when(i == 0)
    def _():
        s_ref[...] = jnp.zeros(s_ref.shape, F32)

    s_ref[...] += jnp.sum(jnp.abs(h), axis=0, keepdims=True)


def _hyena_filters_raw(length, filt_w):
    w1, b1, w2, b2, w3, b3 = filt_w
    nf = 2 * HY_ORDER * HY_WIDTH
    tr = min(length, 512)
    deltas = np.linspace(math.log(HY_DECAY_TARGET) / HY_FAST_DECAY, math.log(HY_DECAY_TARGET) / HY_SLOW_DECAY,
                         HY_WIDTH, dtype=np.float32)
    absd = jnp.asarray(np.abs(deltas))[None, :]
    w1p = jnp.pad(w1, ((0, LANES - HY_POS_DIM), (0, 0)))
    full = lambda shp: pl.BlockSpec(shp, lambda i: (0,) * len(shp))
    return pl.pallas_call(
        _hy_filter_kernel,
        out_shape=(jax.ShapeDtypeStruct((length, nf), F32), jax.ShapeDtypeStruct((1, nf), F32)),
        grid=(length // tr,),
        in_specs=[pl.BlockSpec((tr, LANES), lambda i: (i, 0)), full((LANES, HY_FILTER_HIDDEN)),
                  full((1, HY_FILTER_HIDDEN)), full((HY_FILTER_HIDDEN, HY_FILTER_HIDDEN)), full((1, HY_FILTER_HIDDEN)),
                  full((HY_FILTER_HIDDEN, nf)), full((1, nf)), full((1, HY_WIDTH))],
        out_specs=(pl.BlockSpec((tr, nf), lambda i: (i, 0)), full((1, nf))),
        compiler_params=_cparams(1),
        name="hyena_filter_mlp",
    )(_hyena_features(length), w1p, b1[None, :], w2, b2[None, :], w3, b3[None, :], absd)


def _dot_split(m, x):
    x_hi = x.astype(BF16)
    x_lo = (x - x_hi.astype(F32)).astype(BF16)
    n = x.shape[1]
    r = _dot(m, jnp.concatenate([x_hi, x_lo], axis=1))
    return r[:, :n] + r[:, n:]


def _odft_stage1(src_at, mm, s_re, s_im, unroll=2):
    nk = s_re.shape[0]
    half = nk * HY_SLAB

    def body(j, carry):
        r0 = pl.multiple_of(j * HY_SLAB, HY_SLAB)
        slab = src_at(r0)
        cb = slab.shape[-1]
        res = mm(slab.reshape(-1, cb))
        s_re[:, pl.ds(r0, HY_SLAB), :] = res[:half].reshape(nk, HY_SLAB, cb)
        s_im[:, pl.ds(r0, HY_SLAB), :] = res[half:].reshape(nk, HY_SLAB, cb)
        return carry

    lax.fori_loop(0, HY_N2 // HY_SLAB, body, 0, unroll=unroll)


def _hy_spectrum_kernel(hf_ref, hb_ref, sf_ref, sb_ref, fwd1_ref, fwd2_ref, o_ref, s_re, s_im):
    kh = HY_N2 // 2
    mm1 = lambda x: _dot_split(fwd1_ref[...], x)

    def middle(sign):
        def body(k1, carry):
            a = jnp.concatenate([s_re[k1], s_im[k1]], axis=0)
            x = _dot_split(fwd2_ref[k1], a)
            if sign is None:
                o_ref[0, k1] = x[:kh]
                o_ref[1, k1] = x[kh:]
            else:
                inv_norm = 1.0 / (sf_ref[...] + sb_ref[...])
                o_ref[0, k1] = (o_ref[0, k1] + x[:kh]) * inv_norm
                o_ref[1, k1] = (o_ref[1, k1] - x[kh:]) * inv_norm
            return carry
        lax.fori_loop(0, HY_N1, body, 0, unroll=8)

    _odft_stage1(lambda r0: hf_ref[:, pl.ds(r0, HY_SLAB), :], mm1, s_re, s_im, unroll=4)
    middle(None)

    def bwd_slab(r0):
        slab = hb_ref[:, pl.ds(r0, HY_SLAB), :]
        nhi = lax.broadcasted_iota(jnp.int32, slab.shape, 0)
        r = lax.broadcasted_iota(jnp.int32, slab.shape, 1)
        return jnp.where((nhi == 0) & (r + r0 == 0), 0.0, slab)

    _odft_stage1(bwd_slab, mm1, s_re, s_im, unroll=4)
    middle(-1)


def _hyena_filter_spectrum(seq, w1, b1, w2, b2, w3, b3):
    h_raw, s = _hyena_filters_raw(seq, (w1, b1, w2, b2, w3, b3))
    nh = HY_N1 // 2
    nc = HY_ORDER * HY_WIDTH
    h3 = h_raw.reshape(nh, HY_N2, 2 * nc)
    fwd1, fwd2, _, _ = _hyena_dft_constants(seq)
    fwd1 = jnp.asarray(fwd1, BF16)
    fwd2 = jnp.asarray(fwd2, BF16)
    scb = HY_SPEC_CB
    ncb = nc // scb
    return pl.pallas_call(
        _hy_spectrum_kernel,
        out_shape=jax.ShapeDtypeStruct((2, HY_N1, HY_N2 // 2, nc), F32),
        grid=(ncb,),
        in_specs=[pl.BlockSpec((nh, HY_N2, scb), lambda c: (0, 0, c)),
                  pl.BlockSpec((nh, HY_N2, scb), lambda c: (0, 0, ncb + c)),
                  pl.BlockSpec((1, scb), lambda c: (0, c)),
                  pl.BlockSpec((1, scb), lambda c: (0, ncb + c)),
                  _const_spec(fwd1.shape), _const_spec(fwd2.shape)],
        out_specs=pl.BlockSpec((2, HY_N1, HY_N2 // 2, scb), lambda c: (0, 0, 0, c)),
        scratch_shapes=[pltpu.VMEM((HY_N1, HY_N2, scb), F32), pltpu.VMEM((HY_N1, HY_N2, scb), F32)],
        compiler_params=_cparams(1, VMEM_LIMIT),
        name="hyena_filter_spectrum",
    )(h3, h3, s, s, fwd1, fwd2)


def _short_conv_chunk(ref, c, n_chunks, w_ref, b_ref):
    per = TM // HY_N2
    cur = ref[0, pl.ds(per * c, per)]
    cb = cur.shape[-1]
    cur = cur.reshape(TM, cb)
    prev = ref[0, jnp.maximum(per * c - 1, 0), pl.ds(HY_N2 - SUBLANES, SUBLANES), :][SUBLANES - 1:SUBLANES]
    nxt = ref[0, jnp.minimum(per * c + per, per * n_chunks - 1), pl.ds(0, SUBLANES), :][0:1]
    prev = jnp.where(c > 0, prev, 0.0)
    nxt = jnp.where(c < n_chunks - 1, nxt, 0.0)
    rowi = lax.broadcasted_iota(jnp.int32, (TM, cb), 0)
    dn = jnp.where(rowi == 0, prev, pltpu.roll(cur, 1, 0))
    up = jnp.where(rowi == TM - 1, nxt, pltpu.roll(cur, TM - 1, 0))
    return b_ref[...] + w_ref[0:1, :] * dn + w_ref[1:2, :] * cur + w_ref[2:3, :] * up


def _hy_conv_kernel(*refs, conv_y):
    if conv_y:
        (y_ref, g_ref, h_ref, bias_ref, wy_ref, by_ref, wg_ref, bg_ref, fwd1_ref, fwd2_ref, inv2_ref, inv1_ref,
         o_ref, s_re, s_im, gs_ref, us_ref) = refs
    else:
        (y_ref, g_ref, h_ref, bias_ref, wg_ref, bg_ref, fwd1_ref, fwd2_ref, inv2_ref, inv1_ref,
         o_ref, s_re, s_im, gs_ref) = refs
    grp = pl.program_id(2)
    last = pl.num_programs(2) - 1
    nh = HY_N1 // 2
    per = TM // HY_N2
    n_chunks = nh // per
    cb = o_ref.shape[-1]
    kh = HY_N2 // 2

    @pl.when(grp == 0)
    def _():
        def pre(c, carry):
            gs_ref[pl.ds(per * c, per)] = _short_conv_chunk(g_ref, c, n_chunks, wg_ref, bg_ref).reshape(per, HY_N2, cb)
            if conv_y:
                us_ref[pl.ds(per * c, per)] = _short_conv_chunk(y_ref, c, n_chunks, wy_ref, by_ref).reshape(
                    per, HY_N2, cb)
            return carry
        lax.fori_loop(0, n_chunks, pre, 0)

    if conv_y:
        u_at = lambda r0: us_ref[:, pl.ds(r0, HY_SLAB), :]
    else:
        u_at = lambda r0: y_ref[0, :, pl.ds(r0, HY_SLAB), :]

    _odft_stage1(u_at, lambda x: _dot(fwd1_ref[0], x.astype(BF16)), s_re, s_im, unroll=HY_UNROLL)

    def middle(k1, carry):
        a = jnp.concatenate([s_re[k1], s_im[k1]], axis=0).astype(BF16)
        x = _dot(fwd2_ref[k1], a)
        xr, xi = x[:kh], x[kh:]
        hr, hi = h_ref[0, k1], h_ref[1, k1]
        y = jnp.concatenate([xr * hr - xi * hi, xr * hi + xi * hr], axis=0).astype(BF16)
        bm = _dot(inv2_ref[k1], y)
        s_re[k1] = bm[:HY_N2]
        s_im[k1] = bm[HY_N2:]
        return carry

    lax.fori_loop(0, HY_KG, middle, 0, unroll=HY_KG)

    def partial_conv(r0):
        slab = jnp.concatenate([s_re[:, pl.ds(r0, HY_SLAB), :].reshape(HY_KG * HY_SLAB, cb),
                                s_im[:, pl.ds(r0, HY_SLAB), :].reshape(HY_KG * HY_SLAB, cb)], axis=0).astype(BF16)
        return _dot(inv1_ref[0], slab).reshape(nh, HY_SLAB, cb)

    def post_loop(fn):
        def post(j, carry):
            r0 = pl.multiple_of(j * HY_SLAB, HY_SLAB)
            o_ref[0, :, pl.ds(r0, HY_SLAB), :] = fn(r0, partial_conv(r0))
            return carry
        lax.fori_loop(0, HY_N2 // HY_SLAB, post, 0, unroll=HY_UNROLL)

    @pl.when(grp == 0)
    def _():
        post_loop(lambda r0, part: part)

    @pl.when((grp > 0) & (grp < last))
    def _():
        post_loop(lambda r0, part: o_ref[0, :, pl.ds(r0, HY_SLAB), :] + part)

    @pl.when(grp == last)
    def _():
        post_loop(lambda r0, part: gs_ref[:, pl.ds(r0, HY_SLAB), :]
                  * (o_ref[0, :, pl.ds(r0, HY_SLAB), :] + part + bias_ref[...] * u_at(r0)))


def _hyena_order(y4, y_col0, z4, gate_col0, hspec, order, hy_bias, short_w, short_b, consts, conv_y):
    bsz = z4.shape[0]
    nh = HY_N1 // 2
    ncb = HY_WIDTH // HY_CB
    ngrp = HY_N1 // HY_KG
    fwd1, fwd2, inv2, inv1 = consts
    blk4 = lambda off: pl.BlockSpec((1, nh, HY_N2, HY_CB), lambda c, b, g: (b, 0, 0, off + c))
    rowspec = lambda rows, off: pl.BlockSpec((rows, HY_CB), lambda c, b, g: (0, off + c))
    grouped = lambda shp: pl.BlockSpec(shp, lambda c, b, g: (g, 0, 0))
    in_specs = [blk4(y_col0), blk4(gate_col0),
                pl.BlockSpec((2, HY_KG, HY_N2 // 2, HY_CB), lambda c, b, g: (0, g, 0, order * ncb + c)),
                rowspec(1, 0)]
    args = [y4, z4, hspec, hy_bias.reshape(1, -1)]
    if conv_y:
        in_specs += [rowspec(HY_SHORT, y_col0), rowspec(1, y_col0)]
        args += [short_w, short_b[None, :]]
    in_specs += [rowspec(HY_SHORT, gate_col0), rowspec(1, gate_col0)]
    args += [short_w, short_b[None, :]]
    in_specs += [grouped((1,) + fwd1.shape[1:]), grouped((HY_KG,) + fwd2.shape[1:]),
                 grouped((HY_KG,) + inv2.shape[1:]), grouped((1,) + inv1.shape[1:])]
    args += [fwd1, fwd2, inv2, inv1]
    scratch = [pltpu.VMEM((HY_KG, HY_N2, HY_CB), F32), pltpu.VMEM((HY_KG, HY_N2, HY_CB), F32),
               pltpu.VMEM((nh, HY_N2, HY_CB), F32)]
    if conv_y:
        scratch.append(pltpu.VMEM((nh, HY_N2, HY_CB), F32))
    return pl.pallas_call(
        functools.partial(_hy_conv_kernel, conv_y=conv_y),
        out_shape=jax.ShapeDtypeStruct((bsz, nh, HY_N2, HY_WIDTH), F32),
        grid=(ncb, bsz, ngrp),
        in_specs=in_specs,
        out_specs=pl.BlockSpec((1, nh, HY_N2, HY_CB), lambda c, b, g: (b, 0, 0, c)),
        scratch_shapes=scratch,
        compiler_params=_cparams(3, VMEM_LIMIT),
        name="hyena_conv%d" % order,
    )(*args)


def _hyena_latent(z_hy, hspec, short_w, short_b, hy_bias):
    bsz, t, _ = z_hy.shape
    seq = HY_N1 * HY_N2 // 2
    fwd1, fwd2, inv2, inv1 = _hyena_dft_constants(seq)
    ngrp = HY_N1 // HY_KG
    rows = HY_KG * HY_SLAB
    fwd1 = fwd1.reshape(2, ngrp, rows, -1).transpose(1, 0, 2, 3).reshape(ngrp, 2 * rows, -1)
    inv1 = inv1.reshape(-1, 2, ngrp, rows).transpose(2, 0, 1, 3).reshape(ngrp, -1, 2 * rows)
    consts = tuple(jnp.asarray(m, BF16) for m in (fwd1, fwd2, inv2, inv1))
    z4 = z_hy.reshape(bsz, t // HY_N2, HY_N2, 3 * HY_WIDTH)
    ncb = HY_WIDTH // HY_CB
    y1 = _hyena_order(z4, 2 * ncb, z4, 0, hspec, 0, hy_bias[0], short_w, short_b, consts, True)
    y2 = _hyena_order(y1, 0, z4, ncb, hspec, 1, hy_bias[1], short_w, short_b, consts, False)
    return y2.reshape(bsz, seq, HY_WIDTH)


def _hy_ctx_spectrum_kernel(h_ref, s_ref, fwd_ref, o_ref):
    lc = h_ref.shape[0]
    nc = HY_ORDER * HY_WIDTH
    hp = _dot3
    h = h_ref[...]
    rowi = lax.broadcasted_iota(jnp.int32, (lc, nc), 0)
    xf = hp(fwd_ref[...], h[:, :nc])
    xb = hp(fwd_ref[...], jnp.where(rowi == 0, 0.0, h[:, nc:]))
    inv_norm = 1.0 / (s_ref[:, :nc] + s_ref[:, nc:])
    o_ref[0] = (xf[:lc] + xb[:lc]) * inv_norm
    o_ref[1] = (xf[lc:] - xb[lc:]) * inv_norm


def _hy_ctx_conv_kernel(x1_ref, x2_ref, v_ref, h_ref, bias_ref, w_ref, b_ref, fwd_ref, inv_ref, o_ref):
    lc = o_ref.shape[1]
    rowi = lax.broadcasted_iota(jnp.int32, (lc, HY_WIDTH), 0)

    def short(ref, part):
        cur = ref[0].reshape(lc, HY_WIDTH)
        sl = slice(part * HY_WIDTH, (part + 1) * HY_WIDTH)
        dn = jnp.where(rowi == 0, 0.0, pltpu.roll(cur, 1, 0))
        up = jnp.where(rowi == lc - 1, 0.0, pltpu.roll(cur, lc - 1, 0))
        return b_ref[:, sl] + w_ref[0:1, sl] * dn + w_ref[1:2, sl] * cur + w_ref[2:3, sl] * up

    y = short(v_ref, 2)
    for order, gref in enumerate((x1_ref, x2_ref)):
        sl = slice(order * HY_WIDTH, (order + 1) * HY_WIDTH)
        x = _dot(fwd_ref[...], y.astype(BF16))
        xr, xi = x[:lc], x[lc:]
        hr, hi = h_ref[0, :, sl], h_ref[1, :, sl]
        prod = jnp.concatenate([xr * hr - xi * hi, xr * hi + xi * hr], axis=0).astype(BF16)
        conv = _dot(inv_ref[...], prod)
        y = short(gref, order) * (conv + bias_ref[order:order + 1, :] * y)
    o_ref[0] = y


def _hyena_ctx(z_hy, filt_w, short_w, short_b, hy_bias):
    bsz, t, _ = z_hy.shape
    seq = HY_N1 * HY_N2 // 2
    lc = t - seq
    per = lc // HY_N2
    h_raw, s = _hyena_filters_raw(lc, filt_w)
    fwd, inv = _hyena_ctx_dft_constants(lc)
    nc = HY_ORDER * HY_WIDTH
    full = lambda shp: pl.BlockSpec(shp, lambda *_: (0,) * len(shp))
    hspec = pl.pallas_call(
        _hy_ctx_spectrum_kernel,
        out_shape=jax.ShapeDtypeStruct((2, lc, nc), F32),
        grid=(1,),
        in_specs=[full(h_raw.shape), full(s.shape), full(fwd.shape)],
        out_specs=full((2, lc, nc)),
        compiler_params=_cparams(1),
        name="hyena_ctx_spectrum",
    )(h_raw, s, jnp.asarray(fwd, F32))
    z4 = z_hy.reshape(bsz, t // HY_N2, HY_N2, 3 * HY_WIDTH)
    blk = lambda part: pl.BlockSpec((1, per, HY_N2, HY_WIDTH), lambda b: (b, seq // lc, 0, part))
    return pl.pallas_call(
        _hy_ctx_conv_kernel,
        out_shape=jax.ShapeDtypeStruct((bsz, lc, HY_WIDTH), F32),
        grid=(bsz,),
        in_specs=[blk(0), blk(1), blk(2), full((2, lc, nc)), full((HY_ORDER, HY_WIDTH)),
                  full((HY_SHORT, 3 * HY_WIDTH)), full((1, 3 * HY_WIDTH)), full(fwd.shape), full(inv.shape)],
        out_specs=pl.BlockSpec((1, lc, HY_WIDTH), lambda b: (b, 0, 0)),
        compiler_params=_cparams(1),
        name="hyena_ctx_conv",
    )(z4, z4, z4, hspec, hy_bias, short_w, short_b[None, :], jnp.asarray(fwd, BF16), jnp.asarray(inv, BF16))


FF_CHUNK = 1024


def _merge_mlp_kernel(x_ref, c_ref, mod_ref, ymla_ref, ymlac_ref, of_ref, ob_ref, gr_ref, yhy_ref, yhyc_ref, zg_ref, on_ref,
                      wm_ref, wgl_ref, wh_ref, wo_ref, g2_ref, w1_ref, w2_ref, fg_ref, o_ref, *, final, n_lat_tiles):
    o = of_ref[0].astype(F32) + ob_ref[0].astype(F32)
    silu = gr_ref[0].astype(F32)
    parts = []
    for hd in range(GLA_HEADS):
        sl = slice(hd * GLA_HV, (hd + 1) * GLA_HV)
        parts.append((_rms(o[:, sl]) * on_ref[...] * silu[:, sl]).astype(BF16))
    y_gla = jnp.concatenate(parts, axis=1)
    zg = zg_ref[0].astype(F32)
    d = x_ref.shape[-1]
    m = zg[:, 0:d] * _dot(_token_tile(ymla_ref, ymlac_ref, n_lat_tiles), wm_ref[...])
    m = m + zg[:, d:2 * d] * _dot(y_gla, wgl_ref[...])
    m = m + zg[:, 2 * d:3 * d] * _dot(_token_tile(yhy_ref, yhyc_ref, n_lat_tiles).astype(BF16), wh_ref[...])
    out = _dot(m.astype(BF16), wo_ref[...])
    x = _token_tile(x_ref, c_ref, n_lat_tiles) + mod_ref[0, 2:3, :] * out

    h = (_rms(x) * g2_ref[...] * (1.0 + mod_ref[0, 4:5, :]) + mod_ref[0, 3:4, :]).astype(BF16)
    acc = jnp.zeros(x.shape, F32)
    for j in range(w1_ref.shape[1] // FF_CHUNK):
        a = jnp.maximum(_dot(h, w1_ref[:, j * FF_CHUNK:(j + 1) * FF_CHUNK]), 0.0)
        acc = acc + _dot((a * a).astype(BF16), w2_ref[j * FF_CHUNK:(j + 1) * FF_CHUNK, :])
    xn = x + mod_ref[0, 5:6, :] * acc
    if final:
        xn = _rms(xn) * fg_ref[...]
    o_ref[0] = xn


def _mod_spec(d, n_lat_tiles, bsz):
    return pl.BlockSpec((1, 6, d), lambda b, i: (jnp.where(i < n_lat_tiles, b, bsz), 0, 0))


def _merge_mlp(x_lat, x_ctx, ctx_blk, mod_l, y_mla, y_mla_ctx, o_gla, gr, y_hy, y_hy_ctx, gate, out_norm,
               w_o_mla, w_o_gla, w_o_hy, w_out, g2, w1, w2, final_g, n_tiles, n_lat_tiles, final):
    bsz, _, d = x_lat.shape
    if y_mla_ctx is None:
        y_mla_ctx, y_hy_ctx = y_mla, y_hy
    tile = lambda w: pl.BlockSpec((1, TM, w), lambda b, i: (b, i, 0))
    row = lambda w: pl.BlockSpec((1, w), lambda b, i: (0, 0))
    bf = lambda w: w.astype(BF16)
    return pl.pallas_call(
        functools.partial(_merge_mlp_kernel, final=final, n_lat_tiles=n_lat_tiles),
        out_shape=jax.ShapeDtypeStruct((bsz, n_tiles * TM, d), F32),
        grid=(bsz, n_tiles),
        in_specs=_token_specs(d, n_lat_tiles, ctx_blk) + [_mod_spec(d, n_lat_tiles, bsz)]
        + _token_specs(MLA_OUT, n_lat_tiles, 0) + [tile(GLA_DV), tile(GLA_DV), tile(GLA_DV)]
        + _token_specs(HY_WIDTH, n_lat_tiles, 0) + [
            tile(3 * d), row(GLA_HV),
            _const_spec(w_o_mla.shape), _const_spec(w_o_gla.shape), _const_spec(w_o_hy.shape),
            _const_spec(w_out.shape), row(d), _const_spec(w1.shape), _const_spec(w2.shape), row(d)],
        out_specs=tile(d),
        compiler_params=_cparams(2, VMEM_LIMIT),
        name="merge_mlp",
    )(x_lat, x_ctx, mod_l, y_mla, y_mla_ctx, o_gla[0], o_gla[1], gr, y_hy, y_hy_ctx, gate, out_norm[None, :],
      bf(w_o_mla), bf(w_o_gla), bf(w_o_hy), bf(w_out), g2[None, :], bf(w1), bf(w2), final_g[None, :])


def kernel(x, c, ctx, c_ctx, ada_w, ada_b, norm1_g, norm2_g, w_in, mla_q_norm, mla_w_uq, mla_kv_norm, mla_w_ukv, gla_w_a2, gla_b_a, gla_out_norm, hy_short_w, hy_short_b, hy_f_w1, hy_f_b1, hy_f_w2, hy_f_b2, hy_f_w3, hy_f_b3, hy_bias, w_o_mla, w_o_gla, w_o_hy, w_out, ff_w1, ff_w2, final_norm_g):
    bsz, seq, d = x.shape
    ctx_len = ctx.shape[1]
    n_lat = seq // TM
    assert ctx_len == TM and seq % TM == 0
    n_all = n_lat + 1
    x_lat, x_ctx, ctx_blk = x, ctx, 0
    cc = jnp.zeros((16, d), F32).at[:bsz].set(c).at[bsz].set(c_ctx)
    mod = _modulation(cc, ada_w, ada_b).reshape(DEPTH, 16, 6, d)
    cos, sin = _rope_tables(seq, ctx_len)
    for l in range(DEPTH):
        last = l == DEPTH - 1
        n_tiles = n_lat if last else n_all
        weights = _prep_inproj_weights(w_in[l], mla_w_uq[l], mla_w_ukv[l], gla_w_a2[l], gla_b_a[l])
        q, k, v, gq, gk, gv, gr, glog, z_hy, z_gate = _inproj(x_lat, x_ctx, ctx_blk, mod[l], norm1_g[l], weights,
                                                             mla_q_norm[l], mla_kv_norm[l], cos, sin, n_lat)
        y_mla, y_mla_c = _attention(q, k, v, seq, not last)
        o_gla = _gla(gq, gk, gv, glog, n_lat)
        filt_w = (hy_f_w1[l], hy_f_b1[l], hy_f_w2[l], hy_f_b2[l], hy_f_w3[l], hy_f_b3[l])
        hspec = _hyena_filter_spectrum(seq, *filt_w)
        y_hy = _hyena_latent(z_hy, hspec, hy_short_w[l], hy_short_b[l], hy_bias[l])
        y_hy_c = None if last else _hyena_ctx(z_hy, filt_w, hy_short_w[l], hy_short_b[l], hy_bias[l])
        xc = _merge_mlp(x_lat, x_ctx, ctx_blk, mod[l], y_mla, y_mla_c, o_gla, gr, y_hy, y_hy_c, z_gate,
                        gla_out_norm[l], w_o_mla[l], w_o_gla[l], w_o_hy[l], w_out[l], norm2_g[l], ff_w1[l], ff_w2[l],
                        final_norm_g, n_tiles, n_lat, last)
        x_lat, x_ctx, ctx_blk = xc, xc, n_lat
    return xc
```

```python
import functools
import math

import numpy as np
import jax
import jax.numpy as jnp
from jax import lax
from jax.experimental import pallas as pl
from jax.experimental.pallas import tpu as pltpu

F32 = jnp.float32
BF16 = jnp.bfloat16
LOG2E = 1.4426950408889634

D_MODEL = 1024
DEPTH = 2
GRID_W = 64
EPS = 1e-6
MLA_HEADS = 8
MLA_NOPE = 64
MLA_ROPE = 32
MLA_V = 64
MLA_Q_LORA = 256
MLA_KV_LORA = 128
MLA_SCALE = (MLA_NOPE + MLA_ROPE) ** -0.5
ROPE_BASE = 10000.0
GLA_HEADS = 4
GLA_DK = 256
GLA_DV = 512
GLA_HK = GLA_DK // GLA_HEADS
GLA_HV = GLA_DV // GLA_HEADS
GLA_GATE_RANK = 16
GLA_TAU = 16.0
HY_WIDTH = 512
HY_ORDER = 2
HY_SHORT = 3
HY_BANDS = 16
HY_POS_DIM = 1 + 2 * HY_BANDS
HY_FILTER_HIDDEN = 64
HY_FAST_DECAY = 0.3
HY_SLOW_DECAY = 1.5
HY_DECAY_TARGET = 1e-2
D_FF = 4 * D_MODEL
MLA_OUT = MLA_HEADS * MLA_V
IN_SIZES = (MLA_Q_LORA, MLA_KV_LORA, MLA_ROPE, GLA_DK, GLA_DK, GLA_DV, GLA_DV, GLA_GATE_RANK, GLA_GATE_RANK,
            (HY_ORDER + 1) * HY_WIDTH, 3 * D_MODEL)

LANES = 128
SUBLANES = 8
TM = 256
HEAD_SLOT = 128
V7X_VMEM_BYTES = 64 * 1024 * 1024
VMEM_LIMIT = V7X_VMEM_BYTES * 7 // 8


def _cparams(n_axes, vmem=None):
    return pltpu.CompilerParams(dimension_semantics=("arbitrary",) * n_axes, vmem_limit_bytes=vmem)


def _const_spec(shape):
    nd = len(shape)
    return pl.BlockSpec(shape, lambda *_: (0,) * nd, pipeline_mode=pl.Buffered(1))


def _rms(x):
    return x * lax.rsqrt(jnp.mean(x * x, axis=-1, keepdims=True) + EPS)


def _sigmoid(x):
    return 1.0 / (1.0 + jnp.exp(-x))


def _dot(a, b):
    return jnp.dot(a, b, preferred_element_type=F32)


def _dot_nt(a, b):
    return lax.dot_general(a, b, (((1,), (1,)), ((), ())), preferred_element_type=F32)


def _dot3(a, b):
    a_hi = a.astype(BF16)
    a_lo = (a - a_hi.astype(F32)).astype(BF16)
    b_hi = b.astype(BF16)
    b_lo = (b - b_hi.astype(F32)).astype(BF16)
    m = a.shape[0]
    r = _dot(jnp.concatenate([a_hi, a_lo], axis=0), b_hi)
    return r[:m] + (r[m:] + _dot(a_hi, b_lo))


def _mod_kernel(cc_ref, w_ref, b_ref, o_ref):
    s = cc_ref[...]
    s = s * _sigmoid(s)
    o_ref[0] = _dot3(s, w_ref[0]) + b_ref[0]


def _modulation(cc, ada_w, ada_b):
    tn = 1536
    n6 = ada_w.shape[-1]
    return pl.pallas_call(
        _mod_kernel,
        out_shape=jax.ShapeDtypeStruct((DEPTH, 16, n6), F32),
        grid=(DEPTH, n6 // tn),
        in_specs=[
            pl.BlockSpec((16, D_MODEL), lambda l, j: (0, 0)),
            pl.BlockSpec((1, D_MODEL, tn), lambda l, j: (l, 0, j)),
            pl.BlockSpec((1, 1, tn), lambda l, j: (l, 0, j)),
        ],
        out_specs=pl.BlockSpec((1, 16, tn), lambda l, j: (l, 0, j)),
        compiler_params=_cparams(2),
        name="modulation",
    )(cc, ada_w, ada_b.reshape(DEPTH, 1, n6))


W_A = 768
W_G = 2 * GLA_DK + 2 * GLA_DV


def _token_specs(d, n_lat_tiles, ctx_blk):
    return [pl.BlockSpec((1, TM, d), lambda b, i: (b, jnp.minimum(i, n_lat_tiles - 1), 0)),
            pl.BlockSpec((1, TM, d), lambda b, i: (b, ctx_blk, 0))]


def _token_tile(x_ref, c_ref, n_lat_tiles):
    return jnp.where(pl.program_id(1) < n_lat_tiles, x_ref[0], c_ref[0])


def _inproj_kernel(x_ref, c_ref, mod_ref, g1_ref, wa_ref, wg_ref, wh_ref, wz_ref, qn_ref, kvn_ref, wuq_ref, wukv_ref,
                   wa2_ref, ba_ref, cos_ref, sin_ref,
                   q_out, k_out, v_out, gq_out, gk_out, gv_out, gr_out, glog_out, hy_out, gate_out, *, n_lat_tiles):
    x = _token_tile(x_ref, c_ref, n_lat_tiles)
    shift = mod_ref[0, 0:1, :]
    scale = mod_ref[0, 1:2, :]
    h = (_rms(x) * g1_ref[...] * (1.0 + scale) + shift).astype(BF16)

    za = _dot(h, wa_ref[...])
    cos = cos_ref[...]
    sin = sin_ref[...]

    cqn = (_rms(za[:, 0:256]) * qn_ref[...]).astype(BF16)
    qab = _dot(cqn, wuq_ref[...])
    nq = MLA_HEADS * HEAD_SLOT
    for hd in range(MLA_HEADS):
        sl = slice(hd * HEAD_SLOT, (hd + 1) * HEAD_SLOT)
        qa = qab[:, hd * HEAD_SLOT:(hd + 1) * HEAD_SLOT]
        qb = qab[:, nq + hd * HEAD_SLOT:nq + (hd + 1) * HEAD_SLOT]
        q_out[0, :, sl] = ((qa * cos + qb * sin) * (MLA_SCALE * LOG2E)).astype(BF16)

    ckvn = (_rms(za[:, 256:384]) * kvn_ref[...]).astype(BF16)
    kv = _dot(ckvn, wukv_ref[...])
    krot = za[:, 384:512] * cos + za[:, 512:640] * sin
    for hd in range(MLA_HEADS):
        sl = slice(hd * HEAD_SLOT, (hd + 1) * HEAD_SLOT)
        k_out[0, :, sl] = (kv[:, sl] + krot).astype(BF16)
    ones_hi = (lax.broadcasted_iota(jnp.int32, (1, HEAD_SLOT), 1) >= MLA_V).astype(F32)
    for hd in range(MLA_HEADS):
        sl = slice(hd * HEAD_SLOT, (hd + 1) * HEAD_SLOT)
        v_out[0, :, sl] = (kv[:, nq + hd * HEAD_SLOT:nq + (hd + 1) * HEAD_SLOT] + ones_hi).astype(BF16)

    xg = _dot(za[:, 640:768].astype(BF16), wa2_ref[...]) + ba_ref[...]
    glog_out[0] = (jnp.minimum(xg, 0.0) - jnp.log(1.0 + jnp.exp(-jnp.abs(xg)))) * (1.0 / GLA_TAU)

    zg = _dot(h, wg_ref[...])
    gq_out[0] = (zg[:, 0:GLA_DK] * (GLA_HK ** -0.5)).astype(BF16)
    gk_out[0] = zg[:, GLA_DK:2 * GLA_DK].astype(BF16)
    gv_out[0] = zg[:, 2 * GLA_DK:2 * GLA_DK + GLA_DV].astype(BF16)
    gr = zg[:, 2 * GLA_DK + GLA_DV:]
    gr_out[0] = (gr * _sigmoid(gr)).astype(BF16)

    hy_out[0] = _dot(h, wh_ref[...])
    gate_out[0] = _sigmoid(_dot(h, wz_ref[...])).astype(BF16)


def _rope_partner(w):
    a = MLA_ROPE // 4
    perm = np.concatenate([np.arange(a, 2 * a), np.arange(0, a), np.arange(3 * a, 4 * a), np.arange(2 * a, 3 * a)])
    sign = np.concatenate([-np.ones(a), np.ones(a), -np.ones(a), np.ones(a)]).astype(np.float32)
    return w[:, perm] * sign


def _prep_inproj_weights(w_in, mla_w_uq, mla_w_ukv, gla_w_a2, gla_b_a):
    offs = np.concatenate([[0], np.cumsum(IN_SIZES)])
    seg = [w_in[:, offs[i]:offs[i + 1]] for i in range(len(IN_SIZES))]
    w_cq, w_ckv, w_kr, w_gq, w_gk, w_gv, w_gr, w_af, w_ab, w_hy, w_gate = seg
    d = w_in.shape[0]
    z = lambda n: jnp.zeros((d, n), w_in.dtype)
    kr_tile = jnp.concatenate([z(MLA_NOPE), w_kr, z(HEAD_SLOT - MLA_NOPE - MLA_ROPE)], axis=1)
    krp_tile = jnp.concatenate([z(MLA_NOPE), _rope_partner(w_kr), z(HEAD_SLOT - MLA_NOPE - MLA_ROPE)], axis=1)
    a_tile = jnp.concatenate([w_af, w_ab, z(LANES - 2 * GLA_GATE_RANK)], axis=1)
    wa = jnp.concatenate([w_cq, w_ckv, kr_tile, krp_tile, a_tile], axis=1)
    wg = jnp.concatenate([w_gq, w_gk, w_gv, w_gr], axis=1)

    dh = MLA_NOPE + MLA_ROPE
    zq = lambda n: jnp.zeros((MLA_Q_LORA, n), w_in.dtype)
    plain, partner = [], []
    for hd in range(MLA_HEADS):
        blk = mla_w_uq[:, hd * dh:(hd + 1) * dh]
        plain += [blk, zq(HEAD_SLOT - dh)]
        partner += [zq(MLA_NOPE), _rope_partner(blk[:, MLA_NOPE:]), zq(HEAD_SLOT - dh)]
    wuq = jnp.concatenate(plain + partner, axis=1)

    zk = jnp.zeros((MLA_KV_LORA, HEAD_SLOT - MLA_NOPE), w_in.dtype)
    kcols, vcols = [], []
    for hd in range(MLA_HEADS):
        blk = mla_w_ukv[:, hd * (MLA_NOPE + MLA_V):(hd + 1) * (MLA_NOPE + MLA_V)]
        kcols += [blk[:, :MLA_NOPE], zk]
        vcols += [blk[:, MLA_NOPE:], zk]
    wukv = jnp.concatenate(kcols + vcols, axis=1)

    wa2 = jnp.zeros((LANES, 2 * GLA_DK), w_in.dtype)
    wa2 = wa2.at[0:GLA_GATE_RANK, 0:GLA_DK].set(gla_w_a2[0])
    wa2 = wa2.at[GLA_GATE_RANK:2 * GLA_GATE_RANK, GLA_DK:].set(gla_w_a2[1])
    ba = jnp.concatenate([gla_b_a[0], gla_b_a[1]])[None, :]
    bf = lambda t: t.astype(BF16)
    return bf(wa), bf(wg), bf(w_hy), bf(w_gate), bf(wuq), bf(wukv), bf(wa2), ba


def _rope_tables(seq, ctx_len):
    rows = seq // GRID_W
    row = np.repeat(np.arange(rows, dtype=np.float64), GRID_W)
    col = np.tile(np.arange(GRID_W, dtype=np.float64), rows)
    a = MLA_ROPE // 4
    inv = ROPE_BASE ** (-np.arange(a, dtype=np.float64) / a)
    ang_r = row[:, None] * inv
    ang_c = col[:, None] * inv
    cos32 = np.concatenate([np.cos(ang_r), np.cos(ang_r), np.cos(ang_c), np.cos(ang_c)], axis=1)
    sin32 = np.concatenate([np.sin(ang_r), np.sin(ang_r), np.sin(ang_c), np.sin(ang_c)], axis=1)
    pad_r = HEAD_SLOT - MLA_NOPE - MLA_ROPE
    cos = np.concatenate([np.ones((seq, MLA_NOPE)), cos32, np.zeros((seq, pad_r))], axis=1)
    sin = np.concatenate([np.zeros((seq, MLA_NOPE)), sin32, np.zeros((seq, pad_r))], axis=1)
    cos_c = np.concatenate([np.ones((ctx_len, MLA_NOPE + MLA_ROPE)), np.zeros((ctx_len, pad_r))], axis=1)
    sin_c = np.zeros((ctx_len, HEAD_SLOT))
    return (jnp.asarray(np.concatenate([cos, cos_c], axis=0), F32),
            jnp.asarray(np.concatenate([sin, sin_c], axis=0), F32))


def _inproj(x_lat, x_ctx, ctx_blk, mod_l, g1, weights, q_norm, kv_norm, cos, sin, n_lat_tiles):
    bsz, _, d = x_lat.shape
    nt = n_lat_tiles + 1
    t = nt * TM
    wa, wg, wh, wz, wuq, wukv, wa2, ba = weights
    tile = lambda w: pl.BlockSpec((1, TM, w), lambda b, i: (b, i, 0))
    row = lambda w: pl.BlockSpec((1, w), lambda b, i: (0, 0))
    mod_spec = pl.BlockSpec((1, 6, d), lambda b, i: (jnp.where(i < n_lat_tiles, b, bsz), 0, 0))
    tab = pl.BlockSpec((TM, HEAD_SLOT), lambda b, i: (i, 0))
    nq = MLA_HEADS * HEAD_SLOT
    sds = lambda w, dt: jax.ShapeDtypeStruct((bsz, t, w), dt)
    out_shape = (sds(nq, BF16), sds(nq, BF16), sds(nq, BF16), sds(GLA_DK, BF16), sds(GLA_DK, BF16),
                 sds(GLA_DV, BF16), sds(GLA_DV, BF16), sds(2 * GLA_DK, F32), sds(3 * HY_WIDTH, F32),
                 sds(3 * D_MODEL, BF16))
    out_specs = (tile(nq), tile(nq), tile(nq), tile(GLA_DK), tile(GLA_DK), tile(GLA_DV), tile(GLA_DV),
                 tile(2 * GLA_DK), tile(3 * HY_WIDTH), tile(3 * D_MODEL))
    return pl.pallas_call(
        functools.partial(_inproj_kernel, n_lat_tiles=n_lat_tiles),
        out_shape=out_shape,
        grid=(bsz, nt),
        in_specs=_token_specs(d, n_lat_tiles, ctx_blk) + [
            mod_spec, row(d), _const_spec(wa.shape), _const_spec(wg.shape), _const_spec(wh.shape),
            _const_spec(wz.shape), row(MLA_Q_LORA), row(MLA_KV_LORA), _const_spec(wuq.shape),
            _const_spec(wukv.shape), _const_spec(wa2.shape), row(2 * GLA_DK), tab, tab],
        out_specs=out_specs,
        compiler_params=_cparams(2, VMEM_LIMIT),
        name="inproj",
    )(x_lat, x_ctx, mod_l, g1[None, :], wa, wg, wh, wz, q_norm[None, :], kv_norm[None, :], wuq, wukv, wa2, ba,
      cos, sin)


ATT_TK = 1024


ATT_TQ = 1024
ATT_HEADS = 2


def _attn_kernel(q_ref, k_ref, v_ref, o_ref, m_ref, acc_ref, *, chunks):
    tq = q_ref.shape[1]
    n_heads = q_ref.shape[2] // HEAD_SLOT
    m_ref[...] = jnp.full(m_ref.shape, -jnp.inf, F32)
    acc_ref[...] = jnp.zeros(acc_ref.shape, F32)
    for r0, size in chunks:
        k = k_ref[0, pl.ds(r0, size), :]
        v = v_ref[0, pl.ds(r0, size), :]
        for hd in range(n_heads):
            sl = slice(hd * HEAD_SLOT, (hd + 1) * HEAD_SLOT)
            s = _dot_nt(q_ref[0, :, sl], k[:, sl])
            m_prev = m_ref[hd]
            m_new = jnp.maximum(m_prev, jnp.max(s, axis=1, keepdims=True))
            p = jnp.exp2((s - jnp.concatenate([m_new] * (size // LANES), axis=1)).astype(BF16))
            acc_ref[hd] = jnp.exp2(m_prev - m_new) * acc_ref[hd] + _dot(p, v[:, sl])
            m_ref[hd] = m_new
    lane = lax.broadcasted_iota(jnp.int32, (tq, HEAD_SLOT), 1)
    for pr in range(n_heads // 2):
        a0 = acc_ref[2 * pr]
        a1 = acc_ref[2 * pr + 1]
        o0 = a0 / pltpu.roll(a0, MLA_V, 1)
        o1 = pltpu.roll(a1, MLA_V, 1) / a1
        o_ref[0, :, pr * HEAD_SLOT:(pr + 1) * HEAD_SLOT] = jnp.where(lane < MLA_V, o0, o1).astype(o_ref.dtype)


def _attention(q, k, v, seq, with_ctx_queries):
    bsz, t, _ = q.shape
    ctx_len = t - seq
    assert seq % ATT_TK == 0 and seq % ATT_TQ == 0 and seq % ctx_len == 0 and 2 * MLA_V == HEAD_SLOT
    pair = 2 * HEAD_SLOT
    n_chunks = seq // ATT_TK
    chunks = tuple((j * ATT_TK, ATT_TK) for j in range(n_chunks - 1))
    chunks += (((n_chunks - 1) * ATT_TK, ATT_TK + ctx_len),)
    scratch = lambda nh, tq: [pltpu.VMEM((nh, tq, LANES), F32), pltpu.VMEM((nh, tq, HEAD_SLOT), F32)]
    hps = ATT_HEADS
    y = pl.pallas_call(
        functools.partial(_attn_kernel, chunks=chunks),
        out_shape=jax.ShapeDtypeStruct((bsz, seq, MLA_OUT), BF16),
        grid=(bsz, MLA_HEADS // hps, seq // ATT_TQ),
        in_specs=[
            pl.BlockSpec((1, ATT_TQ, hps * HEAD_SLOT), lambda b, hp, i: (b, i, hp)),
            pl.BlockSpec((1, t, hps * HEAD_SLOT), lambda b, hp, i: (b, 0, hp)),
            pl.BlockSpec((1, t, hps * HEAD_SLOT), lambda b, hp, i: (b, 0, hp)),
        ],
        out_specs=pl.BlockSpec((1, ATT_TQ, hps * MLA_V), lambda b, hp, i: (b, i, hp)),
        scratch_shapes=scratch(hps, ATT_TQ),
        compiler_params=_cparams(3, VMEM_LIMIT),
        name="mla_attention",
    )(q, k, v)
    if not with_ctx_queries:
        return y, None
    cblk = seq // ctx_len
    ctx_rows = lambda w: pl.BlockSpec((1, ctx_len, w), lambda b, hp: (b, cblk, hp))
    y_ctx = pl.pallas_call(
        functools.partial(_attn_kernel, chunks=((0, ctx_len),)),
        out_shape=jax.ShapeDtypeStruct((bsz, ctx_len, MLA_OUT), BF16),
        grid=(bsz, MLA_HEADS // 2),
        in_specs=[ctx_rows(pair), ctx_rows(pair), ctx_rows(pair)],
        out_specs=pl.BlockSpec((1, ctx_len, HEAD_SLOT), lambda b, hp: (b, 0, hp)),
        scratch_shapes=scratch(2, ctx_len),
        compiler_params=_cparams(2),
        name="mla_attention_ctx",
    )(q, k, v)
    return y, y_ctx


GLA_LEVELS = int(math.log2(TM))
GLA_SAFE_SPAN = 60.0


def _gla_level_matrices():
    i = np.arange(TM)[:, None]
    t = np.arange(TM)[None, :]
    fwd = [(t <= i)]
    for lv in range(GLA_LEVELS):
        m = TM >> (lv + 1)
        lo = (i // m) * m
        later = ((i // m) % 2) == 1
        q_part = later & (t >= lo) & (t <= i)
        k_part = (~later) & (t > i) & (t <= lo + m - 1)
        fwd.append(q_part | k_part)
    fwd = np.concatenate(fwd, axis=0).astype(np.float32)
    nb = 1 + GLA_LEVELS
    bwd = fwd.reshape(nb, TM, TM)[:, ::-1, ::-1].reshape(nb * TM, TM)
    return np.stack([fwd, bwd])


def _gla_kernel(qf_ref, kf_ref, vf_ref, gf_ref, qb_ref, kb_ref, vb_ref, gb_ref, mall_ref, of_ref, ob_ref, s_ref, a_ref):
    step = pl.program_id(1)

    @pl.when(step == 0)
    def _():
        s_ref[...] = jnp.zeros(s_ref.shape, F32)

    ins = ((qf_ref, kf_ref, vf_ref, gf_ref), (qb_ref, kb_ref, vb_ref, gb_ref))
    outs = (of_ref, ob_ref)
    qs, ks, vs, g2s, g_cums, g_tots = [], [], [], [], [], []
    for d in range(2):
        q_ref, k_ref, v_ref, g_ref = ins[d]
        qs.append(q_ref[0].astype(F32))
        ks.append(k_ref[0].astype(F32))
        vs.append(v_ref[0])
        g = g_ref[0]
        g2 = jnp.concatenate([g.astype(BF16), (g - g.astype(BF16).astype(F32)).astype(BF16)], axis=1)
        e2 = _dot(mall_ref[d, 0:TM, :], g2)
        g2s.append(g2)
        g_cums.append(e2[:, :GLA_DK] + e2[:, GLA_DK:])
        g_tots.append(jnp.sum(g, axis=0, keepdims=True))

    row = lax.broadcasted_iota(jnp.int32, (TM, TM), 0)
    col = lax.broadcasted_iota(jnp.int32, (TM, TM), 1)
    lane_head = lax.broadcasted_iota(jnp.int32, (TM, GLA_DK), 1) // GLA_HK
    tok = lax.broadcasted_iota(jnp.int32, (TM, GLA_DK), 0)
    eye = row == col

    def stack_heads(t):
        return jnp.concatenate([jnp.where(lane_head == hd, t, 0.0) for hd in range(GLA_HEADS)], axis=0).astype(BF16)

    span = jnp.maximum(jnp.max(-g_tots[0]), jnp.max(-g_tots[1]))

    @pl.when(span < GLA_SAFE_SPAN)
    def _():
        for d in range(2):
            res = _dot_nt(stack_heads(qs[d] * jnp.exp(g_cums[d])), (ks[d] * jnp.exp(-g_cums[d])).astype(BF16))
            seen = (col <= row) if d == 0 else (col >= row)
            for hd in range(GLA_HEADS):
                a_ref[d, hd] = jnp.where(seen, res[hd * TM:(hd + 1) * TM], 0.0)

    @pl.when(span >= GLA_SAFE_SPAN)
    def _():
        for d in range(2):
            q, k = qs[d], ks[d]
            e2l = _dot(mall_ref[d, TM:, :], g2s[d])
            e_lv = e2l[:, :GLA_DK] + e2l[:, GLA_DK:]
            res = _dot_nt(stack_heads(q), k.astype(BF16))
            for hd in range(GLA_HEADS):
                a_ref[d, hd] = jnp.where(eye, res[hd * TM:(hd + 1) * TM], 0.0)
            for lv in range(GLA_LEVELS):
                m = TM >> (lv + 1)
                w = jnp.exp(e_lv[lv * TM:(lv + 1) * TM])
                bit = (tok // m) % 2
                q_act = bit != d
                qt = jnp.where(q_act, q * w, 0.0)
                kt = jnp.where(q_act, 0.0, k * w).astype(BF16)
                res = _dot_nt(stack_heads(qt), kt)
                if m == TM // 2:
                    for hd in range(GLA_HEADS):
                        a_ref[d, hd] += res[hd * TM:(hd + 1) * TM]
                else:
                    same = (row // (2 * m)) == (col // (2 * m))
                    for hd in range(GLA_HEADS):
                        a_ref[d, hd] += jnp.where(same, res[hd * TM:(hd + 1) * TM], 0.0)

    same_head = (lax.broadcasted_iota(jnp.int32, (GLA_DK, GLA_DV), 0) // GLA_HK
                 == lax.broadcasted_iota(jnp.int32, (GLA_DK, GLA_DV), 1) // GLA_HV)
    for d in range(2):
        q, k, v, g_cum, g_tot = qs[d], ks[d], vs[d], g_cums[d], g_tots[d]
        s_old = s_ref[d]
        o_inter = _dot((q * jnp.exp(g_cum)).astype(BF16), s_old.astype(BF16))
        for hd in range(GLA_HEADS):
            sl = slice(hd * GLA_HV, (hd + 1) * GLA_HV)
            o_intra = _dot(a_ref[d, hd].astype(BF16), v[:, sl])
            outs[d][0, :, sl] = (o_intra + o_inter[:, sl]).astype(outs[d].dtype)

        kdec_t = (k * jnp.exp(g_tot - g_cum)).T.astype(BF16)
        upd = _dot(kdec_t, v)
        dec_col = jnp.sum(jnp.where(eye, jnp.broadcast_to(jnp.exp(g_tot), (TM, GLA_DK)), 0.0), axis=1,
                          keepdims=True)
        s_ref[d] = dec_col * s_old + jnp.where(same_head, upd, 0.0)


def _gla(gq, gk, gv, glog, n_lat_tiles):
    assert GLA_DK == TM
    bsz, t, _ = gq.shape
    nt = t // TM
    mall = jnp.asarray(_gla_level_matrices(), dtype=BF16)

    def tile_idx(d, s):
        return jnp.where(s == 0, n_lat_tiles, s - 1 if d == 0 else n_lat_tiles - s)

    def specs(d):
        tok = lambda w: pl.BlockSpec((1, TM, w), lambda b, s: (b, tile_idx(d, s), 0))
        return [tok(GLA_DK), tok(GLA_DK), tok(GLA_DV), pl.BlockSpec((1, TM, GLA_DK), lambda b, s: (b, tile_idx(d, s), d))]

    out_spec = lambda d: pl.BlockSpec((1, TM, GLA_DV), lambda b, s: (b, tile_idx(d, s), 0))
    sds = jax.ShapeDtypeStruct((bsz, t, GLA_DV), BF16)
    return pl.pallas_call(
        _gla_kernel,
        out_shape=(sds, sds),
        grid=(bsz, nt),
        in_specs=specs(0) + specs(1) + [_const_spec(mall.shape)],
        out_specs=(out_spec(0), out_spec(1)),
        scratch_shapes=[pltpu.VMEM((2, GLA_DK, GLA_DV), F32), pltpu.VMEM((2, GLA_HEADS, TM, TM), F32)],
        compiler_params=_cparams(2),
        name="gla_scan",
    )(gq, gk, gv, glog, gq, gk, gv, glog, mall)


HY_N1 = 64
HY_N2 = 128
HY_SLAB = SUBLANES
HY_CB = 256
HY_KG = 16
HY_SPEC_CB = 128
HY_UNROLL = 16


def _hyena_dft_constants(seq):
    n = 2 * seq
    assert n == HY_N1 * HY_N2
    nh = HY_N1 // 2
    kh = HY_N2 // 2
    eye = np.eye(HY_SLAB)
    k1 = np.arange(HY_N1)
    th = 2 * np.pi * np.outer(k1 + 0.5, np.arange(nh)) / HY_N1
    fwd1 = np.concatenate([np.kron(np.cos(th), eye), np.kron(-np.sin(th), eye)], axis=0)
    inv1 = (2.0 / n) * np.concatenate([np.kron(np.cos(th).T, eye), np.kron(-np.sin(th).T, eye)], axis=1)
    nlo = np.arange(HY_N2)
    k2 = np.arange(kh)
    ph = 2 * np.pi * (k2[None, :, None] * nlo[None, None, :] / HY_N2
                      + (k1[:, None, None] + 0.5) * nlo[None, None, :] / n)
    c, s = np.cos(ph), np.sin(ph)
    fwd2 = np.concatenate([np.concatenate([c, s], axis=2), np.concatenate([-s, c], axis=2)], axis=1)
    ct, st = c.transpose(0, 2, 1), s.transpose(0, 2, 1)
    inv2 = np.concatenate([np.concatenate([ct, -st], axis=2), np.concatenate([st, ct], axis=2)], axis=1)
    return fwd1, fwd2, inv2, inv1


def _hyena_ctx_dft_constants(ctx_len):
    n = 2 * ctx_len
    th = 2 * np.pi * np.outer(np.arange(ctx_len) + 0.5, np.arange(ctx_len)) / n
    fwd = np.concatenate([np.cos(th), -np.sin(th)], axis=0)
    inv = (2.0 / n) * np.concatenate([np.cos(th).T, -np.sin(th).T], axis=1)
    return fwd, inv


def _hyena_features(length):
    pos = np.arange(length, dtype=np.float64)
    t = pos / max(length - 1, 1)
    f = np.linspace(1e-4, HY_BANDS - 1, HY_BANDS)
    ang = (2.0 * math.pi / length) * pos[:, None] * f
    feat = np.concatenate([t[:, None], np.cos(ang), np.sin(ang)], axis=-1)
    return jnp.asarray(np.pad(feat, ((0, 0), (0, LANES - HY_POS_DIM))), F32)


def _hy_filter_kernel(feat_ref, w1_ref, b1_ref, w2_ref, b2_ref, w3_ref, b3_ref, absd_ref, h_ref, s_ref):
    i = pl.program_id(0)
    feat = feat_ref[...]
    hp = _dot3
    hdn = jnp.sin(hp(feat, w1_ref[...]) + b1_ref[...])
    hdn = jnp.sin(hp(hdn, w2_ref[...]) + b2_ref[...])
    h = hp(hdn, w3_ref[...]) + b3_ref[...]
    window = jnp.exp(-feat[:, 0:1] * absd_ref[...])
    h = h * jnp.concatenate([window] * (2 * HY_ORDER), axis=1)
    h_ref[...] = h

    @pl.when(i == 0)
    def _():
        s_ref[...] = jnp.zeros(s_ref.shape, F32)

    s_ref[...] += jnp.sum(jnp.abs(h), axis=0, keepdims=True)


def _hyena_filters_raw(length, filt_w):
    w1, b1, w2, b2, w3, b3 = filt_w
    nf = 2 * HY_ORDER * HY_WIDTH
    tr = min(length, 512)
    deltas = np.linspace(math.log(HY_DECAY_TARGET) / HY_FAST_DECAY, math.log(HY_DECAY_TARGET) / HY_SLOW_DECAY,
                         HY_WIDTH, dtype=np.float32)
    absd = jnp.asarray(np.abs(deltas))[None, :]
    w1p = jnp.pad(w1, ((0, LANES - HY_POS_DIM), (0, 0)))
    full = lambda shp: pl.BlockSpec(shp, lambda i: (0,) * len(shp))
    return pl.pallas_call(
        _hy_filter_kernel,
        out_shape=(jax.ShapeDtypeStruct((length, nf), F32), jax.ShapeDtypeStruct((1, nf), F32)),
        grid=(length // tr,),
        in_specs=[pl.BlockSpec((tr, LANES), lambda i: (i, 0)), full((LANES, HY_FILTER_HIDDEN)),
                  full((1, HY_FILTER_HIDDEN)), full((HY_FILTER_HIDDEN, HY_FILTER_HIDDEN)), full((1, HY_FILTER_HIDDEN)),
                  full((HY_FILTER_HIDDEN, nf)), full((1, nf)), full((1, HY_WIDTH))],
        out_specs=(pl.BlockSpec((tr, nf), lambda i: (i, 0)), full((1, nf))),
        compiler_params=_cparams(1),
        name="hyena_filter_mlp",
    )(_hyena_features(length), w1p, b1[None, :], w2, b2[None, :], w3, b3[None, :], absd)


def _dot_split(m, x):
    x_hi = x.astype(BF16)
    x_lo = (x - x_hi.astype(F32)).astype(BF16)
    n = x.shape[1]
    r = _dot(m, jnp.concatenate([x_hi, x_lo], axis=1))
    return r[:, :n] + r[:, n:]


def _odft_stage1(src_at, mm, s_re, s_im, unroll=2):
    nk = s_re.shape[0]
    half = nk * HY_SLAB

    def body(j, carry):
        r0 = pl.multiple_of(j * HY_SLAB, HY_SLAB)
        slab = src_at(r0)
        cb = slab.shape[-1]
        res = mm(slab.reshape(-1, cb))
        s_re[:, pl.ds(r0, HY_SLAB), :] = res[:half].reshape(nk, HY_SLAB, cb)
        s_im[:, pl.ds(r0, HY_SLAB), :] = res[half:].reshape(nk, HY_SLAB, cb)
        return carry

    lax.fori_loop(0, HY_N2 // HY_SLAB, body, 0, unroll=unroll)


def _hy_spectrum_kernel(hf_ref, hb_ref, sf_ref, sb_ref, fwd1_ref, fwd2_ref, o_ref, s_re, s_im):
    kh = HY_N2 // 2
    mm1 = lambda x: _dot_split(fwd1_ref[...], x)

    def middle(sign):
        def body(k1, carry):
            a = jnp.concatenate([s_re[k1], s_im[k1]], axis=0)
            x = _dot_split(fwd2_ref[k1], a)
            if sign is None:
                o_ref[0, k1] = x[:kh]
                o_ref[1, k1] = x[kh:]
            else:
                inv_norm = 1.0 / (sf_ref[...] + sb_ref[...])
                o_ref[0, k1] = (o_ref[0, k1] + x[:kh]) * inv_norm
                o_ref[1, k1] = (o_ref[1, k1] - x[kh:]) * inv_norm
            return carry
        lax.fori_loop(0, HY_N1, body, 0, unroll=8)

    _odft_stage1(lambda r0: hf_ref[:, pl.ds(r0, HY_SLAB), :], mm1, s_re, s_im, unroll=4)
    middle(None)

    def bwd_slab(r0):
        slab = hb_ref[:, pl.ds(r0, HY_SLAB), :]
        nhi = lax.broadcasted_iota(jnp.int32, slab.shape, 0)
        r = lax.broadcasted_iota(jnp.int32, slab.shape, 1)
        return jnp.where((nhi == 0) & (r + r0 == 0), 0.0, slab)

    _odft_stage1(bwd_slab, mm1, s_re, s_im, unroll=4)
    middle(-1)


def _hyena_filter_spectrum(seq, w1, b1, w2, b2, w3, b3):
    h_raw, s = _hyena_filters_raw(seq, (w1, b1, w2, b2, w3, b3))
    nh = HY_N1 // 2
    nc = HY_ORDER * HY_WIDTH
    h3 = h_raw.reshape(nh, HY_N2, 2 * nc)
    fwd1, fwd2, _, _ = _hyena_dft_constants(seq)
    fwd1 = jnp.asarray(fwd1, BF16)
    fwd2 = jnp.asarray(fwd2, BF16)
    scb = HY_SPEC_CB
    ncb = nc // scb
    return pl.pallas_call(
        _hy_spectrum_kernel,
        out_shape=jax.ShapeDtypeStruct((2, HY_N1, HY_N2 // 2, nc), F32),
        grid=(ncb,),
        in_specs=[pl.BlockSpec((nh, HY_N2, scb), lambda c: (0, 0, c)),
                  pl.BlockSpec((nh, HY_N2, scb), lambda c: (0, 0, ncb + c)),
                  pl.BlockSpec((1, scb), lambda c: (0, c)),
                  pl.BlockSpec((1, scb), lambda c: (0, ncb + c)),
                  _const_spec(fwd1.shape), _const_spec(fwd2.shape)],
        out_specs=pl.BlockSpec((2, HY_N1, HY_N2 // 2, scb), lambda c: (0, 0, 0, c)),
        scratch_shapes=[pltpu.VMEM((HY_N1, HY_N2, scb), F32), pltpu.VMEM((HY_N1, HY_N2, scb), F32)],
        compiler_params=_cparams(1, VMEM_LIMIT),
        name="hyena_filter_spectrum",
    )(h3, h3, s, s, fwd1, fwd2)


def _short_conv_chunk(ref, c, n_chunks, w_ref, b_ref):
    per = TM // HY_N2
    cur = ref[0, pl.ds(per * c, per)]
    cb = cur.shape[-1]
    cur = cur.reshape(TM, cb)
    prev = ref[0, jnp.maximum(per * c - 1, 0), pl.ds(HY_N2 - SUBLANES, SUBLANES), :][SUBLANES - 1:SUBLANES]
    nxt = ref[0, jnp.minimum(per * c + per, per * n_chunks - 1), pl.ds(0, SUBLANES), :][0:1]
    prev = jnp.where(c > 0, prev, 0.0)
    nxt = jnp.where(c < n_chunks - 1, nxt, 0.0)
    rowi = lax.broadcasted_iota(jnp.int32, (TM, cb), 0)
    dn = jnp.where(rowi == 0, prev, pltpu.roll(cur, 1, 0))
    up = jnp.where(rowi == TM - 1, nxt, pltpu.roll(cur, TM - 1, 0))
    return b_ref[...] + w_ref[0:1, :] * dn + w_ref[1:2, :] * cur + w_ref[2:3, :] * up


def _hy_conv_kernel(*refs, conv_y):
    if conv_y:
        (y_ref, g_ref, h_ref, bias_ref, wy_ref, by_ref, wg_ref, bg_ref, fwd1_ref, fwd2_ref, inv2_ref, inv1_ref,
         o_ref, s_re, s_im, gs_ref, us_ref) = refs
    else:
        (y_ref, g_ref, h_ref, bias_ref, wg_ref, bg_ref, fwd1_ref, fwd2_ref, inv2_ref, inv1_ref,
         o_ref, s_re, s_im, gs_ref) = refs
    grp = pl.program_id(2)
    last = pl.num_programs(2) - 1
    nh = HY_N1 // 2
    per = TM // HY_N2
    n_chunks = nh // per
    cb = o_ref.shape[-1]
    kh = HY_N2 // 2

    @pl.when(grp == 0)
    def _():
        def pre(c, carry):
            gs_ref[pl.ds(per * c, per)] = _short_conv_chunk(g_ref, c, n_chunks, wg_ref, bg_ref).reshape(per, HY_N2, cb)
            if conv_y:
                us_ref[pl.ds(per * c, per)] = _short_conv_chunk(y_ref, c, n_chunks, wy_ref, by_ref).reshape(
                    per, HY_N2, cb)
            return carry
        lax.fori_loop(0, n_chunks, pre, 0)

    if conv_y:
        u_at = lambda r0: us_ref[:, pl.ds(r0, HY_SLAB), :]
    else:
        u_at = lambda r0: y_ref[0, :, pl.ds(r0, HY_SLAB), :]

    _odft_stage1(u_at, lambda x: _dot(fwd1_ref[0], x.astype(BF16)), s_re, s_im, unroll=HY_UNROLL)

    def middle(k1, carry):
        kg = grp * HY_KG + k1
        a = jnp.concatenate([s_re[k1], s_im[k1]], axis=0).astype(BF16)
        x = _dot(fwd2_ref[kg], a)
        xr, xi = x[:kh], x[kh:]
        hr, hi = h_ref[0, kg], h_ref[1, kg]
        y = jnp.concatenate([xr * hr - xi * hi, xr * hi + xi * hr], axis=0).astype(BF16)
        bm = _dot(inv2_ref[kg], y)
        s_re[k1] = bm[:HY_N2]
        s_im[k1] = bm[HY_N2:]
        return carry

    lax.fori_loop(0, HY_KG, middle, 0, unroll=HY_KG)

    def partial_conv(r0):
        slab = jnp.concatenate([s_re[:, pl.ds(r0, HY_SLAB), :].reshape(HY_KG * HY_SLAB, cb),
                                s_im[:, pl.ds(r0, HY_SLAB), :].reshape(HY_KG * HY_SLAB, cb)], axis=0).astype(BF16)
        return _dot(inv1_ref[0], slab).reshape(nh, HY_SLAB, cb)

    def post_loop(fn):
        def post(j, carry):
            r0 = pl.multiple_of(j * HY_SLAB, HY_SLAB)
            o_ref[0, :, pl.ds(r0, HY_SLAB), :] = fn(r0, partial_conv(r0))
            return carry
        lax.fori_loop(0, HY_N2 // HY_SLAB, post, 0, unroll=HY_UNROLL)

    @pl.when(grp == 0)
    def _():
        post_loop(lambda r0, part: part)

    @pl.when((grp > 0) & (grp < last))
    def _():
        post_loop(lambda r0, part: o_ref[0, :, pl.ds(r0, HY_SLAB), :] + part)

    @pl.when(grp == last)
    def _():
        post_loop(lambda r0, part: gs_ref[:, pl.ds(r0, HY_SLAB), :]
                  * (o_ref[0, :, pl.ds(r0, HY_SLAB), :] + part + bias_ref[...] * u_at(r0)))


def _hyena_order(y4, y_col0, z4, gate_col0, hspec, order, hy_bias, short_w, short_b, consts, conv_y):
    bsz = z4.shape[0]
    nh = HY_N1 // 2
    ncb = HY_WIDTH // HY_CB
    ngrp = HY_N1 // HY_KG
    fwd1, fwd2, inv2, inv1 = consts
    blk4 = lambda off: pl.BlockSpec((1, nh, HY_N2, HY_CB), lambda c, b, g: (b, 0, 0, off + c))
    rowspec = lambda rows, off: pl.BlockSpec((rows, HY_CB), lambda c, b, g: (0, off + c))
    grouped = lambda shp: pl.BlockSpec(shp, lambda c, b, g: (g, 0, 0))
    in_specs = [blk4(y_col0), blk4(gate_col0),
                pl.BlockSpec((2, HY_N1, HY_N2 // 2, HY_CB), lambda c, b, g: (0, 0, 0, order * ncb + c),
                             pipeline_mode=pl.Buffered(1)),
                rowspec(1, 0)]
    args = [y4, z4, hspec, hy_bias.reshape(1, -1)]
    if conv_y:
        in_specs += [rowspec(HY_SHORT, y_col0), rowspec(1, y_col0)]
        args += [short_w, short_b[None, :]]
    in_specs += [rowspec(HY_SHORT, gate_col0), rowspec(1, gate_col0)]
    args += [short_w, short_b[None, :]]
    in_specs += [grouped((1,) + fwd1.shape[1:]), _const_spec(fwd2.shape), _const_spec(inv2.shape),
                 grouped((1,) + inv1.shape[1:])]
    args += [fwd1, fwd2, inv2, inv1]
    scratch = [pltpu.VMEM((HY_KG, HY_N2, HY_CB), F32), pltpu.VMEM((HY_KG, HY_N2, HY_CB), F32),
               pltpu.VMEM((nh, HY_N2, HY_CB), F32)]
    if conv_y:
        scratch.append(pltpu.VMEM((nh, HY_N2, HY_CB), F32))
    return pl.pallas_call(
        functools.partial(_hy_conv_kernel, conv_y=conv_y),
        out_shape=jax.ShapeDtypeStruct((bsz, nh, HY_N2, HY_WIDTH), F32),
        grid=(ncb, bsz, ngrp),
        in_specs=in_specs,
        out_specs=pl.BlockSpec((1, nh, HY_N2, HY_CB), lambda c, b, g: (b, 0, 0, c)),
        scratch_shapes=scratch,
        compiler_params=_cparams(3, VMEM_LIMIT),
        name="hyena_conv%d" % order,
    )(*args)


def _hyena_latent(z_hy, hspec, short_w, short_b, hy_bias):
    bsz, t, _ = z_hy.shape
    seq = HY_N1 * HY_N2 // 2
    fwd1, fwd2, inv2, inv1 = _hyena_dft_constants(seq)
    ngrp = HY_N1 // HY_KG
    rows = HY_KG * HY_SLAB
    fwd1 = fwd1.reshape(2, ngrp, rows, -1).transpose(1, 0, 2, 3).reshape(ngrp, 2 * rows, -1)
    inv1 = inv1.reshape(-1, 2, ngrp, rows).transpose(2, 0, 1, 3).reshape(ngrp, -1, 2 * rows)
    consts = tuple(jnp.asarray(m, BF16) for m in (fwd1, fwd2, inv2, inv1))
    z4 = z_hy.reshape(bsz, t // HY_N2, HY_N2, 3 * HY_WIDTH)
    ncb = HY_WIDTH // HY_CB
    y1 = _hyena_order(z4, 2 * ncb, z4, 0, hspec, 0, hy_bias[0], short_w, short_b, consts, True)
    y2 = _hyena_order(y1, 0, z4, ncb, hspec, 1, hy_bias[1], short_w, short_b, consts, False)
    return y2.reshape(bsz, seq, HY_WIDTH)


def _hy_ctx_spectrum_kernel(h_ref, s_ref, fwd_ref, o_ref):
    lc = h_ref.shape[0]
    nc = HY_ORDER * HY_WIDTH
    hp = _dot3
    h = h_ref[...]
    rowi = lax.broadcasted_iota(jnp.int32, (lc, nc), 0)
    xf = hp(fwd_ref[...], h[:, :nc])
    xb = hp(fwd_ref[...], jnp.where(rowi == 0, 0.0, h[:, nc:]))
    inv_norm = 1.0 / (s_ref[:, :nc] + s_ref[:, nc:])
    o_ref[0] = (xf[:lc] + xb[:lc]) * inv_norm
    o_ref[1] = (xf[lc:] - xb[lc:]) * inv_norm


def _hy_ctx_conv_kernel(x1_ref, x2_ref, v_ref, h_ref, bias_ref, w_ref, b_ref, fwd_ref, inv_ref, o_ref):
    lc = o_ref.shape[1]
    rowi = lax.broadcasted_iota(jnp.int32, (lc, HY_WIDTH), 0)

    def short(ref, part):
        cur = ref[0].reshape(lc, HY_WIDTH)
        sl = slice(part * HY_WIDTH, (part + 1) * HY_WIDTH)
        dn = jnp.where(rowi == 0, 0.0, pltpu.roll(cur, 1, 0))
        up = jnp.where(rowi == lc - 1, 0.0, pltpu.roll(cur, lc - 1, 0))
        return b_ref[:, sl] + w_ref[0:1, sl] * dn + w_ref[1:2, sl] * cur + w_ref[2:3, sl] * up

    y = short(v_ref, 2)
    for order, gref in enumerate((x1_ref, x2_ref)):
        sl = slice(order * HY_WIDTH, (order + 1) * HY_WIDTH)
        x = _dot(fwd_ref[...], y.astype(BF16))
        xr, xi = x[:lc], x[lc:]
        hr, hi = h_ref[0, :, sl], h_ref[1, :, sl]
        prod = jnp.concatenate([xr * hr - xi * hi, xr * hi + xi * hr], axis=0).astype(BF16)
        conv = _dot(inv_ref[...], prod)
        y = short(gref, order) * (conv + bias_ref[order:order + 1, :] * y)
    o_ref[0] = y


def _hyena_ctx(z_hy, filt_w, short_w, short_b, hy_bias):
    bsz, t, _ = z_hy.shape
    seq = HY_N1 * HY_N2 // 2
    lc = t - seq
    per = lc // HY_N2
    h_raw, s = _hyena_filters_raw(lc, filt_w)
    fwd, inv = _hyena_ctx_dft_constants(lc)
    nc = HY_ORDER * HY_WIDTH
    full = lambda shp: pl.BlockSpec(shp, lambda *_: (0,) * len(shp))
    hspec = pl.pallas_call(
        _hy_ctx_spectrum_kernel,
        out_shape=jax.ShapeDtypeStruct((2, lc, nc), F32),
        grid=(1,),
        in_specs=[full(h_raw.shape), full(s.shape), full(fwd.shape)],
        out_specs=full((2, lc, nc)),
        compiler_params=_cparams(1),
        name="hyena_ctx_spectrum",
    )(h_raw, s, jnp.asarray(fwd, F32))
    z4 = z_hy.reshape(bsz, t // HY_N2, HY_N2, 3 * HY_WIDTH)
    blk = lambda part: pl.BlockSpec((1, per, HY_N2, HY_WIDTH), lambda b: (b, seq // lc, 0, part))
    return pl.pallas_call(
        _hy_ctx_conv_kernel,
        out_shape=jax.ShapeDtypeStruct((bsz, lc, HY_WIDTH), F32),
        grid=(bsz,),
        in_specs=[blk(0), blk(1), blk(2), full((2, lc, nc)), full((HY_ORDER, HY_WIDTH)),
                  full((HY_SHORT, 3 * HY_WIDTH)), full((1, 3 * HY_WIDTH)), full(fwd.shape), full(inv.shape)],
        out_specs=pl.BlockSpec((1, lc, HY_WIDTH), lambda b: (b, 0, 0)),
        compiler_params=_cparams(1),
        name="hyena_ctx_conv",
    )(z4, z4, z4, hspec, hy_bias, short_w, short_b[None, :], jnp.asarray(fwd, BF16), jnp.asarray(inv, BF16))


FF_CHUNK = 1024


def _merge_mlp_kernel(x_ref, c_ref, mod_ref, ymla_ref, ymlac_ref, of_ref, ob_ref, gr_ref, yhy_ref, yhyc_ref, zg_ref, on_ref,
                      wm_ref, wgl_ref, wh_ref, wo_ref, g2_ref, w1_ref, w2_ref, fg_ref, o_ref, *, final, n_lat_tiles):
    o = of_ref[0].astype(F32) + ob_ref[0].astype(F32)
    silu = gr_ref[0].astype(F32)
    parts = []
    for hd in range(GLA_HEADS):
        sl = slice(hd * GLA_HV, (hd + 1) * GLA_HV)
        parts.append((_rms(o[:, sl]) * on_ref[...] * silu[:, sl]).astype(BF16))
    y_gla = jnp.concatenate(parts, axis=1)
    zg = zg_ref[0].astype(F32)
    d = x_ref.shape[-1]
    m = zg[:, 0:d] * _dot(_token_tile(ymla_ref, ymlac_ref, n_lat_tiles), wm_ref[...])
    m = m + zg[:, d:2 * d] * _dot(y_gla, wgl_ref[...])
    m = m + zg[:, 2 * d:3 * d] * _dot(_token_tile(yhy_ref, yhyc_ref, n_lat_tiles).astype(BF16), wh_ref[...])
    out = _dot(m.astype(BF16), wo_ref[...])
    x = _token_tile(x_ref, c_ref, n_lat_tiles) + mod_ref[0, 2:3, :] * out

    h = (_rms(x) * g2_ref[...] * (1.0 + mod_ref[0, 4:5, :]) + mod_ref[0, 3:4, :]).astype(BF16)
    acc = jnp.zeros(x.shape, F32)
    for j in range(w1_ref.shape[1] // FF_CHUNK):
        a = jnp.maximum(_dot(h, w1_ref[:, j * FF_CHUNK:(j + 1) * FF_CHUNK]), 0.0)
        acc = acc + _dot((a * a).astype(BF16), w2_ref[j * FF_CHUNK:(j + 1) * FF_CHUNK, :])
    xn = x + mod_ref[0, 5:6, :] * acc
    if final:
        xn = _rms(xn) * fg_ref[...]
    o_ref[0] = xn


def _mod_spec(d, n_lat_tiles, bsz):
    return pl.BlockSpec((1, 6, d), lambda b, i: (jnp.where(i < n_lat_tiles, b, bsz), 0, 0))


def _merge_mlp(x_lat, x_ctx, ctx_blk, mod_l, y_mla, y_mla_ctx, o_gla, gr, y_hy, y_hy_ctx, gate, out_norm,
               w_o_mla, w_o_gla, w_o_hy, w_out, g2, w1, w2, final_g, n_tiles, n_lat_tiles, final):
    bsz, _, d = x_lat.shape
    if y_mla_ctx is None:
        y_mla_ctx, y_hy_ctx = y_mla, y_hy
    tile = lambda w: pl.BlockSpec((1, TM, w), lambda b, i: (b, i, 0))
    row = lambda w: pl.BlockSpec((1, w), lambda b, i: (0, 0))
    bf = lambda w: w.astype(BF16)
    return pl.pallas_call(
        functools.partial(_merge_mlp_kernel, final=final, n_lat_tiles=n_lat_tiles),
        out_shape=jax.ShapeDtypeStruct((bsz, n_tiles * TM, d), F32),
        grid=(bsz, n_tiles),
        in_specs=_token_specs(d, n_lat_tiles, ctx_blk) + [_mod_spec(d, n_lat_tiles, bsz)]
        + _token_specs(MLA_OUT, n_lat_tiles, 0) + [tile(GLA_DV), tile(GLA_DV), tile(GLA_DV)]
        + _token_specs(HY_WIDTH, n_lat_tiles, 0) + [
            tile(3 * d), row(GLA_HV),
            _const_spec(w_o_mla.shape), _const_spec(w_o_gla.shape), _const_spec(w_o_hy.shape),
            _const_spec(w_out.shape), row(d), _const_spec(w1.shape), _const_spec(w2.shape), row(d)],
        out_specs=tile(d),
        compiler_params=_cparams(2, VMEM_LIMIT),
        name="merge_mlp",
    )(x_lat, x_ctx, mod_l, y_mla, y_mla_ctx, o_gla[0], o_gla[1], gr, y_hy, y_hy_ctx, gate, out_norm[None, :],
      bf(w_o_mla), bf(w_o_gla), bf(w_o_hy), bf(w_out), g2[None, :], bf(w1), bf(w2), final_g[None, :])


def kernel(x, c, ctx, c_ctx, ada_w, ada_b, norm1_g, norm2_g, w_in, mla_q_norm, mla_w_uq, mla_kv_norm, mla_w_ukv, gla_w_a2, gla_b_a, gla_out_norm, hy_short_w, hy_short_b, hy_f_w1, hy_f_b1, hy_f_w2, hy_f_b2, hy_f_w3, hy_f_b3, hy_bias, w_o_mla, w_o_gla, w_o_hy, w_out, ff_w1, ff_w2, final_norm_g):
    bsz, seq, d = x.shape
    ctx_len = ctx.shape[1]
    n_lat = seq // TM
    assert ctx_len == TM and seq % TM == 0
    n_all = n_lat + 1
    x_lat, x_ctx, ctx_blk = x, ctx, 0
    cc = jnp.zeros((16, d), F32).at[:bsz].set(c).at[bsz].set(c_ctx)
    mod = _modulation(cc, ada_w, ada_b).reshape(DEPTH, 16, 6, d)
    cos, sin = _rope_tables(seq, ctx_len)
    for l in range(DEPTH):
        last = l == DEPTH - 1
        n_tiles = n_lat if last else n_all
        weights = _prep_inproj_weights(w_in[l], mla_w_uq[l], mla_w_ukv[l], gla_w_a2[l], gla_b_a[l])
        q, k, v, gq, gk, gv, gr, glog, z_hy, z_gate = _inproj(x_lat, x_ctx, ctx_blk, mod[l], norm1_g[l], weights,
                                                             mla_q_norm[l], mla_kv_norm[l], cos, sin, n_lat)
        y_mla, y_mla_c = _attention(q, k, v, seq, not last)
        o_gla = _gla(gq, gk, gv, glog, n_lat)
        filt_w = (hy_f_w1[l], hy_f_b1[l], hy_f_w2[l], hy_f_b2[l], hy_f_w3[l], hy_f_b3[l])
        hspec = _hyena_filter_spectrum(seq, *filt_w)
        y_hy = _hyena_latent(z_hy, hspec, hy_short_w[l], hy_short_b[l], hy_bias[l])
        y_hy_c = None if last else _hyena_ctx(z_hy, filt_w, hy_short_w[l], hy_short_b[l], hy_bias[l])
        xc = _merge_mlp(x_lat, x_ctx, ctx_blk, mod[l], y_mla, y_mla_c, o_gla, gr, y_hy, y_hy_c, z_gate,
                        gla_out_norm[l], w_o_mla[l], w_o_gla[l], w_o_hy[l], w_out[l], norm2_g[l], ff_w1[l], ff_w2[l],
                        final_norm_g, n_tiles, n_lat, last)
        x_lat, x_ctx, ctx_blk = xc, xc, n_lat
    return xc
```

```python
import functools
import math

import numpy as np
import jax
import jax.numpy as jnp
from jax import lax
from jax.experimental import pallas as pl
from jax.experimental.pallas import tpu as pltpu

F32 = jnp.float32
BF16 = jnp.bfloat16
LOG2E = 1.4426950408889634

D_MODEL = 1024
DEPTH = 2
GRID_W = 64
EPS = 1e-6
MLA_HEADS = 8
MLA_NOPE = 64
MLA_ROPE = 32
MLA_V = 64
MLA_Q_LORA = 256
MLA_KV_LORA = 128
MLA_SCALE = (MLA_NOPE + MLA_ROPE) ** -0.5
ROPE_BASE = 10000.0
GLA_HEADS = 4
GLA_DK = 256
GLA_DV = 512
GLA_HK = GLA_DK // GLA_HEADS
GLA_HV = GLA_DV // GLA_HEADS
GLA_GATE_RANK = 16
GLA_TAU = 16.0
HY_WIDTH = 512
HY_ORDER = 2
HY_SHORT = 3
HY_BANDS = 16
HY_POS_DIM = 1 + 2 * HY_BANDS
HY_FILTER_HIDDEN = 64
HY_FAST_DECAY = 0.3
HY_SLOW_DECAY = 1.5
HY_DECAY_TARGET = 1e-2
D_FF = 4 * D_MODEL
MLA_OUT = MLA_HEADS * MLA_V
IN_SIZES = (MLA_Q_LORA, MLA_KV_LORA, MLA_ROPE, GLA_DK, GLA_DK, GLA_DV, GLA_DV, GLA_GATE_RANK, GLA_GATE_RANK,
            (HY_ORDER + 1) * HY_WIDTH, 3 * D_MODEL)

LANES = 128
SUBLANES = 8
TM = 256
HEAD_SLOT = 128
V7X_VMEM_BYTES = 64 * 1024 * 1024
VMEM_LIMIT = V7X_VMEM_BYTES * 7 // 8


def _cparams(n_axes, vmem=None):
    return pltpu.CompilerParams(dimension_semantics=("arbitrary",) * n_axes, vmem_limit_bytes=vmem)


def _const_spec(shape):
    nd = len(shape)
    return pl.BlockSpec(shape, lambda *_: (0,) * nd, pipeline_mode=pl.Buffered(1))


def _rms(x):
    return x * lax.rsqrt(jnp.mean(x * x, axis=-1, keepdims=True) + EPS)


def _sigmoid(x):
    return 1.0 / (1.0 + jnp.exp(-x))


def _dot(a, b):
    return jnp.dot(a, b, preferred_element_type=F32)


def _dot_nt(a, b):
    return lax.dot_general(a, b, (((1,), (1,)), ((), ())), preferred_element_type=F32)


def _dot3(a, b):
    a_hi = a.astype(BF16)
    a_lo = (a - a_hi.astype(F32)).astype(BF16)
    b_hi = b.astype(BF16)
    b_lo = (b - b_hi.astype(F32)).astype(BF16)
    m = a.shape[0]
    r = _dot(jnp.concatenate([a_hi, a_lo], axis=0), b_hi)
    return r[:m] + (r[m:] + _dot(a_hi, b_lo))


def _mod_kernel(cc_ref, w_ref, b_ref, o_ref):
    s = cc_ref[...]
    s = s * _sigmoid(s)
    o_ref[0] = _dot3(s, w_ref[0]) + b_ref[0]


def _modulation(cc, ada_w, ada_b):
    tn = 1536
    n6 = ada_w.shape[-1]
    return pl.pallas_call(
        _mod_kernel,
        out_shape=jax.ShapeDtypeStruct((DEPTH, 16, n6), F32),
        grid=(DEPTH, n6 // tn),
        in_specs=[
            pl.BlockSpec((16, D_MODEL), lambda l, j: (0, 0)),
            pl.BlockSpec((1, D_MODEL, tn), lambda l, j: (l, 0, j)),
            pl.BlockSpec((1, 1, tn), lambda l, j: (l, 0, j)),
        ],
        out_specs=pl.BlockSpec((1, 16, tn), lambda l, j: (l, 0, j)),
        compiler_params=_cparams(2),
        name="modulation",
    )(cc, ada_w, ada_b.reshape(DEPTH, 1, n6))


W_A = 768
W_G = 2 * GLA_DK + 2 * GLA_DV


def _token_specs(d, n_lat_tiles, ctx_blk):
    return [pl.BlockSpec((1, TM, d), lambda b, i: (b, jnp.minimum(i, n_lat_tiles - 1), 0)),
            pl.BlockSpec((1, TM, d), lambda b, i: (b, ctx_blk, 0))]


def _token_tile(x_ref, c_ref, n_lat_tiles):
    if n_lat_tiles is None:
        return x_ref[0]
    return jnp.where(pl.program_id(1) < n_lat_tiles, x_ref[0], c_ref[0])


def _inproj_kernel(x_ref, c_ref, mod_ref, g1_ref, wa_ref, wg_ref, wh_ref, wz_ref, qn_ref, kvn_ref, wuq_ref, wukv_ref,
                   wa2_ref, ba_ref, cos_ref, sin_ref,
                   q_out, k_out, v_out, gq_out, gk_out, gv_out, gr_out, glog_out, hy_out, gate_out, *, n_lat_tiles):
    x = _token_tile(x_ref, c_ref, n_lat_tiles)
    shift = mod_ref[0, 0:1, :]
    scale = mod_ref[0, 1:2, :]
    h = (_rms(x) * (g1_ref[...] * (1.0 + scale)) + shift).astype(BF16)

    za = _dot(h, wa_ref[...])
    cos = cos_ref[...]
    sin = sin_ref[...]

    cqn = (_rms(za[:, 0:256]) * qn_ref[...]).astype(BF16)
    qab = _dot(cqn, wuq_ref[...])
    nq = MLA_HEADS * HEAD_SLOT
    for hd in range(MLA_HEADS):
        sl = slice(hd * HEAD_SLOT, (hd + 1) * HEAD_SLOT)
        qa = qab[:, hd * HEAD_SLOT:(hd + 1) * HEAD_SLOT]
        qb = qab[:, nq + hd * HEAD_SLOT:nq + (hd + 1) * HEAD_SLOT]
        q_out[0, :, sl] = ((qa * cos + qb * sin) * (MLA_SCALE * LOG2E)).astype(BF16)

    ckvn = (_rms(za[:, 256:384]) * kvn_ref[...]).astype(BF16)
    kv = _dot(ckvn, wukv_ref[...])
    krot = za[:, 384:512] * cos + za[:, 512:640] * sin
    for hd in range(MLA_HEADS):
        sl = slice(hd * HEAD_SLOT, (hd + 1) * HEAD_SLOT)
        k_out[0, :, sl] = (kv[:, sl] + krot).astype(BF16)
    ones_hi = (lax.broadcasted_iota(jnp.int32, (1, HEAD_SLOT), 1) >= MLA_V).astype(F32)
    for hd in range(MLA_HEADS):
        sl = slice(hd * HEAD_SLOT, (hd + 1) * HEAD_SLOT)
        v_out[0, :, sl] = (kv[:, nq + hd * HEAD_SLOT:nq + (hd + 1) * HEAD_SLOT] + ones_hi).astype(BF16)

    xg = _dot(za[:, 640:768].astype(BF16), wa2_ref[...]) + ba_ref[...]
    glog_out[0] = (jnp.minimum(xg, 0.0) - jnp.log(1.0 + jnp.exp(-jnp.abs(xg)))) * (1.0 / GLA_TAU)

    zg = _dot(h, wg_ref[...])
    gq_out[0] = (zg[:, 0:GLA_DK] * (GLA_HK ** -0.5)).astype(BF16)
    gk_out[0] = zg[:, GLA_DK:2 * GLA_DK].astype(BF16)
    gv_out[0] = zg[:, 2 * GLA_DK:2 * GLA_DK + GLA_DV].astype(BF16)
    gr = zg[:, 2 * GLA_DK + GLA_DV:]
    gr_out[0] = (gr * _sigmoid(gr)).astype(BF16)

    hy_out[0] = _dot(h, wh_ref[...])
    gate_out[0] = _sigmoid(_dot(h, wz_ref[...])).astype(BF16)


def _rope_partner(w):
    a = MLA_ROPE // 4
    perm = np.concatenate([np.arange(a, 2 * a), np.arange(0, a), np.arange(3 * a, 4 * a), np.arange(2 * a, 3 * a)])
    sign = np.concatenate([-np.ones(a), np.ones(a), -np.ones(a), np.ones(a)]).astype(np.float32)
    return w[:, perm] * sign


def _prep_inproj_weights(w_in, mla_w_uq, mla_w_ukv, gla_w_a2, gla_b_a):
    offs = np.concatenate([[0], np.cumsum(IN_SIZES)])
    seg = [w_in[:, offs[i]:offs[i + 1]] for i in range(len(IN_SIZES))]
    w_cq, w_ckv, w_kr, w_gq, w_gk, w_gv, w_gr, w_af, w_ab, w_hy, w_gate = seg
    d = w_in.shape[0]
    z = lambda n: jnp.zeros((d, n), w_in.dtype)
    kr_tile = jnp.concatenate([z(MLA_NOPE), w_kr, z(HEAD_SLOT - MLA_NOPE - MLA_ROPE)], axis=1)
    krp_tile = jnp.concatenate([z(MLA_NOPE), _rope_partner(w_kr), z(HEAD_SLOT - MLA_NOPE - MLA_ROPE)], axis=1)
    a_tile = jnp.concatenate([w_af, w_ab, z(LANES - 2 * GLA_GATE_RANK)], axis=1)
    wa = jnp.concatenate([w_cq, w_ckv, kr_tile, krp_tile, a_tile], axis=1)
    wg = jnp.concatenate([w_gq, w_gk, w_gv, w_gr], axis=1)

    dh = MLA_NOPE + MLA_ROPE
    zq = lambda n: jnp.zeros((MLA_Q_LORA, n), w_in.dtype)
    plain, partner = [], []
    for hd in range(MLA_HEADS):
        blk = mla_w_uq[:, hd * dh:(hd + 1) * dh]
        plain += [blk, zq(HEAD_SLOT - dh)]
        partner += [zq(MLA_NOPE), _rope_partner(blk[:, MLA_NOPE:]), zq(HEAD_SLOT - dh)]
    wuq = jnp.concatenate(plain + partner, axis=1)

    zk = jnp.zeros((MLA_KV_LORA, HEAD_SLOT - MLA_NOPE), w_in.dtype)
    kcols, vcols = [], []
    for hd in range(MLA_HEADS):
        blk = mla_w_ukv[:, hd * (MLA_NOPE + MLA_V):(hd + 1) * (MLA_NOPE + MLA_V)]
        kcols += [blk[:, :MLA_NOPE], zk]
        vcols += [blk[:, MLA_NOPE:], zk]
    wukv = jnp.concatenate(kcols + vcols, axis=1)

    wa2 = jnp.zeros((LANES, 2 * GLA_DK), w_in.dtype)
    wa2 = wa2.at[0:GLA_GATE_RANK, 0:GLA_DK].set(gla_w_a2[0])
    wa2 = wa2.at[GLA_GATE_RANK:2 * GLA_GATE_RANK, GLA_DK:].set(gla_w_a2[1])
    ba = jnp.concatenate([gla_b_a[0], gla_b_a[1]])[None, :]
    bf = lambda t: t.astype(BF16)
    return bf(wa), bf(wg), bf(w_hy), bf(w_gate), bf(wuq), bf(wukv), bf(wa2), ba


def _rope_tables(seq, ctx_len):
    rows = seq // GRID_W
    row = np.repeat(np.arange(rows, dtype=np.float64), GRID_W)
    col = np.tile(np.arange(GRID_W, dtype=np.float64), rows)
    a = MLA_ROPE // 4
    inv = ROPE_BASE ** (-np.arange(a, dtype=np.float64) / a)
    ang_r = row[:, None] * inv
    ang_c = col[:, None] * inv
    cos32 = np.concatenate([np.cos(ang_r), np.cos(ang_r), np.cos(ang_c), np.cos(ang_c)], axis=1)
    sin32 = np.concatenate([np.sin(ang_r), np.sin(ang_r), np.sin(ang_c), np.sin(ang_c)], axis=1)
    pad_r = HEAD_SLOT - MLA_NOPE - MLA_ROPE
    cos = np.concatenate([np.ones((seq, MLA_NOPE)), cos32, np.zeros((seq, pad_r))], axis=1)
    sin = np.concatenate([np.zeros((seq, MLA_NOPE)), sin32, np.zeros((seq, pad_r))], axis=1)
    cos_c = np.concatenate([np.ones((ctx_len, MLA_NOPE + MLA_ROPE)), np.zeros((ctx_len, pad_r))], axis=1)
    sin_c = np.zeros((ctx_len, HEAD_SLOT))
    return (jnp.asarray(np.concatenate([cos, cos_c], axis=0), F32),
            jnp.asarray(np.concatenate([sin, sin_c], axis=0), F32))


def _inproj(x_lat, x_ctx, ctx_blk, mod_l, g1, weights, q_norm, kv_norm, cos, sin, n_lat_tiles):
    bsz, _, d = x_lat.shape
    nt = n_lat_tiles + 1
    t = nt * TM
    wa, wg, wh, wz, wuq, wukv, wa2, ba = weights
    tile = lambda w: pl.BlockSpec((1, TM, w), lambda b, i: (b, i, 0))
    row = lambda w: pl.BlockSpec((1, w), lambda b, i: (0, 0))
    mod_spec = pl.BlockSpec((1, 6, d), lambda b, i: (jnp.where(i < n_lat_tiles, b, bsz), 0, 0))
    tab = pl.BlockSpec((TM, HEAD_SLOT), lambda b, i: (i, 0))
    nq = MLA_HEADS * HEAD_SLOT
    sds = lambda w, dt: jax.ShapeDtypeStruct((bsz, t, w), dt)
    out_shape = (sds(nq, BF16), sds(nq, BF16), sds(nq, BF16), sds(GLA_DK, BF16), sds(GLA_DK, BF16),
                 sds(GLA_DV, BF16), sds(GLA_DV, BF16), sds(2 * GLA_DK, F32), sds(3 * HY_WIDTH, F32),
                 sds(3 * D_MODEL, BF16))
    out_specs = (tile(nq), tile(nq), tile(nq), tile(GLA_DK), tile(GLA_DK), tile(GLA_DV), tile(GLA_DV),
                 tile(2 * GLA_DK), tile(3 * HY_WIDTH), tile(3 * D_MODEL))
    return pl.pallas_call(
        functools.partial(_inproj_kernel, n_lat_tiles=n_lat_tiles),
        out_shape=out_shape,
        grid=(bsz, nt),
        in_specs=_token_specs(d, n_lat_tiles, ctx_blk) + [
            mod_spec, row(d), _const_spec(wa.shape), _const_spec(wg.shape), _const_spec(wh.shape),
            _const_spec(wz.shape), row(MLA_Q_LORA), row(MLA_KV_LORA), _const_spec(wuq.shape),
            _const_spec(wukv.shape), _const_spec(wa2.shape), row(2 * GLA_DK), tab, tab],
        out_specs=out_specs,
        compiler_params=_cparams(2, VMEM_LIMIT),
        name="inproj",
    )(x_lat, x_ctx, mod_l, g1[None, :], wa, wg, wh, wz, q_norm[None, :], kv_norm[None, :], wuq, wukv, wa2, ba,
      cos, sin)


ATT_TK = 1024


ATT_TQ = 1024
ATT_HEADS = 2


def _attn_kernel(q_ref, k_ref, v_ref, o_ref, m_ref, acc_ref, *, chunks):
    tq = q_ref.shape[1]
    n_heads = q_ref.shape[2] // HEAD_SLOT
    m_ref[...] = jnp.full(m_ref.shape, -jnp.inf, F32)
    acc_ref[...] = jnp.zeros(acc_ref.shape, F32)
    for r0, size in chunks:
        k = k_ref[0, pl.ds(r0, size), :]
        v = v_ref[0, pl.ds(r0, size), :]
        for hd in range(n_heads):
            sl = slice(hd * HEAD_SLOT, (hd + 1) * HEAD_SLOT)
            s = _dot_nt(q_ref[0, :, sl], k[:, sl])
            m_prev = m_ref[hd]
            m_new = jnp.maximum(m_prev, jnp.max(s, axis=1, keepdims=True))
            p = jnp.exp2((s - jnp.concatenate([m_new] * (size // LANES), axis=1)).astype(BF16))
            acc_ref[hd] = jnp.exp2(m_prev - m_new) * acc_ref[hd] + _dot(p, v[:, sl])
            m_ref[hd] = m_new
    lane = lax.broadcasted_iota(jnp.int32, (tq, HEAD_SLOT), 1)
    for pr in range(n_heads // 2):
        a0 = acc_ref[2 * pr]
        a1 = acc_ref[2 * pr + 1]
        o0 = a0 / pltpu.roll(a0, MLA_V, 1)
        o1 = pltpu.roll(a1, MLA_V, 1) / a1
        o_ref[0, :, pr * HEAD_SLOT:(pr + 1) * HEAD_SLOT] = jnp.where(lane < MLA_V, o0, o1).astype(o_ref.dtype)


def _attention(q, k, v, seq, with_ctx_queries):
    bsz, t, _ = q.shape
    ctx_len = t - seq
    assert seq % ATT_TK == 0 and seq % ATT_TQ == 0 and seq % ctx_len == 0 and 2 * MLA_V == HEAD_SLOT
    pair = 2 * HEAD_SLOT
    n_chunks = seq // ATT_TK
    chunks = tuple((j * ATT_TK, ATT_TK) for j in range(n_chunks - 1))
    chunks += (((n_chunks - 1) * ATT_TK, ATT_TK + ctx_len),)
    scratch = lambda nh, tq: [pltpu.VMEM((nh, tq, LANES), F32), pltpu.VMEM((nh, tq, HEAD_SLOT), F32)]
    hps = ATT_HEADS
    y = pl.pallas_call(
        functools.partial(_attn_kernel, chunks=chunks),
        out_shape=jax.ShapeDtypeStruct((bsz, seq, MLA_OUT), BF16),
        grid=(bsz, MLA_HEADS // hps, seq // ATT_TQ),
        in_specs=[
            pl.BlockSpec((1, ATT_TQ, hps * HEAD_SLOT), lambda b, hp, i: (b, i, hp)),
            pl.BlockSpec((1, t, hps * HEAD_SLOT), lambda b, hp, i: (b, 0, hp)),
            pl.BlockSpec((1, t, hps * HEAD_SLOT), lambda b, hp, i: (b, 0, hp)),
        ],
        out_specs=pl.BlockSpec((1, ATT_TQ, hps * MLA_V), lambda b, hp, i: (b, i, hp)),
        scratch_shapes=scratch(hps, ATT_TQ),
        compiler_params=_cparams(3, VMEM_LIMIT),
        name="mla_attention",
    )(q, k, v)
    if not with_ctx_queries:
        return y, None
    cblk = seq // ctx_len
    ctx_rows = lambda w: pl.BlockSpec((1, ctx_len, w), lambda b, hp: (b, cblk, hp))
    y_ctx = pl.pallas_call(
        functools.partial(_attn_kernel, chunks=((0, ctx_len),)),
        out_shape=jax.ShapeDtypeStruct((bsz, ctx_len, MLA_OUT), BF16),
        grid=(bsz, MLA_HEADS // 2),
        in_specs=[ctx_rows(pair), ctx_rows(pair), ctx_rows(pair)],
        out_specs=pl.BlockSpec((1, ctx_len, HEAD_SLOT), lambda b, hp: (b, 0, hp)),
        scratch_shapes=scratch(2, ctx_len),
        compiler_params=_cparams(2),
        name="mla_attention_ctx",
    )(q, k, v)
    return y, y_ctx


GLA_LEVELS = int(math.log2(TM))
GLA_SAFE_SPAN = 60.0


def _gla_level_matrices():
    i = np.arange(TM)[:, None]
    t = np.arange(TM)[None, :]
    fwd = [(t <= i)]
    for lv in range(GLA_LEVELS):
        m = TM >> (lv + 1)
        lo = (i // m) * m
        later = ((i // m) % 2) == 1
        q_part = later & (t >= lo) & (t <= i)
        k_part = (~later) & (t > i) & (t <= lo + m - 1)
        fwd.append(q_part | k_part)
    fwd = np.concatenate(fwd, axis=0).astype(np.float32)
    nb = 1 + GLA_LEVELS
    bwd = fwd.reshape(nb, TM, TM)[:, ::-1, ::-1].reshape(nb * TM, TM)
    return np.stack([fwd, bwd])


def _gla_kernel(qf_ref, kf_ref, vf_ref, gf_ref, qb_ref, kb_ref, vb_ref, gb_ref, mall_ref, of_ref, ob_ref, s_ref, a_ref):
    step = pl.program_id(1)

    @pl.when(step == 0)
    def _():
        s_ref[...] = jnp.zeros(s_ref.shape, F32)

    ins = ((qf_ref, kf_ref, vf_ref, gf_ref), (qb_ref, kb_ref, vb_ref, gb_ref))
    outs = (of_ref, ob_ref)
    qs, ks, vs, g2s, g_cums, g_tots = [], [], [], [], [], []
    for d in range(2):
        q_ref, k_ref, v_ref, g_ref = ins[d]
        qs.append(q_ref[0].astype(F32))
        ks.append(k_ref[0].astype(F32))
        vs.append(v_ref[0])
        g = g_ref[0]
        g2 = jnp.concatenate([g.astype(BF16), (g - g.astype(BF16).astype(F32)).astype(BF16)], axis=1)
        e2 = _dot(mall_ref[d, 0:TM, :], g2)
        g2s.append(g2)
        g_cums.append(e2[:, :GLA_DK] + e2[:, GLA_DK:])
        g_tots.append(jnp.sum(g, axis=0, keepdims=True))

    row = lax.broadcasted_iota(jnp.int32, (TM, TM), 0)
    col = lax.broadcasted_iota(jnp.int32, (TM, TM), 1)
    lane_head = lax.broadcasted_iota(jnp.int32, (TM, GLA_DK), 1) // GLA_HK
    tok = lax.broadcasted_iota(jnp.int32, (TM, GLA_DK), 0)
    eye = row == col

    def stack_heads(t):
        return jnp.concatenate([jnp.where(lane_head == hd, t, 0.0) for hd in range(GLA_HEADS)], axis=0).astype(BF16)

    span = jnp.maximum(jnp.max(-g_tots[0]), jnp.max(-g_tots[1]))

    @pl.when(span < GLA_SAFE_SPAN)
    def _():
        for d in range(2):
            res = _dot_nt(stack_heads(qs[d] * jnp.exp(g_cums[d])), (ks[d] * jnp.exp(-g_cums[d])).astype(BF16))
            seen = (col <= row) if d == 0 else (col >= row)
            for hd in range(GLA_HEADS):
                a_ref[d, hd] = jnp.where(seen, res[hd * TM:(hd + 1) * TM], 0.0)

    @pl.when(span >= GLA_SAFE_SPAN)
    def _():
        for d in range(2):
            q, k = qs[d], ks[d]
            e2l = _dot(mall_ref[d, TM:, :], g2s[d])
            e_lv = e2l[:, :GLA_DK] + e2l[:, GLA_DK:]
            res = _dot_nt(stack_heads(q), k.astype(BF16))
            for hd in range(GLA_HEADS):
                a_ref[d, hd] = jnp.where(eye, res[hd * TM:(hd + 1) * TM], 0.0)
            for lv in range(GLA_LEVELS):
                m = TM >> (lv + 1)
                w = jnp.exp(e_lv[lv * TM:(lv + 1) * TM])
                bit = (tok // m) % 2
                q_act = bit != d
                qt = jnp.where(q_act, q * w, 0.0)
                kt = jnp.where(q_act, 0.0, k * w).astype(BF16)
                res = _dot_nt(stack_heads(qt), kt)
                if m == TM // 2:
                    for hd in range(GLA_HEADS):
                        a_ref[d, hd] += res[hd * TM:(hd + 1) * TM]
                else:
                    same = (row // (2 * m)) == (col // (2 * m))
                    for hd in range(GLA_HEADS):
                        a_ref[d, hd] += jnp.where(same, res[hd * TM:(hd + 1) * TM], 0.0)

    same_head = (lax.broadcasted_iota(jnp.int32, (GLA_DK, GLA_DV), 0) // GLA_HK
                 == lax.broadcasted_iota(jnp.int32, (GLA_DK, GLA_DV), 1) // GLA_HV)
    for d in range(2):
        q, k, v, g_cum, g_tot = qs[d], ks[d], vs[d], g_cums[d], g_tots[d]
        s_old = s_ref[d]
        o_inter = _dot((q * jnp.exp(g_cum)).astype(BF16), s_old.astype(BF16))
        for hd in range(GLA_HEADS):
            sl = slice(hd * GLA_HV, (hd + 1) * GLA_HV)
            o_intra = _dot(a_ref[d, hd].astype(BF16), v[:, sl])
            outs[d][0, :, sl] = (o_intra + o_inter[:, sl]).astype(outs[d].dtype)

        kdec_t = (k * jnp.exp(g_tot - g_cum)).T.astype(BF16)
        upd = _dot(kdec_t, v)
        dec_col = jnp.sum(jnp.where(eye, jnp.broadcast_to(jnp.exp(g_tot), (TM, GLA_DK)), 0.0), axis=1,
                          keepdims=True)
        s_ref[d] = dec_col * s_old + jnp.where(same_head, upd, 0.0)


def _gla(gq, gk, gv, glog, n_lat_tiles):
    assert GLA_DK == TM
    bsz, t, _ = gq.shape
    nt = t // TM
    mall = jnp.asarray(_gla_level_matrices(), dtype=BF16)

    def tile_idx(d, s):
        return jnp.where(s == 0, n_lat_tiles, s - 1 if d == 0 else n_lat_tiles - s)

    def specs(d):
        tok = lambda w: pl.BlockSpec((1, TM, w), lambda b, s: (b, tile_idx(d, s), 0))
        return [tok(GLA_DK), tok(GLA_DK), tok(GLA_DV), pl.BlockSpec((1, TM, GLA_DK), lambda b, s: (b, tile_idx(d, s), d))]

    out_spec = lambda d: pl.BlockSpec((1, TM, GLA_DV), lambda b, s: (b, tile_idx(d, s), 0))
    sds = jax.ShapeDtypeStruct((bsz, t, GLA_DV), BF16)
    return pl.pallas_call(
        _gla_kernel,
        out_shape=(sds, sds),
        grid=(bsz, nt),
        in_specs=specs(0) + specs(1) + [_const_spec(mall.shape)],
        out_specs=(out_spec(0), out_spec(1)),
        scratch_shapes=[pltpu.VMEM((2, GLA_DK, GLA_DV), F32), pltpu.VMEM((2, GLA_HEADS, TM, TM), F32)],
        compiler_params=_cparams(2),
        name="gla_scan",
    )(gq, gk, gv, glog, gq, gk, gv, glog, mall)


HY_N1 = 64
HY_N2 = 128
HY_SLAB = SUBLANES
HY_CB = 256
HY_KG = 16
HY_SPEC_CB = 128
HY_UNROLL = 16


def _hyena_dft_constants(seq):
    n = 2 * seq
    assert n == HY_N1 * HY_N2
    nh = HY_N1 // 2
    kh = HY_N2 // 2
    eye = np.eye(HY_SLAB)
    k1 = np.arange(HY_N1)
    th = 2 * np.pi * np.outer(k1 + 0.5, np.arange(nh)) / HY_N1
    fwd1 = np.concatenate([np.kron(np.cos(th), eye), np.kron(-np.sin(th), eye)], axis=0)
    inv1 = (2.0 / n) * np.concatenate([np.kron(np.cos(th).T, eye), np.kron(-np.sin(th).T, eye)], axis=1)
    nlo = np.arange(HY_N2)
    k2 = np.arange(kh)
    ph = 2 * np.pi * (k2[None, :, None] * nlo[None, None, :] / HY_N2
                      + (k1[:, None, None] + 0.5) * nlo[None, None, :] / n)
    c, s = np.cos(ph), np.sin(ph)
    fwd2 = np.concatenate([np.concatenate([c, s], axis=2), np.concatenate([-s, c], axis=2)], axis=1)
    ct, st = c.transpose(0, 2, 1), s.transpose(0, 2, 1)
    inv2 = np.concatenate([np.concatenate([ct, -st], axis=2), np.concatenate([st, ct], axis=2)], axis=1)
    return fwd1, fwd2, inv2, inv1


def _hyena_ctx_dft_constants(ctx_len):
    n = 2 * ctx_len
    th = 2 * np.pi * np.outer(np.arange(ctx_len) + 0.5, np.arange(ctx_len)) / n
    fwd = np.concatenate([np.cos(th), -np.sin(th)], axis=0)
    inv = (2.0 / n) * np.concatenate([np.cos(th).T, -np.sin(th).T], axis=1)
    return fwd, inv


def _hyena_features(length):
    pos = np.arange(length, dtype=np.float64)
    t = pos / max(length - 1, 1)
    f = np.linspace(1e-4, HY_BANDS - 1, HY_BANDS)
    ang = (2.0 * math.pi / length) * pos[:, None] * f
    feat = np.concatenate([t[:, None], np.cos(ang), np.sin(ang)], axis=-1)
    return jnp.asarray(np.pad(feat, ((0, 0), (0, LANES - HY_POS_DIM))), F32)


def _hy_filter_kernel(feat_ref, w1_ref, b1_ref, w2_ref, b2_ref, w3_ref, b3_ref, absd_ref, h_ref, s_ref):
    i = pl.program_id(0)
    feat = feat_ref[...]
    hp = _dot3
    hdn = jnp.sin(hp(feat, w1_ref[...]) + b1_ref[...])
    hdn = jnp.sin(hp(hdn, w2_ref[...]) + b2_ref[...])
    h = hp(hdn, w3_ref[...]) + b3_ref[...]
    window = jnp.exp(-feat[:, 0:1] * absd_ref[...])
    h = h * jnp.concatenate([window] * (2 * HY_ORDER), axis=1)
    h_ref[...] = h

    @pl.when(i == 0)
    def _():
        s_ref[...] = jnp.zeros(s_ref.shape, F32)

    s_ref[...] += jnp.sum(jnp.abs(h), axis=0, keepdims=True)


def _hyena_filters_raw(length, filt_w):
    w1, b1, w2, b2, w3, b3 = filt_w
    nf = 2 * HY_ORDER * HY_WIDTH
    tr = min(length, 512)
    deltas = np.linspace(math.log(HY_DECAY_TARGET) / HY_FAST_DECAY, math.log(HY_DECAY_TARGET) / HY_SLOW_DECAY,
                         HY_WIDTH, dtype=np.float32)
    absd = jnp.asarray(np.abs(deltas))[None, :]
    w1p = jnp.pad(w1, ((0, LANES - HY_POS_DIM), (0, 0)))
    full = lambda shp: pl.BlockSpec(shp, lambda i: (0,) * len(shp))
    return pl.pallas_call(
        _hy_filter_kernel,
        out_shape=(jax.ShapeDtypeStruct((length, nf), F32), jax.ShapeDtypeStruct((1, nf), F32)),
        grid=(length // tr,),
        in_specs=[pl.BlockSpec((tr, LANES), lambda i: (i, 0)), full((LANES, HY_FILTER_HIDDEN)),
                  full((1, HY_FILTER_HIDDEN)), full((HY_FILTER_HIDDEN, HY_FILTER_HIDDEN)), full((1, HY_FILTER_HIDDEN)),
                  full((HY_FILTER_HIDDEN, nf)), full((1, nf)), full((1, HY_WIDTH))],
        out_specs=(pl.BlockSpec((tr, nf), lambda i: (i, 0)), full((1, nf))),
        compiler_params=_cparams(1),
        name="hyena_filter_mlp",
    )(_hyena_features(length), w1p, b1[None, :], w2, b2[None, :], w3, b3[None, :], absd)


def _dot_split(m, x):
    x_hi = x.astype(BF16)
    x_lo = (x - x_hi.astype(F32)).astype(BF16)
    n = x.shape[1]
    r = _dot(m, jnp.concatenate([x_hi, x_lo], axis=1))
    return r[:, :n] + r[:, n:]


def _odft_stage1(src_at, mm, s_re, s_im, unroll=2):
    nk = s_re.shape[0]
    half = nk * HY_SLAB

    def body(j, carry):
        r0 = pl.multiple_of(j * HY_SLAB, HY_SLAB)
        slab = src_at(r0)
        cb = slab.shape[-1]
        res = mm(slab.reshape(-1, cb))
        s_re[:, pl.ds(r0, HY_SLAB), :] = res[:half].reshape(nk, HY_SLAB, cb)
        s_im[:, pl.ds(r0, HY_SLAB), :] = res[half:].reshape(nk, HY_SLAB, cb)
        return carry

    lax.fori_loop(0, HY_N2 // HY_SLAB, body, 0, unroll=unroll)


def _hy_spectrum_kernel(hf_ref, hb_ref, sf_ref, sb_ref, fwd1_ref, fwd2_ref, o_ref, s_re, s_im):
    kh = HY_N2 // 2
    mm1 = lambda x: _dot_split(fwd1_ref[...], x)

    def middle(sign):
        def body(k1, carry):
            a = jnp.concatenate([s_re[k1], s_im[k1]], axis=0)
            x = _dot_split(fwd2_ref[k1], a)
            if sign is None:
                o_ref[0, k1] = x[:kh]
                o_ref[1, k1] = x[kh:]
            else:
                inv_norm = 1.0 / (sf_ref[...] + sb_ref[...])
                o_ref[0, k1] = (o_ref[0, k1] + x[:kh]) * inv_norm
                o_ref[1, k1] = (o_ref[1, k1] - x[kh:]) * inv_norm
            return carry
        lax.fori_loop(0, HY_N1, body, 0, unroll=16)

    _odft_stage1(lambda r0: hf_ref[:, pl.ds(r0, HY_SLAB), :], mm1, s_re, s_im, unroll=8)
    middle(None)

    def bwd_slab(r0):
        slab = hb_ref[:, pl.ds(r0, HY_SLAB), :]
        nhi = lax.broadcasted_iota(jnp.int32, slab.shape, 0)
        r = lax.broadcasted_iota(jnp.int32, slab.shape, 1)
        return jnp.where((nhi == 0) & (r + r0 == 0), 0.0, slab)

    _odft_stage1(bwd_slab, mm1, s_re, s_im, unroll=8)
    middle(-1)


def _hyena_filter_spectrum(seq, w1, b1, w2, b2, w3, b3):
    h_raw, s = _hyena_filters_raw(seq, (w1, b1, w2, b2, w3, b3))
    nh = HY_N1 // 2
    nc = HY_ORDER * HY_WIDTH
    h3 = h_raw.reshape(nh, HY_N2, 2 * nc)
    fwd1, fwd2, _, _ = _hyena_dft_constants(seq)
    fwd1 = jnp.asarray(fwd1, BF16)
    fwd2 = jnp.asarray(fwd2, BF16)
    scb = HY_SPEC_CB
    ncb = nc // scb
    return pl.pallas_call(
        _hy_spectrum_kernel,
        out_shape=jax.ShapeDtypeStruct((2, HY_N1, HY_N2 // 2, nc), F32),
        grid=(ncb,),
        in_specs=[pl.BlockSpec((nh, HY_N2, scb), lambda c: (0, 0, c)),
                  pl.BlockSpec((nh, HY_N2, scb), lambda c: (0, 0, ncb + c)),
                  pl.BlockSpec((1, scb), lambda c: (0, c)),
                  pl.BlockSpec((1, scb), lambda c: (0, ncb + c)),
                  _const_spec(fwd1.shape), _const_spec(fwd2.shape)],
        out_specs=pl.BlockSpec((2, HY_N1, HY_N2 // 2, scb), lambda c: (0, 0, 0, c)),
        scratch_shapes=[pltpu.VMEM((HY_N1, HY_N2, scb), F32), pltpu.VMEM((HY_N1, HY_N2, scb), F32)],
        compiler_params=_cparams(1, VMEM_LIMIT),
        name="hyena_filter_spectrum",
    )(h3, h3, s, s, fwd1, fwd2)


def _short_conv_chunk(ref, c, n_chunks, w_ref, b_ref):
    per = TM // HY_N2
    cur = ref[0, pl.ds(per * c, per)]
    cb = cur.shape[-1]
    cur = cur.reshape(TM, cb)
    prev = ref[0, jnp.maximum(per * c - 1, 0), pl.ds(HY_N2 - SUBLANES, SUBLANES), :][SUBLANES - 1:SUBLANES]
    nxt = ref[0, jnp.minimum(per * c + per, per * n_chunks - 1), pl.ds(0, SUBLANES), :][0:1]
    prev = jnp.where(c > 0, prev, 0.0)
    nxt = jnp.where(c < n_chunks - 1, nxt, 0.0)
    rowi = lax.broadcasted_iota(jnp.int32, (TM, cb), 0)
    dn = jnp.where(rowi == 0, prev, pltpu.roll(cur, 1, 0))
    up = jnp.where(rowi == TM - 1, nxt, pltpu.roll(cur, TM - 1, 0))
    return b_ref[...] + w_ref[0:1, :] * dn + w_ref[1:2, :] * cur + w_ref[2:3, :] * up


def _hy_conv_kernel(*refs, conv_y):
    if conv_y:
        (y_ref, g_ref, h_ref, bias_ref, wy_ref, by_ref, wg_ref, bg_ref, fwd1_ref, fwd2_ref, inv2_ref, inv1_ref,
         o_ref, s_re, s_im, gs_ref, us_ref) = refs
    else:
        (y_ref, g_ref, h_ref, bias_ref, wg_ref, bg_ref, fwd1_ref, fwd2_ref, inv2_ref, inv1_ref,
         o_ref, s_re, s_im, gs_ref) = refs
    grp = pl.program_id(2)
    last = pl.num_programs(2) - 1
    nh = HY_N1 // 2
    per = TM // HY_N2
    n_chunks = nh // per
    cb = o_ref.shape[-1]
    kh = HY_N2 // 2

    @pl.when(grp == 0)
    def _():
        def pre(c, carry):
            gs_ref[pl.ds(per * c, per)] = _short_conv_chunk(g_ref, c, n_chunks, wg_ref, bg_ref).reshape(per, HY_N2, cb)
            if conv_y:
                us_ref[pl.ds(per * c, per)] = _short_conv_chunk(y_ref, c, n_chunks, wy_ref, by_ref).reshape(
                    per, HY_N2, cb)
            return carry
        lax.fori_loop(0, n_chunks, pre, 0)

    if conv_y:
        u_at = lambda r0: us_ref[:, pl.ds(r0, HY_SLAB), :]
    else:
        u_at = lambda r0: y_ref[0, :, pl.ds(r0, HY_SLAB), :]

    _odft_stage1(u_at, lambda x: _dot(fwd1_ref[0], x.astype(BF16)), s_re, s_im, unroll=HY_UNROLL)

    def middle(k1, carry):
        a = jnp.concatenate([s_re[k1], s_im[k1]], axis=0).astype(BF16)
        x = _dot(fwd2_ref[k1], a)
        xr, xi = x[:kh], x[kh:]
        hr, hi = h_ref[0, k1], h_ref[1, k1]
        y = jnp.concatenate([xr * hr - xi * hi, xr * hi + xi * hr], axis=0).astype(BF16)
        bm = _dot(inv2_ref[k1], y)
        s_re[k1] = bm[:HY_N2]
        s_im[k1] = bm[HY_N2:]
        return carry

    lax.fori_loop(0, HY_KG, middle, 0, unroll=HY_KG)

    def partial_conv(r0):
        slab = jnp.concatenate([s_re[:, pl.ds(r0, HY_SLAB), :].reshape(HY_KG * HY_SLAB, cb),
                                s_im[:, pl.ds(r0, HY_SLAB), :].reshape(HY_KG * HY_SLAB, cb)], axis=0).astype(BF16)
        return _dot(inv1_ref[0], slab).reshape(nh, HY_SLAB, cb)

    def post_loop(fn):
        def post(j, carry):
            r0 = pl.multiple_of(j * HY_SLAB, HY_SLAB)
            o_ref[0, :, pl.ds(r0, HY_SLAB), :] = fn(r0, partial_conv(r0))
            return carry
        lax.fori_loop(0, HY_N2 // HY_SLAB, post, 0, unroll=HY_UNROLL)

    @pl.when(grp == 0)
    def _():
        post_loop(lambda r0, part: part)

    @pl.when((grp > 0) & (grp < last))
    def _():
        post_loop(lambda r0, part: o_ref[0, :, pl.ds(r0, HY_SLAB), :] + part)

    @pl.when(grp == last)
    def _():
        post_loop(lambda r0, part: gs_ref[:, pl.ds(r0, HY_SLAB), :]
                  * (o_ref[0, :, pl.ds(r0, HY_SLAB), :] + part + bias_ref[...] * u_at(r0)))


def _hyena_order(y4, y_col0, z4, gate_col0, hspec, order, hy_bias, short_w, short_b, consts, conv_y):
    bsz = z4.shape[0]
    nh = HY_N1 // 2
    ncb = HY_WIDTH // HY_CB
    ngrp = HY_N1 // HY_KG
    fwd1, fwd2, inv2, inv1 = consts
    blk4 = lambda off: pl.BlockSpec((1, nh, HY_N2, HY_CB), lambda c, b, g: (b, 0, 0, off + c))
    rowspec = lambda rows, off: pl.BlockSpec((rows, HY_CB), lambda c, b, g: (0, off + c))
    grouped = lambda shp: pl.BlockSpec(shp, lambda c, b, g: (g, 0, 0))
    in_specs = [blk4(y_col0), blk4(gate_col0),
                pl.BlockSpec((2, HY_KG, HY_N2 // 2, HY_CB), lambda c, b, g: (0, g, 0, order * ncb + c)),
                rowspec(1, 0)]
    args = [y4, z4, hspec, hy_bias.reshape(1, -1)]
    if conv_y:
        in_specs += [rowspec(HY_SHORT, y_col0), rowspec(1, y_col0)]
        args += [short_w, short_b[None, :]]
    in_specs += [rowspec(HY_SHORT, gate_col0), rowspec(1, gate_col0)]
    args += [short_w, short_b[None, :]]
    in_specs += [grouped((1,) + fwd1.shape[1:]), grouped((HY_KG,) + fwd2.shape[1:]),
                 grouped((HY_KG,) + inv2.shape[1:]), grouped((1,) + inv1.shape[1:])]
    args += [fwd1, fwd2, inv2, inv1]
    scratch = [pltpu.VMEM((HY_KG, HY_N2, HY_CB), F32), pltpu.VMEM((HY_KG, HY_N2, HY_CB), F32),
               pltpu.VMEM((nh, HY_N2, HY_CB), F32)]
    if conv_y:
        scratch.append(pltpu.VMEM((nh, HY_N2, HY_CB), F32))
    return pl.pallas_call(
        functools.partial(_hy_conv_kernel, conv_y=conv_y),
        out_shape=jax.ShapeDtypeStruct((bsz, nh, HY_N2, HY_WIDTH), F32),
        grid=(ncb, bsz, ngrp),
        in_specs=in_specs,
        out_specs=pl.BlockSpec((1, nh, HY_N2, HY_CB), lambda c, b, g: (b, 0, 0, c)),
        scratch_shapes=scratch,
        compiler_params=_cparams(3, VMEM_LIMIT),
        name="hyena_conv%d" % order,
    )(*args)


def _hyena_latent(z_hy, hspec, short_w, short_b, hy_bias):
    bsz, t, _ = z_hy.shape
    seq = HY_N1 * HY_N2 // 2
    fwd1, fwd2, inv2, inv1 = _hyena_dft_constants(seq)
    ngrp = HY_N1 // HY_KG
    rows = HY_KG * HY_SLAB
    fwd1 = fwd1.reshape(2, ngrp, rows, -1).transpose(1, 0, 2, 3).reshape(ngrp, 2 * rows, -1)
    inv1 = inv1.reshape(-1, 2, ngrp, rows).transpose(2, 0, 1, 3).reshape(ngrp, -1, 2 * rows)
    consts = tuple(jnp.asarray(m, BF16) for m in (fwd1, fwd2, inv2, inv1))
    z4 = z_hy.reshape(bsz, t // HY_N2, HY_N2, 3 * HY_WIDTH)
    ncb = HY_WIDTH // HY_CB
    y1 = _hyena_order(z4, 2 * ncb, z4, 0, hspec, 0, hy_bias[0], short_w, short_b, consts, True)
    y2 = _hyena_order(y1, 0, z4, ncb, hspec, 1, hy_bias[1], short_w, short_b, consts, False)
    return y2.reshape(bsz, seq, HY_WIDTH)


def _hy_ctx_spectrum_kernel(h_ref, s_ref, fwd_ref, o_ref):
    lc = h_ref.shape[0]
    nc = HY_ORDER * HY_WIDTH
    hp = _dot3
    h = h_ref[...]
    rowi = lax.broadcasted_iota(jnp.int32, (lc, nc), 0)
    xf = hp(fwd_ref[...], h[:, :nc])
    xb = hp(fwd_ref[...], jnp.where(rowi == 0, 0.0, h[:, nc:]))
    inv_norm = 1.0 / (s_ref[:, :nc] + s_ref[:, nc:])
    o_ref[0] = (xf[:lc] + xb[:lc]) * inv_norm
    o_ref[1] = (xf[lc:] - xb[lc:]) * inv_norm


def _hy_ctx_conv_kernel(x1_ref, x2_ref, v_ref, h_ref, bias_ref, w_ref, b_ref, fwd_ref, inv_ref, o_ref):
    lc = o_ref.shape[1]
    rowi = lax.broadcasted_iota(jnp.int32, (lc, HY_WIDTH), 0)

    def short(ref, part):
        cur = ref[0].reshape(lc, HY_WIDTH)
        sl = slice(part * HY_WIDTH, (part + 1) * HY_WIDTH)
        dn = jnp.where(rowi == 0, 0.0, pltpu.roll(cur, 1, 0))
        up = jnp.where(rowi == lc - 1, 0.0, pltpu.roll(cur, lc - 1, 0))
        return b_ref[:, sl] + w_ref[0:1, sl] * dn + w_ref[1:2, sl] * cur + w_ref[2:3, sl] * up

    y = short(v_ref, 2)
    for order, gref in enumerate((x1_ref, x2_ref)):
        sl = slice(order * HY_WIDTH, (order + 1) * HY_WIDTH)
        x = _dot(fwd_ref[...], y.astype(BF16))
        xr, xi = x[:lc], x[lc:]
        hr, hi = h_ref[0, :, sl], h_ref[1, :, sl]
        prod = jnp.concatenate([xr * hr - xi * hi, xr * hi + xi * hr], axis=0).astype(BF16)
        conv = _dot(inv_ref[...], prod)
        y = short(gref, order) * (conv + bias_ref[order:order + 1, :] * y)
    o_ref[0] = y


def _hyena_ctx(z_hy, filt_w, short_w, short_b, hy_bias):
    bsz, t, _ = z_hy.shape
    seq = HY_N1 * HY_N2 // 2
    lc = t - seq
    per = lc // HY_N2
    h_raw, s = _hyena_filters_raw(lc, filt_w)
    fwd, inv = _hyena_ctx_dft_constants(lc)
    nc = HY_ORDER * HY_WIDTH
    full = lambda shp: pl.BlockSpec(shp, lambda *_: (0,) * len(shp))
    hspec = pl.pallas_call(
        _hy_ctx_spectrum_kernel,
        out_shape=jax.ShapeDtypeStruct((2, lc, nc), F32),
        grid=(1,),
        in_specs=[full(h_raw.shape), full(s.shape), full(fwd.shape)],
        out_specs=full((2, lc, nc)),
        compiler_params=_cparams(1),
        name="hyena_ctx_spectrum",
    )(h_raw, s, jnp.asarray(fwd, F32))
    z4 = z_hy.reshape(bsz, t // HY_N2, HY_N2, 3 * HY_WIDTH)
    blk = lambda part: pl.BlockSpec((1, per, HY_N2, HY_WIDTH), lambda b: (b, seq // lc, 0, part))
    return pl.pallas_call(
        _hy_ctx_conv_kernel,
        out_shape=jax.ShapeDtypeStruct((bsz, lc, HY_WIDTH), F32),
        grid=(bsz,),
        in_specs=[blk(0), blk(1), blk(2), full((2, lc, nc)), full((HY_ORDER, HY_WIDTH)),
                  full((HY_SHORT, 3 * HY_WIDTH)), full((1, 3 * HY_WIDTH)), full(fwd.shape), full(inv.shape)],
        out_specs=pl.BlockSpec((1, lc, HY_WIDTH), lambda b: (b, 0, 0)),
        compiler_params=_cparams(1),
        name="hyena_ctx_conv",
    )(z4, z4, z4, hspec, hy_bias, short_w, short_b[None, :], jnp.asarray(fwd, BF16), jnp.asarray(inv, BF16))


FF_CHUNK = 1024


def _merge_mlp_kernel(x_ref, c_ref, mod_ref, ymla_ref, ymlac_ref, of_ref, ob_ref, gr_ref, yhy_ref, yhyc_ref, zg_ref, on_ref,
                      wm_ref, wgl_ref, wh_ref, wo_ref, g2_ref, w1_ref, w2_ref, fg_ref, o_ref, *, final, n_lat_tiles):
    o = of_ref[0].astype(F32) + ob_ref[0].astype(F32)
    silu = gr_ref[0].astype(F32)
    parts = []
    for hd in range(GLA_HEADS):
        sl = slice(hd * GLA_HV, (hd + 1) * GLA_HV)
        parts.append((_rms(o[:, sl]) * on_ref[...] * silu[:, sl]).astype(BF16))
    y_gla = jnp.concatenate(parts, axis=1)
    zg = zg_ref[0].astype(F32)
    d = x_ref.shape[-1]
    m = zg[:, 0:d] * _dot(_token_tile(ymla_ref, ymlac_ref, n_lat_tiles), wm_ref[...])
    m = m + zg[:, d:2 * d] * _dot(y_gla, wgl_ref[...])
    m = m + zg[:, 2 * d:3 * d] * _dot(_token_tile(yhy_ref, yhyc_ref, n_lat_tiles).astype(BF16), wh_ref[...])
    out = _dot(m.astype(BF16), wo_ref[...])
    x = _token_tile(x_ref, c_ref, n_lat_tiles) + mod_ref[0, 2:3, :] * out

    h = (_rms(x) * (g2_ref[...] * (1.0 + mod_ref[0, 4:5, :])) + mod_ref[0, 3:4, :]).astype(BF16)
    acc = jnp.zeros(x.shape, F32)
    for j in range(w1_ref.shape[1] // FF_CHUNK):
        a = jnp.maximum(_dot(h, w1_ref[:, j * FF_CHUNK:(j + 1) * FF_CHUNK]), 0.0)
        acc = acc + _dot((a * a).astype(BF16), w2_ref[j * FF_CHUNK:(j + 1) * FF_CHUNK, :])
    xn = x + mod_ref[0, 5:6, :] * acc
    if final:
        xn = _rms(xn) * fg_ref[...]
    o_ref[0] = xn


def _mod_spec(d, n_lat_tiles, bsz):
    return pl.BlockSpec((1, 6, d), lambda b, i: (jnp.where(i < n_lat_tiles, b, bsz), 0, 0))


def _merge_mlp(x_lat, x_ctx, ctx_blk, mod_l, y_mla, y_mla_ctx, o_gla, gr, y_hy, y_hy_ctx, gate, out_norm,
               w_o_mla, w_o_gla, w_o_hy, w_out, g2, w1, w2, final_g, n_tiles, n_lat_tiles, final):
    bsz, _, d = x_lat.shape
    if y_mla_ctx is None:
        y_mla_ctx, y_hy_ctx = y_mla, y_hy
    tile = lambda w: pl.BlockSpec((1, TM, w), lambda b, i: (b, i, 0))
    row = lambda w: pl.BlockSpec((1, w), lambda b, i: (0, 0))
    bf = lambda w: w.astype(BF16)
    return pl.pallas_call(
        functools.partial(_merge_mlp_kernel, final=final, n_lat_tiles=n_lat_tiles if n_tiles > n_lat_tiles else None),
        out_shape=jax.ShapeDtypeStruct((bsz, n_tiles * TM, d), F32),
        grid=(bsz, n_tiles),
        in_specs=_token_specs(d, n_lat_tiles, ctx_blk) + [_mod_spec(d, n_lat_tiles, bsz)]
        + _token_specs(MLA_OUT, n_lat_tiles, 0) + [tile(GLA_DV), tile(GLA_DV), tile(GLA_DV)]
        + _token_specs(HY_WIDTH, n_lat_tiles, 0) + [
            tile(3 * d), row(GLA_HV),
            _const_spec(w_o_mla.shape), _const_spec(w_o_gla.shape), _const_spec(w_o_hy.shape),
            _const_spec(w_out.shape), row(d), _const_spec(w1.shape), _const_spec(w2.shape), row(d)],
        out_specs=tile(d),
        compiler_params=_cparams(2, VMEM_LIMIT),
        name="merge_mlp",
    )(x_lat, x_ctx, mod_l, y_mla, y_mla_ctx, o_gla[0], o_gla[1], gr, y_hy, y_hy_ctx, gate, out_norm[None, :],
      bf(w_o_mla), bf(w_o_gla), bf(w_o_hy), bf(w_out), g2[None, :], bf(w1), bf(w2), final_g[None, :])


def kernel(x, c, ctx, c_ctx, ada_w, ada_b, norm1_g, norm2_g, w_in, mla_q_norm, mla_w_uq, mla_kv_norm, mla_w_ukv, gla_w_a2, gla_b_a, gla_out_norm, hy_short_w, hy_short_b, hy_f_w1, hy_f_b1, hy_f_w2, hy_f_b2, hy_f_w3, hy_f_b3, hy_bias, w_o_mla, w_o_gla, w_o_hy, w_out, ff_w1, ff_w2, final_norm_g):
    bsz, seq, d = x.shape
    ctx_len = ctx.shape[1]
    n_lat = seq // TM
    assert ctx_len == TM and seq % TM == 0
    n_all = n_lat + 1
    x_lat, x_ctx, ctx_blk = x, ctx, 0
    cc = jnp.zeros((16, d), F32).at[:bsz].set(c).at[bsz].set(c_ctx)
    mod = _modulation(cc, ada_w, ada_b).reshape(DEPTH, 16, 6, d)
    cos, sin = _rope_tables(seq, ctx_len)
    for l in range(DEPTH):
        last = l == DEPTH - 1
        n_tiles = n_lat if last else n_all
        weights = _prep_inproj_weights(w_in[l], mla_w_uq[l], mla_w_ukv[l], gla_w_a2[l], gla_b_a[l])
        q, k, v, gq, gk, gv, gr, glog, z_hy, z_gate = _inproj(x_lat, x_ctx, ctx_blk, mod[l], norm1_g[l], weights,
                                                             mla_q_norm[l], mla_kv_norm[l], cos, sin, n_lat)
        y_mla, y_mla_c = _attention(q, k, v, seq, not last)
        o_gla = _gla(gq, gk, gv, glog, n_lat)
        filt_w = (hy_f_w1[l], hy_f_b1[l], hy_f_w2[l], hy_f_b2[l], hy_f_w3[l], hy_f_b3[l])
        hspec = _hyena_filter_spectrum(seq, *filt_w)
        y_hy = _hyena_latent(z_hy, hspec, hy_short_w[l], hy_short_b[l], hy_bias[l])
        y_hy_c = None if last else _hyena_ctx(z_hy, filt_w, hy_short_w[l], hy_short_b[l], hy_bias[l])
        xc = _merge_mlp(x_lat, x_ctx, ctx_blk, mod[l], y_mla, y_mla_c, o_gla, gr, y_hy, y_hy_c, z_gate,
                        gla_out_norm[l], w_o_mla[l], w_o_gla[l], w_o_hy[l], w_out[l], norm2_g[l], ff_w1[l], ff_w2[l],
                        final_norm_g, n_tiles, n_lat, last)
        x_lat, x_ctx, ctx_blk = xc, xc, n_lat
    return xc
```

```python
import functools
import math

import numpy as np
import jax
import jax.numpy as jnp
from jax import lax
from jax.experimental import pallas as pl
from jax.experimental.pallas import tpu as pltpu

F32 = jnp.float32
BF16 = jnp.bfloat16
LOG2E = 1.4426950408889634

D_MODEL = 1024
DEPTH = 2
GRID_W = 64
EPS = 1e-6
MLA_HEADS = 8
MLA_NOPE = 64
MLA_ROPE = 32
MLA_V = 64
MLA_Q_LORA = 256
MLA_KV_LORA = 128
MLA_SCALE = (MLA_NOPE + MLA_ROPE) ** -0.5
ROPE_BASE = 10000.0
GLA_HEADS = 4
GLA_DK = 256
GLA_DV = 512
GLA_HK = GLA_DK // GLA_HEADS
GLA_HV = GLA_DV // GLA_HEADS
GLA_GATE_RANK = 16
GLA_TAU = 16.0
HY_WIDTH = 512
HY_ORDER = 2
HY_SHORT = 3
HY_BANDS = 16
HY_POS_DIM = 1 + 2 * HY_BANDS
HY_FILTER_HIDDEN = 64
HY_FAST_DECAY = 0.3
HY_SLOW_DECAY = 1.5
HY_DECAY_TARGET = 1e-2
D_FF = 4 * D_MODEL
MLA_OUT = MLA_HEADS * MLA_V
IN_SIZES = (MLA_Q_LORA, MLA_KV_LORA, MLA_ROPE, GLA_DK, GLA_DK, GLA_DV, GLA_DV, GLA_GATE_RANK, GLA_GATE_RANK,
            (HY_ORDER + 1) * HY_WIDTH, 3 * D_MODEL)

LANES = 128
SUBLANES = 8
TM = 256
HEAD_SLOT = 128
V7X_VMEM_BYTES = 64 * 1024 * 1024
VMEM_LIMIT = V7X_VMEM_BYTES * 7 // 8


def _cparams(n_axes, vmem=None):
    return pltpu.CompilerParams(dimension_semantics=("arbitrary",) * n_axes, vmem_limit_bytes=vmem)


def _const_spec(shape):
    nd = len(shape)
    return pl.BlockSpec(shape, lambda *_: (0,) * nd, pipeline_mode=pl.Buffered(1))


def _rms(x):
    return x * lax.rsqrt(jnp.mean(x * x, axis=-1, keepdims=True) + EPS)


def _sigmoid(x):
    return 1.0 / (1.0 + jnp.exp(-x))


def _dot(a, b):
    return jnp.dot(a, b, preferred_element_type=F32)


def _dot_nt(a, b):
    return lax.dot_general(a, b, (((1,), (1,)), ((), ())), preferred_element_type=F32)


def _dot3(a, b):
    a_hi = a.astype(BF16)
    a_lo = (a - a_hi.astype(F32)).astype(BF16)
    b_hi = b.astype(BF16)
    b_lo = (b - b_hi.astype(F32)).astype(BF16)
    m = a.shape[0]
    r = _dot(jnp.concatenate([a_hi, a_lo], axis=0), b_hi)
    return r[:m] + (r[m:] + _dot(a_hi, b_lo))


def _mod_kernel(cc_ref, w_ref, b_ref, o_ref):
    s = cc_ref[...]
    s = s * _sigmoid(s)
    o_ref[0] = _dot3(s, w_ref[0]) + b_ref[0]


def _modulation(cc, ada_w, ada_b):
    tn = 1536
    n6 = ada_w.shape[-1]
    return pl.pallas_call(
        _mod_kernel,
        out_shape=jax.ShapeDtypeStruct((DEPTH, 16, n6), F32),
        grid=(DEPTH, n6 // tn),
        in_specs=[
            pl.BlockSpec((16, D_MODEL), lambda l, j: (0, 0)),
            pl.BlockSpec((1, D_MODEL, tn), lambda l, j: (l, 0, j)),
            pl.BlockSpec((1, 1, tn), lambda l, j: (l, 0, j)),
        ],
        out_specs=pl.BlockSpec((1, 16, tn), lambda l, j: (l, 0, j)),
        compiler_params=_cparams(2),
        name="modulation",
    )(cc, ada_w, ada_b.reshape(DEPTH, 1, n6))


W_A = 768
W_G = 2 * GLA_DK + 2 * GLA_DV


def _token_specs(d, n_lat_tiles, ctx_blk):
    return [pl.BlockSpec((1, TM, d), lambda b, i: (b, jnp.minimum(i, n_lat_tiles - 1), 0)),
            pl.BlockSpec((1, TM, d), lambda b, i: (b, ctx_blk, 0))]


def _token_tile(x_ref, c_ref, n_lat_tiles):
    if n_lat_tiles is None:
        return x_ref[0]
    return jnp.where(pl.program_id(1) < n_lat_tiles, x_ref[0], c_ref[0])


def _inproj_kernel(x_ref, c_ref, mod_ref, g1_ref, wa_ref, wg_ref, wh_ref, wz_ref, qn_ref, kvn_ref, wuq_ref, wukv_ref,
                   wa2_ref, ba_ref, cos_ref, sin_ref,
                   q_out, k_out, v_out, gq_out, gk_out, gv_out, gr_out, glog_out, hy_out, gate_out, *, n_lat_tiles):
    x = _token_tile(x_ref, c_ref, n_lat_tiles)
    shift = mod_ref[0, 0:1, :]
    scale = mod_ref[0, 1:2, :]
    h = (_rms(x) * (g1_ref[...] * (1.0 + scale)) + shift).astype(BF16)

    za = _dot(h, wa_ref[...])
    cos = cos_ref[...]
    sin = sin_ref[...]

    cqn = (_rms(za[:, 0:256]) * qn_ref[...]).astype(BF16)
    qab = _dot(cqn, wuq_ref[...])
    nq = MLA_HEADS * HEAD_SLOT
    for hd in range(MLA_HEADS):
        sl = slice(hd * HEAD_SLOT, (hd + 1) * HEAD_SLOT)
        qa = qab[:, hd * HEAD_SLOT:(hd + 1) * HEAD_SLOT]
        qb = qab[:, nq + hd * HEAD_SLOT:nq + (hd + 1) * HEAD_SLOT]
        q_out[0, :, sl] = ((qa * cos + qb * sin) * (MLA_SCALE * LOG2E)).astype(BF16)

    ckvn = (_rms(za[:, 256:384]) * kvn_ref[...]).astype(BF16)
    kv = _dot(ckvn, wukv_ref[...])
    krot = za[:, 384:512] * cos + za[:, 512:640] * sin
    for hd in range(MLA_HEADS):
        sl = slice(hd * HEAD_SLOT, (hd + 1) * HEAD_SLOT)
        k_out[0, :, sl] = (kv[:, sl] + krot).astype(BF16)
    ones_hi = (lax.broadcasted_iota(jnp.int32, (1, HEAD_SLOT), 1) >= MLA_V).astype(F32)
    for hd in range(MLA_HEADS):
        sl = slice(hd * HEAD_SLOT, (hd + 1) * HEAD_SLOT)
        v_out[0, :, sl] = (kv[:, nq + hd * HEAD_SLOT:nq + (hd + 1) * HEAD_SLOT] + ones_hi).astype(BF16)

    xg = _dot(za[:, 640:768].astype(BF16), wa2_ref[...]) + ba_ref[...]
    glog_out[0] = (jnp.minimum(xg, 0.0) - jnp.log(1.0 + jnp.exp(-jnp.abs(xg)))) * (1.0 / GLA_TAU)

    zg = _dot(h, wg_ref[...])
    gq_out[0] = (zg[:, 0:GLA_DK] * (GLA_HK ** -0.5)).astype(BF16)
    gk_out[0] = zg[:, GLA_DK:2 * GLA_DK].astype(BF16)
    gv_out[0] = zg[:, 2 * GLA_DK:2 * GLA_DK + GLA_DV].astype(BF16)
    gr = zg[:, 2 * GLA_DK + GLA_DV:]
    gr_out[0] = (gr * _sigmoid(gr)).astype(BF16)

    hy_out[0] = _dot(h, wh_ref[...])
    gate_out[0] = _sigmoid(_dot(h, wz_ref[...])).astype(BF16)


def _rope_partner(w):
    a = MLA_ROPE // 4
    perm = np.concatenate([np.arange(a, 2 * a), np.arange(0, a), np.arange(3 * a, 4 * a), np.arange(2 * a, 3 * a)])
    sign = np.concatenate([-np.ones(a), np.ones(a), -np.ones(a), np.ones(a)]).astype(np.float32)
    return w[:, perm] * sign


def _prep_inproj_weights(w_in, mla_w_uq, mla_w_ukv, gla_w_a2, gla_b_a):
    offs = np.concatenate([[0], np.cumsum(IN_SIZES)])
    seg = [w_in[:, offs[i]:offs[i + 1]] for i in range(len(IN_SIZES))]
    w_cq, w_ckv, w_kr, w_gq, w_gk, w_gv, w_gr, w_af, w_ab, w_hy, w_gate = seg
    d = w_in.shape[0]
    z = lambda n: jnp.zeros((d, n), w_in.dtype)
    kr_tile = jnp.concatenate([z(MLA_NOPE), w_kr, z(HEAD_SLOT - MLA_NOPE - MLA_ROPE)], axis=1)
    krp_tile = jnp.concatenate([z(MLA_NOPE), _rope_partner(w_kr), z(HEAD_SLOT - MLA_NOPE - MLA_ROPE)], axis=1)
    a_tile = jnp.concatenate([w_af, w_ab, z(LANES - 2 * GLA_GATE_RANK)], axis=1)
    wa = jnp.concatenate([w_cq, w_ckv, kr_tile, krp_tile, a_tile], axis=1)
    wg = jnp.concatenate([w_gq, w_gk, w_gv, w_gr], axis=1)

    dh = MLA_NOPE + MLA_ROPE
    zq = lambda n: jnp.zeros((MLA_Q_LORA, n), w_in.dtype)
    plain, partner = [], []
    for hd in range(MLA_HEADS):
        blk = mla_w_uq[:, hd * dh:(hd + 1) * dh]
        plain += [blk, zq(HEAD_SLOT - dh)]
        partner += [zq(MLA_NOPE), _rope_partner(blk[:, MLA_NOPE:]), zq(HEAD_SLOT - dh)]
    wuq = jnp.concatenate(plain + partner, axis=1)

    zk = jnp.zeros((MLA_KV_LORA, HEAD_SLOT - MLA_NOPE), w_in.dtype)
    kcols, vcols = [], []
    for hd in range(MLA_HEADS):
        blk = mla_w_ukv[:, hd * (MLA_NOPE + MLA_V):(hd + 1) * (MLA_NOPE + MLA_V)]
        kcols += [blk[:, :MLA_NOPE], zk]
        vcols += [blk[:, MLA_NOPE:], zk]
    wukv = jnp.concatenate(kcols + vcols, axis=1)

    wa2 = jnp.zeros((LANES, 2 * GLA_DK), w_in.dtype)
    wa2 = wa2.at[0:GLA_GATE_RANK, 0:GLA_DK].set(gla_w_a2[0])
    wa2 = wa2.at[GLA_GATE_RANK:2 * GLA_GATE_RANK, GLA_DK:].set(gla_w_a2[1])
    ba = jnp.concatenate([gla_b_a[0], gla_b_a[1]])[None, :]
    bf = lambda t: t.astype(BF16)
    return bf(wa), bf(wg), bf(w_hy), bf(w_gate), bf(wuq), bf(wukv), bf(wa2), ba


def _rope_tables(seq, ctx_len):
    rows = seq // GRID_W
    row = np.repeat(np.arange(rows, dtype=np.float64), GRID_W)
    col = np.tile(np.arange(GRID_W, dtype=np.float64), rows)
    a = MLA_ROPE // 4
    inv = ROPE_BASE ** (-np.arange(a, dtype=np.float64) / a)
    ang_r = row[:, None] * inv
    ang_c = col[:, None] * inv
    cos32 = np.concatenate([np.cos(ang_r), np.cos(ang_r), np.cos(ang_c), np.cos(ang_c)], axis=1)
    sin32 = np.concatenate([np.sin(ang_r), np.sin(ang_r), np.sin(ang_c), np.sin(ang_c)], axis=1)
    pad_r = HEAD_SLOT - MLA_NOPE - MLA_ROPE
    cos = np.concatenate([np.ones((seq, MLA_NOPE)), cos32, np.zeros((seq, pad_r))], axis=1)
    sin = np.concatenate([np.zeros((seq, MLA_NOPE)), sin32, np.zeros((seq, pad_r))], axis=1)
    cos_c = np.concatenate([np.ones((ctx_len, MLA_NOPE + MLA_ROPE)), np.zeros((ctx_len, pad_r))], axis=1)
    sin_c = np.zeros((ctx_len, HEAD_SLOT))
    return (jnp.asarray(np.concatenate([cos, cos_c], axis=0), F32),
            jnp.asarray(np.concatenate([sin, sin_c], axis=0), F32))


def _inproj(x_lat, x_ctx, ctx_blk, mod_l, g1, weights, q_norm, kv_norm, cos, sin, n_lat_tiles):
    bsz, _, d = x_lat.shape
    nt = n_lat_tiles + 1
    t = nt * TM
    wa, wg, wh, wz, wuq, wukv, wa2, ba = weights
    tile = lambda w: pl.BlockSpec((1, TM, w), lambda b, i: (b, i, 0))
    row = lambda w: pl.BlockSpec((1, w), lambda b, i: (0, 0))
    mod_spec = pl.BlockSpec((1, 6, d), lambda b, i: (jnp.where(i < n_lat_tiles, b, bsz), 0, 0))
    tab = pl.BlockSpec((TM, HEAD_SLOT), lambda b, i: (i, 0))
    nq = MLA_HEADS * HEAD_SLOT
    sds = lambda w, dt: jax.ShapeDtypeStruct((bsz, t, w), dt)
    out_shape = (sds(nq, BF16), sds(nq, BF16), sds(nq, BF16), sds(GLA_DK, BF16), sds(GLA_DK, BF16),
                 sds(GLA_DV, BF16), sds(GLA_DV, BF16), sds(2 * GLA_DK, F32), sds(3 * HY_WIDTH, F32),
                 sds(3 * D_MODEL, BF16))
    out_specs = (tile(nq), tile(nq), tile(nq), tile(GLA_DK), tile(GLA_DK), tile(GLA_DV), tile(GLA_DV),
                 tile(2 * GLA_DK), tile(3 * HY_WIDTH), tile(3 * D_MODEL))
    return pl.pallas_call(
        functools.partial(_inproj_kernel, n_lat_tiles=n_lat_tiles),
        out_shape=out_shape,
        grid=(bsz, nt),
        in_specs=_token_specs(d, n_lat_tiles, ctx_blk) + [
            mod_spec, row(d), _const_spec(wa.shape), _const_spec(wg.shape), _const_spec(wh.shape),
            _const_spec(wz.shape), row(MLA_Q_LORA), row(MLA_KV_LORA), _const_spec(wuq.shape),
            _const_spec(wukv.shape), _const_spec(wa2.shape), row(2 * GLA_DK), tab, tab],
        out_specs=out_specs,
        compiler_params=_cparams(2, VMEM_LIMIT),
        name="inproj",
    )(x_lat, x_ctx, mod_l, g1[None, :], wa, wg, wh, wz, q_norm[None, :], kv_norm[None, :], wuq, wukv, wa2, ba,
      cos, sin)


ATT_TK = 1024


ATT_TQ = 1024
ATT_HEADS = 2


def _attn_kernel(q_ref, k_ref, v_ref, o_ref, m_ref, acc_ref, *, chunks):
    tq = q_ref.shape[1]
    n_heads = q_ref.shape[2] // HEAD_SLOT
    m_ref[...] = jnp.full(m_ref.shape, -jnp.inf, F32)
    acc_ref[...] = jnp.zeros(acc_ref.shape, F32)
    for r0, size in chunks:
        k = k_ref[0, pl.ds(r0, size), :]
        v = v_ref[0, pl.ds(r0, size), :]
        for hd in range(n_heads):
            sl = slice(hd * HEAD_SLOT, (hd + 1) * HEAD_SLOT)
            s = _dot_nt(q_ref[0, :, sl], k[:, sl])
            m_prev = m_ref[hd]
            m_new = jnp.maximum(m_prev, jnp.max(s, axis=1, keepdims=True))
            p = jnp.exp2((s - jnp.concatenate([m_new] * (size // LANES), axis=1)).astype(BF16))
            acc_ref[hd] = jnp.exp2(m_prev - m_new) * acc_ref[hd] + _dot(p, v[:, sl])
            m_ref[hd] = m_new
    lane = lax.broadcasted_iota(jnp.int32, (tq, HEAD_SLOT), 1)
    for pr in range(n_heads // 2):
        a0 = acc_ref[2 * pr]
        a1 = acc_ref[2 * pr + 1]
        o0 = a0 / pltpu.roll(a0, MLA_V, 1)
        o1 = pltpu.roll(a1, MLA_V, 1) / a1
        o_ref[0, :, pr * HEAD_SLOT:(pr + 1) * HEAD_SLOT] = jnp.where(lane < MLA_V, o0, o1).astype(o_ref.dtype)


def _attention(q, k, v, seq, with_ctx_queries):
    bsz, t, _ = q.shape
    ctx_len = t - seq
    assert seq % ATT_TK == 0 and seq % ATT_TQ == 0 and seq % ctx_len == 0 and 2 * MLA_V == HEAD_SLOT
    pair = 2 * HEAD_SLOT
    n_chunks = seq // ATT_TK
    chunks = tuple((j * ATT_TK, ATT_TK) for j in range(n_chunks - 1))
    chunks += (((n_chunks - 1) * ATT_TK, ATT_TK + ctx_len),)
    scratch = lambda nh, tq: [pltpu.VMEM((nh, tq, LANES), F32), pltpu.VMEM((nh, tq, HEAD_SLOT), F32)]
    hps = ATT_HEADS
    y = pl.pallas_call(
        functools.partial(_attn_kernel, chunks=chunks),
        out_shape=jax.ShapeDtypeStruct((bsz, seq, MLA_OUT), BF16),
        grid=(bsz, MLA_HEADS // hps, seq // ATT_TQ),
        in_specs=[
            pl.BlockSpec((1, ATT_TQ, hps * HEAD_SLOT), lambda b, hp, i: (b, i, hp)),
            pl.BlockSpec((1, t, hps * HEAD_SLOT), lambda b, hp, i: (b, 0, hp)),
            pl.BlockSpec((1, t, hps * HEAD_SLOT), lambda b, hp, i: (b, 0, hp)),
        ],
        out_specs=pl.BlockSpec((1, ATT_TQ, hps * MLA_V), lambda b, hp, i: (b, i, hp)),
        scratch_shapes=scratch(hps, ATT_TQ),
        compiler_params=_cparams(3, VMEM_LIMIT),
        name="mla_attention",
    )(q, k, v)
    if not with_ctx_queries:
        return y, None
    cblk = seq // ctx_len
    ctx_rows = lambda w: pl.BlockSpec((1, ctx_len, w), lambda b, hp: (b, cblk, hp))
    y_ctx = pl.pallas_call(
        functools.partial(_attn_kernel, chunks=((0, ctx_len),)),
        out_shape=jax.ShapeDtypeStruct((bsz, ctx_len, MLA_OUT), BF16),
        grid=(bsz, MLA_HEADS // 2),
        in_specs=[ctx_rows(pair), ctx_rows(pair), ctx_rows(pair)],
        out_specs=pl.BlockSpec((1, ctx_len, HEAD_SLOT), lambda b, hp: (b, 0, hp)),
        scratch_shapes=scratch(2, ctx_len),
        compiler_params=_cparams(2),
        name="mla_attention_ctx",
    )(q, k, v)
    return y, y_ctx


GLA_LEVELS = int(math.log2(TM))
GLA_SAFE_SPAN = 60.0


def _gla_level_matrices():
    i = np.arange(TM)[:, None]
    t = np.arange(TM)[None, :]
    fwd = [(t <= i)]
    for lv in range(GLA_LEVELS):
        m = TM >> (lv + 1)
        lo = (i // m) * m
        later = ((i // m) % 2) == 1
        q_part = later & (t >= lo) & (t <= i)
        k_part = (~later) & (t > i) & (t <= lo + m - 1)
        fwd.append(q_part | k_part)
    fwd = np.concatenate(fwd, axis=0).astype(np.float32)
    nb = 1 + GLA_LEVELS
    bwd = fwd.reshape(nb, TM, TM)[:, ::-1, ::-1].reshape(nb * TM, TM)
    return np.stack([fwd, bwd])


def _gla_kernel(qf_ref, kf_ref, vf_ref, gf_ref, qb_ref, kb_ref, vb_ref, gb_ref, mall_ref, of_ref, ob_ref, s_ref, a_ref):
    step = pl.program_id(1)

    @pl.when(step == 0)
    def _():
        s_ref[...] = jnp.zeros(s_ref.shape, F32)

    ins = ((qf_ref, kf_ref, vf_ref, gf_ref), (qb_ref, kb_ref, vb_ref, gb_ref))
    outs = (of_ref, ob_ref)
    qs, ks, vs, g2s, g_cums, g_tots = [], [], [], [], [], []
    for d in range(2):
        q_ref, k_ref, v_ref, g_ref = ins[d]
        qs.append(q_ref[0].astype(F32))
        ks.append(k_ref[0].astype(F32))
        vs.append(v_ref[0])
        g = g_ref[0]
        g2 = jnp.concatenate([g.astype(BF16), (g - g.astype(BF16).astype(F32)).astype(BF16)], axis=1)
        e2 = _dot(mall_ref[d, 0:TM, :], g2)
        g2s.append(g2)
        g_cums.append(e2[:, :GLA_DK] + e2[:, GLA_DK:])
        g_tots.append(jnp.sum(g, axis=0, keepdims=True))

    row = lax.broadcasted_iota(jnp.int32, (TM, TM), 0)
    col = lax.broadcasted_iota(jnp.int32, (TM, TM), 1)
    lane_head = lax.broadcasted_iota(jnp.int32, (TM, GLA_DK), 1) // GLA_HK
    tok = lax.broadcasted_iota(jnp.int32, (TM, GLA_DK), 0)
    eye = row == col

    def stack_heads(t):
        return jnp.concatenate([jnp.where(lane_head == hd, t, 0.0) for hd in range(GLA_HEADS)], axis=0).astype(BF16)

    span = jnp.maximum(jnp.max(-g_tots[0]), jnp.max(-g_tots[1]))

    @pl.when(span < GLA_SAFE_SPAN)
    def _():
        for d in range(2):
            res = _dot_nt(stack_heads(qs[d] * jnp.exp(g_cums[d])), (ks[d] * jnp.exp(-g_cums[d])).astype(BF16))
            seen = (col <= row) if d == 0 else (col >= row)
            for hd in range(GLA_HEADS):
                a_ref[d, hd] = jnp.where(seen, res[hd * TM:(hd + 1) * TM], 0.0)

    @pl.when(span >= GLA_SAFE_SPAN)
    def _():
        for d in range(2):
            q, k = qs[d], ks[d]
            e2l = _dot(mall_ref[d, TM:, :], g2s[d])
            e_lv = e2l[:, :GLA_DK] + e2l[:, GLA_DK:]
            res = _dot_nt(stack_heads(q), k.astype(BF16))
            for hd in range(GLA_HEADS):
                a_ref[d, hd] = jnp.where(eye, res[hd * TM:(hd + 1) * TM], 0.0)
            for lv in range(GLA_LEVELS):
                m = TM >> (lv + 1)
                w = jnp.exp(e_lv[lv * TM:(lv + 1) * TM])
                bit = (tok // m) % 2
                q_act = bit != d
                qt = jnp.where(q_act, q * w, 0.0)
                kt = jnp.where(q_act, 0.0, k * w).astype(BF16)
                res = _dot_nt(stack_heads(qt), kt)
                if m == TM // 2:
                    for hd in range(GLA_HEADS):
                        a_ref[d, hd] += res[hd * TM:(hd + 1) * TM]
                else:
                    same = (row // (2 * m)) == (col // (2 * m))
                    for hd in range(GLA_HEADS):
                        a_ref[d, hd] += jnp.where(same, res[hd * TM:(hd + 1) * TM], 0.0)

    same_head = (lax.broadcasted_iota(jnp.int32, (GLA_DK, GLA_DV), 0) // GLA_HK
                 == lax.broadcasted_iota(jnp.int32, (GLA_DK, GLA_DV), 1) // GLA_HV)
    for d in range(2):
        q, k, v, g_cum, g_tot = qs[d], ks[d], vs[d], g_cums[d], g_tots[d]
        s_old = s_ref[d]
        o_inter = _dot((q * jnp.exp(g_cum)).astype(BF16), s_old.astype(BF16))
        for hd in range(GLA_HEADS):
            sl = slice(hd * GLA_HV, (hd + 1) * GLA_HV)
            o_intra = _dot(a_ref[d, hd].astype(BF16), v[:, sl])
            outs[d][0, :, sl] = (o_intra + o_inter[:, sl]).astype(outs[d].dtype)

        kdec_t = (k * jnp.exp(g_tot - g_cum)).T.astype(BF16)
        upd = _dot(kdec_t, v)
        dec_col = jnp.sum(jnp.where(eye, jnp.broadcast_to(jnp.exp(g_tot), (TM, GLA_DK)), 0.0), axis=1,
                          keepdims=True)
        s_ref[d] = dec_col * s_old + jnp.where(same_head, upd, 0.0)


def _gla(gq, gk, gv, glog, n_lat_tiles):
    assert GLA_DK == TM
    bsz, t, _ = gq.shape
    nt = t // TM
    mall = jnp.asarray(_gla_level_matrices(), dtype=BF16)

    def tile_idx(d, s):
        return jnp.where(s == 0, n_lat_tiles, s - 1 if d == 0 else n_lat_tiles - s)

    def specs(d):
        tok = lambda w: pl.BlockSpec((1, TM, w), lambda b, s: (b, tile_idx(d, s), 0))
        return [tok(GLA_DK), tok(GLA_DK), tok(GLA_DV), pl.BlockSpec((1, TM, GLA_DK), lambda b, s: (b, tile_idx(d, s), d))]

    out_spec = lambda d: pl.BlockSpec((1, TM, GLA_DV), lambda b, s: (b, tile_idx(d, s), 0))
    sds = jax.ShapeDtypeStruct((bsz, t, GLA_DV), BF16)
    return pl.pallas_call(
        _gla_kernel,
        out_shape=(sds, sds),
        grid=(bsz, nt),
        in_specs=specs(0) + specs(1) + [_const_spec(mall.shape)],
        out_specs=(out_spec(0), out_spec(1)),
        scratch_shapes=[pltpu.VMEM((2, GLA_DK, GLA_DV), F32), pltpu.VMEM((2, GLA_HEADS, TM, TM), F32)],
        compiler_params=_cparams(2),
        name="gla_scan",
    )(gq, gk, gv, glog, gq, gk, gv, glog, mall)


HY_N1 = 64
HY_N2 = 128
HY_SLAB = SUBLANES
HY_CB = 256
HY_KG = 16
HY_SPEC_CB = 128
HY_UNROLL = 16


def _hyena_dft_constants(seq):
    n = 2 * seq
    assert n == HY_N1 * HY_N2
    nh = HY_N1 // 2
    kh = HY_N2 // 2
    eye = np.eye(HY_SLAB)
    k1 = np.arange(HY_N1)
    th = 2 * np.pi * np.outer(k1 + 0.5, np.arange(nh)) / HY_N1
    fwd1 = np.concatenate([np.kron(np.cos(th), eye), np.kron(-np.sin(th), eye)], axis=0)
    inv1 = (2.0 / n) * np.concatenate([np.kron(np.cos(th).T, eye), np.kron(-np.sin(th).T, eye)], axis=1)
    nlo = np.arange(HY_N2)
    k2 = np.arange(kh)
    ph = 2 * np.pi * (k2[None, :, None] * nlo[None, None, :] / HY_N2
                      + (k1[:, None, None] + 0.5) * nlo[None, None, :] / n)
    c, s = np.cos(ph), np.sin(ph)
    fwd2 = np.concatenate([np.concatenate([c, s], axis=2), np.concatenate([-s, c], axis=2)], axis=1)
    ct, st = c.transpose(0, 2, 1), s.transpose(0, 2, 1)
    inv2 = np.concatenate([np.concatenate([ct, -st], axis=2), np.concatenate([st, ct], axis=2)], axis=1)
    return fwd1, fwd2, inv2, inv1


def _hyena_ctx_dft_constants(ctx_len):
    n = 2 * ctx_len
    th = 2 * np.pi * np.outer(np.arange(ctx_len) + 0.5, np.arange(ctx_len)) / n
    fwd = np.concatenate([np.cos(th), -np.sin(th)], axis=0)
    inv = (2.0 / n) * np.concatenate([np.cos(th).T, -np.sin(th).T], axis=1)
    return fwd, inv


def _hyena_features(length):
    pos = np.arange(length, dtype=np.float64)
    t = pos / max(length - 1, 1)
    f = np.linspace(1e-4, HY_BANDS - 1, HY_BANDS)
    ang = (2.0 * math.pi / length) * pos[:, None] * f
    feat = np.concatenate([t[:, None], np.cos(ang), np.sin(ang)], axis=-1)
    return jnp.asarray(np.pad(feat, ((0, 0), (0, LANES - HY_POS_DIM))), F32)


def _hy_filter_kernel(feat_ref, w1_ref, b1_ref, w2_ref, b2_ref, w3_ref, b3_ref, absd_ref, h_ref, s_ref):
    i = pl.program_id(0)
    feat = feat_ref[...]
    hp = _dot3
    hdn = jnp.sin(hp(feat, w1_ref[...]) + b1_ref[...])
    hdn = jnp.sin(hp(hdn, w2_ref[...]) + b2_ref[...])
    h = hp(hdn, w3_ref[...]) + b3_ref[...]
    window = jnp.exp(-feat[:, 0:1] * absd_ref[...])
    h = h * jnp.concatenate([window] * (2 * HY_ORDER), axis=1)
    h_ref[...] = h

    @pl.when(i == 0)
    def _():
        s_ref[...] = jnp.zeros(s_ref.shape, F32)

    s_ref[...] += jnp.sum(jnp.abs(h), axis=0, keepdims=True)


def _hyena_filters_raw(length, filt_w):
    w1, b1, w2, b2, w3, b3 = filt_w
    nf = 2 * HY_ORDER * HY_WIDTH
    tr = min(length, 512)
    deltas = np.linspace(math.log(HY_DECAY_TARGET) / HY_FAST_DECAY, math.log(HY_DECAY_TARGET) / HY_SLOW_DECAY,
                         HY_WIDTH, dtype=np.float32)
    absd = jnp.asarray(np.abs(deltas))[None, :]
    w1p = jnp.pad(w1, ((0, LANES - HY_POS_DIM), (0, 0)))
    full = lambda shp: pl.BlockSpec(shp, lambda i: (0,) * len(shp))
    return pl.pallas_call(
        _hy_filter_kernel,
        out_shape=(jax.ShapeDtypeStruct((length, nf), F32), jax.ShapeDtypeStruct((1, nf), F32)),
        grid=(length // tr,),
        in_specs=[pl.BlockSpec((tr, LANES), lambda i: (i, 0)), full((LANES, HY_FILTER_HIDDEN)),
                  full((1, HY_FILTER_HIDDEN)), full((HY_FILTER_HIDDEN, HY_FILTER_HIDDEN)), full((1, HY_FILTER_HIDDEN)),
                  full((HY_FILTER_HIDDEN, nf)), full((1, nf)), full((1, HY_WIDTH))],
        out_specs=(pl.BlockSpec((tr, nf), lambda i: (i, 0)), full((1, nf))),
        compiler_params=_cparams(1),
        name="hyena_filter_mlp",
    )(_hyena_features(length), w1p, b1[None, :], w2, b2[None, :], w3, b3[None, :], absd)


def _dot_split(m, x):
    x_hi = x.astype(BF16)
    x_lo = (x - x_hi.astype(F32)).astype(BF16)
    n = x.shape[1]
    r = _dot(m, jnp.concatenate([x_hi, x_lo], axis=1))
    return r[:, :n] + r[:, n:]


def _odft_stage1(src_at, mm, s_re, s_im, unroll=2):
    nk = s_re.shape[0]
    half = nk * HY_SLAB

    def body(j, carry):
        r0 = pl.multiple_of(j * HY_SLAB, HY_SLAB)
        slab = src_at(r0)
        cb = slab.shape[-1]
        res = mm(slab.reshape(-1, cb))
        s_re[:, pl.ds(r0, HY_SLAB), :] = res[:half].reshape(nk, HY_SLAB, cb)
        s_im[:, pl.ds(r0, HY_SLAB), :] = res[half:].reshape(nk, HY_SLAB, cb)
        return carry

    lax.fori_loop(0, HY_N2 // HY_SLAB, body, 0, unroll=unroll)


def _hy_spectrum_kernel(hf_ref, hb_ref, sf_ref, sb_ref, fwd1_ref, fwd2_ref, o_ref, s_re, s_im):
    kh = HY_N2 // 2
    mm1 = lambda x: _dot_split(fwd1_ref[...], x)

    def middle(sign):
        def body(k1, carry):
            a = jnp.concatenate([s_re[k1], s_im[k1]], axis=0)
            x = _dot_split(fwd2_ref[k1], a)
            if sign is None:
                o_ref[0, k1] = x[:kh]
                o_ref[1, k1] = x[kh:]
            else:
                inv_norm = 1.0 / (sf_ref[...] + sb_ref[...])
                o_ref[0, k1] = (o_ref[0, k1] + x[:kh]) * inv_norm
                o_ref[1, k1] = (o_ref[1, k1] - x[kh:]) * inv_norm
            return carry
        lax.fori_loop(0, HY_N1, body, 0, unroll=32)

    _odft_stage1(lambda r0: hf_ref[:, pl.ds(r0, HY_SLAB), :], mm1, s_re, s_im, unroll=16)
    middle(None)

    def bwd_slab(r0):
        slab = hb_ref[:, pl.ds(r0, HY_SLAB), :]
        nhi = lax.broadcasted_iota(jnp.int32, slab.shape, 0)
        r = lax.broadcasted_iota(jnp.int32, slab.shape, 1)
        return jnp.where((nhi == 0) & (r + r0 == 0), 0.0, slab)

    _odft_stage1(bwd_slab, mm1, s_re, s_im, unroll=16)
    middle(-1)


def _hyena_filter_spectrum(seq, w1, b1, w2, b2, w3, b3):
    h_raw, s = _hyena_filters_raw(seq, (w1, b1, w2, b2, w3, b3))
    nh = HY_N1 // 2
    nc = HY_ORDER * HY_WIDTH
    h3 = h_raw.reshape(nh, HY_N2, 2 * nc)
    fwd1, fwd2, _, _ = _hyena_dft_constants(seq)
    fwd1 = jnp.asarray(fwd1, BF16)
    fwd2 = jnp.asarray(fwd2, BF16)
    scb = HY_SPEC_CB
    ncb = nc // scb
    return pl.pallas_call(
        _hy_spectrum_kernel,
        out_shape=jax.ShapeDtypeStruct((2, HY_N1, HY_N2 // 2, nc), F32),
        grid=(ncb,),
        in_specs=[pl.BlockSpec((nh, HY_N2, scb), lambda c: (0, 0, c)),
                  pl.BlockSpec((nh, HY_N2, scb), lambda c: (0, 0, ncb + c)),
                  pl.BlockSpec((1, scb), lambda c: (0, c)),
                  pl.BlockSpec((1, scb), lambda c: (0, ncb + c)),
                  _const_spec(fwd1.shape), _const_spec(fwd2.shape)],
        out_specs=pl.BlockSpec((2, HY_N1, HY_N2 // 2, scb), lambda c: (0, 0, 0, c)),
        scratch_shapes=[pltpu.VMEM((HY_N1, HY_N2, scb), F32), pltpu.VMEM((HY_N1, HY_N2, scb), F32)],
        compiler_params=_cparams(1, VMEM_LIMIT),
        name="hyena_filter_spectrum",
    )(h3, h3, s, s, fwd1, fwd2)


def _short_conv_chunk(ref, c, n_chunks, w_ref, b_ref):
    per = TM // HY_N2
    cur = ref[0, pl.ds(per * c, per)]
    cb = cur.shape[-1]
    cur = cur.reshape(TM, cb)
    prev = ref[0, jnp.maximum(per * c - 1, 0), pl.ds(HY_N2 - SUBLANES, SUBLANES), :][SUBLANES - 1:SUBLANES]
    nxt = ref[0, jnp.minimum(per * c + per, per * n_chunks - 1), pl.ds(0, SUBLANES), :][0:1]
    prev = jnp.where(c > 0, prev, 0.0)
    nxt = jnp.where(c < n_chunks - 1, nxt, 0.0)
    rowi = lax.broadcasted_iota(jnp.int32, (TM, cb), 0)
    dn = jnp.where(rowi == 0, prev, pltpu.roll(cur, 1, 0))
    up = jnp.where(rowi == TM - 1, nxt, pltpu.roll(cur, TM - 1, 0))
    return b_ref[...] + w_ref[0:1, :] * dn + w_ref[1:2, :] * cur + w_ref[2:3, :] * up


def _hy_conv_kernel(*refs, conv_y):
    if conv_y:
        (y_ref, g_ref, h_ref, bias_ref, wy_ref, by_ref, wg_ref, bg_ref, fwd1_ref, fwd2_ref, inv2_ref, inv1_ref,
         o_ref, s_re, s_im, gs_ref, us_ref) = refs
    else:
        (y_ref, g_ref, h_ref, bias_ref, wg_ref, bg_ref, fwd1_ref, fwd2_ref, inv2_ref, inv1_ref,
         o_ref, s_re, s_im, gs_ref) = refs
    grp = pl.program_id(2)
    last = pl.num_programs(2) - 1
    nh = HY_N1 // 2
    per = TM // HY_N2
    n_chunks = nh // per
    cb = o_ref.shape[-1]
    kh = HY_N2 // 2

    @pl.when(grp == 0)
    def _():
        def pre(c, carry):
            gs_ref[pl.ds(per * c, per)] = _short_conv_chunk(g_ref, c, n_chunks, wg_ref, bg_ref).reshape(per, HY_N2, cb)
            if conv_y:
                us_ref[pl.ds(per * c, per)] = _short_conv_chunk(y_ref, c, n_chunks, wy_ref, by_ref).reshape(
                    per, HY_N2, cb)
            return carry
        lax.fori_loop(0, n_chunks, pre, 0)

    if conv_y:
        u_at = lambda r0: us_ref[:, pl.ds(r0, HY_SLAB), :]
    else:
        u_at = lambda r0: y_ref[0, :, pl.ds(r0, HY_SLAB), :]

    _odft_stage1(u_at, lambda x: _dot(fwd1_ref[0], x.astype(BF16)), s_re, s_im, unroll=HY_UNROLL)

    def middle(k1, carry):
        a = jnp.concatenate([s_re[k1], s_im[k1]], axis=0).astype(BF16)
        x = _dot(fwd2_ref[k1], a)
        xr, xi = x[:kh], x[kh:]
        hr, hi = h_ref[0, k1], h_ref[1, k1]
        y = jnp.concatenate([xr * hr - xi * hi, xr * hi + xi * hr], axis=0).astype(BF16)
        bm = _dot(inv2_ref[k1], y)
        s_re[k1] = bm[:HY_N2]
        s_im[k1] = bm[HY_N2:]
        return carry

    lax.fori_loop(0, HY_KG, middle, 0, unroll=HY_KG)

    def partial_conv(r0):
        slab = jnp.concatenate([s_re[:, pl.ds(r0, HY_SLAB), :].reshape(HY_KG * HY_SLAB, cb),
                                s_im[:, pl.ds(r0, HY_SLAB), :].reshape(HY_KG * HY_SLAB, cb)], axis=0).astype(BF16)
        return _dot(inv1_ref[0], slab).reshape(nh, HY_SLAB, cb)

    def post_loop(fn):
        def post(j, carry):
            r0 = pl.multiple_of(j * HY_SLAB, HY_SLAB)
            o_ref[0, :, pl.ds(r0, HY_SLAB), :] = fn(r0, partial_conv(r0))
            return carry
        lax.fori_loop(0, HY_N2 // HY_SLAB, post, 0, unroll=HY_UNROLL)

    @pl.when(grp == 0)
    def _():
        post_loop(lambda r0, part: part)

    @pl.when((grp > 0) & (grp < last))
    def _():
        post_loop(lambda r0, part: o_ref[0, :, pl.ds(r0, HY_SLAB), :] + part)

    @pl.when(grp == last)
    def _():
        post_loop(lambda r0, part: gs_ref[:, pl.ds(r0, HY_SLAB), :]
                  * (o_ref[0, :, pl.ds(r0, HY_SLAB), :] + part + bias_ref[...] * u_at(r0)))


def _hyena_order(y4, y_col0, z4, gate_col0, hspec, order, hy_bias, short_w, short_b, consts, conv_y):
    bsz = z4.shape[0]
    nh = HY_N1 // 2
    ncb = HY_WIDTH // HY_CB
    ngrp = HY_N1 // HY_KG
    fwd1, fwd2, inv2, inv1 = consts
    blk4 = lambda off: pl.BlockSpec((1, nh, HY_N2, HY_CB), lambda c, b, g: (b, 0, 0, off + c))
    rowspec = lambda rows, off: pl.BlockSpec((rows, HY_CB), lambda c, b, g: (0, off + c))
    grouped = lambda shp: pl.BlockSpec(shp, lambda c, b, g: (g, 0, 0))
    in_specs = [blk4(y_col0), blk4(gate_col0),
                pl.BlockSpec((2, HY_KG, HY_N2 // 2, HY_CB), lambda c, b, g: (0, g, 0, order * ncb + c)),
                rowspec(1, 0)]
    args = [y4, z4, hspec, hy_bias.reshape(1, -1)]
    if conv_y:
        in_specs += [rowspec(HY_SHORT, y_col0), rowspec(1, y_col0)]
        args += [short_w, short_b[None, :]]
    in_specs += [rowspec(HY_SHORT, gate_col0), rowspec(1, gate_col0)]
    args += [short_w, short_b[None, :]]
    in_specs += [grouped((1,) + fwd1.shape[1:]), grouped((HY_KG,) + fwd2.shape[1:]),
                 grouped((HY_KG,) + inv2.shape[1:]), grouped((1,) + inv1.shape[1:])]
    args += [fwd1, fwd2, inv2, inv1]
    scratch = [pltpu.VMEM((HY_KG, HY_N2, HY_CB), F32), pltpu.VMEM((HY_KG, HY_N2, HY_CB), F32),
               pltpu.VMEM((nh, HY_N2, HY_CB), F32)]
    if conv_y:
        scratch.append(pltpu.VMEM((nh, HY_N2, HY_CB), F32))
    return pl.pallas_call(
        functools.partial(_hy_conv_kernel, conv_y=conv_y),
        out_shape=jax.ShapeDtypeStruct((bsz, nh, HY_N2, HY_WIDTH), F32),
        grid=(ncb, bsz, ngrp),
        in_specs=in_specs,
        out_specs=pl.BlockSpec((1, nh, HY_N2, HY_CB), lambda c, b, g: (b, 0, 0, c)),
        scratch_shapes=scratch,
        compiler_params=_cparams(3, VMEM_LIMIT),
        name="hyena_conv%d" % order,
    )(*args)


def _hyena_latent(z_hy, hspec, short_w, short_b, hy_bias):
    bsz, t, _ = z_hy.shape
    seq = HY_N1 * HY_N2 // 2
    fwd1, fwd2, inv2, inv1 = _hyena_dft_constants(seq)
    ngrp = HY_N1 // HY_KG
    rows = HY_KG * HY_SLAB
    fwd1 = fwd1.reshape(2, ngrp, rows, -1).transpose(1, 0, 2, 3).reshape(ngrp, 2 * rows, -1)
    inv1 = inv1.reshape(-1, 2, ngrp, rows).transpose(2, 0, 1, 3).reshape(ngrp, -1, 2 * rows)
    consts = tuple(jnp.asarray(m, BF16) for m in (fwd1, fwd2, inv2, inv1))
    z4 = z_hy.reshape(bsz, t // HY_N2, HY_N2, 3 * HY_WIDTH)
    ncb = HY_WIDTH // HY_CB
    y1 = _hyena_order(z4, 2 * ncb, z4, 0, hspec, 0, hy_bias[0], short_w, short_b, consts, True)
    y2 = _hyena_order(y1, 0, z4, ncb, hspec, 1, hy_bias[1], short_w, short_b, consts, False)
    return y2.reshape(bsz, seq, HY_WIDTH)


def _hy_ctx_spectrum_kernel(h_ref, s_ref, fwd_ref, o_ref):
    lc = h_ref.shape[0]
    nc = HY_ORDER * HY_WIDTH
    hp = _dot3
    h = h_ref[...]
    rowi = lax.broadcasted_iota(jnp.int32, (lc, nc), 0)
    xf = hp(fwd_ref[...], h[:, :nc])
    xb = hp(fwd_ref[...], jnp.where(rowi == 0, 0.0, h[:, nc:]))
    inv_norm = 1.0 / (s_ref[:, :nc] + s_ref[:, nc:])
    o_ref[0] = (xf[:lc] + xb[:lc]) * inv_norm
    o_ref[1] = (xf[lc:] - xb[lc:]) * inv_norm


def _hy_ctx_conv_kernel(x1_ref, x2_ref, v_ref, h_ref, bias_ref, w_ref, b_ref, fwd_ref, inv_ref, o_ref):
    lc = o_ref.shape[1]
    rowi = lax.broadcasted_iota(jnp.int32, (lc, HY_WIDTH), 0)

    def short(ref, part):
        cur = ref[0].reshape(lc, HY_WIDTH)
        sl = slice(part * HY_WIDTH, (part + 1) * HY_WIDTH)
        dn = jnp.where(rowi == 0, 0.0, pltpu.roll(cur, 1, 0))
        up = jnp.where(rowi == lc - 1, 0.0, pltpu.roll(cur, lc - 1, 0))
        return b_ref[:, sl] + w_ref[0:1, sl] * dn + w_ref[1:2, sl] * cur + w_ref[2:3, sl] * up

    y = short(v_ref, 2)
    for order, gref in enumerate((x1_ref, x2_ref)):
        sl = slice(order * HY_WIDTH, (order + 1) * HY_WIDTH)
        x = _dot(fwd_ref[...], y.astype(BF16))
        xr, xi = x[:lc], x[lc:]
        hr, hi = h_ref[0, :, sl], h_ref[1, :, sl]
        prod = jnp.concatenate([xr * hr - xi * hi, xr * hi + xi * hr], axis=0).astype(BF16)
        conv = _dot(inv_ref[...], prod)
        y = short(gref, order) * (conv + bias_ref[order:order + 1, :] * y)
    o_ref[0] = y


def _hyena_ctx(z_hy, filt_w, short_w, short_b, hy_bias):
    bsz, t, _ = z_hy.shape
    seq = HY_N1 * HY_N2 // 2
    lc = t - seq
    per = lc // HY_N2
    h_raw, s = _hyena_filters_raw(lc, filt_w)
    fwd, inv = _hyena_ctx_dft_constants(lc)
    nc = HY_ORDER * HY_WIDTH
    full = lambda shp: pl.BlockSpec(shp, lambda *_: (0,) * len(shp))
    hspec = pl.pallas_call(
        _hy_ctx_spectrum_kernel,
        out_shape=jax.ShapeDtypeStruct((2, lc, nc), F32),
        grid=(1,),
        in_specs=[full(h_raw.shape), full(s.shape), full(fwd.shape)],
        out_specs=full((2, lc, nc)),
        compiler_params=_cparams(1),
        name="hyena_ctx_spectrum",
    )(h_raw, s, jnp.asarray(fwd, F32))
    z4 = z_hy.reshape(bsz, t // HY_N2, HY_N2, 3 * HY_WIDTH)
    blk = lambda part: pl.BlockSpec((1, per, HY_N2, HY_WIDTH), lambda b: (b, seq // lc, 0, part))
    return pl.pallas_call(
        _hy_ctx_conv_kernel,
        out_shape=jax.ShapeDtypeStruct((bsz, lc, HY_WIDTH), F32),
        grid=(bsz,),
        in_specs=[blk(0), blk(1), blk(2), full((2, lc, nc)), full((HY_ORDER, HY_WIDTH)),
                  full((HY_SHORT, 3 * HY_WIDTH)), full((1, 3 * HY_WIDTH)), full(fwd.shape), full(inv.shape)],
        out_specs=pl.BlockSpec((1, lc, HY_WIDTH), lambda b: (b, 0, 0)),
        compiler_params=_cparams(1),
        name="hyena_ctx_conv",
    )(z4, z4, z4, hspec, hy_bias, short_w, short_b[None, :], jnp.asarray(fwd, BF16), jnp.asarray(inv, BF16))


FF_CHUNK = 1024


def _merge_mlp_kernel(x_ref, c_ref, mod_ref, ymla_ref, ymlac_ref, of_ref, ob_ref, gr_ref, yhy_ref, yhyc_ref, zg_ref, on_ref,
                      wm_ref, wgl_ref, wh_ref, wo_ref, g2_ref, w1_ref, w2_ref, fg_ref, o_ref, *, final, n_lat_tiles):
    o = of_ref[0].astype(F32) + ob_ref[0].astype(F32)
    silu = gr_ref[0].astype(F32)
    parts = []
    for hd in range(GLA_HEADS):
        sl = slice(hd * GLA_HV, (hd + 1) * GLA_HV)
        parts.append((_rms(o[:, sl]) * on_ref[...] * silu[:, sl]).astype(BF16))
    y_gla = jnp.concatenate(parts, axis=1)
    zg = zg_ref[0].astype(F32)
    d = x_ref.shape[-1]
    m = zg[:, 0:d] * _dot(_token_tile(ymla_ref, ymlac_ref, n_lat_tiles), wm_ref[...])
    m = m + zg[:, d:2 * d] * _dot(y_gla, wgl_ref[...])
    m = m + zg[:, 2 * d:3 * d] * _dot(_token_tile(yhy_ref, yhyc_ref, n_lat_tiles).astype(BF16), wh_ref[...])
    out = _dot(m.astype(BF16), wo_ref[...])
    x = _token_tile(x_ref, c_ref, n_lat_tiles) + mod_ref[0, 2:3, :] * out

    h = (_rms(x) * (g2_ref[...] * (1.0 + mod_ref[0, 4:5, :])) + mod_ref[0, 3:4, :]).astype(BF16)
    acc = jnp.zeros(x.shape, F32)
    for j in range(w1_ref.shape[1] // FF_CHUNK):
        a = jnp.maximum(_dot(h, w1_ref[:, j * FF_CHUNK:(j + 1) * FF_CHUNK]), 0.0)
        acc = acc + _dot((a * a).astype(BF16), w2_ref[j * FF_CHUNK:(j + 1) * FF_CHUNK, :])
    xn = x + mod_ref[0, 5:6, :] * acc
    if final:
        xn = _rms(xn) * fg_ref[...]
    o_ref[0] = xn


def _mod_spec(d, n_lat_tiles, bsz):
    return pl.BlockSpec((1, 6, d), lambda b, i: (jnp.where(i < n_lat_tiles, b, bsz), 0, 0))


def _merge_mlp(x_lat, x_ctx, ctx_blk, mod_l, y_mla, y_mla_ctx, o_gla, gr, y_hy, y_hy_ctx, gate, out_norm,
               w_o_mla, w_o_gla, w_o_hy, w_out, g2, w1, w2, final_g, n_tiles, n_lat_tiles, final):
    bsz, _, d = x_lat.shape
    if y_mla_ctx is None:
        y_mla_ctx, y_hy_ctx = y_mla, y_hy
    tile = lambda w: pl.BlockSpec((1, TM, w), lambda b, i: (b, i, 0))
    row = lambda w: pl.BlockSpec((1, w), lambda b, i: (0, 0))
    bf = lambda w: w.astype(BF16)
    return pl.pallas_call(
        functools.partial(_merge_mlp_kernel, final=final, n_lat_tiles=n_lat_tiles if n_tiles > n_lat_tiles else None),
        out_shape=jax.ShapeDtypeStruct((bsz, n_tiles * TM, d), F32),
        grid=(bsz, n_tiles),
        in_specs=_token_specs(d, n_lat_tiles, ctx_blk) + [_mod_spec(d, n_lat_tiles, bsz)]
        + _token_specs(MLA_OUT, n_lat_tiles, 0) + [tile(GLA_DV), tile(GLA_DV), tile(GLA_DV)]
        + _token_specs(HY_WIDTH, n_lat_tiles, 0) + [
            tile(3 * d), row(GLA_HV),
            _const_spec(w_o_mla.shape), _const_spec(w_o_gla.shape), _const_spec(w_o_hy.shape),
            _const_spec(w_out.shape), row(d), _const_spec(w1.shape), _const_spec(w2.shape), row(d)],
        out_specs=tile(d),
        compiler_params=_cparams(2, VMEM_LIMIT),
        name="merge_mlp",
    )(x_lat, x_ctx, mod_l, y_mla, y_mla_ctx, o_gla[0], o_gla[1], gr, y_hy, y_hy_ctx, gate, out_norm[None, :],
      bf(w_o_mla), bf(w_o_gla), bf(w_o_hy), bf(w_out), g2[None, :], bf(w1), bf(w2), final_g[None, :])


def kernel(x, c, ctx, c_ctx, ada_w, ada_b, norm1_g, norm2_g, w_in, mla_q_norm, mla_w_uq, mla_kv_norm, mla_w_ukv, gla_w_a2, gla_b_a, gla_out_norm, hy_short_w, hy_short_b, hy_f_w1, hy_f_b1, hy_f_w2, hy_f_b2, hy_f_w3, hy_f_b3, hy_bias, w_o_mla, w_o_gla, w_o_hy, w_out, ff_w1, ff_w2, final_norm_g):
    bsz, seq, d = x.shape
    ctx_len = ctx.shape[1]
    n_lat = seq // TM
    assert ctx_len == TM and seq % TM == 0
    n_all = n_lat + 1
    x_lat, x_ctx, ctx_blk = x, ctx, 0
    cc = jnp.zeros((16, d), F32).at[:bsz].set(c).at[bsz].set(c_ctx)
    mod = _modulation(cc, ada_w, ada_b).reshape(DEPTH, 16, 6, d)
    cos, sin = _rope_tables(seq, ctx_len)
    for l in range(DEPTH):
        last = l == DEPTH - 1
        n_tiles = n_lat if last else n_all
        weights = _prep_inproj_weights(w_in[l], mla_w_uq[l], mla_w_ukv[l], gla_w_a2[l], gla_b_a[l])
        q, k, v, gq, gk, gv, gr, glog, z_hy, z_gate = _inproj(x_lat, x_ctx, ctx_blk, mod[l], norm1_g[l], weights,
                                                             mla_q_norm[l], mla_kv_norm[l], cos, sin, n_lat)
        y_mla, y_mla_c = _attention(q, k, v, seq, not last)
        o_gla = _gla(gq, gk, gv, glog, n_lat)
        filt_w = (hy_f_w1[l], hy_f_b1[l], hy_f_w2[l], hy_f_b2[l], hy_f_w3[l], hy_f_b3[l])
        hspec = _hyena_filter_spectrum(seq, *filt_w)
        y_hy = _hyena_latent(z_hy, hspec, hy_short_w[l], hy_short_b[l], hy_bias[l])
        y_hy_c = None if last else _hyena_ctx(z_hy, filt_w, hy_short_w[l], hy_short_b[l], hy_bias[l])
        xc = _merge_mlp(x_lat, x_ctx, ctx_blk, mod[l], y_mla, y_mla_c, o_gla, gr, y_hy, y_hy_c, z_gate,
                        gla_out_norm[l], w_o_mla[l], w_o_gla[l], w_o_hy[l], w_out[l], norm2_g[l], ff_w1[l], ff_w2[l],
                        final_norm_g, n_tiles, n_lat, last)
        x_lat, x_ctx, ctx_blk = xc, xc, n_lat
    return xc
```

```python
import functools
import math

import numpy as np
import jax
import jax.numpy as jnp
from jax import lax
from jax.experimental import pallas as pl
from jax.experimental.pallas import tpu as pltpu

F32 = jnp.float32
BF16 = jnp.bfloat16
LOG2E = 1.4426950408889634

D_MODEL = 1024
DEPTH = 2
GRID_W = 64
EPS = 1e-6
MLA_HEADS = 8
MLA_NOPE = 64
MLA_ROPE = 32
MLA_V = 64
MLA_Q_LORA = 256
MLA_KV_LORA = 128
MLA_SCALE = (MLA_NOPE + MLA_ROPE) ** -0.5
ROPE_BASE = 10000.0
GLA_HEADS = 4
GLA_DK = 256
GLA_DV = 512
GLA_HK = GLA_DK // GLA_HEADS
GLA_HV = GLA_DV // GLA_HEADS
GLA_GATE_RANK = 16
GLA_TAU = 16.0
HY_WIDTH = 512
HY_ORDER = 2
HY_SHORT = 3
HY_BANDS = 16
HY_POS_DIM = 1 + 2 * HY_BANDS
HY_FILTER_HIDDEN = 64
HY_FAST_DECAY = 0.3
HY_SLOW_DECAY = 1.5
HY_DECAY_TARGET = 1e-2
D_FF = 4 * D_MODEL
MLA_OUT = MLA_HEADS * MLA_V
IN_SIZES = (MLA_Q_LORA, MLA_KV_LORA, MLA_ROPE, GLA_DK, GLA_DK, GLA_DV, GLA_DV, GLA_GATE_RANK, GLA_GATE_RANK,
            (HY_ORDER + 1) * HY_WIDTH, 3 * D_MODEL)

LANES = 128
SUBLANES = 8
TM = 256
HEAD_SLOT = 128
V7X_VMEM_BYTES = 64 * 1024 * 1024
VMEM_LIMIT = V7X_VMEM_BYTES * 7 // 8


def _cparams(n_axes, vmem=None):
    return pltpu.CompilerParams(dimension_semantics=("arbitrary",) * n_axes, vmem_limit_bytes=vmem)


def _const_spec(shape):
    nd = len(shape)
    return pl.BlockSpec(shape, lambda *_: (0,) * nd, pipeline_mode=pl.Buffered(1))


def _rms(x):
    return x * lax.rsqrt(jnp.mean(x * x, axis=-1, keepdims=True) + EPS)


def _sigmoid(x):
    return 1.0 / (1.0 + jnp.exp(-x))


def _dot(a, b):
    return jnp.dot(a, b, preferred_element_type=F32)


def _dot_nt(a, b):
    return lax.dot_general(a, b, (((1,), (1,)), ((), ())), preferred_element_type=F32)


def _dot3(a, b):
    a_hi = a.astype(BF16)
    a_lo = (a - a_hi.astype(F32)).astype(BF16)
    b_hi = b.astype(BF16)
    b_lo = (b - b_hi.astype(F32)).astype(BF16)
    m = a.shape[0]
    r = _dot(jnp.concatenate([a_hi, a_lo], axis=0), b_hi)
    return r[:m] + (r[m:] + _dot(a_hi, b_lo))


def _mod_kernel(cc_ref, w_ref, b_ref, o_ref):
    s = cc_ref[...]
    s = s * _sigmoid(s)
    o_ref[0] = _dot3(s, w_ref[0]) + b_ref[0]


def _modulation(cc, ada_w, ada_b):
    tn = 1536
    n6 = ada_w.shape[-1]
    return pl.pallas_call(
        _mod_kernel,
        out_shape=jax.ShapeDtypeStruct((DEPTH, 16, n6), F32),
        grid=(DEPTH, n6 // tn),
        in_specs=[
            pl.BlockSpec((16, D_MODEL), lambda l, j: (0, 0)),
            pl.BlockSpec((1, D_MODEL, tn), lambda l, j: (l, 0, j)),
            pl.BlockSpec((1, 1, tn), lambda l, j: (l, 0, j)),
        ],
        out_specs=pl.BlockSpec((1, 16, tn), lambda l, j: (l, 0, j)),
        compiler_params=_cparams(2),
        name="modulation",
    )(cc, ada_w, ada_b.reshape(DEPTH, 1, n6))


W_A = 768
W_G = 2 * GLA_DK + 2 * GLA_DV


def _token_specs(d, n_lat_tiles, ctx_blk):
    return [pl.BlockSpec((1, TM, d), lambda b, i: (b, jnp.minimum(i, n_lat_tiles - 1), 0)),
            pl.BlockSpec((1, TM, d), lambda b, i: (b, ctx_blk, 0))]


def _token_tile(x_ref, c_ref, n_lat_tiles):
    if n_lat_tiles is None:
        return x_ref[0]
    return jnp.where(pl.program_id(1) < n_lat_tiles, x_ref[0], c_ref[0])


def _inproj_kernel(x_ref, c_ref, mod_ref, g1_ref, wa_ref, wg_ref, wh_ref, wz_ref, qn_ref, kvn_ref, wuq_ref, wukv_ref,
                   wa2_ref, ba_ref, cos_ref, sin_ref,
                   q_out, k_out, v_out, gq_out, gk_out, gv_out, gr_out, glog_out, hy_out, gate_out, *, n_lat_tiles):
    x = _token_tile(x_ref, c_ref, n_lat_tiles)
    shift = mod_ref[0, 0:1, :]
    scale = mod_ref[0, 1:2, :]
    h = (_rms(x) * (g1_ref[...] * (1.0 + scale)) + shift).astype(BF16)

    za = _dot(h, wa_ref[...])
    cos = cos_ref[...]
    sin = sin_ref[...]

    cqn = (_rms(za[:, 0:256]) * qn_ref[...]).astype(BF16)
    qab = _dot(cqn, wuq_ref[...])
    nq = MLA_HEADS * HEAD_SLOT
    for hd in range(MLA_HEADS):
        sl = slice(hd * HEAD_SLOT, (hd + 1) * HEAD_SLOT)
        qa = qab[:, hd * HEAD_SLOT:(hd + 1) * HEAD_SLOT]
        qb = qab[:, nq + hd * HEAD_SLOT:nq + (hd + 1) * HEAD_SLOT]
        q_out[0, :, sl] = ((qa * cos + qb * sin) * (MLA_SCALE * LOG2E)).astype(BF16)

    ckvn = (_rms(za[:, 256:384]) * kvn_ref[...]).astype(BF16)
    kv = _dot(ckvn, wukv_ref[...])
    krot = za[:, 384:512] * cos + za[:, 512:640] * sin
    for hd in range(MLA_HEADS):
        sl = slice(hd * HEAD_SLOT, (hd + 1) * HEAD_SLOT)
        k_out[0, :, sl] = (kv[:, sl] + krot).astype(BF16)
    ones_hi = (lax.broadcasted_iota(jnp.int32, (1, HEAD_SLOT), 1) >= MLA_V).astype(F32)
    for hd in range(MLA_HEADS):
        sl = slice(hd * HEAD_SLOT, (hd + 1) * HEAD_SLOT)
        v_out[0, :, sl] = (kv[:, nq + hd * HEAD_SLOT:nq + (hd + 1) * HEAD_SLOT] + ones_hi).astype(BF16)

    xg = _dot(za[:, 640:768].astype(BF16), wa2_ref[...]) + ba_ref[...]
    glog_out[0] = (jnp.minimum(xg, 0.0) - jnp.log(1.0 + jnp.exp(-jnp.abs(xg)))) * (1.0 / GLA_TAU)

    zg = _dot(h, wg_ref[...])
    gq_out[0] = (zg[:, 0:GLA_DK] * (GLA_HK ** -0.5)).astype(BF16)
    gk_out[0] = zg[:, GLA_DK:2 * GLA_DK].astype(BF16)
    gv_out[0] = zg[:, 2 * GLA_DK:2 * GLA_DK + GLA_DV].astype(BF16)
    gr = zg[:, 2 * GLA_DK + GLA_DV:]
    gr_out[0] = (gr * _sigmoid(gr)).astype(BF16)

    hy_out[0] = _dot(h, wh_ref[...])
    gate_out[0] = _sigmoid(_dot(h, wz_ref[...])).astype(BF16)


def _rope_partner(w):
    a = MLA_ROPE // 4
    perm = np.concatenate([np.arange(a, 2 * a), np.arange(0, a), np.arange(3 * a, 4 * a), np.arange(2 * a, 3 * a)])
    sign = np.concatenate([-np.ones(a), np.ones(a), -np.ones(a), np.ones(a)]).astype(np.float32)
    return w[:, perm] * sign


def _prep_inproj_weights(w_in, mla_w_uq, mla_w_ukv, gla_w_a2, gla_b_a):
    w_in = w_in.astype(BF16)
    offs = np.concatenate([[0], np.cumsum(IN_SIZES)])
    seg = [w_in[:, offs[i]:offs[i + 1]] for i in range(len(IN_SIZES))]
    w_cq, w_ckv, w_kr, w_gq, w_gk, w_gv, w_gr, w_af, w_ab, w_hy, w_gate = seg
    d = w_in.shape[0]
    z = lambda n: jnp.zeros((d, n), w_in.dtype)
    kr_tile = jnp.concatenate([z(MLA_NOPE), w_kr, z(HEAD_SLOT - MLA_NOPE - MLA_ROPE)], axis=1)
    krp_tile = jnp.concatenate([z(MLA_NOPE), _rope_partner(w_kr), z(HEAD_SLOT - MLA_NOPE - MLA_ROPE)], axis=1)
    a_tile = jnp.concatenate([w_af, w_ab, z(LANES - 2 * GLA_GATE_RANK)], axis=1)
    wa = jnp.concatenate([w_cq, w_ckv, kr_tile, krp_tile, a_tile], axis=1)
    wg = jnp.concatenate([w_gq, w_gk, w_gv, w_gr], axis=1)

    dh = MLA_NOPE + MLA_ROPE
    zq = lambda n: jnp.zeros((MLA_Q_LORA, n), w_in.dtype)
    plain, partner = [], []
    for hd in range(MLA_HEADS):
        blk = mla_w_uq[:, hd * dh:(hd + 1) * dh]
        plain += [blk, zq(HEAD_SLOT - dh)]
        partner += [zq(MLA_NOPE), _rope_partner(blk[:, MLA_NOPE:]), zq(HEAD_SLOT - dh)]
    wuq = jnp.concatenate(plain + partner, axis=1)

    zk = jnp.zeros((MLA_KV_LORA, HEAD_SLOT - MLA_NOPE), w_in.dtype)
    kcols, vcols = [], []
    for hd in range(MLA_HEADS):
        blk = mla_w_ukv[:, hd * (MLA_NOPE + MLA_V):(hd + 1) * (MLA_NOPE + MLA_V)]
        kcols += [blk[:, :MLA_NOPE], zk]
        vcols += [blk[:, MLA_NOPE:], zk]
    wukv = jnp.concatenate(kcols + vcols, axis=1)

    wa2 = jnp.zeros((LANES, 2 * GLA_DK), w_in.dtype)
    wa2 = wa2.at[0:GLA_GATE_RANK, 0:GLA_DK].set(gla_w_a2[0])
    wa2 = wa2.at[GLA_GATE_RANK:2 * GLA_GATE_RANK, GLA_DK:].set(gla_w_a2[1])
    ba = jnp.concatenate([gla_b_a[0], gla_b_a[1]])[None, :]
    bf = lambda t: t.astype(BF16)
    return bf(wa), bf(wg), bf(w_hy), bf(w_gate), bf(wuq), bf(wukv), bf(wa2), ba


def _rope_tables(seq, ctx_len):
    rows = seq // GRID_W
    row = np.repeat(np.arange(rows, dtype=np.float64), GRID_W)
    col = np.tile(np.arange(GRID_W, dtype=np.float64), rows)
    a = MLA_ROPE // 4
    inv = ROPE_BASE ** (-np.arange(a, dtype=np.float64) / a)
    ang_r = row[:, None] * inv
    ang_c = col[:, None] * inv
    cos32 = np.concatenate([np.cos(ang_r), np.cos(ang_r), np.cos(ang_c), np.cos(ang_c)], axis=1)
    sin32 = np.concatenate([np.sin(ang_r), np.sin(ang_r), np.sin(ang_c), np.sin(ang_c)], axis=1)
    pad_r = HEAD_SLOT - MLA_NOPE - MLA_ROPE
    cos = np.concatenate([np.ones((seq, MLA_NOPE)), cos32, np.zeros((seq, pad_r))], axis=1)
    sin = np.concatenate([np.zeros((seq, MLA_NOPE)), sin32, np.zeros((seq, pad_r))], axis=1)
    cos_c = np.concatenate([np.ones((ctx_len, MLA_NOPE + MLA_ROPE)), np.zeros((ctx_len, pad_r))], axis=1)
    sin_c = np.zeros((ctx_len, HEAD_SLOT))
    return (jnp.asarray(np.concatenate([cos, cos_c], axis=0), F32),
            jnp.asarray(np.concatenate([sin, sin_c], axis=0), F32))


def _inproj(x_lat, x_ctx, ctx_blk, mod_l, g1, weights, q_norm, kv_norm, cos, sin, n_lat_tiles):
    bsz, _, d = x_lat.shape
    nt = n_lat_tiles + 1
    t = nt * TM
    wa, wg, wh, wz, wuq, wukv, wa2, ba = weights
    tile = lambda w: pl.BlockSpec((1, TM, w), lambda b, i: (b, i, 0))
    row = lambda w: pl.BlockSpec((1, w), lambda b, i: (0, 0))
    mod_spec = pl.BlockSpec((1, 6, d), lambda b, i: (jnp.where(i < n_lat_tiles, b, bsz), 0, 0))
    tab = pl.BlockSpec((TM, HEAD_SLOT), lambda b, i: (i, 0))
    nq = MLA_HEADS * HEAD_SLOT
    sds = lambda w, dt: jax.ShapeDtypeStruct((bsz, t, w), dt)
    out_shape = (sds(nq, BF16), sds(nq, BF16), sds(nq, BF16), sds(GLA_DK, BF16), sds(GLA_DK, BF16),
                 sds(GLA_DV, BF16), sds(GLA_DV, BF16), sds(2 * GLA_DK, F32), sds(3 * HY_WIDTH, F32),
                 sds(3 * D_MODEL, BF16))
    out_specs = (tile(nq), tile(nq), tile(nq), tile(GLA_DK), tile(GLA_DK), tile(GLA_DV), tile(GLA_DV),
                 tile(2 * GLA_DK), tile(3 * HY_WIDTH), tile(3 * D_MODEL))
    return pl.pallas_call(
        functools.partial(_inproj_kernel, n_lat_tiles=n_lat_tiles),
        out_shape=out_shape,
        grid=(bsz, nt),
        in_specs=_token_specs(d, n_lat_tiles, ctx_blk) + [
            mod_spec, row(d), _const_spec(wa.shape), _const_spec(wg.shape), _const_spec(wh.shape),
            _const_spec(wz.shape), row(MLA_Q_LORA), row(MLA_KV_LORA), _const_spec(wuq.shape),
            _const_spec(wukv.shape), _const_spec(wa2.shape), row(2 * GLA_DK), tab, tab],
        out_specs=out_specs,
        compiler_params=_cparams(2, VMEM_LIMIT),
        name="inproj",
    )(x_lat, x_ctx, mod_l, g1[None, :], wa, wg, wh, wz, q_norm[None, :], kv_norm[None, :], wuq, wukv, wa2, ba,
      cos, sin)


ATT_TK = 1024


ATT_TQ = 1024
ATT_HEADS = 2


def _attn_kernel(q_ref, k_ref, v_ref, o_ref, m_ref, acc_ref, *, chunks):
    tq = q_ref.shape[1]
    n_heads = q_ref.shape[2] // HEAD_SLOT
    m_ref[...] = jnp.full(m_ref.shape, -jnp.inf, F32)
    acc_ref[...] = jnp.zeros(acc_ref.shape, F32)
    for r0, size in chunks:
        k = k_ref[0, pl.ds(r0, size), :]
        v = v_ref[0, pl.ds(r0, size), :]
        for hd in range(n_heads):
            sl = slice(hd * HEAD_SLOT, (hd + 1) * HEAD_SLOT)
            s = _dot_nt(q_ref[0, :, sl], k[:, sl])
            m_prev = m_ref[hd]
            m_new = jnp.maximum(m_prev, jnp.max(s, axis=1, keepdims=True))
            p = jnp.exp2((s - jnp.concatenate([m_new] * (size // LANES), axis=1)).astype(BF16))
            acc_ref[hd] = jnp.exp2(m_prev - m_new) * acc_ref[hd] + _dot(p, v[:, sl])
            m_ref[hd] = m_new
    lane = lax.broadcasted_iota(jnp.int32, (tq, HEAD_SLOT), 1)
    for pr in range(n_heads // 2):
        a0 = acc_ref[2 * pr]
        a1 = acc_ref[2 * pr + 1]
        o0 = a0 / pltpu.roll(a0, MLA_V, 1)
        o1 = pltpu.roll(a1, MLA_V, 1) / a1
        o_ref[0, :, pr * HEAD_SLOT:(pr + 1) * HEAD_SLOT] = jnp.where(lane < MLA_V, o0, o1).astype(o_ref.dtype)


def _attention(q, k, v, seq, with_ctx_queries):
    bsz, t, _ = q.shape
    ctx_len = t - seq
    assert seq % ATT_TK == 0 and seq % ATT_TQ == 0 and seq % ctx_len == 0 and 2 * MLA_V == HEAD_SLOT
    n_chunks = seq // ATT_TK
    chunks = tuple((j * ATT_TK, ATT_TK) for j in range(n_chunks - 1))
    chunks += (((n_chunks - 1) * ATT_TK, ATT_TK + ctx_len),)
    scratch = lambda nh, tq: [pltpu.VMEM((nh, tq, LANES), F32), pltpu.VMEM((nh, tq, HEAD_SLOT), F32)]
    hps = ATT_HEADS
    y = pl.pallas_call(
        functools.partial(_attn_kernel, chunks=chunks),
        out_shape=jax.ShapeDtypeStruct((bsz, seq, MLA_OUT), BF16),
        grid=(bsz, MLA_HEADS // hps, seq // ATT_TQ),
        in_specs=[
            pl.BlockSpec((1, ATT_TQ, hps * HEAD_SLOT), lambda b, hp, i: (b, i, hp)),
            pl.BlockSpec((1, t, hps * HEAD_SLOT), lambda b, hp, i: (b, 0, hp)),
            pl.BlockSpec((1, t, hps * HEAD_SLOT), lambda b, hp, i: (b, 0, hp)),
        ],
        out_specs=pl.BlockSpec((1, ATT_TQ, hps * MLA_V), lambda b, hp, i: (b, i, hp)),
        scratch_shapes=scratch(hps, ATT_TQ),
        compiler_params=_cparams(3, VMEM_LIMIT),
        name="mla_attention",
    )(q, k, v)
    if not with_ctx_queries:
        return y, None
    cblk = seq // ctx_len
    ctx_rows = pl.BlockSpec((1, ctx_len, MLA_HEADS * HEAD_SLOT), lambda b: (b, cblk, 0))
    y_ctx = pl.pallas_call(
        functools.partial(_attn_kernel, chunks=((0, ctx_len),)),
        out_shape=jax.ShapeDtypeStruct((bsz, ctx_len, MLA_OUT), BF16),
        grid=(bsz,),
        in_specs=[ctx_rows, ctx_rows, ctx_rows],
        out_specs=pl.BlockSpec((1, ctx_len, MLA_OUT), lambda b: (b, 0, 0)),
        scratch_shapes=scratch(MLA_HEADS, ctx_len),
        compiler_params=_cparams(1),
        name="mla_attention_ctx",
    )(q, k, v)
    return y, y_ctx


GLA_LEVELS = int(math.log2(TM))
GLA_SAFE_SPAN = 60.0


def _gla_level_matrices():
    i = np.arange(TM)[:, None]
    t = np.arange(TM)[None, :]
    fwd = [(t <= i)]
    for lv in range(GLA_LEVELS):
        m = TM >> (lv + 1)
        lo = (i // m) * m
        later = ((i // m) % 2) == 1
        q_part = later & (t >= lo) & (t <= i)
        k_part = (~later) & (t > i) & (t <= lo + m - 1)
        fwd.append(q_part | k_part)
    fwd = np.concatenate(fwd, axis=0).astype(np.float32)
    nb = 1 + GLA_LEVELS
    bwd = fwd.reshape(nb, TM, TM)[:, ::-1, ::-1].reshape(nb * TM, TM)
    return np.stack([fwd, bwd])


def _gla_kernel(qf_ref, kf_ref, vf_ref, gf_ref, qb_ref, kb_ref, vb_ref, gb_ref, mall_ref, of_ref, ob_ref, s_ref, a_ref):
    step = pl.program_id(1)

    @pl.when(step == 0)
    def _():
        s_ref[...] = jnp.zeros(s_ref.shape, F32)

    ins = ((qf_ref, kf_ref, vf_ref, gf_ref), (qb_ref, kb_ref, vb_ref, gb_ref))
    outs = (of_ref, ob_ref)
    qs, ks, vs, g2s, g_cums, g_tots = [], [], [], [], [], []
    for d in range(2):
        q_ref, k_ref, v_ref, g_ref = ins[d]
        qs.append(q_ref[0].astype(F32))
        ks.append(k_ref[0].astype(F32))
        vs.append(v_ref[0])
        g = g_ref[0]
        g2 = jnp.concatenate([g.astype(BF16), (g - g.astype(BF16).astype(F32)).astype(BF16)], axis=1)
        e2 = _dot(mall_ref[d, 0:TM, :], g2)
        g2s.append(g2)
        g_cums.append(e2[:, :GLA_DK] + e2[:, GLA_DK:])
        g_tots.append(jnp.sum(g, axis=0, keepdims=True))

    row = lax.broadcasted_iota(jnp.int32, (TM, TM), 0)
    col = lax.broadcasted_iota(jnp.int32, (TM, TM), 1)
    lane_head = lax.broadcasted_iota(jnp.int32, (TM, GLA_DK), 1) // GLA_HK
    tok = lax.broadcasted_iota(jnp.int32, (TM, GLA_DK), 0)
    eye = row == col

    def stack_heads(t):
        return jnp.concatenate([jnp.where(lane_head == hd, t, 0.0) for hd in range(GLA_HEADS)], axis=0).astype(BF16)

    span = jnp.maximum(jnp.max(-g_tots[0]), jnp.max(-g_tots[1]))

    @pl.when(span < GLA_SAFE_SPAN)
    def _():
        for d in range(2):
            res = _dot_nt(stack_heads(qs[d] * jnp.exp(g_cums[d])), (ks[d] * jnp.exp(-g_cums[d])).astype(BF16))
            seen = (col <= row) if d == 0 else (col >= row)
            for hd in range(GLA_HEADS):
                a_ref[d, hd] = jnp.where(seen, res[hd * TM:(hd + 1) * TM], 0.0)

    @pl.when(span >= GLA_SAFE_SPAN)
    def _():
        for d in range(2):
            q, k = qs[d], ks[d]
            e2l = _dot(mall_ref[d, TM:, :], g2s[d])
            e_lv = e2l[:, :GLA_DK] + e2l[:, GLA_DK:]
            res = _dot_nt(stack_heads(q), k.astype(BF16))
            for hd in range(GLA_HEADS):
                a_ref[d, hd] = jnp.where(eye, res[hd * TM:(hd + 1) * TM], 0.0)
            for lv in range(GLA_LEVELS):
                m = TM >> (lv + 1)
                w = jnp.exp(e_lv[lv * TM:(lv + 1) * TM])
                bit = (tok // m) % 2
                q_act = bit != d
                qt = jnp.where(q_act, q * w, 0.0)
                kt = jnp.where(q_act, 0.0, k * w).astype(BF16)
                res = _dot_nt(stack_heads(qt), kt)
                if m == TM // 2:
                    for hd in range(GLA_HEADS):
                        a_ref[d, hd] += res[hd * TM:(hd + 1) * TM]
                else:
                    same = (row // (2 * m)) == (col // (2 * m))
                    for hd in range(GLA_HEADS):
                        a_ref[d, hd] += jnp.where(same, res[hd * TM:(hd + 1) * TM], 0.0)

    same_head = (lax.broadcasted_iota(jnp.int32, (GLA_DK, GLA_DV), 0) // GLA_HK
                 == lax.broadcasted_iota(jnp.int32, (GLA_DK, GLA_DV), 1) // GLA_HV)
    for d in range(2):
        q, k, v, g_cum, g_tot = qs[d], ks[d], vs[d], g_cums[d], g_tots[d]
        s_old = s_ref[d]
        o_inter = _dot((q * jnp.exp(g_cum)).astype(BF16), s_old.astype(BF16))
        for hd in range(GLA_HEADS):
            sl = slice(hd * GLA_HV, (hd + 1) * GLA_HV)
            o_intra = _dot(a_ref[d, hd].astype(BF16), v[:, sl])
            outs[d][0, :, sl] = (o_intra + o_inter[:, sl]).astype(outs[d].dtype)

        kdec_t = (k * jnp.exp(g_tot - g_cum)).T.astype(BF16)
        upd = _dot(kdec_t, v)
        dec_col = jnp.sum(jnp.where(eye, jnp.broadcast_to(jnp.exp(g_tot), (TM, GLA_DK)), 0.0), axis=1,
                          keepdims=True)
        s_ref[d] = dec_col * s_old + jnp.where(same_head, upd, 0.0)


def _gla(gq, gk, gv, glog, n_lat_tiles):
    assert GLA_DK == TM
    bsz, t, _ = gq.shape
    nt = t // TM
    mall = jnp.asarray(_gla_level_matrices(), dtype=BF16)

    def tile_idx(d, s):
        return jnp.where(s == 0, n_lat_tiles, s - 1 if d == 0 else n_lat_tiles - s)

    def specs(d):
        tok = lambda w: pl.BlockSpec((1, TM, w), lambda b, s: (b, tile_idx(d, s), 0))
        return [tok(GLA_DK), tok(GLA_DK), tok(GLA_DV), pl.BlockSpec((1, TM, GLA_DK), lambda b, s: (b, tile_idx(d, s), d))]

    out_spec = lambda d: pl.BlockSpec((1, TM, GLA_DV), lambda b, s: (b, tile_idx(d, s), 0))
    sds = jax.ShapeDtypeStruct((bsz, t, GLA_DV), BF16)
    return pl.pallas_call(
        _gla_kernel,
        out_shape=(sds, sds),
        grid=(bsz, nt),
        in_specs=specs(0) + specs(1) + [_const_spec(mall.shape)],
        out_specs=(out_spec(0), out_spec(1)),
        scratch_shapes=[pltpu.VMEM((2, GLA_DK, GLA_DV), F32), pltpu.VMEM((2, GLA_HEADS, TM, TM), F32)],
        compiler_params=_cparams(2),
        name="gla_scan",
    )(gq, gk, gv, glog, gq, gk, gv, glog, mall)


HY_N1 = 64
HY_N2 = 128
HY_SLAB = SUBLANES
HY_CB = 256
HY_KG = 16
HY_SPEC_CB = 128
HY_UNROLL = 16


def _hyena_dft_constants(seq):
    n = 2 * seq
    assert n == HY_N1 * HY_N2
    nh = HY_N1 // 2
    kh = HY_N2 // 2
    eye = np.eye(HY_SLAB)
    k1 = np.arange(HY_N1)
    th = 2 * np.pi * np.outer(k1 + 0.5, np.arange(nh)) / HY_N1
    fwd1 = np.concatenate([np.kron(np.cos(th), eye), np.kron(-np.sin(th), eye)], axis=0)
    inv1 = (2.0 / n) * np.concatenate([np.kron(np.cos(th).T, eye), np.kron(-np.sin(th).T, eye)], axis=1)
    nlo = np.arange(HY_N2)
    k2 = np.arange(kh)
    ph = 2 * np.pi * (k2[None, :, None] * nlo[None, None, :] / HY_N2
                      + (k1[:, None, None] + 0.5) * nlo[None, None, :] / n)
    c, s = np.cos(ph), np.sin(ph)
    fwd2 = np.concatenate([np.concatenate([c, s], axis=2), np.concatenate([-s, c], axis=2)], axis=1)
    ct, st = c.transpose(0, 2, 1), s.transpose(0, 2, 1)
    inv2 = np.concatenate([np.concatenate([ct, -st], axis=2), np.concatenate([st, ct], axis=2)], axis=1)
    return fwd1, fwd2, inv2, inv1


def _hyena_ctx_dft_constants(ctx_len):
    n = 2 * ctx_len
    th = 2 * np.pi * np.outer(np.arange(ctx_len) + 0.5, np.arange(ctx_len)) / n
    fwd = np.concatenate([np.cos(th), -np.sin(th)], axis=0)
    inv = (2.0 / n) * np.concatenate([np.cos(th).T, -np.sin(th).T], axis=1)
    return fwd, inv


def _hyena_features(length):
    pos = np.arange(length, dtype=np.float64)
    t = pos / max(length - 1, 1)
    f = np.linspace(1e-4, HY_BANDS - 1, HY_BANDS)
    ang = (2.0 * math.pi / length) * pos[:, None] * f
    feat = np.concatenate([t[:, None], np.cos(ang), np.sin(ang)], axis=-1)
    return jnp.asarray(np.pad(feat, ((0, 0), (0, LANES - HY_POS_DIM))), F32)


def _hy_filter_kernel(feat_ref, w1_ref, b1_ref, w2_ref, b2_ref, w3_ref, b3_ref, absd_ref, h_ref, s_ref):
    i = pl.program_id(0)
    feat = feat_ref[...]
    hp = _dot3
    hdn = jnp.sin(hp(feat, w1_ref[...]) + b1_ref[...])
    hdn = jnp.sin(hp(hdn, w2_ref[...]) + b2_ref[...])
    h = hp(hdn, w3_ref[...]) + b3_ref[...]
    window = jnp.exp(-feat[:, 0:1] * absd_ref[...])
    h = h * jnp.concatenate([window] * (2 * HY_ORDER), axis=1)
    h_ref[...] = h

    @pl.when(i == 0)
    def _():
        s_ref[...] = jnp.zeros(s_ref.shape, F32)

    s_ref[...] += jnp.sum(jnp.abs(h), axis=0, keepdims=True)


def _hyena_filters_raw(length, filt_w):
    w1, b1, w2, b2, w3, b3 = filt_w
    nf = 2 * HY_ORDER * HY_WIDTH
    tr = min(length, 512)
    deltas = np.linspace(math.log(HY_DECAY_TARGET) / HY_FAST_DECAY, math.log(HY_DECAY_TARGET) / HY_SLOW_DECAY,
                         HY_WIDTH, dtype=np.float32)
    absd = jnp.asarray(np.abs(deltas))[None, :]
    w1p = jnp.pad(w1, ((0, LANES - HY_POS_DIM), (0, 0)))
    full = lambda shp: pl.BlockSpec(shp, lambda i: (0,) * len(shp))
    return pl.pallas_call(
        _hy_filter_kernel,
        out_shape=(jax.ShapeDtypeStruct((length, nf), F32), jax.ShapeDtypeStruct((1, nf), F32)),
        grid=(length // tr,),
        in_specs=[pl.BlockSpec((tr, LANES), lambda i: (i, 0)), full((LANES, HY_FILTER_HIDDEN)),
                  full((1, HY_FILTER_HIDDEN)), full((HY_FILTER_HIDDEN, HY_FILTER_HIDDEN)), full((1, HY_FILTER_HIDDEN)),
                  full((HY_FILTER_HIDDEN, nf)), full((1, nf)), full((1, HY_WIDTH))],
        out_specs=(pl.BlockSpec((tr, nf), lambda i: (i, 0)), full((1, nf))),
        compiler_params=_cparams(1),
        name="hyena_filter_mlp",
    )(_hyena_features(length), w1p, b1[None, :], w2, b2[None, :], w3, b3[None, :], absd)


def _dot_split(m, x):
    x_hi = x.astype(BF16)
    x_lo = (x - x_hi.astype(F32)).astype(BF16)
    n = x.shape[1]
    r = _dot(m, jnp.concatenate([x_hi, x_lo], axis=1))
    return r[:, :n] + r[:, n:]


def _odft_stage1(src_at, mm, s_re, s_im, unroll=2):
    nk = s_re.shape[0]
    half = nk * HY_SLAB

    def body(j, carry):
        r0 = pl.multiple_of(j * HY_SLAB, HY_SLAB)
        slab = src_at(r0)
        cb = slab.shape[-1]
        res = mm(slab.reshape(-1, cb))
        s_re[:, pl.ds(r0, HY_SLAB), :] = res[:half].reshape(nk, HY_SLAB, cb)
        s_im[:, pl.ds(r0, HY_SLAB), :] = res[half:].reshape(nk, HY_SLAB, cb)
        return carry

    lax.fori_loop(0, HY_N2 // HY_SLAB, body, 0, unroll=unroll)


def _hy_spectrum_kernel(hf_ref, hb_ref, sf_ref, sb_ref, fwd1_ref, fwd2_ref, o_ref, s_re, s_im):
    kh = HY_N2 // 2
    mm1 = lambda x: _dot_split(fwd1_ref[...], x)

    def middle(sign):
        def body(k1, carry):
            a = jnp.concatenate([s_re[k1], s_im[k1]], axis=0)
            x = _dot_split(fwd2_ref[k1], a)
            if sign is None:
                o_ref[0, k1] = x[:kh]
                o_ref[1, k1] = x[kh:]
            else:
                inv_norm = 1.0 / (sf_ref[...] + sb_ref[...])
                o_ref[0, k1] = (o_ref[0, k1] + x[:kh]) * inv_norm
                o_ref[1, k1] = (o_ref[1, k1] - x[kh:]) * inv_norm
            return carry
        lax.fori_loop(0, HY_N1, body, 0, unroll=32)

    _odft_stage1(lambda r0: hf_ref[:, pl.ds(r0, HY_SLAB), :], mm1, s_re, s_im, unroll=16)
    middle(None)

    def bwd_slab(r0):
        slab = hb_ref[:, pl.ds(r0, HY_SLAB), :]
        nhi = lax.broadcasted_iota(jnp.int32, slab.shape, 0)
        r = lax.broadcasted_iota(jnp.int32, slab.shape, 1)
        return jnp.where((nhi == 0) & (r + r0 == 0), 0.0, slab)

    _odft_stage1(bwd_slab, mm1, s_re, s_im, unroll=16)
    middle(-1)


def _hyena_filter_spectrum(seq, w1, b1, w2, b2, w3, b3):
    h_raw, s = _hyena_filters_raw(seq, (w1, b1, w2, b2, w3, b3))
    nh = HY_N1 // 2
    nc = HY_ORDER * HY_WIDTH
    h3 = h_raw.reshape(nh, HY_N2, 2 * nc)
    fwd1, fwd2, _, _ = _hyena_dft_constants(seq)
    fwd1 = jnp.asarray(fwd1, BF16)
    fwd2 = jnp.asarray(fwd2, BF16)
    scb = HY_SPEC_CB
    ncb = nc // scb
    return pl.pallas_call(
        _hy_spectrum_kernel,
        out_shape=jax.ShapeDtypeStruct((2, HY_N1, HY_N2 // 2, nc), F32),
        grid=(ncb,),
        in_specs=[pl.BlockSpec((nh, HY_N2, scb), lambda c: (0, 0, c)),
                  pl.BlockSpec((nh, HY_N2, scb), lambda c: (0, 0, ncb + c)),
                  pl.BlockSpec((1, scb), lambda c: (0, c)),
                  pl.BlockSpec((1, scb), lambda c: (0, ncb + c)),
                  _const_spec(fwd1.shape), _const_spec(fwd2.shape)],
        out_specs=pl.BlockSpec((2, HY_N1, HY_N2 // 2, scb), lambda c: (0, 0, 0, c)),
        scratch_shapes=[pltpu.VMEM((HY_N1, HY_N2, scb), F32), pltpu.VMEM((HY_N1, HY_N2, scb), F32)],
        compiler_params=_cparams(1, VMEM_LIMIT),
        name="hyena_filter_spectrum",
    )(h3, h3, s, s, fwd1, fwd2)


def _short_conv_chunk(ref, c, n_chunks, w_ref, b_ref):
    per = TM // HY_N2
    cur = ref[0, pl.ds(per * c, per)]
    cb = cur.shape[-1]
    cur = cur.reshape(TM, cb)
    prev = ref[0, jnp.maximum(per * c - 1, 0), pl.ds(HY_N2 - SUBLANES, SUBLANES), :][SUBLANES - 1:SUBLANES]
    nxt = ref[0, jnp.minimum(per * c + per, per * n_chunks - 1), pl.ds(0, SUBLANES), :][0:1]
    prev = jnp.where(c > 0, prev, 0.0)
    nxt = jnp.where(c < n_chunks - 1, nxt, 0.0)
    rowi = lax.broadcasted_iota(jnp.int32, (TM, cb), 0)
    dn = jnp.where(rowi == 0, prev, pltpu.roll(cur, 1, 0))
    up = jnp.where(rowi == TM - 1, nxt, pltpu.roll(cur, TM - 1, 0))
    return b_ref[...] + w_ref[0:1, :] * dn + w_ref[1:2, :] * cur + w_ref[2:3, :] * up


def _hy_conv_kernel(*refs, conv_y):
    if conv_y:
        (y_ref, g_ref, h_ref, bias_ref, wy_ref, by_ref, wg_ref, bg_ref, fwd1_ref, fwd2_ref, inv2_ref, inv1_ref,
         o_ref, s_re, s_im, gs_ref, us_ref) = refs
    else:
        (y_ref, g_ref, h_ref, bias_ref, wg_ref, bg_ref, fwd1_ref, fwd2_ref, inv2_ref, inv1_ref,
         o_ref, s_re, s_im, gs_ref) = refs
    grp = pl.program_id(2)
    last = pl.num_programs(2) - 1
    nh = HY_N1 // 2
    per = TM // HY_N2
    n_chunks = nh // per
    cb = o_ref.shape[-1]
    kh = HY_N2 // 2

    @pl.when(grp == 0)
    def _():
        def pre(c, carry):
            gs_ref[pl.ds(per * c, per)] = _short_conv_chunk(g_ref, c, n_chunks, wg_ref, bg_ref).reshape(per, HY_N2, cb)
            if conv_y:
                us_ref[pl.ds(per * c, per)] = _short_conv_chunk(y_ref, c, n_chunks, wy_ref, by_ref).reshape(
                    per, HY_N2, cb)
            return carry
        lax.fori_loop(0, n_chunks, pre, 0)

    if conv_y:
        u_at = lambda r0: us_ref[:, pl.ds(r0, HY_SLAB), :]
    else:
        u_at = lambda r0: y_ref[0, :, pl.ds(r0, HY_SLAB), :]

    _odft_stage1(u_at, lambda x: _dot(fwd1_ref[0], x.astype(BF16)), s_re, s_im, unroll=HY_UNROLL)

    def middle(k1, carry):
        a = jnp.concatenate([s_re[k1], s_im[k1]], axis=0).astype(BF16)
        x = _dot(fwd2_ref[k1], a)
        xr, xi = x[:kh], x[kh:]
        hr, hi = h_ref[0, k1], h_ref[1, k1]
        y = jnp.concatenate([xr * hr - xi * hi, xr * hi + xi * hr], axis=0).astype(BF16)
        bm = _dot(inv2_ref[k1], y)
        s_re[k1] = bm[:HY_N2]
        s_im[k1] = bm[HY_N2:]
        return carry

    lax.fori_loop(0, HY_KG, middle, 0, unroll=HY_KG)

    def partial_conv(r0):
        slab = jnp.concatenate([s_re[:, pl.ds(r0, HY_SLAB), :].reshape(HY_KG * HY_SLAB, cb),
                                s_im[:, pl.ds(r0, HY_SLAB), :].reshape(HY_KG * HY_SLAB, cb)], axis=0).astype(BF16)
        return _dot(inv1_ref[0], slab).reshape(nh, HY_SLAB, cb)

    def post_loop(fn):
        def post(j, carry):
            r0 = pl.multiple_of(j * HY_SLAB, HY_SLAB)
            o_ref[0, :, pl.ds(r0, HY_SLAB), :] = fn(r0, partial_conv(r0))
            return carry
        lax.fori_loop(0, HY_N2 // HY_SLAB, post, 0, unroll=HY_UNROLL)

    @pl.when(grp == 0)
    def _():
        post_loop(lambda r0, part: part)

    @pl.when((grp > 0) & (grp < last))
    def _():
        post_loop(lambda r0, part: o_ref[0, :, pl.ds(r0, HY_SLAB), :] + part)

    @pl.when(grp == last)
    def _():
        post_loop(lambda r0, part: gs_ref[:, pl.ds(r0, HY_SLAB), :]
                  * (o_ref[0, :, pl.ds(r0, HY_SLAB), :] + part + bias_ref[...] * u_at(r0)))


def _hyena_order(y4, y_col0, z4, gate_col0, hspec, order, hy_bias, short_w, short_b, consts, conv_y):
    bsz = z4.shape[0]
    nh = HY_N1 // 2
    ncb = HY_WIDTH // HY_CB
    ngrp = HY_N1 // HY_KG
    fwd1, fwd2, inv2, inv1 = consts
    blk4 = lambda off: pl.BlockSpec((1, nh, HY_N2, HY_CB), lambda c, b, g: (b, 0, 0, off + c))
    rowspec = lambda rows, off: pl.BlockSpec((rows, HY_CB), lambda c, b, g: (0, off + c))
    grouped = lambda shp: pl.BlockSpec(shp, lambda c, b, g: (g, 0, 0))
    in_specs = [blk4(y_col0), blk4(gate_col0),
                pl.BlockSpec((2, HY_KG, HY_N2 // 2, HY_CB), lambda c, b, g: (0, g, 0, order * ncb + c)),
                rowspec(1, 0)]
    args = [y4, z4, hspec, hy_bias.reshape(1, -1)]
    if conv_y:
        in_specs += [rowspec(HY_SHORT, y_col0), rowspec(1, y_col0)]
        args += [short_w, short_b[None, :]]
    in_specs += [rowspec(HY_SHORT, gate_col0), rowspec(1, gate_col0)]
    args += [short_w, short_b[None, :]]
    in_specs += [grouped((1,) + fwd1.shape[1:]), grouped((HY_KG,) + fwd2.shape[1:]),
                 grouped((HY_KG,) + inv2.shape[1:]), grouped((1,) + inv1.shape[1:])]
    args += [fwd1, fwd2, inv2, inv1]
    scratch = [pltpu.VMEM((HY_KG, HY_N2, HY_CB), F32), pltpu.VMEM((HY_KG, HY_N2, HY_CB), F32),
               pltpu.VMEM((nh, HY_N2, HY_CB), F32)]
    if conv_y:
        scratch.append(pltpu.VMEM((nh, HY_N2, HY_CB), F32))
    return pl.pallas_call(
        functools.partial(_hy_conv_kernel, conv_y=conv_y),
        out_shape=jax.ShapeDtypeStruct((bsz, nh, HY_N2, HY_WIDTH), F32),
        grid=(ncb, bsz, ngrp),
        in_specs=in_specs,
        out_specs=pl.BlockSpec((1, nh, HY_N2, HY_CB), lambda c, b, g: (b, 0, 0, c)),
        scratch_shapes=scratch,
        compiler_params=_cparams(3, VMEM_LIMIT),
        name="hyena_conv%d" % order,
    )(*args)


def _hyena_latent(z_hy, hspec, short_w, short_b, hy_bias):
    bsz, t, _ = z_hy.shape
    seq = HY_N1 * HY_N2 // 2
    fwd1, fwd2, inv2, inv1 = _hyena_dft_constants(seq)
    ngrp = HY_N1 // HY_KG
    rows = HY_KG * HY_SLAB
    fwd1 = fwd1.reshape(2, ngrp, rows, -1).transpose(1, 0, 2, 3).reshape(ngrp, 2 * rows, -1)
    inv1 = inv1.reshape(-1, 2, ngrp, rows).transpose(2, 0, 1, 3).reshape(ngrp, -1, 2 * rows)
    consts = tuple(jnp.asarray(m, BF16) for m in (fwd1, fwd2, inv2, inv1))
    z4 = z_hy.reshape(bsz, t // HY_N2, HY_N2, 3 * HY_WIDTH)
    ncb = HY_WIDTH // HY_CB
    y1 = _hyena_order(z4, 2 * ncb, z4, 0, hspec, 0, hy_bias[0], short_w, short_b, consts, True)
    y2 = _hyena_order(y1, 0, z4, ncb, hspec, 1, hy_bias[1], short_w, short_b, consts, False)
    return y2.reshape(bsz, seq, HY_WIDTH)


def _hy_ctx_spectrum_kernel(h_ref, s_ref, fwd_ref, o_ref):
    lc = h_ref.shape[0]
    nc = HY_ORDER * HY_WIDTH
    hp = _dot3
    h = h_ref[...]
    rowi = lax.broadcasted_iota(jnp.int32, (lc, nc), 0)
    xf = hp(fwd_ref[...], h[:, :nc])
    xb = hp(fwd_ref[...], jnp.where(rowi == 0, 0.0, h[:, nc:]))
    inv_norm = 1.0 / (s_ref[:, :nc] + s_ref[:, nc:])
    o_ref[0] = (xf[:lc] + xb[:lc]) * inv_norm
    o_ref[1] = (xf[lc:] - xb[lc:]) * inv_norm


def _hy_ctx_conv_kernel(x1_ref, x2_ref, v_ref, h_ref, bias_ref, w_ref, b_ref, fwd_ref, inv_ref, o_ref):
    lc = o_ref.shape[1]
    rowi = lax.broadcasted_iota(jnp.int32, (lc, HY_WIDTH), 0)

    def short(ref, part):
        cur = ref[0].reshape(lc, HY_WIDTH)
        sl = slice(part * HY_WIDTH, (part + 1) * HY_WIDTH)
        dn = jnp.where(rowi == 0, 0.0, pltpu.roll(cur, 1, 0))
        up = jnp.where(rowi == lc - 1, 0.0, pltpu.roll(cur, lc - 1, 0))
        return b_ref[:, sl] + w_ref[0:1, sl] * dn + w_ref[1:2, sl] * cur + w_ref[2:3, sl] * up

    y = short(v_ref, 2)
    for order, gref in enumerate((x1_ref, x2_ref)):
        sl = slice(order * HY_WIDTH, (order + 1) * HY_WIDTH)
        x = _dot(fwd_ref[...], y.astype(BF16))
        xr, xi = x[:lc], x[lc:]
        hr, hi = h_ref[0, :, sl], h_ref[1, :, sl]
        prod = jnp.concatenate([xr * hr - xi * hi, xr * hi + xi * hr], axis=0).astype(BF16)
        conv = _dot(inv_ref[...], prod)
        y = short(gref, order) * (conv + bias_ref[order:order + 1, :] * y)
    o_ref[0] = y


def _hyena_ctx(z_hy, filt_w, short_w, short_b, hy_bias):
    bsz, t, _ = z_hy.shape
    seq = HY_N1 * HY_N2 // 2
    lc = t - seq
    per = lc // HY_N2
    h_raw, s = _hyena_filters_raw(lc, filt_w)
    fwd, inv = _hyena_ctx_dft_constants(lc)
    nc = HY_ORDER * HY_WIDTH
    full = lambda shp: pl.BlockSpec(shp, lambda *_: (0,) * len(shp))
    hspec = pl.pallas_call(
        _hy_ctx_spectrum_kernel,
        out_shape=jax.ShapeDtypeStruct((2, lc, nc), F32),
        grid=(1,),
        in_specs=[full(h_raw.shape), full(s.shape), full(fwd.shape)],
        out_specs=full((2, lc, nc)),
        compiler_params=_cparams(1),
        name="hyena_ctx_spectrum",
    )(h_raw, s, jnp.asarray(fwd, F32))
    z4 = z_hy.reshape(bsz, t // HY_N2, HY_N2, 3 * HY_WIDTH)
    blk = lambda part: pl.BlockSpec((1, per, HY_N2, HY_WIDTH), lambda b: (b, seq // lc, 0, part))
    return pl.pallas_call(
        _hy_ctx_conv_kernel,
        out_shape=jax.ShapeDtypeStruct((bsz, lc, HY_WIDTH), F32),
        grid=(bsz,),
        in_specs=[blk(0), blk(1), blk(2), full((2, lc, nc)), full((HY_ORDER, HY_WIDTH)),
                  full((HY_SHORT, 3 * HY_WIDTH)), full((1, 3 * HY_WIDTH)), full(fwd.shape), full(inv.shape)],
        out_specs=pl.BlockSpec((1, lc, HY_WIDTH), lambda b: (b, 0, 0)),
        compiler_params=_cparams(1),
        name="hyena_ctx_conv",
    )(z4, z4, z4, hspec, hy_bias, short_w, short_b[None, :], jnp.asarray(fwd, BF16), jnp.asarray(inv, BF16))


FF_CHUNK = 1024


def _merge_mlp_kernel(x_ref, c_ref, mod_ref, ymla_ref, ymlac_ref, of_ref, ob_ref, gr_ref, yhy_ref, yhyc_ref, zg_ref, on_ref,
                      wm_ref, wgl_ref, wh_ref, wo_ref, g2_ref, w1_ref, w2_ref, fg_ref, o_ref, *, final, n_lat_tiles):
    o = of_ref[0].astype(F32) + ob_ref[0].astype(F32)
    silu = gr_ref[0].astype(F32)
    parts = []
    for hd in range(GLA_HEADS):
        sl = slice(hd * GLA_HV, (hd + 1) * GLA_HV)
        parts.append((_rms(o[:, sl]) * on_ref[...] * silu[:, sl]).astype(BF16))
    y_gla = jnp.concatenate(parts, axis=1)
    zg = zg_ref[0].astype(F32)
    d = x_ref.shape[-1]
    m = zg[:, 0:d] * _dot(_token_tile(ymla_ref, ymlac_ref, n_lat_tiles), wm_ref[...])
    m = m + zg[:, d:2 * d] * _dot(y_gla, wgl_ref[...])
    m = m + zg[:, 2 * d:3 * d] * _dot(_token_tile(yhy_ref, yhyc_ref, n_lat_tiles).astype(BF16), wh_ref[...])
    out = _dot(m.astype(BF16), wo_ref[...])
    x = _token_tile(x_ref, c_ref, n_lat_tiles) + mod_ref[0, 2:3, :] * out

    h = (_rms(x) * (g2_ref[...] * (1.0 + mod_ref[0, 4:5, :])) + mod_ref[0, 3:4, :]).astype(BF16)
    acc = jnp.zeros(x.shape, F32)
    for j in range(w1_ref.shape[1] // FF_CHUNK):
        a = jnp.maximum(_dot(h, w1_ref[:, j * FF_CHUNK:(j + 1) * FF_CHUNK]), 0.0)
        acc = acc + _dot((a * a).astype(BF16), w2_ref[j * FF_CHUNK:(j + 1) * FF_CHUNK, :])
    xn = x + mod_ref[0, 5:6, :] * acc
    if final:
        xn = _rms(xn) * fg_ref[...]
    o_ref[0] = xn


def _mod_spec(d, n_lat_tiles, bsz):
    return pl.BlockSpec((1, 6, d), lambda b, i: (jnp.where(i < n_lat_tiles, b, bsz), 0, 0))


def _merge_mlp(x_lat, x_ctx, ctx_blk, mod_l, y_mla, y_mla_ctx, o_gla, gr, y_hy, y_hy_ctx, gate, out_norm,
               w_o_mla, w_o_gla, w_o_hy, w_out, g2, w1, w2, final_g, n_tiles, n_lat_tiles, final):
    bsz, _, d = x_lat.shape
    if y_mla_ctx is None:
        y_mla_ctx, y_hy_ctx = y_mla, y_hy
    tile = lambda w: pl.BlockSpec((1, TM, w), lambda b, i: (b, i, 0))
    row = lambda w: pl.BlockSpec((1, w), lambda b, i: (0, 0))
    bf = lambda w: w.astype(BF16)
    return pl.pallas_call(
        functools.partial(_merge_mlp_kernel, final=final, n_lat_tiles=n_lat_tiles if n_tiles > n_lat_tiles else None),
        out_shape=jax.ShapeDtypeStruct((bsz, n_tiles * TM, d), F32),
        grid=(bsz, n_tiles),
        in_specs=_token_specs(d, n_lat_tiles, ctx_blk) + [_mod_spec(d, n_lat_tiles, bsz)]
        + _token_specs(MLA_OUT, n_lat_tiles, 0) + [tile(GLA_DV), tile(GLA_DV), tile(GLA_DV)]
        + _token_specs(HY_WIDTH, n_lat_tiles, 0) + [
            tile(3 * d), row(GLA_HV),
            _const_spec(w_o_mla.shape), _const_spec(w_o_gla.shape), _const_spec(w_o_hy.shape),
            _const_spec(w_out.shape), row(d), _const_spec(w1.shape), _const_spec(w2.shape), row(d)],
        out_specs=tile(d),
        compiler_params=_cparams(2, VMEM_LIMIT),
        name="merge_mlp",
    )(x_lat, x_ctx, mod_l, y_mla, y_mla_ctx, o_gla[0], o_gla[1], gr, y_hy, y_hy_ctx, gate, out_norm[None, :],
      bf(w_o_mla), bf(w_o_gla), bf(w_o_hy), bf(w_out), g2[None, :], bf(w1), bf(w2), final_g[None, :])


def kernel(x, c, ctx, c_ctx, ada_w, ada_b, norm1_g, norm2_g, w_in, mla_q_norm, mla_w_uq, mla_kv_norm, mla_w_ukv, gla_w_a2, gla_b_a, gla_out_norm, hy_short_w, hy_short_b, hy_f_w1, hy_f_b1, hy_f_w2, hy_f_b2, hy_f_w3, hy_f_b3, hy_bias, w_o_mla, w_o_gla, w_o_hy, w_out, ff_w1, ff_w2, final_norm_g):
    bsz, seq, d = x.shape
    ctx_len = ctx.shape[1]
    n_lat = seq // TM
    assert ctx_len == TM and seq % TM == 0
    n_all = n_lat + 1
    x_lat, x_ctx, ctx_blk = x, ctx, 0
    cc = jnp.zeros((16, d), F32).at[:bsz].set(c).at[bsz].set(c_ctx)
    mod = _modulation(cc, ada_w, ada_b).reshape(DEPTH, 16, 6, d)
    cos, sin = _rope_tables(seq, ctx_len)
    for l in range(DEPTH):
        last = l == DEPTH - 1
        n_tiles = n_lat if last else n_all
        weights = _prep_inproj_weights(w_in[l], mla_w_uq[l], mla_w_ukv[l], gla_w_a2[l], gla_b_a[l])
        q, k, v, gq, gk, gv, gr, glog, z_hy, z_gate = _inproj(x_lat, x_ctx, ctx_blk, mod[l], norm1_g[l], weights,
                                                             mla_q_norm[l], mla_kv_norm[l], cos, sin, n_lat)
        y_mla, y_mla_c = _attention(q, k, v, seq, not last)
        o_gla = _gla(gq, gk, gv, glog, n_lat)
        filt_w = (hy_f_w1[l], hy_f_b1[l], hy_f_w2[l], hy_f_b2[l], hy_f_w3[l], hy_f_b3[l])
        hspec = _hyena_filter_spectrum(seq, *filt_w)
        y_hy = _hyena_latent(z_hy, hspec, hy_short_w[l], hy_short_b[l], hy_bias[l])
        y_hy_c = None if last else _hyena_ctx(z_hy, filt_w, hy_short_w[l], hy_short_b[l], hy_bias[l])
        xc = _merge_mlp(x_lat, x_ctx, ctx_blk, mod[l], y_mla, y_mla_c, o_gla, gr, y_hy, y_hy_c, z_gate,
                        gla_out_norm[l], w_o_mla[l], w_o_gla[l], w_o_hy[l], w_out[l], norm2_g[l], ff_w1[l], ff_w2[l],
                        final_norm_g, n_tiles, n_lat, last)
        x_lat, x_ctx, ctx_blk = xc, xc, n_lat
    return xc
```

```python
import functools
import math

import numpy as np
import jax
import jax.numpy as jnp
from jax import lax
from jax.experimental import pallas as pl
from jax.experimental.pallas import tpu as pltpu

F32 = jnp.float32
BF16 = jnp.bfloat16
LOG2E = 1.4426950408889634

D_MODEL = 1024
DEPTH = 2
GRID_W = 64
EPS = 1e-6
MLA_HEADS = 8
MLA_NOPE = 64
MLA_ROPE = 32
MLA_V = 64
MLA_Q_LORA = 256
MLA_KV_LORA = 128
MLA_SCALE = (MLA_NOPE + MLA_ROPE) ** -0.5
ROPE_BASE = 10000.0
GLA_HEADS = 4
GLA_DK = 256
GLA_DV = 512
GLA_HK = GLA_DK // GLA_HEADS
GLA_HV = GLA_DV // GLA_HEADS
GLA_GATE_RANK = 16
GLA_TAU = 16.0
HY_WIDTH = 512
HY_ORDER = 2
HY_SHORT = 3
HY_BANDS = 16
HY_POS_DIM = 1 + 2 * HY_BANDS
HY_FILTER_HIDDEN = 64
HY_FAST_DECAY = 0.3
HY_SLOW_DECAY = 1.5
HY_DECAY_TARGET = 1e-2
D_FF = 4 * D_MODEL
MLA_OUT = MLA_HEADS * MLA_V
IN_SIZES = (MLA_Q_LORA, MLA_KV_LORA, MLA_ROPE, GLA_DK, GLA_DK, GLA_DV, GLA_DV, GLA_GATE_RANK, GLA_GATE_RANK,
            (HY_ORDER + 1) * HY_WIDTH, 3 * D_MODEL)

LANES = 128
SUBLANES = 8
TM = 256
HEAD_SLOT = 128
V7X_VMEM_BYTES = 64 * 1024 * 1024
VMEM_LIMIT = V7X_VMEM_BYTES * 7 // 8


def _cparams(n_axes, vmem=None):
    return pltpu.CompilerParams(dimension_semantics=("arbitrary",) * n_axes, vmem_limit_bytes=vmem)


def _const_spec(shape):
    nd = len(shape)
    return pl.BlockSpec(shape, lambda *_: (0,) * nd, pipeline_mode=pl.Buffered(1))


def _rms(x):
    return x * lax.rsqrt(jnp.mean(x * x, axis=-1, keepdims=True) + EPS)


def _sigmoid(x):
    return 1.0 / (1.0 + jnp.exp(-x))


def _dot(a, b):
    return jnp.dot(a, b, preferred_element_type=F32)


def _dot_nt(a, b):
    return lax.dot_general(a, b, (((1,), (1,)), ((), ())), preferred_element_type=F32)


def _dot3(a, b):
    a_hi = a.astype(BF16)
    a_lo = (a - a_hi.astype(F32)).astype(BF16)
    b_hi = b.astype(BF16)
    b_lo = (b - b_hi.astype(F32)).astype(BF16)
    m = a.shape[0]
    r = _dot(jnp.concatenate([a_hi, a_lo], axis=0), b_hi)
    return r[:m] + (r[m:] + _dot(a_hi, b_lo))


def _mod_kernel(cc_ref, w_ref, b_ref, o_ref):
    s = cc_ref[...]
    s = s * _sigmoid(s)
    o_ref[0] = _dot3(s, w_ref[0]) + b_ref[0]


def _modulation(cc, ada_w, ada_b):
    tn = 1536
    n6 = ada_w.shape[-1]
    return pl.pallas_call(
        _mod_kernel,
        out_shape=jax.ShapeDtypeStruct((DEPTH, 16, n6), F32),
        grid=(DEPTH, n6 // tn),
        in_specs=[
            pl.BlockSpec((16, D_MODEL), lambda l, j: (0, 0)),
            pl.BlockSpec((1, D_MODEL, tn), lambda l, j: (l, 0, j)),
            pl.BlockSpec((1, 1, tn), lambda l, j: (l, 0, j)),
        ],
        out_specs=pl.BlockSpec((1, 16, tn), lambda l, j: (l, 0, j)),
        compiler_params=_cparams(2),
        name="modulation",
    )(cc, ada_w, ada_b.reshape(DEPTH, 1, n6))


W_A = 768
W_G = 2 * GLA_DK + 2 * GLA_DV


def _token_specs(d, n_lat_tiles, ctx_blk):
    return [pl.BlockSpec((1, TM, d), lambda b, i: (b, jnp.minimum(i, n_lat_tiles - 1), 0)),
            pl.BlockSpec((1, TM, d), lambda b, i: (b, ctx_blk, 0))]


def _token_tile(x_ref, c_ref, n_lat_tiles):
    if n_lat_tiles is None:
        return x_ref[0]
    return jnp.where(pl.program_id(1) < n_lat_tiles, x_ref[0], c_ref[0])


def _inproj_kernel(x_ref, c_ref, mod_ref, g1_ref, wa_ref, wg_ref, wh_ref, wz_ref, qn_ref, kvn_ref, wuq_ref, wukv_ref,
                   wa2_ref, ba_ref, cos_ref, sin_ref,
                   q_out, k_out, v_out, gq_out, gk_out, gv_out, gr_out, glog_out, hy_out, gate_out, *, n_lat_tiles):
    x = _token_tile(x_ref, c_ref, n_lat_tiles)
    shift = mod_ref[0, 0:1, :]
    scale = mod_ref[0, 1:2, :]
    h = (_rms(x) * (g1_ref[...] * (1.0 + scale)) + shift).astype(BF16)

    za = _dot(h, wa_ref[...])
    cos = cos_ref[...]
    sin = sin_ref[...]

    cqn = (_rms(za[:, 0:256]) * qn_ref[...]).astype(BF16)
    qab = _dot(cqn, wuq_ref[...])
    nq = MLA_HEADS * HEAD_SLOT
    for hd in range(MLA_HEADS):
        sl = slice(hd * HEAD_SLOT, (hd + 1) * HEAD_SLOT)
        qa = qab[:, hd * HEAD_SLOT:(hd + 1) * HEAD_SLOT]
        qb = qab[:, nq + hd * HEAD_SLOT:nq + (hd + 1) * HEAD_SLOT]
        q_out[0, :, sl] = ((qa * cos + qb * sin) * (MLA_SCALE * LOG2E)).astype(BF16)

    ckvn = (_rms(za[:, 256:384]) * kvn_ref[...]).astype(BF16)
    kv = _dot(ckvn, wukv_ref[...])
    krot = za[:, 384:512] * cos + za[:, 512:640] * sin
    for hd in range(MLA_HEADS):
        sl = slice(hd * HEAD_SLOT, (hd + 1) * HEAD_SLOT)
        k_out[0, :, sl] = (kv[:, sl] + krot).astype(BF16)
    ones_hi = (lax.broadcasted_iota(jnp.int32, (1, HEAD_SLOT), 1) >= MLA_V).astype(F32)
    for hd in range(MLA_HEADS):
        sl = slice(hd * HEAD_SLOT, (hd + 1) * HEAD_SLOT)
        v_out[0, :, sl] = (kv[:, nq + hd * HEAD_SLOT:nq + (hd + 1) * HEAD_SLOT] + ones_hi).astype(BF16)

    xg = _dot(za[:, 640:768].astype(BF16), wa2_ref[...]) + ba_ref[...]
    glog_out[0] = (jnp.minimum(xg, 0.0) - jnp.log(1.0 + jnp.exp(-jnp.abs(xg)))) * (1.0 / GLA_TAU)

    zg = _dot(h, wg_ref[...])
    gq_out[0] = (zg[:, 0:GLA_DK] * (GLA_HK ** -0.5)).astype(BF16)
    gk_out[0] = zg[:, GLA_DK:2 * GLA_DK].astype(BF16)
    gv_out[0] = zg[:, 2 * GLA_DK:2 * GLA_DK + GLA_DV].astype(BF16)
    gr = zg[:, 2 * GLA_DK + GLA_DV:]
    gr_out[0] = (gr * _sigmoid(gr)).astype(BF16)

    hy_out[0] = _dot(h, wh_ref[...])
    gate_out[0] = _sigmoid(_dot(h, wz_ref[...])).astype(BF16)


def _rope_partner(w):
    a = MLA_ROPE // 4
    perm = np.concatenate([np.arange(a, 2 * a), np.arange(0, a), np.arange(3 * a, 4 * a), np.arange(2 * a, 3 * a)])
    sign = np.concatenate([-np.ones(a), np.ones(a), -np.ones(a), np.ones(a)]).astype(np.float32)
    return w[:, perm] * sign


def _prep_inproj_weights(w_in, mla_w_uq, mla_w_ukv, gla_w_a2, gla_b_a):
    w_in = w_in.astype(BF16)
    offs = np.concatenate([[0], np.cumsum(IN_SIZES)])
    seg = [w_in[:, offs[i]:offs[i + 1]] for i in range(len(IN_SIZES))]
    w_cq, w_ckv, w_kr, w_gq, w_gk, w_gv, w_gr, w_af, w_ab, w_hy, w_gate = seg
    d = w_in.shape[0]
    z = lambda n: jnp.zeros((d, n), w_in.dtype)
    kr_tile = jnp.concatenate([z(MLA_NOPE), w_kr, z(HEAD_SLOT - MLA_NOPE - MLA_ROPE)], axis=1)
    krp_tile = jnp.concatenate([z(MLA_NOPE), _rope_partner(w_kr), z(HEAD_SLOT - MLA_NOPE - MLA_ROPE)], axis=1)
    a_tile = jnp.concatenate([w_af, w_ab, z(LANES - 2 * GLA_GATE_RANK)], axis=1)
    wa = jnp.concatenate([w_cq, w_ckv, kr_tile, krp_tile, a_tile], axis=1)
    wg = jnp.concatenate([w_gq, w_gk, w_gv, w_gr], axis=1)

    dh = MLA_NOPE + MLA_ROPE
    zq = lambda n: jnp.zeros((MLA_Q_LORA, n), w_in.dtype)
    plain, partner = [], []
    for hd in range(MLA_HEADS):
        blk = mla_w_uq[:, hd * dh:(hd + 1) * dh]
        plain += [blk, zq(HEAD_SLOT - dh)]
        partner += [zq(MLA_NOPE), _rope_partner(blk[:, MLA_NOPE:]), zq(HEAD_SLOT - dh)]
    wuq = jnp.concatenate(plain + partner, axis=1)

    zk = jnp.zeros((MLA_KV_LORA, HEAD_SLOT - MLA_NOPE), w_in.dtype)
    kcols, vcols = [], []
    for hd in range(MLA_HEADS):
        blk = mla_w_ukv[:, hd * (MLA_NOPE + MLA_V):(hd + 1) * (MLA_NOPE + MLA_V)]
        kcols += [blk[:, :MLA_NOPE], zk]
        vcols += [blk[:, MLA_NOPE:], zk]
    wukv = jnp.concatenate(kcols + vcols, axis=1)

    wa2 = jnp.zeros((LANES, 2 * GLA_DK), gla_w_a2.dtype)
    wa2 = wa2.at[0:GLA_GATE_RANK, 0:GLA_DK].set(gla_w_a2[0])
    wa2 = wa2.at[GLA_GATE_RANK:2 * GLA_GATE_RANK, GLA_DK:].set(gla_w_a2[1])
    ba = jnp.concatenate([gla_b_a[0], gla_b_a[1]])[None, :]
    bf = lambda t: t.astype(BF16)
    return bf(wa), bf(wg), bf(w_hy), bf(w_gate), bf(wuq), bf(wukv), bf(wa2), ba


def _rope_tables(seq, ctx_len):
    rows = seq // GRID_W
    row = np.repeat(np.arange(rows, dtype=np.float64), GRID_W)
    col = np.tile(np.arange(GRID_W, dtype=np.float64), rows)
    a = MLA_ROPE // 4
    inv = ROPE_BASE ** (-np.arange(a, dtype=np.float64) / a)
    ang_r = row[:, None] * inv
    ang_c = col[:, None] * inv
    cos32 = np.concatenate([np.cos(ang_r), np.cos(ang_r), np.cos(ang_c), np.cos(ang_c)], axis=1)
    sin32 = np.concatenate([np.sin(ang_r), np.sin(ang_r), np.sin(ang_c), np.sin(ang_c)], axis=1)
    pad_r = HEAD_SLOT - MLA_NOPE - MLA_ROPE
    cos = np.concatenate([np.ones((seq, MLA_NOPE)), cos32, np.zeros((seq, pad_r))], axis=1)
    sin = np.concatenate([np.zeros((seq, MLA_NOPE)), sin32, np.zeros((seq, pad_r))], axis=1)
    cos_c = np.concatenate([np.ones((ctx_len, MLA_NOPE + MLA_ROPE)), np.zeros((ctx_len, pad_r))], axis=1)
    sin_c = np.zeros((ctx_len, HEAD_SLOT))
    return (jnp.asarray(np.concatenate([cos, cos_c], axis=0), F32),
            jnp.asarray(np.concatenate([sin, sin_c], axis=0), F32))


def _inproj(x_lat, x_ctx, ctx_blk, mod_l, g1, weights, q_norm, kv_norm, cos, sin, n_lat_tiles):
    bsz, _, d = x_lat.shape
    nt = n_lat_tiles + 1
    t = nt * TM
    wa, wg, wh, wz, wuq, wukv, wa2, ba = weights
    tile = lambda w: pl.BlockSpec((1, TM, w), lambda b, i: (b, i, 0))
    row = lambda w: pl.BlockSpec((1, w), lambda b, i: (0, 0))
    mod_spec = pl.BlockSpec((1, 6, d), lambda b, i: (jnp.where(i < n_lat_tiles, b, bsz), 0, 0))
    tab = pl.BlockSpec((TM, HEAD_SLOT), lambda b, i: (i, 0))
    nq = MLA_HEADS * HEAD_SLOT
    sds = lambda w, dt: jax.ShapeDtypeStruct((bsz, t, w), dt)
    out_shape = (sds(nq, BF16), sds(nq, BF16), sds(nq, BF16), sds(GLA_DK, BF16), sds(GLA_DK, BF16),
                 sds(GLA_DV, BF16), sds(GLA_DV, BF16), sds(2 * GLA_DK, F32), sds(3 * HY_WIDTH, F32),
                 sds(3 * D_MODEL, BF16))
    out_specs = (tile(nq), tile(nq), tile(nq), tile(GLA_DK), tile(GLA_DK), tile(GLA_DV), tile(GLA_DV),
                 tile(2 * GLA_DK), tile(3 * HY_WIDTH), tile(3 * D_MODEL))
    return pl.pallas_call(
        functools.partial(_inproj_kernel, n_lat_tiles=n_lat_tiles),
        out_shape=out_shape,
        grid=(bsz, nt),
        in_specs=_token_specs(d, n_lat_tiles, ctx_blk) + [
            mod_spec, row(d), _const_spec(wa.shape), _const_spec(wg.shape), _const_spec(wh.shape),
            _const_spec(wz.shape), row(MLA_Q_LORA), row(MLA_KV_LORA), _const_spec(wuq.shape),
            _const_spec(wukv.shape), _const_spec(wa2.shape), row(2 * GLA_DK), tab, tab],
        out_specs=out_specs,
        compiler_params=_cparams(2, VMEM_LIMIT),
        name="inproj",
    )(x_lat, x_ctx, mod_l, g1[None, :], wa, wg, wh, wz, q_norm[None, :], kv_norm[None, :], wuq, wukv, wa2, ba,
      cos, sin)


ATT_TK = 1024


ATT_TQ = 1024
ATT_HEADS = 2


def _attn_kernel(q_ref, k_ref, v_ref, o_ref, m_ref, acc_ref, *, chunks):
    tq = q_ref.shape[1]
    n_heads = q_ref.shape[2] // HEAD_SLOT
    m_ref[...] = jnp.full(m_ref.shape, -jnp.inf, F32)
    acc_ref[...] = jnp.zeros(acc_ref.shape, F32)
    for r0, size in chunks:
        k = k_ref[0, pl.ds(r0, size), :]
        v = v_ref[0, pl.ds(r0, size), :]
        for hd in range(n_heads):
            sl = slice(hd * HEAD_SLOT, (hd + 1) * HEAD_SLOT)
            s = _dot_nt(q_ref[0, :, sl], k[:, sl])
            m_prev = m_ref[hd]
            m_new = jnp.maximum(m_prev, jnp.max(s, axis=1, keepdims=True))
            p = jnp.exp2((s - jnp.concatenate([m_new] * (size // LANES), axis=1)).astype(BF16))
            acc_ref[hd] = jnp.exp2(m_prev - m_new) * acc_ref[hd] + _dot(p, v[:, sl])
            m_ref[hd] = m_new
    lane = lax.broadcasted_iota(jnp.int32, (tq, HEAD_SLOT), 1)
    for pr in range(n_heads // 2):
        a0 = acc_ref[2 * pr]
        a1 = acc_ref[2 * pr + 1]
        o0 = a0 / pltpu.roll(a0, MLA_V, 1)
        o1 = pltpu.roll(a1, MLA_V, 1) / a1
        o_ref[0, :, pr * HEAD_SLOT:(pr + 1) * HEAD_SLOT] = jnp.where(lane < MLA_V, o0, o1).astype(o_ref.dtype)


def _attention(q, k, v, seq, with_ctx_queries):
    bsz, t, _ = q.shape
    ctx_len = t - seq
    assert seq % ATT_TK == 0 and seq % ATT_TQ == 0 and seq % ctx_len == 0 and 2 * MLA_V == HEAD_SLOT
    n_chunks = seq // ATT_TK
    chunks = tuple((j * ATT_TK, ATT_TK) for j in range(n_chunks - 1))
    chunks += (((n_chunks - 1) * ATT_TK, ATT_TK + ctx_len),)
    scratch = lambda nh, tq: [pltpu.VMEM((nh, tq, LANES), F32), pltpu.VMEM((nh, tq, HEAD_SLOT), F32)]
    hps = ATT_HEADS
    y = pl.pallas_call(
        functools.partial(_attn_kernel, chunks=chunks),
        out_shape=jax.ShapeDtypeStruct((bsz, seq, MLA_OUT), BF16),
        grid=(bsz, MLA_HEADS // hps, seq // ATT_TQ),
        in_specs=[
            pl.BlockSpec((1, ATT_TQ, hps * HEAD_SLOT), lambda b, hp, i: (b, i, hp)),
            pl.BlockSpec((1, t, hps * HEAD_SLOT), lambda b, hp, i: (b, 0, hp)),
            pl.BlockSpec((1, t, hps * HEAD_SLOT), lambda b, hp, i: (b, 0, hp)),
        ],
        out_specs=pl.BlockSpec((1, ATT_TQ, hps * MLA_V), lambda b, hp, i: (b, i, hp)),
        scratch_shapes=scratch(hps, ATT_TQ),
        compiler_params=_cparams(3, VMEM_LIMIT),
        name="mla_attention",
    )(q, k, v)
    if not with_ctx_queries:
        return y, None
    cblk = seq // ctx_len
    ctx_rows = pl.BlockSpec((1, ctx_len, MLA_HEADS * HEAD_SLOT), lambda b: (b, cblk, 0))
    y_ctx = pl.pallas_call(
        functools.partial(_attn_kernel, chunks=((0, ctx_len),)),
        out_shape=jax.ShapeDtypeStruct((bsz, ctx_len, MLA_OUT), BF16),
        grid=(bsz,),
        in_specs=[ctx_rows, ctx_rows, ctx_rows],
        out_specs=pl.BlockSpec((1, ctx_len, MLA_OUT), lambda b: (b, 0, 0)),
        scratch_shapes=scratch(MLA_HEADS, ctx_len),
        compiler_params=_cparams(1),
        name="mla_attention_ctx",
    )(q, k, v)
    return y, y_ctx


GLA_LEVELS = int(math.log2(TM))
GLA_SAFE_SPAN = 60.0


def _gla_level_matrices():
    i = np.arange(TM)[:, None]
    t = np.arange(TM)[None, :]
    fwd = [(t <= i)]
    for lv in range(GLA_LEVELS):
        m = TM >> (lv + 1)
        lo = (i // m) * m
        later = ((i // m) % 2) == 1
        q_part = later & (t >= lo) & (t <= i)
        k_part = (~later) & (t > i) & (t <= lo + m - 1)
        fwd.append(q_part | k_part)
    fwd = np.concatenate(fwd, axis=0).astype(np.float32)
    nb = 1 + GLA_LEVELS
    bwd = fwd.reshape(nb, TM, TM)[:, ::-1, ::-1].reshape(nb * TM, TM)
    return np.stack([fwd, bwd])


def _gla_kernel(qf_ref, kf_ref, vf_ref, gf_ref, qb_ref, kb_ref, vb_ref, gb_ref, mall_ref, of_ref, ob_ref, s_ref, a_ref):
    step = pl.program_id(1)

    @pl.when(step == 0)
    def _():
        s_ref[...] = jnp.zeros(s_ref.shape, F32)

    ins = ((qf_ref, kf_ref, vf_ref, gf_ref), (qb_ref, kb_ref, vb_ref, gb_ref))
    outs = (of_ref, ob_ref)
    qs, ks, vs, g2s, g_cums, g_tots = [], [], [], [], [], []
    for d in range(2):
        q_ref, k_ref, v_ref, g_ref = ins[d]
        qs.append(q_ref[0].astype(F32))
        ks.append(k_ref[0].astype(F32))
        vs.append(v_ref[0])
        g = g_ref[0]
        g2 = jnp.concatenate([g.astype(BF16), (g - g.astype(BF16).astype(F32)).astype(BF16)], axis=1)
        e2 = _dot(mall_ref[d, 0:TM, :], g2)
        g2s.append(g2)
        g_cums.append(e2[:, :GLA_DK] + e2[:, GLA_DK:])
        g_tots.append(jnp.sum(g, axis=0, keepdims=True))

    row = lax.broadcasted_iota(jnp.int32, (TM, TM), 0)
    col = lax.broadcasted_iota(jnp.int32, (TM, TM), 1)
    lane_head = lax.broadcasted_iota(jnp.int32, (TM, GLA_DK), 1) // GLA_HK
    tok = lax.broadcasted_iota(jnp.int32, (TM, GLA_DK), 0)
    eye = row == col

    def stack_heads(t):
        return jnp.concatenate([jnp.where(lane_head == hd, t, 0.0) for hd in range(GLA_HEADS)], axis=0).astype(BF16)

    span = jnp.maximum(jnp.max(-g_tots[0]), jnp.max(-g_tots[1]))

    @pl.when(span < GLA_SAFE_SPAN)
    def _():
        for d in range(2):
            res = _dot_nt(stack_heads(qs[d] * jnp.exp(g_cums[d])), (ks[d] * jnp.exp(-g_cums[d])).astype(BF16))
            seen = (col <= row) if d == 0 else (col >= row)
            for hd in range(GLA_HEADS):
                a_ref[d, hd] = jnp.where(seen, res[hd * TM:(hd + 1) * TM], 0.0)

    @pl.when(span >= GLA_SAFE_SPAN)
    def _():
        for d in range(2):
            q, k = qs[d], ks[d]
            e2l = _dot(mall_ref[d, TM:, :], g2s[d])
            e_lv = e2l[:, :GLA_DK] + e2l[:, GLA_DK:]
            res = _dot_nt(stack_heads(q), k.astype(BF16))
            for hd in range(GLA_HEADS):
                a_ref[d, hd] = jnp.where(eye, res[hd * TM:(hd + 1) * TM], 0.0)
            for lv in range(GLA_LEVELS):
                m = TM >> (lv + 1)
                w = jnp.exp(e_lv[lv * TM:(lv + 1) * TM])
                bit = (tok // m) % 2
                q_act = bit != d
                qt = jnp.where(q_act, q * w, 0.0)
                kt = jnp.where(q_act, 0.0, k * w).astype(BF16)
                res = _dot_nt(stack_heads(qt), kt)
                if m == TM // 2:
                    for hd in range(GLA_HEADS):
                        a_ref[d, hd] += res[hd * TM:(hd + 1) * TM]
                else:
                    same = (row // (2 * m)) == (col // (2 * m))
                    for hd in range(GLA_HEADS):
                        a_ref[d, hd] += jnp.where(same, res[hd * TM:(hd + 1) * TM], 0.0)

    same_head = (lax.broadcasted_iota(jnp.int32, (GLA_DK, GLA_DV), 0) // GLA_HK
                 == lax.broadcasted_iota(jnp.int32, (GLA_DK, GLA_DV), 1) // GLA_HV)
    for d in range(2):
        q, k, v, g_cum, g_tot = qs[d], ks[d], vs[d], g_cums[d], g_tots[d]
        s_old = s_ref[d]
        o_inter = _dot((q * jnp.exp(g_cum)).astype(BF16), s_old.astype(BF16))
        for hd in range(GLA_HEADS):
            sl = slice(hd * GLA_HV, (hd + 1) * GLA_HV)
            o_intra = _dot(a_ref[d, hd].astype(BF16), v[:, sl])
            outs[d][0, :, sl] = (o_intra + o_inter[:, sl]).astype(outs[d].dtype)

        kdec_t = (k * jnp.exp(g_tot - g_cum)).T.astype(BF16)
        upd = _dot(kdec_t, v)
        dec_col = jnp.sum(jnp.where(eye, jnp.broadcast_to(jnp.exp(g_tot), (TM, GLA_DK)), 0.0), axis=1,
                          keepdims=True)
        s_ref[d] = dec_col * s_old + jnp.where(same_head, upd, 0.0)


def _gla(gq, gk, gv, glog, n_lat_tiles):
    assert GLA_DK == TM
    bsz, t, _ = gq.shape
    nt = t // TM
    mall = jnp.asarray(_gla_level_matrices(), dtype=BF16)

    def tile_idx(d, s):
        return jnp.where(s == 0, n_lat_tiles, s - 1 if d == 0 else n_lat_tiles - s)

    def specs(d):
        tok = lambda w: pl.BlockSpec((1, TM, w), lambda b, s: (b, tile_idx(d, s), 0))
        return [tok(GLA_DK), tok(GLA_DK), tok(GLA_DV), pl.BlockSpec((1, TM, GLA_DK), lambda b, s: (b, tile_idx(d, s), d))]

    out_spec = lambda d: pl.BlockSpec((1, TM, GLA_DV), lambda b, s: (b, tile_idx(d, s), 0))
    sds = jax.ShapeDtypeStruct((bsz, t, GLA_DV), BF16)
    return pl.pallas_call(
        _gla_kernel,
        out_shape=(sds, sds),
        grid=(bsz, nt),
        in_specs=specs(0) + specs(1) + [_const_spec(mall.shape)],
        out_specs=(out_spec(0), out_spec(1)),
        scratch_shapes=[pltpu.VMEM((2, GLA_DK, GLA_DV), F32), pltpu.VMEM((2, GLA_HEADS, TM, TM), F32)],
        compiler_params=_cparams(2),
        name="gla_scan",
    )(gq, gk, gv, glog, gq, gk, gv, glog, mall)


HY_N1 = 64
HY_N2 = 128
HY_SLAB = SUBLANES
HY_CB = 256
HY_KG = 16
HY_SPEC_CB = 128
HY_UNROLL = 16


def _hyena_dft_constants(seq):
    n = 2 * seq
    assert n == HY_N1 * HY_N2
    nh = HY_N1 // 2
    kh = HY_N2 // 2
    eye = np.eye(HY_SLAB)
    k1 = np.arange(HY_N1)
    th = 2 * np.pi * np.outer(k1 + 0.5, np.arange(nh)) / HY_N1
    fwd1 = np.concatenate([np.kron(np.cos(th), eye), np.kron(-np.sin(th), eye)], axis=0)
    inv1 = (2.0 / n) * np.concatenate([np.kron(np.cos(th).T, eye), np.kron(-np.sin(th).T, eye)], axis=1)
    nlo = np.arange(HY_N2)
    k2 = np.arange(kh)
    ph = 2 * np.pi * (k2[None, :, None] * nlo[None, None, :] / HY_N2
                      + (k1[:, None, None] + 0.5) * nlo[None, None, :] / n)
    c, s = np.cos(ph), np.sin(ph)
    fwd2 = np.concatenate([np.concatenate([c, s], axis=2), np.concatenate([-s, c], axis=2)], axis=1)
    ct, st = c.transpose(0, 2, 1), s.transpose(0, 2, 1)
    inv2 = np.concatenate([np.concatenate([ct, -st], axis=2), np.concatenate([st, ct], axis=2)], axis=1)
    return fwd1, fwd2, inv2, inv1


def _hyena_ctx_dft_constants(ctx_len):
    n = 2 * ctx_len
    th = 2 * np.pi * np.outer(np.arange(ctx_len) + 0.5, np.arange(ctx_len)) / n
    fwd = np.concatenate([np.cos(th), -np.sin(th)], axis=0)
    inv = (2.0 / n) * np.concatenate([np.cos(th).T, -np.sin(th).T], axis=1)
    return fwd, inv


def _hyena_features(length):
    pos = np.arange(length, dtype=np.float64)
    t = pos / max(length - 1, 1)
    f = np.linspace(1e-4, HY_BANDS - 1, HY_BANDS)
    ang = (2.0 * math.pi / length) * pos[:, None] * f
    feat = np.concatenate([t[:, None], np.cos(ang), np.sin(ang)], axis=-1)
    return jnp.asarray(np.pad(feat, ((0, 0), (0, LANES - HY_POS_DIM))), F32)


def _hy_filter_kernel(feat_ref, w1_ref, b1_ref, w2_ref, b2_ref, w3_ref, b3_ref, absd_ref, h_ref, s_ref):
    i = pl.program_id(0)
    feat = feat_ref[...]
    hp = _dot3
    hdn = jnp.sin(hp(feat, w1_ref[...]) + b1_ref[...])
    hdn = jnp.sin(hp(hdn, w2_ref[...]) + b2_ref[...])
    h = hp(hdn, w3_ref[...]) + b3_ref[...]
    window = jnp.exp(-feat[:, 0:1] * absd_ref[...])
    h = h * jnp.concatenate([window] * (2 * HY_ORDER), axis=1)
    h_ref[...] = h

    @pl.when(i == 0)
    def _():
        s_ref[...] = jnp.zeros(s_ref.shape, F32)

    s_ref[...] += jnp.sum(jnp.abs(h), axis=0, keepdims=True)


def _hyena_filters_raw(length, filt_w):
    w1, b1, w2, b2, w3, b3 = filt_w
    nf = 2 * HY_ORDER * HY_WIDTH
    tr = min(length, 512)
    deltas = np.linspace(math.log(HY_DECAY_TARGET) / HY_FAST_DECAY, math.log(HY_DECAY_TARGET) / HY_SLOW_DECAY,
                         HY_WIDTH, dtype=np.float32)
    absd = jnp.asarray(np.abs(deltas))[None, :]
    w1p = jnp.pad(w1, ((0, LANES - HY_POS_DIM), (0, 0)))
    full = lambda shp: pl.BlockSpec(shp, lambda i: (0,) * len(shp))
    return pl.pallas_call(
        _hy_filter_kernel,
        out_shape=(jax.ShapeDtypeStruct((length, nf), F32), jax.ShapeDtypeStruct((1, nf), F32)),
        grid=(length // tr,),
        in_specs=[pl.BlockSpec((tr, LANES), lambda i: (i, 0)), full((LANES, HY_FILTER_HIDDEN)),
                  full((1, HY_FILTER_HIDDEN)), full((HY_FILTER_HIDDEN, HY_FILTER_HIDDEN)), full((1, HY_FILTER_HIDDEN)),
                  full((HY_FILTER_HIDDEN, nf)), full((1, nf)), full((1, HY_WIDTH))],
        out_specs=(pl.BlockSpec((tr, nf), lambda i: (i, 0)), full((1, nf))),
        compiler_params=_cparams(1),
        name="hyena_filter_mlp",
    )(_hyena_features(length), w1p, b1[None, :], w2, b2[None, :], w3, b3[None, :], absd)


def _dot_split(m, x):
    x_hi = x.astype(BF16)
    x_lo = (x - x_hi.astype(F32)).astype(BF16)
    n = x.shape[1]
    r = _dot(m, jnp.concatenate([x_hi, x_lo], axis=1))
    return r[:, :n] + r[:, n:]


def _odft_stage1(src_at, mm, s_re, s_im, unroll=2):
    nk = s_re.shape[0]
    half = nk * HY_SLAB

    def body(j, carry):
        r0 = pl.multiple_of(j * HY_SLAB, HY_SLAB)
        slab = src_at(r0)
        cb = slab.shape[-1]
        res = mm(slab.reshape(-1, cb))
        s_re[:, pl.ds(r0, HY_SLAB), :] = res[:half].reshape(nk, HY_SLAB, cb)
        s_im[:, pl.ds(r0, HY_SLAB), :] = res[half:].reshape(nk, HY_SLAB, cb)
        return carry

    lax.fori_loop(0, HY_N2 // HY_SLAB, body, 0, unroll=unroll)


def _hy_spectrum_kernel(hf_ref, hb_ref, sf_ref, sb_ref, fwd1_ref, fwd2_ref, o_ref, s_re, s_im):
    kh = HY_N2 // 2
    mm1 = lambda x: _dot_split(fwd1_ref[...], x)

    def middle(sign):
        def body(k1, carry):
            a = jnp.concatenate([s_re[k1], s_im[k1]], axis=0)
            x = _dot_split(fwd2_ref[k1], a)
            if sign is None:
                o_ref[0, k1] = x[:kh]
                o_ref[1, k1] = x[kh:]
            else:
                inv_norm = 1.0 / (sf_ref[...] + sb_ref[...])
                o_ref[0, k1] = (o_ref[0, k1] + x[:kh]) * inv_norm
                o_ref[1, k1] = (o_ref[1, k1] - x[kh:]) * inv_norm
            return carry
        lax.fori_loop(0, HY_N1, body, 0, unroll=32)

    _odft_stage1(lambda r0: hf_ref[:, pl.ds(r0, HY_SLAB), :], mm1, s_re, s_im, unroll=16)
    middle(None)

    def bwd_slab(r0):
        slab = hb_ref[:, pl.ds(r0, HY_SLAB), :]
        nhi = lax.broadcasted_iota(jnp.int32, slab.shape, 0)
        r = lax.broadcasted_iota(jnp.int32, slab.shape, 1)
        return jnp.where((nhi == 0) & (r + r0 == 0), 0.0, slab)

    _odft_stage1(bwd_slab, mm1, s_re, s_im, unroll=16)
    middle(-1)


def _hyena_filter_spectrum(seq, w1, b1, w2, b2, w3, b3):
    h_raw, s = _hyena_filters_raw(seq, (w1, b1, w2, b2, w3, b3))
    nh = HY_N1 // 2
    nc = HY_ORDER * HY_WIDTH
    h3 = h_raw.reshape(nh, HY_N2, 2 * nc)
    fwd1, fwd2, _, _ = _hyena_dft_constants(seq)
    fwd1 = jnp.asarray(fwd1, BF16)
    fwd2 = jnp.asarray(fwd2, BF16)
    scb = HY_SPEC_CB
    ncb = nc // scb
    return pl.pallas_call(
        _hy_spectrum_kernel,
        out_shape=jax.ShapeDtypeStruct((2, HY_N1, HY_N2 // 2, nc), F32),
        grid=(ncb,),
        in_specs=[pl.BlockSpec((nh, HY_N2, scb), lambda c: (0, 0, c)),
                  pl.BlockSpec((nh, HY_N2, scb), lambda c: (0, 0, ncb + c)),
                  pl.BlockSpec((1, scb), lambda c: (0, c)),
                  pl.BlockSpec((1, scb), lambda c: (0, ncb + c)),
                  _const_spec(fwd1.shape), _const_spec(fwd2.shape)],
        out_specs=pl.BlockSpec((2, HY_N1, HY_N2 // 2, scb), lambda c: (0, 0, 0, c)),
        scratch_shapes=[pltpu.VMEM((HY_N1, HY_N2, scb), F32), pltpu.VMEM((HY_N1, HY_N2, scb), F32)],
        compiler_params=_cparams(1, VMEM_LIMIT),
        name="hyena_filter_spectrum",
    )(h3, h3, s, s, fwd1, fwd2)


def _short_conv_chunk(ref, c, n_chunks, w_ref, b_ref):
    per = TM // HY_N2
    cur = ref[0, pl.ds(per * c, per)]
    cb = cur.shape[-1]
    cur = cur.reshape(TM, cb)
    prev = ref[0, jnp.maximum(per * c - 1, 0), pl.ds(HY_N2 - SUBLANES, SUBLANES), :][SUBLANES - 1:SUBLANES]
    nxt = ref[0, jnp.minimum(per * c + per, per * n_chunks - 1), pl.ds(0, SUBLANES), :][0:1]
    prev = jnp.where(c > 0, prev, 0.0)
    nxt = jnp.where(c < n_chunks - 1, nxt, 0.0)
    rowi = lax.broadcasted_iota(jnp.int32, (TM, cb), 0)
    dn = jnp.where(rowi == 0, prev, pltpu.roll(cur, 1, 0))
    up = jnp.where(rowi == TM - 1, nxt, pltpu.roll(cur, TM - 1, 0))
    return b_ref[...] + w_ref[0:1, :] * dn + w_ref[1:2, :] * cur + w_ref[2:3, :] * up


def _hy_conv_kernel(*refs, conv_y):
    if conv_y:
        (y_ref, g_ref, h_ref, bias_ref, wy_ref, by_ref, wg_ref, bg_ref, fwd1_ref, fwd2_ref, inv2_ref, inv1_ref,
         o_ref, s_re, s_im, gs_ref, us_ref) = refs
    else:
        (y_ref, g_ref, h_ref, bias_ref, wg_ref, bg_ref, fwd1_ref, fwd2_ref, inv2_ref, inv1_ref,
         o_ref, s_re, s_im, gs_ref) = refs
    grp = pl.program_id(2)
    last = pl.num_programs(2) - 1
    nh = HY_N1 // 2
    per = TM // HY_N2
    n_chunks = nh // per
    cb = o_ref.shape[-1]
    kh = HY_N2 // 2

    @pl.when(grp == 0)
    def _():
        def pre(c, carry):
            gs_ref[pl.ds(per * c, per)] = _short_conv_chunk(g_ref, c, n_chunks, wg_ref, bg_ref).reshape(per, HY_N2, cb)
            if conv_y:
                us_ref[pl.ds(per * c, per)] = _short_conv_chunk(y_ref, c, n_chunks, wy_ref, by_ref).reshape(
                    per, HY_N2, cb)
            return carry
        lax.fori_loop(0, n_chunks, pre, 0)

    if conv_y:
        u_at = lambda r0: us_ref[:, pl.ds(r0, HY_SLAB), :]
    else:
        u_at = lambda r0: y_ref[0, :, pl.ds(r0, HY_SLAB), :]

    _odft_stage1(u_at, lambda x: _dot(fwd1_ref[0], x.astype(BF16)), s_re, s_im, unroll=HY_UNROLL)

    def middle(k1, carry):
        a = jnp.concatenate([s_re[k1], s_im[k1]], axis=0).astype(BF16)
        x = _dot(fwd2_ref[k1], a)
        xr, xi = x[:kh], x[kh:]
        hr, hi = h_ref[0, k1], h_ref[1, k1]
        y = jnp.concatenate([xr * hr - xi * hi, xr * hi + xi * hr], axis=0).astype(BF16)
        bm = _dot(inv2_ref[k1], y)
        s_re[k1] = bm[:HY_N2]
        s_im[k1] = bm[HY_N2:]
        return carry

    lax.fori_loop(0, HY_KG, middle, 0, unroll=HY_KG)

    def partial_conv(r0):
        slab = jnp.concatenate([s_re[:, pl.ds(r0, HY_SLAB), :].reshape(HY_KG * HY_SLAB, cb),
                                s_im[:, pl.ds(r0, HY_SLAB), :].reshape(HY_KG * HY_SLAB, cb)], axis=0).astype(BF16)
        return _dot(inv1_ref[0], slab).reshape(nh, HY_SLAB, cb)

    def post_loop(fn):
        def post(j, carry):
            r0 = pl.multiple_of(j * HY_SLAB, HY_SLAB)
            o_ref[0, :, pl.ds(r0, HY_SLAB), :] = fn(r0, partial_conv(r0))
            return carry
        lax.fori_loop(0, HY_N2 // HY_SLAB, post, 0, unroll=HY_UNROLL)

    @pl.when(grp == 0)
    def _():
        post_loop(lambda r0, part: part)

    @pl.when((grp > 0) & (grp < last))
    def _():
        post_loop(lambda r0, part: o_ref[0, :, pl.ds(r0, HY_SLAB), :] + part)

    @pl.when(grp == last)
    def _():
        post_loop(lambda r0, part: gs_ref[:, pl.ds(r0, HY_SLAB), :]
                  * (o_ref[0, :, pl.ds(r0, HY_SLAB), :] + part + bias_ref[...] * u_at(r0)))


def _hyena_order(y4, y_col0, z4, gate_col0, hspec, order, hy_bias, short_w, short_b, consts, conv_y):
    bsz = z4.shape[0]
    nh = HY_N1 // 2
    ncb = HY_WIDTH // HY_CB
    ngrp = HY_N1 // HY_KG
    fwd1, fwd2, inv2, inv1 = consts
    blk4 = lambda off: pl.BlockSpec((1, nh, HY_N2, HY_CB), lambda c, b, g: (b, 0, 0, off + c))
    rowspec = lambda rows, off: pl.BlockSpec((rows, HY_CB), lambda c, b, g: (0, off + c))
    grouped = lambda shp: pl.BlockSpec(shp, lambda c, b, g: (g, 0, 0))
    in_specs = [blk4(y_col0), blk4(gate_col0),
                pl.BlockSpec((2, HY_KG, HY_N2 // 2, HY_CB), lambda c, b, g: (0, g, 0, order * ncb + c)),
                rowspec(1, 0)]
    args = [y4, z4, hspec, hy_bias.reshape(1, -1)]
    if conv_y:
        in_specs += [rowspec(HY_SHORT, y_col0), rowspec(1, y_col0)]
        args += [short_w, short_b[None, :]]
    in_specs += [rowspec(HY_SHORT, gate_col0), rowspec(1, gate_col0)]
    args += [short_w, short_b[None, :]]
    in_specs += [grouped((1,) + fwd1.shape[1:]), grouped((HY_KG,) + fwd2.shape[1:]),
                 grouped((HY_KG,) + inv2.shape[1:]), grouped((1,) + inv1.shape[1:])]
    args += [fwd1, fwd2, inv2, inv1]
    scratch = [pltpu.VMEM((HY_KG, HY_N2, HY_CB), F32), pltpu.VMEM((HY_KG, HY_N2, HY_CB), F32),
               pltpu.VMEM((nh, HY_N2, HY_CB), F32)]
    if conv_y:
        scratch.append(pltpu.VMEM((nh, HY_N2, HY_CB), F32))
    return pl.pallas_call(
        functools.partial(_hy_conv_kernel, conv_y=conv_y),
        out_shape=jax.ShapeDtypeStruct((bsz, nh, HY_N2, HY_WIDTH), F32),
        grid=(ncb, bsz, ngrp),
        in_specs=in_specs,
        out_specs=pl.BlockSpec((1, nh, HY_N2, HY_CB), lambda c, b, g: (b, 0, 0, c)),
        scratch_shapes=scratch,
        compiler_params=_cparams(3, VMEM_LIMIT),
        name="hyena_conv%d" % order,
    )(*args)


def _hyena_latent(z_hy, hspec, short_w, short_b, hy_bias):
    bsz, t, _ = z_hy.shape
    seq = HY_N1 * HY_N2 // 2
    fwd1, fwd2, inv2, inv1 = _hyena_dft_constants(seq)
    ngrp = HY_N1 // HY_KG
    rows = HY_KG * HY_SLAB
    fwd1 = fwd1.reshape(2, ngrp, rows, -1).transpose(1, 0, 2, 3).reshape(ngrp, 2 * rows, -1)
    inv1 = inv1.reshape(-1, 2, ngrp, rows).transpose(2, 0, 1, 3).reshape(ngrp, -1, 2 * rows)
    consts = tuple(jnp.asarray(m, BF16) for m in (fwd1, fwd2, inv2, inv1))
    z4 = z_hy.reshape(bsz, t // HY_N2, HY_N2, 3 * HY_WIDTH)
    ncb = HY_WIDTH // HY_CB
    y1 = _hyena_order(z4, 2 * ncb, z4, 0, hspec, 0, hy_bias[0], short_w, short_b, consts, True)
    y2 = _hyena_order(y1, 0, z4, ncb, hspec, 1, hy_bias[1], short_w, short_b, consts, False)
    return y2.reshape(bsz, seq, HY_WIDTH)


def _hy_ctx_spectrum_kernel(h_ref, s_ref, fwd_ref, o_ref):
    lc = h_ref.shape[0]
    nc = HY_ORDER * HY_WIDTH
    hp = _dot3
    h = h_ref[...]
    rowi = lax.broadcasted_iota(jnp.int32, (lc, nc), 0)
    xf = hp(fwd_ref[...], h[:, :nc])
    xb = hp(fwd_ref[...], jnp.where(rowi == 0, 0.0, h[:, nc:]))
    inv_norm = 1.0 / (s_ref[:, :nc] + s_ref[:, nc:])
    o_ref[0] = (xf[:lc] + xb[:lc]) * inv_norm
    o_ref[1] = (xf[lc:] - xb[lc:]) * inv_norm


def _hy_ctx_conv_kernel(x1_ref, x2_ref, v_ref, h_ref, bias_ref, w_ref, b_ref, fwd_ref, inv_ref, o_ref):
    lc = o_ref.shape[1]
    rowi = lax.broadcasted_iota(jnp.int32, (lc, HY_WIDTH), 0)

    def short(ref, part):
        cur = ref[0].reshape(lc, HY_WIDTH)
        sl = slice(part * HY_WIDTH, (part + 1) * HY_WIDTH)
        dn = jnp.where(rowi == 0, 0.0, pltpu.roll(cur, 1, 0))
        up = jnp.where(rowi == lc - 1, 0.0, pltpu.roll(cur, lc - 1, 0))
        return b_ref[:, sl] + w_ref[0:1, sl] * dn + w_ref[1:2, sl] * cur + w_ref[2:3, sl] * up

    y = short(v_ref, 2)
    for order, gref in enumerate((x1_ref, x2_ref)):
        sl = slice(order * HY_WIDTH, (order + 1) * HY_WIDTH)
        x = _dot(fwd_ref[...], y.astype(BF16))
        xr, xi = x[:lc], x[lc:]
        hr, hi = h_ref[0, :, sl], h_ref[1, :, sl]
        prod = jnp.concatenate([xr * hr - xi * hi, xr * hi + xi * hr], axis=0).astype(BF16)
        conv = _dot(inv_ref[...], prod)
        y = short(gref, order) * (conv + bias_ref[order:order + 1, :] * y)
    o_ref[0] = y


def _hyena_ctx(z_hy, filt_w, short_w, short_b, hy_bias):
    bsz, t, _ = z_hy.shape
    seq = HY_N1 * HY_N2 // 2
    lc = t - seq
    per = lc // HY_N2
    h_raw, s = _hyena_filters_raw(lc, filt_w)
    fwd, inv = _hyena_ctx_dft_constants(lc)
    nc = HY_ORDER * HY_WIDTH
    full = lambda shp: pl.BlockSpec(shp, lambda *_: (0,) * len(shp))
    hspec = pl.pallas_call(
        _hy_ctx_spectrum_kernel,
        out_shape=jax.ShapeDtypeStruct((2, lc, nc), F32),
        grid=(1,),
        in_specs=[full(h_raw.shape), full(s.shape), full(fwd.shape)],
        out_specs=full((2, lc, nc)),
        compiler_params=_cparams(1),
        name="hyena_ctx_spectrum",
    )(h_raw, s, jnp.asarray(fwd, F32))
    z4 = z_hy.reshape(bsz, t // HY_N2, HY_N2, 3 * HY_WIDTH)
    blk = lambda part: pl.BlockSpec((1, per, HY_N2, HY_WIDTH), lambda b: (b, seq // lc, 0, part))
    return pl.pallas_call(
        _hy_ctx_conv_kernel,
        out_shape=jax.ShapeDtypeStruct((bsz, lc, HY_WIDTH), F32),
        grid=(bsz,),
        in_specs=[blk(0), blk(1), blk(2), full((2, lc, nc)), full((HY_ORDER, HY_WIDTH)),
                  full((HY_SHORT, 3 * HY_WIDTH)), full((1, 3 * HY_WIDTH)), full(fwd.shape), full(inv.shape)],
        out_specs=pl.BlockSpec((1, lc, HY_WIDTH), lambda b: (b, 0, 0)),
        compiler_params=_cparams(1),
        name="hyena_ctx_conv",
    )(z4, z4, z4, hspec, hy_bias, short_w, short_b[None, :], jnp.asarray(fwd, BF16), jnp.asarray(inv, BF16))


FF_CHUNK = 1024


def _merge_mlp_kernel(x_ref, c_ref, mod_ref, ymla_ref, ymlac_ref, of_ref, ob_ref, gr_ref, yhy_ref, yhyc_ref, zg_ref, on_ref,
                      wm_ref, wgl_ref, wh_ref, wo_ref, g2_ref, w1_ref, w2_ref, fg_ref, o_ref, *, final, n_lat_tiles):
    o = of_ref[0].astype(F32) + ob_ref[0].astype(F32)
    silu = gr_ref[0].astype(F32)
    parts = []
    for hd in range(GLA_HEADS):
        sl = slice(hd * GLA_HV, (hd + 1) * GLA_HV)
        parts.append((_rms(o[:, sl]) * on_ref[...] * silu[:, sl]).astype(BF16))
    y_gla = jnp.concatenate(parts, axis=1)
    zg = zg_ref[0].astype(F32)
    d = x_ref.shape[-1]
    m = zg[:, 0:d] * _dot(_token_tile(ymla_ref, ymlac_ref, n_lat_tiles), wm_ref[...])
    m = m + zg[:, d:2 * d] * _dot(y_gla, wgl_ref[...])
    m = m + zg[:, 2 * d:3 * d] * _dot(_token_tile(yhy_ref, yhyc_ref, n_lat_tiles).astype(BF16), wh_ref[...])
    out = _dot(m.astype(BF16), wo_ref[...])
    x = _token_tile(x_ref, c_ref, n_lat_tiles) + mod_ref[0, 2:3, :] * out

    h = (_rms(x) * (g2_ref[...] * (1.0 + mod_ref[0, 4:5, :])) + mod_ref[0, 3:4, :]).astype(BF16)
    acc = jnp.zeros(x.shape, F32)
    for j in range(w1_ref.shape[1] // FF_CHUNK):
        a = jnp.maximum(_dot(h, w1_ref[:, j * FF_CHUNK:(j + 1) * FF_CHUNK]), 0.0)
        acc = acc + _dot((a * a).astype(BF16), w2_ref[j * FF_CHUNK:(j + 1) * FF_CHUNK, :])
    xn = x + mod_ref[0, 5:6, :] * acc
    if final:
        xn = _rms(xn) * fg_ref[...]
    o_ref[0] = xn


def _mod_spec(d, n_lat_tiles, bsz):
    return pl.BlockSpec((1, 6, d), lambda b, i: (jnp.where(i < n_lat_tiles, b, bsz), 0, 0))


def _merge_mlp(x_lat, x_ctx, ctx_blk, mod_l, y_mla, y_mla_ctx, o_gla, gr, y_hy, y_hy_ctx, gate, out_norm,
               w_o_mla, w_o_gla, w_o_hy, w_out, g2, w1, w2, final_g, n_tiles, n_lat_tiles, final):
    bsz, _, d = x_lat.shape
    if y_mla_ctx is None:
        y_mla_ctx, y_hy_ctx = y_mla, y_hy
    tile = lambda w: pl.BlockSpec((1, TM, w), lambda b, i: (b, i, 0))
    row = lambda w: pl.BlockSpec((1, w), lambda b, i: (0, 0))
    bf = lambda w: w.astype(BF16)
    return pl.pallas_call(
        functools.partial(_merge_mlp_kernel, final=final, n_lat_tiles=n_lat_tiles if n_tiles > n_lat_tiles else None),
        out_shape=jax.ShapeDtypeStruct((bsz, n_tiles * TM, d), F32),
        grid=(bsz, n_tiles),
        in_specs=_token_specs(d, n_lat_tiles, ctx_blk) + [_mod_spec(d, n_lat_tiles, bsz)]
        + _token_specs(MLA_OUT, n_lat_tiles, 0) + [tile(GLA_DV), tile(GLA_DV), tile(GLA_DV)]
        + _token_specs(HY_WIDTH, n_lat_tiles, 0) + [
            tile(3 * d), row(GLA_HV),
            _const_spec(w_o_mla.shape), _const_spec(w_o_gla.shape), _const_spec(w_o_hy.shape),
            _const_spec(w_out.shape), row(d), _const_spec(w1.shape), _const_spec(w2.shape), row(d)],
        out_specs=tile(d),
        compiler_params=_cparams(2, VMEM_LIMIT),
        name="merge_mlp",
    )(x_lat, x_ctx, mod_l, y_mla, y_mla_ctx, o_gla[0], o_gla[1], gr, y_hy, y_hy_ctx, gate, out_norm[None, :],
      bf(w_o_mla), bf(w_o_gla), bf(w_o_hy), bf(w_out), g2[None, :], bf(w1), bf(w2), final_g[None, :])


def kernel(x, c, ctx, c_ctx, ada_w, ada_b, norm1_g, norm2_g, w_in, mla_q_norm, mla_w_uq, mla_kv_norm, mla_w_ukv, gla_w_a2, gla_b_a, gla_out_norm, hy_short_w, hy_short_b, hy_f_w1, hy_f_b1, hy_f_w2, hy_f_b2, hy_f_w3, hy_f_b3, hy_bias, w_o_mla, w_o_gla, w_o_hy, w_out, ff_w1, ff_w2, final_norm_g):
    bsz, seq, d = x.shape
    ctx_len = ctx.shape[1]
    n_lat = seq // TM
    assert ctx_len == TM and seq % TM == 0
    n_all = n_lat + 1
    x_lat, x_ctx, ctx_blk = x, ctx, 0
    cc = jnp.zeros((16, d), F32).at[:bsz].set(c).at[bsz].set(c_ctx)
    mod = _modulation(cc, ada_w, ada_b).reshape(DEPTH, 16, 6, d)
    cos, sin = _rope_tables(seq, ctx_len)
    for l in range(DEPTH):
        last = l == DEPTH - 1
        n_tiles = n_lat if last else n_all
        weights = _prep_inproj_weights(w_in[l], mla_w_uq[l], mla_w_ukv[l], gla_w_a2[l], gla_b_a[l])
        q, k, v, gq, gk, gv, gr, glog, z_hy, z_gate = _inproj(x_lat, x_ctx, ctx_blk, mod[l], norm1_g[l], weights,
                                                             mla_q_norm[l], mla_kv_norm[l], cos, sin, n_lat)
        y_mla, y_mla_c = _attention(q, k, v, seq, not last)
        o_gla = _gla(gq, gk, gv, glog, n_lat)
        filt_w = (hy_f_w1[l], hy_f_b1[l], hy_f_w2[l], hy_f_b2[l], hy_f_w3[l], hy_f_b3[l])
        hspec = _hyena_filter_spectrum(seq, *filt_w)
        y_hy = _hyena_latent(z_hy, hspec, hy_short_w[l], hy_short_b[l], hy_bias[l])
        y_hy_c = None if last else _hyena_ctx(z_hy, filt_w, hy_short_w[l], hy_short_b[l], hy_bias[l])
        xc = _merge_mlp(x_lat, x_ctx, ctx_blk, mod[l], y_mla, y_mla_c, o_gla, gr, y_hy, y_hy_c, z_gate,
                        gla_out_norm[l], w_o_mla[l], w_o_gla[l], w_o_hy[l], w_out[l], norm2_g[l], ff_w1[l], ff_w2[l],
                        final_norm_g, n_tiles, n_lat, last)
        x_lat, x_ctx, ctx_blk = xc, xc, n_lat
    return xc
```

```python
import functools
import math

import numpy as np
import jax
import jax.numpy as jnp
from jax import lax
from jax.experimental import pallas as pl
from jax.experimental.pallas import tpu as pltpu

F32 = jnp.float32
BF16 = jnp.bfloat16
LOG2E = 1.4426950408889634

D_MODEL = 1024
DEPTH = 2
GRID_W = 64
EPS = 1e-6
MLA_HEADS = 8
MLA_NOPE = 64
MLA_ROPE = 32
MLA_V = 64
MLA_Q_LORA = 256
MLA_KV_LORA = 128
MLA_SCALE = (MLA_NOPE + MLA_ROPE) ** -0.5
ROPE_BASE = 10000.0
GLA_HEADS = 4
GLA_DK = 256
GLA_DV = 512
GLA_HK = GLA_DK // GLA_HEADS
GLA_HV = GLA_DV // GLA_HEADS
GLA_GATE_RANK = 16
GLA_TAU = 16.0
HY_WIDTH = 512
HY_ORDER = 2
HY_SHORT = 3
HY_BANDS = 16
HY_POS_DIM = 1 + 2 * HY_BANDS
HY_FILTER_HIDDEN = 64
HY_FAST_DECAY = 0.3
HY_SLOW_DECAY = 1.5
HY_DECAY_TARGET = 1e-2
D_FF = 4 * D_MODEL
MLA_OUT = MLA_HEADS * MLA_V
IN_SIZES = (MLA_Q_LORA, MLA_KV_LORA, MLA_ROPE, GLA_DK, GLA_DK, GLA_DV, GLA_DV, GLA_GATE_RANK, GLA_GATE_RANK,
            (HY_ORDER + 1) * HY_WIDTH, 3 * D_MODEL)

LANES = 128
SUBLANES = 8
TM = 256
HEAD_SLOT = 128
V7X_VMEM_BYTES = 64 * 1024 * 1024
VMEM_LIMIT = V7X_VMEM_BYTES * 7 // 8


def _cparams(n_axes, vmem=None):
    return pltpu.CompilerParams(dimension_semantics=("arbitrary",) * n_axes, vmem_limit_bytes=vmem)


def _const_spec(shape):
    nd = len(shape)
    return pl.BlockSpec(shape, lambda *_: (0,) * nd, pipeline_mode=pl.Buffered(1))


def _rms(x):
    return x * lax.rsqrt(jnp.mean(x * x, axis=-1, keepdims=True) + EPS)


def _sigmoid(x):
    return 1.0 / (1.0 + jnp.exp(-x))


def _dot(a, b):
    return jnp.dot(a, b, preferred_element_type=F32)


def _dot_nt(a, b):
    return lax.dot_general(a, b, (((1,), (1,)), ((), ())), preferred_element_type=F32)


def _dot3(a, b):
    a_hi = a.astype(BF16)
    a_lo = (a - a_hi.astype(F32)).astype(BF16)
    b_hi = b.astype(BF16)
    b_lo = (b - b_hi.astype(F32)).astype(BF16)
    m = a.shape[0]
    r = _dot(jnp.concatenate([a_hi, a_lo], axis=0), b_hi)
    return r[:m] + (r[m:] + _dot(a_hi, b_lo))


def _mod_kernel(cc_ref, w_ref, b_ref, o_ref):
    s = cc_ref[...]
    s = s * _sigmoid(s)
    o_ref[0] = _dot3(s, w_ref[0]) + b_ref[0]


def _modulation(cc, ada_w, ada_b):
    tn = 1536
    n6 = ada_w.shape[-1]
    return pl.pallas_call(
        _mod_kernel,
        out_shape=jax.ShapeDtypeStruct((DEPTH, 16, n6), F32),
        grid=(DEPTH, n6 // tn),
        in_specs=[
            pl.BlockSpec((16, D_MODEL), lambda l, j: (0, 0)),
            pl.BlockSpec((1, D_MODEL, tn), lambda l, j: (l, 0, j)),
            pl.BlockSpec((1, 1, tn), lambda l, j: (l, 0, j)),
        ],
        out_specs=pl.BlockSpec((1, 16, tn), lambda l, j: (l, 0, j)),
        compiler_params=_cparams(2),
        name="modulation",
    )(cc, ada_w, ada_b.reshape(DEPTH, 1, n6))


W_A = 768
W_G = 2 * GLA_DK + 2 * GLA_DV


def _token_specs(d, n_lat_tiles, ctx_blk):
    return [pl.BlockSpec((1, TM, d), lambda b, i: (b, jnp.minimum(i, n_lat_tiles - 1), 0)),
            pl.BlockSpec((1, TM, d), lambda b, i: (b, ctx_blk, 0))]


def _token_tile(x_ref, c_ref, n_lat_tiles):
    if n_lat_tiles is None:
        return x_ref[0]
    return jnp.where(pl.program_id(1) < n_lat_tiles, x_ref[0], c_ref[0])


def _inproj_kernel(x_ref, c_ref, mod_ref, g1_ref, wa_ref, wg_ref, wh_ref, wz_ref, qn_ref, kvn_ref, wuq_ref, wukv_ref,
                   wa2_ref, ba_ref, cos_ref, sin_ref,
                   q_out, k_out, v_out, gq_out, gk_out, gv_out, gr_out, glog_out, hy_out, gate_out, *, n_lat_tiles):
    x = _token_tile(x_ref, c_ref, n_lat_tiles)
    shift = mod_ref[0, 0:1, :]
    scale = mod_ref[0, 1:2, :]
    h = (_rms(x) * (g1_ref[...] * (1.0 + scale)) + shift).astype(BF16)

    za = _dot(h, wa_ref[...])
    cos = cos_ref[...]
    sin = sin_ref[...]

    cqn = (_rms(za[:, 0:256]) * qn_ref[...]).astype(BF16)
    qab = _dot(cqn, wuq_ref[...])
    nq = MLA_HEADS * HEAD_SLOT
    for hd in range(MLA_HEADS):
        sl = slice(hd * HEAD_SLOT, (hd + 1) * HEAD_SLOT)
        qa = qab[:, hd * HEAD_SLOT:(hd + 1) * HEAD_SLOT]
        qb = qab[:, nq + hd * HEAD_SLOT:nq + (hd + 1) * HEAD_SLOT]
        q_out[0, :, sl] = ((qa * cos + qb * sin) * (MLA_SCALE * LOG2E)).astype(BF16)

    ckvn = (_rms(za[:, 256:384]) * kvn_ref[...]).astype(BF16)
    kv = _dot(ckvn, wukv_ref[...])
    krot = za[:, 384:512] * cos + za[:, 512:640] * sin
    for hd in range(MLA_HEADS):
        sl = slice(hd * HEAD_SLOT, (hd + 1) * HEAD_SLOT)
        k_out[0, :, sl] = (kv[:, sl] + krot).astype(BF16)
    ones_hi = (lax.broadcasted_iota(jnp.int32, (1, HEAD_SLOT), 1) >= MLA_V).astype(F32)
    for hd in range(MLA_HEADS):
        sl = slice(hd * HEAD_SLOT, (hd + 1) * HEAD_SLOT)
        v_out[0, :, sl] = (kv[:, nq + hd * HEAD_SLOT:nq + (hd + 1) * HEAD_SLOT] + ones_hi).astype(BF16)

    xg = _dot(za[:, 640:768].astype(BF16), wa2_ref[...]) + ba_ref[...]
    glog_out[0] = (jnp.minimum(xg, 0.0) - jnp.log(1.0 + jnp.exp(-jnp.abs(xg)))) * (1.0 / GLA_TAU)

    zg = _dot(h, wg_ref[...])
    gq_out[0] = (zg[:, 0:GLA_DK] * (GLA_HK ** -0.5)).astype(BF16)
    gk_out[0] = zg[:, GLA_DK:2 * GLA_DK].astype(BF16)
    gv_out[0] = zg[:, 2 * GLA_DK:2 * GLA_DK + GLA_DV].astype(BF16)
    gr = zg[:, 2 * GLA_DK + GLA_DV:]
    gr_out[0] = (gr * _sigmoid(gr)).astype(BF16)

    hy_out[0] = _dot(h, wh_ref[...])
    gate_out[0] = _sigmoid(_dot(h, wz_ref[...])).astype(BF16)


def _rope_partner(w):
    a = MLA_ROPE // 4
    perm = np.concatenate([np.arange(a, 2 * a), np.arange(0, a), np.arange(3 * a, 4 * a), np.arange(2 * a, 3 * a)])
    sign = np.concatenate([-np.ones(a), np.ones(a), -np.ones(a), np.ones(a)]).astype(np.float32)
    return w[:, perm] * sign


def _prep_inproj_weights(w_in, mla_w_uq, mla_w_ukv, gla_w_a2, gla_b_a):
    w_in = w_in.astype(BF16)
    offs = np.concatenate([[0], np.cumsum(IN_SIZES)])
    seg = [w_in[:, offs[i]:offs[i + 1]] for i in range(len(IN_SIZES))]
    w_cq, w_ckv, w_kr, w_gq, w_gk, w_gv, w_gr, w_af, w_ab, w_hy, w_gate = seg
    d = w_in.shape[0]
    z = lambda n: jnp.zeros((d, n), w_in.dtype)
    kr_tile = jnp.concatenate([z(MLA_NOPE), w_kr, z(HEAD_SLOT - MLA_NOPE - MLA_ROPE)], axis=1)
    krp_tile = jnp.concatenate([z(MLA_NOPE), _rope_partner(w_kr), z(HEAD_SLOT - MLA_NOPE - MLA_ROPE)], axis=1)
    a_tile = jnp.concatenate([w_af, w_ab, z(LANES - 2 * GLA_GATE_RANK)], axis=1)
    wa = jnp.concatenate([w_cq, w_ckv, kr_tile, krp_tile, a_tile], axis=1)
    wg = jnp.concatenate([w_gq, w_gk, w_gv, w_gr], axis=1)

    dh = MLA_NOPE + MLA_ROPE
    zq = lambda n: jnp.zeros((MLA_Q_LORA, n), w_in.dtype)
    plain, partner = [], []
    for hd in range(MLA_HEADS):
        blk = mla_w_uq[:, hd * dh:(hd + 1) * dh]
        plain += [blk, zq(HEAD_SLOT - dh)]
        partner += [zq(MLA_NOPE), _rope_partner(blk[:, MLA_NOPE:]), zq(HEAD_SLOT - dh)]
    wuq = jnp.concatenate(plain + partner, axis=1)

    zk = jnp.zeros((MLA_KV_LORA, HEAD_SLOT - MLA_NOPE), w_in.dtype)
    kcols, vcols = [], []
    for hd in range(MLA_HEADS):
        blk = mla_w_ukv[:, hd * (MLA_NOPE + MLA_V):(hd + 1) * (MLA_NOPE + MLA_V)]
        kcols += [blk[:, :MLA_NOPE], zk]
        vcols += [blk[:, MLA_NOPE:], zk]
    wukv = jnp.concatenate(kcols + vcols, axis=1)

    wa2 = jnp.zeros((LANES, 2 * GLA_DK), gla_w_a2.dtype)
    wa2 = wa2.at[0:GLA_GATE_RANK, 0:GLA_DK].set(gla_w_a2[0])
    wa2 = wa2.at[GLA_GATE_RANK:2 * GLA_GATE_RANK, GLA_DK:].set(gla_w_a2[1])
    ba = jnp.concatenate([gla_b_a[0], gla_b_a[1]])[None, :]
    bf = lambda t: t.astype(BF16)
    return bf(wa), bf(wg), bf(w_hy), bf(w_gate), bf(wuq), bf(wukv), bf(wa2), ba


def _rope_tables(seq, ctx_len):
    rows = seq // GRID_W
    row = np.repeat(np.arange(rows, dtype=np.float64), GRID_W)
    col = np.tile(np.arange(GRID_W, dtype=np.float64), rows)
    a = MLA_ROPE // 4
    inv = ROPE_BASE ** (-np.arange(a, dtype=np.float64) / a)
    ang_r = row[:, None] * inv
    ang_c = col[:, None] * inv
    cos32 = np.concatenate([np.cos(ang_r), np.cos(ang_r), np.cos(ang_c), np.cos(ang_c)], axis=1)
    sin32 = np.concatenate([np.sin(ang_r), np.sin(ang_r), np.sin(ang_c), np.sin(ang_c)], axis=1)
    pad_r = HEAD_SLOT - MLA_NOPE - MLA_ROPE
    cos = np.concatenate([np.ones((seq, MLA_NOPE)), cos32, np.zeros((seq, pad_r))], axis=1)
    sin = np.concatenate([np.zeros((seq, MLA_NOPE)), sin32, np.zeros((seq, pad_r))], axis=1)
    cos_c = np.concatenate([np.ones((ctx_len, MLA_NOPE + MLA_ROPE)), np.zeros((ctx_len, pad_r))], axis=1)
    sin_c = np.zeros((ctx_len, HEAD_SLOT))
    return (jnp.asarray(np.concatenate([cos, cos_c], axis=0), F32),
            jnp.asarray(np.concatenate([sin, sin_c], axis=0), F32))


def _inproj(x_lat, x_ctx, ctx_blk, mod_l, g1, weights, q_norm, kv_norm, cos, sin, n_lat_tiles):
    bsz, _, d = x_lat.shape
    nt = n_lat_tiles + 1
    t = nt * TM
    wa, wg, wh, wz, wuq, wukv, wa2, ba = weights
    tile = lambda w: pl.BlockSpec((1, TM, w), lambda b, i: (b, i, 0))
    row = lambda w: pl.BlockSpec((1, w), lambda b, i: (0, 0))
    mod_spec = pl.BlockSpec((1, 6, d), lambda b, i: (jnp.where(i < n_lat_tiles, b, bsz), 0, 0))
    tab = pl.BlockSpec((TM, HEAD_SLOT), lambda b, i: (i, 0))
    nq = MLA_HEADS * HEAD_SLOT
    sds = lambda w, dt: jax.ShapeDtypeStruct((bsz, t, w), dt)
    out_shape = (sds(nq, BF16), sds(nq, BF16), sds(nq, BF16), sds(GLA_DK, BF16), sds(GLA_DK, BF16),
                 sds(GLA_DV, BF16), sds(GLA_DV, BF16), sds(2 * GLA_DK, F32), sds(3 * HY_WIDTH, F32),
                 sds(3 * D_MODEL, BF16))
    out_specs = (tile(nq), tile(nq), tile(nq), tile(GLA_DK), tile(GLA_DK), tile(GLA_DV), tile(GLA_DV),
                 tile(2 * GLA_DK), tile(3 * HY_WIDTH), tile(3 * D_MODEL))
    return pl.pallas_call(
        functools.partial(_inproj_kernel, n_lat_tiles=n_lat_tiles),
        out_shape=out_shape,
        grid=(bsz, nt),
        in_specs=_token_specs(d, n_lat_tiles, ctx_blk) + [
            mod_spec, row(d), _const_spec(wa.shape), _const_spec(wg.shape), _const_spec(wh.shape),
            _const_spec(wz.shape), row(MLA_Q_LORA), row(MLA_KV_LORA), _const_spec(wuq.shape),
            _const_spec(wukv.shape), _const_spec(wa2.shape), row(2 * GLA_DK), tab, tab],
        out_specs=out_specs,
        compiler_params=_cparams(2, VMEM_LIMIT),
        name="inproj",
    )(x_lat, x_ctx, mod_l, g1[None, :], wa, wg, wh, wz, q_norm[None, :], kv_norm[None, :], wuq, wukv, wa2, ba,
      cos, sin)


ATT_TK = 1024


ATT_TQ = 1024
ATT_HEADS = 2


def _attn_kernel(q_ref, k_ref, v_ref, o_ref, m_ref, acc_ref, *, chunks):
    tq = q_ref.shape[1]
    n_heads = q_ref.shape[2] // HEAD_SLOT
    m_ref[...] = jnp.full(m_ref.shape, -jnp.inf, F32)
    acc_ref[...] = jnp.zeros(acc_ref.shape, F32)
    for r0, size in chunks:
        k = k_ref[0, pl.ds(r0, size), :]
        v = v_ref[0, pl.ds(r0, size), :]
        for hd in range(n_heads):
            sl = slice(hd * HEAD_SLOT, (hd + 1) * HEAD_SLOT)
            s = _dot_nt(q_ref[0, :, sl], k[:, sl])
            m_prev = m_ref[hd]
            m_new = jnp.maximum(m_prev, jnp.max(s, axis=1, keepdims=True))
            p = jnp.exp2((s - jnp.concatenate([m_new] * (size // LANES), axis=1)).astype(BF16))
            acc_ref[hd] = jnp.exp2(m_prev - m_new) * acc_ref[hd] + _dot(p, v[:, sl])
            m_ref[hd] = m_new
    lane = lax.broadcasted_iota(jnp.int32, (tq, HEAD_SLOT), 1)
    for pr in range(n_heads // 2):
        a0 = acc_ref[2 * pr]
        a1 = acc_ref[2 * pr + 1]
        o0 = a0 / pltpu.roll(a0, MLA_V, 1)
        o1 = pltpu.roll(a1, MLA_V, 1) / a1
        o_ref[0, :, pr * HEAD_SLOT:(pr + 1) * HEAD_SLOT] = jnp.where(lane < MLA_V, o0, o1).astype(o_ref.dtype)


def _attention(q, k, v, seq, with_ctx_queries):
    bsz, t, _ = q.shape
    ctx_len = t - seq
    assert seq % ATT_TK == 0 and seq % ATT_TQ == 0 and seq % ctx_len == 0 and 2 * MLA_V == HEAD_SLOT
    n_chunks = seq // ATT_TK
    chunks = tuple((j * ATT_TK, ATT_TK) for j in range(n_chunks - 1))
    chunks += (((n_chunks - 1) * ATT_TK, ATT_TK + ctx_len),)
    scratch = lambda nh, tq: [pltpu.VMEM((nh, tq, LANES), F32), pltpu.VMEM((nh, tq, HEAD_SLOT), F32)]
    hps = ATT_HEADS
    y = pl.pallas_call(
        functools.partial(_attn_kernel, chunks=chunks),
        out_shape=jax.ShapeDtypeStruct((bsz, seq, MLA_OUT), BF16),
        grid=(bsz, MLA_HEADS // hps, seq // ATT_TQ),
        in_specs=[
            pl.BlockSpec((1, ATT_TQ, hps * HEAD_SLOT), lambda b, hp, i: (b, i, hp)),
            pl.BlockSpec((1, t, hps * HEAD_SLOT), lambda b, hp, i: (b, 0, hp)),
            pl.BlockSpec((1, t, hps * HEAD_SLOT), lambda b, hp, i: (b, 0, hp)),
        ],
        out_specs=pl.BlockSpec((1, ATT_TQ, hps * MLA_V), lambda b, hp, i: (b, i, hp)),
        scratch_shapes=scratch(hps, ATT_TQ),
        compiler_params=_cparams(3, VMEM_LIMIT),
        name="mla_attention",
    )(q, k, v)
    if not with_ctx_queries:
        return y, None
    cblk = seq // ctx_len
    ctx_rows = pl.BlockSpec((1, ctx_len, MLA_HEADS * HEAD_SLOT), lambda b: (b, cblk, 0))
    y_ctx = pl.pallas_call(
        functools.partial(_attn_kernel, chunks=((0, ctx_len),)),
        out_shape=jax.ShapeDtypeStruct((bsz, ctx_len, MLA_OUT), BF16),
        grid=(bsz,),
        in_specs=[ctx_rows, ctx_rows, ctx_rows],
        out_specs=pl.BlockSpec((1, ctx_len, MLA_OUT), lambda b: (b, 0, 0)),
        scratch_shapes=scratch(MLA_HEADS, ctx_len),
        compiler_params=_cparams(1),
        name="mla_attention_ctx",
    )(q, k, v)
    return y, y_ctx


GLA_LEVELS = int(math.log2(TM))
GLA_SAFE_SPAN = 60.0


def _gla_level_matrices():
    i = np.arange(TM)[:, None]
    t = np.arange(TM)[None, :]
    fwd = [(t <= i)]
    for lv in range(GLA_LEVELS):
        m = TM >> (lv + 1)
        lo = (i // m) * m
        later = ((i // m) % 2) == 1
        q_part = later & (t >= lo) & (t <= i)
        k_part = (~later) & (t > i) & (t <= lo + m - 1)
        fwd.append(q_part | k_part)
    fwd = np.concatenate(fwd, axis=0).astype(np.float32)
    nb = 1 + GLA_LEVELS
    bwd = fwd.reshape(nb, TM, TM)[:, ::-1, ::-1].reshape(nb * TM, TM)
    return np.stack([fwd, bwd])


def _gla_kernel(qf_ref, kf_ref, vf_ref, gf_ref, qb_ref, kb_ref, vb_ref, gb_ref, mall_ref, of_ref, ob_ref, s_ref, a_ref):
    step = pl.program_id(1)

    @pl.when(step == 0)
    def _():
        s_ref[...] = jnp.zeros(s_ref.shape, F32)

    ins = ((qf_ref, kf_ref, vf_ref, gf_ref), (qb_ref, kb_ref, vb_ref, gb_ref))
    outs = (of_ref, ob_ref)
    qs, ks, vs, g2s, g_cums, g_tots = [], [], [], [], [], []
    for d in range(2):
        q_ref, k_ref, v_ref, g_ref = ins[d]
        qs.append(q_ref[0].astype(F32))
        ks.append(k_ref[0].astype(F32))
        vs.append(v_ref[0])
        g = g_ref[0]
        g2 = jnp.concatenate([g.astype(BF16), (g - g.astype(BF16).astype(F32)).astype(BF16)], axis=1)
        e2 = _dot(mall_ref[d, 0:TM, :], g2)
        g2s.append(g2)
        g_cums.append(e2[:, :GLA_DK] + e2[:, GLA_DK:])
        g_tots.append(jnp.sum(g, axis=0, keepdims=True))

    row = lax.broadcasted_iota(jnp.int32, (TM, TM), 0)
    col = lax.broadcasted_iota(jnp.int32, (TM, TM), 1)
    lane_head = lax.broadcasted_iota(jnp.int32, (TM, GLA_DK), 1) // GLA_HK
    tok = lax.broadcasted_iota(jnp.int32, (TM, GLA_DK), 0)
    eye = row == col

    def stack_heads(t):
        return jnp.concatenate([jnp.where(lane_head == hd, t, 0.0) for hd in range(GLA_HEADS)], axis=0).astype(BF16)

    span = jnp.maximum(jnp.max(-g_tots[0]), jnp.max(-g_tots[1]))

    @pl.when(span < GLA_SAFE_SPAN)
    def _():
        for d in range(2):
            res = _dot_nt(stack_heads(qs[d] * jnp.exp(g_cums[d])), (ks[d] * jnp.exp(-g_cums[d])).astype(BF16))
            seen = (col <= row) if d == 0 else (col >= row)
            for hd in range(GLA_HEADS):
                a_ref[d, hd] = jnp.where(seen, res[hd * TM:(hd + 1) * TM], 0.0)

    @pl.when(span >= GLA_SAFE_SPAN)
    def _():
        for d in range(2):
            q, k = qs[d], ks[d]
            e2l = _dot(mall_ref[d, TM:, :], g2s[d])
            e_lv = e2l[:, :GLA_DK] + e2l[:, GLA_DK:]
            res = _dot_nt(stack_heads(q), k.astype(BF16))
            for hd in range(GLA_HEADS):
                a_ref[d, hd] = jnp.where(eye, res[hd * TM:(hd + 1) * TM], 0.0)
            for lv in range(GLA_LEVELS):
                m = TM >> (lv + 1)
                w = jnp.exp(e_lv[lv * TM:(lv + 1) * TM])
                bit = (tok // m) % 2
                q_act = bit != d
                qt = jnp.where(q_act, q * w, 0.0)
                kt = jnp.where(q_act, 0.0, k * w).astype(BF16)
                res = _dot_nt(stack_heads(qt), kt)
                if m == TM // 2:
                    for hd in range(GLA_HEADS):
                        a_ref[d, hd] += res[hd * TM:(hd + 1) * TM]
                else:
                    same = (row // (2 * m)) == (col // (2 * m))
                    for hd in range(GLA_HEADS):
                        a_ref[d, hd] += jnp.where(same, res[hd * TM:(hd + 1) * TM], 0.0)

    same_head = (lax.broadcasted_iota(jnp.int32, (GLA_DK, GLA_DV), 0) // GLA_HK
                 == lax.broadcasted_iota(jnp.int32, (GLA_DK, GLA_DV), 1) // GLA_HV)
    for d in range(2):
        q, k, v, g_cum, g_tot = qs[d], ks[d], vs[d], g_cums[d], g_tots[d]
        s_old = s_ref[d]
        o_inter = _dot((q * jnp.exp(g_cum)).astype(BF16), s_old.astype(BF16))
        for hd in range(GLA_HEADS):
            sl = slice(hd * GLA_HV, (hd + 1) * GLA_HV)
            o_intra = _dot(a_ref[d, hd].astype(BF16), v[:, sl])
            outs[d][0, :, sl] = (o_intra + o_inter[:, sl]).astype(outs[d].dtype)

        kdec_t = (k * jnp.exp(g_tot - g_cum)).T.astype(BF16)
        upd = _dot(kdec_t, v)
        dec_col = jnp.sum(jnp.where(eye, jnp.broadcast_to(jnp.exp(g_tot), (TM, GLA_DK)), 0.0), axis=1,
                          keepdims=True)
        s_ref[d] = dec_col * s_old + jnp.where(same_head, upd, 0.0)


def _gla(gq, gk, gv, glog, n_lat_tiles):
    assert GLA_DK == TM
    bsz, t, _ = gq.shape
    nt = t // TM
    mall = jnp.asarray(_gla_level_matrices(), dtype=BF16)

    def tile_idx(d, s):
        return jnp.where(s == 0, n_lat_tiles, s - 1 if d == 0 else n_lat_tiles - s)

    def specs(d):
        tok = lambda w: pl.BlockSpec((1, TM, w), lambda b, s: (b, tile_idx(d, s), 0))
        return [tok(GLA_DK), tok(GLA_DK), tok(GLA_DV), pl.BlockSpec((1, TM, GLA_DK), lambda b, s: (b, tile_idx(d, s), d))]

    out_spec = lambda d: pl.BlockSpec((1, TM, GLA_DV), lambda b, s: (b, tile_idx(d, s), 0))
    sds = jax.ShapeDtypeStruct((bsz, t, GLA_DV), BF16)
    return pl.pallas_call(
        _gla_kernel,
        out_shape=(sds, sds),
        grid=(bsz, nt),
        in_specs=specs(0) + specs(1) + [_const_spec(mall.shape)],
        out_specs=(out_spec(0), out_spec(1)),
        scratch_shapes=[pltpu.VMEM((2, GLA_DK, GLA_DV), F32), pltpu.VMEM((2, GLA_HEADS, TM, TM), F32)],
        compiler_params=_cparams(2),
        name="gla_scan",
    )(gq, gk, gv, glog, gq, gk, gv, glog, mall)


HY_N1 = 64
HY_N2 = 128
HY_SLAB = SUBLANES
HY_CB = 256
HY_KG = 32
HY_SPEC_CB = 128
HY_UNROLL = 16


def _hyena_dft_constants(seq):
    n = 2 * seq
    assert n == HY_N1 * HY_N2
    nh = HY_N1 // 2
    kh = HY_N2 // 2
    eye = np.eye(HY_SLAB)
    k1 = np.arange(HY_N1)
    th = 2 * np.pi * np.outer(k1 + 0.5, np.arange(nh)) / HY_N1
    fwd1 = np.concatenate([np.kron(np.cos(th), eye), np.kron(-np.sin(th), eye)], axis=0)
    inv1 = (2.0 / n) * np.concatenate([np.kron(np.cos(th).T, eye), np.kron(-np.sin(th).T, eye)], axis=1)
    nlo = np.arange(HY_N2)
    k2 = np.arange(kh)
    ph = 2 * np.pi * (k2[None, :, None] * nlo[None, None, :] / HY_N2
                      + (k1[:, None, None] + 0.5) * nlo[None, None, :] / n)
    c, s = np.cos(ph), np.sin(ph)
    fwd2 = np.concatenate([np.concatenate([c, s], axis=2), np.concatenate([-s, c], axis=2)], axis=1)
    ct, st = c.transpose(0, 2, 1), s.transpose(0, 2, 1)
    inv2 = np.concatenate([np.concatenate([ct, -st], axis=2), np.concatenate([st, ct], axis=2)], axis=1)
    return fwd1, fwd2, inv2, inv1


def _hyena_ctx_dft_constants(ctx_len):
    n = 2 * ctx_len
    th = 2 * np.pi * np.outer(np.arange(ctx_len) + 0.5, np.arange(ctx_len)) / n
    fwd = np.concatenate([np.cos(th), -np.sin(th)], axis=0)
    inv = (2.0 / n) * np.concatenate([np.cos(th).T, -np.sin(th).T], axis=1)
    return fwd, inv


def _hyena_features(length):
    pos = np.arange(length, dtype=np.float64)
    t = pos / max(length - 1, 1)
    f = np.linspace(1e-4, HY_BANDS - 1, HY_BANDS)
    ang = (2.0 * math.pi / length) * pos[:, None] * f
    feat = np.concatenate([t[:, None], np.cos(ang), np.sin(ang)], axis=-1)
    return jnp.asarray(np.pad(feat, ((0, 0), (0, LANES - HY_POS_DIM))), F32)


def _hy_filter_kernel(feat_ref, w1_ref, b1_ref, w2_ref, b2_ref, w3_ref, b3_ref, absd_ref, h_ref, s_ref):
    i = pl.program_id(0)
    feat = feat_ref[...]
    hp = _dot3
    hdn = jnp.sin(hp(feat, w1_ref[...]) + b1_ref[...])
    hdn = jnp.sin(hp(hdn, w2_ref[...]) + b2_ref[...])
    h = hp(hdn, w3_ref[...]) + b3_ref[...]
    window = jnp.exp(-feat[:, 0:1] * absd_ref[...])
    h = h * jnp.concatenate([window] * (2 * HY_ORDER), axis=1)
    h_ref[...] = h

    @pl.when(i == 0)
    def _():
        s_ref[...] = jnp.zeros(s_ref.shape, F32)

    s_ref[...] += jnp.sum(jnp.abs(h), axis=0, keepdims=True)


def _hyena_filters_raw(length, filt_w):
    w1, b1, w2, b2, w3, b3 = filt_w
    nf = 2 * HY_ORDER * HY_WIDTH
    tr = min(length, 512)
    deltas = np.linspace(math.log(HY_DECAY_TARGET) / HY_FAST_DECAY, math.log(HY_DECAY_TARGET) / HY_SLOW_DECAY,
                         HY_WIDTH, dtype=np.float32)
    absd = jnp.asarray(np.abs(deltas))[None, :]
    w1p = jnp.pad(w1, ((0, LANES - HY_POS_DIM), (0, 0)))
    full = lambda shp: pl.BlockSpec(shp, lambda i: (0,) * len(shp))
    return pl.pallas_call(
        _hy_filter_kernel,
        out_shape=(jax.ShapeDtypeStruct((length, nf), F32), jax.ShapeDtypeStruct((1, nf), F32)),
        grid=(length // tr,),
        in_specs=[pl.BlockSpec((tr, LANES), lambda i: (i, 0)), full((LANES, HY_FILTER_HIDDEN)),
                  full((1, HY_FILTER_HIDDEN)), full((HY_FILTER_HIDDEN, HY_FILTER_HIDDEN)), full((1, HY_FILTER_HIDDEN)),
                  full((HY_FILTER_HIDDEN, nf)), full((1, nf)), full((1, HY_WIDTH))],
        out_specs=(pl.BlockSpec((tr, nf), lambda i: (i, 0)), full((1, nf))),
        compiler_params=_cparams(1),
        name="hyena_filter_mlp",
    )(_hyena_features(length), w1p, b1[None, :], w2, b2[None, :], w3, b3[None, :], absd)


def _dot_split(m, x):
    x_hi = x.astype(BF16)
    x_lo = (x - x_hi.astype(F32)).astype(BF16)
    n = x.shape[1]
    r = _dot(m, jnp.concatenate([x_hi, x_lo], axis=1))
    return r[:, :n] + r[:, n:]


def _odft_stage1(src_at, mm, s_re, s_im, unroll=2):
    nk = s_re.shape[0]
    half = nk * HY_SLAB

    def body(j, carry):
        r0 = pl.multiple_of(j * HY_SLAB, HY_SLAB)
        slab = src_at(r0)
        cb = slab.shape[-1]
        res = mm(slab.reshape(-1, cb))
        s_re[:, pl.ds(r0, HY_SLAB), :] = res[:half].reshape(nk, HY_SLAB, cb)
        s_im[:, pl.ds(r0, HY_SLAB), :] = res[half:].reshape(nk, HY_SLAB, cb)
        return carry

    lax.fori_loop(0, HY_N2 // HY_SLAB, body, 0, unroll=unroll)


def _hy_spectrum_kernel(hf_ref, hb_ref, sf_ref, sb_ref, fwd1_ref, fwd2_ref, o_ref, s_re, s_im):
    kh = HY_N2 // 2
    mm1 = lambda x: _dot_split(fwd1_ref[...], x)

    def middle(sign):
        def body(k1, carry):
            a = jnp.concatenate([s_re[k1], s_im[k1]], axis=0)
            x = _dot_split(fwd2_ref[k1], a)
            if sign is None:
                o_ref[0, k1] = x[:kh]
                o_ref[1, k1] = x[kh:]
            else:
                inv_norm = 1.0 / (sf_ref[...] + sb_ref[...])
                o_ref[0, k1] = (o_ref[0, k1] + x[:kh]) * inv_norm
                o_ref[1, k1] = (o_ref[1, k1] - x[kh:]) * inv_norm
            return carry
        lax.fori_loop(0, HY_N1, body, 0, unroll=32)

    _odft_stage1(lambda r0: hf_ref[:, pl.ds(r0, HY_SLAB), :], mm1, s_re, s_im, unroll=16)
    middle(None)

    def bwd_slab(r0):
        slab = hb_ref[:, pl.ds(r0, HY_SLAB), :]
        nhi = lax.broadcasted_iota(jnp.int32, slab.shape, 0)
        r = lax.broadcasted_iota(jnp.int32, slab.shape, 1)
        return jnp.where((nhi == 0) & (r + r0 == 0), 0.0, slab)

    _odft_stage1(bwd_slab, mm1, s_re, s_im, unroll=16)
    middle(-1)


def _hyena_filter_spectrum(seq, w1, b1, w2, b2, w3, b3):
    h_raw, s = _hyena_filters_raw(seq, (w1, b1, w2, b2, w3, b3))
    nh = HY_N1 // 2
    nc = HY_ORDER * HY_WIDTH
    h3 = h_raw.reshape(nh, HY_N2, 2 * nc)
    fwd1, fwd2, _, _ = _hyena_dft_constants(seq)
    fwd1 = jnp.asarray(fwd1, BF16)
    fwd2 = jnp.asarray(fwd2, BF16)
    scb = HY_SPEC_CB
    ncb = nc // scb
    return pl.pallas_call(
        _hy_spectrum_kernel,
        out_shape=jax.ShapeDtypeStruct((2, HY_N1, HY_N2 // 2, nc), F32),
        grid=(ncb,),
        in_specs=[pl.BlockSpec((nh, HY_N2, scb), lambda c: (0, 0, c)),
                  pl.BlockSpec((nh, HY_N2, scb), lambda c: (0, 0, ncb + c)),
                  pl.BlockSpec((1, scb), lambda c: (0, c)),
                  pl.BlockSpec((1, scb), lambda c: (0, ncb + c)),
                  _const_spec(fwd1.shape), _const_spec(fwd2.shape)],
        out_specs=pl.BlockSpec((2, HY_N1, HY_N2 // 2, scb), lambda c: (0, 0, 0, c)),
        scratch_shapes=[pltpu.VMEM((HY_N1, HY_N2, scb), F32), pltpu.VMEM((HY_N1, HY_N2, scb), F32)],
        compiler_params=_cparams(1, VMEM_LIMIT),
        name="hyena_filter_spectrum",
    )(h3, h3, s, s, fwd1, fwd2)


def _short_conv_chunk(ref, c, n_chunks, w_ref, b_ref):
    per = TM // HY_N2
    cur = ref[0, pl.ds(per * c, per)]
    cb = cur.shape[-1]
    cur = cur.reshape(TM, cb)
    prev = ref[0, jnp.maximum(per * c - 1, 0), pl.ds(HY_N2 - SUBLANES, SUBLANES), :][SUBLANES - 1:SUBLANES]
    nxt = ref[0, jnp.minimum(per * c + per, per * n_chunks - 1), pl.ds(0, SUBLANES), :][0:1]
    prev = jnp.where(c > 0, prev, 0.0)
    nxt = jnp.where(c < n_chunks - 1, nxt, 0.0)
    rowi = lax.broadcasted_iota(jnp.int32, (TM, cb), 0)
    dn = jnp.where(rowi == 0, prev, pltpu.roll(cur, 1, 0))
    up = jnp.where(rowi == TM - 1, nxt, pltpu.roll(cur, TM - 1, 0))
    return b_ref[...] + w_ref[0:1, :] * dn + w_ref[1:2, :] * cur + w_ref[2:3, :] * up


def _hy_conv_kernel(*refs, conv_y):
    if conv_y:
        (y_ref, g_ref, h_ref, bias_ref, wy_ref, by_ref, wg_ref, bg_ref, fwd1_ref, fwd2_ref, inv2_ref, inv1_ref,
         o_ref, s_re, s_im, gs_ref, us_ref) = refs
    else:
        (y_ref, g_ref, h_ref, bias_ref, wg_ref, bg_ref, fwd1_ref, fwd2_ref, inv2_ref, inv1_ref,
         o_ref, s_re, s_im, gs_ref) = refs
    grp = pl.program_id(2)
    last = pl.num_programs(2) - 1
    nh = HY_N1 // 2
    per = TM // HY_N2
    n_chunks = nh // per
    cb = o_ref.shape[-1]
    kh = HY_N2 // 2

    @pl.when(grp == 0)
    def _():
        def pre(c, carry):
            gs_ref[pl.ds(per * c, per)] = _short_conv_chunk(g_ref, c, n_chunks, wg_ref, bg_ref).reshape(per, HY_N2, cb)
            if conv_y:
                us_ref[pl.ds(per * c, per)] = _short_conv_chunk(y_ref, c, n_chunks, wy_ref, by_ref).reshape(
                    per, HY_N2, cb)
            return carry
        lax.fori_loop(0, n_chunks, pre, 0)

    if conv_y:
        u_at = lambda r0: us_ref[:, pl.ds(r0, HY_SLAB), :]
    else:
        u_at = lambda r0: y_ref[0, :, pl.ds(r0, HY_SLAB), :]

    _odft_stage1(u_at, lambda x: _dot(fwd1_ref[0], x.astype(BF16)), s_re, s_im, unroll=HY_UNROLL)

    def middle(k1, carry):
        a = jnp.concatenate([s_re[k1], s_im[k1]], axis=0).astype(BF16)
        x = _dot(fwd2_ref[k1], a)
        xr, xi = x[:kh], x[kh:]
        hr, hi = h_ref[0, k1], h_ref[1, k1]
        y = jnp.concatenate([xr * hr - xi * hi, xr * hi + xi * hr], axis=0).astype(BF16)
        bm = _dot(inv2_ref[k1], y)
        s_re[k1] = bm[:HY_N2]
        s_im[k1] = bm[HY_N2:]
        return carry

    lax.fori_loop(0, HY_KG, middle, 0, unroll=HY_KG)

    def partial_conv(r0):
        slab = jnp.concatenate([s_re[:, pl.ds(r0, HY_SLAB), :].reshape(HY_KG * HY_SLAB, cb),
                                s_im[:, pl.ds(r0, HY_SLAB), :].reshape(HY_KG * HY_SLAB, cb)], axis=0).astype(BF16)
        return _dot(inv1_ref[0], slab).reshape(nh, HY_SLAB, cb)

    def post_loop(fn):
        def post(j, carry):
            r0 = pl.multiple_of(j * HY_SLAB, HY_SLAB)
            o_ref[0, :, pl.ds(r0, HY_SLAB), :] = fn(r0, partial_conv(r0))
            return carry
        lax.fori_loop(0, HY_N2 // HY_SLAB, post, 0, unroll=HY_UNROLL)

    @pl.when(grp == 0)
    def _():
        post_loop(lambda r0, part: part)

    @pl.when((grp > 0) & (grp < last))
    def _():
        post_loop(lambda r0, part: o_ref[0, :, pl.ds(r0, HY_SLAB), :] + part)

    @pl.when(grp == last)
    def _():
        post_loop(lambda r0, part: gs_ref[:, pl.ds(r0, HY_SLAB), :]
                  * (o_ref[0, :, pl.ds(r0, HY_SLAB), :] + part + bias_ref[...] * u_at(r0)))


def _hyena_order(y4, y_col0, z4, gate_col0, hspec, order, hy_bias, short_w, short_b, consts, conv_y):
    bsz = z4.shape[0]
    nh = HY_N1 // 2
    ncb = HY_WIDTH // HY_CB
    ngrp = HY_N1 // HY_KG
    fwd1, fwd2, inv2, inv1 = consts
    blk4 = lambda off: pl.BlockSpec((1, nh, HY_N2, HY_CB), lambda c, b, g: (b, 0, 0, off + c))
    rowspec = lambda rows, off: pl.BlockSpec((rows, HY_CB), lambda c, b, g: (0, off + c))
    grouped = lambda shp: pl.BlockSpec(shp, lambda c, b, g: (g, 0, 0))
    in_specs = [blk4(y_col0), blk4(gate_col0),
                pl.BlockSpec((2, HY_KG, HY_N2 // 2, HY_CB), lambda c, b, g: (0, g, 0, order * ncb + c)),
                rowspec(1, 0)]
    args = [y4, z4, hspec, hy_bias.reshape(1, -1)]
    if conv_y:
        in_specs += [rowspec(HY_SHORT, y_col0), rowspec(1, y_col0)]
        args += [short_w, short_b[None, :]]
    in_specs += [rowspec(HY_SHORT, gate_col0), rowspec(1, gate_col0)]
    args += [short_w, short_b[None, :]]
    in_specs += [grouped((1,) + fwd1.shape[1:]), grouped((HY_KG,) + fwd2.shape[1:]),
                 grouped((HY_KG,) + inv2.shape[1:]), grouped((1,) + inv1.shape[1:])]
    args += [fwd1, fwd2, inv2, inv1]
    scratch = [pltpu.VMEM((HY_KG, HY_N2, HY_CB), F32), pltpu.VMEM((HY_KG, HY_N2, HY_CB), F32),
               pltpu.VMEM((nh, HY_N2, HY_CB), F32)]
    if conv_y:
        scratch.append(pltpu.VMEM((nh, HY_N2, HY_CB), F32))
    return pl.pallas_call(
        functools.partial(_hy_conv_kernel, conv_y=conv_y),
        out_shape=jax.ShapeDtypeStruct((bsz, nh, HY_N2, HY_WIDTH), F32),
        grid=(ncb, bsz, ngrp),
        in_specs=in_specs,
        out_specs=pl.BlockSpec((1, nh, HY_N2, HY_CB), lambda c, b, g: (b, 0, 0, c)),
        scratch_shapes=scratch,
        compiler_params=_cparams(3, V7X_VMEM_BYTES * 15 // 16),
        name="hyena_conv%d" % order,
    )(*args)


def _hyena_latent(z_hy, hspec, short_w, short_b, hy_bias):
    bsz, t, _ = z_hy.shape
    seq = HY_N1 * HY_N2 // 2
    fwd1, fwd2, inv2, inv1 = _hyena_dft_constants(seq)
    ngrp = HY_N1 // HY_KG
    rows = HY_KG * HY_SLAB
    fwd1 = fwd1.reshape(2, ngrp, rows, -1).transpose(1, 0, 2, 3).reshape(ngrp, 2 * rows, -1)
    inv1 = inv1.reshape(-1, 2, ngrp, rows).transpose(2, 0, 1, 3).reshape(ngrp, -1, 2 * rows)
    consts = tuple(jnp.asarray(m, BF16) for m in (fwd1, fwd2, inv2, inv1))
    z4 = z_hy.reshape(bsz, t // HY_N2, HY_N2, 3 * HY_WIDTH)
    ncb = HY_WIDTH // HY_CB
    y1 = _hyena_order(z4, 2 * ncb, z4, 0, hspec, 0, hy_bias[0], short_w, short_b, consts, True)
    y2 = _hyena_order(y1, 0, z4, ncb, hspec, 1, hy_bias[1], short_w, short_b, consts, False)
    return y2.reshape(bsz, seq, HY_WIDTH)


def _hy_ctx_spectrum_kernel(h_ref, s_ref, fwd_ref, o_ref):
    lc = h_ref.shape[0]
    nc = HY_ORDER * HY_WIDTH
    hp = _dot3
    h = h_ref[...]
    rowi = lax.broadcasted_iota(jnp.int32, (lc, nc), 0)
    xf = hp(fwd_ref[...], h[:, :nc])
    xb = hp(fwd_ref[...], jnp.where(rowi == 0, 0.0, h[:, nc:]))
    inv_norm = 1.0 / (s_ref[:, :nc] + s_ref[:, nc:])
    o_ref[0] = (xf[:lc] + xb[:lc]) * inv_norm
    o_ref[1] = (xf[lc:] - xb[lc:]) * inv_norm


def _hy_ctx_conv_kernel(x1_ref, x2_ref, v_ref, h_ref, bias_ref, w_ref, b_ref, fwd_ref, inv_ref, o_ref):
    lc = o_ref.shape[1]
    rowi = lax.broadcasted_iota(jnp.int32, (lc, HY_WIDTH), 0)

    def short(ref, part):
        cur = ref[0].reshape(lc, HY_WIDTH)
        sl = slice(part * HY_WIDTH, (part + 1) * HY_WIDTH)
        dn = jnp.where(rowi == 0, 0.0, pltpu.roll(cur, 1, 0))
        up = jnp.where(rowi == lc - 1, 0.0, pltpu.roll(cur, lc - 1, 0))
        return b_ref[:, sl] + w_ref[0:1, sl] * dn + w_ref[1:2, sl] * cur + w_ref[2:3, sl] * up

    y = short(v_ref, 2)
    for order, gref in enumerate((x1_ref, x2_ref)):
        sl = slice(order * HY_WIDTH, (order + 1) * HY_WIDTH)
        x = _dot(fwd_ref[...], y.astype(BF16))
        xr, xi = x[:lc], x[lc:]
        hr, hi = h_ref[0, :, sl], h_ref[1, :, sl]
        prod = jnp.concatenate([xr * hr - xi * hi, xr * hi + xi * hr], axis=0).astype(BF16)
        conv = _dot(inv_ref[...], prod)
        y = short(gref, order) * (conv + bias_ref[order:order + 1, :] * y)
    o_ref[0] = y


def _hyena_ctx(z_hy, filt_w, short_w, short_b, hy_bias):
    bsz, t, _ = z_hy.shape
    seq = HY_N1 * HY_N2 // 2
    lc = t - seq
    per = lc // HY_N2
    h_raw, s = _hyena_filters_raw(lc, filt_w)
    fwd, inv = _hyena_ctx_dft_constants(lc)
    nc = HY_ORDER * HY_WIDTH
    full = lambda shp: pl.BlockSpec(shp, lambda *_: (0,) * len(shp))
    hspec = pl.pallas_call(
        _hy_ctx_spectrum_kernel,
        out_shape=jax.ShapeDtypeStruct((2, lc, nc), F32),
        grid=(1,),
        in_specs=[full(h_raw.shape), full(s.shape), full(fwd.shape)],
        out_specs=full((2, lc, nc)),
        compiler_params=_cparams(1),
        name="hyena_ctx_spectrum",
    )(h_raw, s, jnp.asarray(fwd, F32))
    z4 = z_hy.reshape(bsz, t // HY_N2, HY_N2, 3 * HY_WIDTH)
    blk = lambda part: pl.BlockSpec((1, per, HY_N2, HY_WIDTH), lambda b: (b, seq // lc, 0, part))
    return pl.pallas_call(
        _hy_ctx_conv_kernel,
        out_shape=jax.ShapeDtypeStruct((bsz, lc, HY_WIDTH), F32),
        grid=(bsz,),
        in_specs=[blk(0), blk(1), blk(2), full((2, lc, nc)), full((HY_ORDER, HY_WIDTH)),
                  full((HY_SHORT, 3 * HY_WIDTH)), full((1, 3 * HY_WIDTH)), full(fwd.shape), full(inv.shape)],
        out_specs=pl.BlockSpec((1, lc, HY_WIDTH), lambda b: (b, 0, 0)),
        compiler_params=_cparams(1),
        name="hyena_ctx_conv",
    )(z4, z4, z4, hspec, hy_bias, short_w, short_b[None, :], jnp.asarray(fwd, BF16), jnp.asarray(inv, BF16))


FF_CHUNK = 1024


def _merge_mlp_kernel(x_ref, c_ref, mod_ref, ymla_ref, ymlac_ref, of_ref, ob_ref, gr_ref, yhy_ref, yhyc_ref, zg_ref, on_ref,
                      wm_ref, wgl_ref, wh_ref, wo_ref, g2_ref, w1_ref, w2_ref, fg_ref, o_ref, *, final, n_lat_tiles):
    o = of_ref[0].astype(F32) + ob_ref[0].astype(F32)
    silu = gr_ref[0].astype(F32)
    parts = []
    for hd in range(GLA_HEADS):
        sl = slice(hd * GLA_HV, (hd + 1) * GLA_HV)
        parts.append((_rms(o[:, sl]) * on_ref[...] * silu[:, sl]).astype(BF16))
    y_gla = jnp.concatenate(parts, axis=1)
    zg = zg_ref[0].astype(F32)
    d = x_ref.shape[-1]
    m = zg[:, 0:d] * _dot(_token_tile(ymla_ref, ymlac_ref, n_lat_tiles), wm_ref[...])
    m = m + zg[:, d:2 * d] * _dot(y_gla, wgl_ref[...])
    m = m + zg[:, 2 * d:3 * d] * _dot(_token_tile(yhy_ref, yhyc_ref, n_lat_tiles).astype(BF16), wh_ref[...])
    out = _dot(m.astype(BF16), wo_ref[...])
    x = _token_tile(x_ref, c_ref, n_lat_tiles) + mod_ref[0, 2:3, :] * out

    h = (_rms(x) * (g2_ref[...] * (1.0 + mod_ref[0, 4:5, :])) + mod_ref[0, 3:4, :]).astype(BF16)
    acc = jnp.zeros(x.shape, F32)
    for j in range(w1_ref.shape[1] // FF_CHUNK):
        a = jnp.maximum(_dot(h, w1_ref[:, j * FF_CHUNK:(j + 1) * FF_CHUNK]), 0.0)
        acc = acc + _dot((a * a).astype(BF16), w2_ref[j * FF_CHUNK:(j + 1) * FF_CHUNK, :])
    xn = x + mod_ref[0, 5:6, :] * acc
    if final:
        xn = _rms(xn) * fg_ref[...]
    o_ref[0] = xn


def _mod_spec(d, n_lat_tiles, bsz):
    return pl.BlockSpec((1, 6, d), lambda b, i: (jnp.where(i < n_lat_tiles, b, bsz), 0, 0))


def _merge_mlp(x_lat, x_ctx, ctx_blk, mod_l, y_mla, y_mla_ctx, o_gla, gr, y_hy, y_hy_ctx, gate, out_norm,
               w_o_mla, w_o_gla, w_o_hy, w_out, g2, w1, w2, final_g, n_tiles, n_lat_tiles, final):
    bsz, _, d = x_lat.shape
    if y_mla_ctx is None:
        y_mla_ctx, y_hy_ctx = y_mla, y_hy
    tile = lambda w: pl.BlockSpec((1, TM, w), lambda b, i: (b, i, 0))
    row = lambda w: pl.BlockSpec((1, w), lambda b, i: (0, 0))
    bf = lambda w: w.astype(BF16)
    return pl.pallas_call(
        functools.partial(_merge_mlp_kernel, final=final, n_lat_tiles=n_lat_tiles if n_tiles > n_lat_tiles else None),
        out_shape=jax.ShapeDtypeStruct((bsz, n_tiles * TM, d), F32),
        grid=(bsz, n_tiles),
        in_specs=_token_specs(d, n_lat_tiles, ctx_blk) + [_mod_spec(d, n_lat_tiles, bsz)]
        + _token_specs(MLA_OUT, n_lat_tiles, 0) + [tile(GLA_DV), tile(GLA_DV), tile(GLA_DV)]
        + _token_specs(HY_WIDTH, n_lat_tiles, 0) + [
            tile(3 * d), row(GLA_HV),
            _const_spec(w_o_mla.shape), _const_spec(w_o_gla.shape), _const_spec(w_o_hy.shape),
            _const_spec(w_out.shape), row(d), _const_spec(w1.shape), _const_spec(w2.shape), row(d)],
        out_specs=tile(d),
        compiler_params=_cparams(2, VMEM_LIMIT),
        name="merge_mlp",
    )(x_lat, x_ctx, mod_l, y_mla, y_mla_ctx, o_gla[0], o_gla[1], gr, y_hy, y_hy_ctx, gate, out_norm[None, :],
      bf(w_o_mla), bf(w_o_gla), bf(w_o_hy), bf(w_out), g2[None, :], bf(w1), bf(w2), final_g[None, :])


def kernel(x, c, ctx, c_ctx, ada_w, ada_b, norm1_g, norm2_g, w_in, mla_q_norm, mla_w_uq, mla_kv_norm, mla_w_ukv, gla_w_a2, gla_b_a, gla_out_norm, hy_short_w, hy_short_b, hy_f_w1, hy_f_b1, hy_f_w2, hy_f_b2, hy_f_w3, hy_f_b3, hy_bias, w_o_mla, w_o_gla, w_o_hy, w_out, ff_w1, ff_w2, final_norm_g):
    bsz, seq, d = x.shape
    ctx_len = ctx.shape[1]
    n_lat = seq // TM
    assert ctx_len == TM and seq % TM == 0
    n_all = n_lat + 1
    x_lat, x_ctx, ctx_blk = x, ctx, 0
    cc = jnp.zeros((16, d), F32).at[:bsz].set(c).at[bsz].set(c_ctx)
    mod = _modulation(cc, ada_w, ada_b).reshape(DEPTH, 16, 6, d)
    cos, sin = _rope_tables(seq, ctx_len)
    for l in range(DEPTH):
        last = l == DEPTH - 1
        n_tiles = n_lat if last else n_all
        weights = _prep_inproj_weights(w_in[l], mla_w_uq[l], mla_w_ukv[l], gla_w_a2[l], gla_b_a[l])
        q, k, v, gq, gk, gv, gr, glog, z_hy, z_gate = _inproj(x_lat, x_ctx, ctx_blk, mod[l], norm1_g[l], weights,
                                                             mla_q_norm[l], mla_kv_norm[l], cos, sin, n_lat)
        y_mla, y_mla_c = _attention(q, k, v, seq, not last)
        o_gla = _gla(gq, gk, gv, glog, n_lat)
        filt_w = (hy_f_w1[l], hy_f_b1[l], hy_f_w2[l], hy_f_b2[l], hy_f_w3[l], hy_f_b3[l])
        hspec = _hyena_filter_spectrum(seq, *filt_w)
        y_hy = _hyena_latent(z_hy, hspec, hy_short_w[l], hy_short_b[l], hy_bias[l])
        y_hy_c = None if last else _hyena_ctx(z_hy, filt_w, hy_short_w[l], hy_short_b[l], hy_bias[l])
        xc = _merge_mlp(x_lat, x_ctx, ctx_blk, mod[l], y_mla, y_mla_c, o_gla, gr, y_hy, y_hy_c, z_gate,
                        gla_out_norm[l], w_o_mla[l], w_o_gla[l], w_o_hy[l], w_out[l], norm2_g[l], ff_w1[l], ff_w2[l],
                        final_norm_g, n_tiles, n_lat, last)
        x_lat, x_ctx, ctx_blk = xc, xc, n_lat
    return xc
```

```python
import functools
import math

import numpy as np
import jax
import jax.numpy as jnp
from jax import lax
from jax.experimental import pallas as pl
from jax.experimental.pallas import tpu as pltpu

F32 = jnp.float32
BF16 = jnp.bfloat16
LOG2E = 1.4426950408889634

D_MODEL = 1024
DEPTH = 2
GRID_W = 64
EPS = 1e-6
MLA_HEADS = 8
MLA_NOPE = 64
MLA_ROPE = 32
MLA_V = 64
MLA_Q_LORA = 256
MLA_KV_LORA = 128
MLA_SCALE = (MLA_NOPE + MLA_ROPE) ** -0.5
ROPE_BASE = 10000.0
GLA_HEADS = 4
GLA_DK = 256
GLA_DV = 512
GLA_HK = GLA_DK // GLA_HEADS
GLA_HV = GLA_DV // GLA_HEADS
GLA_GATE_RANK = 16
GLA_TAU = 16.0
HY_WIDTH = 512
HY_ORDER = 2
HY_SHORT = 3
HY_BANDS = 16
HY_POS_DIM = 1 + 2 * HY_BANDS
HY_FILTER_HIDDEN = 64
HY_FAST_DECAY = 0.3
HY_SLOW_DECAY = 1.5
HY_DECAY_TARGET = 1e-2
D_FF = 4 * D_MODEL
MLA_OUT = MLA_HEADS * MLA_V
IN_SIZES = (MLA_Q_LORA, MLA_KV_LORA, MLA_ROPE, GLA_DK, GLA_DK, GLA_DV, GLA_DV, GLA_GATE_RANK, GLA_GATE_RANK,
            (HY_ORDER + 1) * HY_WIDTH, 3 * D_MODEL)

LANES = 128
SUBLANES = 8
TM = 256
HEAD_SLOT = 128
V7X_VMEM_BYTES = 64 * 1024 * 1024
VMEM_LIMIT = V7X_VMEM_BYTES * 7 // 8


def _cparams(n_axes, vmem=None):
    return pltpu.CompilerParams(dimension_semantics=("arbitrary",) * n_axes, vmem_limit_bytes=vmem)


def _const_spec(shape):
    nd = len(shape)
    return pl.BlockSpec(shape, lambda *_: (0,) * nd, pipeline_mode=pl.Buffered(1))


def _rms(x):
    return x * lax.rsqrt(jnp.mean(x * x, axis=-1, keepdims=True) + EPS)


def _sigmoid(x):
    return 1.0 / (1.0 + jnp.exp(-x))


def _dot(a, b):
    return jnp.dot(a, b, preferred_element_type=F32)


def _dot_nt(a, b):
    return lax.dot_general(a, b, (((1,), (1,)), ((), ())), preferred_element_type=F32)


def _dot3(a, b):
    a_hi = a.astype(BF16)
    a_lo = (a - a_hi.astype(F32)).astype(BF16)
    b_hi = b.astype(BF16)
    b_lo = (b - b_hi.astype(F32)).astype(BF16)
    m = a.shape[0]
    r = _dot(jnp.concatenate([a_hi, a_lo], axis=0), b_hi)
    return r[:m] + (r[m:] + _dot(a_hi, b_lo))


def _mod_kernel(cc_ref, w_ref, b_ref, o_ref):
    s = cc_ref[...]
    s = s * _sigmoid(s)
    o_ref[0] = _dot3(s, w_ref[0]) + b_ref[0]


def _modulation(cc, ada_w, ada_b):
    tn = 1536
    n6 = ada_w.shape[-1]
    return pl.pallas_call(
        _mod_kernel,
        out_shape=jax.ShapeDtypeStruct((DEPTH, 16, n6), F32),
        grid=(DEPTH, n6 // tn),
        in_specs=[
            pl.BlockSpec((16, D_MODEL), lambda l, j: (0, 0)),
            pl.BlockSpec((1, D_MODEL, tn), lambda l, j: (l, 0, j)),
            pl.BlockSpec((1, 1, tn), lambda l, j: (l, 0, j)),
        ],
        out_specs=pl.BlockSpec((1, 16, tn), lambda l, j: (l, 0, j)),
        compiler_params=_cparams(2),
        name="modulation",
    )(cc, ada_w, ada_b.reshape(DEPTH, 1, n6))


W_A = 768
W_G = 2 * GLA_DK + 2 * GLA_DV


def _token_specs(d, n_lat_tiles, ctx_blk):
    return [pl.BlockSpec((1, TM, d), lambda b, i: (b, jnp.minimum(i, n_lat_tiles - 1), 0)),
            pl.BlockSpec((1, TM, d), lambda b, i: (b, ctx_blk, 0))]


def _token_tile(x_ref, c_ref, n_lat_tiles):
    if n_lat_tiles is None:
        return x_ref[0]
    return jnp.where(pl.program_id(1) < n_lat_tiles, x_ref[0], c_ref[0])


def _inproj_kernel(x_ref, c_ref, mod_ref, g1_ref, wa_ref, wg_ref, wh_ref, wz_ref, qn_ref, kvn_ref, wuq_ref, wukv_ref,
                   wa2_ref, ba_ref, cos_ref, sin_ref,
                   q_out, k_out, v_out, gq_out, gk_out, gv_out, gr_out, glog_out, hy_out, gate_out, *, n_lat_tiles):
    x = _token_tile(x_ref, c_ref, n_lat_tiles)
    shift = mod_ref[0, 0:1, :]
    scale = mod_ref[0, 1:2, :]
    h = (_rms(x) * (g1_ref[...] * (1.0 + scale)) + shift).astype(BF16)

    za = _dot(h, wa_ref[...])
    cos = cos_ref[...]
    sin = sin_ref[...]

    cqn = (_rms(za[:, 0:256]) * qn_ref[...]).astype(BF16)
    qab = _dot(cqn, wuq_ref[...])
    nq = MLA_HEADS * HEAD_SLOT
    for hd in range(MLA_HEADS):
        sl = slice(hd * HEAD_SLOT, (hd + 1) * HEAD_SLOT)
        qa = qab[:, hd * HEAD_SLOT:(hd + 1) * HEAD_SLOT]
        qb = qab[:, nq + hd * HEAD_SLOT:nq + (hd + 1) * HEAD_SLOT]
        q_out[0, :, sl] = ((qa * cos + qb * sin) * (MLA_SCALE * LOG2E)).astype(BF16)

    ckvn = (_rms(za[:, 256:384]) * kvn_ref[...]).astype(BF16)
    kv = _dot(ckvn, wukv_ref[...])
    krot = za[:, 384:512] * cos + za[:, 512:640] * sin
    for hd in range(MLA_HEADS):
        sl = slice(hd * HEAD_SLOT, (hd + 1) * HEAD_SLOT)
        k_out[0, :, sl] = (kv[:, sl] + krot).astype(BF16)
    ones_hi = (lax.broadcasted_iota(jnp.int32, (1, HEAD_SLOT), 1) >= MLA_V).astype(F32)
    for hd in range(MLA_HEADS):
        sl = slice(hd * HEAD_SLOT, (hd + 1) * HEAD_SLOT)
        v_out[0, :, sl] = (kv[:, nq + hd * HEAD_SLOT:nq + (hd + 1) * HEAD_SLOT] + ones_hi).astype(BF16)

    xg = _dot(za[:, 640:768].astype(BF16), wa2_ref[...]) + ba_ref[...]
    glog_out[0] = (jnp.minimum(xg, 0.0) - jnp.log(1.0 + jnp.exp(-jnp.abs(xg)))) * (1.0 / GLA_TAU)

    zg = _dot(h, wg_ref[...])
    gq_out[0] = (zg[:, 0:GLA_DK] * (GLA_HK ** -0.5)).astype(BF16)
    gk_out[0] = zg[:, GLA_DK:2 * GLA_DK].astype(BF16)
    gv_out[0] = zg[:, 2 * GLA_DK:2 * GLA_DK + GLA_DV].astype(BF16)
    gr = zg[:, 2 * GLA_DK + GLA_DV:]
    gr_out[0] = (gr * _sigmoid(gr)).astype(BF16)

    hy_out[0] = _dot(h, wh_ref[...])
    gate_out[0] = _sigmoid(_dot(h, wz_ref[...])).astype(BF16)


def _rope_partner(w):
    a = MLA_ROPE // 4
    perm = np.concatenate([np.arange(a, 2 * a), np.arange(0, a), np.arange(3 * a, 4 * a), np.arange(2 * a, 3 * a)])
    sign = np.concatenate([-np.ones(a), np.ones(a), -np.ones(a), np.ones(a)]).astype(np.float32)
    return w[:, perm] * sign


def _prep_inproj_weights(w_in, mla_w_uq, mla_w_ukv, gla_w_a2, gla_b_a):
    w_in = w_in.astype(BF16)
    offs = np.concatenate([[0], np.cumsum(IN_SIZES)])
    seg = [w_in[:, offs[i]:offs[i + 1]] for i in range(len(IN_SIZES))]
    w_cq, w_ckv, w_kr, w_gq, w_gk, w_gv, w_gr, w_af, w_ab, w_hy, w_gate = seg
    d = w_in.shape[0]
    z = lambda n: jnp.zeros((d, n), w_in.dtype)
    kr_tile = jnp.concatenate([z(MLA_NOPE), w_kr, z(HEAD_SLOT - MLA_NOPE - MLA_ROPE)], axis=1)
    krp_tile = jnp.concatenate([z(MLA_NOPE), _rope_partner(w_kr), z(HEAD_SLOT - MLA_NOPE - MLA_ROPE)], axis=1)
    a_tile = jnp.concatenate([w_af, w_ab, z(LANES - 2 * GLA_GATE_RANK)], axis=1)
    wa = jnp.concatenate([w_cq, w_ckv, kr_tile, krp_tile, a_tile], axis=1)
    wg = jnp.concatenate([w_gq, w_gk, w_gv, w_gr], axis=1)

    dh = MLA_NOPE + MLA_ROPE
    zq = lambda n: jnp.zeros((MLA_Q_LORA, n), w_in.dtype)
    plain, partner = [], []
    for hd in range(MLA_HEADS):
        blk = mla_w_uq[:, hd * dh:(hd + 1) * dh]
        plain += [blk, zq(HEAD_SLOT - dh)]
        partner += [zq(MLA_NOPE), _rope_partner(blk[:, MLA_NOPE:]), zq(HEAD_SLOT - dh)]
    wuq = jnp.concatenate(plain + partner, axis=1)

    zk = jnp.zeros((MLA_KV_LORA, HEAD_SLOT - MLA_NOPE), w_in.dtype)
    kcols, vcols = [], []
    for hd in range(MLA_HEADS):
        blk = mla_w_ukv[:, hd * (MLA_NOPE + MLA_V):(hd + 1) * (MLA_NOPE + MLA_V)]
        kcols += [blk[:, :MLA_NOPE], zk]
        vcols += [blk[:, MLA_NOPE:], zk]
    wukv = jnp.concatenate(kcols + vcols, axis=1)

    wa2 = jnp.zeros((LANES, 2 * GLA_DK), gla_w_a2.dtype)
    wa2 = wa2.at[0:GLA_GATE_RANK, 0:GLA_DK].set(gla_w_a2[0])
    wa2 = wa2.at[GLA_GATE_RANK:2 * GLA_GATE_RANK, GLA_DK:].set(gla_w_a2[1])
    ba = jnp.concatenate([gla_b_a[0], gla_b_a[1]])[None, :]
    bf = lambda t: t.astype(BF16)
    return bf(wa), bf(wg), bf(w_hy), bf(w_gate), bf(wuq), bf(wukv), bf(wa2), ba


def _rope_tables(seq, ctx_len):
    rows = seq // GRID_W
    row = np.repeat(np.arange(rows, dtype=np.float64), GRID_W)
    col = np.tile(np.arange(GRID_W, dtype=np.float64), rows)
    a = MLA_ROPE // 4
    inv = ROPE_BASE ** (-np.arange(a, dtype=np.float64) / a)
    ang_r = row[:, None] * inv
    ang_c = col[:, None] * inv
    cos32 = np.concatenate([np.cos(ang_r), np.cos(ang_r), np.cos(ang_c), np.cos(ang_c)], axis=1)
    sin32 = np.concatenate([np.sin(ang_r), np.sin(ang_r), np.sin(ang_c), np.sin(ang_c)], axis=1)
    pad_r = HEAD_SLOT - MLA_NOPE - MLA_ROPE
    cos = np.concatenate([np.ones((seq, MLA_NOPE)), cos32, np.zeros((seq, pad_r))], axis=1)
    sin = np.concatenate([np.zeros((seq, MLA_NOPE)), sin32, np.zeros((seq, pad_r))], axis=1)
    cos_c = np.concatenate([np.ones((ctx_len, MLA_NOPE + MLA_ROPE)), np.zeros((ctx_len, pad_r))], axis=1)
    sin_c = np.zeros((ctx_len, HEAD_SLOT))
    return (jnp.asarray(np.concatenate([cos, cos_c], axis=0), F32),
            jnp.asarray(np.concatenate([sin, sin_c], axis=0), F32))


def _inproj(x_lat, x_ctx, ctx_blk, mod_l, g1, weights, q_norm, kv_norm, cos, sin, n_lat_tiles):
    bsz, _, d = x_lat.shape
    nt = n_lat_tiles + 1
    t = nt * TM
    wa, wg, wh, wz, wuq, wukv, wa2, ba = weights
    tile = lambda w: pl.BlockSpec((1, TM, w), lambda b, i: (b, i, 0))
    row = lambda w: pl.BlockSpec((1, w), lambda b, i: (0, 0))
    mod_spec = pl.BlockSpec((1, 6, d), lambda b, i: (jnp.where(i < n_lat_tiles, b, bsz), 0, 0))
    tab = pl.BlockSpec((TM, HEAD_SLOT), lambda b, i: (i, 0))
    nq = MLA_HEADS * HEAD_SLOT
    sds = lambda w, dt: jax.ShapeDtypeStruct((bsz, t, w), dt)
    out_shape = (sds(nq, BF16), sds(nq, BF16), sds(nq, BF16), sds(GLA_DK, BF16), sds(GLA_DK, BF16),
                 sds(GLA_DV, BF16), sds(GLA_DV, BF16), sds(2 * GLA_DK, F32), sds(3 * HY_WIDTH, F32),
                 sds(3 * D_MODEL, BF16))
    out_specs = (tile(nq), tile(nq), tile(nq), tile(GLA_DK), tile(GLA_DK), tile(GLA_DV), tile(GLA_DV),
                 tile(2 * GLA_DK), tile(3 * HY_WIDTH), tile(3 * D_MODEL))
    return pl.pallas_call(
        functools.partial(_inproj_kernel, n_lat_tiles=n_lat_tiles),
        out_shape=out_shape,
        grid=(bsz, nt),
        in_specs=_token_specs(d, n_lat_tiles, ctx_blk) + [
            mod_spec, row(d), _const_spec(wa.shape), _const_spec(wg.shape), _const_spec(wh.shape),
            _const_spec(wz.shape), row(MLA_Q_LORA), row(MLA_KV_LORA), _const_spec(wuq.shape),
            _const_spec(wukv.shape), _const_spec(wa2.shape), row(2 * GLA_DK), tab, tab],
        out_specs=out_specs,
        compiler_params=_cparams(2, VMEM_LIMIT),
        name="inproj",
    )(x_lat, x_ctx, mod_l, g1[None, :], wa, wg, wh, wz, q_norm[None, :], kv_norm[None, :], wuq, wukv, wa2, ba,
      cos, sin)


ATT_TK = 1024


ATT_TQ = 2048
ATT_HEADS = 2


def _attn_kernel(q_ref, k_ref, v_ref, o_ref, m_ref, acc_ref, *, chunks):
    tq = q_ref.shape[1]
    n_heads = q_ref.shape[2] // HEAD_SLOT
    m_ref[...] = jnp.full(m_ref.shape, -jnp.inf, F32)
    acc_ref[...] = jnp.zeros(acc_ref.shape, F32)
    for r0, size in chunks:
        k = k_ref[0, pl.ds(r0, size), :]
        v = v_ref[0, pl.ds(r0, size), :]
        for hd in range(n_heads):
            sl = slice(hd * HEAD_SLOT, (hd + 1) * HEAD_SLOT)
            s = _dot_nt(q_ref[0, :, sl], k[:, sl])
            m_prev = m_ref[hd]
            m_new = jnp.maximum(m_prev, jnp.max(s, axis=1, keepdims=True))
            p = jnp.exp2((s - jnp.concatenate([m_new] * (size // LANES), axis=1)).astype(BF16))
            acc_ref[hd] = jnp.exp2(m_prev - m_new) * acc_ref[hd] + _dot(p, v[:, sl])
            m_ref[hd] = m_new
    lane = lax.broadcasted_iota(jnp.int32, (tq, HEAD_SLOT), 1)
    for pr in range(n_heads // 2):
        a0 = acc_ref[2 * pr]
        a1 = acc_ref[2 * pr + 1]
        o0 = a0 / pltpu.roll(a0, MLA_V, 1)
        o1 = pltpu.roll(a1, MLA_V, 1) / a1
        o_ref[0, :, pr * HEAD_SLOT:(pr + 1) * HEAD_SLOT] = jnp.where(lane < MLA_V, o0, o1).astype(o_ref.dtype)


def _attention(q, k, v, seq, with_ctx_queries):
    bsz, t, _ = q.shape
    ctx_len = t - seq
    assert seq % ATT_TK == 0 and seq % ATT_TQ == 0 and seq % ctx_len == 0 and 2 * MLA_V == HEAD_SLOT
    n_chunks = seq // ATT_TK
    chunks = tuple((j * ATT_TK, ATT_TK) for j in range(n_chunks - 1))
    chunks += (((n_chunks - 1) * ATT_TK, ATT_TK + ctx_len),)
    scratch = lambda nh, tq: [pltpu.VMEM((nh, tq, LANES), F32), pltpu.VMEM((nh, tq, HEAD_SLOT), F32)]
    hps = ATT_HEADS
    y = pl.pallas_call(
        functools.partial(_attn_kernel, chunks=chunks),
        out_shape=jax.ShapeDtypeStruct((bsz, seq, MLA_OUT), BF16),
        grid=(bsz, MLA_HEADS // hps, seq // ATT_TQ),
        in_specs=[
            pl.BlockSpec((1, ATT_TQ, hps * HEAD_SLOT), lambda b, hp, i: (b, i, hp)),
            pl.BlockSpec((1, t, hps * HEAD_SLOT), lambda b, hp, i: (b, 0, hp)),
            pl.BlockSpec((1, t, hps * HEAD_SLOT), lambda b, hp, i: (b, 0, hp)),
        ],
        out_specs=pl.BlockSpec((1, ATT_TQ, hps * MLA_V), lambda b, hp, i: (b, i, hp)),
        scratch_shapes=scratch(hps, ATT_TQ),
        compiler_params=_cparams(3, VMEM_LIMIT),
        name="mla_attention",
    )(q, k, v)
    if not with_ctx_queries:
        return y, None
    cblk = seq // ctx_len
    ctx_rows = pl.BlockSpec((1, ctx_len, MLA_HEADS * HEAD_SLOT), lambda b: (b, cblk, 0))
    y_ctx = pl.pallas_call(
        functools.partial(_attn_kernel, chunks=((0, ctx_len),)),
        out_shape=jax.ShapeDtypeStruct((bsz, ctx_len, MLA_OUT), BF16),
        grid=(bsz,),
        in_specs=[ctx_rows, ctx_rows, ctx_rows],
        out_specs=pl.BlockSpec((1, ctx_len, MLA_OUT), lambda b: (b, 0, 0)),
        scratch_shapes=scratch(MLA_HEADS, ctx_len),
        compiler_params=_cparams(1),
        name="mla_attention_ctx",
    )(q, k, v)
    return y, y_ctx


GLA_LEVELS = int(math.log2(TM))
GLA_SAFE_SPAN = 60.0


def _gla_level_matrices():
    i = np.arange(TM)[:, None]
    t = np.arange(TM)[None, :]
    fwd = [(t <= i)]
    for lv in range(GLA_LEVELS):
        m = TM >> (lv + 1)
        lo = (i // m) * m
        later = ((i // m) % 2) == 1
        q_part = later & (t >= lo) & (t <= i)
        k_part = (~later) & (t > i) & (t <= lo + m - 1)
        fwd.append(q_part | k_part)
    fwd = np.concatenate(fwd, axis=0).astype(np.float32)
    nb = 1 + GLA_LEVELS
    bwd = fwd.reshape(nb, TM, TM)[:, ::-1, ::-1].reshape(nb * TM, TM)
    return np.stack([fwd, bwd])


def _gla_kernel(qf_ref, kf_ref, vf_ref, gf_ref, qb_ref, kb_ref, vb_ref, gb_ref, mall_ref, of_ref, ob_ref, s_ref, a_ref):
    step = pl.program_id(1)

    @pl.when(step == 0)
    def _():
        s_ref[...] = jnp.zeros(s_ref.shape, F32)

    ins = ((qf_ref, kf_ref, vf_ref, gf_ref), (qb_ref, kb_ref, vb_ref, gb_ref))
    outs = (of_ref, ob_ref)
    qs, ks, vs, g2s, g_cums, g_tots = [], [], [], [], [], []
    for d in range(2):
        q_ref, k_ref, v_ref, g_ref = ins[d]
        qs.append(q_ref[0].astype(F32))
        ks.append(k_ref[0].astype(F32))
        vs.append(v_ref[0])
        g = g_ref[0]
        g2 = jnp.concatenate([g.astype(BF16), (g - g.astype(BF16).astype(F32)).astype(BF16)], axis=1)
        e2 = _dot(mall_ref[d, 0:TM, :], g2)
        g2s.append(g2)
        g_cums.append(e2[:, :GLA_DK] + e2[:, GLA_DK:])
        g_tots.append(jnp.sum(g, axis=0, keepdims=True))

    row = lax.broadcasted_iota(jnp.int32, (TM, TM), 0)
    col = lax.broadcasted_iota(jnp.int32, (TM, TM), 1)
    lane_head = lax.broadcasted_iota(jnp.int32, (TM, GLA_DK), 1) // GLA_HK
    tok = lax.broadcasted_iota(jnp.int32, (TM, GLA_DK), 0)
    eye = row == col

    def stack_heads(t):
        return jnp.concatenate([jnp.where(lane_head == hd, t, 0.0) for hd in range(GLA_HEADS)], axis=0).astype(BF16)

    span = jnp.maximum(jnp.max(-g_tots[0]), jnp.max(-g_tots[1]))

    @pl.when(span < GLA_SAFE_SPAN)
    def _():
        for d in range(2):
            res = _dot_nt(stack_heads(qs[d] * jnp.exp(g_cums[d])), (ks[d] * jnp.exp(-g_cums[d])).astype(BF16))
            seen = (col <= row) if d == 0 else (col >= row)
            for hd in range(GLA_HEADS):
                a_ref[d, hd] = jnp.where(seen, res[hd * TM:(hd + 1) * TM], 0.0)

    @pl.when(span >= GLA_SAFE_SPAN)
    def _():
        for d in range(2):
            q, k = qs[d], ks[d]
            e2l = _dot(mall_ref[d, TM:, :], g2s[d])
            e_lv = e2l[:, :GLA_DK] + e2l[:, GLA_DK:]
            res = _dot_nt(stack_heads(q), k.astype(BF16))
            for hd in range(GLA_HEADS):
                a_ref[d, hd] = jnp.where(eye, res[hd * TM:(hd + 1) * TM], 0.0)
            for lv in range(GLA_LEVELS):
                m = TM >> (lv + 1)
                w = jnp.exp(e_lv[lv * TM:(lv + 1) * TM])
                bit = (tok // m) % 2
                q_act = bit != d
                qt = jnp.where(q_act, q * w, 0.0)
                kt = jnp.where(q_act, 0.0, k * w).astype(BF16)
                res = _dot_nt(stack_heads(qt), kt)
                if m == TM // 2:
                    for hd in range(GLA_HEADS):
                        a_ref[d, hd] += res[hd * TM:(hd + 1) * TM]
                else:
                    same = (row // (2 * m)) == (col // (2 * m))
                    for hd in range(GLA_HEADS):
                        a_ref[d, hd] += jnp.where(same, res[hd * TM:(hd + 1) * TM], 0.0)

    same_head = (lax.broadcasted_iota(jnp.int32, (GLA_DK, GLA_DV), 0) // GLA_HK
                 == lax.broadcasted_iota(jnp.int32, (GLA_DK, GLA_DV), 1) // GLA_HV)
    for d in range(2):
        q, k, v, g_cum, g_tot = qs[d], ks[d], vs[d], g_cums[d], g_tots[d]
        s_old = s_ref[d]
        o_inter = _dot((q * jnp.exp(g_cum)).astype(BF16), s_old.astype(BF16))
        for hd in range(GLA_HEADS):
            sl = slice(hd * GLA_HV, (hd + 1) * GLA_HV)
            o_intra = _dot(a_ref[d, hd].astype(BF16), v[:, sl])
            outs[d][0, :, sl] = (o_intra + o_inter[:, sl]).astype(outs[d].dtype)

        kdec_t = (k * jnp.exp(g_tot - g_cum)).T.astype(BF16)
        upd = _dot(kdec_t, v)
        dec_col = jnp.sum(jnp.where(eye, jnp.broadcast_to(jnp.exp(g_tot), (TM, GLA_DK)), 0.0), axis=1,
                          keepdims=True)
        s_ref[d] = dec_col * s_old + jnp.where(same_head, upd, 0.0)


def _gla(gq, gk, gv, glog, n_lat_tiles):
    assert GLA_DK == TM
    bsz, t, _ = gq.shape
    nt = t // TM
    mall = jnp.asarray(_gla_level_matrices(), dtype=BF16)

    def tile_idx(d, s):
        return jnp.where(s == 0, n_lat_tiles, s - 1 if d == 0 else n_lat_tiles - s)

    def specs(d):
        tok = lambda w: pl.BlockSpec((1, TM, w), lambda b, s: (b, tile_idx(d, s), 0))
        return [tok(GLA_DK), tok(GLA_DK), tok(GLA_DV), pl.BlockSpec((1, TM, GLA_DK), lambda b, s: (b, tile_idx(d, s), d))]

    out_spec = lambda d: pl.BlockSpec((1, TM, GLA_DV), lambda b, s: (b, tile_idx(d, s), 0))
    sds = jax.ShapeDtypeStruct((bsz, t, GLA_DV), BF16)
    return pl.pallas_call(
        _gla_kernel,
        out_shape=(sds, sds),
        grid=(bsz, nt),
        in_specs=specs(0) + specs(1) + [_const_spec(mall.shape)],
        out_specs=(out_spec(0), out_spec(1)),
        scratch_shapes=[pltpu.VMEM((2, GLA_DK, GLA_DV), F32), pltpu.VMEM((2, GLA_HEADS, TM, TM), F32)],
        compiler_params=_cparams(2),
        name="gla_scan",
    )(gq, gk, gv, glog, gq, gk, gv, glog, mall)


HY_N1 = 64
HY_N2 = 128
HY_SLAB = SUBLANES
HY_CB = 256
HY_KG = 32
HY_SPEC_CB = 128
HY_UNROLL = 16


def _hyena_dft_constants(seq):
    n = 2 * seq
    assert n == HY_N1 * HY_N2
    nh = HY_N1 // 2
    kh = HY_N2 // 2
    eye = np.eye(HY_SLAB)
    k1 = np.arange(HY_N1)
    th = 2 * np.pi * np.outer(k1 + 0.5, np.arange(nh)) / HY_N1
    fwd1 = np.concatenate([np.kron(np.cos(th), eye), np.kron(-np.sin(th), eye)], axis=0)
    inv1 = (2.0 / n) * np.concatenate([np.kron(np.cos(th).T, eye), np.kron(-np.sin(th).T, eye)], axis=1)
    nlo = np.arange(HY_N2)
    k2 = np.arange(kh)
    ph = 2 * np.pi * (k2[None, :, None] * nlo[None, None, :] / HY_N2
                      + (k1[:, None, None] + 0.5) * nlo[None, None, :] / n)
    c, s = np.cos(ph), np.sin(ph)
    fwd2 = np.concatenate([np.concatenate([c, s], axis=2), np.concatenate([-s, c], axis=2)], axis=1)
    ct, st = c.transpose(0, 2, 1), s.transpose(0, 2, 1)
    inv2 = np.concatenate([np.concatenate([ct, -st], axis=2), np.concatenate([st, ct], axis=2)], axis=1)
    return fwd1, fwd2, inv2, inv1


def _hyena_ctx_dft_constants(ctx_len):
    n = 2 * ctx_len
    th = 2 * np.pi * np.outer(np.arange(ctx_len) + 0.5, np.arange(ctx_len)) / n
    fwd = np.concatenate([np.cos(th), -np.sin(th)], axis=0)
    inv = (2.0 / n) * np.concatenate([np.cos(th).T, -np.sin(th).T], axis=1)
    return fwd, inv


def _hyena_features(length):
    pos = np.arange(length, dtype=np.float64)
    t = pos / max(length - 1, 1)
    f = np.linspace(1e-4, HY_BANDS - 1, HY_BANDS)
    ang = (2.0 * math.pi / length) * pos[:, None] * f
    feat = np.concatenate([t[:, None], np.cos(ang), np.sin(ang)], axis=-1)
    return jnp.asarray(np.pad(feat, ((0, 0), (0, LANES - HY_POS_DIM))), F32)


def _hy_filter_kernel(feat_ref, w1_ref, b1_ref, w2_ref, b2_ref, w3_ref, b3_ref, absd_ref, h_ref, s_ref):
    i = pl.program_id(0)
    feat = feat_ref[...]
    hp = _dot3
    hdn = jnp.sin(hp(feat, w1_ref[...]) + b1_ref[...])
    hdn = jnp.sin(hp(hdn, w2_ref[...]) + b2_ref[...])
    h = hp(hdn, w3_ref[...]) + b3_ref[...]
    window = jnp.exp(-feat[:, 0:1] * absd_ref[...])
    h = h * jnp.concatenate([window] * (2 * HY_ORDER), axis=1)
    h_ref[...] = h

    @pl.when(i == 0)
    def _():
        s_ref[...] = jnp.zeros(s_ref.shape, F32)

    s_ref[...] += jnp.sum(jnp.abs(h), axis=0, keepdims=True)


def _hyena_filters_raw(length, filt_w):
    w1, b1, w2, b2, w3, b3 = filt_w
    nf = 2 * HY_ORDER * HY_WIDTH
    tr = min(length, 512)
    deltas = np.linspace(math.log(HY_DECAY_TARGET) / HY_FAST_DECAY, math.log(HY_DECAY_TARGET) / HY_SLOW_DECAY,
                         HY_WIDTH, dtype=np.float32)
    absd = jnp.asarray(np.abs(deltas))[None, :]
    w1p = jnp.pad(w1, ((0, LANES - HY_POS_DIM), (0, 0)))
    full = lambda shp: pl.BlockSpec(shp, lambda i: (0,) * len(shp))
    return pl.pallas_call(
        _hy_filter_kernel,
        out_shape=(jax.ShapeDtypeStruct((length, nf), F32), jax.ShapeDtypeStruct((1, nf), F32)),
        grid=(length // tr,),
        in_specs=[pl.BlockSpec((tr, LANES), lambda i: (i, 0)), full((LANES, HY_FILTER_HIDDEN)),
                  full((1, HY_FILTER_HIDDEN)), full((HY_FILTER_HIDDEN, HY_FILTER_HIDDEN)), full((1, HY_FILTER_HIDDEN)),
                  full((HY_FILTER_HIDDEN, nf)), full((1, nf)), full((1, HY_WIDTH))],
        out_specs=(pl.BlockSpec((tr, nf), lambda i: (i, 0)), full((1, nf))),
        compiler_params=_cparams(1),
        name="hyena_filter_mlp",
    )(_hyena_features(length), w1p, b1[None, :], w2, b2[None, :], w3, b3[None, :], absd)


def _dot_split(m, x):
    x_hi = x.astype(BF16)
    x_lo = (x - x_hi.astype(F32)).astype(BF16)
    n = x.shape[1]
    r = _dot(m, jnp.concatenate([x_hi, x_lo], axis=1))
    return r[:, :n] + r[:, n:]


def _odft_stage1(src_at, mm, s_re, s_im, unroll=2):
    nk = s_re.shape[0]
    half = nk * HY_SLAB

    def body(j, carry):
        r0 = pl.multiple_of(j * HY_SLAB, HY_SLAB)
        slab = src_at(r0)
        cb = slab.shape[-1]
        res = mm(slab.reshape(-1, cb))
        s_re[:, pl.ds(r0, HY_SLAB), :] = res[:half].reshape(nk, HY_SLAB, cb)
        s_im[:, pl.ds(r0, HY_SLAB), :] = res[half:].reshape(nk, HY_SLAB, cb)
        return carry

    lax.fori_loop(0, HY_N2 // HY_SLAB, body, 0, unroll=unroll)


def _hy_spectrum_kernel(hf_ref, hb_ref, sf_ref, sb_ref, fwd1_ref, fwd2_ref, o_ref, s_re, s_im):
    kh = HY_N2 // 2
    mm1 = lambda x: _dot_split(fwd1_ref[...], x)

    def middle(sign):
        def body(k1, carry):
            a = jnp.concatenate([s_re[k1], s_im[k1]], axis=0)
            x = _dot_split(fwd2_ref[k1], a)
            if sign is None:
                o_ref[0, k1] = x[:kh]
                o_ref[1, k1] = x[kh:]
            else:
                inv_norm = 1.0 / (sf_ref[...] + sb_ref[...])
                o_ref[0, k1] = (o_ref[0, k1] + x[:kh]) * inv_norm
                o_ref[1, k1] = (o_ref[1, k1] - x[kh:]) * inv_norm
            return carry
        lax.fori_loop(0, HY_N1, body, 0, unroll=32)

    _odft_stage1(lambda r0: hf_ref[:, pl.ds(r0, HY_SLAB), :], mm1, s_re, s_im, unroll=16)
    middle(None)

    def bwd_slab(r0):
        slab = hb_ref[:, pl.ds(r0, HY_SLAB), :]
        nhi = lax.broadcasted_iota(jnp.int32, slab.shape, 0)
        r = lax.broadcasted_iota(jnp.int32, slab.shape, 1)
        return jnp.where((nhi == 0) & (r + r0 == 0), 0.0, slab)

    _odft_stage1(bwd_slab, mm1, s_re, s_im, unroll=16)
    middle(-1)


def _hyena_filter_spectrum(seq, w1, b1, w2, b2, w3, b3):
    h_raw, s = _hyena_filters_raw(seq, (w1, b1, w2, b2, w3, b3))
    nh = HY_N1 // 2
    nc = HY_ORDER * HY_WIDTH
    h3 = h_raw.reshape(nh, HY_N2, 2 * nc)
    fwd1, fwd2, _, _ = _hyena_dft_constants(seq)
    fwd1 = jnp.asarray(fwd1, BF16)
    fwd2 = jnp.asarray(fwd2, BF16)
    scb = HY_SPEC_CB
    ncb = nc // scb
    return pl.pallas_call(
        _hy_spectrum_kernel,
        out_shape=jax.ShapeDtypeStruct((2, HY_N1, HY_N2 // 2, nc), F32),
        grid=(ncb,),
        in_specs=[pl.BlockSpec((nh, HY_N2, scb), lambda c: (0, 0, c)),
                  pl.BlockSpec((nh, HY_N2, scb), lambda c: (0, 0, ncb + c)),
                  pl.BlockSpec((1, scb), lambda c: (0, c)),
                  pl.BlockSpec((1, scb), lambda c: (0, ncb + c)),
                  _const_spec(fwd1.shape), _const_spec(fwd2.shape)],
        out_specs=pl.BlockSpec((2, HY_N1, HY_N2 // 2, scb), lambda c: (0, 0, 0, c)),
        scratch_shapes=[pltpu.VMEM((HY_N1, HY_N2, scb), F32), pltpu.VMEM((HY_N1, HY_N2, scb), F32)],
        compiler_params=_cparams(1, VMEM_LIMIT),
        name="hyena_filter_spectrum",
    )(h3, h3, s, s, fwd1, fwd2)


def _short_conv_chunk(ref, c, n_chunks, w_ref, b_ref):
    per = TM // HY_N2
    cur = ref[0, pl.ds(per * c, per)]
    cb = cur.shape[-1]
    cur = cur.reshape(TM, cb)
    prev = ref[0, jnp.maximum(per * c - 1, 0), pl.ds(HY_N2 - SUBLANES, SUBLANES), :][SUBLANES - 1:SUBLANES]
    nxt = ref[0, jnp.minimum(per * c + per, per * n_chunks - 1), pl.ds(0, SUBLANES), :][0:1]
    prev = jnp.where(c > 0, prev, 0.0)
    nxt = jnp.where(c < n_chunks - 1, nxt, 0.0)
    rowi = lax.broadcasted_iota(jnp.int32, (TM, cb), 0)
    dn = jnp.where(rowi == 0, prev, pltpu.roll(cur, 1, 0))
    up = jnp.where(rowi == TM - 1, nxt, pltpu.roll(cur, TM - 1, 0))
    return b_ref[...] + w_ref[0:1, :] * dn + w_ref[1:2, :] * cur + w_ref[2:3, :] * up


def _hy_conv_kernel(*refs, conv_y):
    if conv_y:
        (y_ref, g_ref, h_ref, bias_ref, wy_ref, by_ref, wg_ref, bg_ref, fwd1_ref, fwd2_ref, inv2_ref, inv1_ref,
         o_ref, s_re, s_im, gs_ref, us_ref) = refs
    else:
        (y_ref, g_ref, h_ref, bias_ref, wg_ref, bg_ref, fwd1_ref, fwd2_ref, inv2_ref, inv1_ref,
         o_ref, s_re, s_im, gs_ref) = refs
    grp = pl.program_id(2)
    last = pl.num_programs(2) - 1
    nh = HY_N1 // 2
    per = TM // HY_N2
    n_chunks = nh // per
    cb = o_ref.shape[-1]
    kh = HY_N2 // 2

    @pl.when(grp == 0)
    def _():
        def pre(c, carry):
            gs_ref[pl.ds(per * c, per)] = _short_conv_chunk(g_ref, c, n_chunks, wg_ref, bg_ref).reshape(per, HY_N2, cb)
            if conv_y:
                us_ref[pl.ds(per * c, per)] = _short_conv_chunk(y_ref, c, n_chunks, wy_ref, by_ref).reshape(
                    per, HY_N2, cb)
            return carry
        lax.fori_loop(0, n_chunks, pre, 0)

    if conv_y:
        u_at = lambda r0: us_ref[:, pl.ds(r0, HY_SLAB), :]
    else:
        u_at = lambda r0: y_ref[0, :, pl.ds(r0, HY_SLAB), :]

    _odft_stage1(u_at, lambda x: _dot(fwd1_ref[0], x.astype(BF16)), s_re, s_im, unroll=HY_UNROLL)

    def middle(k1, carry):
        a = jnp.concatenate([s_re[k1], s_im[k1]], axis=0).astype(BF16)
        x = _dot(fwd2_ref[k1], a)
        xr, xi = x[:kh], x[kh:]
        hr, hi = h_ref[0, k1], h_ref[1, k1]
        y = jnp.concatenate([xr * hr - xi * hi, xr * hi + xi * hr], axis=0).astype(BF16)
        bm = _dot(inv2_ref[k1], y)
        s_re[k1] = bm[:HY_N2]
        s_im[k1] = bm[HY_N2:]
        return carry

    lax.fori_loop(0, HY_KG, middle, 0, unroll=HY_KG)

    def partial_conv(r0):
        slab = jnp.concatenate([s_re[:, pl.ds(r0, HY_SLAB), :].reshape(HY_KG * HY_SLAB, cb),
                                s_im[:, pl.ds(r0, HY_SLAB), :].reshape(HY_KG * HY_SLAB, cb)], axis=0).astype(BF16)
        return _dot(inv1_ref[0], slab).reshape(nh, HY_SLAB, cb)

    def post_loop(fn):
        def post(j, carry):
            r0 = pl.multiple_of(j * HY_SLAB, HY_SLAB)
            o_ref[0, :, pl.ds(r0, HY_SLAB), :] = fn(r0, partial_conv(r0))
            return carry
        lax.fori_loop(0, HY_N2 // HY_SLAB, post, 0, unroll=HY_UNROLL)

    @pl.when(grp == 0)
    def _():
        post_loop(lambda r0, part: part)

    @pl.when((grp > 0) & (grp < last))
    def _():
        post_loop(lambda r0, part: o_ref[0, :, pl.ds(r0, HY_SLAB), :] + part)

    @pl.when(grp == last)
    def _():
        post_loop(lambda r0, part: gs_ref[:, pl.ds(r0, HY_SLAB), :]
                  * (o_ref[0, :, pl.ds(r0, HY_SLAB), :] + part + bias_ref[...] * u_at(r0)))


def _hyena_order(y4, y_col0, z4, gate_col0, hspec, order, hy_bias, short_w, short_b, consts, conv_y):
    bsz = z4.shape[0]
    nh = HY_N1 // 2
    ncb = HY_WIDTH // HY_CB
    ngrp = HY_N1 // HY_KG
    fwd1, fwd2, inv2, inv1 = consts
    blk4 = lambda off: pl.BlockSpec((1, nh, HY_N2, HY_CB), lambda c, b, g: (b, 0, 0, off + c))
    rowspec = lambda rows, off: pl.BlockSpec((rows, HY_CB), lambda c, b, g: (0, off + c))
    grouped = lambda shp: pl.BlockSpec(shp, lambda c, b, g: (g, 0, 0))
    in_specs = [blk4(y_col0), blk4(gate_col0),
                pl.BlockSpec((2, HY_KG, HY_N2 // 2, HY_CB), lambda c, b, g: (0, g, 0, order * ncb + c)),
                rowspec(1, 0)]
    args = [y4, z4, hspec, hy_bias.reshape(1, -1)]
    if conv_y:
        in_specs += [rowspec(HY_SHORT, y_col0), rowspec(1, y_col0)]
        args += [short_w, short_b[None, :]]
    in_specs += [rowspec(HY_SHORT, gate_col0), rowspec(1, gate_col0)]
    args += [short_w, short_b[None, :]]
    in_specs += [grouped((1,) + fwd1.shape[1:]), grouped((HY_KG,) + fwd2.shape[1:]),
                 grouped((HY_KG,) + inv2.shape[1:]), grouped((1,) + inv1.shape[1:])]
    args += [fwd1, fwd2, inv2, inv1]
    scratch = [pltpu.VMEM((HY_KG, HY_N2, HY_CB), F32), pltpu.VMEM((HY_KG, HY_N2, HY_CB), F32),
               pltpu.VMEM((nh, HY_N2, HY_CB), F32)]
    if conv_y:
        scratch.append(pltpu.VMEM((nh, HY_N2, HY_CB), F32))
    return pl.pallas_call(
        functools.partial(_hy_conv_kernel, conv_y=conv_y),
        out_shape=jax.ShapeDtypeStruct((bsz, nh, HY_N2, HY_WIDTH), F32),
        grid=(ncb, bsz, ngrp),
        in_specs=in_specs,
        out_specs=pl.BlockSpec((1, nh, HY_N2, HY_CB), lambda c, b, g: (b, 0, 0, c)),
        scratch_shapes=scratch,
        compiler_params=_cparams(3, V7X_VMEM_BYTES * 15 // 16),
        name="hyena_conv%d" % order,
    )(*args)


def _hyena_latent(z_hy, hspec, short_w, short_b, hy_bias):
    bsz, t, _ = z_hy.shape
    seq = HY_N1 * HY_N2 // 2
    fwd1, fwd2, inv2, inv1 = _hyena_dft_constants(seq)
    ngrp = HY_N1 // HY_KG
    rows = HY_KG * HY_SLAB
    fwd1 = fwd1.reshape(2, ngrp, rows, -1).transpose(1, 0, 2, 3).reshape(ngrp, 2 * rows, -1)
    inv1 = inv1.reshape(-1, 2, ngrp, rows).transpose(2, 0, 1, 3).reshape(ngrp, -1, 2 * rows)
    consts = tuple(jnp.asarray(m, BF16) for m in (fwd1, fwd2, inv2, inv1))
    z4 = z_hy.reshape(bsz, t // HY_N2, HY_N2, 3 * HY_WIDTH)
    ncb = HY_WIDTH // HY_CB
    y1 = _hyena_order(z4, 2 * ncb, z4, 0, hspec, 0, hy_bias[0], short_w, short_b, consts, True)
    y2 = _hyena_order(y1, 0, z4, ncb, hspec, 1, hy_bias[1], short_w, short_b, consts, False)
    return y2.reshape(bsz, seq, HY_WIDTH)


def _hy_ctx_spectrum_kernel(h_ref, s_ref, fwd_ref, o_ref):
    lc = h_ref.shape[0]
    nc = HY_ORDER * HY_WIDTH
    hp = _dot3
    h = h_ref[...]
    rowi = lax.broadcasted_iota(jnp.int32, (lc, nc), 0)
    xf = hp(fwd_ref[...], h[:, :nc])
    xb = hp(fwd_ref[...], jnp.where(rowi == 0, 0.0, h[:, nc:]))
    inv_norm = 1.0 / (s_ref[:, :nc] + s_ref[:, nc:])
    o_ref[0] = (xf[:lc] + xb[:lc]) * inv_norm
    o_ref[1] = (xf[lc:] - xb[lc:]) * inv_norm


def _hy_ctx_conv_kernel(x1_ref, x2_ref, v_ref, h_ref, bias_ref, w_ref, b_ref, fwd_ref, inv_ref, o_ref):
    lc = o_ref.shape[1]
    rowi = lax.broadcasted_iota(jnp.int32, (lc, HY_WIDTH), 0)

    def short(ref, part):
        cur = ref[0].reshape(lc, HY_WIDTH)
        sl = slice(part * HY_WIDTH, (part + 1) * HY_WIDTH)
        dn = jnp.where(rowi == 0, 0.0, pltpu.roll(cur, 1, 0))
        up = jnp.where(rowi == lc - 1, 0.0, pltpu.roll(cur, lc - 1, 0))
        return b_ref[:, sl] + w_ref[0:1, sl] * dn + w_ref[1:2, sl] * cur + w_ref[2:3, sl] * up

    y = short(v_ref, 2)
    for order, gref in enumerate((x1_ref, x2_ref)):
        sl = slice(order * HY_WIDTH, (order + 1) * HY_WIDTH)
        x = _dot(fwd_ref[...], y.astype(BF16))
        xr, xi = x[:lc], x[lc:]
        hr, hi = h_ref[0, :, sl], h_ref[1, :, sl]
        prod = jnp.concatenate([xr * hr - xi * hi, xr * hi + xi * hr], axis=0).astype(BF16)
        conv = _dot(inv_ref[...], prod)
        y = short(gref, order) * (conv + bias_ref[order:order + 1, :] * y)
    o_ref[0] = y


def _hyena_ctx(z_hy, filt_w, short_w, short_b, hy_bias):
    bsz, t, _ = z_hy.shape
    seq = HY_N1 * HY_N2 // 2
    lc = t - seq
    per = lc // HY_N2
    h_raw, s = _hyena_filters_raw(lc, filt_w)
    fwd, inv = _hyena_ctx_dft_constants(lc)
    nc = HY_ORDER * HY_WIDTH
    full = lambda shp: pl.BlockSpec(shp, lambda *_: (0,) * len(shp))
    hspec = pl.pallas_call(
        _hy_ctx_spectrum_kernel,
        out_shape=jax.ShapeDtypeStruct((2, lc, nc), F32),
        grid=(1,),
        in_specs=[full(h_raw.shape), full(s.shape), full(fwd.shape)],
        out_specs=full((2, lc, nc)),
        compiler_params=_cparams(1),
        name="hyena_ctx_spectrum",
    )(h_raw, s, jnp.asarray(fwd, F32))
    z4 = z_hy.reshape(bsz, t // HY_N2, HY_N2, 3 * HY_WIDTH)
    blk = lambda part: pl.BlockSpec((1, per, HY_N2, HY_WIDTH), lambda b: (b, seq // lc, 0, part))
    return pl.pallas_call(
        _hy_ctx_conv_kernel,
        out_shape=jax.ShapeDtypeStruct((bsz, lc, HY_WIDTH), F32),
        grid=(bsz,),
        in_specs=[blk(0), blk(1), blk(2), full((2, lc, nc)), full((HY_ORDER, HY_WIDTH)),
                  full((HY_SHORT, 3 * HY_WIDTH)), full((1, 3 * HY_WIDTH)), full(fwd.shape), full(inv.shape)],
        out_specs=pl.BlockSpec((1, lc, HY_WIDTH), lambda b: (b, 0, 0)),
        compiler_params=_cparams(1),
        name="hyena_ctx_conv",
    )(z4, z4, z4, hspec, hy_bias, short_w, short_b[None, :], jnp.asarray(fwd, BF16), jnp.asarray(inv, BF16))


FF_CHUNK = 1024


def _merge_mlp_kernel(x_ref, c_ref, mod_ref, ymla_ref, ymlac_ref, of_ref, ob_ref, gr_ref, yhy_ref, yhyc_ref, zg_ref, on_ref,
                      wm_ref, wgl_ref, wh_ref, wo_ref, g2_ref, w1_ref, w2_ref, fg_ref, o_ref, *, final, n_lat_tiles):
    o = of_ref[0].astype(F32) + ob_ref[0].astype(F32)
    silu = gr_ref[0].astype(F32)
    parts = []
    for hd in range(GLA_HEADS):
        sl = slice(hd * GLA_HV, (hd + 1) * GLA_HV)
        parts.append((_rms(o[:, sl]) * on_ref[...] * silu[:, sl]).astype(BF16))
    y_gla = jnp.concatenate(parts, axis=1)
    zg = zg_ref[0].astype(F32)
    d = x_ref.shape[-1]
    m = zg[:, 0:d] * _dot(_token_tile(ymla_ref, ymlac_ref, n_lat_tiles), wm_ref[...])
    m = m + zg[:, d:2 * d] * _dot(y_gla, wgl_ref[...])
    m = m + zg[:, 2 * d:3 * d] * _dot(_token_tile(yhy_ref, yhyc_ref, n_lat_tiles).astype(BF16), wh_ref[...])
    out = _dot(m.astype(BF16), wo_ref[...])
    x = _token_tile(x_ref, c_ref, n_lat_tiles) + mod_ref[0, 2:3, :] * out

    h = (_rms(x) * (g2_ref[...] * (1.0 + mod_ref[0, 4:5, :])) + mod_ref[0, 3:4, :]).astype(BF16)
    acc = jnp.zeros(x.shape, F32)
    for j in range(w1_ref.shape[1] // FF_CHUNK):
        a = jnp.maximum(_dot(h, w1_ref[:, j * FF_CHUNK:(j + 1) * FF_CHUNK]), 0.0)
        acc = acc + _dot((a * a).astype(BF16), w2_ref[j * FF_CHUNK:(j + 1) * FF_CHUNK, :])
    xn = x + mod_ref[0, 5:6, :] * acc
    if final:
        xn = _rms(xn) * fg_ref[...]
    o_ref[0] = xn


def _mod_spec(d, n_lat_tiles, bsz):
    return pl.BlockSpec((1, 6, d), lambda b, i: (jnp.where(i < n_lat_tiles, b, bsz), 0, 0))


def _merge_mlp(x_lat, x_ctx, ctx_blk, mod_l, y_mla, y_mla_ctx, o_gla, gr, y_hy, y_hy_ctx, gate, out_norm,
               w_o_mla, w_o_gla, w_o_hy, w_out, g2, w1, w2, final_g, n_tiles, n_lat_tiles, final):
    bsz, _, d = x_lat.shape
    if y_mla_ctx is None:
        y_mla_ctx, y_hy_ctx = y_mla, y_hy
    tile = lambda w: pl.BlockSpec((1, TM, w), lambda b, i: (b, i, 0))
    row = lambda w: pl.BlockSpec((1, w), lambda b, i: (0, 0))
    bf = lambda w: w.astype(BF16)
    return pl.pallas_call(
        functools.partial(_merge_mlp_kernel, final=final, n_lat_tiles=n_lat_tiles if n_tiles > n_lat_tiles else None),
        out_shape=jax.ShapeDtypeStruct((bsz, n_tiles * TM, d), F32),
        grid=(bsz, n_tiles),
        in_specs=_token_specs(d, n_lat_tiles, ctx_blk) + [_mod_spec(d, n_lat_tiles, bsz)]
        + _token_specs(MLA_OUT, n_lat_tiles, 0) + [tile(GLA_DV), tile(GLA_DV), tile(GLA_DV)]
        + _token_specs(HY_WIDTH, n_lat_tiles, 0) + [
            tile(3 * d), row(GLA_HV),
            _const_spec(w_o_mla.shape), _const_spec(w_o_gla.shape), _const_spec(w_o_hy.shape),
            _const_spec(w_out.shape), row(d), _const_spec(w1.shape), _const_spec(w2.shape), row(d)],
        out_specs=tile(d),
        compiler_params=_cparams(2, VMEM_LIMIT),
        name="merge_mlp",
    )(x_lat, x_ctx, mod_l, y_mla, y_mla_ctx, o_gla[0], o_gla[1], gr, y_hy, y_hy_ctx, gate, out_norm[None, :],
      bf(w_o_mla), bf(w_o_gla), bf(w_o_hy), bf(w_out), g2[None, :], bf(w1), bf(w2), final_g[None, :])


def kernel(x, c, ctx, c_ctx, ada_w, ada_b, norm1_g, norm2_g, w_in, mla_q_norm, mla_w_uq, mla_kv_norm, mla_w_ukv, gla_w_a2, gla_b_a, gla_out_norm, hy_short_w, hy_short_b, hy_f_w1, hy_f_b1, hy_f_w2, hy_f_b2, hy_f_w3, hy_f_b3, hy_bias, w_o_mla, w_o_gla, w_o_hy, w_out, ff_w1, ff_w2, final_norm_g):
    bsz, seq, d = x.shape
    ctx_len = ctx.shape[1]
    n_lat = seq // TM
    assert ctx_len == TM and seq % TM == 0
    n_all = n_lat + 1
    x_lat, x_ctx, ctx_blk = x, ctx, 0
    cc = jnp.zeros((16, d), F32).at[:bsz].set(c).at[bsz].set(c_ctx)
    mod = _modulation(cc, ada_w, ada_b).reshape(DEPTH, 16, 6, d)
    cos, sin = _rope_tables(seq, ctx_len)
    for l in range(DEPTH):
        last = l == DEPTH - 1
        n_tiles = n_lat if last else n_all
        weights = _prep_inproj_weights(w_in[l], mla_w_uq[l], mla_w_ukv[l], gla_w_a2[l], gla_b_a[l])
        q, k, v, gq, gk, gv, gr, glog, z_hy, z_gate = _inproj(x_lat, x_ctx, ctx_blk, mod[l], norm1_g[l], weights,
                                                             mla_q_norm[l], mla_kv_norm[l], cos, sin, n_lat)
        y_mla, y_mla_c = _attention(q, k, v, seq, not last)
        o_gla = _gla(gq, gk, gv, glog, n_lat)
        filt_w = (hy_f_w1[l], hy_f_b1[l], hy_f_w2[l], hy_f_b2[l], hy_f_w3[l], hy_f_b3[l])
        hspec = _hyena_filter_spectrum(seq, *filt_w)
        y_hy = _hyena_latent(z_hy, hspec, hy_short_w[l], hy_short_b[l], hy_bias[l])
        y_hy_c = None if last else _hyena_ctx(z_hy, filt_w, hy_short_w[l], hy_short_b[l], hy_bias[l])
        xc = _merge_mlp(x_lat, x_ctx, ctx_blk, mod[l], y_mla, y_mla_c, o_gla, gr, y_hy, y_hy_c, z_gate,
                        gla_out_norm[l], w_o_mla[l], w_o_gla[l], w_o_hy[l], w_out[l], norm2_g[l], ff_w1[l], ff_w2[l],
                        final_norm_g, n_tiles, n_lat, last)
        x_lat, x_ctx, ctx_blk = xc, xc, n_lat
    return xc
```
